```python
import jax, jax.numpy as jnp
from jax import lax
import numpy as np

D_MODEL = 1024
BATCH = 4
SEQ = 4096
DEPTH = 1

GLA_HEADS = 4
GLA_DK = 64
GLA_DV = 128
GLA_GATE_RANK = 16
GLA_GATE_TAU = 16.0
GLA_CHUNK = 64
DIFF_HEADS = 4
DIFF_DH = 64
DIFF_DV = 2 * DIFF_DH
ROPE_THETA = 500000.0
ROPE_DIM = DIFF_DH // 4
Q_BLOCK = 128
NEG_INF = -1e30
GLA_QK_W = GLA_HEADS * GLA_DK
GLA_V_W = GLA_HEADS * GLA_DV
DIFF_QK_W = DIFF_HEADS * 2 * DIFF_DH
DIFF_V_W = DIFF_HEADS * DIFF_DV
IN_SIZES = (GLA_QK_W, GLA_QK_W, GLA_V_W, GLA_V_W, GLA_GATE_RANK,
            DIFF_QK_W, DIFF_QK_W, DIFF_V_W, D_MODEL, D_MODEL)
D_IN = sum(IN_SIZES)
N_GROUPS = 4
EXPERTS_PER_GROUP = 8
N_EXPERTS = N_GROUPS * EXPERTS_PER_GROUP
TOP_K = 2
D_EXPERT = 512
MOE_BLOCK = 128
NORM_EPS = 1e-6
N_MOD = 6

kernel_name = "hybrid_gla_diffattn_hmoe_block"


def rms_norm(x, gain):
    xf = x.astype(jnp.float32)
    y = xf * lax.rsqrt(jnp.mean(xf * xf, axis=-1, keepdims=True) + NORM_EPS)
    return (y * gain.astype(jnp.float32)).astype(x.dtype)


def rope_tables(positions):
    half = ROPE_DIM // 2
    inv_freq = ROPE_THETA ** (-jnp.arange(half, dtype=jnp.float32) / half)
    ang = positions.astype(jnp.float32)[:, :, None] * inv_freq
    return jnp.cos(ang)[:, :, None, None, :], jnp.sin(ang)[:, :, None, None, :]


def partial_rope(x, cos, sin):
    half = ROPE_DIM // 2
    xr = x[..., :ROPE_DIM].astype(jnp.float32)
    x1, x2 = xr[..., :half], xr[..., half:]
    rot = jnp.concatenate([x1 * cos - x2 * sin, x2 * cos + x1 * sin], axis=-1)
    return jnp.concatenate([rot.astype(x.dtype), x[..., ROPE_DIM:]], axis=-1)


def gla_chunked(q, k, v, log_a):
    B, S, H, _ = q.shape
    N = S // GLA_CHUNK
    def to_chunks(t):
        return t.astype(jnp.float32).reshape(B, N, GLA_CHUNK, H, -1).transpose(0, 3, 1, 2, 4)
    q, k, v, g = to_chunks(q), to_chunks(k), to_chunks(v), to_chunks(log_a)
    b = jnp.cumsum(g, axis=3)
    b_last = b[:, :, :, -1:, :]
    q_in = q * jnp.exp(b)
    k_in = k * jnp.exp(-b)
    causal = jnp.tril(jnp.ones((GLA_CHUNK, GLA_CHUNK), dtype=bool))
    att = jnp.where(causal, jnp.einsum('bhnck,bhnsk->bhncs', q_in, k_in), 0.0)
    o_intra = jnp.einsum('bhncs,bhnsv->bhncv', att, v)
    s_chunk = jnp.einsum('bhnsk,bhnsv->bhnkv', k * jnp.exp(b_last - b), v)
    decay = jnp.exp(b_last[:, :, :, 0, :])

    def step(state, inp):
        d, s_c = inp
        return state * d[..., None] + s_c, state

    init = jnp.zeros((B, H, GLA_DK, GLA_DV), jnp.float32)
    _, s_prev = lax.scan(step, init, (jnp.moveaxis(decay, 2, 0), jnp.moveaxis(s_chunk, 2, 0)))
    s_prev = jnp.moveaxis(s_prev, 0, 2)
    o = o_intra + jnp.einsum('bhnck,bhnkv->bhncv', q_in, s_prev)
    return o.transpose(0, 2, 3, 1, 4).reshape(B, S, H, GLA_DV)


def gla_branch(q, k, v, r, a_lr, alpha_up, alpha_b, out_g):
    B, S, _ = q.shape
    q = q.reshape(B, S, GLA_HEADS, GLA_DK) * (GLA_DK ** -0.5)
    k = k.reshape(B, S, GLA_HEADS, GLA_DK)
    v = v.reshape(B, S, GLA_HEADS, GLA_DV)
    log_a = jax.nn.log_sigmoid((a_lr @ alpha_up + alpha_b).astype(jnp.float32)) / GLA_GATE_TAU
    o = gla_chunked(q, k, v, log_a.reshape(B, S, GLA_HEADS, GLA_DK)).astype(v.dtype)
    o = rms_norm(o, out_g) * jax.nn.silu(r.reshape(B, S, GLA_HEADS, GLA_DV))
    return o.reshape(B, S, GLA_V_W)


def diff_branch(q, k, v, cos, sin, q_g, k_g, lq1, lk1, lq2, lk2, out_g, lambda_init):
    B, S, _ = q.shape
    q = partial_rope(rms_norm(q.reshape(B, S, DIFF_HEADS, 2, DIFF_DH), q_g), cos, sin)
    k = partial_rope(rms_norm(k.reshape(B, S, DIFF_HEADS, 2, DIFF_DH), k_g), cos, sin)
    q = q.transpose(0, 2, 3, 1, 4)
    k = k.transpose(0, 2, 3, 1, 4)
    v = v.reshape(B, S, DIFF_HEADS, DIFF_DV).transpose(0, 2, 1, 3)
    f32 = jnp.float32
    lam = (jnp.exp(jnp.sum(lq1.astype(f32) * lk1.astype(f32)))
           - jnp.exp(jnp.sum(lq2.astype(f32) * lk2.astype(f32))) + lambda_init)
    n_blk = S // Q_BLOCK
    qb = q.reshape(B, DIFF_HEADS, 2, n_blk, Q_BLOCK, DIFF_DH).transpose(3, 0, 1, 2, 4, 5)
    key_idx = jnp.arange(S)
    scale = DIFF_DH ** -0.5

    def one_block(args):
        q_blk, blk = args
        s = jnp.einsum('bhcqd,bhckd->bhcqk', q_blk, k, preferred_element_type=f32) * scale
        q_idx = blk * Q_BLOCK + jnp.arange(Q_BLOCK)
        s = jnp.where(key_idx[None, :] <= q_idx[:, None], s, NEG_INF)
        p = jax.nn.softmax(s, axis=-1)
        a = p[:, :, 0] - lam * p[:, :, 1]
        return jnp.einsum('bhqk,bhkv->bhqv', a.astype(v.dtype), v)

    o = lax.map(one_block, (qb, jnp.arange(n_blk)))
    o = o.transpose(1, 0, 3, 2, 4).reshape(B, S, DIFF_HEADS, DIFF_DV)
    o = rms_norm(o, out_g) * (1.0 - lambda_init)
    return o.reshape(B, S, DIFF_V_W)


def hier_moe(h, w_rg, b_rg, w_re, b_re, w1, w3, w2):
    B, S, D = h.shape
    T = B * S
    xf = h.reshape(T, D)
    p_g = jax.nn.softmax((xf @ w_rg + b_rg).astype(jnp.float32), axis=-1)
    p_top, g_idx = lax.top_k(p_g, 1)
    e_logits = (xf @ w_re + b_re).astype(jnp.float32).reshape(T, N_GROUPS, EXPERTS_PER_GROUP)
    e_in = e_logits[jnp.arange(T), g_idx[:, 0]]
    pe_top, e_loc = lax.top_k(jax.nn.softmax(e_in, axis=-1), TOP_K)
    pe_top = pe_top / jnp.sum(pe_top, axis=-1, keepdims=True)
    weights = p_top * pe_top
    expert_ids = g_idx * EXPERTS_PER_GROUP + e_loc
    A = T * TOP_K
    flat_e = expert_ids.reshape(A)
    flat_tok = jnp.repeat(jnp.arange(T, dtype=jnp.int32), TOP_K)
    flat_w = weights.reshape(A)
    order = jnp.argsort(flat_e)
    sorted_e = flat_e[order]
    counts = jnp.bincount(flat_e, length=N_EXPERTS)
    starts = jnp.cumsum(counts) - counts
    padded = ((counts + MOE_BLOCK - 1) // MOE_BLOCK) * MOE_BLOCK
    pad_ends = jnp.cumsum(padded)
    pad_starts = pad_ends - padded
    dest = pad_starts[sorted_e] + (jnp.arange(A) - starts[sorted_e])
    NB = (A + N_EXPERTS * (MOE_BLOCK - 1) + MOE_BLOCK - 1) // MOE_BLOCK
    P = NB * MOE_BLOCK
    slot_tok = jnp.full((P,), T, jnp.int32).at[dest].set(flat_tok[order])
    slot_w = jnp.zeros((P,), jnp.float32).at[dest].set(flat_w[order])
    block_e = jnp.minimum(jnp.searchsorted(pad_ends, jnp.arange(NB) * MOE_BLOCK, side='right'),
                          N_EXPERTS - 1)
    x_pad = jnp.concatenate([xf, jnp.zeros((1, D), xf.dtype)], axis=0)
    xb = x_pad[slot_tok].reshape(NB, MOE_BLOCK, D)

    def expert_block(args):
        xi, e = args
        return (jax.nn.silu(xi @ w1[e]) * (xi @ w3[e])) @ w2[e]

    yb = lax.map(expert_block, (xb, block_e)).reshape(P, D)
    yb = (yb * slot_w[:, None]).astype(h.dtype)
    out = jnp.zeros((T + 1, D), h.dtype).at[slot_tok].add(yb)[:T]
    return out.reshape(B, S, D)


def setup_inputs(seed: int = 0) -> dict:
    key = jax.random.key(seed)
    ks = jax.random.split(key, 32)
    L, D = DEPTH, D_MODEL
    def nrm(k, shape, scale):
        return jax.random.normal(k, shape, jnp.float32) * scale
    def gain(k, shape):
        return 1.0 + 0.01 * jax.random.normal(k, shape, jnp.float32)
    offsets = jax.random.randint(ks[2], (BATCH, 1), 0, 1024, dtype=jnp.int32)
    return {
        "x": nrm(ks[0], (BATCH, SEQ, D), 1.0),
        "c": nrm(ks[1], (BATCH, D), 1.0),
        "positions": offsets + jnp.arange(SEQ, dtype=jnp.int32)[None, :],
        "ada_w": nrm(ks[3], (L, D, N_MOD * D), D ** -0.5),
        "ada_b": nrm(ks[4], (L, N_MOD * D), 0.02),
        "norm1_g": gain(ks[5], (L, D)),
        "norm2_g": gain(ks[6], (L, D)),
        "w_in": nrm(ks[7], (L, D, D_IN), D ** -0.5),
        "gla_alpha_up": nrm(ks[8], (L, GLA_GATE_RANK, GLA_QK_W), GLA_GATE_RANK ** -0.5),
        "gla_alpha_b": nrm(ks[9], (L, GLA_QK_W), 0.1),
        "gla_out_g": gain(ks[10], (L, GLA_DV)),
        "diff_q_g": gain(ks[11], (L, DIFF_DH)),
        "diff_k_g": gain(ks[12], (L, DIFF_DH)),
        "diff_lq1": nrm(ks[13], (L, DIFF_DH), 0.1),
        "diff_lk1": nrm(ks[14], (L, DIFF_DH), 0.1),
        "diff_lq2": nrm(ks[15], (L, DIFF_DH), 0.1),
        "diff_lk2": nrm(ks[16], (L, DIFF_DH), 0.1),
        "diff_out_g": gain(ks[17], (L, DIFF_DV)),
        "w_branch_a": nrm(ks[18], (L, GLA_V_W, D), GLA_V_W ** -0.5),
        "w_branch_b": nrm(ks[19], (L, DIFF_V_W, D), DIFF_V_W ** -0.5),
        "w_out": nrm(ks[20], (L, D, D), D ** -0.5),
        "router_group_w": nrm(ks[21], (L, D, N_GROUPS), D ** -0.5),
        "router_group_b": nrm(ks[22], (L, N_GROUPS), 0.01),
        "router_expert_w": nrm(ks[23], (L, D, N_EXPERTS), D ** -0.5),
        "router_expert_b": nrm(ks[24], (L, N_EXPERTS), 0.01),
        "expert_w1": nrm(ks[25], (L, N_EXPERTS, D, D_EXPERT), D ** -0.5),
        "expert_w3": nrm(ks[26], (L, N_EXPERTS, D, D_EXPERT), D ** -0.5),
        "expert_w2": nrm(ks[27], (L, N_EXPERTS, D_EXPERT, D), D_EXPERT ** -0.5),
    }


def reference(x, c, positions, ada_w, ada_b, norm1_g, norm2_g, w_in, gla_alpha_up, gla_alpha_b,
              gla_out_g, diff_q_g, diff_k_g, diff_lq1, diff_lk1, diff_lq2, diff_lk2, diff_out_g,
              w_branch_a, w_branch_b, w_out, router_group_w, router_group_b, router_expert_w,
              router_expert_b, expert_w1, expert_w3, expert_w2):
    B, S, D = x.shape
    cos, sin = rope_tables(positions)
    c_act = jax.nn.silu(c)
    split_idx = np.cumsum(IN_SIZES)[:-1].tolist()
    for l in range(DEPTH):
        lambda_init = 0.8 - 0.6 * float(np.exp(-0.3 * l))
        mod = (c_act @ ada_w[l] + ada_b[l]).reshape(B, N_MOD, D)
        shift1, scale1, gate1, shift2, scale2, gate2 = [mod[:, i, None, :] for i in range(N_MOD)]
        h = rms_norm(x, norm1_g[l]) * (1.0 + scale1) + shift1
        u = h @ w_in[l]
        (g_q, g_k, g_v, g_r, g_lr, d_q, d_k, d_v, gate_a, gate_b) = jnp.split(u, split_idx, axis=-1)
        o_a = gla_branch(g_q, g_k, g_v, g_r, g_lr, gla_alpha_up[l], gla_alpha_b[l], gla_out_g[l])
        o_b = diff_branch(d_q, d_k, d_v, cos, sin, diff_q_g[l], diff_k_g[l], diff_lq1[l],
                          diff_lk1[l], diff_lq2[l], diff_lk2[l], diff_out_g[l], lambda_init)
        merged = (jax.nn.sigmoid(gate_a) * (o_a @ w_branch_a[l])
                  + jax.nn.sigmoid(gate_b) * (o_b @ w_branch_b[l]))
        x = x + gate1 * (merged @ w_out[l])
        h2 = rms_norm(x, norm2_g[l]) * (1.0 + scale2) + shift2
        y = hier_moe(h2, router_group_w[l], router_group_b[l], router_expert_w[l],
                     router_expert_b[l], expert_w1[l], expert_w3[l], expert_w2[l])
        x = x + gate2 * y
    return x
```

```python
import functools

import jax
import jax.numpy as jnp
from jax import lax
from jax.experimental import pallas as pl
from jax.experimental.pallas import tpu as pltpu

F32 = jnp.float32
BF16 = jnp.bfloat16
I32 = jnp.int32

D_MODEL = 1024
BATCH = 4
SEQ = 4096
TOKENS = BATCH * SEQ
N_MOD = 6
NORM_EPS = 1e-6

GLA_HEADS = 4
GLA_DK = 64
GLA_DV = 128
GLA_GATE_RANK = 16
GLA_GATE_TAU = 16.0
GLA_CHUNK = 64
GLA_QK_W = GLA_HEADS * GLA_DK
GLA_V_W = GLA_HEADS * GLA_DV

DIFF_HEADS = 4
DIFF_DH = 64
DIFF_DV = 2 * DIFF_DH
DIFF_QK_W = DIFF_HEADS * 2 * DIFF_DH
DIFF_V_W = DIFF_HEADS * DIFF_DV
ROPE_THETA = 500000.0
ROPE_DIM = DIFF_DH // 4
ROPE_HALF = ROPE_DIM // 2
NEG_INF = -1e30
LAMBDA_INIT = 0.8 - 0.6 * 1.0

N_GROUPS = 4
EXPERTS_PER_GROUP = 8
N_EXPERTS = N_GROUPS * EXPERTS_PER_GROUP
TOP_K = 2
D_EXPERT = 512

LANES = 128
LR_PAD = LANES
D_IN_PAD = 2 * GLA_QK_W + 2 * GLA_V_W + 2 * DIFF_QK_W + DIFF_V_W + 2 * D_MODEL + LR_PAD

TM_IN = 512
TQ = 512
N_KV = SEQ // TQ
TT_GLA = 512
TM_MERGE = 512
TM_RANK = 256
TM_ROW = 256
MOE_BLK = 256
N_BLOCKS = (TOKENS * TOP_K + N_EXPERTS * (MOE_BLK - 1) + MOE_BLK - 1) // MOE_BLK
ROUTE_W = 8

VMEM_LIMIT = 56 * 1024 * 1024


def _cparams(sem):
    return pltpu.CompilerParams(dimension_semantics=sem, vmem_limit_bytes=VMEM_LIMIT)


def _sigmoid(x):
    return 1.0 / (1.0 + jnp.exp(-x))


def _mod_kernel(ct_ref, w_ref, b_ref, o_ref):
    ct = ct_ref[...]
    ca = ct * _sigmoid(ct)
    w = w_ref[...]
    rows = []
    for b in range(BATCH):
        rows.append(jnp.sum(w * ca[:, b:b + 1], axis=0, keepdims=True) + b_ref[...])
    rows.append(jnp.zeros((8 - BATCH, w.shape[1]), F32))
    o_ref[...] = jnp.concatenate(rows, axis=0)


def _mod_call(c, ada_w, ada_b):
    tn = D_MODEL
    ct = jnp.zeros((D_MODEL, 8), F32).at[:, :BATCH].set(c.T)
    out = pl.pallas_call(
        _mod_kernel,
        grid=(N_MOD,),
        in_specs=[
            pl.BlockSpec((D_MODEL, 8), lambda j: (0, 0)),
            pl.BlockSpec((D_MODEL, tn), lambda j: (0, j)),
            pl.BlockSpec((1, tn), lambda j: (0, j)),
        ],
        out_specs=pl.BlockSpec((8, tn), lambda j: (0, j)),
        out_shape=jax.ShapeDtypeStruct((8, N_MOD * D_MODEL), F32),
        compiler_params=_cparams(("arbitrary",)),
        name="mod",
    )(ct, ada_w, ada_b.reshape(1, N_MOD * D_MODEL))
    return out[:BATCH].reshape(BATCH, N_MOD, D_MODEL)


def _in_kernel(x_ref, mod_ref, g1_ref, w_ref, bd_ref, qg_ref, kg_ref, pos_ref, invf_ref,
               gqk_ref, gv_ref, gr_ref, lr_ref, qT_ref, k_ref, vT_ref, sa_ref, sb_ref):
    x = x_ref[...]
    shift1 = mod_ref[0, 0:1, :]
    scale1 = mod_ref[0, 1:2, :]
    ms = jnp.mean(x * x, axis=-1, keepdims=True)
    h = (x * lax.rsqrt(ms + NORM_EPS) * g1_ref[...]) * (1.0 + scale1) + shift1
    hb = h.astype(BF16)

    def proj(c0, c1):
        return jnp.dot(hb, w_ref[:, c0:c1], preferred_element_type=F32)

    gqk_ref[...] = proj(0, 512).astype(BF16)
    gv_ref[...] = proj(512, 1024).astype(BF16)
    gr_ref[...] = proj(1024, 1536).astype(BF16)
    sa_ref[...] = _sigmoid(proj(3072, 4096)).astype(BF16)
    sb_ref[...] = _sigmoid(proj(4096, 5120)).astype(BF16)
    lr_ref[...] = proj(5120, 5248)

    tm = x.shape[0]
    ang = pos_ref[...].astype(F32) * invf_ref[...]
    cos = jnp.cos(ang)
    sin = jnp.sin(ang)
    cos4 = jnp.concatenate([cos] * DIFF_HEADS, axis=1)
    sin4 = jnp.concatenate([sin] * DIFF_HEADS, axis=1)
    lane = lax.broadcasted_iota(I32, (tm, DIFF_QK_W), 1)
    first_half = (lane % DIFF_DH) < ROPE_HALF
    bd = bd_ref[...]

    def norm_rope(t, gain_row):
        t2 = t * t
        hi = t2.astype(BF16)
        lo = (t2 - hi.astype(F32)).astype(BF16)
        gms = (jnp.dot(hi, bd, preferred_element_type=F32)
               + jnp.dot(lo, bd, preferred_element_type=F32))
        t = t * lax.rsqrt(gms + NORM_EPS) * gain_row
        nxt = pltpu.roll(t, DIFF_QK_W - ROPE_HALF, 1)
        prv = pltpu.roll(t, ROPE_HALF, 1)
        return t * cos4 + jnp.where(first_half, -nxt, prv) * sin4

    dq = norm_rope(proj(1536, 2048), qg_ref[...]) * (DIFF_DH ** -0.5)
    dk = norm_rope(proj(2048, 2560), kg_ref[...])
    dv = proj(2560, 3072)
    for hd in range(DIFF_HEADS):
        sl = slice(hd * LANES, (hd + 1) * LANES)
        qT_ref[0, hd] = dq[:, sl].T.astype(BF16)
        k_ref[0, hd, 0] = dk[:, sl].astype(BF16)
        vT_ref[0, hd, 0] = dv[:, sl].T.astype(BF16)


def _in_call(x2, mod, norm1_g, w_in_p, bd, qg_row, kg_row, pos_col, invf_row):
    nb = SEQ // TM_IN
    tok_spec = lambda w: pl.BlockSpec((TM_IN, w), lambda i: (i, 0))
    const2 = lambda r, c: pl.BlockSpec((r, c), lambda i: (0, 0))
    out_shapes = (
        jax.ShapeDtypeStruct((TOKENS, 2 * GLA_QK_W), BF16),
        jax.ShapeDtypeStruct((TOKENS, GLA_V_W), BF16),
        jax.ShapeDtypeStruct((TOKENS, GLA_V_W), BF16),
        jax.ShapeDtypeStruct((TOKENS, LR_PAD), F32),
        jax.ShapeDtypeStruct((BATCH, DIFF_HEADS, LANES, SEQ), BF16),
        jax.ShapeDtypeStruct((BATCH, DIFF_HEADS, N_KV, TQ, LANES), BF16),
        jax.ShapeDtypeStruct((BATCH, DIFF_HEADS, N_KV, LANES, TQ), BF16),
        jax.ShapeDtypeStruct((TOKENS, D_MODEL), BF16),
        jax.ShapeDtypeStruct((TOKENS, D_MODEL), BF16),
    )
    out_specs = (
        tok_spec(2 * GLA_QK_W), tok_spec(GLA_V_W), tok_spec(GLA_V_W), tok_spec(LR_PAD),
        pl.BlockSpec((1, DIFF_HEADS, LANES, TM_IN), lambda i: (i // nb, 0, 0, i % nb)),
        pl.BlockSpec((1, DIFF_HEADS, 1, TM_IN, LANES), lambda i: (i // nb, 0, i % nb, 0, 0)),
        pl.BlockSpec((1, DIFF_HEADS, 1, LANES, TM_IN), lambda i: (i // nb, 0, i % nb, 0, 0)),
        tok_spec(D_MODEL), tok_spec(D_MODEL),
    )
    return pl.pallas_call(
        _in_kernel,
        grid=(TOKENS // TM_IN,),
        in_specs=[
            tok_spec(D_MODEL),
            pl.BlockSpec((1, N_MOD, D_MODEL), lambda i: (i // nb, 0, 0)),
            const2(1, D_MODEL),
            pl.BlockSpec((D_MODEL, D_IN_PAD), lambda i: (0, 0), pipeline_mode=pl.Buffered(1)),
            const2(DIFF_QK_W, DIFF_QK_W),
            const2(1, DIFF_QK_W), const2(1, DIFF_QK_W),
            tok_spec(1),
            const2(1, LANES),
        ],
        out_specs=out_specs,
        out_shape=out_shapes,
        compiler_params=_cparams(("parallel",)),
        name="in_proj",
    )(x2, mod, norm1_g, w_in_p, bd, qg_row, kg_row, pos_col, invf_row)


def _gla_kernel(qk_ref, v_ref, r_ref, lr_ref, au_ref, ab_ref, og_ref, o_ref, state_ref, oacc_ref):
    tt = qk_ref.shape[0]
    n_chunks = tt // GLA_CHUNK

    @pl.when(pl.program_id(1) == 0)
    def _():
        state_ref[...] = jnp.zeros_like(state_ref)

    z = jnp.dot(lr_ref[...], au_ref[...], preferred_element_type=F32,
                precision=lax.Precision.HIGHEST) + ab_ref[...]
    g = (jnp.minimum(z, 0.0) - jnp.log1p(jnp.exp(-jnp.abs(z)))) * (1.0 / GLA_GATE_TAU)

    row = lax.broadcasted_iota(I32, (tt, GLA_QK_W), 0) % GLA_CHUNK
    b = g
    step = 1
    while step < GLA_CHUNK:
        b = b + jnp.where(row >= step, pltpu.roll(b, step, 0), 0.0)
        step *= 2

    b_last_rows = [b[c * GLA_CHUNK + GLA_CHUNK - 1:(c + 1) * GLA_CHUNK, :] for c in range(n_chunks)]
    b_last = jnp.concatenate(
        [jnp.broadcast_to(bl, (GLA_CHUNK, GLA_QK_W)) for bl in b_last_rows], axis=0)

    qk = qk_ref[...].astype(F32)
    q = qk[:, :GLA_QK_W] * (GLA_DK ** -0.5)
    k = qk[:, GLA_QK_W:]
    q_in = (q * jnp.exp(b)).astype(BF16)
    k_in = (k * jnp.exp(-b)).astype(BF16)
    k_dec = (k * jnp.exp(b_last - b)).astype(BF16)

    ci = lax.broadcasted_iota(I32, (GLA_CHUNK, GLA_CHUNK), 0)
    cj = lax.broadcasted_iota(I32, (GLA_CHUNK, GLA_CHUNK), 1)
    causal = ci >= cj
    eye = ci == cj

    for c in range(n_chunks):
        rs = slice(c * GLA_CHUNK, (c + 1) * GLA_CHUNK)
        decay_row = jnp.exp(b_last_rows[c])
        for hd in range(GLA_HEADS):
            ks = slice(hd * GLA_DK, (hd + 1) * GLA_DK)
            vs = slice(hd * GLA_DV, (hd + 1) * GLA_DV)
            qc = q_in[rs, ks]
            kc = k_in[rs, ks]
            kd = k_dec[rs, ks]
            vc = v_ref[rs, vs]
            att = lax.dot_general(qc, kc, (((1,), (1,)), ((), ())), preferred_element_type=F32)
            att = jnp.where(causal, att, 0.0)
            s_prev = state_ref[hd]
            o = (jnp.dot(att.astype(BF16), vc, preferred_element_type=F32)
                 + jnp.dot(qc, s_prev.astype(BF16), preferred_element_type=F32))
            oacc_ref[rs, vs] = o
            dcol = jnp.sum(jnp.where(eye, decay_row[:, ks], 0.0), axis=1, keepdims=True)
            s_new = lax.dot_general(kd, vc, (((0,), (0,)), ((), ())), preferred_element_type=F32)
            state_ref[hd] = s_prev * dcol + s_new

    for hd in range(GLA_HEADS):
        vs = slice(hd * GLA_DV, (hd + 1) * GLA_DV)
        oh = oacc_ref[:, vs]
        ms = jnp.mean(oh * oh, axis=-1, keepdims=True)
        y = oh * lax.rsqrt(ms + NORM_EPS) * og_ref[...]
        r = r_ref[:, vs].astype(F32)
        o_ref[:, vs] = (y * (r * _sigmoid(r))).astype(BF16)


def _gla_call(gqk, gv, gr, lr, au_pad, ab_row, og_row):
    nt = SEQ // TT_GLA
    tok = lambda w: pl.BlockSpec((TT_GLA, w), lambda b, t: (b * nt + t, 0))
    const2 = lambda r, c: pl.BlockSpec((r, c), lambda b, t: (0, 0))
    return pl.pallas_call(
        _gla_kernel,
        grid=(BATCH, nt),
        in_specs=[tok(2 * GLA_QK_W), tok(GLA_V_W), tok(GLA_V_W), tok(LR_PAD),
                  const2(LR_PAD, GLA_QK_W), const2(1, GLA_QK_W), const2(1, GLA_DV)],
        out_specs=tok(GLA_V_W),
        out_shape=jax.ShapeDtypeStruct((TOKENS, GLA_V_W), BF16),
        scratch_shapes=[pltpu.VMEM((GLA_HEADS, GLA_DK, GLA_DV), F32),
                        pltpu.VMEM((TT_GLA, GLA_V_W), F32)],
        compiler_params=_cparams(("parallel", "arbitrary")),
        name="gla",
    )(gqk, gv, gr, lr, au_pad, ab_row, og_row)


def _attn_kernel(qT_ref, k_ref, vT_ref, lam_ref, og_ref, o_ref):
    i = pl.program_id(2)
    qT = qT_ref[0, 0]
    rowq = lax.broadcasted_iota(I32, qT.shape, 0)
    zero = jnp.zeros_like(qT)
    q1 = jnp.where(rowq < DIFF_DH, qT, zero)
    q2 = jnp.where(rowq >= DIFF_DH, qT, zero)
    key_i = lax.broadcasted_iota(I32, (TQ, TQ), 0)
    qry_i = lax.broadcasted_iota(I32, (TQ, TQ), 1)
    diag_mask = key_i <= qry_i

    def online(s, vb, m, l, acc):
        m_new = jnp.maximum(m, jnp.max(s, axis=0, keepdims=True))
        alpha = jnp.exp(m - m_new)
        p = jnp.exp(s - m_new)
        l = l * alpha + jnp.sum(p, axis=0, keepdims=True)
        acc = acc * alpha + jnp.dot(vb, p.astype(BF16), preferred_element_type=F32)
        return m_new, l, acc

    def block(j, carry, masked):
        m1, l1, a1, m2, l2, a2 = carry
        kb = k_ref[0, 0, j]
        vb = vT_ref[0, 0, j]
        s1 = jnp.dot(kb, q1, preferred_element_type=F32)
        s2 = jnp.dot(kb, q2, preferred_element_type=F32)
        if masked:
            s1 = jnp.where(diag_mask, s1, NEG_INF)
            s2 = jnp.where(diag_mask, s2, NEG_INF)
        m1, l1, a1 = online(s1, vb, m1, l1, a1)
        m2, l2, a2 = online(s2, vb, m2, l2, a2)
        return m1, l1, a1, m2, l2, a2

    m0 = jnp.full((1, TQ), NEG_INF, F32)
    l0 = jnp.zeros((1, TQ), F32)
    a0 = jnp.zeros((DIFF_DV, TQ), F32)
    carry = lax.fori_loop(0, i, lambda j, c: block(j, c, False), (m0, l0, a0, m0, l0, a0))
    m1, l1, a1, m2, l2, a2 = block(i, carry, True)

    lam_p = lam_ref[...]
    lam = (jnp.exp(jnp.sum(lam_p[0:1] * lam_p[1:2], axis=1, keepdims=True))
           - jnp.exp(jnp.sum(lam_p[2:3] * lam_p[3:4], axis=1, keepdims=True)) + LAMBDA_INIT)
    oT = a1 / l1 - lam * (a2 / l2)
    ms = jnp.mean(oT * oT, axis=0, keepdims=True)
    y = oT * lax.rsqrt(ms + NORM_EPS) * og_ref[...] * (1.0 - LAMBDA_INIT)
    o_ref[...] = y.T.astype(BF16)


def _attn_call(qT, kk, vT, lam_p, og_col):
    nq = SEQ // TQ
    return pl.pallas_call(
        _attn_kernel,
        grid=(BATCH, DIFF_HEADS, nq),
        in_specs=[
            pl.BlockSpec((1, 1, LANES, TQ), lambda b, h, i: (b, h, 0, i)),
            pl.BlockSpec((1, 1, N_KV, TQ, LANES), lambda b, h, i: (b, h, 0, 0, 0)),
            pl.BlockSpec((1, 1, N_KV, LANES, TQ), lambda b, h, i: (b, h, 0, 0, 0)),
            pl.BlockSpec((4, DIFF_DH), lambda b, h, i: (0, 0)),
            pl.BlockSpec((DIFF_DV, 1), lambda b, h, i: (0, 0)),
        ],
        out_specs=pl.BlockSpec((TQ, DIFF_DV), lambda b, h, i: (b * nq + i, h)),
        out_shape=jax.ShapeDtypeStruct((TOKENS, DIFF_V_W), BF16),
        compiler_params=_cparams(("parallel", "parallel", "arbitrary")),
        name="attn",
    )(qT, kk, vT, lam_p, og_col)


def _merge_kernel(x_ref, oa_ref, ob_ref, sa_ref, sb_ref, mod_ref, g2_ref, wa_ref, wb_ref, wo_ref,
                  wr_ref, br_ref, x1_ref, h2_ref, rf_ref, ri_ref):
    tm = x_ref.shape[0]
    ma = jnp.dot(oa_ref[...], wa_ref[...], preferred_element_type=F32)
    mb = jnp.dot(ob_ref[...], wb_ref[...], preferred_element_type=F32)
    merged = sa_ref[...].astype(F32) * ma + sb_ref[...].astype(F32) * mb
    y = jnp.dot(merged.astype(BF16), wo_ref[...], preferred_element_type=F32)
    gate1 = mod_ref[0, 2:3, :]
    shift2 = mod_ref[0, 3:4, :]
    scale2 = mod_ref[0, 4:5, :]
    x1 = x_ref[...] + gate1 * y
    x1_ref[...] = x1
    ms = jnp.mean(x1 * x1, axis=-1, keepdims=True)
    h2 = (x1 * lax.rsqrt(ms + NORM_EPS) * g2_ref[...]) * (1.0 + scale2) + shift2
    h2_ref[...] = h2

    logits = jnp.dot(h2, wr_ref[...], preferred_element_type=F32,
                     precision=lax.Precision.HIGHEST) + br_ref[...]
    lane = lax.broadcasted_iota(I32, (tm, LANES), 1).astype(F32)
    ninf = -jnp.inf
    big = float(LANES)

    def first_argmax(v):
        vmax = jnp.max(v, axis=1, keepdims=True)
        idx = jnp.min(jnp.where(v == vmax, lane, big), axis=1, keepdims=True)
        return vmax, idx

    gl = jnp.where(lane < N_GROUPS, logits, ninf)
    gmax, gidx = first_argmax(gl)
    p_top = 1.0 / jnp.sum(jnp.exp(gl - gmax), axis=1, keepdims=True)
    lo = N_GROUPS + EXPERTS_PER_GROUP * gidx
    el = jnp.where((lane >= lo) & (lane < lo + EXPERTS_PER_GROUP), logits, ninf)
    e1max, e1 = first_argmax(el)
    e2max, e2 = first_argmax(jnp.where(lane == e1, ninf, el))
    t = jnp.exp(e2max - e1max)
    w1 = 1.0 / (1.0 + t)
    w2 = t / (1.0 + t)
    col = lax.broadcasted_iota(I32, (tm, ROUTE_W), 1)
    rf_ref[...] = jnp.where(col == 0, p_top * w1, jnp.where(col == 1, p_top * w2, 0.0))
    ids = jnp.where(col == 0, e1 - N_GROUPS, jnp.where(col == 1, e2 - N_GROUPS, 0.0))
    ri_ref[...] = ids.astype(I32)


def _merge_call(x2, o_a, o_b, sa, sb, mod, norm2_g, wa, wb, wo, wr, br):
    nb = SEQ // TM_MERGE
    tok = lambda w: pl.BlockSpec((TM_MERGE, w), lambda i: (i, 0))
    const2 = lambda r, c: pl.BlockSpec((r, c), lambda i: (0, 0))
    return pl.pallas_call(
        _merge_kernel,
        grid=(TOKENS // TM_MERGE,),
        in_specs=[tok(D_MODEL), tok(GLA_V_W), tok(DIFF_V_W), tok(D_MODEL), tok(D_MODEL),
                  pl.BlockSpec((1, N_MOD, D_MODEL), lambda i: (i // nb, 0, 0)),
                  const2(1, D_MODEL),
                  const2(GLA_V_W, D_MODEL), const2(DIFF_V_W, D_MODEL), const2(D_MODEL, D_MODEL),
                  const2(D_MODEL, LANES), const2(1, LANES)],
        out_specs=(tok(D_MODEL), tok(D_MODEL), tok(ROUTE_W), tok(ROUTE_W)),
        out_shape=(jax.ShapeDtypeStruct((TOKENS, D_MODEL), F32),
                   jax.ShapeDtypeStruct((TOKENS, D_MODEL), F32),
                   jax.ShapeDtypeStruct((TOKENS, ROUTE_W), F32),
                   jax.ShapeDtypeStruct((TOKENS, ROUTE_W), I32)),
        compiler_params=_cparams(("parallel",)),
        name="merge",
    )(x2, o_a, o_b, sa, sb, mod, norm2_g, wa, wb, wo, wr, br)


def _rank_kernel(ri_ref, rank_ref, cnt_ref, carry_ref):
    tm = ri_ref.shape[0]

    @pl.when(pl.program_id(0) == 0)
    def _():
        carry_ref[...] = jnp.zeros_like(carry_ref)

    ri = ri_ref[...]
    e1 = ri[:, 0:1]
    e2 = ri[:, 1:2]
    lane = lax.broadcasted_iota(I32, (tm, LANES), 1)
    hit1 = lane == e1
    hit2 = lane == e2
    onehot = jnp.where(hit1 | hit2, 1.0, 0.0)
    ti = lax.broadcasted_iota(I32, (tm, tm), 0)
    tj = lax.broadcasted_iota(I32, (tm, tm), 1)
    strict_lower = jnp.where(ti > tj, 1.0, 0.0).astype(BF16)
    before = jnp.dot(strict_lower, onehot.astype(BF16), preferred_element_type=F32) + carry_ref[...]
    r1 = jnp.sum(jnp.where(hit1, before, 0.0), axis=1, keepdims=True)
    r2 = jnp.sum(jnp.where(hit2, before, 0.0), axis=1, keepdims=True)
    col = lax.broadcasted_iota(I32, (tm, ROUTE_W), 1)
    rank_ref[...] = jnp.where(col == 0, r1, jnp.where(col == 1, r2, 0.0)).astype(I32)
    carry_ref[...] = carry_ref[...] + jnp.sum(onehot, axis=0, keepdims=True)
    cnt_ref[...] = carry_ref[...]


def _rank_call(ri):
    return pl.pallas_call(
        _rank_kernel,
        grid=(TOKENS // TM_RANK,),
        in_specs=[pl.BlockSpec((TM_RANK, ROUTE_W), lambda i: (i, 0))],
        out_specs=(pl.BlockSpec((TM_RANK, ROUTE_W), lambda i: (i, 0)),
                   pl.BlockSpec((1, LANES), lambda i: (0, 0))),
        out_shape=(jax.ShapeDtypeStruct((TOKENS, ROUTE_W), I32),
                   jax.ShapeDtypeStruct((1, LANES), F32)),
        scratch_shapes=[pltpu.VMEM((1, LANES), F32)],
        compiler_params=_cparams(("arbitrary",)),
        name="rank",
    )(ri)


def _row_copy(src_ref, src_row, dst_ref, dst_row, sem):
    return pltpu.make_async_copy(src_ref.at[pl.ds(src_row, 1)], dst_ref.at[pl.ds(dst_row, 1)], sem)


def _dispatch_kernel(dest_ref, h2_ref, xs_in_ref, xs_ref, sem):
    del xs_in_ref
    tm = h2_ref.shape[0]
    base = pl.program_id(0) * tm

    def issue(r, carry):
        for kk in range(TOP_K):
            _row_copy(h2_ref, r, xs_ref, dest_ref[(base + r) * TOP_K + kk], sem).start()
        return carry

    lax.fori_loop(0, tm, issue, 0)

    def drain(r, carry):
        for kk in range(TOP_K):
            _row_copy(h2_ref, r, xs_ref, dest_ref[(base + r) * TOP_K + kk], sem).wait()
        return carry

    lax.fori_loop(0, tm, drain, 0)


def _dispatch_call(dest, h2):
    xs0 = jnp.zeros((N_BLOCKS * MOE_BLK, D_MODEL), F32)
    grid_spec = pltpu.PrefetchScalarGridSpec(
        num_scalar_prefetch=1,
        grid=(TOKENS // TM_ROW,),
        in_specs=[pl.BlockSpec((TM_ROW, D_MODEL), lambda i, d: (i, 0)),
                  pl.BlockSpec(memory_space=pl.ANY)],
        out_specs=pl.BlockSpec(memory_space=pl.ANY),
        scratch_shapes=[pltpu.SemaphoreType.DMA(())],
    )
    return pl.pallas_call(
        _dispatch_kernel,
        grid_spec=grid_spec,
        out_shape=jax.ShapeDtypeStruct(xs0.shape, F32),
        input_output_aliases={2: 0},
        compiler_params=_cparams(("arbitrary",)),
        name="dispatch",
    )(dest, h2, xs0)


def _expert_kernel(be_ref, nu_ref, xs_ref, w1_ref, w3_ref, w2_ref, y_ref, w1b, w3b, w2b):
    i = pl.program_id(0)
    e = be_ref[i]
    prev = be_ref[jnp.maximum(i - 1, 0)]
    used = i < nu_ref[0]

    @pl.when(used & ((i == 0) | (e != prev)))
    def _():
        w1b[...] = w1_ref[0].astype(BF16)
        w3b[...] = w3_ref[0].astype(BF16)
        w2b[...] = w2_ref[0].astype(BF16)

    @pl.when(used)
    def _():
        xb = xs_ref[...].astype(BF16)
        a = jnp.dot(xb, w1b[...], preferred_element_type=F32)
        g = jnp.dot(xb, w3b[...], preferred_element_type=F32)
        hmid = (a * _sigmoid(a)) * g
        y_ref[...] = jnp.dot(hmid.astype(BF16), w2b[...], preferred_element_type=F32)

    @pl.when(jnp.logical_not(used))
    def _():
        y_ref[...] = jnp.zeros_like(y_ref)


def _expert_call(block_e, n_used, xs, w1, w3, w2):
    grid_spec = pltpu.PrefetchScalarGridSpec(
        num_scalar_prefetch=2,
        grid=(N_BLOCKS,),
        in_specs=[pl.BlockSpec((MOE_BLK, D_MODEL), lambda i, be, nu: (i, 0)),
                  pl.BlockSpec((1, D_MODEL, D_EXPERT), lambda i, be, nu: (be[i], 0, 0)),
                  pl.BlockSpec((1, D_MODEL, D_EXPERT), lambda i, be, nu: (be[i], 0, 0)),
                  pl.BlockSpec((1, D_EXPERT, D_MODEL), lambda i, be, nu: (be[i], 0, 0))],
        out_specs=pl.BlockSpec((MOE_BLK, D_MODEL), lambda i, be, nu: (i, 0)),
        scratch_shapes=[pltpu.VMEM((D_MODEL, D_EXPERT), BF16),
                        pltpu.VMEM((D_MODEL, D_EXPERT), BF16),
                        pltpu.VMEM((D_EXPERT, D_MODEL), BF16)],
    )
    return pl.pallas_call(
        _expert_kernel,
        grid_spec=grid_spec,
        out_shape=jax.ShapeDtypeStruct((N_BLOCKS * MOE_BLK, D_MODEL), F32),
        compiler_params=_cparams(("arbitrary",)),
        name="experts",
    )(block_e, n_used, xs, w1, w3, w2)


def _combine_kernel(dest_ref, x1_ref, rf_ref, mod_ref, y_ref, o_ref, ybuf, sem):
    tm = x1_ref.shape[0]
    base = pl.program_id(0) * tm

    def issue(r, carry):
        for kk in range(TOP_K):
            _row_copy(y_ref, dest_ref[(base + r) * TOP_K + kk], ybuf.at[kk], r, sem).start()
        return carry

    lax.fori_loop(0, tm, issue, 0)

    def drain(r, carry):
        for kk in range(TOP_K):
            _row_copy(y_ref, dest_ref[(base + r) * TOP_K + kk], ybuf.at[kk], r, sem).wait()
        return carry

    lax.fori_loop(0, tm, drain, 0)

    rf = rf_ref[...]
    gate2 = mod_ref[0, 5:6, :]
    moe = rf[:, 0:1] * ybuf[0] + rf[:, 1:2] * ybuf[1]
    o_ref[...] = x1_ref[...] + gate2 * moe


def _combine_call(dest, x1, rf, mod, y):
    nb = SEQ // TM_ROW
    grid_spec = pltpu.PrefetchScalarGridSpec(
        num_scalar_prefetch=1,
        grid=(TOKENS // TM_ROW,),
        in_specs=[pl.BlockSpec((TM_ROW, D_MODEL), lambda i, d: (i, 0)),
                  pl.BlockSpec((TM_ROW, ROUTE_W), lambda i, d: (i, 0)),
                  pl.BlockSpec((1, N_MOD, D_MODEL), lambda i, d: (i // nb, 0, 0)),
                  pl.BlockSpec(memory_space=pl.ANY)],
        out_specs=pl.BlockSpec((TM_ROW, D_MODEL), lambda i, d: (i, 0)),
        scratch_shapes=[pltpu.VMEM((TOP_K, TM_ROW, D_MODEL), F32),
                        pltpu.SemaphoreType.DMA(())],
    )
    return pl.pallas_call(
        _combine_kernel,
        grid_spec=grid_spec,
        out_shape=jax.ShapeDtypeStruct((TOKENS, D_MODEL), F32),
        compiler_params=_cparams(("arbitrary",)),
        name="combine",
    )(dest, x1, rf, mod, y)


def kernel(x, c, positions, ada_w, ada_b, norm1_g, norm2_g, w_in, gla_alpha_up, gla_alpha_b,
           gla_out_g, diff_q_g, diff_k_g, diff_lq1, diff_lk1, diff_lq2, diff_lk2, diff_out_g,
           w_branch_a, w_branch_b, w_out, router_group_w, router_group_b, router_expert_w,
           router_expert_b, expert_w1, expert_w3, expert_w2):
    assert x.shape == (BATCH, SEQ, D_MODEL) and ada_w.shape[0] == 1
    x2 = x.reshape(TOKENS, D_MODEL)

    lr0 = 2 * GLA_QK_W + 2 * GLA_V_W
    w_in0 = w_in[0]
    w_in_p = jnp.concatenate(
        [w_in0[:, :lr0], w_in0[:, lr0 + GLA_GATE_RANK:], w_in0[:, lr0:lr0 + GLA_GATE_RANK],
         jnp.zeros((D_MODEL, LR_PAD - GLA_GATE_RANK), F32)], axis=1).astype(BF16)
    au_pad = jnp.zeros((LR_PAD, GLA_QK_W), F32).at[:GLA_GATE_RANK].set(gla_alpha_up[0])
    gid = jnp.arange(DIFF_QK_W) // DIFF_DH
    bd = jnp.where(gid[:, None] == gid[None, :], 1.0 / DIFF_DH, 0.0).astype(BF16)
    qg_row = jnp.tile(diff_q_g[0], DIFF_QK_W // DIFF_DH).reshape(1, DIFF_QK_W)
    kg_row = jnp.tile(diff_k_g[0], DIFF_QK_W // DIFF_DH).reshape(1, DIFF_QK_W)
    inv_freq = ROPE_THETA ** (-jnp.arange(ROPE_HALF, dtype=F32) / ROPE_HALF)
    invf64 = jnp.concatenate([inv_freq, inv_freq, jnp.zeros((DIFF_DH - ROPE_DIM,), F32)])
    invf_row = jnp.tile(invf64, LANES // DIFF_DH).reshape(1, LANES)
    pos_col = positions.reshape(TOKENS, 1)
    lam_p = jnp.concatenate([diff_lq1, diff_lk1, diff_lq2, diff_lk2], axis=0)
    wr = (jnp.zeros((D_MODEL, LANES), F32)
          .at[:, :N_GROUPS].set(router_group_w[0])
          .at[:, N_GROUPS:N_GROUPS + N_EXPERTS].set(router_expert_w[0]))
    br = (jnp.zeros((1, LANES), F32)
          .at[0, :N_GROUPS].set(router_group_b[0])
          .at[0, N_GROUPS:N_GROUPS + N_EXPERTS].set(router_expert_b[0]))

    mod = _mod_call(c, ada_w[0], ada_b[0])
    gqk, gv, gr, lr, qT, kk, vT, sa, sb = _in_call(
        x2, mod, norm1_g, w_in_p, bd, qg_row, kg_row, pos_col, invf_row)
    o_a = _gla_call(gqk, gv, gr, lr, au_pad, gla_alpha_b, gla_out_g)
    o_b = _attn_call(qT, kk, vT, lam_p, diff_out_g.reshape(DIFF_DV, 1))
    x1, h2, rf, ri = _merge_call(
        x2, o_a, o_b, sa, sb, mod, norm2_g, w_branch_a[0].astype(BF16),
        w_branch_b[0].astype(BF16), w_out[0].astype(BF16), wr, br)
    rank, cnt = _rank_call(ri)

    counts = cnt[0, :N_EXPERTS].astype(I32)
    padded = ((counts + MOE_BLK - 1) // MOE_BLK) * MOE_BLK
    pad_ends = jnp.cumsum(padded)
    pad_starts = pad_ends - padded
    dest = (pad_starts[ri[:, :TOP_K]] + rank[:, :TOP_K]).reshape(TOKENS * TOP_K)
    n_used = (pad_ends[-1:] // MOE_BLK).astype(I32)
    block_start = jnp.arange(N_BLOCKS, dtype=I32) * MOE_BLK
    block_e = jnp.minimum(
        jnp.sum((pad_ends[None, :] <= block_start[:, None]).astype(I32), axis=1), N_EXPERTS - 1)

    xs = _dispatch_call(dest, h2)
    y = _expert_call(block_e, n_used, xs, expert_w1[0], expert_w3[0], expert_w2[0])
    out = _combine_call(dest, x1, rf, mod, y)
    return out.reshape(BATCH, SEQ, D_MODEL)
```

```python
import math

import jax
import jax.numpy as jnp
from jax import lax
from jax.experimental import pallas as pl
from jax.experimental.pallas import tpu as pltpu

F32 = jnp.float32
BF16 = jnp.bfloat16
I32 = jnp.int32

D_MODEL = 1024
BATCH = 4
SEQ = 4096
TOKENS = BATCH * SEQ
N_MOD = 6
NORM_EPS = 1e-6

GLA_HEADS = 4
GLA_DK = 64
GLA_DV = 128
GLA_GATE_RANK = 16
GLA_GATE_TAU = 16.0
GLA_CHUNK = 64
GLA_QK_W = GLA_HEADS * GLA_DK
GLA_V_W = GLA_HEADS * GLA_DV

DIFF_HEADS = 4
DIFF_DH = 64
DIFF_DV = 2 * DIFF_DH
DIFF_QK_W = DIFF_HEADS * 2 * DIFF_DH
DIFF_V_W = DIFF_HEADS * DIFF_DV
ROPE_THETA = 500000.0
ROPE_DIM = DIFF_DH // 4
ROPE_HALF = ROPE_DIM // 2
NEG_INF = -1e30
LAMBDA_INIT = 0.8 - 0.6 * 1.0

N_GROUPS = 4
EXPERTS_PER_GROUP = 8
N_EXPERTS = N_GROUPS * EXPERTS_PER_GROUP
TOP_K = 2
D_EXPERT = 512

LANES = 128
SUBLANES = 8
LR_PAD = LANES
D_IN_PAD = 2 * GLA_QK_W + 2 * GLA_V_W + 2 * DIFF_QK_W + DIFF_V_W + 2 * D_MODEL + LR_PAD

TM_IN = 512
TQ = 512
ATT_SUB = 32
Q_SCALE = DIFF_DH ** -0.5 * math.log2(math.e)
N_KV = SEQ // TQ
TT_GLA = 512
TM_MERGE = 512
TM_RANK = 512
TM_ROW = 256
MOE_BLK = 256
N_BLOCKS = (TOKENS * TOP_K + N_EXPERTS * (MOE_BLK - 1) + MOE_BLK - 1) // MOE_BLK
ROUTE_W = 8

VMEM_LIMIT = 56 * 1024 * 1024


def _cparams(sem):
    return pltpu.CompilerParams(dimension_semantics=sem, vmem_limit_bytes=VMEM_LIMIT)


def _sigmoid(x):
    return 1.0 / (1.0 + jnp.exp(-x))


def _mod_kernel(ct_ref, w_ref, b_ref, o_ref):
    ct = ct_ref[...]
    ca = ct * _sigmoid(ct)
    w = w_ref[...]
    rows = []
    for b in range(BATCH):
        rows.append(jnp.sum(w * ca[:, b:b + 1], axis=0, keepdims=True) + b_ref[...])
    rows.append(jnp.zeros((8 - BATCH, w.shape[1]), F32))
    o_ref[...] = jnp.concatenate(rows, axis=0)


def _mod_call(c, ada_w, ada_b):
    tn = D_MODEL
    ct = jnp.zeros((D_MODEL, 8), F32).at[:, :BATCH].set(c.T)
    out = pl.pallas_call(
        _mod_kernel,
        grid=(N_MOD,),
        in_specs=[
            pl.BlockSpec((D_MODEL, 8), lambda j: (0, 0)),
            pl.BlockSpec((D_MODEL, tn), lambda j: (0, j)),
            pl.BlockSpec((1, tn), lambda j: (0, j)),
        ],
        out_specs=pl.BlockSpec((8, tn), lambda j: (0, j)),
        out_shape=jax.ShapeDtypeStruct((8, N_MOD * D_MODEL), F32),
        compiler_params=_cparams(("arbitrary",)),
        name="mod",
    )(ct, ada_w, ada_b.reshape(1, N_MOD * D_MODEL))
    return out[:BATCH].reshape(BATCH, N_MOD, D_MODEL)


def _in_kernel(x_ref, mod_ref, g1_ref, w_ref, bd_ref, qg_ref, kg_ref, pos_ref, invf_ref,
               gqk_ref, gv_ref, gr_ref, lr_ref, qT_ref, k_ref, vT_ref, sa_ref, sb_ref):
    x = x_ref[...]
    shift1 = mod_ref[0, 0:1, :]
    scale1 = mod_ref[0, 1:2, :]
    ms = jnp.mean(x * x, axis=-1, keepdims=True)
    h = (x * lax.rsqrt(ms + NORM_EPS) * g1_ref[...]) * (1.0 + scale1) + shift1
    hb = h.astype(BF16)

    def proj(c0, c1):
        return jnp.dot(hb, w_ref[:, c0:c1], preferred_element_type=F32)

    gqk_ref[...] = proj(0, 512).astype(BF16)
    gv_ref[...] = proj(512, 1024).astype(BF16)
    gr_ref[...] = proj(1024, 1536).astype(BF16)
    sa_ref[...] = _sigmoid(proj(3072, 4096)).astype(BF16)
    sb_ref[...] = _sigmoid(proj(4096, 5120)).astype(BF16)
    lr_ref[...] = proj(5120, 5248)

    tm = x.shape[0]
    ang = pos_ref[...].astype(F32) * invf_ref[...]
    cos = jnp.cos(ang)
    sin = jnp.sin(ang)
    cos4 = jnp.concatenate([cos] * DIFF_HEADS, axis=1)
    sin4 = jnp.concatenate([sin] * DIFF_HEADS, axis=1)
    lane = lax.broadcasted_iota(I32, (tm, DIFF_QK_W), 1)
    first_half = (lane % DIFF_DH) < ROPE_HALF
    bd = bd_ref[...]

    def norm_rope(t, gain_row):
        t2 = t * t
        hi = t2.astype(BF16)
        lo = (t2 - hi.astype(F32)).astype(BF16)
        gms = (jnp.dot(hi, bd, preferred_element_type=F32)
               + jnp.dot(lo, bd, preferred_element_type=F32))
        t = t * lax.rsqrt(gms + NORM_EPS) * gain_row
        nxt = pltpu.roll(t, DIFF_QK_W - ROPE_HALF, 1)
        prv = pltpu.roll(t, ROPE_HALF, 1)
        return t * cos4 + jnp.where(first_half, -nxt, prv) * sin4

    dq = norm_rope(proj(1536, 2048), qg_ref[...]) * Q_SCALE
    dk = norm_rope(proj(2048, 2560), kg_ref[...])
    dv = proj(2560, 3072)
    for hd in range(DIFF_HEADS):
        sl = slice(hd * LANES, (hd + 1) * LANES)
        qT_ref[0, hd] = dq[:, sl].T.astype(BF16)
        k_ref[0, hd, 0] = dk[:, sl].astype(BF16)
        vT_ref[0, hd, 0] = dv[:, sl].T.astype(BF16)


def _in_call(x2, mod, norm1_g, w_in_p, bd, qg_row, kg_row, pos_col, invf_row):
    nb = SEQ // TM_IN
    tok_spec = lambda w: pl.BlockSpec((TM_IN, w), lambda i: (i, 0))
    const2 = lambda r, c: pl.BlockSpec((r, c), lambda i: (0, 0))
    out_shapes = (
        jax.ShapeDtypeStruct((TOKENS, 2 * GLA_QK_W), BF16),
        jax.ShapeDtypeStruct((TOKENS, GLA_V_W), BF16),
        jax.ShapeDtypeStruct((TOKENS, GLA_V_W), BF16),
        jax.ShapeDtypeStruct((TOKENS, LR_PAD), F32),
        jax.ShapeDtypeStruct((BATCH, DIFF_HEADS, LANES, SEQ), BF16),
        jax.ShapeDtypeStruct((BATCH, DIFF_HEADS, N_KV, TQ, LANES), BF16),
        jax.ShapeDtypeStruct((BATCH, DIFF_HEADS, N_KV, LANES, TQ), BF16),
        jax.ShapeDtypeStruct((TOKENS, D_MODEL), BF16),
        jax.ShapeDtypeStruct((TOKENS, D_MODEL), BF16),
    )
    out_specs = (
        tok_spec(2 * GLA_QK_W), tok_spec(GLA_V_W), tok_spec(GLA_V_W), tok_spec(LR_PAD),
        pl.BlockSpec((1, DIFF_HEADS, LANES, TM_IN), lambda i: (i // nb, 0, 0, i % nb)),
        pl.BlockSpec((1, DIFF_HEADS, 1, TM_IN, LANES), lambda i: (i // nb, 0, i % nb, 0, 0)),
        pl.BlockSpec((1, DIFF_HEADS, 1, LANES, TM_IN), lambda i: (i // nb, 0, i % nb, 0, 0)),
        tok_spec(D_MODEL), tok_spec(D_MODEL),
    )
    return pl.pallas_call(
        _in_kernel,
        grid=(TOKENS // TM_IN,),
        in_specs=[
            tok_spec(D_MODEL),
            pl.BlockSpec((1, N_MOD, D_MODEL), lambda i: (i // nb, 0, 0)),
            const2(1, D_MODEL),
            pl.BlockSpec((D_MODEL, D_IN_PAD), lambda i: (0, 0), pipeline_mode=pl.Buffered(1)),
            const2(DIFF_QK_W, DIFF_QK_W),
            const2(1, DIFF_QK_W), const2(1, DIFF_QK_W),
            tok_spec(1),
            const2(1, LANES),
        ],
        out_specs=out_specs,
        out_shape=out_shapes,
        compiler_params=_cparams(("parallel",)),
        name="in_proj",
    )(x2, mod, norm1_g, w_in_p, bd, qg_row, kg_row, pos_col, invf_row)


def _gla_kernel(qk_ref, v_ref, r_ref, lr_ref, au_ref, ab_ref, og_ref, o_ref, state_ref, oacc_ref):
    tt = qk_ref.shape[0]
    n_chunks = tt // GLA_CHUNK

    @pl.when(pl.program_id(1) == 0)
    def _():
        state_ref[...] = jnp.zeros_like(state_ref)

    z = jnp.dot(lr_ref[...], au_ref[...], preferred_element_type=F32,
                precision=lax.Precision.HIGHEST) + ab_ref[...]
    g = (jnp.minimum(z, 0.0) - jnp.log1p(jnp.exp(-jnp.abs(z)))) * (1.0 / GLA_GATE_TAU)

    row = lax.broadcasted_iota(I32, (tt, GLA_QK_W), 0) % GLA_CHUNK
    b = g
    step = 1
    while step < GLA_CHUNK:
        b = b + jnp.where(row >= step, pltpu.roll(b, step, 0), 0.0)
        step *= 2

    b_last_rows = [b[c * GLA_CHUNK + GLA_CHUNK - 1:(c + 1) * GLA_CHUNK, :] for c in range(n_chunks)]
    b_last = jnp.concatenate(
        [jnp.broadcast_to(bl, (GLA_CHUNK, GLA_QK_W)) for bl in b_last_rows], axis=0)

    qk = qk_ref[...].astype(F32)
    q = qk[:, :GLA_QK_W] * (GLA_DK ** -0.5)
    k = qk[:, GLA_QK_W:]
    q_in = (q * jnp.exp(b)).astype(BF16)
    k_in = (k * jnp.exp(-b)).astype(BF16)
    k_dec = (k * jnp.exp(b_last - b)).astype(BF16)

    ci = lax.broadcasted_iota(I32, (GLA_CHUNK, GLA_CHUNK), 0)
    cj = lax.broadcasted_iota(I32, (GLA_CHUNK, GLA_CHUNK), 1)
    causal = ci >= cj
    eye = ci == cj

    for c in range(n_chunks):
        rs = slice(c * GLA_CHUNK, (c + 1) * GLA_CHUNK)
        decay_row = jnp.exp(b_last_rows[c])
        for hd in range(GLA_HEADS):
            ks = slice(hd * GLA_DK, (hd + 1) * GLA_DK)
            vs = slice(hd * GLA_DV, (hd + 1) * GLA_DV)
            qc = q_in[rs, ks]
            kc = k_in[rs, ks]
            kd = k_dec[rs, ks]
            vc = v_ref[rs, vs]
            att = lax.dot_general(qc, kc, (((1,), (1,)), ((), ())), preferred_element_type=F32)
            att = jnp.where(causal, att, 0.0)
            s_prev = state_ref[hd]
            o = (jnp.dot(att.astype(BF16), vc, preferred_element_type=F32)
                 + jnp.dot(qc, s_prev.astype(BF16), preferred_element_type=F32))
            oacc_ref[rs, vs] = o
            dcol = jnp.sum(jnp.where(eye, decay_row[:, ks], 0.0), axis=1, keepdims=True)
            s_new = lax.dot_general(kd, vc, (((0,), (0,)), ((), ())), preferred_element_type=F32)
            state_ref[hd] = s_prev * dcol + s_new

    for hd in range(GLA_HEADS):
        vs = slice(hd * GLA_DV, (hd + 1) * GLA_DV)
        oh = oacc_ref[:, vs]
        ms = jnp.mean(oh * oh, axis=-1, keepdims=True)
        y = oh * lax.rsqrt(ms + NORM_EPS) * og_ref[...]
        r = r_ref[:, vs].astype(F32)
        o_ref[:, vs] = (y * (r * _sigmoid(r))).astype(BF16)


def _gla_call(gqk, gv, gr, lr, au_pad, ab_row, og_row):
    nt = SEQ // TT_GLA
    tok = lambda w: pl.BlockSpec((TT_GLA, w), lambda b, t: (b * nt + t, 0))
    const2 = lambda r, c: pl.BlockSpec((r, c), lambda b, t: (0, 0))
    return pl.pallas_call(
        _gla_kernel,
        grid=(BATCH, nt),
        in_specs=[tok(2 * GLA_QK_W), tok(GLA_V_W), tok(GLA_V_W), tok(LR_PAD),
                  const2(LR_PAD, GLA_QK_W), const2(1, GLA_QK_W), const2(1, GLA_DV)],
        out_specs=tok(GLA_V_W),
        out_shape=jax.ShapeDtypeStruct((TOKENS, GLA_V_W), BF16),
        scratch_shapes=[pltpu.VMEM((GLA_HEADS, GLA_DK, GLA_DV), F32),
                        pltpu.VMEM((TT_GLA, GLA_V_W), F32)],
        compiler_params=_cparams(("parallel", "arbitrary")),
        name="gla",
    )(gqk, gv, gr, lr, au_pad, ab_row, og_row)


def _attn_kernel(qT_ref, k_ref, vT_ref, lam_ref, og_ref, o_ref,
                 q_scr, s_scr, p_scr, acc_scr, m_scr, l_scr):
    i = pl.program_id(2)
    qT = qT_ref[0, 0]
    rowq = lax.broadcasted_iota(I32, qT.shape, 0)
    zero = jnp.zeros_like(qT)
    q_scr[0] = jnp.where(rowq < DIFF_DH, qT, zero)
    q_scr[1] = jnp.where(rowq >= DIFF_DH, qT, zero)
    m_scr[...] = jnp.full(m_scr.shape, NEG_INF, F32)
    l_scr[...] = jnp.zeros(l_scr.shape, F32)
    acc_scr[...] = jnp.zeros(acc_scr.shape, F32)
    n_sub = TQ // ATT_SUB

    def fold8(t, op):
        return op(t.reshape(t.shape[0] // SUBLANES, SUBLANES, TQ), axis=0)

    def scores(c, j):
        s_scr[c] = jnp.dot(k_ref[0, 0, j], q_scr[c], preferred_element_type=F32)

    def load_s(c, r, masked):
        s = s_scr[c, r * ATT_SUB:(r + 1) * ATT_SUB, :]
        if masked:
            key_i = lax.broadcasted_iota(I32, (ATT_SUB, TQ), 0) + r * ATT_SUB
            qry_i = lax.broadcasted_iota(I32, (ATT_SUB, TQ), 1)
            s = jnp.where(key_i <= qry_i, s, NEG_INF)
        return s

    def softmax_pv(c, j, masked):
        m8 = fold8(load_s(c, 0, masked), jnp.max)
        for r in range(1, n_sub):
            m8 = jnp.maximum(m8, fold8(load_s(c, r, masked), jnp.max))
        m_old = m_scr[c]
        m_new = jnp.maximum(m_old, jnp.max(m8, axis=0, keepdims=True))
        alpha = jnp.exp2(m_old - m_new)
        l8 = l_scr[c] * alpha
        for r in range(n_sub):
            p = jnp.exp2(load_s(c, r, masked) - m_new)
            l8 = l8 + fold8(p, jnp.sum)
            p_scr[c, r * ATT_SUB:(r + 1) * ATT_SUB, :] = p.astype(BF16)
        m_scr[c] = m_new
        l_scr[c] = l8
        acc_scr[c] = acc_scr[c] * alpha + jnp.dot(vT_ref[0, 0, j], p_scr[c],
                                                  preferred_element_type=F32)

    scores(0, 0)

    def body(j, carry):
        scores(1, j)
        softmax_pv(0, j, False)
        scores(0, j + 1)
        softmax_pv(1, j, False)
        return carry

    lax.fori_loop(0, i, body, 0)
    scores(1, i)
    softmax_pv(0, i, True)
    softmax_pv(1, i, True)

    l1 = jnp.sum(l_scr[0], axis=0, keepdims=True)
    l2 = jnp.sum(l_scr[1], axis=0, keepdims=True)

    lam_p = lam_ref[...]
    lam = (jnp.exp(jnp.sum(lam_p[0:1] * lam_p[1:2], axis=1, keepdims=True))
           - jnp.exp(jnp.sum(lam_p[2:3] * lam_p[3:4], axis=1, keepdims=True)) + LAMBDA_INIT)
    oT = acc_scr[0] / l1 - lam * (acc_scr[1] / l2)
    ms = jnp.mean(oT * oT, axis=0, keepdims=True)
    y = oT * lax.rsqrt(ms + NORM_EPS) * og_ref[...] * (1.0 - LAMBDA_INIT)
    o_ref[...] = y.T.astype(BF16)


def _attn_call(qT, kk, vT, lam_p, og_col):
    nq = SEQ // TQ
    return pl.pallas_call(
        _attn_kernel,
        grid=(BATCH, DIFF_HEADS, nq),
        in_specs=[
            pl.BlockSpec((1, 1, LANES, TQ), lambda b, h, i: (b, h, 0, i)),
            pl.BlockSpec((1, 1, N_KV, TQ, LANES), lambda b, h, i: (b, h, 0, 0, 0)),
            pl.BlockSpec((1, 1, N_KV, LANES, TQ), lambda b, h, i: (b, h, 0, 0, 0)),
            pl.BlockSpec((4, DIFF_DH), lambda b, h, i: (0, 0)),
            pl.BlockSpec((DIFF_DV, 1), lambda b, h, i: (0, 0)),
        ],
        out_specs=pl.BlockSpec((TQ, DIFF_DV), lambda b, h, i: (b * nq + i, h)),
        out_shape=jax.ShapeDtypeStruct((TOKENS, DIFF_V_W), BF16),
        scratch_shapes=[pltpu.VMEM((2, LANES, TQ), BF16),
                        pltpu.VMEM((2, TQ, TQ), F32),
                        pltpu.VMEM((2, TQ, TQ), BF16),
                        pltpu.VMEM((2, DIFF_DV, TQ), F32),
                        pltpu.VMEM((2, 1, TQ), F32),
                        pltpu.VMEM((2, SUBLANES, TQ), F32)],
        compiler_params=_cparams(("parallel", "parallel", "arbitrary")),
        name="attn",
    )(qT, kk, vT, lam_p, og_col)


def _merge_kernel(x_ref, oa_ref, ob_ref, sa_ref, sb_ref, mod_ref, g2_ref, wa_ref, wb_ref, wo_ref,
                  wrh_ref, wrl_ref, br_ref, x1_ref, h2_ref, rf_ref, ri_ref):
    tm = x_ref.shape[0]
    ma = jnp.dot(oa_ref[...], wa_ref[...], preferred_element_type=F32)
    mb = jnp.dot(ob_ref[...], wb_ref[...], preferred_element_type=F32)
    merged = sa_ref[...].astype(F32) * ma + sb_ref[...].astype(F32) * mb
    y = jnp.dot(merged.astype(BF16), wo_ref[...], preferred_element_type=F32)
    gate1 = mod_ref[0, 2:3, :]
    shift2 = mod_ref[0, 3:4, :]
    scale2 = mod_ref[0, 4:5, :]
    x1 = x_ref[...] + gate1 * y
    x1_ref[...] = x1
    ms = jnp.mean(x1 * x1, axis=-1, keepdims=True)
    h2 = (x1 * lax.rsqrt(ms + NORM_EPS) * g2_ref[...]) * (1.0 + scale2) + shift2
    h2_ref[...] = h2

    h2_hi = h2.astype(BF16)
    h2_lo = (h2 - h2_hi.astype(F32)).astype(BF16)
    logits = (jnp.dot(h2_hi, wrh_ref[...], preferred_element_type=F32)
              + jnp.dot(h2_lo, wrh_ref[...], preferred_element_type=F32)
              + jnp.dot(h2_hi, wrl_ref[...], preferred_element_type=F32)) + br_ref[...]
    lane = lax.broadcasted_iota(I32, (tm, LANES), 1).astype(F32)
    ninf = -jnp.inf
    big = float(LANES)

    def first_argmax(v):
        vmax = jnp.max(v, axis=1, keepdims=True)
        idx = jnp.min(jnp.where(v == vmax, lane, big), axis=1, keepdims=True)
        return vmax, idx

    gl = jnp.where(lane < N_GROUPS, logits, ninf)
    gmax, gidx = first_argmax(gl)
    p_top = 1.0 / jnp.sum(jnp.exp(gl - gmax), axis=1, keepdims=True)
    lo = N_GROUPS + EXPERTS_PER_GROUP * gidx
    el = jnp.where((lane >= lo) & (lane < lo + EXPERTS_PER_GROUP), logits, ninf)
    e1max, e1 = first_argmax(el)
    e2max, e2 = first_argmax(jnp.where(lane == e1, ninf, el))
    t = jnp.exp(e2max - e1max)
    w1 = 1.0 / (1.0 + t)
    w2 = t / (1.0 + t)
    col = lax.broadcasted_iota(I32, (tm, ROUTE_W), 1)
    rf_ref[...] = jnp.where(col == 0, p_top * w1, jnp.where(col == 1, p_top * w2, 0.0))
    ids = jnp.where(col == 0, e1 - N_GROUPS, jnp.where(col == 1, e2 - N_GROUPS, 0.0))
    ri_ref[...] = ids.astype(I32)


def _merge_call(x2, o_a, o_b, sa, sb, mod, norm2_g, wa, wb, wo, wr_hi, wr_lo, br):
    nb = SEQ // TM_MERGE
    tok = lambda w: pl.BlockSpec((TM_MERGE, w), lambda i: (i, 0))
    const2 = lambda r, c: pl.BlockSpec((r, c), lambda i: (0, 0))
    return pl.pallas_call(
        _merge_kernel,
        grid=(TOKENS // TM_MERGE,),
        in_specs=[tok(D_MODEL), tok(GLA_V_W), tok(DIFF_V_W), tok(D_MODEL), tok(D_MODEL),
                  pl.BlockSpec((1, N_MOD, D_MODEL), lambda i: (i // nb, 0, 0)),
                  const2(1, D_MODEL),
                  const2(GLA_V_W, D_MODEL), const2(DIFF_V_W, D_MODEL), const2(D_MODEL, D_MODEL),
                  const2(D_MODEL, LANES), const2(D_MODEL, LANES), const2(1, LANES)],
        out_specs=(tok(D_MODEL), tok(D_MODEL), tok(ROUTE_W), tok(ROUTE_W)),
        out_shape=(jax.ShapeDtypeStruct((TOKENS, D_MODEL), F32),
                   jax.ShapeDtypeStruct((TOKENS, D_MODEL), F32),
                   jax.ShapeDtypeStruct((TOKENS, ROUTE_W), F32),
                   jax.ShapeDtypeStruct((TOKENS, ROUTE_W), I32)),
        compiler_params=_cparams(("parallel",)),
        name="merge",
    )(x2, o_a, o_b, sa, sb, mod, norm2_g, wa, wb, wo, wr_hi, wr_lo, br)


def _rank_kernel(ri_ref, rank_ref, cnt_ref, carry_ref):
    tm = ri_ref.shape[0]

    @pl.when(pl.program_id(0) == 0)
    def _():
        carry_ref[...] = jnp.zeros_like(carry_ref)

    ri = ri_ref[...]
    e1 = ri[:, 0:1]
    e2 = ri[:, 1:2]
    lane = lax.broadcasted_iota(I32, (tm, LANES), 1)
    hit1 = lane == e1
    hit2 = lane == e2
    onehot = jnp.where(hit1 | hit2, 1.0, 0.0)
    ti = lax.broadcasted_iota(I32, (tm, tm), 0)
    tj = lax.broadcasted_iota(I32, (tm, tm), 1)
    strict_lower = jnp.where(ti > tj, 1.0, 0.0).astype(BF16)
    before = jnp.dot(strict_lower, onehot.astype(BF16), preferred_element_type=F32) + carry_ref[...]
    r1 = jnp.sum(jnp.where(hit1, before, 0.0), axis=1, keepdims=True)
    r2 = jnp.sum(jnp.where(hit2, before, 0.0), axis=1, keepdims=True)
    carry_ref[...] = carry_ref[...] + jnp.sum(onehot, axis=0, keepdims=True)
    cnt_ref[...] = carry_ref[...]

    eye = ti == tj

    def to_row(colv):
        return jnp.sum(jnp.where(eye, colv, 0.0), axis=0, keepdims=True)

    rows = [to_row(e1.astype(F32)), to_row(e2.astype(F32)), to_row(r1), to_row(r2),
            jnp.zeros((ROUTE_W - 4, tm), F32)]
    rank_ref[0] = jnp.concatenate(rows, axis=0).astype(I32)


def _rank_call(ri):
    return pl.pallas_call(
        _rank_kernel,
        grid=(TOKENS // TM_RANK,),
        in_specs=[pl.BlockSpec((TM_RANK, ROUTE_W), lambda i: (i, 0))],
        out_specs=(pl.BlockSpec((1, ROUTE_W, TM_RANK), lambda i: (i, 0, 0)),
                   pl.BlockSpec((1, LANES), lambda i: (0, 0))),
        out_shape=(jax.ShapeDtypeStruct((TOKENS // TM_RANK, ROUTE_W, TM_RANK), I32),
                   jax.ShapeDtypeStruct((1, LANES), F32)),
        scratch_shapes=[pltpu.VMEM((1, LANES), F32)],
        compiler_params=_cparams(("arbitrary",)),
        name="rank",
    )(ri)


def _row_copy(src_ref, src_row, dst_ref, dst_row, sem):
    return pltpu.make_async_copy(src_ref.at[pl.ds(src_row, 1)], dst_ref.at[pl.ds(dst_row, 1)], sem)


def _tail_block_copy(zero_ref, xs_ref, seg_end, sem):
    start = pl.multiple_of(seg_end - MOE_BLK, MOE_BLK)
    return pltpu.make_async_copy(zero_ref, xs_ref.at[pl.ds(start, MOE_BLK)], sem)


def _dispatch_kernel(d1_ref, d2_ref, pe_ref, h2_ref, xs_ref, zero_scr, sem, zsem):
    tm = h2_ref.shape[0]
    step = pl.program_id(0)
    base = step * tm

    @pl.when(step == 0)
    def _():
        zero_scr[...] = jnp.zeros_like(zero_scr)

        def seg_nonempty(e):
            return pe_ref[e] > jnp.where(e == 0, 0, pe_ref[jnp.maximum(e - 1, 0)])

        def zissue(e, carry):
            @pl.when(seg_nonempty(e))
            def _():
                _tail_block_copy(zero_scr, xs_ref, pe_ref[e], zsem).start()
            return carry

        def zdrain(e, carry):
            @pl.when(seg_nonempty(e))
            def _():
                _tail_block_copy(zero_scr, xs_ref, pe_ref[e], zsem).wait()
            return carry

        def block_copy(b):
            start = pl.multiple_of(b * MOE_BLK, MOE_BLK)
            return pltpu.make_async_copy(zero_scr, xs_ref.at[pl.ds(start, MOE_BLK)], zsem)

        def tissue(b, carry):
            block_copy(b).start()
            return carry

        def tdrain(b, carry):
            block_copy(b).wait()
            return carry

        n_used = pe_ref[N_EXPERTS - 1] // MOE_BLK
        lax.fori_loop(0, N_EXPERTS, zissue, 0)
        lax.fori_loop(n_used, N_BLOCKS, tissue, 0)
        lax.fori_loop(0, N_EXPERTS, zdrain, 0)
        lax.fori_loop(n_used, N_BLOCKS, tdrain, 0)

    def issue(r, carry):
        _row_copy(h2_ref, r, xs_ref, d1_ref[base + r], sem).start()
        _row_copy(h2_ref, r, xs_ref, d2_ref[base + r], sem).start()
        return carry

    lax.fori_loop(0, tm, issue, 0, unroll=8)
    for _ in range(TOP_K):
        pltpu.make_async_copy(h2_ref, xs_ref.at[pl.ds(0, tm)], sem).wait()


def _dispatch_call(dest1, dest2, pad_ends, h2):
    grid_spec = pltpu.PrefetchScalarGridSpec(
        num_scalar_prefetch=3,
        grid=(TOKENS // TM_ROW,),
        in_specs=[pl.BlockSpec((TM_ROW, D_MODEL), lambda i, d1, d2, pe: (i, 0))],
        out_specs=pl.BlockSpec(memory_space=pl.ANY),
        scratch_shapes=[pltpu.VMEM((MOE_BLK, D_MODEL), F32),
                        pltpu.SemaphoreType.DMA(()),
                        pltpu.SemaphoreType.DMA(())],
    )
    return pl.pallas_call(
        _dispatch_kernel,
        grid_spec=grid_spec,
        out_shape=jax.ShapeDtypeStruct((N_BLOCKS * MOE_BLK, D_MODEL), F32),
        compiler_params=_cparams(("arbitrary",)),
        name="dispatch",
    )(dest1, dest2, pad_ends, h2)


def _expert_kernel(be_ref, nu_ref, xs_ref, w1_ref, w3_ref, w2_ref, y_ref, w1b, w3b, w2b):
    i = pl.program_id(0)
    e = be_ref[i]
    prev = be_ref[jnp.maximum(i - 1, 0)]
    used = i < nu_ref[0]

    @pl.when(used & ((i == 0) | (e != prev)))
    def _():
        w1b[...] = w1_ref[0].astype(BF16)
        w3b[...] = w3_ref[0].astype(BF16)
        w2b[...] = w2_ref[0].astype(BF16)

    @pl.when(used)
    def _():
        xb = xs_ref[...].astype(BF16)
        a = jnp.dot(xb, w1b[...], preferred_element_type=F32)
        g = jnp.dot(xb, w3b[...], preferred_element_type=F32)
        hmid = (a * _sigmoid(a)) * g
        y_ref[...] = jnp.dot(hmid.astype(BF16), w2b[...], preferred_element_type=F32)

    @pl.when(jnp.logical_not(used))
    def _():
        y_ref[...] = jnp.zeros_like(y_ref)


def _expert_call(block_e, n_used, xs, w1, w3, w2):
    grid_spec = pltpu.PrefetchScalarGridSpec(
        num_scalar_prefetch=2,
        grid=(N_BLOCKS,),
        in_specs=[pl.BlockSpec((MOE_BLK, D_MODEL), lambda i, be, nu: (jnp.minimum(i, nu[0] - 1), 0)),
                  pl.BlockSpec((1, D_MODEL, D_EXPERT), lambda i, be, nu: (be[i], 0, 0)),
                  pl.BlockSpec((1, D_MODEL, D_EXPERT), lambda i, be, nu: (be[i], 0, 0)),
                  pl.BlockSpec((1, D_EXPERT, D_MODEL), lambda i, be, nu: (be[i], 0, 0))],
        out_specs=pl.BlockSpec((MOE_BLK, D_MODEL), lambda i, be, nu: (i, 0)),
        scratch_shapes=[pltpu.VMEM((D_MODEL, D_EXPERT), BF16),
                        pltpu.VMEM((D_MODEL, D_EXPERT), BF16),
                        pltpu.VMEM((D_EXPERT, D_MODEL), BF16)],
    )
    return pl.pallas_call(
        _expert_kernel,
        grid_spec=grid_spec,
        out_shape=jax.ShapeDtypeStruct((N_BLOCKS * MOE_BLK, D_MODEL), F32),
        compiler_params=_cparams(("arbitrary",)),
        name="experts",
    )(block_e, n_used, xs, w1, w3, w2)


def _combine_kernel(d1_ref, d2_ref, x1_ref, rf_ref, mod_ref, y_ref, o_ref, ybuf, sem):
    tm = x1_ref.shape[0]
    base = pl.program_id(0) * tm

    def issue(r, carry):
        _row_copy(y_ref, d1_ref[base + r], ybuf.at[0], r, sem).start()
        _row_copy(y_ref, d2_ref[base + r], ybuf.at[1], r, sem).start()
        return carry

    lax.fori_loop(0, tm, issue, 0, unroll=8)
    for kk in range(TOP_K):
        pltpu.make_async_copy(y_ref.at[pl.ds(0, tm)], ybuf.at[kk], sem).wait()

    rf = rf_ref[...]
    gate2 = mod_ref[0, 5:6, :]
    moe = rf[:, 0:1] * ybuf[0] + rf[:, 1:2] * ybuf[1]
    o_ref[...] = x1_ref[...] + gate2 * moe


def _combine_call(dest1, dest2, x1, rf, mod, y):
    nb = SEQ // TM_ROW
    grid_spec = pltpu.PrefetchScalarGridSpec(
        num_scalar_prefetch=2,
        grid=(TOKENS // TM_ROW,),
        in_specs=[pl.BlockSpec((TM_ROW, D_MODEL), lambda i, d1, d2: (i, 0)),
                  pl.BlockSpec((TM_ROW, ROUTE_W), lambda i, d1, d2: (i, 0)),
                  pl.BlockSpec((1, N_MOD, D_MODEL), lambda i, d1, d2: (i // nb, 0, 0)),
                  pl.BlockSpec(memory_space=pl.ANY)],
        out_specs=pl.BlockSpec((TM_ROW, D_MODEL), lambda i, d1, d2: (i, 0)),
        scratch_shapes=[pltpu.VMEM((TOP_K, TM_ROW, D_MODEL), F32),
                        pltpu.SemaphoreType.DMA(())],
    )
    return pl.pallas_call(
        _combine_kernel,
        grid_spec=grid_spec,
        out_shape=jax.ShapeDtypeStruct((TOKENS, D_MODEL), F32),
        compiler_params=_cparams(("arbitrary",)),
        name="combine",
    )(dest1, dest2, x1, rf, mod, y)


def kernel(x, c, positions, ada_w, ada_b, norm1_g, norm2_g, w_in, gla_alpha_up, gla_alpha_b,
           gla_out_g, diff_q_g, diff_k_g, diff_lq1, diff_lk1, diff_lq2, diff_lk2, diff_out_g,
           w_branch_a, w_branch_b, w_out, router_group_w, router_group_b, router_expert_w,
           router_expert_b, expert_w1, expert_w3, expert_w2):
    assert x.shape == (BATCH, SEQ, D_MODEL) and ada_w.shape[0] == 1
    x2 = x.reshape(TOKENS, D_MODEL)

    lr0 = 2 * GLA_QK_W + 2 * GLA_V_W
    w_in0 = w_in[0]
    w_in_p = jnp.concatenate(
        [w_in0[:, :lr0], w_in0[:, lr0 + GLA_GATE_RANK:], w_in0[:, lr0:lr0 + GLA_GATE_RANK],
         jnp.zeros((D_MODEL, LR_PAD - GLA_GATE_RANK), F32)], axis=1).astype(BF16)
    au_pad = jnp.zeros((LR_PAD, GLA_QK_W), F32).at[:GLA_GATE_RANK].set(gla_alpha_up[0])
    gid = jnp.arange(DIFF_QK_W) // DIFF_DH
    bd = jnp.where(gid[:, None] == gid[None, :], 1.0 / DIFF_DH, 0.0).astype(BF16)
    qg_row = jnp.tile(diff_q_g[0], DIFF_QK_W // DIFF_DH).reshape(1, DIFF_QK_W)
    kg_row = jnp.tile(diff_k_g[0], DIFF_QK_W // DIFF_DH).reshape(1, DIFF_QK_W)
    inv_freq = ROPE_THETA ** (-jnp.arange(ROPE_HALF, dtype=F32) / ROPE_HALF)
    invf64 = jnp.concatenate([inv_freq, inv_freq, jnp.zeros((DIFF_DH - ROPE_DIM,), F32)])
    invf_row = jnp.tile(invf64, LANES // DIFF_DH).reshape(1, LANES)
    pos_col = positions.reshape(TOKENS, 1)
    lam_p = jnp.concatenate([diff_lq1, diff_lk1, diff_lq2, diff_lk2], axis=0)
    wr = (jnp.zeros((D_MODEL, LANES), F32)
          .at[:, :N_GROUPS].set(router_group_w[0])
          .at[:, N_GROUPS:N_GROUPS + N_EXPERTS].set(router_expert_w[0]))
    wr_hi = wr.astype(BF16)
    wr_lo = (wr - wr_hi.astype(F32)).astype(BF16)
    br = (jnp.zeros((1, LANES), F32)
          .at[0, :N_GROUPS].set(router_group_b[0])
          .at[0, N_GROUPS:N_GROUPS + N_EXPERTS].set(router_expert_b[0]))

    mod = _mod_call(c, ada_w[0], ada_b[0])
    gqk, gv, gr, lr, qT, kk, vT, sa, sb = _in_call(
        x2, mod, norm1_g, w_in_p, bd, qg_row, kg_row, pos_col, invf_row)
    o_a = _gla_call(gqk, gv, gr, lr, au_pad, gla_alpha_b, gla_out_g)
    o_b = _attn_call(qT, kk, vT, lam_p, diff_out_g.reshape(DIFF_DV, 1))
    x1, h2, rf, ri = _merge_call(
        x2, o_a, o_b, sa, sb, mod, norm2_g, w_branch_a[0].astype(BF16),
        w_branch_b[0].astype(BF16), w_out[0].astype(BF16), wr_hi, wr_lo, br)
    rank, cnt = _rank_call(ri)

    counts = cnt[0, :N_EXPERTS].astype(I32)
    padded = ((counts + MOE_BLK - 1) // MOE_BLK) * MOE_BLK
    pad_ends = jnp.cumsum(padded)
    pad_starts = pad_ends - padded
    rows = rank.transpose(1, 0, 2).reshape(ROUTE_W, TOKENS)
    dest1 = pad_starts[rows[0]] + rows[2]
    dest2 = pad_starts[rows[1]] + rows[3]
    n_used = (pad_ends[-1:] // MOE_BLK).astype(I32)
    block_start = jnp.arange(N_BLOCKS, dtype=I32) * MOE_BLK
    block_e = jnp.minimum(
        jnp.sum((pad_ends[None, :] <= block_start[:, None]).astype(I32), axis=1), N_EXPERTS - 1)

    xs = _dispatch_call(dest1, dest2, pad_ends.astype(I32), h2)
    y = _expert_call(block_e, n_used, xs, expert_w1[0], expert_w3[0], expert_w2[0])
    out = _combine_call(dest1, dest2, x1, rf, mod, y)
    return out.reshape(BATCH, SEQ, D_MODEL)
```

```python
import math

import jax
import jax.numpy as jnp
from jax import lax
from jax.experimental import pallas as pl
from jax.experimental.pallas import tpu as pltpu

F32 = jnp.float32
BF16 = jnp.bfloat16
I32 = jnp.int32

D_MODEL = 1024
BATCH = 4
SEQ = 4096
TOKENS = BATCH * SEQ
N_MOD = 6
NORM_EPS = 1e-6

GLA_HEADS = 4
GLA_DK = 64
GLA_DV = 128
GLA_GATE_RANK = 16
GLA_GATE_TAU = 16.0
GLA_CHUNK = 64
GLA_QK_W = GLA_HEADS * GLA_DK
GLA_V_W = GLA_HEADS * GLA_DV

DIFF_HEADS = 4
DIFF_DH = 64
DIFF_DV = 2 * DIFF_DH
DIFF_QK_W = DIFF_HEADS * 2 * DIFF_DH
DIFF_V_W = DIFF_HEADS * DIFF_DV
ROPE_THETA = 500000.0
ROPE_DIM = DIFF_DH // 4
ROPE_HALF = ROPE_DIM // 2
NEG_INF = -1e30
LAMBDA_INIT = 0.8 - 0.6 * 1.0

N_GROUPS = 4
EXPERTS_PER_GROUP = 8
N_EXPERTS = N_GROUPS * EXPERTS_PER_GROUP
TOP_K = 2
D_EXPERT = 512

LANES = 128
SUBLANES = 8
LR_PAD = LANES
D_IN_PAD = 2 * GLA_QK_W + 2 * GLA_V_W + 2 * DIFF_QK_W + DIFF_V_W + 2 * D_MODEL + LR_PAD

TM_IN = 512
TQ = 512
ATT_SUB = 32
Q_SCALE = DIFF_DH ** -0.5 * math.log2(math.e)
N_KV = SEQ // TQ
TT_GLA = 512
TM_MERGE = 512
TM_RANK = 512
TM_ROW = 256
MOE_BLK = 256
N_BLOCKS = (TOKENS * TOP_K + N_EXPERTS * (MOE_BLK - 1) + MOE_BLK - 1) // MOE_BLK
ROUTE_W = 8

VMEM_LIMIT = 56 * 1024 * 1024


def _cparams(sem):
    return pltpu.CompilerParams(dimension_semantics=sem, vmem_limit_bytes=VMEM_LIMIT)


def _sigmoid(x):
    return 1.0 / (1.0 + jnp.exp(-x))


def _mod_kernel(ct_ref, w_ref, b_ref, o_ref):
    ct = ct_ref[...]
    ca = ct * _sigmoid(ct)
    w = w_ref[...]
    rows = []
    for b in range(BATCH):
        rows.append(jnp.sum(w * ca[:, b:b + 1], axis=0, keepdims=True) + b_ref[...])
    rows.append(jnp.zeros((8 - BATCH, w.shape[1]), F32))
    o_ref[...] = jnp.concatenate(rows, axis=0)


def _mod_call(c, ada_w, ada_b):
    tn = D_MODEL
    ct = jnp.zeros((D_MODEL, 8), F32).at[:, :BATCH].set(c.T)
    out = pl.pallas_call(
        _mod_kernel,
        grid=(N_MOD,),
        in_specs=[
            pl.BlockSpec((D_MODEL, 8), lambda j: (0, 0)),
            pl.BlockSpec((D_MODEL, tn), lambda j: (0, j)),
            pl.BlockSpec((1, tn), lambda j: (0, j)),
        ],
        out_specs=pl.BlockSpec((8, tn), lambda j: (0, j)),
        out_shape=jax.ShapeDtypeStruct((8, N_MOD * D_MODEL), F32),
        compiler_params=_cparams(("arbitrary",)),
        name="mod",
    )(ct, ada_w, ada_b.reshape(1, N_MOD * D_MODEL))
    return out[:BATCH].reshape(BATCH, N_MOD, D_MODEL)


def _in_kernel(x_ref, mod_ref, g1_ref, w_ref, bd_ref, qg_ref, kg_ref, pos_ref, invf_ref,
               gqk_ref, gv_ref, gr_ref, lr_ref, qT_ref, k_ref, vT_ref, sa_ref, sb_ref):
    x = x_ref[...]
    shift1 = mod_ref[0, 0:1, :]
    scale1 = mod_ref[0, 1:2, :]
    ms = jnp.mean(x * x, axis=-1, keepdims=True)
    h = (x * lax.rsqrt(ms + NORM_EPS) * g1_ref[...]) * (1.0 + scale1) + shift1
    hb = h.astype(BF16)

    def proj(c0, c1):
        return jnp.dot(hb, w_ref[:, c0:c1], preferred_element_type=F32)

    gqk_ref[...] = proj(0, 512).astype(BF16)
    gv_ref[...] = proj(512, 1024).astype(BF16)
    gr_ref[...] = proj(1024, 1536).astype(BF16)
    sa_ref[...] = _sigmoid(proj(3072, 4096)).astype(BF16)
    sb_ref[...] = _sigmoid(proj(4096, 5120)).astype(BF16)
    lr_ref[...] = proj(5120, 5248)

    tm = x.shape[0]
    ang = pos_ref[...].astype(F32) * invf_ref[...]
    cos = jnp.cos(ang)
    sin = jnp.sin(ang)
    cos4 = jnp.concatenate([cos] * DIFF_HEADS, axis=1)
    sin4 = jnp.concatenate([sin] * DIFF_HEADS, axis=1)
    lane = lax.broadcasted_iota(I32, (tm, DIFF_QK_W), 1)
    first_half = (lane % DIFF_DH) < ROPE_HALF
    bd = bd_ref[...]

    def norm_rope(t, gain_row):
        t2 = t * t
        hi = t2.astype(BF16)
        lo = (t2 - hi.astype(F32)).astype(BF16)
        gms = (jnp.dot(hi, bd, preferred_element_type=F32)
               + jnp.dot(lo, bd, preferred_element_type=F32))
        t = t * lax.rsqrt(gms + NORM_EPS) * gain_row
        nxt = pltpu.roll(t, DIFF_QK_W - ROPE_HALF, 1)
        prv = pltpu.roll(t, ROPE_HALF, 1)
        return t * cos4 + jnp.where(first_half, -nxt, prv) * sin4

    dq = norm_rope(proj(1536, 2048), qg_ref[...]) * Q_SCALE
    dk = norm_rope(proj(2048, 2560), kg_ref[...])
    dv = proj(2560, 3072)
    for hd in range(DIFF_HEADS):
        sl = slice(hd * LANES, (hd + 1) * LANES)
        qT_ref[0, hd] = dq[:, sl].T.astype(BF16)
        k_ref[0, hd, 0] = dk[:, sl].astype(BF16)
        vT_ref[0, hd, 0] = dv[:, sl].T.astype(BF16)


def _in_call(x2, mod, norm1_g, w_in_p, bd, qg_row, kg_row, pos_col, invf_row):
    nb = SEQ // TM_IN
    tok_spec = lambda w: pl.BlockSpec((TM_IN, w), lambda i: (i, 0))
    const2 = lambda r, c: pl.BlockSpec((r, c), lambda i: (0, 0))
    out_shapes = (
        jax.ShapeDtypeStruct((TOKENS, 2 * GLA_QK_W), BF16),
        jax.ShapeDtypeStruct((TOKENS, GLA_V_W), BF16),
        jax.ShapeDtypeStruct((TOKENS, GLA_V_W), BF16),
        jax.ShapeDtypeStruct((TOKENS, LR_PAD), F32),
        jax.ShapeDtypeStruct((BATCH, DIFF_HEADS, LANES, SEQ), BF16),
        jax.ShapeDtypeStruct((BATCH, DIFF_HEADS, N_KV, TQ, LANES), BF16),
        jax.ShapeDtypeStruct((BATCH, DIFF_HEADS, N_KV, LANES, TQ), BF16),
        jax.ShapeDtypeStruct((TOKENS, D_MODEL), BF16),
        jax.ShapeDtypeStruct((TOKENS, D_MODEL), BF16),
    )
    out_specs = (
        tok_spec(2 * GLA_QK_W), tok_spec(GLA_V_W), tok_spec(GLA_V_W), tok_spec(LR_PAD),
        pl.BlockSpec((1, DIFF_HEADS, LANES, TM_IN), lambda i: (i // nb, 0, 0, i % nb)),
        pl.BlockSpec((1, DIFF_HEADS, 1, TM_IN, LANES), lambda i: (i // nb, 0, i % nb, 0, 0)),
        pl.BlockSpec((1, DIFF_HEADS, 1, LANES, TM_IN), lambda i: (i // nb, 0, i % nb, 0, 0)),
        tok_spec(D_MODEL), tok_spec(D_MODEL),
    )
    return pl.pallas_call(
        _in_kernel,
        grid=(TOKENS // TM_IN,),
        in_specs=[
            tok_spec(D_MODEL),
            pl.BlockSpec((1, N_MOD, D_MODEL), lambda i: (i // nb, 0, 0)),
            const2(1, D_MODEL),
            pl.BlockSpec((D_MODEL, D_IN_PAD), lambda i: (0, 0), pipeline_mode=pl.Buffered(1)),
            const2(DIFF_QK_W, DIFF_QK_W),
            const2(1, DIFF_QK_W), const2(1, DIFF_QK_W),
            tok_spec(1),
            const2(1, LANES),
        ],
        out_specs=out_specs,
        out_shape=out_shapes,
        compiler_params=_cparams(("parallel",)),
        name="in_proj",
    )(x2, mod, norm1_g, w_in_p, bd, qg_row, kg_row, pos_col, invf_row)


def _gla_kernel(qk_ref, v_ref, r_ref, lr_ref, auh_ref, aul_ref, ab_ref, og_ref, o_ref,
                state_ref, oacc_ref, snew_ref):
    tt = qk_ref.shape[0]
    n_chunks = tt // GLA_CHUNK

    @pl.when(pl.program_id(1) == 0)
    def _():
        state_ref[...] = jnp.zeros_like(state_ref)

    lr = lr_ref[...]
    lr_hi = lr.astype(BF16)
    lr_lo = (lr - lr_hi.astype(F32)).astype(BF16)
    z = (jnp.dot(lr_hi, auh_ref[...], preferred_element_type=F32)
         + jnp.dot(lr_lo, auh_ref[...], preferred_element_type=F32)
         + jnp.dot(lr_hi, aul_ref[...], preferred_element_type=F32)) + ab_ref[...]
    g = (jnp.minimum(z, 0.0) - jnp.log(1.0 + jnp.exp(-jnp.abs(z)))) * (1.0 / GLA_GATE_TAU)

    row = lax.broadcasted_iota(I32, (tt, GLA_QK_W), 0) % GLA_CHUNK
    b = g
    step = 1
    while step < GLA_CHUNK:
        b = b + jnp.where(row >= step, pltpu.roll(b, step, 0), 0.0)
        step *= 2

    b_last_rows = [b[c * GLA_CHUNK + GLA_CHUNK - 1:(c + 1) * GLA_CHUNK, :] for c in range(n_chunks)]
    b_last = jnp.concatenate(
        [jnp.broadcast_to(bl, (GLA_CHUNK, GLA_QK_W)) for bl in b_last_rows], axis=0)

    qk = qk_ref[...].astype(F32)
    q = qk[:, :GLA_QK_W] * (GLA_DK ** -0.5)
    k = qk[:, GLA_QK_W:]
    q_in = (q * jnp.exp(b)).astype(BF16)
    k_in = (k * jnp.exp(-b)).astype(BF16)
    k_dec = (k * jnp.exp(b_last - b)).astype(BF16)

    ci = lax.broadcasted_iota(I32, (GLA_CHUNK, GLA_CHUNK), 0)
    cj = lax.broadcasted_iota(I32, (GLA_CHUNK, GLA_CHUNK), 1)
    causal = ci >= cj
    eye = ci == cj

    pairs = [(c, hd) for c in range(n_chunks) for hd in range(GLA_HEADS)]

    def rows(c):
        return slice(c * GLA_CHUNK, (c + 1) * GLA_CHUNK)

    def kcols(hd):
        return slice(hd * GLA_DK, (hd + 1) * GLA_DK)

    def vcols(hd):
        return slice(hd * GLA_DV, (hd + 1) * GLA_DV)

    att = {}
    for c, hd in pairs:
        a = lax.dot_general(q_in[rows(c), kcols(hd)], k_in[rows(c), kcols(hd)],
                            (((1,), (1,)), ((), ())), preferred_element_type=F32)
        att[c, hd] = jnp.where(causal, a, 0.0).astype(BF16)
    for c, hd in pairs:
        oacc_ref[rows(c), vcols(hd)] = jnp.dot(att[c, hd], v_ref[rows(c), vcols(hd)],
                                               preferred_element_type=F32)
    for c, hd in pairs:
        snew_ref[c * GLA_HEADS + hd] = lax.dot_general(
            k_dec[rows(c), kcols(hd)], v_ref[rows(c), vcols(hd)],
            (((0,), (0,)), ((), ())), preferred_element_type=F32)

    decay_rows = jnp.exp(jnp.concatenate(
        b_last_rows + [jnp.zeros((LANES - n_chunks, GLA_QK_W), F32)], axis=0))
    decay_cols = decay_rows.T
    states = [state_ref[hd] for hd in range(GLA_HEADS)]
    for c in range(n_chunks):
        for hd in range(GLA_HEADS):
            s_prev = states[hd]
            oacc_ref[rows(c), vcols(hd)] += jnp.dot(
                q_in[rows(c), kcols(hd)], s_prev.astype(BF16), preferred_element_type=F32)
            dcol = decay_cols[kcols(hd), c:c + 1]
            states[hd] = s_prev * dcol + snew_ref[c * GLA_HEADS + hd]
    for hd in range(GLA_HEADS):
        state_ref[hd] = states[hd]

    for hd in range(GLA_HEADS):
        vs = slice(hd * GLA_DV, (hd + 1) * GLA_DV)
        oh = oacc_ref[:, vs]
        ms = jnp.mean(oh * oh, axis=-1, keepdims=True)
        y = oh * lax.rsqrt(ms + NORM_EPS) * og_ref[...]
        r = r_ref[:, vs].astype(F32)
        o_ref[:, vs] = (y * (r * _sigmoid(r))).astype(BF16)


def _gla_call(gqk, gv, gr, lr, au_hi, au_lo, ab_row, og_row):
    nt = SEQ // TT_GLA
    tok = lambda w: pl.BlockSpec((TT_GLA, w), lambda b, t: (b * nt + t, 0))
    const2 = lambda r, c: pl.BlockSpec((r, c), lambda b, t: (0, 0))
    return pl.pallas_call(
        _gla_kernel,
        grid=(BATCH, nt),
        in_specs=[tok(2 * GLA_QK_W), tok(GLA_V_W), tok(GLA_V_W), tok(LR_PAD),
                  const2(LR_PAD, GLA_QK_W), const2(LR_PAD, GLA_QK_W), const2(1, GLA_QK_W),
                  const2(1, GLA_DV)],
        out_specs=tok(GLA_V_W),
        out_shape=jax.ShapeDtypeStruct((TOKENS, GLA_V_W), BF16),
        scratch_shapes=[pltpu.VMEM((GLA_HEADS, GLA_DK, GLA_DV), F32),
                        pltpu.VMEM((TT_GLA, GLA_V_W), F32),
                        pltpu.VMEM((TT_GLA // GLA_CHUNK * GLA_HEADS, GLA_DK, GLA_DV), F32)],
        compiler_params=_cparams(("parallel", "arbitrary")),
        name="gla",
    )(gqk, gv, gr, lr, au_hi, au_lo, ab_row, og_row)


def _attn_kernel(qT_ref, k_ref, vT_ref, lam_ref, og_ref, o_ref,
                 q_scr, s_scr, p_scr, acc_scr, m_scr, l_scr):
    i = pl.program_id(2)
    qT = qT_ref[0, 0]
    rowq = lax.broadcasted_iota(I32, qT.shape, 0)
    zero = jnp.zeros_like(qT)
    q_scr[0] = jnp.where(rowq < DIFF_DH, qT, zero)
    q_scr[1] = jnp.where(rowq >= DIFF_DH, qT, zero)
    m_scr[...] = jnp.full(m_scr.shape, NEG_INF, F32)
    l_scr[...] = jnp.zeros(l_scr.shape, F32)
    acc_scr[...] = jnp.zeros(acc_scr.shape, F32)
    n_sub = TQ // ATT_SUB

    def fold8(t, op):
        return op(t.reshape(t.shape[0] // SUBLANES, SUBLANES, TQ), axis=0)

    def scores(c, j):
        s_scr[c] = jnp.dot(k_ref[0, 0, j], q_scr[c], preferred_element_type=F32)

    def load_s(c, r, masked):
        s = s_scr[c, r * ATT_SUB:(r + 1) * ATT_SUB, :]
        if masked:
            key_i = lax.broadcasted_iota(I32, (ATT_SUB, TQ), 0) + r * ATT_SUB
            qry_i = lax.broadcasted_iota(I32, (ATT_SUB, TQ), 1)
            s = jnp.where(key_i <= qry_i, s, NEG_INF)
        return s

    def softmax_pv(c, j, masked):
        m8 = fold8(load_s(c, 0, masked), jnp.max)
        for r in range(1, n_sub):
            m8 = jnp.maximum(m8, fold8(load_s(c, r, masked), jnp.max))
        m_old = m_scr[c]
        m_new = jnp.maximum(m_old, jnp.max(m8, axis=0, keepdims=True))
        alpha = jnp.exp2(m_old - m_new)
        l8 = l_scr[c] * alpha
        for r in range(n_sub):
            p = jnp.exp2(load_s(c, r, masked) - m_new)
            l8 = l8 + fold8(p, jnp.sum)
            p_scr[c, r * ATT_SUB:(r + 1) * ATT_SUB, :] = p.astype(BF16)
        m_scr[c] = m_new
        l_scr[c] = l8
        acc_scr[c] = acc_scr[c] * alpha + jnp.dot(vT_ref[0, 0, j], p_scr[c],
                                                  preferred_element_type=F32)

    scores(0, 0)

    def body(j, carry):
        scores(1, j)
        softmax_pv(0, j, False)
        scores(0, j + 1)
        softmax_pv(1, j, False)
        return carry

    lax.fori_loop(0, i, body, 0)
    scores(1, i)
    softmax_pv(0, i, True)
    softmax_pv(1, i, True)

    l1 = jnp.sum(l_scr[0], axis=0, keepdims=True)
    l2 = jnp.sum(l_scr[1], axis=0, keepdims=True)

    lam_p = lam_ref[...]
    lam = (jnp.exp(jnp.sum(lam_p[0:1] * lam_p[1:2], axis=1, keepdims=True))
           - jnp.exp(jnp.sum(lam_p[2:3] * lam_p[3:4], axis=1, keepdims=True)) + LAMBDA_INIT)
    oT = acc_scr[0] / l1 - lam * (acc_scr[1] / l2)
    ms = jnp.mean(oT * oT, axis=0, keepdims=True)
    y = oT * lax.rsqrt(ms + NORM_EPS) * og_ref[...] * (1.0 - LAMBDA_INIT)
    o_ref[...] = y.T.astype(BF16)


def _attn_call(qT, kk, vT, lam_p, og_col):
    nq = SEQ // TQ
    return pl.pallas_call(
        _attn_kernel,
        grid=(BATCH, DIFF_HEADS, nq),
        in_specs=[
            pl.BlockSpec((1, 1, LANES, TQ), lambda b, h, i: (b, h, 0, i)),
            pl.BlockSpec((1, 1, N_KV, TQ, LANES), lambda b, h, i: (b, h, 0, 0, 0)),
            pl.BlockSpec((1, 1, N_KV, LANES, TQ), lambda b, h, i: (b, h, 0, 0, 0)),
            pl.BlockSpec((4, DIFF_DH), lambda b, h, i: (0, 0)),
            pl.BlockSpec((DIFF_DV, 1), lambda b, h, i: (0, 0)),
        ],
        out_specs=pl.BlockSpec((TQ, DIFF_DV), lambda b, h, i: (b * nq + i, h)),
        out_shape=jax.ShapeDtypeStruct((TOKENS, DIFF_V_W), BF16),
        scratch_shapes=[pltpu.VMEM((2, LANES, TQ), BF16),
                        pltpu.VMEM((2, TQ, TQ), F32),
                        pltpu.VMEM((2, TQ, TQ), BF16),
                        pltpu.VMEM((2, DIFF_DV, TQ), F32),
                        pltpu.VMEM((2, 1, TQ), F32),
                        pltpu.VMEM((2, SUBLANES, TQ), F32)],
        compiler_params=_cparams(("parallel", "parallel", "arbitrary")),
        name="attn",
    )(qT, kk, vT, lam_p, og_col)


def _merge_kernel(x_ref, oa_ref, ob_ref, sa_ref, sb_ref, mod_ref, g2_ref, wa_ref, wb_ref, wo_ref,
                  wrh_ref, wrl_ref, br_ref, x1_ref, h2_ref, rf_ref, ri_ref):
    tm = x_ref.shape[0]
    ma = jnp.dot(oa_ref[...], wa_ref[...], preferred_element_type=F32)
    mb = jnp.dot(ob_ref[...], wb_ref[...], preferred_element_type=F32)
    merged = sa_ref[...].astype(F32) * ma + sb_ref[...].astype(F32) * mb
    y = jnp.dot(merged.astype(BF16), wo_ref[...], preferred_element_type=F32)
    gate1 = mod_ref[0, 2:3, :]
    shift2 = mod_ref[0, 3:4, :]
    scale2 = mod_ref[0, 4:5, :]
    x1 = x_ref[...] + gate1 * y
    x1_ref[...] = x1
    ms = jnp.mean(x1 * x1, axis=-1, keepdims=True)
    h2 = (x1 * lax.rsqrt(ms + NORM_EPS) * g2_ref[...]) * (1.0 + scale2) + shift2
    h2_ref[...] = h2

    h2_hi = h2.astype(BF16)
    h2_lo = (h2 - h2_hi.astype(F32)).astype(BF16)
    logits = (jnp.dot(h2_hi, wrh_ref[...], preferred_element_type=F32)
              + jnp.dot(h2_lo, wrh_ref[...], preferred_element_type=F32)
              + jnp.dot(h2_hi, wrl_ref[...], preferred_element_type=F32)) + br_ref[...]
    lane = lax.broadcasted_iota(I32, (tm, LANES), 1).astype(F32)
    ninf = -jnp.inf
    big = float(LANES)

    def first_argmax(v):
        vmax = jnp.max(v, axis=1, keepdims=True)
        idx = jnp.min(jnp.where(v == vmax, lane, big), axis=1, keepdims=True)
        return vmax, idx

    gl = jnp.where(lane < N_GROUPS, logits, ninf)
    gmax, gidx = first_argmax(gl)
    p_top = 1.0 / jnp.sum(jnp.exp(gl - gmax), axis=1, keepdims=True)
    lo = N_GROUPS + EXPERTS_PER_GROUP * gidx
    el = jnp.where((lane >= lo) & (lane < lo + EXPERTS_PER_GROUP), logits, ninf)
    e1max, e1 = first_argmax(el)
    e2max, e2 = first_argmax(jnp.where(lane == e1, ninf, el))
    t = jnp.exp(e2max - e1max)
    w1 = 1.0 / (1.0 + t)
    w2 = t / (1.0 + t)
    col = lax.broadcasted_iota(I32, (tm, ROUTE_W), 1)
    rf_ref[...] = jnp.where(col == 0, p_top * w1, jnp.where(col == 1, p_top * w2, 0.0))
    ids = jnp.where(col == 0, e1 - N_GROUPS, jnp.where(col == 1, e2 - N_GROUPS, 0.0))
    ri_ref[...] = ids.astype(I32)


def _merge_call(x2, o_a, o_b, sa, sb, mod, norm2_g, wa, wb, wo, wr_hi, wr_lo, br):
    nb = SEQ // TM_MERGE
    tok = lambda w: pl.BlockSpec((TM_MERGE, w), lambda i: (i, 0))
    const2 = lambda r, c: pl.BlockSpec((r, c), lambda i: (0, 0))
    return pl.pallas_call(
        _merge_kernel,
        grid=(TOKENS // TM_MERGE,),
        in_specs=[tok(D_MODEL), tok(GLA_V_W), tok(DIFF_V_W), tok(D_MODEL), tok(D_MODEL),
                  pl.BlockSpec((1, N_MOD, D_MODEL), lambda i: (i // nb, 0, 0)),
                  const2(1, D_MODEL),
                  const2(GLA_V_W, D_MODEL), const2(DIFF_V_W, D_MODEL), const2(D_MODEL, D_MODEL),
                  const2(D_MODEL, LANES), const2(D_MODEL, LANES), const2(1, LANES)],
        out_specs=(tok(D_MODEL), tok(D_MODEL), tok(ROUTE_W), tok(ROUTE_W)),
        out_shape=(jax.ShapeDtypeStruct((TOKENS, D_MODEL), F32),
                   jax.ShapeDtypeStruct((TOKENS, D_MODEL), F32),
                   jax.ShapeDtypeStruct((TOKENS, ROUTE_W), F32),
                   jax.ShapeDtypeStruct((TOKENS, ROUTE_W), I32)),
        compiler_params=_cparams(("parallel",)),
        name="merge",
    )(x2, o_a, o_b, sa, sb, mod, norm2_g, wa, wb, wo, wr_hi, wr_lo, br)


def _rank_kernel(ri_ref, rank_ref, cnt_ref, carry_ref):
    tm = ri_ref.shape[0]

    @pl.when(pl.program_id(0) == 0)
    def _():
        carry_ref[...] = jnp.zeros_like(carry_ref)

    ri = ri_ref[...]
    e1 = ri[:, 0:1]
    e2 = ri[:, 1:2]
    lane = lax.broadcasted_iota(I32, (tm, LANES), 1)
    hit1 = lane == e1
    hit2 = lane == e2
    onehot = jnp.where(hit1 | hit2, 1.0, 0.0)
    ti = lax.broadcasted_iota(I32, (tm, tm), 0)
    tj = lax.broadcasted_iota(I32, (tm, tm), 1)
    strict_lower = jnp.where(ti > tj, 1.0, 0.0).astype(BF16)
    before = jnp.dot(strict_lower, onehot.astype(BF16), preferred_element_type=F32) + carry_ref[...]
    r1 = jnp.sum(jnp.where(hit1, before, 0.0), axis=1, keepdims=True)
    r2 = jnp.sum(jnp.where(hit2, before, 0.0), axis=1, keepdims=True)
    carry_ref[...] = carry_ref[...] + jnp.sum(onehot, axis=0, keepdims=True)
    cnt_ref[...] = carry_ref[...]

    eye = ti == tj

    def to_row(colv):
        return jnp.sum(jnp.where(eye, colv, 0.0), axis=0, keepdims=True)

    rows = [to_row(e1.astype(F32)), to_row(e2.astype(F32)), to_row(r1), to_row(r2),
            jnp.zeros((ROUTE_W - 4, tm), F32)]
    rank_ref[0] = jnp.concatenate(rows, axis=0).astype(I32)


def _rank_call(ri):
    return pl.pallas_call(
        _rank_kernel,
        grid=(TOKENS // TM_RANK,),
        in_specs=[pl.BlockSpec((TM_RANK, ROUTE_W), lambda i: (i, 0))],
        out_specs=(pl.BlockSpec((1, ROUTE_W, TM_RANK), lambda i: (i, 0, 0)),
                   pl.BlockSpec((1, LANES), lambda i: (0, 0))),
        out_shape=(jax.ShapeDtypeStruct((TOKENS // TM_RANK, ROUTE_W, TM_RANK), I32),
                   jax.ShapeDtypeStruct((1, LANES), F32)),
        scratch_shapes=[pltpu.VMEM((1, LANES), F32)],
        compiler_params=_cparams(("arbitrary",)),
        name="rank",
    )(ri)


def _row_copy(src_ref, src_row, dst_ref, dst_row, sem):
    return pltpu.make_async_copy(src_ref.at[pl.ds(src_row, 1)], dst_ref.at[pl.ds(dst_row, 1)], sem)


def _tail_block_copy(zero_ref, xs_ref, seg_end, sem):
    start = pl.multiple_of(seg_end - MOE_BLK, MOE_BLK)
    return pltpu.make_async_copy(zero_ref, xs_ref.at[pl.ds(start, MOE_BLK)], sem)


def _dispatch_kernel(d1_ref, d2_ref, pe_ref, h2_ref, xs_ref, zero_scr, sem, zsem):
    tm = h2_ref.shape[0]
    step = pl.program_id(0)
    base = step * tm

    @pl.when(step == 0)
    def _():
        zero_scr[...] = jnp.zeros_like(zero_scr)

        def seg_nonempty(e):
            return pe_ref[e] > jnp.where(e == 0, 0, pe_ref[jnp.maximum(e - 1, 0)])

        def zissue(e, carry):
            @pl.when(seg_nonempty(e))
            def _():
                _tail_block_copy(zero_scr, xs_ref, pe_ref[e], zsem).start()
            return carry

        def zdrain(e, carry):
            @pl.when(seg_nonempty(e))
            def _():
                _tail_block_copy(zero_scr, xs_ref, pe_ref[e], zsem).wait()
            return carry

        def block_copy(b):
            start = pl.multiple_of(b * MOE_BLK, MOE_BLK)
            return pltpu.make_async_copy(zero_scr, xs_ref.at[pl.ds(start, MOE_BLK)], zsem)

        def tissue(b, carry):
            block_copy(b).start()
            return carry

        def tdrain(b, carry):
            block_copy(b).wait()
            return carry

        n_used = pe_ref[N_EXPERTS - 1] // MOE_BLK
        lax.fori_loop(0, N_EXPERTS, zissue, 0)
        lax.fori_loop(n_used, N_BLOCKS, tissue, 0)
        lax.fori_loop(0, N_EXPERTS, zdrain, 0)
        lax.fori_loop(n_used, N_BLOCKS, tdrain, 0)

    def issue(r, carry):
        _row_copy(h2_ref, r, xs_ref, d1_ref[base + r], sem).start()
        _row_copy(h2_ref, r, xs_ref, d2_ref[base + r], sem).start()
        return carry

    lax.fori_loop(0, tm, issue, 0, unroll=8)
    for _ in range(TOP_K):
        pltpu.make_async_copy(h2_ref, xs_ref.at[pl.ds(0, tm)], sem).wait()


def _dispatch_call(dest1, dest2, pad_ends, h2):
    grid_spec = pltpu.PrefetchScalarGridSpec(
        num_scalar_prefetch=3,
        grid=(TOKENS // TM_ROW,),
        in_specs=[pl.BlockSpec((TM_ROW, D_MODEL), lambda i, d1, d2, pe: (i, 0))],
        out_specs=pl.BlockSpec(memory_space=pl.ANY),
        scratch_shapes=[pltpu.VMEM((MOE_BLK, D_MODEL), F32),
                        pltpu.SemaphoreType.DMA(()),
                        pltpu.SemaphoreType.DMA(())],
    )
    return pl.pallas_call(
        _dispatch_kernel,
        grid_spec=grid_spec,
        out_shape=jax.ShapeDtypeStruct((N_BLOCKS * MOE_BLK, D_MODEL), F32),
        compiler_params=_cparams(("arbitrary",)),
        name="dispatch",
    )(dest1, dest2, pad_ends, h2)


def _expert_kernel(be_ref, nu_ref, xs_ref, w1_ref, w3_ref, w2_ref, y_ref, w1b, w3b, w2b):
    i = pl.program_id(0)
    e = be_ref[i]
    prev = be_ref[jnp.maximum(i - 1, 0)]
    used = i < nu_ref[0]

    @pl.when(used & ((i == 0) | (e != prev)))
    def _():
        w1b[...] = w1_ref[0].astype(BF16)
        w3b[...] = w3_ref[0].astype(BF16)
        w2b[...] = w2_ref[0].astype(BF16)

    @pl.when(used)
    def _():
        xb = xs_ref[...].astype(BF16)
        a = jnp.dot(xb, w1b[...], preferred_element_type=F32)
        g = jnp.dot(xb, w3b[...], preferred_element_type=F32)
        hmid = (a * _sigmoid(a)) * g
        y_ref[...] = jnp.dot(hmid.astype(BF16), w2b[...], preferred_element_type=F32)

    @pl.when(jnp.logical_not(used))
    def _():
        y_ref[...] = jnp.zeros_like(y_ref)


def _expert_call(block_e, n_used, xs, w1, w3, w2):
    grid_spec = pltpu.PrefetchScalarGridSpec(
        num_scalar_prefetch=2,
        grid=(N_BLOCKS,),
        in_specs=[pl.BlockSpec((MOE_BLK, D_MODEL), lambda i, be, nu: (jnp.minimum(i, nu[0] - 1), 0)),
                  pl.BlockSpec((1, D_MODEL, D_EXPERT), lambda i, be, nu: (be[i], 0, 0)),
                  pl.BlockSpec((1, D_MODEL, D_EXPERT), lambda i, be, nu: (be[i], 0, 0)),
                  pl.BlockSpec((1, D_EXPERT, D_MODEL), lambda i, be, nu: (be[i], 0, 0))],
        out_specs=pl.BlockSpec((MOE_BLK, D_MODEL), lambda i, be, nu: (i, 0)),
        scratch_shapes=[pltpu.VMEM((D_MODEL, D_EXPERT), BF16),
                        pltpu.VMEM((D_MODEL, D_EXPERT), BF16),
                        pltpu.VMEM((D_EXPERT, D_MODEL), BF16)],
    )
    return pl.pallas_call(
        _expert_kernel,
        grid_spec=grid_spec,
        out_shape=jax.ShapeDtypeStruct((N_BLOCKS * MOE_BLK, D_MODEL), F32),
        compiler_params=_cparams(("arbitrary",)),
        name="experts",
    )(block_e, n_used, xs, w1, w3, w2)


def _combine_kernel(d1_ref, d2_ref, x1_ref, rf_ref, mod_ref, y_ref, o_ref, ybuf, sem):
    tm = x1_ref.shape[0]
    base = pl.program_id(0) * tm

    def issue(r, carry):
        _row_copy(y_ref, d1_ref[base + r], ybuf.at[0], r, sem).start()
        _row_copy(y_ref, d2_ref[base + r], ybuf.at[1], r, sem).start()
        return carry

    lax.fori_loop(0, tm, issue, 0, unroll=8)
    for kk in range(TOP_K):
        pltpu.make_async_copy(y_ref.at[pl.ds(0, tm)], ybuf.at[kk], sem).wait()

    rf = rf_ref[...]
    gate2 = mod_ref[0, 5:6, :]
    moe = rf[:, 0:1] * ybuf[0] + rf[:, 1:2] * ybuf[1]
    o_ref[...] = x1_ref[...] + gate2 * moe


def _combine_call(dest1, dest2, x1, rf, mod, y):
    nb = SEQ // TM_ROW
    grid_spec = pltpu.PrefetchScalarGridSpec(
        num_scalar_prefetch=2,
        grid=(TOKENS // TM_ROW,),
        in_specs=[pl.BlockSpec((TM_ROW, D_MODEL), lambda i, d1, d2: (i, 0)),
                  pl.BlockSpec((TM_ROW, ROUTE_W), lambda i, d1, d2: (i, 0)),
                  pl.BlockSpec((1, N_MOD, D_MODEL), lambda i, d1, d2: (i // nb, 0, 0)),
                  pl.BlockSpec(memory_space=pl.ANY)],
        out_specs=pl.BlockSpec((TM_ROW, D_MODEL), lambda i, d1, d2: (i, 0)),
        scratch_shapes=[pltpu.VMEM((TOP_K, TM_ROW, D_MODEL), F32),
                        pltpu.SemaphoreType.DMA(())],
    )
    return pl.pallas_call(
        _combine_kernel,
        grid_spec=grid_spec,
        out_shape=jax.ShapeDtypeStruct((TOKENS, D_MODEL), F32),
        compiler_params=_cparams(("arbitrary",)),
        name="combine",
    )(dest1, dest2, x1, rf, mod, y)


def kernel(x, c, positions, ada_w, ada_b, norm1_g, norm2_g, w_in, gla_alpha_up, gla_alpha_b,
           gla_out_g, diff_q_g, diff_k_g, diff_lq1, diff_lk1, diff_lq2, diff_lk2, diff_out_g,
           w_branch_a, w_branch_b, w_out, router_group_w, router_group_b, router_expert_w,
           router_expert_b, expert_w1, expert_w3, expert_w2):
    assert x.shape == (BATCH, SEQ, D_MODEL) and ada_w.shape[0] == 1
    x2 = x.reshape(TOKENS, D_MODEL)

    lr0 = 2 * GLA_QK_W + 2 * GLA_V_W
    w_in0 = w_in[0]
    w_in_p = jnp.concatenate(
        [w_in0[:, :lr0], w_in0[:, lr0 + GLA_GATE_RANK:], w_in0[:, lr0:lr0 + GLA_GATE_RANK],
         jnp.zeros((D_MODEL, LR_PAD - GLA_GATE_RANK), F32)], axis=1).astype(BF16)
    au_pad = jnp.zeros((LR_PAD, GLA_QK_W), F32).at[:GLA_GATE_RANK].set(gla_alpha_up[0])
    au_hi = au_pad.astype(BF16)
    au_lo = (au_pad - au_hi.astype(F32)).astype(BF16)
    gid = jnp.arange(DIFF_QK_W) // DIFF_DH
    bd = jnp.where(gid[:, None] == gid[None, :], 1.0 / DIFF_DH, 0.0).astype(BF16)
    qg_row = jnp.tile(diff_q_g[0], DIFF_QK_W // DIFF_DH).reshape(1, DIFF_QK_W)
    kg_row = jnp.tile(diff_k_g[0], DIFF_QK_W // DIFF_DH).reshape(1, DIFF_QK_W)
    inv_freq = ROPE_THETA ** (-jnp.arange(ROPE_HALF, dtype=F32) / ROPE_HALF)
    invf64 = jnp.concatenate([inv_freq, inv_freq, jnp.zeros((DIFF_DH - ROPE_DIM,), F32)])
    invf_row = jnp.tile(invf64, LANES // DIFF_DH).reshape(1, LANES)
    pos_col = positions.reshape(TOKENS, 1)
    lam_p = jnp.concatenate([diff_lq1, diff_lk1, diff_lq2, diff_lk2], axis=0)
    wr = (jnp.zeros((D_MODEL, LANES), F32)
          .at[:, :N_GROUPS].set(router_group_w[0])
          .at[:, N_GROUPS:N_GROUPS + N_EXPERTS].set(router_expert_w[0]))
    wr_hi = wr.astype(BF16)
    wr_lo = (wr - wr_hi.astype(F32)).astype(BF16)
    br = (jnp.zeros((1, LANES), F32)
          .at[0, :N_GROUPS].set(router_group_b[0])
          .at[0, N_GROUPS:N_GROUPS + N_EXPERTS].set(router_expert_b[0]))

    mod = _mod_call(c, ada_w[0], ada_b[0])
    gqk, gv, gr, lr, qT, kk, vT, sa, sb = _in_call(
        x2, mod, norm1_g, w_in_p, bd, qg_row, kg_row, pos_col, invf_row)
    o_a = _gla_call(gqk, gv, gr, lr, au_hi, au_lo, gla_alpha_b, gla_out_g)
    o_b = _attn_call(qT, kk, vT, lam_p, diff_out_g.reshape(DIFF_DV, 1))
    x1, h2, rf, ri = _merge_call(
        x2, o_a, o_b, sa, sb, mod, norm2_g, w_branch_a[0].astype(BF16),
        w_branch_b[0].astype(BF16), w_out[0].astype(BF16), wr_hi, wr_lo, br)
    rank, cnt = _rank_call(ri)

    counts = cnt[0, :N_EXPERTS].astype(I32)
    padded = ((counts + MOE_BLK - 1) // MOE_BLK) * MOE_BLK
    pad_ends = jnp.cumsum(padded)
    pad_starts = pad_ends - padded
    rows = rank.transpose(1, 0, 2).reshape(ROUTE_W, TOKENS)
    dest1 = pad_starts[rows[0]] + rows[2]
    dest2 = pad_starts[rows[1]] + rows[3]
    n_used = (pad_ends[-1:] // MOE_BLK).astype(I32)
    block_start = jnp.arange(N_BLOCKS, dtype=I32) * MOE_BLK
    block_e = jnp.minimum(
        jnp.sum((pad_ends[None, :] <= block_start[:, None]).astype(I32), axis=1), N_EXPERTS - 1)

    xs = _dispatch_call(dest1, dest2, pad_ends.astype(I32), h2)
    y = _expert_call(block_e, n_used, xs, expert_w1[0], expert_w3[0], expert_w2[0])
    out = _combine_call(dest1, dest2, x1, rf, mod, y)
    return out.reshape(BATCH, SEQ, D_MODEL)
```

```python
import math

import jax
import jax.numpy as jnp
from jax import lax
from jax.experimental import pallas as pl
from jax.experimental.pallas import tpu as pltpu
from jax.experimental.pallas import tpu_sc as plsc

F32 = jnp.float32
BF16 = jnp.bfloat16
I32 = jnp.int32

D_MODEL = 1024
BATCH = 4
SEQ = 4096
TOKENS = BATCH * SEQ
N_MOD = 6
NORM_EPS = 1e-6

GLA_HEADS = 4
GLA_DK = 64
GLA_DV = 128
GLA_GATE_RANK = 16
GLA_GATE_TAU = 16.0
GLA_CHUNK = 64
GLA_QK_W = GLA_HEADS * GLA_DK
GLA_V_W = GLA_HEADS * GLA_DV

DIFF_HEADS = 4
DIFF_DH = 64
DIFF_DV = 2 * DIFF_DH
DIFF_QK_W = DIFF_HEADS * 2 * DIFF_DH
DIFF_V_W = DIFF_HEADS * DIFF_DV
ROPE_THETA = 500000.0
ROPE_DIM = DIFF_DH // 4
ROPE_HALF = ROPE_DIM // 2
NEG_INF = -1e30
LAMBDA_INIT = 0.8 - 0.6 * 1.0

N_GROUPS = 4
EXPERTS_PER_GROUP = 8
N_EXPERTS = N_GROUPS * EXPERTS_PER_GROUP
TOP_K = 2
D_EXPERT = 512

LANES = 128
SUBLANES = 8
ROW_TILE_S = D_MODEL // LANES
SC_CORES = 2
SC_SUBCORES = 16
SC_WORKERS = SC_CORES * SC_SUBCORES
SC_CHUNK = 64
LR_PAD = LANES
D_IN_PAD = 2 * GLA_QK_W + 2 * GLA_V_W + 2 * DIFF_QK_W + DIFF_V_W + 2 * D_MODEL + LR_PAD

TM_IN = 512
TQ = 512
ATT_SUB = 32
Q_SCALE = DIFF_DH ** -0.5 * math.log2(math.e)
N_KV = SEQ // TQ
TT_GLA = 512
TM_MERGE = 512
TM_RANK = 512
TM_ROW = 256
MOE_BLK = 256
N_BLOCKS = (TOKENS * TOP_K + N_EXPERTS * (MOE_BLK - 1) + MOE_BLK - 1) // MOE_BLK
ROUTE_W = 8

VMEM_LIMIT = 56 * 1024 * 1024


def _cparams(sem):
    return pltpu.CompilerParams(dimension_semantics=sem, vmem_limit_bytes=VMEM_LIMIT)


def _sigmoid(x):
    return 1.0 / (1.0 + jnp.exp(-x))


def _row_tile_spec(rows, index_map):
    return pl.BlockSpec((rows * ROW_TILE_S, LANES), index_map)


def _as_row_tiles(a2d):
    return a2d.reshape(a2d.shape[0] // ROW_TILE_S, ROW_TILE_S, LANES)


def _as_2d(a3d):
    return a3d.reshape(a3d.shape[0] * ROW_TILE_S, LANES)


def _store_row_tiles(ref, val):
    for s in range(ROW_TILE_S):
        ref[pl.ds(s, val.shape[0], stride=ROW_TILE_S), :] = val[:, s * LANES:(s + 1) * LANES]


def _load_row_tiles(ref):
    rows = ref.shape[0] // ROW_TILE_S
    return jnp.concatenate(
        [ref[pl.ds(s, rows, stride=ROW_TILE_S), :] for s in range(ROW_TILE_S)], axis=1)


def _mod_kernel(ct_ref, w_ref, b_ref, o_ref):
    ct = ct_ref[...]
    ca = ct * _sigmoid(ct)
    w = w_ref[...]
    rows = []
    for b in range(BATCH):
        rows.append(jnp.sum(w * ca[:, b:b + 1], axis=0, keepdims=True) + b_ref[...])
    rows.append(jnp.zeros((8 - BATCH, w.shape[1]), F32))
    o_ref[...] = jnp.concatenate(rows, axis=0)


def _mod_call(c, ada_w, ada_b):
    tn = D_MODEL
    ct = jnp.zeros((D_MODEL, 8), F32).at[:, :BATCH].set(c.T)
    out = pl.pallas_call(
        _mod_kernel,
        grid=(N_MOD,),
        in_specs=[
            pl.BlockSpec((D_MODEL, 8), lambda j: (0, 0)),
            pl.BlockSpec((D_MODEL, tn), lambda j: (0, j)),
            pl.BlockSpec((1, tn), lambda j: (0, j)),
        ],
        out_specs=pl.BlockSpec((8, tn), lambda j: (0, j)),
        out_shape=jax.ShapeDtypeStruct((8, N_MOD * D_MODEL), F32),
        compiler_params=_cparams(("arbitrary",)),
        name="mod",
    )(ct, ada_w, ada_b.reshape(1, N_MOD * D_MODEL))
    return out[:BATCH].reshape(BATCH, N_MOD, D_MODEL)


def _in_kernel(x_ref, mod_ref, g1_ref, w_ref, bd_ref, qg_ref, kg_ref, pos_ref, invf_ref,
               gqk_ref, gv_ref, gr_ref, lr_ref, qT_ref, k_ref, vT_ref, sa_ref, sb_ref):
    x = x_ref[...]
    shift1 = mod_ref[0, 0:1, :]
    scale1 = mod_ref[0, 1:2, :]
    ms = jnp.mean(x * x, axis=-1, keepdims=True)
    h = (x * lax.rsqrt(ms + NORM_EPS) * g1_ref[...]) * (1.0 + scale1) + shift1
    hb = h.astype(BF16)

    def proj(c0, c1):
        return jnp.dot(hb, w_ref[:, c0:c1], preferred_element_type=F32)

    gqk_ref[...] = proj(0, 512).astype(BF16)
    gv_ref[...] = proj(512, 1024).astype(BF16)
    gr_ref[...] = proj(1024, 1536).astype(BF16)
    sa_ref[...] = _sigmoid(proj(3072, 4096)).astype(BF16)
    sb_ref[...] = _sigmoid(proj(4096, 5120)).astype(BF16)
    lr_ref[...] = proj(5120, 5248)

    tm = x.shape[0]
    ang = pos_ref[...].astype(F32) * invf_ref[...]
    cos = jnp.cos(ang)
    sin = jnp.sin(ang)
    cos4 = jnp.concatenate([cos] * DIFF_HEADS, axis=1)
    sin4 = jnp.concatenate([sin] * DIFF_HEADS, axis=1)
    lane = lax.broadcasted_iota(I32, (tm, DIFF_QK_W), 1)
    first_half = (lane % DIFF_DH) < ROPE_HALF
    bd = bd_ref[...]

    def norm_rope(t, gain_row):
        t2 = t * t
        hi = t2.astype(BF16)
        lo = (t2 - hi.astype(F32)).astype(BF16)
        gms = (jnp.dot(hi, bd, preferred_element_type=F32)
               + jnp.dot(lo, bd, preferred_element_type=F32))
        t = t * lax.rsqrt(gms + NORM_EPS) * gain_row
        nxt = pltpu.roll(t, DIFF_QK_W - ROPE_HALF, 1)
        prv = pltpu.roll(t, ROPE_HALF, 1)
        return t * cos4 + jnp.where(first_half, -nxt, prv) * sin4

    dq = norm_rope(proj(1536, 2048), qg_ref[...]) * Q_SCALE
    dk = norm_rope(proj(2048, 2560), kg_ref[...])
    dv = proj(2560, 3072)
    for hd in range(DIFF_HEADS):
        sl = slice(hd * LANES, (hd + 1) * LANES)
        qT_ref[0, hd] = dq[:, sl].T.astype(BF16)
        k_ref[0, hd, 0] = dk[:, sl].astype(BF16)
        vT_ref[0, hd, 0] = dv[:, sl].T.astype(BF16)


def _in_call(x2, mod, norm1_g, w_in_p, bd, qg_row, kg_row, pos_col, invf_row):
    nb = SEQ // TM_IN
    tok_spec = lambda w: pl.BlockSpec((TM_IN, w), lambda i: (i, 0))
    const2 = lambda r, c: pl.BlockSpec((r, c), lambda i: (0, 0))
    out_shapes = (
        jax.ShapeDtypeStruct((TOKENS, 2 * GLA_QK_W), BF16),
        jax.ShapeDtypeStruct((TOKENS, GLA_V_W), BF16),
        jax.ShapeDtypeStruct((TOKENS, GLA_V_W), BF16),
        jax.ShapeDtypeStruct((TOKENS, LR_PAD), F32),
        jax.ShapeDtypeStruct((BATCH, DIFF_HEADS, LANES, SEQ), BF16),
        jax.ShapeDtypeStruct((BATCH, DIFF_HEADS, N_KV, TQ, LANES), BF16),
        jax.ShapeDtypeStruct((BATCH, DIFF_HEADS, N_KV, LANES, TQ), BF16),
        jax.ShapeDtypeStruct((TOKENS, D_MODEL), BF16),
        jax.ShapeDtypeStruct((TOKENS, D_MODEL), BF16),
    )
    out_specs = (
        tok_spec(2 * GLA_QK_W), tok_spec(GLA_V_W), tok_spec(GLA_V_W), tok_spec(LR_PAD),
        pl.BlockSpec((1, DIFF_HEADS, LANES, TM_IN), lambda i: (i // nb, 0, 0, i % nb)),
        pl.BlockSpec((1, DIFF_HEADS, 1, TM_IN, LANES), lambda i: (i // nb, 0, i % nb, 0, 0)),
        pl.BlockSpec((1, DIFF_HEADS, 1, LANES, TM_IN), lambda i: (i // nb, 0, i % nb, 0, 0)),
        tok_spec(D_MODEL), tok_spec(D_MODEL),
    )
    return pl.pallas_call(
        _in_kernel,
        grid=(TOKENS // TM_IN,),
        in_specs=[
            tok_spec(D_MODEL),
            pl.BlockSpec((1, N_MOD, D_MODEL), lambda i: (i // nb, 0, 0)),
            const2(1, D_MODEL),
            pl.BlockSpec((D_MODEL, D_IN_PAD), lambda i: (0, 0), pipeline_mode=pl.Buffered(1)),
            const2(DIFF_QK_W, DIFF_QK_W),
            const2(1, DIFF_QK_W), const2(1, DIFF_QK_W),
            tok_spec(1),
            const2(1, LANES),
        ],
        out_specs=out_specs,
        out_shape=out_shapes,
        compiler_params=_cparams(("parallel",)),
        name="in_proj",
    )(x2, mod, norm1_g, w_in_p, bd, qg_row, kg_row, pos_col, invf_row)


def _gla_kernel(qk_ref, v_ref, r_ref, lr_ref, auh_ref, aul_ref, ab_ref, og_ref, o_ref,
                state_ref, oacc_ref, snew_ref):
    tt = qk_ref.shape[0]
    n_chunks = tt // GLA_CHUNK

    @pl.when(pl.program_id(1) == 0)
    def _():
        state_ref[...] = jnp.zeros_like(state_ref)

    lr = lr_ref[...]
    lr_hi = lr.astype(BF16)
    lr_lo = (lr - lr_hi.astype(F32)).astype(BF16)
    z = (jnp.dot(lr_hi, auh_ref[...], preferred_element_type=F32)
         + jnp.dot(lr_lo, auh_ref[...], preferred_element_type=F32)
         + jnp.dot(lr_hi, aul_ref[...], preferred_element_type=F32)) + ab_ref[...]
    g = (jnp.minimum(z, 0.0) - jnp.log(1.0 + jnp.exp(-jnp.abs(z)))) * (1.0 / GLA_GATE_TAU)

    row = lax.broadcasted_iota(I32, (tt, GLA_QK_W), 0) % GLA_CHUNK
    b = g
    step = 1
    while step < GLA_CHUNK:
        b = b + jnp.where(row >= step, pltpu.roll(b, step, 0), 0.0)
        step *= 2

    b_last_rows = [b[c * GLA_CHUNK + GLA_CHUNK - 1:(c + 1) * GLA_CHUNK, :] for c in range(n_chunks)]
    b_last = jnp.concatenate(
        [jnp.broadcast_to(bl, (GLA_CHUNK, GLA_QK_W)) for bl in b_last_rows], axis=0)

    qk = qk_ref[...].astype(F32)
    q = qk[:, :GLA_QK_W] * (GLA_DK ** -0.5)
    k = qk[:, GLA_QK_W:]
    q_in = (q * jnp.exp(b)).astype(BF16)
    k_in = (k * jnp.exp(-b)).astype(BF16)
    k_dec = (k * jnp.exp(b_last - b)).astype(BF16)

    ci = lax.broadcasted_iota(I32, (GLA_CHUNK, GLA_CHUNK), 0)
    cj = lax.broadcasted_iota(I32, (GLA_CHUNK, GLA_CHUNK), 1)
    causal = ci >= cj
    eye = ci == cj

    pairs = [(c, hd) for c in range(n_chunks) for hd in range(GLA_HEADS)]

    def rows(c):
        return slice(c * GLA_CHUNK, (c + 1) * GLA_CHUNK)

    def kcols(hd):
        return slice(hd * GLA_DK, (hd + 1) * GLA_DK)

    def vcols(hd):
        return slice(hd * GLA_DV, (hd + 1) * GLA_DV)

    att = {}
    for c, hd in pairs:
        a = lax.dot_general(q_in[rows(c), kcols(hd)], k_in[rows(c), kcols(hd)],
                            (((1,), (1,)), ((), ())), preferred_element_type=F32)
        att[c, hd] = jnp.where(causal, a, 0.0).astype(BF16)
    for c, hd in pairs:
        oacc_ref[rows(c), vcols(hd)] = jnp.dot(att[c, hd], v_ref[rows(c), vcols(hd)],
                                               preferred_element_type=F32)
    for c, hd in pairs:
        snew_ref[c * GLA_HEADS + hd] = lax.dot_general(
            k_dec[rows(c), kcols(hd)], v_ref[rows(c), vcols(hd)],
            (((0,), (0,)), ((), ())), preferred_element_type=F32)

    decay_rows = jnp.exp(jnp.concatenate(
        b_last_rows + [jnp.zeros((LANES - n_chunks, GLA_QK_W), F32)], axis=0))
    decay_cols = decay_rows.T
    states = [state_ref[hd] for hd in range(GLA_HEADS)]
    for c in range(n_chunks):
        for hd in range(GLA_HEADS):
            s_prev = states[hd]
            oacc_ref[rows(c), vcols(hd)] += jnp.dot(
                q_in[rows(c), kcols(hd)], s_prev.astype(BF16), preferred_element_type=F32)
            dcol = decay_cols[kcols(hd), c:c + 1]
            states[hd] = s_prev * dcol + snew_ref[c * GLA_HEADS + hd]
    for hd in range(GLA_HEADS):
        state_ref[hd] = states[hd]

    for hd in range(GLA_HEADS):
        vs = slice(hd * GLA_DV, (hd + 1) * GLA_DV)
        oh = oacc_ref[:, vs]
        ms = jnp.mean(oh * oh, axis=-1, keepdims=True)
        y = oh * lax.rsqrt(ms + NORM_EPS) * og_ref[...]
        r = r_ref[:, vs].astype(F32)
        o_ref[:, vs] = (y * (r * _sigmoid(r))).astype(BF16)


def _gla_call(gqk, gv, gr, lr, au_hi, au_lo, ab_row, og_row):
    nt = SEQ // TT_GLA
    tok = lambda w: pl.BlockSpec((TT_GLA, w), lambda b, t: (b * nt + t, 0))
    const2 = lambda r, c: pl.BlockSpec((r, c), lambda b, t: (0, 0))
    return pl.pallas_call(
        _gla_kernel,
        grid=(BATCH, nt),
        in_specs=[tok(2 * GLA_QK_W), tok(GLA_V_W), tok(GLA_V_W), tok(LR_PAD),
                  const2(LR_PAD, GLA_QK_W), const2(LR_PAD, GLA_QK_W), const2(1, GLA_QK_W),
                  const2(1, GLA_DV)],
        out_specs=tok(GLA_V_W),
        out_shape=jax.ShapeDtypeStruct((TOKENS, GLA_V_W), BF16),
        scratch_shapes=[pltpu.VMEM((GLA_HEADS, GLA_DK, GLA_DV), F32),
                        pltpu.VMEM((TT_GLA, GLA_V_W), F32),
                        pltpu.VMEM((TT_GLA // GLA_CHUNK * GLA_HEADS, GLA_DK, GLA_DV), F32)],
        compiler_params=_cparams(("parallel", "arbitrary")),
        name="gla",
    )(gqk, gv, gr, lr, au_hi, au_lo, ab_row, og_row)


def _attn_kernel(qT_ref, k_ref, vT_ref, lam_ref, og_ref, o_ref,
                 q_scr, s_scr, p_scr, acc_scr, m_scr, l_scr):
    i = pl.program_id(2)
    qT = qT_ref[0, 0]
    rowq = lax.broadcasted_iota(I32, qT.shape, 0)
    zero = jnp.zeros_like(qT)
    q_scr[0] = jnp.where(rowq < DIFF_DH, qT, zero)
    q_scr[1] = jnp.where(rowq >= DIFF_DH, qT, zero)
    m_scr[...] = jnp.full(m_scr.shape, NEG_INF, F32)
    l_scr[...] = jnp.zeros(l_scr.shape, F32)
    acc_scr[...] = jnp.zeros(acc_scr.shape, F32)
    n_sub = TQ // ATT_SUB

    def fold8(t, op):
        return op(t.reshape(t.shape[0] // SUBLANES, SUBLANES, TQ), axis=0)

    def scores(c, j):
        s_scr[c] = jnp.dot(k_ref[0, 0, j], q_scr[c], preferred_element_type=F32)

    def load_s(c, r, masked):
        s = s_scr[c, r * ATT_SUB:(r + 1) * ATT_SUB, :]
        if masked:
            key_i = lax.broadcasted_iota(I32, (ATT_SUB, TQ), 0) + r * ATT_SUB
            qry_i = lax.broadcasted_iota(I32, (ATT_SUB, TQ), 1)
            s = jnp.where(key_i <= qry_i, s, NEG_INF)
        return s

    def softmax_pv(c, j, masked):
        m8 = fold8(load_s(c, 0, masked), jnp.max)
        for r in range(1, n_sub):
            m8 = jnp.maximum(m8, fold8(load_s(c, r, masked), jnp.max))
        m_old = m_scr[c]
        m_new = jnp.maximum(m_old, jnp.max(m8, axis=0, keepdims=True))
        alpha = jnp.exp2(m_old - m_new)
        l8 = l_scr[c] * alpha
        for r in range(n_sub):
            p = jnp.exp2(load_s(c, r, masked) - m_new)
            l8 = l8 + fold8(p, jnp.sum)
            p_scr[c, r * ATT_SUB:(r + 1) * ATT_SUB, :] = p.astype(BF16)
        m_scr[c] = m_new
        l_scr[c] = l8
        acc_scr[c] = acc_scr[c] * alpha + jnp.dot(vT_ref[0, 0, j], p_scr[c],
                                                  preferred_element_type=F32)

    scores(0, 0)

    def body(j, carry):
        scores(1, j)
        softmax_pv(0, j, False)
        scores(0, j + 1)
        softmax_pv(1, j, False)
        return carry

    lax.fori_loop(0, i, body, 0)
    scores(1, i)
    softmax_pv(0, i, True)
    softmax_pv(1, i, True)

    l1 = jnp.sum(l_scr[0], axis=0, keepdims=True)
    l2 = jnp.sum(l_scr[1], axis=0, keepdims=True)

    lam_p = lam_ref[...]
    lam = (jnp.exp(jnp.sum(lam_p[0:1] * lam_p[1:2], axis=1, keepdims=True))
           - jnp.exp(jnp.sum(lam_p[2:3] * lam_p[3:4], axis=1, keepdims=True)) + LAMBDA_INIT)
    oT = acc_scr[0] / l1 - lam * (acc_scr[1] / l2)
    ms = jnp.mean(oT * oT, axis=0, keepdims=True)
    y = oT * lax.rsqrt(ms + NORM_EPS) * og_ref[...] * (1.0 - LAMBDA_INIT)
    o_ref[...] = y.T.astype(BF16)


def _attn_call(qT, kk, vT, lam_p, og_col):
    nq = SEQ // TQ
    return pl.pallas_call(
        _attn_kernel,
        grid=(BATCH, DIFF_HEADS, nq),
        in_specs=[
            pl.BlockSpec((1, 1, LANES, TQ), lambda b, h, i: (b, h, 0, i)),
            pl.BlockSpec((1, 1, N_KV, TQ, LANES), lambda b, h, i: (b, h, 0, 0, 0)),
            pl.BlockSpec((1, 1, N_KV, LANES, TQ), lambda b, h, i: (b, h, 0, 0, 0)),
            pl.BlockSpec((4, DIFF_DH), lambda b, h, i: (0, 0)),
            pl.BlockSpec((DIFF_DV, 1), lambda b, h, i: (0, 0)),
        ],
        out_specs=pl.BlockSpec((TQ, DIFF_DV), lambda b, h, i: (b * nq + i, h)),
        out_shape=jax.ShapeDtypeStruct((TOKENS, DIFF_V_W), BF16),
        scratch_shapes=[pltpu.VMEM((2, LANES, TQ), BF16),
                        pltpu.VMEM((2, TQ, TQ), F32),
                        pltpu.VMEM((2, TQ, TQ), BF16),
                        pltpu.VMEM((2, DIFF_DV, TQ), F32),
                        pltpu.VMEM((2, 1, TQ), F32),
                        pltpu.VMEM((2, SUBLANES, TQ), F32)],
        compiler_params=_cparams(("parallel", "parallel", "arbitrary")),
        name="attn",
    )(qT, kk, vT, lam_p, og_col)


def _merge_kernel(x_ref, oa_ref, ob_ref, sa_ref, sb_ref, mod_ref, g2_ref, wa_ref, wb_ref, wo_ref,
                  wrh_ref, wrl_ref, br_ref, x1_ref, h2_ref, rf_ref, ri_ref):
    tm = x_ref.shape[0]
    ma = jnp.dot(oa_ref[...], wa_ref[...], preferred_element_type=F32)
    mb = jnp.dot(ob_ref[...], wb_ref[...], preferred_element_type=F32)
    merged = sa_ref[...].astype(F32) * ma + sb_ref[...].astype(F32) * mb
    y = jnp.dot(merged.astype(BF16), wo_ref[...], preferred_element_type=F32)
    gate1 = mod_ref[0, 2:3, :]
    shift2 = mod_ref[0, 3:4, :]
    scale2 = mod_ref[0, 4:5, :]
    x1 = x_ref[...] + gate1 * y
    x1_ref[...] = x1
    ms = jnp.mean(x1 * x1, axis=-1, keepdims=True)
    h2 = (x1 * lax.rsqrt(ms + NORM_EPS) * g2_ref[...]) * (1.0 + scale2) + shift2
    _store_row_tiles(h2_ref, h2)

    h2_hi = h2.astype(BF16)
    h2_lo = (h2 - h2_hi.astype(F32)).astype(BF16)
    logits = (jnp.dot(h2_hi, wrh_ref[...], preferred_element_type=F32)
              + jnp.dot(h2_lo, wrh_ref[...], preferred_element_type=F32)
              + jnp.dot(h2_hi, wrl_ref[...], preferred_element_type=F32)) + br_ref[...]
    lane = lax.broadcasted_iota(I32, (tm, LANES), 1).astype(F32)
    ninf = -jnp.inf
    big = float(LANES)

    def first_argmax(v):
        vmax = jnp.max(v, axis=1, keepdims=True)
        idx = jnp.min(jnp.where(v == vmax, lane, big), axis=1, keepdims=True)
        return vmax, idx

    gl = jnp.where(lane < N_GROUPS, logits, ninf)
    gmax, gidx = first_argmax(gl)
    p_top = 1.0 / jnp.sum(jnp.exp(gl - gmax), axis=1, keepdims=True)
    lo = N_GROUPS + EXPERTS_PER_GROUP * gidx
    el = jnp.where((lane >= lo) & (lane < lo + EXPERTS_PER_GROUP), logits, ninf)
    e1max, e1 = first_argmax(el)
    e2max, e2 = first_argmax(jnp.where(lane == e1, ninf, el))
    t = jnp.exp(e2max - e1max)
    w1 = 1.0 / (1.0 + t)
    w2 = t / (1.0 + t)
    col = lax.broadcasted_iota(I32, (tm, ROUTE_W), 1)
    rf_ref[...] = jnp.where(col == 0, p_top * w1, jnp.where(col == 1, p_top * w2, 0.0))
    ids = jnp.where(col == 0, e1 - N_GROUPS, jnp.where(col == 1, e2 - N_GROUPS, 0.0))
    ri_ref[...] = ids.astype(I32)


def _merge_call(x2, o_a, o_b, sa, sb, mod, norm2_g, wa, wb, wo, wr_hi, wr_lo, br):
    nb = SEQ // TM_MERGE
    tok = lambda w: pl.BlockSpec((TM_MERGE, w), lambda i: (i, 0))
    const2 = lambda r, c: pl.BlockSpec((r, c), lambda i: (0, 0))
    return pl.pallas_call(
        _merge_kernel,
        grid=(TOKENS // TM_MERGE,),
        in_specs=[tok(D_MODEL), tok(GLA_V_W), tok(DIFF_V_W), tok(D_MODEL), tok(D_MODEL),
                  pl.BlockSpec((1, N_MOD, D_MODEL), lambda i: (i // nb, 0, 0)),
                  const2(1, D_MODEL),
                  const2(GLA_V_W, D_MODEL), const2(DIFF_V_W, D_MODEL), const2(D_MODEL, D_MODEL),
                  const2(D_MODEL, LANES), const2(D_MODEL, LANES), const2(1, LANES)],
        out_specs=(tok(D_MODEL), _row_tile_spec(TM_MERGE, lambda i: (i, 0)),
                   tok(ROUTE_W), tok(ROUTE_W)),
        out_shape=(jax.ShapeDtypeStruct((TOKENS, D_MODEL), F32),
                   jax.ShapeDtypeStruct((TOKENS * ROW_TILE_S, LANES), F32),
                   jax.ShapeDtypeStruct((TOKENS, ROUTE_W), F32),
                   jax.ShapeDtypeStruct((TOKENS, ROUTE_W), I32)),
        compiler_params=_cparams(("parallel",)),
        name="merge",
    )(x2, o_a, o_b, sa, sb, mod, norm2_g, wa, wb, wo, wr_hi, wr_lo, br)


def _rank_kernel(ri_ref, rank_ref, cnt_ref, carry_ref):
    tm = ri_ref.shape[0]

    @pl.when(pl.program_id(0) == 0)
    def _():
        carry_ref[...] = jnp.zeros_like(carry_ref)

    ri = ri_ref[...]
    e1 = ri[:, 0:1]
    e2 = ri[:, 1:2]
    lane = lax.broadcasted_iota(I32, (tm, LANES), 1)
    hit1 = lane == e1
    hit2 = lane == e2
    onehot = jnp.where(hit1 | hit2, 1.0, 0.0)
    ti = lax.broadcasted_iota(I32, (tm, tm), 0)
    tj = lax.broadcasted_iota(I32, (tm, tm), 1)
    strict_lower = jnp.where(ti > tj, 1.0, 0.0).astype(BF16)
    before = jnp.dot(strict_lower, onehot.astype(BF16), preferred_element_type=F32) + carry_ref[...]
    r1 = jnp.sum(jnp.where(hit1, before, 0.0), axis=1, keepdims=True)
    r2 = jnp.sum(jnp.where(hit2, before, 0.0), axis=1, keepdims=True)
    carry_ref[...] = carry_ref[...] + jnp.sum(onehot, axis=0, keepdims=True)
    cnt_ref[...] = carry_ref[...]

    eye = ti == tj

    def to_row(colv):
        return jnp.sum(jnp.where(eye, colv, 0.0), axis=0, keepdims=True)

    rows = [to_row(e1.astype(F32)), to_row(e2.astype(F32)), to_row(r1), to_row(r2),
            jnp.zeros((ROUTE_W - 4, tm), F32)]
    rank_ref[0] = jnp.concatenate(rows, axis=0).astype(I32)


def _rank_call(ri):
    return pl.pallas_call(
        _rank_kernel,
        grid=(TOKENS // TM_RANK,),
        in_specs=[pl.BlockSpec((TM_RANK, ROUTE_W), lambda i: (i, 0))],
        out_specs=(pl.BlockSpec((1, ROUTE_W, TM_RANK), lambda i: (i, 0, 0)),
                   pl.BlockSpec((1, LANES), lambda i: (0, 0))),
        out_shape=(jax.ShapeDtypeStruct((TOKENS // TM_RANK, ROUTE_W, TM_RANK), I32),
                   jax.ShapeDtypeStruct((1, LANES), F32)),
        scratch_shapes=[pltpu.VMEM((1, LANES), F32)],
        compiler_params=_cparams(("arbitrary",)),
        name="rank",
    )(ri)


def _sc_params():
    return pltpu.CompilerParams(use_tc_tiling_on_sc=True)


def _sc_mesh():
    return plsc.VectorSubcoreMesh(core_axis_name="core", subcore_axis_name="subcore")


def _sc_worker_base(per_worker):
    wid = lax.axis_index("subcore") * SC_CORES + lax.axis_index("core")
    return wid * per_worker


def _sc_scatter_rows(src, dest1, dest2, n_out):
    n = src.shape[0]
    per_worker = n // SC_WORKERS
    assert per_worker * SC_WORKERS == n and per_worker % SC_CHUNK == 0

    def body(src_hbm, d1_hbm, d2_hbm, out_hbm, idx_v, rows_v):
        base = _sc_worker_base(per_worker)

        @pl.loop(0, per_worker // SC_CHUNK)
        def _(j):
            start = pl.multiple_of(base + j * SC_CHUNK, SC_CHUNK)
            pltpu.sync_copy(src_hbm.at[pl.ds(start, SC_CHUNK)], rows_v)
            for d_hbm in (d1_hbm, d2_hbm):
                pltpu.sync_copy(d_hbm.at[pl.ds(start, SC_CHUNK)], idx_v)
                pltpu.sync_copy(rows_v, out_hbm.at[idx_v])

    return pl.kernel(
        body,
        out_type=jax.ShapeDtypeStruct((n_out, ROW_TILE_S, LANES), F32),
        mesh=_sc_mesh(),
        scratch_types=[pltpu.VMEM((SC_CHUNK,), I32),
                       pltpu.VMEM((SC_CHUNK, ROW_TILE_S, LANES), F32)],
        compiler_params=_sc_params(),
        name="sc_dispatch",
    )(src, dest1, dest2)


def _sc_gather_rows(table, idx):
    n = idx.shape[0]
    per_worker = n // SC_WORKERS
    assert per_worker * SC_WORKERS == n and per_worker % SC_CHUNK == 0

    def body(table_hbm, idx_hbm, out_hbm, idx_v, rows_v):
        base = _sc_worker_base(per_worker)

        @pl.loop(0, per_worker // SC_CHUNK)
        def _(j):
            start = pl.multiple_of(base + j * SC_CHUNK, SC_CHUNK)
            pltpu.sync_copy(idx_hbm.at[pl.ds(start, SC_CHUNK)], idx_v)
            pltpu.sync_copy(table_hbm.at[idx_v], rows_v)
            pltpu.sync_copy(rows_v, out_hbm.at[pl.ds(start, SC_CHUNK)])

    return pl.kernel(
        body,
        out_type=jax.ShapeDtypeStruct((n, ROW_TILE_S, LANES), F32),
        mesh=_sc_mesh(),
        scratch_types=[pltpu.VMEM((SC_CHUNK,), I32),
                       pltpu.VMEM((SC_CHUNK, ROW_TILE_S, LANES), F32)],
        compiler_params=_sc_params(),
        name="sc_gather",
    )(table, idx)


def _row_copy(src_ref, src_row, dst_ref, dst_row, sem):
    return pltpu.make_async_copy(src_ref.at[pl.ds(src_row, 1)], dst_ref.at[pl.ds(dst_row, 1)], sem)


def _tail_block_copy(zero_ref, xs_ref, seg_end, sem):
    start = pl.multiple_of(seg_end - MOE_BLK, MOE_BLK)
    return pltpu.make_async_copy(zero_ref, xs_ref.at[pl.ds(start, MOE_BLK)], sem)


def _dispatch_kernel(d1_ref, d2_ref, pe_ref, h2_ref, xs_ref, zero_scr, sem, zsem):
    tm = h2_ref.shape[0]
    step = pl.program_id(0)
    base = step * tm

    @pl.when(step == 0)
    def _():
        zero_scr[...] = jnp.zeros_like(zero_scr)

        def seg_nonempty(e):
            return pe_ref[e] > jnp.where(e == 0, 0, pe_ref[jnp.maximum(e - 1, 0)])

        def zissue(e, carry):
            @pl.when(seg_nonempty(e))
            def _():
                _tail_block_copy(zero_scr, xs_ref, pe_ref[e], zsem).start()
            return carry

        def zdrain(e, carry):
            @pl.when(seg_nonempty(e))
            def _():
                _tail_block_copy(zero_scr, xs_ref, pe_ref[e], zsem).wait()
            return carry

        def block_copy(b):
            start = pl.multiple_of(b * MOE_BLK, MOE_BLK)
            return pltpu.make_async_copy(zero_scr, xs_ref.at[pl.ds(start, MOE_BLK)], zsem)

        def tissue(b, carry):
            block_copy(b).start()
            return carry

        def tdrain(b, carry):
            block_copy(b).wait()
            return carry

        n_used = pe_ref[N_EXPERTS - 1] // MOE_BLK
        lax.fori_loop(0, N_EXPERTS, zissue, 0)
        lax.fori_loop(n_used, N_BLOCKS, tissue, 0)
        lax.fori_loop(0, N_EXPERTS, zdrain, 0)
        lax.fori_loop(n_used, N_BLOCKS, tdrain, 0)

    def issue(r, carry):
        _row_copy(h2_ref, r, xs_ref, d1_ref[base + r], sem).start()
        _row_copy(h2_ref, r, xs_ref, d2_ref[base + r], sem).start()
        return carry

    lax.fori_loop(0, tm, issue, 0, unroll=8)
    for _ in range(TOP_K):
        pltpu.make_async_copy(h2_ref, xs_ref.at[pl.ds(0, tm)], sem).wait()


def _dispatch_call(dest1, dest2, pad_ends, h2):
    grid_spec = pltpu.PrefetchScalarGridSpec(
        num_scalar_prefetch=3,
        grid=(TOKENS // TM_ROW,),
        in_specs=[pl.BlockSpec((TM_ROW, D_MODEL), lambda i, d1, d2, pe: (i, 0))],
        out_specs=pl.BlockSpec(memory_space=pl.ANY),
        scratch_shapes=[pltpu.VMEM((MOE_BLK, D_MODEL), F32),
                        pltpu.SemaphoreType.DMA(()),
                        pltpu.SemaphoreType.DMA(())],
    )
    return pl.pallas_call(
        _dispatch_kernel,
        grid_spec=grid_spec,
        out_shape=jax.ShapeDtypeStruct((N_BLOCKS * MOE_BLK, D_MODEL), F32),
        compiler_params=_cparams(("arbitrary",)),
        name="dispatch",
    )(dest1, dest2, pad_ends, h2)


def _expert_kernel(be_ref, nv_ref, nu_ref, xs_ref, w1_ref, w3_ref, w2_ref, y_ref, w1b, w3b, w2b):
    i = pl.program_id(0)
    e = be_ref[i]
    prev = be_ref[jnp.maximum(i - 1, 0)]
    used = i < nu_ref[0]

    @pl.when(used & ((i == 0) | (e != prev)))
    def _():
        w1b[...] = w1_ref[0].astype(BF16)
        w3b[...] = w3_ref[0].astype(BF16)
        w2b[...] = w2_ref[0].astype(BF16)

    @pl.when(used)
    def _():
        row = lax.broadcasted_iota(I32, (MOE_BLK, D_MODEL), 0)
        xb = jnp.where(row < nv_ref[i], _load_row_tiles(xs_ref), 0.0).astype(BF16)
        a = jnp.dot(xb, w1b[...], preferred_element_type=F32)
        g = jnp.dot(xb, w3b[...], preferred_element_type=F32)
        hmid = (a * _sigmoid(a)) * g
        _store_row_tiles(y_ref, jnp.dot(hmid.astype(BF16), w2b[...], preferred_element_type=F32))

    @pl.when(jnp.logical_not(used))
    def _():
        y_ref[...] = jnp.zeros_like(y_ref)


def _expert_call(block_e, n_valid, n_used, xs, w1, w3, w2):
    grid_spec = pltpu.PrefetchScalarGridSpec(
        num_scalar_prefetch=3,
        grid=(N_BLOCKS,),
        in_specs=[_row_tile_spec(MOE_BLK, lambda i, be, nv, nu: (jnp.minimum(i, nu[0] - 1), 0)),
                  pl.BlockSpec((1, D_MODEL, D_EXPERT), lambda i, be, nv, nu: (be[i], 0, 0)),
                  pl.BlockSpec((1, D_MODEL, D_EXPERT), lambda i, be, nv, nu: (be[i], 0, 0)),
                  pl.BlockSpec((1, D_EXPERT, D_MODEL), lambda i, be, nv, nu: (be[i], 0, 0))],
        out_specs=_row_tile_spec(MOE_BLK, lambda i, be, nv, nu: (i, 0)),
        scratch_shapes=[pltpu.VMEM((D_MODEL, D_EXPERT), BF16),
                        pltpu.VMEM((D_MODEL, D_EXPERT), BF16),
                        pltpu.VMEM((D_EXPERT, D_MODEL), BF16)],
    )
    return pl.pallas_call(
        _expert_kernel,
        grid_spec=grid_spec,
        out_shape=jax.ShapeDtypeStruct((N_BLOCKS * MOE_BLK * ROW_TILE_S, LANES), F32),
        compiler_params=_cparams(("arbitrary",)),
        name="experts",
    )(block_e, n_valid, n_used, xs, w1, w3, w2)


def _combine_rows_kernel(x1_ref, rf_ref, mod_ref, ya_ref, yb_ref, o_ref):
    rf = rf_ref[...]
    gate2 = mod_ref[0, 5:6, :]
    moe = rf[:, 0:1] * _load_row_tiles(ya_ref) + rf[:, 1:2] * _load_row_tiles(yb_ref)
    o_ref[...] = x1_ref[...] + gate2 * moe


def _combine_rows_call(x1, rf, mod, yg):
    nb = SEQ // TM_ROW
    n_tiles = TOKENS // TM_ROW
    return pl.pallas_call(
        _combine_rows_kernel,
        grid=(n_tiles,),
        in_specs=[pl.BlockSpec((TM_ROW, D_MODEL), lambda i: (i, 0)),
                  pl.BlockSpec((TM_ROW, ROUTE_W), lambda i: (i, 0)),
                  pl.BlockSpec((1, N_MOD, D_MODEL), lambda i: (i // nb, 0, 0)),
                  _row_tile_spec(TM_ROW, lambda i: (i, 0)),
                  _row_tile_spec(TM_ROW, lambda i: (i + n_tiles, 0))],
        out_specs=pl.BlockSpec((TM_ROW, D_MODEL), lambda i: (i, 0)),
        out_shape=jax.ShapeDtypeStruct((TOKENS, D_MODEL), F32),
        compiler_params=_cparams(("parallel",)),
        name="combine",
    )(x1, rf, mod, yg, yg)


def _combine_kernel(d1_ref, d2_ref, x1_ref, rf_ref, mod_ref, y_ref, o_ref, ybuf, sem):
    tm = x1_ref.shape[0]
    base = pl.program_id(0) * tm

    def issue(r, carry):
        _row_copy(y_ref, d1_ref[base + r], ybuf.at[0], r, sem).start()
        _row_copy(y_ref, d2_ref[base + r], ybuf.at[1], r, sem).start()
        return carry

    lax.fori_loop(0, tm, issue, 0, unroll=8)
    for kk in range(TOP_K):
        pltpu.make_async_copy(y_ref.at[pl.ds(0, tm)], ybuf.at[kk], sem).wait()

    rf = rf_ref[...]
    gate2 = mod_ref[0, 5:6, :]
    moe = rf[:, 0:1] * ybuf[0] + rf[:, 1:2] * ybuf[1]
    o_ref[...] = x1_ref[...] + gate2 * moe


def _combine_call(dest1, dest2, x1, rf, mod, y):
    nb = SEQ // TM_ROW
    grid_spec = pltpu.PrefetchScalarGridSpec(
        num_scalar_prefetch=2,
        grid=(TOKENS // TM_ROW,),
        in_specs=[pl.BlockSpec((TM_ROW, D_MODEL), lambda i, d1, d2: (i, 0)),
                  pl.BlockSpec((TM_ROW, ROUTE_W), lambda i, d1, d2: (i, 0)),
                  pl.BlockSpec((1, N_MOD, D_MODEL), lambda i, d1, d2: (i // nb, 0, 0)),
                  pl.BlockSpec(memory_space=pl.ANY)],
        out_specs=pl.BlockSpec((TM_ROW, D_MODEL), lambda i, d1, d2: (i, 0)),
        scratch_shapes=[pltpu.VMEM((TOP_K, TM_ROW, D_MODEL), F32),
                        pltpu.SemaphoreType.DMA(())],
    )
    return pl.pallas_call(
        _combine_kernel,
        grid_spec=grid_spec,
        out_shape=jax.ShapeDtypeStruct((TOKENS, D_MODEL), F32),
        compiler_params=_cparams(("arbitrary",)),
        name="combine",
    )(dest1, dest2, x1, rf, mod, y)


def kernel(x, c, positions, ada_w, ada_b, norm1_g, norm2_g, w_in, gla_alpha_up, gla_alpha_b,
           gla_out_g, diff_q_g, diff_k_g, diff_lq1, diff_lk1, diff_lq2, diff_lk2, diff_out_g,
           w_branch_a, w_branch_b, w_out, router_group_w, router_group_b, router_expert_w,
           router_expert_b, expert_w1, expert_w3, expert_w2):
    assert x.shape == (BATCH, SEQ, D_MODEL) and ada_w.shape[0] == 1
    x2 = x.reshape(TOKENS, D_MODEL)

    lr0 = 2 * GLA_QK_W + 2 * GLA_V_W
    w_in0 = w_in[0]
    w_in_p = jnp.concatenate(
        [w_in0[:, :lr0], w_in0[:, lr0 + GLA_GATE_RANK:], w_in0[:, lr0:lr0 + GLA_GATE_RANK],
         jnp.zeros((D_MODEL, LR_PAD - GLA_GATE_RANK), F32)], axis=1).astype(BF16)
    au_pad = jnp.zeros((LR_PAD, GLA_QK_W), F32).at[:GLA_GATE_RANK].set(gla_alpha_up[0])
    au_hi = au_pad.astype(BF16)
    au_lo = (au_pad - au_hi.astype(F32)).astype(BF16)
    gid = jnp.arange(DIFF_QK_W) // DIFF_DH
    bd = jnp.where(gid[:, None] == gid[None, :], 1.0 / DIFF_DH, 0.0).astype(BF16)
    qg_row = jnp.tile(diff_q_g[0], DIFF_QK_W // DIFF_DH).reshape(1, DIFF_QK_W)
    kg_row = jnp.tile(diff_k_g[0], DIFF_QK_W // DIFF_DH).reshape(1, DIFF_QK_W)
    inv_freq = ROPE_THETA ** (-jnp.arange(ROPE_HALF, dtype=F32) / ROPE_HALF)
    invf64 = jnp.concatenate([inv_freq, inv_freq, jnp.zeros((DIFF_DH - ROPE_DIM,), F32)])
    invf_row = jnp.tile(invf64, LANES // DIFF_DH).reshape(1, LANES)
    pos_col = positions.reshape(TOKENS, 1)
    lam_p = jnp.concatenate([diff_lq1, diff_lk1, diff_lq2, diff_lk2], axis=0)
    wr = (jnp.zeros((D_MODEL, LANES), F32)
          .at[:, :N_GROUPS].set(router_group_w[0])
          .at[:, N_GROUPS:N_GROUPS + N_EXPERTS].set(router_expert_w[0]))
    wr_hi = wr.astype(BF16)
    wr_lo = (wr - wr_hi.astype(F32)).astype(BF16)
    br = (jnp.zeros((1, LANES), F32)
          .at[0, :N_GROUPS].set(router_group_b[0])
          .at[0, N_GROUPS:N_GROUPS + N_EXPERTS].set(router_expert_b[0]))

    mod = _mod_call(c, ada_w[0], ada_b[0])
    gqk, gv, gr, lr, qT, kk, vT, sa, sb = _in_call(
        x2, mod, norm1_g, w_in_p, bd, qg_row, kg_row, pos_col, invf_row)
    o_a = _gla_call(gqk, gv, gr, lr, au_hi, au_lo, gla_alpha_b, gla_out_g)
    o_b = _attn_call(qT, kk, vT, lam_p, diff_out_g.reshape(DIFF_DV, 1))
    x1, h2, rf, ri = _merge_call(
        x2, o_a, o_b, sa, sb, mod, norm2_g, w_branch_a[0].astype(BF16),
        w_branch_b[0].astype(BF16), w_out[0].astype(BF16), wr_hi, wr_lo, br)
    rank, cnt = _rank_call(ri)

    counts = cnt[0, :N_EXPERTS].astype(I32)
    padded = ((counts + MOE_BLK - 1) // MOE_BLK) * MOE_BLK
    pad_ends = jnp.cumsum(padded)
    pad_starts = pad_ends - padded
    rows = rank.transpose(1, 0, 2).reshape(ROUTE_W, TOKENS)
    dest1 = pad_starts[rows[0]] + rows[2]
    dest2 = pad_starts[rows[1]] + rows[3]
    n_used = (pad_ends[-1:] // MOE_BLK).astype(I32)
    block_start = jnp.arange(N_BLOCKS, dtype=I32) * MOE_BLK
    block_e = jnp.minimum(
        jnp.sum((pad_ends[None, :] <= block_start[:, None]).astype(I32), axis=1), N_EXPERTS - 1)

    seg_valid_end = (pad_starts + counts)[block_e]
    n_valid = jnp.clip(seg_valid_end - block_start, 0, MOE_BLK).astype(I32)

    xs = _sc_scatter_rows(_as_row_tiles(h2), dest1, dest2, N_BLOCKS * MOE_BLK)
    y = _expert_call(block_e, n_valid, n_used, _as_2d(xs), expert_w1[0], expert_w3[0], expert_w2[0])
    yg = _sc_gather_rows(_as_row_tiles(y), jnp.concatenate([dest1, dest2]))
    out = _combine_rows_call(x1, rf, mod, _as_2d(yg))
    return out.reshape(BATCH, SEQ, D_MODEL)
```

```python
import math

import jax
import jax.numpy as jnp
from jax import lax
from jax.experimental import pallas as pl
from jax.experimental.pallas import tpu as pltpu
from jax.experimental.pallas import tpu_sc as plsc

F32 = jnp.float32
BF16 = jnp.bfloat16
I32 = jnp.int32

D_MODEL = 1024
BATCH = 4
SEQ = 4096
TOKENS = BATCH * SEQ
N_MOD = 6
NORM_EPS = 1e-6

GLA_HEADS = 4
GLA_DK = 64
GLA_DV = 128
GLA_GATE_RANK = 16
GLA_GATE_TAU = 16.0
GLA_CHUNK = 64
GLA_QK_W = GLA_HEADS * GLA_DK
GLA_V_W = GLA_HEADS * GLA_DV

DIFF_HEADS = 4
DIFF_DH = 64
DIFF_DV = 2 * DIFF_DH
DIFF_QK_W = DIFF_HEADS * 2 * DIFF_DH
DIFF_V_W = DIFF_HEADS * DIFF_DV
ROPE_THETA = 500000.0
ROPE_DIM = DIFF_DH // 4
ROPE_HALF = ROPE_DIM // 2
NEG_INF = -1e30
LAMBDA_INIT = 0.8 - 0.6 * 1.0

N_GROUPS = 4
EXPERTS_PER_GROUP = 8
N_EXPERTS = N_GROUPS * EXPERTS_PER_GROUP
TOP_K = 2
D_EXPERT = 512

LANES = 128
SUBLANES = 8
ROW_TILE_S = D_MODEL // LANES
SC_CORES = 2
SC_SUBCORES = 16
SC_WORKERS = SC_CORES * SC_SUBCORES
SC_CHUNK = 64
LR_PAD = LANES
D_IN_PAD = 2 * GLA_QK_W + 2 * GLA_V_W + 2 * DIFF_QK_W + DIFF_V_W + 2 * D_MODEL + LR_PAD

TM_IN = 512
TQ = 512
ATT_SUB = 32
Q_SCALE = DIFF_DH ** -0.5 * math.log2(math.e)
N_KV = SEQ // TQ
TT_GLA = 512
TM_MERGE = 512
TM_RANK = 512
TM_ROW = 256
MOE_BLK = 256
N_BLOCKS = (TOKENS * TOP_K + N_EXPERTS * (MOE_BLK - 1) + MOE_BLK - 1) // MOE_BLK
ROUTE_W = 8

VMEM_LIMIT = 56 * 1024 * 1024


def _cparams(sem):
    return pltpu.CompilerParams(dimension_semantics=sem, vmem_limit_bytes=VMEM_LIMIT)


def _sigmoid(x):
    return 1.0 / (1.0 + jnp.exp(-x))


def _row_tile_spec(rows, index_map):
    return pl.BlockSpec((rows * ROW_TILE_S, LANES), index_map)


def _as_row_tiles(a2d):
    return a2d.reshape(a2d.shape[0] // ROW_TILE_S, ROW_TILE_S, LANES)


def _as_2d(a3d):
    return a3d.reshape(a3d.shape[0] * ROW_TILE_S, LANES)


def _store_row_tiles(ref, val):
    for s in range(ROW_TILE_S):
        ref[pl.ds(s, val.shape[0], stride=ROW_TILE_S), :] = val[:, s * LANES:(s + 1) * LANES]


def _load_row_tiles(ref):
    rows = ref.shape[0] // ROW_TILE_S
    return jnp.concatenate(
        [ref[pl.ds(s, rows, stride=ROW_TILE_S), :] for s in range(ROW_TILE_S)], axis=1)


def _mod_kernel(ct_ref, w_ref, b_ref, o_ref):
    ct = ct_ref[...]
    ca = ct * _sigmoid(ct)
    w = w_ref[...]
    rows = []
    for b in range(BATCH):
        rows.append(jnp.sum(w * ca[:, b:b + 1], axis=0, keepdims=True) + b_ref[...])
    rows.append(jnp.zeros((8 - BATCH, w.shape[1]), F32))
    o_ref[...] = jnp.concatenate(rows, axis=0)


def _mod_call(c, ada_w, ada_b):
    tn = D_MODEL
    ct = jnp.zeros((D_MODEL, 8), F32).at[:, :BATCH].set(c.T)
    out = pl.pallas_call(
        _mod_kernel,
        grid=(N_MOD,),
        in_specs=[
            pl.BlockSpec((D_MODEL, 8), lambda j: (0, 0)),
            pl.BlockSpec((D_MODEL, tn), lambda j: (0, j)),
            pl.BlockSpec((1, tn), lambda j: (0, j)),
        ],
        out_specs=pl.BlockSpec((8, tn), lambda j: (0, j)),
        out_shape=jax.ShapeDtypeStruct((8, N_MOD * D_MODEL), F32),
        compiler_params=_cparams(("arbitrary",)),
        name="mod",
    )(ct, ada_w, ada_b.reshape(1, N_MOD * D_MODEL))
    return out[:BATCH].reshape(BATCH, N_MOD, D_MODEL)


def _wprep_kernel(w_ref, o_ref):
    w = w_ref[...]
    lr0 = 2 * GLA_QK_W + 2 * GLA_V_W
    rest = D_IN_PAD - LR_PAD - lr0
    o_ref[:, :lr0] = w[:, :lr0].astype(BF16)
    o_ref[:, lr0:lr0 + rest] = w[:, lr0 + GLA_GATE_RANK:].astype(BF16)
    lr = jnp.concatenate([w[:, lr0:lr0 + GLA_GATE_RANK],
                          jnp.zeros((w.shape[0], LR_PAD - GLA_GATE_RANK), F32)], axis=1)
    o_ref[:, lr0 + rest:] = lr.astype(BF16)


def _wprep_call(w_in0):
    d_in = w_in0.shape[1]
    rows = 128
    return pl.pallas_call(
        _wprep_kernel,
        grid=(D_MODEL // rows,),
        in_specs=[pl.BlockSpec((rows, d_in), lambda i: (i, 0))],
        out_specs=pl.BlockSpec((rows, D_IN_PAD), lambda i: (i, 0)),
        out_shape=jax.ShapeDtypeStruct((D_MODEL, D_IN_PAD), BF16),
        compiler_params=_cparams(("parallel",)),
        name="w_prep",
    )(w_in0)


def _in_kernel(x_ref, mod_ref, g1_ref, w_ref, bd_ref, qg_ref, kg_ref, pos_ref, invf_ref,
               gqk_ref, gv_ref, gr_ref, lr_ref, qT_ref, k_ref, vT_ref, sa_ref, sb_ref,
               cos_scr, sin_scr):
    ang = pos_ref[...].astype(F32) * invf_ref[...]
    cos_scr[...] = jnp.cos(ang)
    sin_scr[...] = jnp.sin(ang)

    x = x_ref[...]
    shift1 = mod_ref[0, 0:1, :]
    scale1 = mod_ref[0, 1:2, :]
    ms = jnp.mean(x * x, axis=-1, keepdims=True)
    h = (x * lax.rsqrt(ms + NORM_EPS) * g1_ref[...]) * (1.0 + scale1) + shift1
    hb = h.astype(BF16)

    def proj(c0, c1):
        return jnp.dot(hb, w_ref[:, c0:c1], preferred_element_type=F32)

    tm = x.shape[0]
    cos4 = jnp.concatenate([cos_scr[...]] * DIFF_HEADS, axis=1)
    sin4 = jnp.concatenate([sin_scr[...]] * DIFF_HEADS, axis=1)
    lane = lax.broadcasted_iota(I32, (tm, DIFF_QK_W), 1)
    first_half = (lane % DIFF_DH) < ROPE_HALF
    bd = bd_ref[...]

    def norm_rope(t, gain_row):
        t2 = t * t
        hi = t2.astype(BF16)
        lo = (t2 - hi.astype(F32)).astype(BF16)
        gms = (jnp.dot(hi, bd, preferred_element_type=F32)
               + jnp.dot(lo, bd, preferred_element_type=F32))
        t = t * lax.rsqrt(gms + NORM_EPS) * gain_row
        nxt = pltpu.roll(t, DIFF_QK_W - ROPE_HALF, 1)
        prv = pltpu.roll(t, ROPE_HALF, 1)
        return t * cos4 + jnp.where(first_half, -nxt, prv) * sin4

    dq_raw = proj(1536, 2048)
    dk_raw = proj(2048, 2560)
    gqk_ref[...] = proj(0, 512).astype(BF16)
    dq = norm_rope(dq_raw, qg_ref[...]) * Q_SCALE
    gv_ref[...] = proj(512, 1024).astype(BF16)
    dk = norm_rope(dk_raw, kg_ref[...])
    dv = proj(2560, 3072)
    for hd in range(DIFF_HEADS):
        sl = slice(hd * LANES, (hd + 1) * LANES)
        qT_ref[0, hd] = dq[:, sl].T.astype(BF16)
        k_ref[0, hd, 0] = dk[:, sl].astype(BF16)
    gr_ref[...] = proj(1024, 1536).astype(BF16)
    for hd in range(DIFF_HEADS):
        sl = slice(hd * LANES, (hd + 1) * LANES)
        vT_ref[0, hd, 0] = dv[:, sl].T.astype(BF16)
    sa_ref[...] = _sigmoid(proj(3072, 4096)).astype(BF16)
    sb_ref[...] = _sigmoid(proj(4096, 5120)).astype(BF16)
    lr_ref[...] = proj(5120, 5248)


def _in_call(x2, mod, norm1_g, w_in_p, bd, qg_row, kg_row, pos_col, invf_row):
    nb = SEQ // TM_IN
    tok_spec = lambda w: pl.BlockSpec((TM_IN, w), lambda i: (i, 0))
    const2 = lambda r, c: pl.BlockSpec((r, c), lambda i: (0, 0))
    out_shapes = (
        jax.ShapeDtypeStruct((TOKENS, 2 * GLA_QK_W), BF16),
        jax.ShapeDtypeStruct((TOKENS, GLA_V_W), BF16),
        jax.ShapeDtypeStruct((TOKENS, GLA_V_W), BF16),
        jax.ShapeDtypeStruct((TOKENS, LR_PAD), F32),
        jax.ShapeDtypeStruct((BATCH, DIFF_HEADS, LANES, SEQ), BF16),
        jax.ShapeDtypeStruct((BATCH, DIFF_HEADS, N_KV, TQ, LANES), BF16),
        jax.ShapeDtypeStruct((BATCH, DIFF_HEADS, N_KV, LANES, TQ), BF16),
        jax.ShapeDtypeStruct((TOKENS, D_MODEL), BF16),
        jax.ShapeDtypeStruct((TOKENS, D_MODEL), BF16),
    )
    out_specs = (
        tok_spec(2 * GLA_QK_W), tok_spec(GLA_V_W), tok_spec(GLA_V_W), tok_spec(LR_PAD),
        pl.BlockSpec((1, DIFF_HEADS, LANES, TM_IN), lambda i: (i // nb, 0, 0, i % nb)),
        pl.BlockSpec((1, DIFF_HEADS, 1, TM_IN, LANES), lambda i: (i // nb, 0, i % nb, 0, 0)),
        pl.BlockSpec((1, DIFF_HEADS, 1, LANES, TM_IN), lambda i: (i // nb, 0, i % nb, 0, 0)),
        tok_spec(D_MODEL), tok_spec(D_MODEL),
    )
    return pl.pallas_call(
        _in_kernel,
        grid=(TOKENS // TM_IN,),
        in_specs=[
            tok_spec(D_MODEL),
            pl.BlockSpec((1, N_MOD, D_MODEL), lambda i: (i // nb, 0, 0)),
            const2(1, D_MODEL),
            pl.BlockSpec((D_MODEL, D_IN_PAD), lambda i: (0, 0), pipeline_mode=pl.Buffered(1)),
            const2(DIFF_QK_W, DIFF_QK_W),
            const2(1, DIFF_QK_W), const2(1, DIFF_QK_W),
            tok_spec(1),
            const2(1, LANES),
        ],
        out_specs=out_specs,
        out_shape=out_shapes,
        scratch_shapes=[pltpu.VMEM((TM_IN, LANES), F32), pltpu.VMEM((TM_IN, LANES), F32)],
        compiler_params=_cparams(("parallel",)),
        name="in_proj",
    )(x2, mod, norm1_g, w_in_p, bd, qg_row, kg_row, pos_col, invf_row)


def _gla_kernel(qk_ref, v_ref, r_ref, lr_ref, auh_ref, aul_ref, ab_ref, og_ref, o_ref,
                state_ref, oacc_ref, snew_ref):
    tt = qk_ref.shape[0]
    n_chunks = tt // GLA_CHUNK

    @pl.when(pl.program_id(1) == 0)
    def _():
        state_ref[...] = jnp.zeros_like(state_ref)

    lr = lr_ref[...]
    lr_hi = lr.astype(BF16)
    lr_lo = (lr - lr_hi.astype(F32)).astype(BF16)
    z = (jnp.dot(lr_hi, auh_ref[...], preferred_element_type=F32)
         + jnp.dot(lr_lo, auh_ref[...], preferred_element_type=F32)
         + jnp.dot(lr_hi, aul_ref[...], preferred_element_type=F32)) + ab_ref[...]
    g = (jnp.minimum(z, 0.0) - jnp.log(1.0 + jnp.exp(-jnp.abs(z)))) * (1.0 / GLA_GATE_TAU)

    row = lax.broadcasted_iota(I32, (tt, GLA_QK_W), 0) % GLA_CHUNK
    b = g
    step = 1
    while step < GLA_CHUNK:
        b = b + jnp.where(row >= step, pltpu.roll(b, step, 0), 0.0)
        step *= 2

    b_last_rows = [b[c * GLA_CHUNK + GLA_CHUNK - 1:(c + 1) * GLA_CHUNK, :] for c in range(n_chunks)]
    b_last = jnp.concatenate(
        [jnp.broadcast_to(bl, (GLA_CHUNK, GLA_QK_W)) for bl in b_last_rows], axis=0)

    qk = qk_ref[...].astype(F32)
    q = qk[:, :GLA_QK_W] * (GLA_DK ** -0.5)
    k = qk[:, GLA_QK_W:]
    q_in = (q * jnp.exp(b)).astype(BF16)
    k_in = (k * jnp.exp(-b)).astype(BF16)
    k_dec = (k * jnp.exp(b_last - b)).astype(BF16)

    ci = lax.broadcasted_iota(I32, (GLA_CHUNK, GLA_CHUNK), 0)
    cj = lax.broadcasted_iota(I32, (GLA_CHUNK, GLA_CHUNK), 1)
    causal = ci >= cj
    eye = ci == cj

    pairs = [(c, hd) for c in range(n_chunks) for hd in range(GLA_HEADS)]

    def rows(c):
        return slice(c * GLA_CHUNK, (c + 1) * GLA_CHUNK)

    def kcols(hd):
        return slice(hd * GLA_DK, (hd + 1) * GLA_DK)

    def vcols(hd):
        return slice(hd * GLA_DV, (hd + 1) * GLA_DV)

    att = {}
    for c, hd in pairs:
        a = lax.dot_general(q_in[rows(c), kcols(hd)], k_in[rows(c), kcols(hd)],
                            (((1,), (1,)), ((), ())), preferred_element_type=F32)
        att[c, hd] = jnp.where(causal, a, 0.0).astype(BF16)
    for c, hd in pairs:
        oacc_ref[rows(c), vcols(hd)] = jnp.dot(att[c, hd], v_ref[rows(c), vcols(hd)],
                                               preferred_element_type=F32)
    for c, hd in pairs:
        snew_ref[c * GLA_HEADS + hd] = lax.dot_general(
            k_dec[rows(c), kcols(hd)], v_ref[rows(c), vcols(hd)],
            (((0,), (0,)), ((), ())), preferred_element_type=F32)

    decay_rows = jnp.exp(jnp.concatenate(
        b_last_rows + [jnp.zeros((LANES - n_chunks, GLA_QK_W), F32)], axis=0))
    decay_cols = decay_rows.T
    states = [state_ref[hd] for hd in range(GLA_HEADS)]
    for c in range(n_chunks):
        for hd in range(GLA_HEADS):
            s_prev = states[hd]
            oacc_ref[rows(c), vcols(hd)] += jnp.dot(
                q_in[rows(c), kcols(hd)], s_prev.astype(BF16), preferred_element_type=F32)
            dcol = decay_cols[kcols(hd), c:c + 1]
            states[hd] = s_prev * dcol + snew_ref[c * GLA_HEADS + hd]
    for hd in range(GLA_HEADS):
        state_ref[hd] = states[hd]

    for hd in range(GLA_HEADS):
        vs = slice(hd * GLA_DV, (hd + 1) * GLA_DV)
        oh = oacc_ref[:, vs]
        ms = jnp.mean(oh * oh, axis=-1, keepdims=True)
        y = oh * lax.rsqrt(ms + NORM_EPS) * og_ref[...]
        r = r_ref[:, vs].astype(F32)
        o_ref[:, vs] = (y * (r * _sigmoid(r))).astype(BF16)


def _gla_call(gqk, gv, gr, lr, au_hi, au_lo, ab_row, og_row):
    nt = SEQ // TT_GLA
    tok = lambda w: pl.BlockSpec((TT_GLA, w), lambda b, t: (b * nt + t, 0))
    const2 = lambda r, c: pl.BlockSpec((r, c), lambda b, t: (0, 0))
    return pl.pallas_call(
        _gla_kernel,
        grid=(BATCH, nt),
        in_specs=[tok(2 * GLA_QK_W), tok(GLA_V_W), tok(GLA_V_W), tok(LR_PAD),
                  const2(LR_PAD, GLA_QK_W), const2(LR_PAD, GLA_QK_W), const2(1, GLA_QK_W),
                  const2(1, GLA_DV)],
        out_specs=tok(GLA_V_W),
        out_shape=jax.ShapeDtypeStruct((TOKENS, GLA_V_W), BF16),
        scratch_shapes=[pltpu.VMEM((GLA_HEADS, GLA_DK, GLA_DV), F32),
                        pltpu.VMEM((TT_GLA, GLA_V_W), F32),
                        pltpu.VMEM((TT_GLA // GLA_CHUNK * GLA_HEADS, GLA_DK, GLA_DV), F32)],
        compiler_params=_cparams(("parallel", "arbitrary")),
        name="gla",
    )(gqk, gv, gr, lr, au_hi, au_lo, ab_row, og_row)


def _attn_kernel(qT_ref, k_ref, vT_ref, lam_ref, og_ref, o_ref,
                 q_scr, s_scr, p_scr, acc_scr, m_scr, l_scr):
    i = pl.program_id(2)
    qT = qT_ref[0, 0]
    rowq = lax.broadcasted_iota(I32, qT.shape, 0)
    zero = jnp.zeros_like(qT)
    q_scr[0] = jnp.where(rowq < DIFF_DH, qT, zero)
    q_scr[1] = jnp.where(rowq >= DIFF_DH, qT, zero)
    m_scr[...] = jnp.full(m_scr.shape, NEG_INF, F32)
    l_scr[...] = jnp.zeros(l_scr.shape, F32)
    acc_scr[...] = jnp.zeros(acc_scr.shape, F32)
    n_sub = TQ // ATT_SUB

    def fold8(t, op):
        return op(t.reshape(t.shape[0] // SUBLANES, SUBLANES, TQ), axis=0)

    def scores(c, j):
        s_scr[c] = jnp.dot(k_ref[0, 0, j], q_scr[c], preferred_element_type=F32)

    def load_s(c, r, masked):
        s = s_scr[c, r * ATT_SUB:(r + 1) * ATT_SUB, :]
        if masked:
            key_i = lax.broadcasted_iota(I32, (ATT_SUB, TQ), 0) + r * ATT_SUB
            qry_i = lax.broadcasted_iota(I32, (ATT_SUB, TQ), 1)
            s = jnp.where(key_i <= qry_i, s, NEG_INF)
        return s

    def softmax_pv(c, j, masked):
        m8 = fold8(load_s(c, 0, masked), jnp.max)
        for r in range(1, n_sub):
            m8 = jnp.maximum(m8, fold8(load_s(c, r, masked), jnp.max))
        m_old = m_scr[c]
        m_new = jnp.maximum(m_old, jnp.max(m8, axis=0, keepdims=True))
        alpha = jnp.exp2(m_old - m_new)
        l8 = l_scr[c] * alpha
        for r in range(n_sub):
            p = jnp.exp2(load_s(c, r, masked) - m_new)
            l8 = l8 + fold8(p, jnp.sum)
            p_scr[c, r * ATT_SUB:(r + 1) * ATT_SUB, :] = p.astype(BF16)
        m_scr[c] = m_new
        l_scr[c] = l8
        acc_scr[c] = acc_scr[c] * alpha + jnp.dot(vT_ref[0, 0, j], p_scr[c],
                                                  preferred_element_type=F32)

    scores(0, 0)

    def body(j, carry):
        scores(1, j)
        softmax_pv(0, j, False)
        scores(0, j + 1)
        softmax_pv(1, j, False)
        return carry

    lax.fori_loop(0, i, body, 0)
    scores(1, i)
    softmax_pv(0, i, True)
    softmax_pv(1, i, True)

    l1 = jnp.sum(l_scr[0], axis=0, keepdims=True)
    l2 = jnp.sum(l_scr[1], axis=0, keepdims=True)

    lam_p = lam_ref[...]
    lam = (jnp.exp(jnp.sum(lam_p[0:1] * lam_p[1:2], axis=1, keepdims=True))
           - jnp.exp(jnp.sum(lam_p[2:3] * lam_p[3:4], axis=1, keepdims=True)) + LAMBDA_INIT)
    oT = acc_scr[0] / l1 - lam * (acc_scr[1] / l2)
    ms = jnp.mean(oT * oT, axis=0, keepdims=True)
    y = oT * lax.rsqrt(ms + NORM_EPS) * og_ref[...] * (1.0 - LAMBDA_INIT)
    o_ref[...] = y.T.astype(BF16)


def _attn_call(qT, kk, vT, lam_p, og_col):
    nq = SEQ // TQ
    return pl.pallas_call(
        _attn_kernel,
        grid=(BATCH, DIFF_HEADS, nq),
        in_specs=[
            pl.BlockSpec((1, 1, LANES, TQ), lambda b, h, i: (b, h, 0, i)),
            pl.BlockSpec((1, 1, N_KV, TQ, LANES), lambda b, h, i: (b, h, 0, 0, 0)),
            pl.BlockSpec((1, 1, N_KV, LANES, TQ), lambda b, h, i: (b, h, 0, 0, 0)),
            pl.BlockSpec((4, DIFF_DH), lambda b, h, i: (0, 0)),
            pl.BlockSpec((DIFF_DV, 1), lambda b, h, i: (0, 0)),
        ],
        out_specs=pl.BlockSpec((TQ, DIFF_DV), lambda b, h, i: (b * nq + i, h)),
        out_shape=jax.ShapeDtypeStruct((TOKENS, DIFF_V_W), BF16),
        scratch_shapes=[pltpu.VMEM((2, LANES, TQ), BF16),
                        pltpu.VMEM((2, TQ, TQ), F32),
                        pltpu.VMEM((2, TQ, TQ), BF16),
                        pltpu.VMEM((2, DIFF_DV, TQ), F32),
                        pltpu.VMEM((2, 1, TQ), F32),
                        pltpu.VMEM((2, SUBLANES, TQ), F32)],
        compiler_params=_cparams(("parallel", "parallel", "arbitrary")),
        name="attn",
    )(qT, kk, vT, lam_p, og_col)


def _merge_kernel(x_ref, oa_ref, ob_ref, sa_ref, sb_ref, mod_ref, g2_ref, wa_ref, wb_ref, wo_ref,
                  wrh_ref, wrl_ref, br_ref, x1_ref, h2_ref, rf_ref, ri_ref):
    tm = x_ref.shape[0]
    ma = jnp.dot(oa_ref[...], wa_ref[...], preferred_element_type=F32)
    mb = jnp.dot(ob_ref[...], wb_ref[...], preferred_element_type=F32)
    merged = sa_ref[...].astype(F32) * ma + sb_ref[...].astype(F32) * mb
    y = jnp.dot(merged.astype(BF16), wo_ref[...], preferred_element_type=F32)
    gate1 = mod_ref[0, 2:3, :]
    shift2 = mod_ref[0, 3:4, :]
    scale2 = mod_ref[0, 4:5, :]
    x1 = x_ref[...] + gate1 * y
    x1_ref[...] = x1
    ms = jnp.mean(x1 * x1, axis=-1, keepdims=True)
    h2 = (x1 * lax.rsqrt(ms + NORM_EPS) * g2_ref[...]) * (1.0 + scale2) + shift2
    _store_row_tiles(h2_ref, h2)

    h2_hi = h2.astype(BF16)
    h2_lo = (h2 - h2_hi.astype(F32)).astype(BF16)
    logits = (jnp.dot(h2_hi, wrh_ref[...], preferred_element_type=F32)
              + jnp.dot(h2_lo, wrh_ref[...], preferred_element_type=F32)
              + jnp.dot(h2_hi, wrl_ref[...], preferred_element_type=F32)) + br_ref[...]
    lane = lax.broadcasted_iota(I32, (tm, LANES), 1).astype(F32)
    ninf = -jnp.inf
    big = float(LANES)

    def first_argmax(v):
        vmax = jnp.max(v, axis=1, keepdims=True)
        idx = jnp.min(jnp.where(v == vmax, lane, big), axis=1, keepdims=True)
        return vmax, idx

    gl = jnp.where(lane < N_GROUPS, logits, ninf)
    gmax, gidx = first_argmax(gl)
    p_top = 1.0 / jnp.sum(jnp.exp(gl - gmax), axis=1, keepdims=True)
    lo = N_GROUPS + EXPERTS_PER_GROUP * gidx
    el = jnp.where((lane >= lo) & (lane < lo + EXPERTS_PER_GROUP), logits, ninf)
    e1max, e1 = first_argmax(el)
    e2max, e2 = first_argmax(jnp.where(lane == e1, ninf, el))
    t = jnp.exp(e2max - e1max)
    w1 = 1.0 / (1.0 + t)
    w2 = t / (1.0 + t)
    col = lax.broadcasted_iota(I32, (tm, ROUTE_W), 1)
    rf_ref[...] = jnp.where(col == 0, p_top * w1, jnp.where(col == 1, p_top * w2, 0.0))
    ids = jnp.where(col == 0, e1 - N_GROUPS, jnp.where(col == 1, e2 - N_GROUPS, 0.0))
    ri_ref[...] = ids.astype(I32)


def _merge_call(x2, o_a, o_b, sa, sb, mod, norm2_g, wa, wb, wo, wr_hi, wr_lo, br):
    nb = SEQ // TM_MERGE
    tok = lambda w: pl.BlockSpec((TM_MERGE, w), lambda i: (i, 0))
    const2 = lambda r, c: pl.BlockSpec((r, c), lambda i: (0, 0))
    return pl.pallas_call(
        _merge_kernel,
        grid=(TOKENS // TM_MERGE,),
        in_specs=[tok(D_MODEL), tok(GLA_V_W), tok(DIFF_V_W), tok(D_MODEL), tok(D_MODEL),
                  pl.BlockSpec((1, N_MOD, D_MODEL), lambda i: (i // nb, 0, 0)),
                  const2(1, D_MODEL),
                  const2(GLA_V_W, D_MODEL), const2(DIFF_V_W, D_MODEL), const2(D_MODEL, D_MODEL),
                  const2(D_MODEL, LANES), const2(D_MODEL, LANES), const2(1, LANES)],
        out_specs=(tok(D_MODEL), _row_tile_spec(TM_MERGE, lambda i: (i, 0)),
                   tok(ROUTE_W), tok(ROUTE_W)),
        out_shape=(jax.ShapeDtypeStruct((TOKENS, D_MODEL), F32),
                   jax.ShapeDtypeStruct((TOKENS * ROW_TILE_S, LANES), F32),
                   jax.ShapeDtypeStruct((TOKENS, ROUTE_W), F32),
                   jax.ShapeDtypeStruct((TOKENS, ROUTE_W), I32)),
        compiler_params=_cparams(("parallel",)),
        name="merge",
    )(x2, o_a, o_b, sa, sb, mod, norm2_g, wa, wb, wo, wr_hi, wr_lo, br)


def _rank_kernel(ri_ref, rank_ref, cnt_ref, carry_ref):
    tm = ri_ref.shape[0]

    @pl.when(pl.program_id(0) == 0)
    def _():
        carry_ref[...] = jnp.zeros_like(carry_ref)

    ri = ri_ref[...]
    e1 = ri[:, 0:1]
    e2 = ri[:, 1:2]
    lane = lax.broadcasted_iota(I32, (tm, LANES), 1)
    hit1 = lane == e1
    hit2 = lane == e2
    onehot = jnp.where(hit1 | hit2, 1.0, 0.0)
    ti = lax.broadcasted_iota(I32, (tm, tm), 0)
    tj = lax.broadcasted_iota(I32, (tm, tm), 1)
    strict_lower = jnp.where(ti > tj, 1.0, 0.0).astype(BF16)
    before = jnp.dot(strict_lower, onehot.astype(BF16), preferred_element_type=F32) + carry_ref[...]
    r1 = jnp.sum(jnp.where(hit1, before, 0.0), axis=1, keepdims=True)
    r2 = jnp.sum(jnp.where(hit2, before, 0.0), axis=1, keepdims=True)
    carry_ref[...] = carry_ref[...] + jnp.sum(onehot, axis=0, keepdims=True)
    cnt_ref[...] = carry_ref[...]

    eye = ti == tj

    def to_row(colv):
        return jnp.sum(jnp.where(eye, colv, 0.0), axis=0, keepdims=True)

    rows = [to_row(e1.astype(F32)), to_row(e2.astype(F32)), to_row(r1), to_row(r2),
            jnp.zeros((ROUTE_W - 4, tm), F32)]
    rank_ref[0] = jnp.concatenate(rows, axis=0).astype(I32)


def _rank_call(ri):
    return pl.pallas_call(
        _rank_kernel,
        grid=(TOKENS // TM_RANK,),
        in_specs=[pl.BlockSpec((TM_RANK, ROUTE_W), lambda i: (i, 0))],
        out_specs=(pl.BlockSpec((1, ROUTE_W, TM_RANK), lambda i: (i, 0, 0)),
                   pl.BlockSpec((1, LANES), lambda i: (0, 0))),
        out_shape=(jax.ShapeDtypeStruct((TOKENS // TM_RANK, ROUTE_W, TM_RANK), I32),
                   jax.ShapeDtypeStruct((1, LANES), F32)),
        scratch_shapes=[pltpu.VMEM((1, LANES), F32)],
        compiler_params=_cparams(("arbitrary",)),
        name="rank",
    )(ri)


def _plan_kernel(rk_ref, cnt_ref, dest_ref, blk_ref):
    n_tiles, _, tm = rk_ref.shape
    cnt = cnt_ref[...]
    padded = jnp.floor((cnt + (MOE_BLK - 1)) * (1.0 / MOE_BLK)) * MOE_BLK
    lane = lax.broadcasted_iota(I32, (1, LANES), 1)
    seg_end = padded
    step = 1
    while step < N_EXPERTS:
        seg_end = seg_end + jnp.where(lane >= step, pltpu.roll(seg_end, step, 1), 0.0)
        step *= 2
    seg_start = seg_end - padded
    valid_end = seg_start + cnt

    ei = lax.broadcasted_iota(I32, (LANES, LANES), 0)
    ej = lax.broadcasted_iota(I32, (LANES, LANES), 1)

    def to_col(rowv):
        return jnp.sum(jnp.where(ei == ej, rowv, 0.0), axis=1, keepdims=True)

    start_col, end_col, valid_col = to_col(seg_start), to_col(seg_end), to_col(valid_end)

    nb_pad = blk_ref.shape[1]
    e_sub = lax.broadcasted_iota(I32, (LANES, nb_pad), 0)
    b_start = (lax.broadcasted_iota(I32, (1, nb_pad), 1) * MOE_BLK).astype(F32)
    ends_before = jnp.where((e_sub < N_EXPERTS) & (end_col <= b_start), 1.0, 0.0)
    block_e = jnp.minimum(jnp.sum(ends_before, axis=0, keepdims=True), N_EXPERTS - 1.0)
    block_valid_end = jnp.sum(jnp.where(e_sub.astype(F32) == block_e, valid_col, 0.0),
                              axis=0, keepdims=True)
    n_valid = jnp.clip(block_valid_end - b_start, 0.0, float(MOE_BLK))
    n_used = jnp.max(seg_end, axis=1, keepdims=True) * (1.0 / MOE_BLK)
    blk_ref[...] = jnp.concatenate(
        [block_e, n_valid, jnp.broadcast_to(n_used, (1, nb_pad)),
         jnp.zeros((ROUTE_W - 3, nb_pad), F32)], axis=0).astype(I32)

    e_tok = lax.broadcasted_iota(I32, (LANES, tm), 0)

    def tile(t, carry):
        rk = rk_ref[t]
        d = [jnp.sum(jnp.where(e_tok == rk[k:k + 1], start_col, 0.0), axis=0, keepdims=True)
             + rk[k + 2:k + 3].astype(F32) for k in range(TOP_K)]
        dest_ref[t] = jnp.concatenate(d + [jnp.zeros((ROUTE_W - TOP_K, tm), F32)], axis=0).astype(I32)
        return carry

    lax.fori_loop(0, n_tiles, tile, 0)


def _plan_call(rank_rows, cnt):
    n_tiles = TOKENS // TM_RANK
    nb_pad = -(-N_BLOCKS // LANES) * LANES
    dest, blk = pl.pallas_call(
        _plan_kernel,
        out_shape=(jax.ShapeDtypeStruct((n_tiles, ROUTE_W, TM_RANK), I32),
                   jax.ShapeDtypeStruct((ROUTE_W, nb_pad), I32)),
        compiler_params=pltpu.CompilerParams(vmem_limit_bytes=VMEM_LIMIT),
        name="plan",
    )(rank_rows, cnt)
    dest1 = dest[:, 0, :].reshape(TOKENS)
    dest2 = dest[:, 1, :].reshape(TOKENS)
    return dest1, dest2, blk[0, :N_BLOCKS], blk[1, :N_BLOCKS], blk[2, :1]


def _sc_params():
    return pltpu.CompilerParams(use_tc_tiling_on_sc=True)


def _sc_mesh():
    return plsc.VectorSubcoreMesh(core_axis_name="core", subcore_axis_name="subcore")


def _sc_worker_base(per_worker):
    wid = lax.axis_index("subcore") * SC_CORES + lax.axis_index("core")
    return wid * per_worker


def _sc_scatter_rows(src, dest1, dest2, n_out):
    n = src.shape[0]
    per_worker = n // SC_WORKERS
    assert per_worker * SC_WORKERS == n and per_worker % SC_CHUNK == 0

    def body(src_hbm, d1_hbm, d2_hbm, out_hbm, idx_v, rows_v):
        base = _sc_worker_base(per_worker)

        @pl.loop(0, per_worker // SC_CHUNK)
        def _(j):
            start = pl.multiple_of(base + j * SC_CHUNK, SC_CHUNK)
            pltpu.sync_copy(src_hbm.at[pl.ds(start, SC_CHUNK)], rows_v)
            for d_hbm in (d1_hbm, d2_hbm):
                pltpu.sync_copy(d_hbm.at[pl.ds(start, SC_CHUNK)], idx_v)
                pltpu.sync_copy(rows_v, out_hbm.at[idx_v])

    return pl.kernel(
        body,
        out_type=jax.ShapeDtypeStruct((n_out, ROW_TILE_S, LANES), F32),
        mesh=_sc_mesh(),
        scratch_types=[pltpu.VMEM((SC_CHUNK,), I32),
                       pltpu.VMEM((SC_CHUNK, ROW_TILE_S, LANES), F32)],
        compiler_params=_sc_params(),
        name="sc_dispatch",
    )(src, dest1, dest2)


def _sc_gather_rows(table, idx):
    n = idx.shape[0]
    per_worker = n // SC_WORKERS
    half = SC_CHUNK // 2
    assert per_worker * SC_WORKERS == n and per_worker % SC_CHUNK == 0

    def body(table_hbm, idx_hbm, out_hbm, idx_v, rows_a, rows_b, gsem_a, gsem_b, wsem_a, wsem_b):
        base = pl.multiple_of(_sc_worker_base(per_worker), SC_CHUNK)
        pltpu.sync_copy(idx_hbm.at[pl.ds(base, per_worker)], idx_v)

        @pl.loop(0, per_worker // SC_CHUNK)
        def _(j):
            off_a = pl.multiple_of(j * SC_CHUNK, SC_CHUNK)
            off_b = pl.multiple_of(j * SC_CHUNK + half, half)
            ga = pltpu.async_copy(table_hbm.at[idx_v.at[pl.ds(off_a, half)]], rows_a, gsem_a)
            gb = pltpu.async_copy(table_hbm.at[idx_v.at[pl.ds(off_b, half)]], rows_b, gsem_b)
            ga.wait()
            wa = pltpu.async_copy(rows_a, out_hbm.at[pl.ds(base + off_a, half)], wsem_a)
            gb.wait()
            wb = pltpu.async_copy(rows_b, out_hbm.at[pl.ds(base + off_b, half)], wsem_b)
            wa.wait()
            wb.wait()

    return pl.kernel(
        body,
        out_type=jax.ShapeDtypeStruct((n, ROW_TILE_S, LANES), F32),
        mesh=_sc_mesh(),
        scratch_types=[pltpu.VMEM((per_worker,), I32),
                       pltpu.VMEM((half, ROW_TILE_S, LANES), F32),
                       pltpu.VMEM((half, ROW_TILE_S, LANES), F32),
                       pltpu.SemaphoreType.DMA, pltpu.SemaphoreType.DMA,
                       pltpu.SemaphoreType.DMA, pltpu.SemaphoreType.DMA],
        compiler_params=_sc_params(),
        name="sc_gather",
    )(table, idx)


def _expert_kernel(be_ref, nv_ref, nu_ref, xs_ref, w1_ref, w3_ref, w2_ref, y_ref, w1b, w3b, w2b):
    i = pl.program_id(0)
    e = be_ref[i]
    prev = be_ref[jnp.maximum(i - 1, 0)]
    used = i < nu_ref[0]

    @pl.when(used & ((i == 0) | (e != prev)))
    def _():
        w1b[...] = w1_ref[0].astype(BF16)
        w3b[...] = w3_ref[0].astype(BF16)
        w2b[...] = w2_ref[0].astype(BF16)

    @pl.when(used)
    def _():
        row = lax.broadcasted_iota(I32, (MOE_BLK, D_MODEL), 0)
        xb = jnp.where(row < nv_ref[i], _load_row_tiles(xs_ref), 0.0).astype(BF16)
        a = jnp.dot(xb, w1b[...], preferred_element_type=F32)
        g = jnp.dot(xb, w3b[...], preferred_element_type=F32)
        hmid = (a * _sigmoid(a)) * g
        _store_row_tiles(y_ref, jnp.dot(hmid.astype(BF16), w2b[...], preferred_element_type=F32))

    @pl.when(jnp.logical_not(used))
    def _():
        y_ref[...] = jnp.zeros_like(y_ref)


def _expert_call(block_e, n_valid, n_used, xs, w1, w3, w2):
    grid_spec = pltpu.PrefetchScalarGridSpec(
        num_scalar_prefetch=3,
        grid=(N_BLOCKS,),
        in_specs=[_row_tile_spec(MOE_BLK, lambda i, be, nv, nu: (jnp.minimum(i, nu[0] - 1), 0)),
                  pl.BlockSpec((1, D_MODEL, D_EXPERT), lambda i, be, nv, nu: (be[i], 0, 0)),
                  pl.BlockSpec((1, D_MODEL, D_EXPERT), lambda i, be, nv, nu: (be[i], 0, 0)),
                  pl.BlockSpec((1, D_EXPERT, D_MODEL), lambda i, be, nv, nu: (be[i], 0, 0))],
        out_specs=_row_tile_spec(MOE_BLK, lambda i, be, nv, nu: (i, 0)),
        scratch_shapes=[pltpu.VMEM((D_MODEL, D_EXPERT), BF16),
                        pltpu.VMEM((D_MODEL, D_EXPERT), BF16),
                        pltpu.VMEM((D_EXPERT, D_MODEL), BF16)],
    )
    return pl.pallas_call(
        _expert_kernel,
        grid_spec=grid_spec,
        out_shape=jax.ShapeDtypeStruct((N_BLOCKS * MOE_BLK * ROW_TILE_S, LANES), F32),
        compiler_params=_cparams(("arbitrary",)),
        name="experts",
    )(block_e, n_valid, n_used, xs, w1, w3, w2)


def _combine_rows_kernel(x1_ref, rf_ref, mod_ref, ya_ref, yb_ref, o_ref):
    rf = rf_ref[...]
    gate2 = mod_ref[0, 5:6, :]
    moe = rf[:, 0:1] * _load_row_tiles(ya_ref) + rf[:, 1:2] * _load_row_tiles(yb_ref)
    o_ref[...] = x1_ref[...] + gate2 * moe


def _combine_rows_call(x1, rf, mod, yg):
    nb = SEQ // TM_ROW
    n_tiles = TOKENS // TM_ROW
    return pl.pallas_call(
        _combine_rows_kernel,
        grid=(n_tiles,),
        in_specs=[pl.BlockSpec((TM_ROW, D_MODEL), lambda i: (i, 0)),
                  pl.BlockSpec((TM_ROW, ROUTE_W), lambda i: (i, 0)),
                  pl.BlockSpec((1, N_MOD, D_MODEL), lambda i: (i // nb, 0, 0)),
                  _row_tile_spec(TM_ROW, lambda i: (i, 0)),
                  _row_tile_spec(TM_ROW, lambda i: (i + n_tiles, 0))],
        out_specs=pl.BlockSpec((TM_ROW, D_MODEL), lambda i: (i, 0)),
        out_shape=jax.ShapeDtypeStruct((TOKENS, D_MODEL), F32),
        compiler_params=_cparams(("parallel",)),
        name="combine",
    )(x1, rf, mod, yg, yg)


def kernel(x, c, positions, ada_w, ada_b, norm1_g, norm2_g, w_in, gla_alpha_up, gla_alpha_b,
           gla_out_g, diff_q_g, diff_k_g, diff_lq1, diff_lk1, diff_lq2, diff_lk2, diff_out_g,
           w_branch_a, w_branch_b, w_out, router_group_w, router_group_b, router_expert_w,
           router_expert_b, expert_w1, expert_w3, expert_w2):
    assert x.shape == (BATCH, SEQ, D_MODEL) and ada_w.shape[0] == 1
    x2 = x.reshape(TOKENS, D_MODEL)

    w_in_p = _wprep_call(w_in[0])
    au_pad = jnp.zeros((LR_PAD, GLA_QK_W), F32).at[:GLA_GATE_RANK].set(gla_alpha_up[0])
    au_hi = au_pad.astype(BF16)
    au_lo = (au_pad - au_hi.astype(F32)).astype(BF16)
    gid = jnp.arange(DIFF_QK_W) // DIFF_DH
    bd = jnp.where(gid[:, None] == gid[None, :], 1.0 / DIFF_DH, 0.0).astype(BF16)
    qg_row = jnp.tile(diff_q_g[0], DIFF_QK_W // DIFF_DH).reshape(1, DIFF_QK_W)
    kg_row = jnp.tile(diff_k_g[0], DIFF_QK_W // DIFF_DH).reshape(1, DIFF_QK_W)
    inv_freq = ROPE_THETA ** (-jnp.arange(ROPE_HALF, dtype=F32) / ROPE_HALF)
    invf64 = jnp.concatenate([inv_freq, inv_freq, jnp.zeros((DIFF_DH - ROPE_DIM,), F32)])
    invf_row = jnp.tile(invf64, LANES // DIFF_DH).reshape(1, LANES)
    pos_col = positions.reshape(TOKENS, 1)
    lam_p = jnp.concatenate([diff_lq1, diff_lk1, diff_lq2, diff_lk2], axis=0)
    wr = (jnp.zeros((D_MODEL, LANES), F32)
          .at[:, :N_GROUPS].set(router_group_w[0])
          .at[:, N_GROUPS:N_GROUPS + N_EXPERTS].set(router_expert_w[0]))
    wr_hi = wr.astype(BF16)
    wr_lo = (wr - wr_hi.astype(F32)).astype(BF16)
    br = (jnp.zeros((1, LANES), F32)
          .at[0, :N_GROUPS].set(router_group_b[0])
          .at[0, N_GROUPS:N_GROUPS + N_EXPERTS].set(router_expert_b[0]))

    mod = _mod_call(c, ada_w[0], ada_b[0])
    gqk, gv, gr, lr, qT, kk, vT, sa, sb = _in_call(
        x2, mod, norm1_g, w_in_p, bd, qg_row, kg_row, pos_col, invf_row)
    o_a = _gla_call(gqk, gv, gr, lr, au_hi, au_lo, gla_alpha_b, gla_out_g)
    o_b = _attn_call(qT, kk, vT, lam_p, diff_out_g.reshape(DIFF_DV, 1))
    x1, h2, rf, ri = _merge_call(
        x2, o_a, o_b, sa, sb, mod, norm2_g, w_branch_a[0].astype(BF16),
        w_branch_b[0].astype(BF16), w_out[0].astype(BF16), wr_hi, wr_lo, br)
    rank, cnt = _rank_call(ri)

    dest1, dest2, block_e, n_valid, n_used = _plan_call(rank, cnt)

    xs = _sc_scatter_rows(_as_row_tiles(h2), dest1, dest2, N_BLOCKS * MOE_BLK)
    y = _expert_call(block_e, n_valid, n_used, _as_2d(xs), expert_w1[0], expert_w3[0], expert_w2[0])
    yg = _sc_gather_rows(_as_row_tiles(y), jnp.concatenate([dest1, dest2]))
    out = _combine_rows_call(x1, rf, mod, _as_2d(yg))
    return out.reshape(BATCH, SEQ, D_MODEL)
```

```python
import math

import jax
import jax.numpy as jnp
from jax import lax
from jax.experimental import pallas as pl
from jax.experimental.pallas import tpu as pltpu
from jax.experimental.pallas import tpu_sc as plsc

F32 = jnp.float32
BF16 = jnp.bfloat16
I32 = jnp.int32

D_MODEL = 1024
BATCH = 4
SEQ = 4096
TOKENS = BATCH * SEQ
N_MOD = 6
NORM_EPS = 1e-6

GLA_HEADS = 4
GLA_DK = 64
GLA_DV = 128
GLA_GATE_RANK = 16
GLA_GATE_TAU = 16.0
GLA_CHUNK = 64
GLA_QK_W = GLA_HEADS * GLA_DK
GLA_V_W = GLA_HEADS * GLA_DV

DIFF_HEADS = 4
DIFF_DH = 64
DIFF_DV = 2 * DIFF_DH
DIFF_QK_W = DIFF_HEADS * 2 * DIFF_DH
DIFF_V_W = DIFF_HEADS * DIFF_DV
ROPE_THETA = 500000.0
ROPE_DIM = DIFF_DH // 4
ROPE_HALF = ROPE_DIM // 2
NEG_INF = -1e30
LAMBDA_INIT = 0.8 - 0.6 * 1.0

N_GROUPS = 4
EXPERTS_PER_GROUP = 8
N_EXPERTS = N_GROUPS * EXPERTS_PER_GROUP
TOP_K = 2
D_EXPERT = 512

LANES = 128
SUBLANES = 8
ROW_TILE_S = D_MODEL // LANES
SC_CORES = 2
SC_SUBCORES = 16
SC_WORKERS = SC_CORES * SC_SUBCORES
SC_CHUNK = 64
LR_PAD = LANES
D_IN_PAD = 2 * GLA_QK_W + 2 * GLA_V_W + 2 * DIFF_QK_W + DIFF_V_W + 2 * D_MODEL + LR_PAD

TM_IN = 512
TQ = 512
V_ROWS = DIFF_DV + 16
ATT_SUB = 32
Q_SCALE = DIFF_DH ** -0.5 * math.log2(math.e)
N_KV = SEQ // TQ
TT_GLA = 512
TM_MERGE = 512
TM_RANK = 512
TM_ROW = 256
MOE_BLK = 256
N_BLOCKS = (TOKENS * TOP_K + N_EXPERTS * (MOE_BLK - 1) + MOE_BLK - 1) // MOE_BLK
ROUTE_W = 8

VMEM_LIMIT = 56 * 1024 * 1024


def _cparams(sem):
    return pltpu.CompilerParams(dimension_semantics=sem, vmem_limit_bytes=VMEM_LIMIT)


def _sigmoid(x):
    return 1.0 / (1.0 + jnp.exp(-x))


def _row_tile_spec(rows, index_map):
    return pl.BlockSpec((rows * ROW_TILE_S, LANES), index_map)


def _as_row_tiles(a2d):
    return a2d.reshape(a2d.shape[0] // ROW_TILE_S, ROW_TILE_S, LANES)


def _as_2d(a3d):
    return a3d.reshape(a3d.shape[0] * ROW_TILE_S, LANES)


def _store_row_tiles(ref, val):
    for s in range(ROW_TILE_S):
        ref[pl.ds(s, val.shape[0], stride=ROW_TILE_S), :] = val[:, s * LANES:(s + 1) * LANES]


def _load_row_tiles(ref):
    rows = ref.shape[0] // ROW_TILE_S
    return jnp.concatenate(
        [ref[pl.ds(s, rows, stride=ROW_TILE_S), :] for s in range(ROW_TILE_S)], axis=1)


def _mod_kernel(ct_ref, w_ref, b_ref, o_ref):
    ct = ct_ref[...]
    ca = ct * _sigmoid(ct)
    w = w_ref[...]
    rows = []
    for b in range(BATCH):
        rows.append(jnp.sum(w * ca[:, b:b + 1], axis=0, keepdims=True) + b_ref[...])
    rows.append(jnp.zeros((8 - BATCH, w.shape[1]), F32))
    o_ref[...] = jnp.concatenate(rows, axis=0)


def _mod_call(c, ada_w, ada_b):
    tn = D_MODEL
    ct = jnp.zeros((D_MODEL, 8), F32).at[:, :BATCH].set(c.T)
    out = pl.pallas_call(
        _mod_kernel,
        grid=(N_MOD,),
        in_specs=[
            pl.BlockSpec((D_MODEL, 8), lambda j: (0, 0)),
            pl.BlockSpec((D_MODEL, tn), lambda j: (0, j)),
            pl.BlockSpec((1, tn), lambda j: (0, j)),
        ],
        out_specs=pl.BlockSpec((8, tn), lambda j: (0, j)),
        out_shape=jax.ShapeDtypeStruct((8, N_MOD * D_MODEL), F32),
        compiler_params=_cparams(("arbitrary",)),
        name="mod",
    )(ct, ada_w, ada_b.reshape(1, N_MOD * D_MODEL))
    return out[:BATCH].reshape(BATCH, N_MOD, D_MODEL)


def _wprep_kernel(w_ref, o_ref):
    w = w_ref[...]
    lr0 = 2 * GLA_QK_W + 2 * GLA_V_W
    rest = D_IN_PAD - LR_PAD - lr0
    o_ref[:, :lr0] = w[:, :lr0].astype(BF16)
    o_ref[:, lr0:lr0 + rest] = w[:, lr0 + GLA_GATE_RANK:].astype(BF16)
    lr = jnp.concatenate([w[:, lr0:lr0 + GLA_GATE_RANK],
                          jnp.zeros((w.shape[0], LR_PAD - GLA_GATE_RANK), F32)], axis=1)
    o_ref[:, lr0 + rest:] = lr.astype(BF16)


def _wprep_call(w_in0):
    d_in = w_in0.shape[1]
    rows = 128
    return pl.pallas_call(
        _wprep_kernel,
        grid=(D_MODEL // rows,),
        in_specs=[pl.BlockSpec((rows, d_in), lambda i: (i, 0))],
        out_specs=pl.BlockSpec((rows, D_IN_PAD), lambda i: (i, 0)),
        out_shape=jax.ShapeDtypeStruct((D_MODEL, D_IN_PAD), BF16),
        compiler_params=_cparams(("parallel",)),
        name="w_prep",
    )(w_in0)


def _in_kernel(x_ref, mod_ref, g1_ref, w_ref, bd_ref, qg_ref, kg_ref, pos_ref, invf_ref,
               gqk_ref, gv_ref, gr_ref, lr_ref, qT_ref, k_ref, vT_ref, sa_ref, sb_ref,
               cos_scr, sin_scr):
    ang = pos_ref[...].astype(F32) * invf_ref[...]
    cos_scr[...] = jnp.cos(ang)
    sin_scr[...] = jnp.sin(ang)

    x = x_ref[...]
    shift1 = mod_ref[0, 0:1, :]
    scale1 = mod_ref[0, 1:2, :]
    ms = jnp.mean(x * x, axis=-1, keepdims=True)
    h = (x * lax.rsqrt(ms + NORM_EPS) * g1_ref[...]) * (1.0 + scale1) + shift1
    hb = h.astype(BF16)

    def proj(c0, c1):
        return jnp.dot(hb, w_ref[:, c0:c1], preferred_element_type=F32)

    tm = x.shape[0]
    cos4 = jnp.concatenate([cos_scr[...]] * DIFF_HEADS, axis=1)
    sin4 = jnp.concatenate([sin_scr[...]] * DIFF_HEADS, axis=1)
    lane = lax.broadcasted_iota(I32, (tm, DIFF_QK_W), 1)
    first_half = (lane % DIFF_DH) < ROPE_HALF
    bd = bd_ref[...]

    def norm_rope(t, gain_row):
        t2 = t * t
        hi = t2.astype(BF16)
        lo = (t2 - hi.astype(F32)).astype(BF16)
        gms = (jnp.dot(hi, bd, preferred_element_type=F32)
               + jnp.dot(lo, bd, preferred_element_type=F32))
        t = t * lax.rsqrt(gms + NORM_EPS) * gain_row
        nxt = pltpu.roll(t, DIFF_QK_W - ROPE_HALF, 1)
        prv = pltpu.roll(t, ROPE_HALF, 1)
        return t * cos4 + jnp.where(first_half, -nxt, prv) * sin4

    dq_raw = proj(1536, 2048)
    dk_raw = proj(2048, 2560)
    gqk_ref[...] = proj(0, 512).astype(BF16)
    dq = norm_rope(dq_raw, qg_ref[...]) * Q_SCALE
    gv_ref[...] = proj(512, 1024).astype(BF16)
    dk = norm_rope(dk_raw, kg_ref[...])
    dv = proj(2560, 3072)
    for hd in range(DIFF_HEADS):
        sl = slice(hd * LANES, (hd + 1) * LANES)
        qT_ref[0, hd, 0] = dq[:, sl].T.astype(BF16)
        k_ref[0, hd, 0] = dk[:, sl].astype(BF16)
    gr_ref[...] = proj(1024, 1536).astype(BF16)
    for hd in range(DIFF_HEADS):
        sl = slice(hd * LANES, (hd + 1) * LANES)
        vT_ref[0, hd, 0] = jnp.concatenate(
            [dv[:, sl].T, jnp.ones((V_ROWS - DIFF_DV, tm), F32)], axis=0).astype(BF16)
    sa_ref[...] = _sigmoid(proj(3072, 4096)).astype(BF16)
    sb_ref[...] = _sigmoid(proj(4096, 5120)).astype(BF16)
    lr_ref[...] = proj(5120, 5248)


def _in_call(x2, mod, norm1_g, w_in_p, bd, qg_row, kg_row, pos_col, invf_row):
    nb = SEQ // TM_IN
    tok_spec = lambda w: pl.BlockSpec((TM_IN, w), lambda i: (i, 0))
    const2 = lambda r, c: pl.BlockSpec((r, c), lambda i: (0, 0))
    out_shapes = (
        jax.ShapeDtypeStruct((TOKENS, 2 * GLA_QK_W), BF16),
        jax.ShapeDtypeStruct((TOKENS, GLA_V_W), BF16),
        jax.ShapeDtypeStruct((TOKENS, GLA_V_W), BF16),
        jax.ShapeDtypeStruct((TOKENS, LR_PAD), F32),
        jax.ShapeDtypeStruct((BATCH, DIFF_HEADS, N_KV, LANES, TQ), BF16),
        jax.ShapeDtypeStruct((BATCH, DIFF_HEADS, N_KV, TQ, LANES), BF16),
        jax.ShapeDtypeStruct((BATCH, DIFF_HEADS, N_KV, V_ROWS, TQ), BF16),
        jax.ShapeDtypeStruct((TOKENS, D_MODEL), BF16),
        jax.ShapeDtypeStruct((TOKENS, D_MODEL), BF16),
    )
    out_specs = (
        tok_spec(2 * GLA_QK_W), tok_spec(GLA_V_W), tok_spec(GLA_V_W), tok_spec(LR_PAD),
        pl.BlockSpec((1, DIFF_HEADS, 1, LANES, TM_IN), lambda i: (i // nb, 0, i % nb, 0, 0)),
        pl.BlockSpec((1, DIFF_HEADS, 1, TM_IN, LANES), lambda i: (i // nb, 0, i % nb, 0, 0)),
        pl.BlockSpec((1, DIFF_HEADS, 1, V_ROWS, TM_IN), lambda i: (i // nb, 0, i % nb, 0, 0)),
        tok_spec(D_MODEL), tok_spec(D_MODEL),
    )
    return pl.pallas_call(
        _in_kernel,
        grid=(TOKENS // TM_IN,),
        in_specs=[
            tok_spec(D_MODEL),
            pl.BlockSpec((1, N_MOD, D_MODEL), lambda i: (i // nb, 0, 0)),
            const2(1, D_MODEL),
            pl.BlockSpec((D_MODEL, D_IN_PAD), lambda i: (0, 0), pipeline_mode=pl.Buffered(1)),
            const2(DIFF_QK_W, DIFF_QK_W),
            const2(1, DIFF_QK_W), const2(1, DIFF_QK_W),
            tok_spec(1),
            const2(1, LANES),
        ],
        out_specs=out_specs,
        out_shape=out_shapes,
        scratch_shapes=[pltpu.VMEM((TM_IN, LANES), F32), pltpu.VMEM((TM_IN, LANES), F32)],
        compiler_params=_cparams(("parallel",)),
        name="in_proj",
    )(x2, mod, norm1_g, w_in_p, bd, qg_row, kg_row, pos_col, invf_row)


def _gla_kernel(qk_ref, v_ref, r_ref, lr_ref, auh_ref, aul_ref, ab_ref, og_ref, o_ref,
                state_ref, oacc_ref, snew_ref):
    tt = qk_ref.shape[0]
    n_chunks = tt // GLA_CHUNK

    @pl.when(pl.program_id(1) == 0)
    def _():
        state_ref[...] = jnp.zeros_like(state_ref)

    lr = lr_ref[...]
    lr_hi = lr.astype(BF16)
    lr_lo = (lr - lr_hi.astype(F32)).astype(BF16)
    z = (jnp.dot(lr_hi, auh_ref[...], preferred_element_type=F32)
         + jnp.dot(lr_lo, auh_ref[...], preferred_element_type=F32)
         + jnp.dot(lr_hi, aul_ref[...], preferred_element_type=F32)) + ab_ref[...]
    g = (jnp.minimum(z, 0.0) - jnp.log(1.0 + jnp.exp(-jnp.abs(z)))) * (1.0 / GLA_GATE_TAU)

    row = lax.broadcasted_iota(I32, (tt, GLA_QK_W), 0) % GLA_CHUNK
    b = g
    step = 1
    while step < GLA_CHUNK:
        b = b + jnp.where(row >= step, pltpu.roll(b, step, 0), 0.0)
        step *= 2

    b_last_rows = [b[c * GLA_CHUNK + GLA_CHUNK - 1:(c + 1) * GLA_CHUNK, :] for c in range(n_chunks)]
    b_last = jnp.concatenate(
        [jnp.broadcast_to(bl, (GLA_CHUNK, GLA_QK_W)) for bl in b_last_rows], axis=0)

    qk = qk_ref[...].astype(F32)
    q = qk[:, :GLA_QK_W] * (GLA_DK ** -0.5)
    k = qk[:, GLA_QK_W:]
    q_in = (q * jnp.exp(b)).astype(BF16)
    k_in = (k * jnp.exp(-b)).astype(BF16)
    k_dec = (k * jnp.exp(b_last - b)).astype(BF16)

    ci = lax.broadcasted_iota(I32, (GLA_CHUNK, GLA_CHUNK), 0)
    cj = lax.broadcasted_iota(I32, (GLA_CHUNK, GLA_CHUNK), 1)
    causal = ci >= cj

    pairs = [(c, hd) for c in range(n_chunks) for hd in range(GLA_HEADS)]

    def rows(c):
        return slice(c * GLA_CHUNK, (c + 1) * GLA_CHUNK)

    def kcols(hd):
        return slice(hd * GLA_DK, (hd + 1) * GLA_DK)

    def vcols(hd):
        return slice(hd * GLA_DV, (hd + 1) * GLA_DV)

    att = {}
    for c, hd in pairs:
        a = lax.dot_general(q_in[rows(c), kcols(hd)], k_in[rows(c), kcols(hd)],
                            (((1,), (1,)), ((), ())), preferred_element_type=F32)
        att[c, hd] = jnp.where(causal, a, 0.0).astype(BF16)
    for c, hd in pairs:
        oacc_ref[rows(c), vcols(hd)] = jnp.dot(att[c, hd], v_ref[rows(c), vcols(hd)],
                                               preferred_element_type=F32)
    for c, hd in pairs:
        snew_ref[c * GLA_HEADS + hd] = lax.dot_general(
            k_dec[rows(c), kcols(hd)], v_ref[rows(c), vcols(hd)],
            (((0,), (0,)), ((), ())), preferred_element_type=F32)

    decay_rows = jnp.exp(jnp.concatenate(
        b_last_rows + [jnp.zeros((LANES - n_chunks, GLA_QK_W), F32)], axis=0))
    decay_cols = decay_rows.T
    states = [state_ref[hd] for hd in range(GLA_HEADS)]
    for c in range(n_chunks):
        for hd in range(GLA_HEADS):
            s_prev = states[hd]
            oacc_ref[rows(c), vcols(hd)] += jnp.dot(
                q_in[rows(c), kcols(hd)], s_prev.astype(BF16), preferred_element_type=F32)
            dcol = decay_cols[kcols(hd), c:c + 1]
            states[hd] = s_prev * dcol + snew_ref[c * GLA_HEADS + hd]
    for hd in range(GLA_HEADS):
        state_ref[hd] = states[hd]

    for hd in range(GLA_HEADS):
        vs = slice(hd * GLA_DV, (hd + 1) * GLA_DV)
        oh = oacc_ref[:, vs]
        ms = jnp.mean(oh * oh, axis=-1, keepdims=True)
        y = oh * lax.rsqrt(ms + NORM_EPS) * og_ref[...]
        r = r_ref[:, vs].astype(F32)
        o_ref[:, vs] = (y * (r * _sigmoid(r))).astype(BF16)


def _gla_call(gqk, gv, gr, lr, au_hi, au_lo, ab_row, og_row):
    nt = SEQ // TT_GLA
    tok = lambda w: pl.BlockSpec((TT_GLA, w), lambda b, t: (b * nt + t, 0))
    const2 = lambda r, c: pl.BlockSpec((r, c), lambda b, t: (0, 0))
    return pl.pallas_call(
        _gla_kernel,
        grid=(BATCH, nt),
        in_specs=[tok(2 * GLA_QK_W), tok(GLA_V_W), tok(GLA_V_W), tok(LR_PAD),
                  const2(LR_PAD, GLA_QK_W), const2(LR_PAD, GLA_QK_W), const2(1, GLA_QK_W),
                  const2(1, GLA_DV)],
        out_specs=tok(GLA_V_W),
        out_shape=jax.ShapeDtypeStruct((TOKENS, GLA_V_W), BF16),
        scratch_shapes=[pltpu.VMEM((GLA_HEADS, GLA_DK, GLA_DV), F32),
                        pltpu.VMEM((TT_GLA, GLA_V_W), F32),
                        pltpu.VMEM((TT_GLA // GLA_CHUNK * GLA_HEADS, GLA_DK, GLA_DV), F32)],
        compiler_params=_cparams(("parallel", "arbitrary")),
        name="gla",
    )(gqk, gv, gr, lr, au_hi, au_lo, ab_row, og_row)


def _attn_kernel(qT_ref, k_ref, vT_ref, lam_ref, og_ref, o_ref,
                 q_scr, s_scr, p_scr, acc_scr, m_scr):
    def q_block(i, carry):
        _attn_q_block(i, qT_ref, k_ref, vT_ref, lam_ref, og_ref, o_ref,
                      q_scr, s_scr, p_scr, acc_scr, m_scr)
        return carry

    lax.fori_loop(0, N_KV, q_block, 0)


def _attn_q_block(i, qT_ref, k_ref, vT_ref, lam_ref, og_ref, o_ref,
                  q_scr, s_scr, p_scr, acc_scr, m_scr):
    qT = qT_ref[0, 0, i]
    rowq = lax.broadcasted_iota(I32, qT.shape, 0)
    zero = jnp.zeros_like(qT)
    q_scr[0] = jnp.where(rowq < DIFF_DH, qT, zero)
    q_scr[1] = jnp.where(rowq >= DIFF_DH, qT, zero)
    m_scr[...] = jnp.full(m_scr.shape, NEG_INF, F32)
    acc_scr[...] = jnp.zeros(acc_scr.shape, F32)
    n_sub = TQ // ATT_SUB

    def fold8(t, op):
        return op(t.reshape(t.shape[0] // SUBLANES, SUBLANES, TQ), axis=0)

    def scores(c, j):
        s_scr[c] = jnp.dot(k_ref[0, 0, j], q_scr[c], preferred_element_type=F32)

    def load_s(c, r, masked):
        s = s_scr[c, r * ATT_SUB:(r + 1) * ATT_SUB, :]
        if masked:
            key_i = lax.broadcasted_iota(I32, (ATT_SUB, TQ), 0) + r * ATT_SUB
            qry_i = lax.broadcasted_iota(I32, (ATT_SUB, TQ), 1)
            s = jnp.where(key_i <= qry_i, s, NEG_INF)
        return s

    def softmax_pv(c, j, masked):
        m8 = fold8(load_s(c, 0, masked), jnp.max)
        for r in range(1, n_sub):
            m8 = jnp.maximum(m8, fold8(load_s(c, r, masked), jnp.max))
        m_old = m_scr[c]
        m_new = jnp.maximum(m_old, jnp.max(m8, axis=0, keepdims=True))
        alpha = jnp.exp2(m_old - m_new)
        for r in range(n_sub):
            p = jnp.exp2(load_s(c, r, masked) - m_new)
            p_scr[c, r * ATT_SUB:(r + 1) * ATT_SUB, :] = p.astype(BF16)
        m_scr[c] = m_new
        acc_scr[c] = acc_scr[c] * alpha + jnp.dot(vT_ref[0, 0, j], p_scr[c],
                                                  preferred_element_type=F32)

    scores(0, 0)

    def body(j, carry):
        scores(1, j)
        softmax_pv(0, j, False)
        scores(0, j + 1)
        softmax_pv(1, j, False)
        return carry

    lax.fori_loop(0, i, body, 0)
    scores(1, i)
    softmax_pv(0, i, True)
    softmax_pv(1, i, True)

    l1 = acc_scr[0, DIFF_DV:DIFF_DV + 1, :]
    l2 = acc_scr[1, DIFF_DV:DIFF_DV + 1, :]

    lam_p = lam_ref[...]
    lam = (jnp.exp(jnp.sum(lam_p[0:1] * lam_p[1:2], axis=1, keepdims=True))
           - jnp.exp(jnp.sum(lam_p[2:3] * lam_p[3:4], axis=1, keepdims=True)) + LAMBDA_INIT)
    oT = acc_scr[0, :DIFF_DV, :] / l1 - lam * (acc_scr[1, :DIFF_DV, :] / l2)
    ms = jnp.mean(oT * oT, axis=0, keepdims=True)
    y = oT * lax.rsqrt(ms + NORM_EPS) * og_ref[...] * (1.0 - LAMBDA_INIT)
    o_ref[pl.ds(pl.multiple_of(i * TQ, TQ), TQ), :] = y.T.astype(BF16)


def _attn_call(qT, kk, vT, lam_p, og_col):
    return pl.pallas_call(
        _attn_kernel,
        grid=(BATCH, DIFF_HEADS),
        in_specs=[
            pl.BlockSpec((1, 1, N_KV, LANES, TQ), lambda b, h: (b, h, 0, 0, 0)),
            pl.BlockSpec((1, 1, N_KV, TQ, LANES), lambda b, h: (b, h, 0, 0, 0)),
            pl.BlockSpec((1, 1, N_KV, V_ROWS, TQ), lambda b, h: (b, h, 0, 0, 0)),
            pl.BlockSpec((4, DIFF_DH), lambda b, h: (0, 0)),
            pl.BlockSpec((DIFF_DV, 1), lambda b, h: (0, 0)),
        ],
        out_specs=pl.BlockSpec((SEQ, DIFF_DV), lambda b, h: (b, h)),
        out_shape=jax.ShapeDtypeStruct((TOKENS, DIFF_V_W), BF16),
        scratch_shapes=[pltpu.VMEM((2, LANES, TQ), BF16),
                        pltpu.VMEM((2, TQ, TQ), F32),
                        pltpu.VMEM((2, TQ, TQ), BF16),
                        pltpu.VMEM((2, V_ROWS, TQ), F32),
                        pltpu.VMEM((2, 1, TQ), F32)],
        compiler_params=_cparams(("parallel", "parallel")),
        name="attn",
    )(qT, kk, vT, lam_p, og_col)


def _merge_kernel(x_ref, oa_ref, ob_ref, sa_ref, sb_ref, mod_ref, g2_ref, wa_ref, wb_ref, wo_ref,
                  wrh_ref, wrl_ref, br_ref, x1_ref, h2_ref, rf_ref, ri_ref):
    tm = x_ref.shape[0]
    ma = jnp.dot(oa_ref[...], wa_ref[...], preferred_element_type=F32)
    mb = jnp.dot(ob_ref[...], wb_ref[...], preferred_element_type=F32)
    merged = sa_ref[...].astype(F32) * ma + sb_ref[...].astype(F32) * mb
    y = jnp.dot(merged.astype(BF16), wo_ref[...], preferred_element_type=F32)
    gate1 = mod_ref[0, 2:3, :]
    shift2 = mod_ref[0, 3:4, :]
    scale2 = mod_ref[0, 4:5, :]
    x1 = x_ref[...] + gate1 * y
    x1_ref[...] = x1
    ms = jnp.mean(x1 * x1, axis=-1, keepdims=True)
    h2 = (x1 * lax.rsqrt(ms + NORM_EPS) * g2_ref[...]) * (1.0 + scale2) + shift2
    _store_row_tiles(h2_ref, h2)

    h2_hi = h2.astype(BF16)
    h2_lo = (h2 - h2_hi.astype(F32)).astype(BF16)
    logits = (jnp.dot(h2_hi, wrh_ref[...], preferred_element_type=F32)
              + jnp.dot(h2_lo, wrh_ref[...], preferred_element_type=F32)
              + jnp.dot(h2_hi, wrl_ref[...], preferred_element_type=F32)) + br_ref[...]
    lane = lax.broadcasted_iota(I32, (tm, LANES), 1).astype(F32)
    ninf = -jnp.inf
    big = float(LANES)

    def first_argmax(v):
        vmax = jnp.max(v, axis=1, keepdims=True)
        idx = jnp.min(jnp.where(v == vmax, lane, big), axis=1, keepdims=True)
        return vmax, idx

    gl = jnp.where(lane < N_GROUPS, logits, ninf)
    gmax, gidx = first_argmax(gl)
    p_top = 1.0 / jnp.sum(jnp.exp(gl - gmax), axis=1, keepdims=True)
    lo = N_GROUPS + EXPERTS_PER_GROUP * gidx
    el = jnp.where((lane >= lo) & (lane < lo + EXPERTS_PER_GROUP), logits, ninf)
    e1max, e1 = first_argmax(el)
    e2max, e2 = first_argmax(jnp.where(lane == e1, ninf, el))
    t = jnp.exp(e2max - e1max)
    w1 = 1.0 / (1.0 + t)
    w2 = t / (1.0 + t)
    col = lax.broadcasted_iota(I32, (tm, ROUTE_W), 1)
    rf_ref[...] = jnp.where(col == 0, p_top * w1, jnp.where(col == 1, p_top * w2, 0.0))
    ids = jnp.where(col == 0, e1 - N_GROUPS, jnp.where(col == 1, e2 - N_GROUPS, 0.0))
    ri_ref[...] = ids.astype(I32)


def _merge_call(x2, o_a, o_b, sa, sb, mod, norm2_g, wa, wb, wo, wr_hi, wr_lo, br):
    nb = SEQ // TM_MERGE
    tok = lambda w: pl.BlockSpec((TM_MERGE, w), lambda i: (i, 0))
    const2 = lambda r, c: pl.BlockSpec((r, c), lambda i: (0, 0))
    return pl.pallas_call(
        _merge_kernel,
        grid=(TOKENS // TM_MERGE,),
        in_specs=[tok(D_MODEL), tok(GLA_V_W), tok(DIFF_V_W), tok(D_MODEL), tok(D_MODEL),
                  pl.BlockSpec((1, N_MOD, D_MODEL), lambda i: (i // nb, 0, 0)),
                  const2(1, D_MODEL),
                  const2(GLA_V_W, D_MODEL), const2(DIFF_V_W, D_MODEL), const2(D_MODEL, D_MODEL),
                  const2(D_MODEL, LANES), const2(D_MODEL, LANES), const2(1, LANES)],
        out_specs=(tok(D_MODEL), _row_tile_spec(TM_MERGE, lambda i: (i, 0)),
                   tok(ROUTE_W), tok(ROUTE_W)),
        out_shape=(jax.ShapeDtypeStruct((TOKENS, D_MODEL), F32),
                   jax.ShapeDtypeStruct((TOKENS * ROW_TILE_S, LANES), F32),
                   jax.ShapeDtypeStruct((TOKENS, ROUTE_W), F32),
                   jax.ShapeDtypeStruct((TOKENS, ROUTE_W), I32)),
        compiler_params=_cparams(("parallel",)),
        name="merge",
    )(x2, o_a, o_b, sa, sb, mod, norm2_g, wa, wb, wo, wr_hi, wr_lo, br)


def _rank_kernel(ri_ref, rank_ref, cnt_ref, carry_ref, tri_ref):
    tm = ri_ref.shape[0]

    @pl.when(pl.program_id(0) == 0)
    def _():
        carry_ref[...] = jnp.zeros_like(carry_ref)
        ti = lax.broadcasted_iota(I32, (tm, tm), 0)
        tj = lax.broadcasted_iota(I32, (tm, tm), 1)
        tri_ref[...] = jnp.where(ti > tj, 1.0, 0.0).astype(BF16)

    ri = ri_ref[...]
    e1 = ri[:, 0:1]
    e2 = ri[:, 1:2]
    lane = lax.broadcasted_iota(I32, (tm, LANES), 1)
    hit1 = lane == e1
    hit2 = lane == e2
    onehot = jnp.where(hit1 | hit2, 1.0, 0.0)
    before = jnp.dot(tri_ref[...], onehot.astype(BF16), preferred_element_type=F32) + carry_ref[...]
    r1 = jnp.sum(jnp.where(hit1, before, 0.0), axis=1, keepdims=True)
    r2 = jnp.sum(jnp.where(hit2, before, 0.0), axis=1, keepdims=True)
    carry_ref[...] = carry_ref[...] + jnp.sum(onehot, axis=0, keepdims=True)
    cnt_ref[...] = carry_ref[...]

    cols = jnp.where(lane == 0, e1.astype(F32),
                     jnp.where(lane == 1, e2.astype(F32),
                               jnp.where(lane == 2, r1, jnp.where(lane == 3, r2, 0.0))))
    rank_ref[0] = cols.T[0:ROUTE_W, :].astype(I32)


def _rank_call(ri):
    return pl.pallas_call(
        _rank_kernel,
        grid=(TOKENS // TM_RANK,),
        in_specs=[pl.BlockSpec((TM_RANK, ROUTE_W), lambda i: (i, 0))],
        out_specs=(pl.BlockSpec((1, ROUTE_W, TM_RANK), lambda i: (i, 0, 0)),
                   pl.BlockSpec((1, LANES), lambda i: (0, 0))),
        out_shape=(jax.ShapeDtypeStruct((TOKENS // TM_RANK, ROUTE_W, TM_RANK), I32),
                   jax.ShapeDtypeStruct((1, LANES), F32)),
        scratch_shapes=[pltpu.VMEM((1, LANES), F32), pltpu.VMEM((TM_RANK, TM_RANK), BF16)],
        compiler_params=_cparams(("arbitrary",)),
        name="rank",
    )(ri)


def _plan_kernel(rk_ref, cnt_ref, dest_ref, blk_ref):
    n_tiles, _, tm = rk_ref.shape
    cnt = cnt_ref[...]
    padded = jnp.floor((cnt + (MOE_BLK - 1)) * (1.0 / MOE_BLK)) * MOE_BLK
    lane = lax.broadcasted_iota(I32, (1, LANES), 1)
    seg_end = padded
    step = 1
    while step < N_EXPERTS:
        seg_end = seg_end + jnp.where(lane >= step, pltpu.roll(seg_end, step, 1), 0.0)
        step *= 2
    seg_start = seg_end - padded
    valid_end = seg_start + cnt

    ei = lax.broadcasted_iota(I32, (LANES, LANES), 0)
    ej = lax.broadcasted_iota(I32, (LANES, LANES), 1)

    def to_col(rowv):
        return jnp.sum(jnp.where(ei == ej, rowv, 0.0), axis=1, keepdims=True)

    start_col, end_col, valid_col = to_col(seg_start), to_col(seg_end), to_col(valid_end)

    nb_pad = blk_ref.shape[1]
    e_sub = lax.broadcasted_iota(I32, (LANES, nb_pad), 0)
    b_start = (lax.broadcasted_iota(I32, (1, nb_pad), 1) * MOE_BLK).astype(F32)
    ends_before = jnp.where((e_sub < N_EXPERTS) & (end_col <= b_start), 1.0, 0.0)
    block_e = jnp.minimum(jnp.sum(ends_before, axis=0, keepdims=True), N_EXPERTS - 1.0)
    block_valid_end = jnp.sum(jnp.where(e_sub.astype(F32) == block_e, valid_col, 0.0),
                              axis=0, keepdims=True)
    n_valid = jnp.clip(block_valid_end - b_start, 0.0, float(MOE_BLK))
    n_used = jnp.max(seg_end, axis=1, keepdims=True) * (1.0 / MOE_BLK)
    blk_ref[...] = jnp.concatenate(
        [block_e, n_valid, jnp.broadcast_to(n_used, (1, nb_pad)),
         jnp.zeros((ROUTE_W - 3, nb_pad), F32)], axis=0).astype(I32)

    e_tok = lax.broadcasted_iota(I32, (LANES, tm), 0)

    def tile(t, carry):
        rk = rk_ref[t]
        d = [jnp.sum(jnp.where(e_tok == rk[k:k + 1], start_col, 0.0), axis=0, keepdims=True)
             + rk[k + 2:k + 3].astype(F32) for k in range(TOP_K)]
        dest_ref[t] = jnp.concatenate(d + [jnp.zeros((ROUTE_W - TOP_K, tm), F32)], axis=0).astype(I32)
        return carry

    lax.fori_loop(0, n_tiles, tile, 0)


def _plan_call(rank_rows, cnt):
    n_tiles = TOKENS // TM_RANK
    nb_pad = -(-N_BLOCKS // LANES) * LANES
    dest, blk = pl.pallas_call(
        _plan_kernel,
        out_shape=(jax.ShapeDtypeStruct((n_tiles, ROUTE_W, TM_RANK), I32),
                   jax.ShapeDtypeStruct((ROUTE_W, nb_pad), I32)),
        compiler_params=pltpu.CompilerParams(vmem_limit_bytes=VMEM_LIMIT),
        name="plan",
    )(rank_rows, cnt)
    dest1 = dest[:, 0, :].reshape(TOKENS)
    dest2 = dest[:, 1, :].reshape(TOKENS)
    return dest1, dest2, blk[0, :N_BLOCKS], blk[1, :N_BLOCKS], blk[2, :1]


def _sc_params():
    return pltpu.CompilerParams(use_tc_tiling_on_sc=True)


def _sc_mesh():
    return plsc.VectorSubcoreMesh(core_axis_name="core", subcore_axis_name="subcore")


def _sc_worker_base(per_worker):
    wid = lax.axis_index("subcore") * SC_CORES + lax.axis_index("core")
    return wid * per_worker


def _sc_scatter_rows(src, dest1, dest2, n_out):
    n = src.shape[0]
    per_worker = n // SC_WORKERS
    assert per_worker * SC_WORKERS == n and per_worker % SC_CHUNK == 0

    def body(src_hbm, d1_hbm, d2_hbm, out_hbm, idx_v, rows_v):
        base = _sc_worker_base(per_worker)

        @pl.loop(0, per_worker // SC_CHUNK)
        def _(j):
            start = pl.multiple_of(base + j * SC_CHUNK, SC_CHUNK)
            pltpu.sync_copy(src_hbm.at[pl.ds(start, SC_CHUNK)], rows_v)
            for d_hbm in (d1_hbm, d2_hbm):
                pltpu.sync_copy(d_hbm.at[pl.ds(start, SC_CHUNK)], idx_v)
                pltpu.sync_copy(rows_v, out_hbm.at[idx_v])

    return pl.kernel(
        body,
        out_type=jax.ShapeDtypeStruct((n_out, ROW_TILE_S, LANES), F32),
        mesh=_sc_mesh(),
        scratch_types=[pltpu.VMEM((SC_CHUNK,), I32),
                       pltpu.VMEM((SC_CHUNK, ROW_TILE_S, LANES), F32)],
        compiler_params=_sc_params(),
        name="sc_dispatch",
    )(src, dest1, dest2)


def _sc_gather_rows(table, idx):
    n = idx.shape[0]
    per_worker = n // SC_WORKERS
    half = SC_CHUNK // 2
    assert per_worker * SC_WORKERS == n and per_worker % SC_CHUNK == 0

    def body(table_hbm, idx_hbm, out_hbm, idx_v, rows_a, rows_b, gsem_a, gsem_b, wsem_a, wsem_b):
        base = pl.multiple_of(_sc_worker_base(per_worker), SC_CHUNK)
        pltpu.sync_copy(idx_hbm.at[pl.ds(base, per_worker)], idx_v)

        @pl.loop(0, per_worker // SC_CHUNK)
        def _(j):
            off_a = pl.multiple_of(j * SC_CHUNK, SC_CHUNK)
            off_b = pl.multiple_of(j * SC_CHUNK + half, half)
            ga = pltpu.async_copy(table_hbm.at[idx_v.at[pl.ds(off_a, half)]], rows_a, gsem_a)
            gb = pltpu.async_copy(table_hbm.at[idx_v.at[pl.ds(off_b, half)]], rows_b, gsem_b)
            ga.wait()
            wa = pltpu.async_copy(rows_a, out_hbm.at[pl.ds(base + off_a, half)], wsem_a)
            gb.wait()
            wb = pltpu.async_copy(rows_b, out_hbm.at[pl.ds(base + off_b, half)], wsem_b)
            wa.wait()
            wb.wait()

    return pl.kernel(
        body,
        out_type=jax.ShapeDtypeStruct((n, ROW_TILE_S, LANES), F32),
        mesh=_sc_mesh(),
        scratch_types=[pltpu.VMEM((per_worker,), I32),
                       pltpu.VMEM((half, ROW_TILE_S, LANES), F32),
                       pltpu.VMEM((half, ROW_TILE_S, LANES), F32),
                       pltpu.SemaphoreType.DMA, pltpu.SemaphoreType.DMA,
                       pltpu.SemaphoreType.DMA, pltpu.SemaphoreType.DMA],
        compiler_params=_sc_params(),
        name="sc_gather",
    )(table, idx)


def _expert_kernel(be_ref, nv_ref, nu_ref, xs_ref, w1_ref, w3_ref, w2_ref, y_ref, w1b, w3b, w2b):
    i = pl.program_id(0)
    e = be_ref[i]
    prev = be_ref[jnp.maximum(i - 1, 0)]
    used = i < nu_ref[0]

    @pl.when(used & ((i == 0) | (e != prev)))
    def _():
        w1b[...] = w1_ref[0].astype(BF16)
        w3b[...] = w3_ref[0].astype(BF16)
        w2b[...] = w2_ref[0].astype(BF16)

    @pl.when(used)
    def _():
        row = lax.broadcasted_iota(I32, (MOE_BLK, D_MODEL), 0)
        xb = jnp.where(row < nv_ref[i], _load_row_tiles(xs_ref), 0.0).astype(BF16)
        a = jnp.dot(xb, w1b[...], preferred_element_type=F32)
        g = jnp.dot(xb, w3b[...], preferred_element_type=F32)
        hmid = (a * _sigmoid(a)) * g
        _store_row_tiles(y_ref, jnp.dot(hmid.astype(BF16), w2b[...], preferred_element_type=F32))

    @pl.when(jnp.logical_not(used))
    def _():
        y_ref[...] = jnp.zeros_like(y_ref)


def _expert_call(block_e, n_valid, n_used, xs, w1, w3, w2):
    grid_spec = pltpu.PrefetchScalarGridSpec(
        num_scalar_prefetch=3,
        grid=(N_BLOCKS,),
        in_specs=[_row_tile_spec(MOE_BLK, lambda i, be, nv, nu: (jnp.minimum(i, nu[0] - 1), 0)),
                  pl.BlockSpec((1, D_MODEL, D_EXPERT), lambda i, be, nv, nu: (be[i], 0, 0)),
                  pl.BlockSpec((1, D_MODEL, D_EXPERT), lambda i, be, nv, nu: (be[i], 0, 0)),
                  pl.BlockSpec((1, D_EXPERT, D_MODEL), lambda i, be, nv, nu: (be[i], 0, 0))],
        out_specs=_row_tile_spec(MOE_BLK, lambda i, be, nv, nu: (i, 0)),
        scratch_shapes=[pltpu.VMEM((D_MODEL, D_EXPERT), BF16),
                        pltpu.VMEM((D_MODEL, D_EXPERT), BF16),
                        pltpu.VMEM((D_EXPERT, D_MODEL), BF16)],
    )
    return pl.pallas_call(
        _expert_kernel,
        grid_spec=grid_spec,
        out_shape=jax.ShapeDtypeStruct((N_BLOCKS * MOE_BLK * ROW_TILE_S, LANES), F32),
        compiler_params=_cparams(("arbitrary",)),
        name="experts",
    )(block_e, n_valid, n_used, xs, w1, w3, w2)


def _combine_rows_kernel(x1_ref, rf_ref, mod_ref, ya_ref, yb_ref, *rest):
    o_ref = rest[-1]
    rf = rf_ref[...]
    gate2 = mod_ref[0, 5:6, :]
    moe = rf[:, 0:1] * _load_row_tiles(ya_ref) + rf[:, 1:2] * _load_row_tiles(yb_ref)
    o_ref[...] = x1_ref[...] + gate2 * moe


def _combine_rows_call(x1, rf, mod, yg, part, prev=None):
    nb = SEQ // TM_ROW
    n_half = TOKENS // TM_ROW // 2
    t0 = part * n_half
    in_specs = [pl.BlockSpec((TM_ROW, D_MODEL), lambda i: (t0 + i, 0)),
                pl.BlockSpec((TM_ROW, ROUTE_W), lambda i: (t0 + i, 0)),
                pl.BlockSpec((1, N_MOD, D_MODEL), lambda i: ((t0 + i) // nb, 0, 0)),
                _row_tile_spec(TM_ROW, lambda i: (i, 0)),
                _row_tile_spec(TM_ROW, lambda i: (i + n_half, 0))]
    args = [x1, rf, mod, yg, yg]
    aliases = {}
    if prev is not None:
        in_specs.append(pl.BlockSpec(memory_space=pl.ANY))
        args.append(prev)
        aliases = {len(args) - 1: 0}
    return pl.pallas_call(
        _combine_rows_kernel,
        grid=(n_half,),
        in_specs=in_specs,
        out_specs=pl.BlockSpec((TM_ROW, D_MODEL), lambda i: (t0 + i, 0)),
        out_shape=jax.ShapeDtypeStruct((TOKENS, D_MODEL), F32),
        input_output_aliases=aliases,
        compiler_params=_cparams(("parallel",)),
        name="combine",
    )(*args)


def kernel(x, c, positions, ada_w, ada_b, norm1_g, norm2_g, w_in, gla_alpha_up, gla_alpha_b,
           gla_out_g, diff_q_g, diff_k_g, diff_lq1, diff_lk1, diff_lq2, diff_lk2, diff_out_g,
           w_branch_a, w_branch_b, w_out, router_group_w, router_group_b, router_expert_w,
           router_expert_b, expert_w1, expert_w3, expert_w2):
    assert x.shape == (BATCH, SEQ, D_MODEL) and ada_w.shape[0] == 1
    x2 = x.reshape(TOKENS, D_MODEL)

    w_in_p = _wprep_call(w_in[0])
    au_pad = jnp.zeros((LR_PAD, GLA_QK_W), F32).at[:GLA_GATE_RANK].set(gla_alpha_up[0])
    au_hi = au_pad.astype(BF16)
    au_lo = (au_pad - au_hi.astype(F32)).astype(BF16)
    gid = jnp.arange(DIFF_QK_W) // DIFF_DH
    bd = jnp.where(gid[:, None] == gid[None, :], 1.0 / DIFF_DH, 0.0).astype(BF16)
    qg_row = jnp.tile(diff_q_g[0], DIFF_QK_W // DIFF_DH).reshape(1, DIFF_QK_W)
    kg_row = jnp.tile(diff_k_g[0], DIFF_QK_W // DIFF_DH).reshape(1, DIFF_QK_W)
    inv_freq = ROPE_THETA ** (-jnp.arange(ROPE_HALF, dtype=F32) / ROPE_HALF)
    invf64 = jnp.concatenate([inv_freq, inv_freq, jnp.zeros((DIFF_DH - ROPE_DIM,), F32)])
    invf_row = jnp.tile(invf64, LANES // DIFF_DH).reshape(1, LANES)
    pos_col = positions.reshape(TOKENS, 1)
    lam_p = jnp.concatenate([diff_lq1, diff_lk1, diff_lq2, diff_lk2], axis=0)
    wr = (jnp.zeros((D_MODEL, LANES), F32)
          .at[:, :N_GROUPS].set(router_group_w[0])
          .at[:, N_GROUPS:N_GROUPS + N_EXPERTS].set(router_expert_w[0]))
    wr_hi = wr.astype(BF16)
    wr_lo = (wr - wr_hi.astype(F32)).astype(BF16)
    br = (jnp.zeros((1, LANES), F32)
          .at[0, :N_GROUPS].set(router_group_b[0])
          .at[0, N_GROUPS:N_GROUPS + N_EXPERTS].set(router_expert_b[0]))

    mod = _mod_call(c, ada_w[0], ada_b[0])
    gqk, gv, gr, lr, qT, kk, vT, sa, sb = _in_call(
        x2, mod, norm1_g, w_in_p, bd, qg_row, kg_row, pos_col, invf_row)
    o_a = _gla_call(gqk, gv, gr, lr, au_hi, au_lo, gla_alpha_b, gla_out_g)
    o_b = _attn_call(qT, kk, vT, lam_p, diff_out_g.reshape(DIFF_DV, 1))
    x1, h2, rf, ri = _merge_call(
        x2, o_a, o_b, sa, sb, mod, norm2_g, w_branch_a[0].astype(BF16),
        w_branch_b[0].astype(BF16), w_out[0].astype(BF16), wr_hi, wr_lo, br)
    rank, cnt = _rank_call(ri)

    dest1, dest2, block_e, n_valid, n_used = _plan_call(rank, cnt)

    xs = _sc_scatter_rows(_as_row_tiles(h2), dest1, dest2, N_BLOCKS * MOE_BLK)
    y = _expert_call(block_e, n_valid, n_used, _as_2d(xs), expert_w1[0], expert_w3[0], expert_w2[0])
    y3 = _as_row_tiles(y)
    half = TOKENS // 2
    out = None
    for part in range(2):
        tok = slice(part * half, (part + 1) * half)
        yg = _sc_gather_rows(y3, jnp.concatenate([dest1[tok], dest2[tok]]))
        out = _combine_rows_call(x1, rf, mod, _as_2d(yg), part, out)
    return out.reshape(BATCH, SEQ, D_MODEL)
```

```python
import math

import jax
import jax.numpy as jnp
from jax import lax
from jax.experimental import pallas as pl
from jax.experimental.pallas import tpu as pltpu
from jax.experimental.pallas import tpu_sc as plsc

F32 = jnp.float32
BF16 = jnp.bfloat16
I32 = jnp.int32

D_MODEL = 1024
BATCH = 4
SEQ = 4096
TOKENS = BATCH * SEQ
N_MOD = 6
NORM_EPS = 1e-6

GLA_HEADS = 4
GLA_DK = 64
GLA_DV = 128
GLA_GATE_RANK = 16
GLA_GATE_TAU = 16.0
GLA_CHUNK = 64
GLA_QK_W = GLA_HEADS * GLA_DK
GLA_V_W = GLA_HEADS * GLA_DV

DIFF_HEADS = 4
DIFF_DH = 64
DIFF_DV = 2 * DIFF_DH
DIFF_QK_W = DIFF_HEADS * 2 * DIFF_DH
DIFF_V_W = DIFF_HEADS * DIFF_DV
ROPE_THETA = 500000.0
ROPE_DIM = DIFF_DH // 4
ROPE_HALF = ROPE_DIM // 2
NEG_INF = -1e30
LAMBDA_INIT = 0.8 - 0.6 * 1.0

N_GROUPS = 4
EXPERTS_PER_GROUP = 8
N_EXPERTS = N_GROUPS * EXPERTS_PER_GROUP
TOP_K = 2
D_EXPERT = 512

LANES = 128
SUBLANES = 8
ROW_TILE_S = D_MODEL // LANES
SC_CORES = 2
SC_SUBCORES = 16
SC_WORKERS = SC_CORES * SC_SUBCORES
SC_CHUNK = 64
LR_PAD = LANES
D_IN_PAD = 2 * GLA_QK_W + 2 * GLA_V_W + 2 * DIFF_QK_W + DIFF_V_W + 2 * D_MODEL + LR_PAD

TM_IN = 512
TQ = 512
V_ROWS = DIFF_DV + 16
ATT_SUB = 32
Q_SCALE = DIFF_DH ** -0.5 * math.log2(math.e)
N_KV = SEQ // TQ
TT_GLA = 512
TM_MERGE = 512
TM_ROW = 256
MOE_BLK = 256
N_BLOCKS = (TOKENS * TOP_K + N_EXPERTS * (MOE_BLK - 1) + MOE_BLK - 1) // MOE_BLK
ROUTE_W = 8

VMEM_LIMIT = 56 * 1024 * 1024


def _cparams(sem):
    return pltpu.CompilerParams(dimension_semantics=sem, vmem_limit_bytes=VMEM_LIMIT)


def _sigmoid(x):
    return 1.0 / (1.0 + jnp.exp(-x))


def _row_tile_spec(rows, index_map):
    return pl.BlockSpec((rows * ROW_TILE_S, LANES), index_map)


def _as_row_tiles(a2d):
    return a2d.reshape(a2d.shape[0] // ROW_TILE_S, ROW_TILE_S, LANES)


def _as_2d(a3d):
    return a3d.reshape(a3d.shape[0] * ROW_TILE_S, LANES)


def _store_row_tiles(ref, val):
    for s in range(ROW_TILE_S):
        ref[pl.ds(s, val.shape[0], stride=ROW_TILE_S), :] = val[:, s * LANES:(s + 1) * LANES]


def _load_row_tiles(ref):
    rows = ref.shape[0] // ROW_TILE_S
    return jnp.concatenate(
        [ref[pl.ds(s, rows, stride=ROW_TILE_S), :] for s in range(ROW_TILE_S)], axis=1)


def _mod_kernel(ct_ref, w_ref, b_ref, o_ref):
    ct = ct_ref[...]
    ca = ct * _sigmoid(ct)
    w = w_ref[...]
    rows = []
    for b in range(BATCH):
        rows.append(jnp.sum(w * ca[:, b:b + 1], axis=0, keepdims=True) + b_ref[...])
    rows.append(jnp.zeros((8 - BATCH, w.shape[1]), F32))
    o_ref[...] = jnp.concatenate(rows, axis=0)


def _mod_call(c, ada_w, ada_b):
    tn = D_MODEL
    ct = jnp.zeros((D_MODEL, 8), F32).at[:, :BATCH].set(c.T)
    out = pl.pallas_call(
        _mod_kernel,
        grid=(N_MOD,),
        in_specs=[
            pl.BlockSpec((D_MODEL, 8), lambda j: (0, 0)),
            pl.BlockSpec((D_MODEL, tn), lambda j: (0, j)),
            pl.BlockSpec((1, tn), lambda j: (0, j)),
        ],
        out_specs=pl.BlockSpec((8, tn), lambda j: (0, j)),
        out_shape=jax.ShapeDtypeStruct((8, N_MOD * D_MODEL), F32),
        compiler_params=_cparams(("arbitrary",)),
        name="mod",
    )(ct, ada_w, ada_b.reshape(1, N_MOD * D_MODEL))
    return out[:BATCH].reshape(BATCH, N_MOD, D_MODEL)


def _wprep_kernel(w_ref, o_ref):
    w = w_ref[...]
    lr0 = 2 * GLA_QK_W + 2 * GLA_V_W
    rest = D_IN_PAD - LR_PAD - lr0
    o_ref[:, :lr0] = w[:, :lr0].astype(BF16)
    o_ref[:, lr0:lr0 + rest] = w[:, lr0 + GLA_GATE_RANK:].astype(BF16)
    lr = jnp.concatenate([w[:, lr0:lr0 + GLA_GATE_RANK],
                          jnp.zeros((w.shape[0], LR_PAD - GLA_GATE_RANK), F32)], axis=1)
    o_ref[:, lr0 + rest:] = lr.astype(BF16)


def _wprep_call(w_in0):
    d_in = w_in0.shape[1]
    rows = 128
    return pl.pallas_call(
        _wprep_kernel,
        grid=(D_MODEL // rows,),
        in_specs=[pl.BlockSpec((rows, d_in), lambda i: (i, 0))],
        out_specs=pl.BlockSpec((rows, D_IN_PAD), lambda i: (i, 0)),
        out_shape=jax.ShapeDtypeStruct((D_MODEL, D_IN_PAD), BF16),
        compiler_params=_cparams(("parallel",)),
        name="w_prep",
    )(w_in0)


def _in_kernel(x_ref, mod_ref, g1_ref, w_ref, bd_ref, qg_ref, kg_ref, pos_ref, invf_ref,
               gqk_ref, gv_ref, gr_ref, lr_ref, qT_ref, k_ref, vT_ref, sa_ref, sb_ref,
               cos_scr, sin_scr):
    ang = pos_ref[...].astype(F32) * invf_ref[...]
    cos_scr[...] = jnp.cos(ang)
    sin_scr[...] = jnp.sin(ang)

    x = x_ref[...]
    shift1 = mod_ref[0, 0:1, :]
    scale1 = mod_ref[0, 1:2, :]
    ms = jnp.mean(x * x, axis=-1, keepdims=True)
    h = (x * lax.rsqrt(ms + NORM_EPS) * g1_ref[...]) * (1.0 + scale1) + shift1
    hb = h.astype(BF16)

    def proj(c0, c1):
        return jnp.dot(hb, w_ref[:, c0:c1], preferred_element_type=F32)

    tm = x.shape[0]
    cos4 = jnp.concatenate([cos_scr[...]] * DIFF_HEADS, axis=1)
    sin4 = jnp.concatenate([sin_scr[...]] * DIFF_HEADS, axis=1)
    lane = lax.broadcasted_iota(I32, (tm, DIFF_QK_W), 1)
    first_half = (lane % DIFF_DH) < ROPE_HALF
    bd = bd_ref[...]

    def norm_rope(t, gain_row):
        t2 = t * t
        hi = t2.astype(BF16)
        lo = (t2 - hi.astype(F32)).astype(BF16)
        gms = (jnp.dot(hi, bd, preferred_element_type=F32)
               + jnp.dot(lo, bd, preferred_element_type=F32))
        t = t * lax.rsqrt(gms + NORM_EPS) * gain_row
        nxt = pltpu.roll(t, DIFF_QK_W - ROPE_HALF, 1)
        prv = pltpu.roll(t, ROPE_HALF, 1)
        return t * cos4 + jnp.where(first_half, -nxt, prv) * sin4

    dq_raw = proj(1536, 2048)
    dk_raw = proj(2048, 2560)
    gqk_ref[...] = proj(0, 512).astype(BF16)
    dq = norm_rope(dq_raw, qg_ref[...]) * Q_SCALE
    gv_ref[...] = proj(512, 1024).astype(BF16)
    dk = norm_rope(dk_raw, kg_ref[...])
    dv = proj(2560, 3072)
    for hd in range(DIFF_HEADS):
        sl = slice(hd * LANES, (hd + 1) * LANES)
        qT_ref[0, hd, 0] = dq[:, sl].T.astype(BF16)
        k_ref[0, hd, 0] = dk[:, sl].astype(BF16)
    gr_ref[...] = proj(1024, 1536).astype(BF16)
    for hd in range(DIFF_HEADS):
        sl = slice(hd * LANES, (hd + 1) * LANES)
        vT_ref[0, hd, 0] = jnp.concatenate(
            [dv[:, sl].T, jnp.ones((V_ROWS - DIFF_DV, tm), F32)], axis=0).astype(BF16)
    sa_ref[...] = _sigmoid(proj(3072, 4096)).astype(BF16)
    sb_ref[...] = _sigmoid(proj(4096, 5120)).astype(BF16)
    lr_ref[...] = proj(5120, 5248)


def _in_call(x2, mod, norm1_g, w_in_p, bd, qg_row, kg_row, pos_col, invf_row):
    nb = SEQ // TM_IN
    tok_spec = lambda w: pl.BlockSpec((TM_IN, w), lambda i: (i, 0))
    const2 = lambda r, c: pl.BlockSpec((r, c), lambda i: (0, 0))
    out_shapes = (
        jax.ShapeDtypeStruct((TOKENS, 2 * GLA_QK_W), BF16),
        jax.ShapeDtypeStruct((TOKENS, GLA_V_W), BF16),
        jax.ShapeDtypeStruct((TOKENS, GLA_V_W), BF16),
        jax.ShapeDtypeStruct((TOKENS, LR_PAD), F32),
        jax.ShapeDtypeStruct((BATCH, DIFF_HEADS, N_KV, LANES, TQ), BF16),
        jax.ShapeDtypeStruct((BATCH, DIFF_HEADS, N_KV, TQ, LANES), BF16),
        jax.ShapeDtypeStruct((BATCH, DIFF_HEADS, N_KV, V_ROWS, TQ), BF16),
        jax.ShapeDtypeStruct((TOKENS, D_MODEL), BF16),
        jax.ShapeDtypeStruct((TOKENS, D_MODEL), BF16),
    )
    out_specs = (
        tok_spec(2 * GLA_QK_W), tok_spec(GLA_V_W), tok_spec(GLA_V_W), tok_spec(LR_PAD),
        pl.BlockSpec((1, DIFF_HEADS, 1, LANES, TM_IN), lambda i: (i // nb, 0, i % nb, 0, 0)),
        pl.BlockSpec((1, DIFF_HEADS, 1, TM_IN, LANES), lambda i: (i // nb, 0, i % nb, 0, 0)),
        pl.BlockSpec((1, DIFF_HEADS, 1, V_ROWS, TM_IN), lambda i: (i // nb, 0, i % nb, 0, 0)),
        tok_spec(D_MODEL), tok_spec(D_MODEL),
    )
    return pl.pallas_call(
        _in_kernel,
        grid=(TOKENS // TM_IN,),
        in_specs=[
            tok_spec(D_MODEL),
            pl.BlockSpec((1, N_MOD, D_MODEL), lambda i: (i // nb, 0, 0)),
            const2(1, D_MODEL),
            pl.BlockSpec((D_MODEL, D_IN_PAD), lambda i: (0, 0), pipeline_mode=pl.Buffered(1)),
            const2(DIFF_QK_W, DIFF_QK_W),
            const2(1, DIFF_QK_W), const2(1, DIFF_QK_W),
            tok_spec(1),
            const2(1, LANES),
        ],
        out_specs=out_specs,
        out_shape=out_shapes,
        scratch_shapes=[pltpu.VMEM((TM_IN, LANES), F32), pltpu.VMEM((TM_IN, LANES), F32)],
        compiler_params=_cparams(("parallel",)),
        name="in_proj",
    )(x2, mod, norm1_g, w_in_p, bd, qg_row, kg_row, pos_col, invf_row)


def _gla_kernel(qk_ref, v_ref, r_ref, lr_ref, auh_ref, aul_ref, ab_ref, og_ref, o_ref,
                state_ref, oacc_ref, snew_ref):
    tt = qk_ref.shape[0]
    n_chunks = tt // GLA_CHUNK

    @pl.when(pl.program_id(1) == 0)
    def _():
        state_ref[...] = jnp.zeros_like(state_ref)

    lr = lr_ref[...]
    lr_hi = lr.astype(BF16)
    lr_lo = (lr - lr_hi.astype(F32)).astype(BF16)
    z = (jnp.dot(lr_hi, auh_ref[...], preferred_element_type=F32)
         + jnp.dot(lr_lo, auh_ref[...], preferred_element_type=F32)
         + jnp.dot(lr_hi, aul_ref[...], preferred_element_type=F32)) + ab_ref[...]
    g = (jnp.minimum(z, 0.0) - jnp.log(1.0 + jnp.exp(-jnp.abs(z)))) * (1.0 / GLA_GATE_TAU)

    row = lax.broadcasted_iota(I32, (tt, GLA_QK_W), 0) % GLA_CHUNK
    b = g
    step = 1
    while step < GLA_CHUNK:
        b = b + jnp.where(row >= step, pltpu.roll(b, step, 0), 0.0)
        step *= 2

    b_last_rows = [b[c * GLA_CHUNK + GLA_CHUNK - 1:(c + 1) * GLA_CHUNK, :] for c in range(n_chunks)]
    b_last = jnp.concatenate(
        [jnp.broadcast_to(bl, (GLA_CHUNK, GLA_QK_W)) for bl in b_last_rows], axis=0)

    qk = qk_ref[...].astype(F32)
    q = qk[:, :GLA_QK_W] * (GLA_DK ** -0.5)
    k = qk[:, GLA_QK_W:]
    q_in = (q * jnp.exp(b)).astype(BF16)
    k_in = (k * jnp.exp(-b)).astype(BF16)
    k_dec = (k * jnp.exp(b_last - b)).astype(BF16)

    ci = lax.broadcasted_iota(I32, (GLA_CHUNK, GLA_CHUNK), 0)
    cj = lax.broadcasted_iota(I32, (GLA_CHUNK, GLA_CHUNK), 1)
    causal = ci >= cj

    pairs = [(c, hd) for c in range(n_chunks) for hd in range(GLA_HEADS)]

    def rows(c):
        return slice(c * GLA_CHUNK, (c + 1) * GLA_CHUNK)

    def kcols(hd):
        return slice(hd * GLA_DK, (hd + 1) * GLA_DK)

    def vcols(hd):
        return slice(hd * GLA_DV, (hd + 1) * GLA_DV)

    att = {}
    for c, hd in pairs:
        a = lax.dot_general(q_in[rows(c), kcols(hd)], k_in[rows(c), kcols(hd)],
                            (((1,), (1,)), ((), ())), preferred_element_type=F32)
        att[c, hd] = jnp.where(causal, a, 0.0).astype(BF16)
    for c, hd in pairs:
        oacc_ref[rows(c), vcols(hd)] = jnp.dot(att[c, hd], v_ref[rows(c), vcols(hd)],
                                               preferred_element_type=F32)
    for c, hd in pairs:
        snew_ref[c * GLA_HEADS + hd] = lax.dot_general(
            k_dec[rows(c), kcols(hd)], v_ref[rows(c), vcols(hd)],
            (((0,), (0,)), ((), ())), preferred_element_type=F32)

    decay_rows = jnp.exp(jnp.concatenate(
        b_last_rows + [jnp.zeros((LANES - n_chunks, GLA_QK_W), F32)], axis=0))
    decay_cols = decay_rows.T
    states = [state_ref[hd] for hd in range(GLA_HEADS)]
    for c in range(n_chunks):
        for hd in range(GLA_HEADS):
            s_prev = states[hd]
            oacc_ref[rows(c), vcols(hd)] += jnp.dot(
                q_in[rows(c), kcols(hd)], s_prev.astype(BF16), preferred_element_type=F32)
            dcol = decay_cols[kcols(hd), c:c + 1]
            states[hd] = s_prev * dcol + snew_ref[c * GLA_HEADS + hd]
    for hd in range(GLA_HEADS):
        state_ref[hd] = states[hd]

    for hd in range(GLA_HEADS):
        vs = slice(hd * GLA_DV, (hd + 1) * GLA_DV)
        oh = oacc_ref[:, vs]
        ms = jnp.mean(oh * oh, axis=-1, keepdims=True)
        y = oh * lax.rsqrt(ms + NORM_EPS) * og_ref[...]
        r = r_ref[:, vs].astype(F32)
        o_ref[:, vs] = (y * (r * _sigmoid(r))).astype(BF16)


def _gla_call(gqk, gv, gr, lr, au_hi, au_lo, ab_row, og_row):
    nt = SEQ // TT_GLA
    tok = lambda w: pl.BlockSpec((TT_GLA, w), lambda b, t: (b * nt + t, 0))
    const2 = lambda r, c: pl.BlockSpec((r, c), lambda b, t: (0, 0))
    return pl.pallas_call(
        _gla_kernel,
        grid=(BATCH, nt),
        in_specs=[tok(2 * GLA_QK_W), tok(GLA_V_W), tok(GLA_V_W), tok(LR_PAD),
                  const2(LR_PAD, GLA_QK_W), const2(LR_PAD, GLA_QK_W), const2(1, GLA_QK_W),
                  const2(1, GLA_DV)],
        out_specs=tok(GLA_V_W),
        out_shape=jax.ShapeDtypeStruct((TOKENS, GLA_V_W), BF16),
        scratch_shapes=[pltpu.VMEM((GLA_HEADS, GLA_DK, GLA_DV), F32),
                        pltpu.VMEM((TT_GLA, GLA_V_W), F32),
                        pltpu.VMEM((TT_GLA // GLA_CHUNK * GLA_HEADS, GLA_DK, GLA_DV), F32)],
        compiler_params=_cparams(("parallel", "arbitrary")),
        name="gla",
    )(gqk, gv, gr, lr, au_hi, au_lo, ab_row, og_row)


def _attn_kernel(qT_ref, k_ref, vT_ref, lam_ref, og_ref, o_ref,
                 q_scr, s_scr, p_scr, acc_scr, m_scr):
    def q_block(i, carry):
        _attn_q_block(i, qT_ref, k_ref, vT_ref, lam_ref, og_ref, o_ref,
                      q_scr, s_scr, p_scr, acc_scr, m_scr)
        return carry

    lax.fori_loop(0, N_KV, q_block, 0)


def _attn_q_block(i, qT_ref, k_ref, vT_ref, lam_ref, og_ref, o_ref,
                  q_scr, s_scr, p_scr, acc_scr, m_scr):
    qT = qT_ref[0, 0, i]
    rowq = lax.broadcasted_iota(I32, qT.shape, 0)
    zero = jnp.zeros_like(qT)
    q_scr[0] = jnp.where(rowq < DIFF_DH, qT, zero)
    q_scr[1] = jnp.where(rowq >= DIFF_DH, qT, zero)
    m_scr[...] = jnp.full(m_scr.shape, NEG_INF, F32)
    acc_scr[...] = jnp.zeros(acc_scr.shape, F32)
    n_sub = TQ // ATT_SUB

    def fold8(t, op):
        return op(t.reshape(t.shape[0] // SUBLANES, SUBLANES, TQ), axis=0)

    def scores(c, j):
        s_scr[c] = jnp.dot(k_ref[0, 0, j], q_scr[c], preferred_element_type=F32)

    def load_s(c, r, masked):
        s = s_scr[c, r * ATT_SUB:(r + 1) * ATT_SUB, :]
        if masked:
            key_i = lax.broadcasted_iota(I32, (ATT_SUB, TQ), 0) + r * ATT_SUB
            qry_i = lax.broadcasted_iota(I32, (ATT_SUB, TQ), 1)
            s = jnp.where(key_i <= qry_i, s, NEG_INF)
        return s

    def softmax_pv(c, j, masked):
        m8 = fold8(load_s(c, 0, masked), jnp.max)
        for r in range(1, n_sub):
            m8 = jnp.maximum(m8, fold8(load_s(c, r, masked), jnp.max))
        m_old = m_scr[c]
        m_new = jnp.maximum(m_old, jnp.max(m8, axis=0, keepdims=True))
        alpha = jnp.exp2(m_old - m_new)
        for r in range(n_sub):
            p = jnp.exp2(load_s(c, r, masked) - m_new)
            p_scr[c, r * ATT_SUB:(r + 1) * ATT_SUB, :] = p.astype(BF16)
        m_scr[c] = m_new
        acc_scr[c] = acc_scr[c] * alpha + jnp.dot(vT_ref[0, 0, j], p_scr[c],
                                                  preferred_element_type=F32)

    scores(0, 0)

    def body(j, carry):
        scores(1, j)
        softmax_pv(0, j, False)
        scores(0, j + 1)
        softmax_pv(1, j, False)
        return carry

    lax.fori_loop(0, i, body, 0)
    scores(1, i)
    softmax_pv(0, i, True)
    softmax_pv(1, i, True)

    l1 = acc_scr[0, DIFF_DV:DIFF_DV + 1, :]
    l2 = acc_scr[1, DIFF_DV:DIFF_DV + 1, :]

    lam_p = lam_ref[...]
    lam = (jnp.exp(jnp.sum(lam_p[0:1] * lam_p[1:2], axis=1, keepdims=True))
           - jnp.exp(jnp.sum(lam_p[2:3] * lam_p[3:4], axis=1, keepdims=True)) + LAMBDA_INIT)
    oT = acc_scr[0, :DIFF_DV, :] / l1 - lam * (acc_scr[1, :DIFF_DV, :] / l2)
    ms = jnp.mean(oT * oT, axis=0, keepdims=True)
    y = oT * lax.rsqrt(ms + NORM_EPS) * og_ref[...] * (1.0 - LAMBDA_INIT)
    o_ref[pl.ds(pl.multiple_of(i * TQ, TQ), TQ), :] = y.T.astype(BF16)


def _attn_call(qT, kk, vT, lam_p, og_col):
    return pl.pallas_call(
        _attn_kernel,
        grid=(BATCH, DIFF_HEADS),
        in_specs=[
            pl.BlockSpec((1, 1, N_KV, LANES, TQ), lambda b, h: (b, h, 0, 0, 0)),
            pl.BlockSpec((1, 1, N_KV, TQ, LANES), lambda b, h: (b, h, 0, 0, 0)),
            pl.BlockSpec((1, 1, N_KV, V_ROWS, TQ), lambda b, h: (b, h, 0, 0, 0)),
            pl.BlockSpec((4, DIFF_DH), lambda b, h: (0, 0)),
            pl.BlockSpec((DIFF_DV, 1), lambda b, h: (0, 0)),
        ],
        out_specs=pl.BlockSpec((SEQ, DIFF_DV), lambda b, h: (b, h)),
        out_shape=jax.ShapeDtypeStruct((TOKENS, DIFF_V_W), BF16),
        scratch_shapes=[pltpu.VMEM((2, LANES, TQ), BF16),
                        pltpu.VMEM((2, TQ, TQ), F32),
                        pltpu.VMEM((2, TQ, TQ), BF16),
                        pltpu.VMEM((2, V_ROWS, TQ), F32),
                        pltpu.VMEM((2, 1, TQ), F32)],
        compiler_params=_cparams(("parallel", "parallel")),
        name="attn",
    )(qT, kk, vT, lam_p, og_col)


def _merge_kernel(x_ref, oa_ref, ob_ref, sa_ref, sb_ref, mod_ref, g2_ref, wa_ref, wb_ref, wo_ref,
                  wrh_ref, wrl_ref, br_ref, x1_ref, h2_ref, rf_ref, rank_ref, cnt_ref,
                  carry_ref, tri_ref):
    tm = x_ref.shape[0]

    @pl.when(pl.program_id(0) == 0)
    def _():
        carry_ref[...] = jnp.zeros_like(carry_ref)
        ti = lax.broadcasted_iota(I32, (tm, tm), 0)
        tj = lax.broadcasted_iota(I32, (tm, tm), 1)
        tri_ref[...] = jnp.where(ti > tj, 1.0, 0.0).astype(BF16)

    ma = jnp.dot(oa_ref[...], wa_ref[...], preferred_element_type=F32)
    mb = jnp.dot(ob_ref[...], wb_ref[...], preferred_element_type=F32)
    merged = sa_ref[...].astype(F32) * ma + sb_ref[...].astype(F32) * mb
    y = jnp.dot(merged.astype(BF16), wo_ref[...], preferred_element_type=F32)
    gate1 = mod_ref[0, 2:3, :]
    shift2 = mod_ref[0, 3:4, :]
    scale2 = mod_ref[0, 4:5, :]
    x1 = x_ref[...] + gate1 * y
    x1_ref[...] = x1
    ms = jnp.mean(x1 * x1, axis=-1, keepdims=True)
    h2 = (x1 * lax.rsqrt(ms + NORM_EPS) * g2_ref[...]) * (1.0 + scale2) + shift2
    _store_row_tiles(h2_ref, h2)

    h2_hi = h2.astype(BF16)
    h2_lo = (h2 - h2_hi.astype(F32)).astype(BF16)
    logits = (jnp.dot(h2_hi, wrh_ref[...], preferred_element_type=F32)
              + jnp.dot(h2_lo, wrh_ref[...], preferred_element_type=F32)
              + jnp.dot(h2_hi, wrl_ref[...], preferred_element_type=F32)) + br_ref[...]
    lane = lax.broadcasted_iota(I32, (tm, LANES), 1).astype(F32)
    ninf = -jnp.inf
    big = float(LANES)

    def first_argmax(v):
        vmax = jnp.max(v, axis=1, keepdims=True)
        idx = jnp.min(jnp.where(v == vmax, lane, big), axis=1, keepdims=True)
        return vmax, idx

    gl = jnp.where(lane < N_GROUPS, logits, ninf)
    gmax, gidx = first_argmax(gl)
    p_top = 1.0 / jnp.sum(jnp.exp(gl - gmax), axis=1, keepdims=True)
    lo = N_GROUPS + EXPERTS_PER_GROUP * gidx
    el = jnp.where((lane >= lo) & (lane < lo + EXPERTS_PER_GROUP), logits, ninf)
    e1max, e1 = first_argmax(el)
    e2max, e2 = first_argmax(jnp.where(lane == e1, ninf, el))
    t = jnp.exp(e2max - e1max)
    w1 = 1.0 / (1.0 + t)
    w2 = t / (1.0 + t)
    col = lax.broadcasted_iota(I32, (tm, ROUTE_W), 1)
    rf_ref[...] = jnp.where(col == 0, p_top * w1, jnp.where(col == 1, p_top * w2, 0.0))

    x1id = e1 - N_GROUPS
    x2id = e2 - N_GROUPS
    hit1 = lane == x1id
    hit2 = lane == x2id
    onehot = jnp.where(hit1 | hit2, 1.0, 0.0)
    before = jnp.dot(tri_ref[...], onehot.astype(BF16), preferred_element_type=F32) + carry_ref[...]
    r1 = jnp.sum(jnp.where(hit1, before, 0.0), axis=1, keepdims=True)
    r2 = jnp.sum(jnp.where(hit2, before, 0.0), axis=1, keepdims=True)
    carry_ref[...] = carry_ref[...] + jnp.sum(onehot, axis=0, keepdims=True)
    cnt_ref[...] = carry_ref[...]
    cols = jnp.where(lane == 0.0, x1id, jnp.where(lane == 1.0, x2id,
                                                  jnp.where(lane == 2.0, r1,
                                                            jnp.where(lane == 3.0, r2, 0.0))))
    rank_ref[0] = cols.T[0:ROUTE_W, :].astype(I32)


def _merge_call(x2, o_a, o_b, sa, sb, mod, norm2_g, wa, wb, wo, wr_hi, wr_lo, br):
    nb = SEQ // TM_MERGE
    tok = lambda w: pl.BlockSpec((TM_MERGE, w), lambda i: (i, 0))
    const2 = lambda r, c: pl.BlockSpec((r, c), lambda i: (0, 0))
    return pl.pallas_call(
        _merge_kernel,
        grid=(TOKENS // TM_MERGE,),
        in_specs=[tok(D_MODEL), tok(GLA_V_W), tok(DIFF_V_W), tok(D_MODEL), tok(D_MODEL),
                  pl.BlockSpec((1, N_MOD, D_MODEL), lambda i: (i // nb, 0, 0)),
                  const2(1, D_MODEL),
                  const2(GLA_V_W, D_MODEL), const2(DIFF_V_W, D_MODEL), const2(D_MODEL, D_MODEL),
                  const2(D_MODEL, LANES), const2(D_MODEL, LANES), const2(1, LANES)],
        out_specs=(tok(D_MODEL), _row_tile_spec(TM_MERGE, lambda i: (i, 0)),
                   tok(ROUTE_W),
                   pl.BlockSpec((1, ROUTE_W, TM_MERGE), lambda i: (i, 0, 0)),
                   pl.BlockSpec((1, LANES), lambda i: (0, 0))),
        out_shape=(jax.ShapeDtypeStruct((TOKENS, D_MODEL), F32),
                   jax.ShapeDtypeStruct((TOKENS * ROW_TILE_S, LANES), F32),
                   jax.ShapeDtypeStruct((TOKENS, ROUTE_W), F32),
                   jax.ShapeDtypeStruct((TOKENS // TM_MERGE, ROUTE_W, TM_MERGE), I32),
                   jax.ShapeDtypeStruct((1, LANES), F32)),
        scratch_shapes=[pltpu.VMEM((1, LANES), F32), pltpu.VMEM((TM_MERGE, TM_MERGE), BF16)],
        compiler_params=_cparams(("arbitrary",)),
        name="merge",
    )(x2, o_a, o_b, sa, sb, mod, norm2_g, wa, wb, wo, wr_hi, wr_lo, br)


def _plan_kernel(rk_ref, cnt_ref, dest_ref, blk_ref):
    n_tiles, _, tm = rk_ref.shape
    cnt = cnt_ref[...]
    padded = jnp.floor((cnt + (MOE_BLK - 1)) * (1.0 / MOE_BLK)) * MOE_BLK
    lane = lax.broadcasted_iota(I32, (1, LANES), 1)
    seg_end = padded
    step = 1
    while step < N_EXPERTS:
        seg_end = seg_end + jnp.where(lane >= step, pltpu.roll(seg_end, step, 1), 0.0)
        step *= 2
    seg_start = seg_end - padded
    valid_end = seg_start + cnt

    ei = lax.broadcasted_iota(I32, (LANES, LANES), 0)
    ej = lax.broadcasted_iota(I32, (LANES, LANES), 1)

    def to_col(rowv):
        return jnp.sum(jnp.where(ei == ej, rowv, 0.0), axis=1, keepdims=True)

    start_col, end_col, valid_col = to_col(seg_start), to_col(seg_end), to_col(valid_end)

    nb_pad = blk_ref.shape[1]
    e_sub = lax.broadcasted_iota(I32, (LANES, nb_pad), 0)
    b_start = (lax.broadcasted_iota(I32, (1, nb_pad), 1) * MOE_BLK).astype(F32)
    ends_before = jnp.where((e_sub < N_EXPERTS) & (end_col <= b_start), 1.0, 0.0)
    block_e = jnp.minimum(jnp.sum(ends_before, axis=0, keepdims=True), N_EXPERTS - 1.0)
    block_valid_end = jnp.sum(jnp.where(e_sub.astype(F32) == block_e, valid_col, 0.0),
                              axis=0, keepdims=True)
    n_valid = jnp.clip(block_valid_end - b_start, 0.0, float(MOE_BLK))
    n_used = jnp.max(seg_end, axis=1, keepdims=True) * (1.0 / MOE_BLK)
    nonempty = (to_col(cnt) > 0.0) & (e_sub < N_EXPERTS)
    e_subf = e_sub.astype(F32)
    no_next = float(LANES)
    nxt = jnp.min(jnp.where(nonempty & (e_subf > block_e), e_subf, no_next), axis=0, keepdims=True)
    nxt = jnp.where(nxt == no_next, -1.0, nxt)
    seg_idx = jnp.sum(jnp.where(nonempty & (e_subf < block_e), 1.0, 0.0), axis=0, keepdims=True)
    slot = seg_idx - 2.0 * jnp.floor(seg_idx * 0.5)
    blk_ref[...] = jnp.concatenate(
        [block_e, n_valid, jnp.broadcast_to(n_used, (1, nb_pad)), nxt, slot,
         jnp.zeros((ROUTE_W - 5, nb_pad), F32)], axis=0).astype(I32)

    e_tok = lax.broadcasted_iota(I32, (LANES, tm), 0)

    def tile(t, carry):
        rk = rk_ref[t]
        d = [jnp.sum(jnp.where(e_tok == rk[k:k + 1], start_col, 0.0), axis=0, keepdims=True)
             + rk[k + 2:k + 3].astype(F32) for k in range(TOP_K)]
        dest_ref[t] = jnp.concatenate(d + [jnp.zeros((ROUTE_W - TOP_K, tm), F32)], axis=0).astype(I32)
        return carry

    lax.fori_loop(0, n_tiles, tile, 0)


def _plan_call(rank_rows, cnt):
    n_tiles = TOKENS // TM_MERGE
    nb_pad = -(-N_BLOCKS // LANES) * LANES
    dest, blk = pl.pallas_call(
        _plan_kernel,
        out_shape=(jax.ShapeDtypeStruct((n_tiles, ROUTE_W, TM_MERGE), I32),
                   jax.ShapeDtypeStruct((ROUTE_W, nb_pad), I32)),
        compiler_params=pltpu.CompilerParams(vmem_limit_bytes=VMEM_LIMIT),
        name="plan",
    )(rank_rows, cnt)
    dest1 = dest[:, 0, :].reshape(TOKENS)
    dest2 = dest[:, 1, :].reshape(TOKENS)
    tables = tuple(blk[r, :N_BLOCKS] for r in (0, 1, 3, 4))
    return dest1, dest2, tables, blk[2, :1]


def _sc_params():
    return pltpu.CompilerParams(use_tc_tiling_on_sc=True)


def _sc_mesh():
    return plsc.VectorSubcoreMesh(core_axis_name="core", subcore_axis_name="subcore")


def _sc_worker_base(per_worker):
    wid = lax.axis_index("subcore") * SC_CORES + lax.axis_index("core")
    return wid * per_worker


def _sc_scatter_rows(src, dest1, dest2, n_out):
    n = src.shape[0]
    per_worker = n // SC_WORKERS
    assert per_worker * SC_WORKERS == n and per_worker % SC_CHUNK == 0

    def body(src_hbm, d1_hbm, d2_hbm, out_hbm, idx_v, rows_v):
        base = _sc_worker_base(per_worker)

        @pl.loop(0, per_worker // SC_CHUNK)
        def _(j):
            start = pl.multiple_of(base + j * SC_CHUNK, SC_CHUNK)
            pltpu.sync_copy(src_hbm.at[pl.ds(start, SC_CHUNK)], rows_v)
            for d_hbm in (d1_hbm, d2_hbm):
                pltpu.sync_copy(d_hbm.at[pl.ds(start, SC_CHUNK)], idx_v)
                pltpu.sync_copy(rows_v, out_hbm.at[idx_v])

    return pl.kernel(
        body,
        out_type=jax.ShapeDtypeStruct((n_out, ROW_TILE_S, LANES), F32),
        mesh=_sc_mesh(),
        scratch_types=[pltpu.VMEM((SC_CHUNK,), I32),
                       pltpu.VMEM((SC_CHUNK, ROW_TILE_S, LANES), F32)],
        compiler_params=_sc_params(),
        name="sc_dispatch",
    )(src, dest1, dest2)


def _sc_gather_rows(table, idx):
    n = idx.shape[0]
    per_worker = n // SC_WORKERS
    half = SC_CHUNK // 2
    assert per_worker * SC_WORKERS == n and per_worker % SC_CHUNK == 0

    def body(table_hbm, idx_hbm, out_hbm, idx_v, rows_a, rows_b, gsem_a, gsem_b, wsem_a, wsem_b):
        base = pl.multiple_of(_sc_worker_base(per_worker), SC_CHUNK)
        pltpu.sync_copy(idx_hbm.at[pl.ds(base, per_worker)], idx_v)

        @pl.loop(0, per_worker // SC_CHUNK)
        def _(j):
            off_a = pl.multiple_of(j * SC_CHUNK, SC_CHUNK)
            off_b = pl.multiple_of(j * SC_CHUNK + half, half)
            ga = pltpu.async_copy(table_hbm.at[idx_v.at[pl.ds(off_a, half)]], rows_a, gsem_a)
            gb = pltpu.async_copy(table_hbm.at[idx_v.at[pl.ds(off_b, half)]], rows_b, gsem_b)
            ga.wait()
            wa = pltpu.async_copy(rows_a, out_hbm.at[pl.ds(base + off_a, half)], wsem_a)
            gb.wait()
            wb = pltpu.async_copy(rows_b, out_hbm.at[pl.ds(base + off_b, half)], wsem_b)
            wa.wait()
            wb.wait()

    return pl.kernel(
        body,
        out_type=jax.ShapeDtypeStruct((n, ROW_TILE_S, LANES), F32),
        mesh=_sc_mesh(),
        scratch_types=[pltpu.VMEM((per_worker,), I32),
                       pltpu.VMEM((half, ROW_TILE_S, LANES), F32),
                       pltpu.VMEM((half, ROW_TILE_S, LANES), F32),
                       pltpu.SemaphoreType.DMA, pltpu.SemaphoreType.DMA,
                       pltpu.SemaphoreType.DMA, pltpu.SemaphoreType.DMA],
        compiler_params=_sc_params(),
        name="sc_gather",
    )(table, idx)


def _weight_fetch(w_hbm, stage, sem, e, slot):
    return [pltpu.make_async_copy(w.at[e], st.at[slot], sem.at[slot]) for w, st in zip(w_hbm, stage)]


def _expert_kernel(be_ref, nv_ref, nx_ref, sl_ref, nu_ref, xs_ref, w1_hbm, w3_hbm, w2_hbm, y_ref,
                   w1s, w3s, w2s, w1b, w3b, w2b, sem):
    i = pl.program_id(0)
    e = be_ref[i]
    used = i < nu_ref[0]
    first = (i == 0) | (e != be_ref[jnp.maximum(i - 1, 0)])
    w_hbm = (w1_hbm, w3_hbm, w2_hbm)
    stage = (w1s, w3s, w2s)

    @pl.when(used & first)
    def _():
        slot = sl_ref[i]

        @pl.when(i == 0)
        def _():
            for cp in _weight_fetch(w_hbm, stage, sem, e, slot):
                cp.start()

        for cp in _weight_fetch(w_hbm, stage, sem, e, slot):
            cp.wait()
        w1b[...] = w1s[slot].astype(BF16)
        w3b[...] = w3s[slot].astype(BF16)
        w2b[...] = w2s[slot].astype(BF16)

        @pl.when(nx_ref[i] >= 0)
        def _():
            for cp in _weight_fetch(w_hbm, stage, sem, nx_ref[i], 1 - slot):
                cp.start()

    @pl.when(used)
    def _():
        row = lax.broadcasted_iota(I32, (MOE_BLK, D_MODEL), 0)
        xb = jnp.where(row < nv_ref[i], _load_row_tiles(xs_ref), 0.0).astype(BF16)
        a = jnp.dot(xb, w1b[...], preferred_element_type=F32)
        g = jnp.dot(xb, w3b[...], preferred_element_type=F32)
        hmid = (a * _sigmoid(a)) * g
        _store_row_tiles(y_ref, jnp.dot(hmid.astype(BF16), w2b[...], preferred_element_type=F32))

    @pl.when(jnp.logical_not(used))
    def _():
        y_ref[...] = jnp.zeros_like(y_ref)


def _expert_call(tables, n_used, xs, w1, w3, w2):
    n_tab = len(tables)
    grid_spec = pltpu.PrefetchScalarGridSpec(
        num_scalar_prefetch=n_tab + 1,
        grid=(N_BLOCKS,),
        in_specs=[_row_tile_spec(MOE_BLK, lambda i, *pf: (jnp.minimum(i, pf[n_tab][0] - 1), 0)),
                  pl.BlockSpec(memory_space=pl.ANY),
                  pl.BlockSpec(memory_space=pl.ANY),
                  pl.BlockSpec(memory_space=pl.ANY)],
        out_specs=_row_tile_spec(MOE_BLK, lambda i, *pf: (i, 0)),
        scratch_shapes=[pltpu.VMEM((2, D_MODEL, D_EXPERT), F32),
                        pltpu.VMEM((2, D_MODEL, D_EXPERT), F32),
                        pltpu.VMEM((2, D_EXPERT, D_MODEL), F32),
                        pltpu.VMEM((D_MODEL, D_EXPERT), BF16),
                        pltpu.VMEM((D_MODEL, D_EXPERT), BF16),
                        pltpu.VMEM((D_EXPERT, D_MODEL), BF16),
                        pltpu.SemaphoreType.DMA((2,))],
    )
    return pl.pallas_call(
        _expert_kernel,
        grid_spec=grid_spec,
        out_shape=jax.ShapeDtypeStruct((N_BLOCKS * MOE_BLK * ROW_TILE_S, LANES), F32),
        compiler_params=_cparams(("arbitrary",)),
        name="experts",
    )(*tables, n_used, xs, w1, w3, w2)


def _combine_rows_kernel(x1_ref, rf_ref, mod_ref, ya_ref, yb_ref, *rest):
    o_ref = rest[-1]
    rf = rf_ref[...]
    gate2 = mod_ref[0, 5:6, :]
    moe = rf[:, 0:1] * _load_row_tiles(ya_ref) + rf[:, 1:2] * _load_row_tiles(yb_ref)
    o_ref[...] = x1_ref[...] + gate2 * moe


def _combine_rows_call(x1, rf, mod, yg, part, prev=None):
    nb = SEQ // TM_ROW
    n_half = TOKENS // TM_ROW // 2
    t0 = part * n_half
    in_specs = [pl.BlockSpec((TM_ROW, D_MODEL), lambda i: (t0 + i, 0)),
                pl.BlockSpec((TM_ROW, ROUTE_W), lambda i: (t0 + i, 0)),
                pl.BlockSpec((1, N_MOD, D_MODEL), lambda i: ((t0 + i) // nb, 0, 0)),
                _row_tile_spec(TM_ROW, lambda i: (i, 0)),
                _row_tile_spec(TM_ROW, lambda i: (i + n_half, 0))]
    args = [x1, rf, mod, yg, yg]
    aliases = {}
    if prev is not None:
        in_specs.append(pl.BlockSpec(memory_space=pl.ANY))
        args.append(prev)
        aliases = {len(args) - 1: 0}
    return pl.pallas_call(
        _combine_rows_kernel,
        grid=(n_half,),
        in_specs=in_specs,
        out_specs=pl.BlockSpec((TM_ROW, D_MODEL), lambda i: (t0 + i, 0)),
        out_shape=jax.ShapeDtypeStruct((TOKENS, D_MODEL), F32),
        input_output_aliases=aliases,
        compiler_params=_cparams(("parallel",)),
        name="combine",
    )(*args)


def kernel(x, c, positions, ada_w, ada_b, norm1_g, norm2_g, w_in, gla_alpha_up, gla_alpha_b,
           gla_out_g, diff_q_g, diff_k_g, diff_lq1, diff_lk1, diff_lq2, diff_lk2, diff_out_g,
           w_branch_a, w_branch_b, w_out, router_group_w, router_group_b, router_expert_w,
           router_expert_b, expert_w1, expert_w3, expert_w2):
    assert x.shape == (BATCH, SEQ, D_MODEL) and ada_w.shape[0] == 1
    x2 = x.reshape(TOKENS, D_MODEL)

    w_in_p = _wprep_call(w_in[0])
    au_pad = jnp.zeros((LR_PAD, GLA_QK_W), F32).at[:GLA_GATE_RANK].set(gla_alpha_up[0])
    au_hi = au_pad.astype(BF16)
    au_lo = (au_pad - au_hi.astype(F32)).astype(BF16)
    gid = jnp.arange(DIFF_QK_W) // DIFF_DH
    bd = jnp.where(gid[:, None] == gid[None, :], 1.0 / DIFF_DH, 0.0).astype(BF16)
    qg_row = jnp.tile(diff_q_g[0], DIFF_QK_W // DIFF_DH).reshape(1, DIFF_QK_W)
    kg_row = jnp.tile(diff_k_g[0], DIFF_QK_W // DIFF_DH).reshape(1, DIFF_QK_W)
    inv_freq = ROPE_THETA ** (-jnp.arange(ROPE_HALF, dtype=F32) / ROPE_HALF)
    invf64 = jnp.concatenate([inv_freq, inv_freq, jnp.zeros((DIFF_DH - ROPE_DIM,), F32)])
    invf_row = jnp.tile(invf64, LANES // DIFF_DH).reshape(1, LANES)
    pos_col = positions.reshape(TOKENS, 1)
    lam_p = jnp.concatenate([diff_lq1, diff_lk1, diff_lq2, diff_lk2], axis=0)
    wr = (jnp.zeros((D_MODEL, LANES), F32)
          .at[:, :N_GROUPS].set(router_group_w[0])
          .at[:, N_GROUPS:N_GROUPS + N_EXPERTS].set(router_expert_w[0]))
    wr_hi = wr.astype(BF16)
    wr_lo = (wr - wr_hi.astype(F32)).astype(BF16)
    br = (jnp.zeros((1, LANES), F32)
          .at[0, :N_GROUPS].set(router_group_b[0])
          .at[0, N_GROUPS:N_GROUPS + N_EXPERTS].set(router_expert_b[0]))

    mod = _mod_call(c, ada_w[0], ada_b[0])
    gqk, gv, gr, lr, qT, kk, vT, sa, sb = _in_call(
        x2, mod, norm1_g, w_in_p, bd, qg_row, kg_row, pos_col, invf_row)
    o_a = _gla_call(gqk, gv, gr, lr, au_hi, au_lo, gla_alpha_b, gla_out_g)
    o_b = _attn_call(qT, kk, vT, lam_p, diff_out_g.reshape(DIFF_DV, 1))
    x1, h2, rf, rank, cnt = _merge_call(
        x2, o_a, o_b, sa, sb, mod, norm2_g, w_branch_a[0].astype(BF16),
        w_branch_b[0].astype(BF16), w_out[0].astype(BF16), wr_hi, wr_lo, br)
    dest1, dest2, tables, n_used = _plan_call(rank, cnt)

    xs = _sc_scatter_rows(_as_row_tiles(h2), dest1, dest2, N_BLOCKS * MOE_BLK)
    y = _expert_call(tables, n_used, _as_2d(xs), expert_w1[0], expert_w3[0], expert_w2[0])
    y3 = _as_row_tiles(y)
    half = TOKENS // 2
    out = None
    for part in range(2):
        tok = slice(part * half, (part + 1) * half)
        yg = _sc_gather_rows(y3, jnp.concatenate([dest1[tok], dest2[tok]]))
        out = _combine_rows_call(x1, rf, mod, _as_2d(yg), part, out)
    return out.reshape(BATCH, SEQ, D_MODEL)
```

```python
import math

import jax
import jax.numpy as jnp
from jax import lax
from jax.experimental import pallas as pl
from jax.experimental.pallas import tpu as pltpu
from jax.experimental.pallas import tpu_sc as plsc

F32 = jnp.float32
BF16 = jnp.bfloat16
I32 = jnp.int32

D_MODEL = 1024
BATCH = 4
SEQ = 4096
TOKENS = BATCH * SEQ
N_MOD = 6
NORM_EPS = 1e-6

GLA_HEADS = 4
GLA_DK = 64
GLA_DV = 128
GLA_GATE_RANK = 16
GLA_GATE_TAU = 16.0
GLA_CHUNK = 64
GLA_QK_W = GLA_HEADS * GLA_DK
GLA_V_W = GLA_HEADS * GLA_DV

DIFF_HEADS = 4
DIFF_DH = 64
DIFF_DV = 2 * DIFF_DH
DIFF_QK_W = DIFF_HEADS * 2 * DIFF_DH
DIFF_V_W = DIFF_HEADS * DIFF_DV
ROPE_THETA = 500000.0
ROPE_DIM = DIFF_DH // 4
ROPE_HALF = ROPE_DIM // 2
NEG_INF = -1e30
LAMBDA_INIT = 0.8 - 0.6 * 1.0

N_GROUPS = 4
EXPERTS_PER_GROUP = 8
N_EXPERTS = N_GROUPS * EXPERTS_PER_GROUP
TOP_K = 2
D_EXPERT = 512

LANES = 128
SUBLANES = 8
ROW_TILE_S = D_MODEL // LANES
SC_CORES = 2
SC_SUBCORES = 16
SC_WORKERS = SC_CORES * SC_SUBCORES
SC_CHUNK = 64
LR_PAD = LANES
D_IN_PAD = 2 * GLA_QK_W + 2 * GLA_V_W + 2 * DIFF_QK_W + DIFF_V_W + 2 * D_MODEL + LR_PAD

TM_IN = 512
TQ = 512
V_ROWS = DIFF_DV + 16
ATT_SUB = 32
Q_SCALE = DIFF_DH ** -0.5 * math.log2(math.e)
N_KV = SEQ // TQ
TT_GLA = 512
TM_MERGE = 512
TM_ROW = 256
MOE_BLK = 256
N_BLOCKS = (TOKENS * TOP_K + N_EXPERTS * (MOE_BLK - 1) + MOE_BLK - 1) // MOE_BLK
ROUTE_W = 8

VMEM_LIMIT = 56 * 1024 * 1024


def _cparams(sem):
    return pltpu.CompilerParams(dimension_semantics=sem, vmem_limit_bytes=VMEM_LIMIT)


def _sigmoid(x):
    return 1.0 / (1.0 + jnp.exp(-x))


def _row_tile_spec(rows, index_map):
    return pl.BlockSpec((rows * ROW_TILE_S, LANES), index_map)


def _as_row_tiles(a2d):
    return a2d.reshape(a2d.shape[0] // ROW_TILE_S, ROW_TILE_S, LANES)


def _as_2d(a3d):
    return a3d.reshape(a3d.shape[0] * ROW_TILE_S, LANES)


def _store_row_tiles(ref, val):
    for s in range(ROW_TILE_S):
        ref[pl.ds(s, val.shape[0], stride=ROW_TILE_S), :] = val[:, s * LANES:(s + 1) * LANES]


def _load_row_tiles(ref):
    rows = ref.shape[0] // ROW_TILE_S
    return jnp.concatenate(
        [ref[pl.ds(s, rows, stride=ROW_TILE_S), :] for s in range(ROW_TILE_S)], axis=1)


def _mod_kernel(ct_ref, w_ref, b_ref, o_ref):
    ct = ct_ref[...]
    ca = ct * _sigmoid(ct)
    w = w_ref[...]
    rows = []
    for b in range(BATCH):
        rows.append(jnp.sum(w * ca[:, b:b + 1], axis=0, keepdims=True) + b_ref[...])
    rows.append(jnp.zeros((8 - BATCH, w.shape[1]), F32))
    o_ref[...] = jnp.concatenate(rows, axis=0)


def _mod_call(c, ada_w, ada_b):
    tn = D_MODEL
    ct = jnp.zeros((D_MODEL, 8), F32).at[:, :BATCH].set(c.T)
    out = pl.pallas_call(
        _mod_kernel,
        grid=(N_MOD,),
        in_specs=[
            pl.BlockSpec((D_MODEL, 8), lambda j: (0, 0)),
            pl.BlockSpec((D_MODEL, tn), lambda j: (0, j)),
            pl.BlockSpec((1, tn), lambda j: (0, j)),
        ],
        out_specs=pl.BlockSpec((8, tn), lambda j: (0, j)),
        out_shape=jax.ShapeDtypeStruct((8, N_MOD * D_MODEL), F32),
        compiler_params=_cparams(("arbitrary",)),
        name="mod",
    )(ct, ada_w, ada_b.reshape(1, N_MOD * D_MODEL))
    return out[:BATCH].reshape(BATCH, N_MOD, D_MODEL)


def _wprep_kernel(w_ref, o_ref):
    w = w_ref[...]
    lr0 = 2 * GLA_QK_W + 2 * GLA_V_W
    rest = D_IN_PAD - LR_PAD - lr0
    o_ref[:, :lr0] = w[:, :lr0].astype(BF16)
    o_ref[:, lr0:lr0 + rest] = w[:, lr0 + GLA_GATE_RANK:].astype(BF16)
    lr = jnp.concatenate([w[:, lr0:lr0 + GLA_GATE_RANK],
                          jnp.zeros((w.shape[0], LR_PAD - GLA_GATE_RANK), F32)], axis=1)
    o_ref[:, lr0 + rest:] = lr.astype(BF16)


def _wprep_call(w_in0):
    d_in = w_in0.shape[1]
    rows = 128
    return pl.pallas_call(
        _wprep_kernel,
        grid=(D_MODEL // rows,),
        in_specs=[pl.BlockSpec((rows, d_in), lambda i: (i, 0))],
        out_specs=pl.BlockSpec((rows, D_IN_PAD), lambda i: (i, 0)),
        out_shape=jax.ShapeDtypeStruct((D_MODEL, D_IN_PAD), BF16),
        compiler_params=_cparams(("parallel",)),
        name="w_prep",
    )(w_in0)


def _in_kernel(x_ref, mod_ref, g1_ref, w_ref, bd_ref, qg_ref, kg_ref, pos_ref, invf_ref,
               gqk_ref, gv_ref, gr_ref, lr_ref, qT_ref, k_ref, vT_ref, sa_ref, sb_ref,
               cos_scr, sin_scr):
    ang = pos_ref[...].astype(F32) * invf_ref[...]
    cos_scr[...] = jnp.cos(ang)
    sin_scr[...] = jnp.sin(ang)

    x = x_ref[...]
    shift1 = mod_ref[0, 0:1, :]
    scale1 = mod_ref[0, 1:2, :]
    ms = jnp.mean(x * x, axis=-1, keepdims=True)
    h = (x * lax.rsqrt(ms + NORM_EPS) * g1_ref[...]) * (1.0 + scale1) + shift1
    hb = h.astype(BF16)

    def proj(c0, c1):
        return jnp.dot(hb, w_ref[:, c0:c1], preferred_element_type=F32)

    tm = x.shape[0]
    cos4 = jnp.concatenate([cos_scr[...]] * DIFF_HEADS, axis=1)
    sin4 = jnp.concatenate([sin_scr[...]] * DIFF_HEADS, axis=1)
    lane = lax.broadcasted_iota(I32, (tm, DIFF_QK_W), 1)
    first_half = (lane % DIFF_DH) < ROPE_HALF
    bd = bd_ref[...]

    def norm_rope(t, gain_row):
        t2 = t * t
        hi = t2.astype(BF16)
        lo = (t2 - hi.astype(F32)).astype(BF16)
        gms = (jnp.dot(hi, bd, preferred_element_type=F32)
               + jnp.dot(lo, bd, preferred_element_type=F32))
        t = t * lax.rsqrt(gms + NORM_EPS) * gain_row
        nxt = pltpu.roll(t, DIFF_QK_W - ROPE_HALF, 1)
        prv = pltpu.roll(t, ROPE_HALF, 1)
        return t * cos4 + jnp.where(first_half, -nxt, prv) * sin4

    dq_raw = proj(1536, 2048)
    dk_raw = proj(2048, 2560)
    gqk_ref[...] = proj(0, 512).astype(BF16)
    dq = norm_rope(dq_raw, qg_ref[...]) * Q_SCALE
    gv_ref[...] = proj(512, 1024).astype(BF16)
    dk = norm_rope(dk_raw, kg_ref[...])
    dv = proj(2560, 3072)
    for hd in range(DIFF_HEADS):
        sl = slice(hd * LANES, (hd + 1) * LANES)
        qT_ref[0, hd, 0] = dq[:, sl].T.astype(BF16)
        k_ref[0, hd, 0] = dk[:, sl].astype(BF16)
    gr_ref[...] = proj(1024, 1536).astype(BF16)
    for hd in range(DIFF_HEADS):
        sl = slice(hd * LANES, (hd + 1) * LANES)
        vT_ref[0, hd, 0] = jnp.concatenate(
            [dv[:, sl].T, jnp.ones((V_ROWS - DIFF_DV, tm), F32)], axis=0).astype(BF16)
    sa_ref[...] = _sigmoid(proj(3072, 4096)).astype(BF16)
    sb_ref[...] = _sigmoid(proj(4096, 5120)).astype(BF16)
    lr_ref[...] = proj(5120, 5248)


def _in_call(x2, mod, norm1_g, w_in_p, bd, qg_row, kg_row, pos_col, invf_row):
    nb = SEQ // TM_IN
    tok_spec = lambda w: pl.BlockSpec((TM_IN, w), lambda i: (i, 0))
    const2 = lambda r, c: pl.BlockSpec((r, c), lambda i: (0, 0))
    out_shapes = (
        jax.ShapeDtypeStruct((TOKENS, 2 * GLA_QK_W), BF16),
        jax.ShapeDtypeStruct((TOKENS, GLA_V_W), BF16),
        jax.ShapeDtypeStruct((TOKENS, GLA_V_W), BF16),
        jax.ShapeDtypeStruct((TOKENS, LR_PAD), F32),
        jax.ShapeDtypeStruct((BATCH, DIFF_HEADS, N_KV, LANES, TQ), BF16),
        jax.ShapeDtypeStruct((BATCH, DIFF_HEADS, N_KV, TQ, LANES), BF16),
        jax.ShapeDtypeStruct((BATCH, DIFF_HEADS, N_KV, V_ROWS, TQ), BF16),
        jax.ShapeDtypeStruct((TOKENS, D_MODEL), BF16),
        jax.ShapeDtypeStruct((TOKENS, D_MODEL), BF16),
    )
    out_specs = (
        tok_spec(2 * GLA_QK_W), tok_spec(GLA_V_W), tok_spec(GLA_V_W), tok_spec(LR_PAD),
        pl.BlockSpec((1, DIFF_HEADS, 1, LANES, TM_IN), lambda i: (i // nb, 0, i % nb, 0, 0)),
        pl.BlockSpec((1, DIFF_HEADS, 1, TM_IN, LANES), lambda i: (i // nb, 0, i % nb, 0, 0)),
        pl.BlockSpec((1, DIFF_HEADS, 1, V_ROWS, TM_IN), lambda i: (i // nb, 0, i % nb, 0, 0)),
        tok_spec(D_MODEL), tok_spec(D_MODEL),
    )
    return pl.pallas_call(
        _in_kernel,
        grid=(TOKENS // TM_IN,),
        in_specs=[
            tok_spec(D_MODEL),
            pl.BlockSpec((1, N_MOD, D_MODEL), lambda i: (i // nb, 0, 0)),
            const2(1, D_MODEL),
            pl.BlockSpec((D_MODEL, D_IN_PAD), lambda i: (0, 0), pipeline_mode=pl.Buffered(1)),
            const2(DIFF_QK_W, DIFF_QK_W),
            const2(1, DIFF_QK_W), const2(1, DIFF_QK_W),
            tok_spec(1),
            const2(1, LANES),
        ],
        out_specs=out_specs,
        out_shape=out_shapes,
        scratch_shapes=[pltpu.VMEM((TM_IN, LANES), F32), pltpu.VMEM((TM_IN, LANES), F32)],
        compiler_params=_cparams(("parallel",)),
        name="in_proj",
    )(x2, mod, norm1_g, w_in_p, bd, qg_row, kg_row, pos_col, invf_row)


def _gla_kernel(qk_ref, v_ref, r_ref, lr_ref, auh_ref, aul_ref, ab_ref, og_ref, o_ref,
                state_ref, oacc_ref, snew_ref):
    tt = qk_ref.shape[0]
    n_chunks = tt // GLA_CHUNK

    @pl.when(pl.program_id(1) == 0)
    def _():
        state_ref[...] = jnp.zeros_like(state_ref)

    lr = lr_ref[...]
    lr_hi = lr.astype(BF16)
    lr_lo = (lr - lr_hi.astype(F32)).astype(BF16)
    z = (jnp.dot(lr_hi, auh_ref[...], preferred_element_type=F32)
         + jnp.dot(lr_lo, auh_ref[...], preferred_element_type=F32)
         + jnp.dot(lr_hi, aul_ref[...], preferred_element_type=F32)) + ab_ref[...]
    g = (jnp.minimum(z, 0.0) - jnp.log(1.0 + jnp.exp(-jnp.abs(z)))) * (1.0 / GLA_GATE_TAU)

    row = lax.broadcasted_iota(I32, (tt, GLA_QK_W), 0) % GLA_CHUNK
    b = g
    step = 1
    while step < GLA_CHUNK:
        b = b + jnp.where(row >= step, pltpu.roll(b, step, 0), 0.0)
        step *= 2

    b_last_rows = [b[c * GLA_CHUNK + GLA_CHUNK - 1:(c + 1) * GLA_CHUNK, :] for c in range(n_chunks)]
    b_last = jnp.concatenate(
        [jnp.broadcast_to(bl, (GLA_CHUNK, GLA_QK_W)) for bl in b_last_rows], axis=0)

    qk = qk_ref[...].astype(F32)
    q = qk[:, :GLA_QK_W] * (GLA_DK ** -0.5)
    k = qk[:, GLA_QK_W:]
    q_in = (q * jnp.exp(b)).astype(BF16)
    k_in = (k * jnp.exp(-b)).astype(BF16)
    k_dec = (k * jnp.exp(b_last - b)).astype(BF16)

    ci = lax.broadcasted_iota(I32, (GLA_CHUNK, GLA_CHUNK), 0)
    cj = lax.broadcasted_iota(I32, (GLA_CHUNK, GLA_CHUNK), 1)
    causal = ci >= cj

    pairs = [(c, hd) for c in range(n_chunks) for hd in range(GLA_HEADS)]

    def rows(c):
        return slice(c * GLA_CHUNK, (c + 1) * GLA_CHUNK)

    def kcols(hd):
        return slice(hd * GLA_DK, (hd + 1) * GLA_DK)

    def vcols(hd):
        return slice(hd * GLA_DV, (hd + 1) * GLA_DV)

    att = {}
    for c, hd in pairs:
        a = lax.dot_general(q_in[rows(c), kcols(hd)], k_in[rows(c), kcols(hd)],
                            (((1,), (1,)), ((), ())), preferred_element_type=F32)
        att[c, hd] = jnp.where(causal, a, 0.0).astype(BF16)
    for c, hd in pairs:
        oacc_ref[rows(c), vcols(hd)] = jnp.dot(att[c, hd], v_ref[rows(c), vcols(hd)],
                                               preferred_element_type=F32)
    for c, hd in pairs:
        snew_ref[c * GLA_HEADS + hd] = lax.dot_general(
            k_dec[rows(c), kcols(hd)], v_ref[rows(c), vcols(hd)],
            (((0,), (0,)), ((), ())), preferred_element_type=F32)

    decay_rows = jnp.exp(jnp.concatenate(
        b_last_rows + [jnp.zeros((LANES - n_chunks, GLA_QK_W), F32)], axis=0))
    decay_cols = decay_rows.T
    states = [state_ref[hd] for hd in range(GLA_HEADS)]
    for c in range(n_chunks):
        for hd in range(GLA_HEADS):
            s_prev = states[hd]
            oacc_ref[rows(c), vcols(hd)] += jnp.dot(
                q_in[rows(c), kcols(hd)], s_prev.astype(BF16), preferred_element_type=F32)
            dcol = decay_cols[kcols(hd), c:c + 1]
            states[hd] = s_prev * dcol + snew_ref[c * GLA_HEADS + hd]
    for hd in range(GLA_HEADS):
        state_ref[hd] = states[hd]

    for hd in range(GLA_HEADS):
        vs = slice(hd * GLA_DV, (hd + 1) * GLA_DV)
        oh = oacc_ref[:, vs]
        ms = jnp.mean(oh * oh, axis=-1, keepdims=True)
        y = oh * lax.rsqrt(ms + NORM_EPS) * og_ref[...]
        r = r_ref[:, vs].astype(F32)
        o_ref[:, vs] = (y * (r * _sigmoid(r))).astype(BF16)


def _gla_call(gqk, gv, gr, lr, au_hi, au_lo, ab_row, og_row):
    nt = SEQ // TT_GLA
    tok = lambda w: pl.BlockSpec((TT_GLA, w), lambda b, t: (b * nt + t, 0))
    const2 = lambda r, c: pl.BlockSpec((r, c), lambda b, t: (0, 0))
    return pl.pallas_call(
        _gla_kernel,
        grid=(BATCH, nt),
        in_specs=[tok(2 * GLA_QK_W), tok(GLA_V_W), tok(GLA_V_W), tok(LR_PAD),
                  const2(LR_PAD, GLA_QK_W), const2(LR_PAD, GLA_QK_W), const2(1, GLA_QK_W),
                  const2(1, GLA_DV)],
        out_specs=tok(GLA_V_W),
        out_shape=jax.ShapeDtypeStruct((TOKENS, GLA_V_W), BF16),
        scratch_shapes=[pltpu.VMEM((GLA_HEADS, GLA_DK, GLA_DV), F32),
                        pltpu.VMEM((TT_GLA, GLA_V_W), F32),
                        pltpu.VMEM((TT_GLA // GLA_CHUNK * GLA_HEADS, GLA_DK, GLA_DV), F32)],
        compiler_params=_cparams(("parallel", "arbitrary")),
        name="gla",
    )(gqk, gv, gr, lr, au_hi, au_lo, ab_row, og_row)


def _attn_kernel(qT_ref, k_ref, vT_ref, lam_ref, og_ref, o_ref,
                 q_scr, s_scr, p_scr, acc_scr, m_scr):
    def q_block(i, carry):
        _attn_q_block(i, qT_ref, k_ref, vT_ref, lam_ref, og_ref, o_ref,
                      q_scr, s_scr, p_scr, acc_scr, m_scr)
        return carry

    lax.fori_loop(0, N_KV, q_block, 0)


def _attn_q_block(i, qT_ref, k_ref, vT_ref, lam_ref, og_ref, o_ref,
                  q_scr, s_scr, p_scr, acc_scr, m_scr):
    qT = qT_ref[0, 0, i]
    rowq = lax.broadcasted_iota(I32, qT.shape, 0)
    zero = jnp.zeros_like(qT)
    q_scr[0] = jnp.where(rowq < DIFF_DH, qT, zero)
    q_scr[1] = jnp.where(rowq >= DIFF_DH, qT, zero)
    m_scr[...] = jnp.full(m_scr.shape, NEG_INF, F32)
    acc_scr[...] = jnp.zeros(acc_scr.shape, F32)
    n_sub = TQ // ATT_SUB

    def fold8(t, op):
        return op(t.reshape(t.shape[0] // SUBLANES, SUBLANES, TQ), axis=0)

    def scores(c, j):
        s_scr[c] = jnp.dot(k_ref[0, 0, j], q_scr[c], preferred_element_type=F32)

    def load_s(c, r, masked):
        s = s_scr[c, r * ATT_SUB:(r + 1) * ATT_SUB, :]
        if masked:
            key_i = lax.broadcasted_iota(I32, (ATT_SUB, TQ), 0) + r * ATT_SUB
            qry_i = lax.broadcasted_iota(I32, (ATT_SUB, TQ), 1)
            s = jnp.where(key_i <= qry_i, s, NEG_INF)
        return s

    def softmax_pv(c, j, masked):
        m8 = fold8(load_s(c, 0, masked), jnp.max)
        for r in range(1, n_sub):
            m8 = jnp.maximum(m8, fold8(load_s(c, r, masked), jnp.max))
        m_old = m_scr[c]
        m_new = jnp.maximum(m_old, jnp.max(m8, axis=0, keepdims=True))
        alpha = jnp.exp2(m_old - m_new)
        for r in range(n_sub):
            p = jnp.exp2(load_s(c, r, masked) - m_new)
            p_scr[c, r * ATT_SUB:(r + 1) * ATT_SUB, :] = p.astype(BF16)
        m_scr[c] = m_new
        acc_scr[c] = acc_scr[c] * alpha + jnp.dot(vT_ref[0, 0, j], p_scr[c],
                                                  preferred_element_type=F32)

    scores(0, 0)

    def body(j, carry):
        scores(1, j)
        softmax_pv(0, j, False)
        scores(0, j + 1)
        softmax_pv(1, j, False)
        return carry

    lax.fori_loop(0, i, body, 0)
    scores(1, i)
    softmax_pv(0, i, True)
    softmax_pv(1, i, True)

    l1 = acc_scr[0, DIFF_DV:DIFF_DV + 1, :]
    l2 = acc_scr[1, DIFF_DV:DIFF_DV + 1, :]

    lam_p = lam_ref[...]
    lam = (jnp.exp(jnp.sum(lam_p[0:1] * lam_p[1:2], axis=1, keepdims=True))
           - jnp.exp(jnp.sum(lam_p[2:3] * lam_p[3:4], axis=1, keepdims=True)) + LAMBDA_INIT)
    oT = acc_scr[0, :DIFF_DV, :] / l1 - lam * (acc_scr[1, :DIFF_DV, :] / l2)
    ms = jnp.mean(oT * oT, axis=0, keepdims=True)
    y = oT * lax.rsqrt(ms + NORM_EPS) * og_ref[...] * (1.0 - LAMBDA_INIT)
    o_ref[pl.ds(pl.multiple_of(i * TQ, TQ), TQ), :] = y.T.astype(BF16)


def _attn_call(qT, kk, vT, lam_p, og_col):
    return pl.pallas_call(
        _attn_kernel,
        grid=(BATCH, DIFF_HEADS),
        in_specs=[
            pl.BlockSpec((1, 1, N_KV, LANES, TQ), lambda b, h: (b, h, 0, 0, 0)),
            pl.BlockSpec((1, 1, N_KV, TQ, LANES), lambda b, h: (b, h, 0, 0, 0)),
            pl.BlockSpec((1, 1, N_KV, V_ROWS, TQ), lambda b, h: (b, h, 0, 0, 0)),
            pl.BlockSpec((4, DIFF_DH), lambda b, h: (0, 0)),
            pl.BlockSpec((DIFF_DV, 1), lambda b, h: (0, 0)),
        ],
        out_specs=pl.BlockSpec((SEQ, DIFF_DV), lambda b, h: (b, h)),
        out_shape=jax.ShapeDtypeStruct((TOKENS, DIFF_V_W), BF16),
        scratch_shapes=[pltpu.VMEM((2, LANES, TQ), BF16),
                        pltpu.VMEM((2, TQ, TQ), F32),
                        pltpu.VMEM((2, TQ, TQ), BF16),
                        pltpu.VMEM((2, V_ROWS, TQ), F32),
                        pltpu.VMEM((2, 1, TQ), F32)],
        compiler_params=_cparams(("parallel", "parallel")),
        name="attn",
    )(qT, kk, vT, lam_p, og_col)


def _merge_kernel(x_ref, oa_ref, ob_ref, sa_ref, sb_ref, mod_ref, g2_ref, wa_ref, wb_ref, wo_ref,
                  wrh_ref, wrl_ref, br_ref, x1_ref, h2_ref, rf_ref, rank_ref, cnt_ref,
                  carry_ref, tri_ref):
    tm = x_ref.shape[0]

    @pl.when(pl.program_id(0) == 0)
    def _():
        carry_ref[...] = jnp.zeros_like(carry_ref)
        ti = lax.broadcasted_iota(I32, (tm, tm), 0)
        tj = lax.broadcasted_iota(I32, (tm, tm), 1)
        tri_ref[...] = jnp.where(ti > tj, 1.0, 0.0).astype(BF16)

    ma = jnp.dot(oa_ref[...], wa_ref[...], preferred_element_type=F32)
    mb = jnp.dot(ob_ref[...], wb_ref[...], preferred_element_type=F32)
    merged = sa_ref[...].astype(F32) * ma + sb_ref[...].astype(F32) * mb
    y = jnp.dot(merged.astype(BF16), wo_ref[...], preferred_element_type=F32)
    gate1 = mod_ref[0, 2:3, :]
    shift2 = mod_ref[0, 3:4, :]
    scale2 = mod_ref[0, 4:5, :]
    x1 = x_ref[...] + gate1 * y
    x1_ref[...] = x1
    ms = jnp.mean(x1 * x1, axis=-1, keepdims=True)
    h2 = (x1 * lax.rsqrt(ms + NORM_EPS) * g2_ref[...]) * (1.0 + scale2) + shift2
    _store_row_tiles(h2_ref, h2)

    h2_hi = h2.astype(BF16)
    h2_lo = (h2 - h2_hi.astype(F32)).astype(BF16)
    logits = (jnp.dot(h2_hi, wrh_ref[...], preferred_element_type=F32)
              + jnp.dot(h2_lo, wrh_ref[...], preferred_element_type=F32)
              + jnp.dot(h2_hi, wrl_ref[...], preferred_element_type=F32)) + br_ref[...]
    lane = lax.broadcasted_iota(I32, (tm, LANES), 1).astype(F32)
    ninf = -jnp.inf
    big = float(LANES)

    def first_argmax(v):
        vmax = jnp.max(v, axis=1, keepdims=True)
        idx = jnp.min(jnp.where(v == vmax, lane, big), axis=1, keepdims=True)
        return vmax, idx

    gl = jnp.where(lane < N_GROUPS, logits, ninf)
    gmax, gidx = first_argmax(gl)
    p_top = 1.0 / jnp.sum(jnp.exp(gl - gmax), axis=1, keepdims=True)
    lo = N_GROUPS + EXPERTS_PER_GROUP * gidx
    el = jnp.where((lane >= lo) & (lane < lo + EXPERTS_PER_GROUP), logits, ninf)
    e1max, e1 = first_argmax(el)
    e2max, e2 = first_argmax(jnp.where(lane == e1, ninf, el))
    t = jnp.exp(e2max - e1max)
    w1 = 1.0 / (1.0 + t)
    w2 = t / (1.0 + t)
    col = lax.broadcasted_iota(I32, (tm, ROUTE_W), 1)
    rf_ref[...] = jnp.where(col == 0, p_top * w1, jnp.where(col == 1, p_top * w2, 0.0))

    x1id = e1 - N_GROUPS
    x2id = e2 - N_GROUPS
    hit1 = lane == x1id
    hit2 = lane == x2id
    onehot = jnp.where(hit1 | hit2, 1.0, 0.0)
    before = jnp.dot(tri_ref[...], onehot.astype(BF16), preferred_element_type=F32) + carry_ref[...]
    r1 = jnp.sum(jnp.where(hit1, before, 0.0), axis=1, keepdims=True)
    r2 = jnp.sum(jnp.where(hit2, before, 0.0), axis=1, keepdims=True)
    carry_ref[...] = carry_ref[...] + jnp.sum(onehot, axis=0, keepdims=True)
    cnt_ref[...] = carry_ref[...]
    cols = jnp.where(lane == 0.0, x1id, jnp.where(lane == 1.0, x2id,
                                                  jnp.where(lane == 2.0, r1,
                                                            jnp.where(lane == 3.0, r2, 0.0))))
    rank_ref[0] = cols.T[0:ROUTE_W, :].astype(I32)


def _merge_call(x2, o_a, o_b, sa, sb, mod, norm2_g, wa, wb, wo, wr_hi, wr_lo, br):
    nb = SEQ // TM_MERGE
    tok = lambda w: pl.BlockSpec((TM_MERGE, w), lambda i: (i, 0))
    const2 = lambda r, c: pl.BlockSpec((r, c), lambda i: (0, 0))
    return pl.pallas_call(
        _merge_kernel,
        grid=(TOKENS // TM_MERGE,),
        in_specs=[tok(D_MODEL), tok(GLA_V_W), tok(DIFF_V_W), tok(D_MODEL), tok(D_MODEL),
                  pl.BlockSpec((1, N_MOD, D_MODEL), lambda i: (i // nb, 0, 0)),
                  const2(1, D_MODEL),
                  const2(GLA_V_W, D_MODEL), const2(DIFF_V_W, D_MODEL), const2(D_MODEL, D_MODEL),
                  const2(D_MODEL, LANES), const2(D_MODEL, LANES), const2(1, LANES)],
        out_specs=(tok(D_MODEL), _row_tile_spec(TM_MERGE, lambda i: (i, 0)),
                   tok(ROUTE_W),
                   pl.BlockSpec((1, ROUTE_W, TM_MERGE), lambda i: (i, 0, 0)),
                   pl.BlockSpec((1, LANES), lambda i: (0, 0))),
        out_shape=(jax.ShapeDtypeStruct((TOKENS, D_MODEL), F32),
                   jax.ShapeDtypeStruct((TOKENS * ROW_TILE_S, LANES), F32),
                   jax.ShapeDtypeStruct((TOKENS, ROUTE_W), F32),
                   jax.ShapeDtypeStruct((TOKENS // TM_MERGE, ROUTE_W, TM_MERGE), I32),
                   jax.ShapeDtypeStruct((1, LANES), F32)),
        scratch_shapes=[pltpu.VMEM((1, LANES), F32), pltpu.VMEM((TM_MERGE, TM_MERGE), BF16)],
        compiler_params=_cparams(("arbitrary",)),
        name="merge",
    )(x2, o_a, o_b, sa, sb, mod, norm2_g, wa, wb, wo, wr_hi, wr_lo, br)


def _plan_kernel(rk_ref, cnt_ref, dest_ref, blk_ref):
    n_tiles, _, tm = rk_ref.shape
    cnt = cnt_ref[...]
    padded = jnp.floor((cnt + (MOE_BLK - 1)) * (1.0 / MOE_BLK)) * MOE_BLK
    lane = lax.broadcasted_iota(I32, (1, LANES), 1)
    seg_end = padded
    step = 1
    while step < N_EXPERTS:
        seg_end = seg_end + jnp.where(lane >= step, pltpu.roll(seg_end, step, 1), 0.0)
        step *= 2
    seg_start = seg_end - padded
    valid_end = seg_start + cnt

    ei = lax.broadcasted_iota(I32, (LANES, LANES), 0)
    ej = lax.broadcasted_iota(I32, (LANES, LANES), 1)

    def to_col(rowv):
        return jnp.sum(jnp.where(ei == ej, rowv, 0.0), axis=1, keepdims=True)

    start_col, end_col, valid_col = to_col(seg_start), to_col(seg_end), to_col(valid_end)

    nb_pad = blk_ref.shape[1]
    e_sub = lax.broadcasted_iota(I32, (LANES, nb_pad), 0)
    b_start = (lax.broadcasted_iota(I32, (1, nb_pad), 1) * MOE_BLK).astype(F32)
    ends_before = jnp.where((e_sub < N_EXPERTS) & (end_col <= b_start), 1.0, 0.0)
    block_e = jnp.minimum(jnp.sum(ends_before, axis=0, keepdims=True), N_EXPERTS - 1.0)
    block_valid_end = jnp.sum(jnp.where(e_sub.astype(F32) == block_e, valid_col, 0.0),
                              axis=0, keepdims=True)
    n_valid = jnp.clip(block_valid_end - b_start, 0.0, float(MOE_BLK))
    n_used = jnp.max(seg_end, axis=1, keepdims=True) * (1.0 / MOE_BLK)
    nonempty = (to_col(cnt) > 0.0) & (e_sub < N_EXPERTS)
    e_subf = e_sub.astype(F32)
    no_next = float(LANES)
    nxt = jnp.min(jnp.where(nonempty & (e_subf > block_e), e_subf, no_next), axis=0, keepdims=True)
    nxt = jnp.where(nxt == no_next, -1.0, nxt)
    seg_idx = jnp.sum(jnp.where(nonempty & (e_subf < block_e), 1.0, 0.0), axis=0, keepdims=True)
    slot = seg_idx - 2.0 * jnp.floor(seg_idx * 0.5)
    blk_ref[...] = jnp.concatenate(
        [block_e, n_valid, jnp.broadcast_to(n_used, (1, nb_pad)), nxt, slot,
         jnp.zeros((ROUTE_W - 5, nb_pad), F32)], axis=0).astype(I32)

    e_tok = lax.broadcasted_iota(I32, (LANES, tm), 0)

    def tile(t, carry):
        rk = rk_ref[t]
        d = [jnp.sum(jnp.where(e_tok == rk[k:k + 1], start_col, 0.0), axis=0, keepdims=True)
             + rk[k + 2:k + 3].astype(F32) for k in range(TOP_K)]
        dest_ref[t] = jnp.concatenate(d + [jnp.zeros((ROUTE_W - TOP_K, tm), F32)], axis=0).astype(I32)
        return carry

    lax.fori_loop(0, n_tiles, tile, 0)


def _plan_call(rank_rows, cnt):
    n_tiles = TOKENS // TM_MERGE
    nb_pad = -(-N_BLOCKS // LANES) * LANES
    dest, blk = pl.pallas_call(
        _plan_kernel,
        out_shape=(jax.ShapeDtypeStruct((n_tiles, ROUTE_W, TM_MERGE), I32),
                   jax.ShapeDtypeStruct((ROUTE_W, nb_pad), I32)),
        compiler_params=pltpu.CompilerParams(vmem_limit_bytes=VMEM_LIMIT),
        name="plan",
    )(rank_rows, cnt)
    dest1 = dest[:, 0, :].reshape(TOKENS)
    dest2 = dest[:, 1, :].reshape(TOKENS)
    tables = tuple(blk[r, :N_BLOCKS] for r in (0, 1, 3, 4))
    return dest1, dest2, tables, blk[2, :1]


def _sc_params():
    return pltpu.CompilerParams(use_tc_tiling_on_sc=True)


def _sc_mesh():
    return plsc.VectorSubcoreMesh(core_axis_name="core", subcore_axis_name="subcore")


def _sc_worker_base(per_worker):
    wid = lax.axis_index("subcore") * SC_CORES + lax.axis_index("core")
    return wid * per_worker


def _sc_scatter_rows(src, dest1, dest2, n_out):
    n = src.shape[0]
    per_worker = n // SC_WORKERS
    assert per_worker * SC_WORKERS == n and per_worker % SC_CHUNK == 0

    def body(src_hbm, d1_hbm, d2_hbm, out_hbm, idx_v, rows_v):
        base = _sc_worker_base(per_worker)

        @pl.loop(0, per_worker // SC_CHUNK)
        def _(j):
            start = pl.multiple_of(base + j * SC_CHUNK, SC_CHUNK)
            pltpu.sync_copy(src_hbm.at[pl.ds(start, SC_CHUNK)], rows_v)
            for d_hbm in (d1_hbm, d2_hbm):
                pltpu.sync_copy(d_hbm.at[pl.ds(start, SC_CHUNK)], idx_v)
                pltpu.sync_copy(rows_v, out_hbm.at[idx_v])

    return pl.kernel(
        body,
        out_type=jax.ShapeDtypeStruct((n_out, ROW_TILE_S, LANES), F32),
        mesh=_sc_mesh(),
        scratch_types=[pltpu.VMEM((SC_CHUNK,), I32),
                       pltpu.VMEM((SC_CHUNK, ROW_TILE_S, LANES), F32)],
        compiler_params=_sc_params(),
        name="sc_dispatch",
    )(src, dest1, dest2)


def _sc_gather_rows(table, idx):
    n = idx.shape[0]
    per_worker = n // SC_WORKERS
    half = SC_CHUNK // 2
    assert per_worker * SC_WORKERS == n and per_worker % SC_CHUNK == 0

    def body(table_hbm, idx_hbm, out_hbm, idx_v, rows_a, rows_b, gsem_a, gsem_b, wsem_a, wsem_b):
        base = pl.multiple_of(_sc_worker_base(per_worker), SC_CHUNK)
        pltpu.sync_copy(idx_hbm.at[pl.ds(base, per_worker)], idx_v)

        @pl.loop(0, per_worker // SC_CHUNK)
        def _(j):
            off_a = pl.multiple_of(j * SC_CHUNK, SC_CHUNK)
            off_b = pl.multiple_of(j * SC_CHUNK + half, half)
            ga = pltpu.async_copy(table_hbm.at[idx_v.at[pl.ds(off_a, half)]], rows_a, gsem_a)
            gb = pltpu.async_copy(table_hbm.at[idx_v.at[pl.ds(off_b, half)]], rows_b, gsem_b)
            ga.wait()
            wa = pltpu.async_copy(rows_a, out_hbm.at[pl.ds(base + off_a, half)], wsem_a)
            gb.wait()
            wb = pltpu.async_copy(rows_b, out_hbm.at[pl.ds(base + off_b, half)], wsem_b)
            wa.wait()
            wb.wait()

    return pl.kernel(
        body,
        out_type=jax.ShapeDtypeStruct((n, ROW_TILE_S, LANES), F32),
        mesh=_sc_mesh(),
        scratch_types=[pltpu.VMEM((per_worker,), I32),
                       pltpu.VMEM((half, ROW_TILE_S, LANES), F32),
                       pltpu.VMEM((half, ROW_TILE_S, LANES), F32),
                       pltpu.SemaphoreType.DMA, pltpu.SemaphoreType.DMA,
                       pltpu.SemaphoreType.DMA, pltpu.SemaphoreType.DMA],
        compiler_params=_sc_params(),
        name="sc_gather",
    )(table, idx)


def _weight_fetch(w_hbm, stage, sem, e, slot):
    return [pltpu.make_async_copy(w.at[e], st.at[slot], sem.at[slot]) for w, st in zip(w_hbm, stage)]


def _expert_kernel(be_ref, nv_ref, nx_ref, sl_ref, nu_ref, xs_ref, w1_hbm, w3_hbm, w2_hbm, y_ref,
                   w1s, w3s, w2s, w1b, w3b, w2b, sem):
    i = pl.program_id(0)
    e = be_ref[i]
    used = i < nu_ref[0]
    first = (i == 0) | (e != be_ref[jnp.maximum(i - 1, 0)])
    w_hbm = (w1_hbm, w3_hbm, w2_hbm)
    stage = (w1s, w3s, w2s)

    @pl.when(used & first)
    def _():
        slot = sl_ref[i]

        @pl.when(i == 0)
        def _():
            for cp in _weight_fetch(w_hbm, stage, sem, e, slot):
                cp.start(priority=1)

        for cp in _weight_fetch(w_hbm, stage, sem, e, slot):
            cp.wait()
        w1b[...] = w1s[slot].astype(BF16)
        w3b[...] = w3s[slot].astype(BF16)
        w2b[...] = w2s[slot].astype(BF16)

        @pl.when(nx_ref[i] >= 0)
        def _():
            for cp in _weight_fetch(w_hbm, stage, sem, nx_ref[i], 1 - slot):
                cp.start(priority=1)

    @pl.when(used)
    def _():
        row = lax.broadcasted_iota(I32, (MOE_BLK, D_MODEL), 0)
        xb = jnp.where(row < nv_ref[i], _load_row_tiles(xs_ref), 0.0).astype(BF16)
        a = jnp.dot(xb, w1b[...], preferred_element_type=F32)
        g = jnp.dot(xb, w3b[...], preferred_element_type=F32)
        hmid = (a * _sigmoid(a)) * g
        _store_row_tiles(y_ref, jnp.dot(hmid.astype(BF16), w2b[...], preferred_element_type=F32))

    @pl.when(jnp.logical_not(used))
    def _():
        y_ref[...] = jnp.zeros_like(y_ref)


def _expert_call(tables, n_used, xs, w1, w3, w2):
    n_tab = len(tables)
    grid_spec = pltpu.PrefetchScalarGridSpec(
        num_scalar_prefetch=n_tab + 1,
        grid=(N_BLOCKS,),
        in_specs=[_row_tile_spec(MOE_BLK, lambda i, *pf: (jnp.minimum(i, pf[n_tab][0] - 1), 0)),
                  pl.BlockSpec(memory_space=pl.ANY),
                  pl.BlockSpec(memory_space=pl.ANY),
                  pl.BlockSpec(memory_space=pl.ANY)],
        out_specs=_row_tile_spec(MOE_BLK, lambda i, *pf: (i, 0)),
        scratch_shapes=[pltpu.VMEM((2, D_MODEL, D_EXPERT), F32),
                        pltpu.VMEM((2, D_MODEL, D_EXPERT), F32),
                        pltpu.VMEM((2, D_EXPERT, D_MODEL), F32),
                        pltpu.VMEM((D_MODEL, D_EXPERT), BF16),
                        pltpu.VMEM((D_MODEL, D_EXPERT), BF16),
                        pltpu.VMEM((D_EXPERT, D_MODEL), BF16),
                        pltpu.SemaphoreType.DMA((2,))],
    )
    return pl.pallas_call(
        _expert_kernel,
        grid_spec=grid_spec,
        out_shape=jax.ShapeDtypeStruct((N_BLOCKS * MOE_BLK * ROW_TILE_S, LANES), F32),
        compiler_params=_cparams(("arbitrary",)),
        name="experts",
    )(*tables, n_used, xs, w1, w3, w2)


def _combine_rows_kernel(x1_ref, rf_ref, mod_ref, ya_ref, yb_ref, *rest):
    o_ref = rest[-1]
    rf = rf_ref[...]
    gate2 = mod_ref[0, 5:6, :]
    moe = rf[:, 0:1] * _load_row_tiles(ya_ref) + rf[:, 1:2] * _load_row_tiles(yb_ref)
    o_ref[...] = x1_ref[...] + gate2 * moe


def _combine_rows_call(x1, rf, mod, yg, part, prev=None):
    nb = SEQ // TM_ROW
    n_half = TOKENS // TM_ROW // 2
    t0 = part * n_half
    in_specs = [pl.BlockSpec((TM_ROW, D_MODEL), lambda i: (t0 + i, 0)),
                pl.BlockSpec((TM_ROW, ROUTE_W), lambda i: (t0 + i, 0)),
                pl.BlockSpec((1, N_MOD, D_MODEL), lambda i: ((t0 + i) // nb, 0, 0)),
                _row_tile_spec(TM_ROW, lambda i: (i, 0)),
                _row_tile_spec(TM_ROW, lambda i: (i + n_half, 0))]
    args = [x1, rf, mod, yg, yg]
    aliases = {}
    if prev is not None:
        in_specs.append(pl.BlockSpec(memory_space=pl.ANY))
        args.append(prev)
        aliases = {len(args) - 1: 0}
    return pl.pallas_call(
        _combine_rows_kernel,
        grid=(n_half,),
        in_specs=in_specs,
        out_specs=pl.BlockSpec((TM_ROW, D_MODEL), lambda i: (t0 + i, 0)),
        out_shape=jax.ShapeDtypeStruct((TOKENS, D_MODEL), F32),
        input_output_aliases=aliases,
        compiler_params=_cparams(("parallel",)),
        name="combine",
    )(*args)


def kernel(x, c, positions, ada_w, ada_b, norm1_g, norm2_g, w_in, gla_alpha_up, gla_alpha_b,
           gla_out_g, diff_q_g, diff_k_g, diff_lq1, diff_lk1, diff_lq2, diff_lk2, diff_out_g,
           w_branch_a, w_branch_b, w_out, router_group_w, router_group_b, router_expert_w,
           router_expert_b, expert_w1, expert_w3, expert_w2):
    assert x.shape == (BATCH, SEQ, D_MODEL) and ada_w.shape[0] == 1
    x2 = x.reshape(TOKENS, D_MODEL)

    w_in_p = _wprep_call(w_in[0])
    au_pad = jnp.zeros((LR_PAD, GLA_QK_W), F32).at[:GLA_GATE_RANK].set(gla_alpha_up[0])
    au_hi = au_pad.astype(BF16)
    au_lo = (au_pad - au_hi.astype(F32)).astype(BF16)
    gid = jnp.arange(DIFF_QK_W) // DIFF_DH
    bd = jnp.where(gid[:, None] == gid[None, :], 1.0 / DIFF_DH, 0.0).astype(BF16)
    qg_row = jnp.tile(diff_q_g[0], DIFF_QK_W // DIFF_DH).reshape(1, DIFF_QK_W)
    kg_row = jnp.tile(diff_k_g[0], DIFF_QK_W // DIFF_DH).reshape(1, DIFF_QK_W)
    inv_freq = ROPE_THETA ** (-jnp.arange(ROPE_HALF, dtype=F32) / ROPE_HALF)
    invf64 = jnp.concatenate([inv_freq, inv_freq, jnp.zeros((DIFF_DH - ROPE_DIM,), F32)])
    invf_row = jnp.tile(invf64, LANES // DIFF_DH).reshape(1, LANES)
    pos_col = positions.reshape(TOKENS, 1)
    lam_p = jnp.concatenate([diff_lq1, diff_lk1, diff_lq2, diff_lk2], axis=0)
    wr = (jnp.zeros((D_MODEL, LANES), F32)
          .at[:, :N_GROUPS].set(router_group_w[0])
          .at[:, N_GROUPS:N_GROUPS + N_EXPERTS].set(router_expert_w[0]))
    wr_hi = wr.astype(BF16)
    wr_lo = (wr - wr_hi.astype(F32)).astype(BF16)
    br = (jnp.zeros((1, LANES), F32)
          .at[0, :N_GROUPS].set(router_group_b[0])
          .at[0, N_GROUPS:N_GROUPS + N_EXPERTS].set(router_expert_b[0]))

    mod = _mod_call(c, ada_w[0], ada_b[0])
    gqk, gv, gr, lr, qT, kk, vT, sa, sb = _in_call(
        x2, mod, norm1_g, w_in_p, bd, qg_row, kg_row, pos_col, invf_row)
    o_a = _gla_call(gqk, gv, gr, lr, au_hi, au_lo, gla_alpha_b, gla_out_g)
    o_b = _attn_call(qT, kk, vT, lam_p, diff_out_g.reshape(DIFF_DV, 1))
    x1, h2, rf, rank, cnt = _merge_call(
        x2, o_a, o_b, sa, sb, mod, norm2_g, w_branch_a[0].astype(BF16),
        w_branch_b[0].astype(BF16), w_out[0].astype(BF16), wr_hi, wr_lo, br)
    dest1, dest2, tables, n_used = _plan_call(rank, cnt)

    xs = _sc_scatter_rows(_as_row_tiles(h2), dest1, dest2, N_BLOCKS * MOE_BLK)
    y = _expert_call(tables, n_used, _as_2d(xs), expert_w1[0], expert_w3[0], expert_w2[0])
    y3 = _as_row_tiles(y)
    half = TOKENS // 2
    out = None
    for part in range(2):
        tok = slice(part * half, (part + 1) * half)
        yg = _sc_gather_rows(y3, jnp.concatenate([dest1[tok], dest2[tok]]))
        out = _combine_rows_call(x1, rf, mod, _as_2d(yg), part, out)
    return out.reshape(BATCH, SEQ, D_MODEL)
```

```python
import math

import jax
import jax.numpy as jnp
from jax import lax
from jax.experimental import pallas as pl
from jax.experimental.pallas import tpu as pltpu
from jax.experimental.pallas import tpu_sc as plsc

F32 = jnp.float32
BF16 = jnp.bfloat16
I32 = jnp.int32

D_MODEL = 1024
BATCH = 4
SEQ = 4096
TOKENS = BATCH * SEQ
N_MOD = 6
NORM_EPS = 1e-6

GLA_HEADS = 4
GLA_DK = 64
GLA_DV = 128
GLA_GATE_RANK = 16
GLA_GATE_TAU = 16.0
GLA_CHUNK = 64
GLA_QK_W = GLA_HEADS * GLA_DK
GLA_V_W = GLA_HEADS * GLA_DV

DIFF_HEADS = 4
DIFF_DH = 64
DIFF_DV = 2 * DIFF_DH
DIFF_QK_W = DIFF_HEADS * 2 * DIFF_DH
DIFF_V_W = DIFF_HEADS * DIFF_DV
ROPE_THETA = 500000.0
ROPE_DIM = DIFF_DH // 4
ROPE_HALF = ROPE_DIM // 2
NEG_INF = -1e30
LAMBDA_INIT = 0.8 - 0.6 * 1.0

N_GROUPS = 4
EXPERTS_PER_GROUP = 8
N_EXPERTS = N_GROUPS * EXPERTS_PER_GROUP
TOP_K = 2
D_EXPERT = 512

LANES = 128
SUBLANES = 8
ROW_TILE_S = D_MODEL // LANES
SC_CORES = 2
SC_SUBCORES = 16
SC_WORKERS = SC_CORES * SC_SUBCORES
SC_CHUNK = 64
LR_PAD = LANES
D_IN_PAD = 2 * GLA_QK_W + 2 * GLA_V_W + 2 * DIFF_QK_W + DIFF_V_W + 2 * D_MODEL + LR_PAD

TM_IN = 512
TQ = 512
V_ROWS = DIFF_DV + 16
ATT_SUB = 32
Q_SCALE = DIFF_DH ** -0.5 * math.log2(math.e)
N_KV = SEQ // TQ
TT_GLA = 512
TM_MERGE = 512
TM_ROW = 256
MOE_BLK = 512
MOE_SUB = 256
N_BLOCKS = (TOKENS * TOP_K + N_EXPERTS * (MOE_BLK - 1) + MOE_BLK - 1) // MOE_BLK
ROUTE_W = 8

VMEM_LIMIT = 56 * 1024 * 1024


def _cparams(sem):
    return pltpu.CompilerParams(dimension_semantics=sem, vmem_limit_bytes=VMEM_LIMIT)


def _sigmoid(x):
    return 1.0 / (1.0 + jnp.exp(-x))


def _row_tile_spec(rows, index_map):
    return pl.BlockSpec((rows * ROW_TILE_S, LANES), index_map)


def _as_row_tiles(a2d):
    return a2d.reshape(a2d.shape[0] // ROW_TILE_S, ROW_TILE_S, LANES)


def _as_2d(a3d):
    return a3d.reshape(a3d.shape[0] * ROW_TILE_S, LANES)


def _store_row_tiles(ref, val, row0=0):
    for s in range(ROW_TILE_S):
        ref[pl.ds(row0 * ROW_TILE_S + s, val.shape[0], stride=ROW_TILE_S), :] = (
            val[:, s * LANES:(s + 1) * LANES])


def _load_row_tiles(ref, row0=0, rows=None):
    rows = ref.shape[0] // ROW_TILE_S if rows is None else rows
    return jnp.concatenate(
        [ref[pl.ds(row0 * ROW_TILE_S + s, rows, stride=ROW_TILE_S), :] for s in range(ROW_TILE_S)],
        axis=1)


def _mod_kernel(ct_ref, w_ref, b_ref, o_ref):
    ct = ct_ref[...]
    ca = ct * _sigmoid(ct)
    w = w_ref[...]
    rows = []
    for b in range(BATCH):
        rows.append(jnp.sum(w * ca[:, b:b + 1], axis=0, keepdims=True) + b_ref[...])
    rows.append(jnp.zeros((8 - BATCH, w.shape[1]), F32))
    o_ref[...] = jnp.concatenate(rows, axis=0)


def _mod_call(c, ada_w, ada_b):
    tn = D_MODEL
    ct = jnp.zeros((D_MODEL, 8), F32).at[:, :BATCH].set(c.T)
    out = pl.pallas_call(
        _mod_kernel,
        grid=(N_MOD,),
        in_specs=[
            pl.BlockSpec((D_MODEL, 8), lambda j: (0, 0)),
            pl.BlockSpec((D_MODEL, tn), lambda j: (0, j)),
            pl.BlockSpec((1, tn), lambda j: (0, j)),
        ],
        out_specs=pl.BlockSpec((8, tn), lambda j: (0, j)),
        out_shape=jax.ShapeDtypeStruct((8, N_MOD * D_MODEL), F32),
        compiler_params=_cparams(("arbitrary",)),
        name="mod",
    )(ct, ada_w, ada_b.reshape(1, N_MOD * D_MODEL))
    return out[:BATCH].reshape(BATCH, N_MOD, D_MODEL)


def _wprep_kernel(w_ref, o_ref):
    w = w_ref[...]
    lr0 = 2 * GLA_QK_W + 2 * GLA_V_W
    rest = D_IN_PAD - LR_PAD - lr0
    o_ref[:, :lr0] = w[:, :lr0].astype(BF16)
    o_ref[:, lr0:lr0 + rest] = w[:, lr0 + GLA_GATE_RANK:].astype(BF16)
    lr = jnp.concatenate([w[:, lr0:lr0 + GLA_GATE_RANK],
                          jnp.zeros((w.shape[0], LR_PAD - GLA_GATE_RANK), F32)], axis=1)
    o_ref[:, lr0 + rest:] = lr.astype(BF16)


def _wprep_call(w_in0):
    d_in = w_in0.shape[1]
    rows = 128
    return pl.pallas_call(
        _wprep_kernel,
        grid=(D_MODEL // rows,),
        in_specs=[pl.BlockSpec((rows, d_in), lambda i: (i, 0))],
        out_specs=pl.BlockSpec((rows, D_IN_PAD), lambda i: (i, 0)),
        out_shape=jax.ShapeDtypeStruct((D_MODEL, D_IN_PAD), BF16),
        compiler_params=_cparams(("parallel",)),
        name="w_prep",
    )(w_in0)


def _in_kernel(x_ref, mod_ref, g1_ref, w_ref, bd_ref, qg_ref, kg_ref, pos_ref, invf_ref,
               gqk_ref, gv_ref, gr_ref, lr_ref, qT_ref, k_ref, vT_ref, sa_ref, sb_ref,
               cos_scr, sin_scr):
    ang = pos_ref[...].astype(F32) * invf_ref[...]
    cos_scr[...] = jnp.cos(ang)
    sin_scr[...] = jnp.sin(ang)

    x = x_ref[...]
    shift1 = mod_ref[0, 0:1, :]
    scale1 = mod_ref[0, 1:2, :]
    ms = jnp.mean(x * x, axis=-1, keepdims=True)
    h = (x * lax.rsqrt(ms + NORM_EPS) * g1_ref[...]) * (1.0 + scale1) + shift1
    hb = h.astype(BF16)

    def proj(c0, c1):
        return jnp.dot(hb, w_ref[:, c0:c1], preferred_element_type=F32)

    tm = x.shape[0]
    cos4 = jnp.concatenate([cos_scr[...]] * DIFF_HEADS, axis=1)
    sin4 = jnp.concatenate([sin_scr[...]] * DIFF_HEADS, axis=1)
    lane = lax.broadcasted_iota(I32, (tm, DIFF_QK_W), 1)
    first_half = (lane % DIFF_DH) < ROPE_HALF
    bd = bd_ref[...]

    def norm_rope(t, gain_row):
        t2 = t * t
        hi = t2.astype(BF16)
        lo = (t2 - hi.astype(F32)).astype(BF16)
        gms = (jnp.dot(hi, bd, preferred_element_type=F32)
               + jnp.dot(lo, bd, preferred_element_type=F32))
        t = t * lax.rsqrt(gms + NORM_EPS) * gain_row
        nxt = pltpu.roll(t, DIFF_QK_W - ROPE_HALF, 1)
        prv = pltpu.roll(t, ROPE_HALF, 1)
        return t * cos4 + jnp.where(first_half, -nxt, prv) * sin4

    dq_raw = proj(1536, 2048)
    dk_raw = proj(2048, 2560)
    gqk_ref[...] = proj(0, 512).astype(BF16)
    dq = norm_rope(dq_raw, qg_ref[...]) * Q_SCALE
    gv_ref[...] = proj(512, 1024).astype(BF16)
    dk = norm_rope(dk_raw, kg_ref[...])
    dv = proj(2560, 3072)
    for hd in range(DIFF_HEADS):
        sl = slice(hd * LANES, (hd + 1) * LANES)
        qT_ref[0, hd, 0] = dq[:, sl].T.astype(BF16)
        k_ref[0, hd, 0] = dk[:, sl].astype(BF16)
    gr_ref[...] = proj(1024, 1536).astype(BF16)
    for hd in range(DIFF_HEADS):
        sl = slice(hd * LANES, (hd + 1) * LANES)
        vT_ref[0, hd, 0] = jnp.concatenate(
            [dv[:, sl].T, jnp.ones((V_ROWS - DIFF_DV, tm), F32)], axis=0).astype(BF16)
    sa_ref[...] = _sigmoid(proj(3072, 4096)).astype(BF16)
    sb_ref[...] = _sigmoid(proj(4096, 5120)).astype(BF16)
    lr_ref[...] = proj(5120, 5248)


def _in_call(x2, mod, norm1_g, w_in_p, bd, qg_row, kg_row, pos_col, invf_row):
    nb = SEQ // TM_IN
    tok_spec = lambda w: pl.BlockSpec((TM_IN, w), lambda i: (i, 0))
    const2 = lambda r, c: pl.BlockSpec((r, c), lambda i: (0, 0))
    out_shapes = (
        jax.ShapeDtypeStruct((TOKENS, 2 * GLA_QK_W), BF16),
        jax.ShapeDtypeStruct((TOKENS, GLA_V_W), BF16),
        jax.ShapeDtypeStruct((TOKENS, GLA_V_W), BF16),
        jax.ShapeDtypeStruct((TOKENS, LR_PAD), F32),
        jax.ShapeDtypeStruct((BATCH, DIFF_HEADS, N_KV, LANES, TQ), BF16),
        jax.ShapeDtypeStruct((BATCH, DIFF_HEADS, N_KV, TQ, LANES), BF16),
        jax.ShapeDtypeStruct((BATCH, DIFF_HEADS, N_KV, V_ROWS, TQ), BF16),
        jax.ShapeDtypeStruct((TOKENS, D_MODEL), BF16),
        jax.ShapeDtypeStruct((TOKENS, D_MODEL), BF16),
    )
    out_specs = (
        tok_spec(2 * GLA_QK_W), tok_spec(GLA_V_W), tok_spec(GLA_V_W), tok_spec(LR_PAD),
        pl.BlockSpec((1, DIFF_HEADS, 1, LANES, TM_IN), lambda i: (i // nb, 0, i % nb, 0, 0)),
        pl.BlockSpec((1, DIFF_HEADS, 1, TM_IN, LANES), lambda i: (i // nb, 0, i % nb, 0, 0)),
        pl.BlockSpec((1, DIFF_HEADS, 1, V_ROWS, TM_IN), lambda i: (i // nb, 0, i % nb, 0, 0)),
        tok_spec(D_MODEL), tok_spec(D_MODEL),
    )
    return pl.pallas_call(
        _in_kernel,
        grid=(TOKENS // TM_IN,),
        in_specs=[
            tok_spec(D_MODEL),
            pl.BlockSpec((1, N_MOD, D_MODEL), lambda i: (i // nb, 0, 0)),
            const2(1, D_MODEL),
            pl.BlockSpec((D_MODEL, D_IN_PAD), lambda i: (0, 0), pipeline_mode=pl.Buffered(1)),
            const2(DIFF_QK_W, DIFF_QK_W),
            const2(1, DIFF_QK_W), const2(1, DIFF_QK_W),
            tok_spec(1),
            const2(1, LANES),
        ],
        out_specs=out_specs,
        out_shape=out_shapes,
        scratch_shapes=[pltpu.VMEM((TM_IN, LANES), F32), pltpu.VMEM((TM_IN, LANES), F32)],
        compiler_params=_cparams(("parallel",)),
        name="in_proj",
    )(x2, mod, norm1_g, w_in_p, bd, qg_row, kg_row, pos_col, invf_row)


def _gla_kernel(qk_ref, v_ref, r_ref, lr_ref, auh_ref, aul_ref, ab_ref, og_ref, o_ref,
                state_ref, oacc_ref, snew_ref):
    tt = qk_ref.shape[0]
    n_chunks = tt // GLA_CHUNK

    @pl.when(pl.program_id(1) == 0)
    def _():
        state_ref[...] = jnp.zeros_like(state_ref)

    lr = lr_ref[...]
    lr_hi = lr.astype(BF16)
    lr_lo = (lr - lr_hi.astype(F32)).astype(BF16)
    z = (jnp.dot(lr_hi, auh_ref[...], preferred_element_type=F32)
         + jnp.dot(lr_lo, auh_ref[...], preferred_element_type=F32)
         + jnp.dot(lr_hi, aul_ref[...], preferred_element_type=F32)) + ab_ref[...]
    g = (jnp.minimum(z, 0.0) - jnp.log(1.0 + jnp.exp(-jnp.abs(z)))) * (1.0 / GLA_GATE_TAU)

    row = lax.broadcasted_iota(I32, (tt, GLA_QK_W), 0) % GLA_CHUNK
    b = g
    step = 1
    while step < GLA_CHUNK:
        b = b + jnp.where(row >= step, pltpu.roll(b, step, 0), 0.0)
        step *= 2

    b_last_rows = [b[c * GLA_CHUNK + GLA_CHUNK - 1:(c + 1) * GLA_CHUNK, :] for c in range(n_chunks)]
    b_last = jnp.concatenate(
        [jnp.broadcast_to(bl, (GLA_CHUNK, GLA_QK_W)) for bl in b_last_rows], axis=0)

    qk = qk_ref[...].astype(F32)
    q = qk[:, :GLA_QK_W] * (GLA_DK ** -0.5)
    k = qk[:, GLA_QK_W:]
    q_in = (q * jnp.exp(b)).astype(BF16)
    k_in = (k * jnp.exp(-b)).astype(BF16)
    k_dec = (k * jnp.exp(b_last - b)).astype(BF16)

    ci = lax.broadcasted_iota(I32, (GLA_CHUNK, GLA_CHUNK), 0)
    cj = lax.broadcasted_iota(I32, (GLA_CHUNK, GLA_CHUNK), 1)
    causal = ci >= cj

    pairs = [(c, hd) for c in range(n_chunks) for hd in range(GLA_HEADS)]

    def rows(c):
        return slice(c * GLA_CHUNK, (c + 1) * GLA_CHUNK)

    def kcols(hd):
        return slice(hd * GLA_DK, (hd + 1) * GLA_DK)

    def vcols(hd):
        return slice(hd * GLA_DV, (hd + 1) * GLA_DV)

    att = {}
    for c, hd in pairs:
        a = lax.dot_general(q_in[rows(c), kcols(hd)], k_in[rows(c), kcols(hd)],
                            (((1,), (1,)), ((), ())), preferred_element_type=F32)
        att[c, hd] = jnp.where(causal, a, 0.0).astype(BF16)
    for c, hd in pairs:
        oacc_ref[rows(c), vcols(hd)] = jnp.dot(att[c, hd], v_ref[rows(c), vcols(hd)],
                                               preferred_element_type=F32)
    for c, hd in pairs:
        snew_ref[c * GLA_HEADS + hd] = lax.dot_general(
            k_dec[rows(c), kcols(hd)], v_ref[rows(c), vcols(hd)],
            (((0,), (0,)), ((), ())), preferred_element_type=F32)

    decay_rows = jnp.exp(jnp.concatenate(
        b_last_rows + [jnp.zeros((LANES - n_chunks, GLA_QK_W), F32)], axis=0))
    decay_cols = decay_rows.T
    states = [state_ref[hd] for hd in range(GLA_HEADS)]
    for c in range(n_chunks):
        for hd in range(GLA_HEADS):
            s_prev = states[hd]
            oacc_ref[rows(c), vcols(hd)] += jnp.dot(
                q_in[rows(c), kcols(hd)], s_prev.astype(BF16), preferred_element_type=F32)
            dcol = decay_cols[kcols(hd), c:c + 1]
            states[hd] = s_prev * dcol + snew_ref[c * GLA_HEADS + hd]
    for hd in range(GLA_HEADS):
        state_ref[hd] = states[hd]

    for hd in range(GLA_HEADS):
        vs = slice(hd * GLA_DV, (hd + 1) * GLA_DV)
        oh = oacc_ref[:, vs]
        ms = jnp.mean(oh * oh, axis=-1, keepdims=True)
        y = oh * lax.rsqrt(ms + NORM_EPS) * og_ref[...]
        r = r_ref[:, vs].astype(F32)
        o_ref[:, vs] = (y * (r * _sigmoid(r))).astype(BF16)


def _gla_call(gqk, gv, gr, lr, au_hi, au_lo, ab_row, og_row):
    nt = SEQ // TT_GLA
    tok = lambda w: pl.BlockSpec((TT_GLA, w), lambda b, t: (b * nt + t, 0))
    const2 = lambda r, c: pl.BlockSpec((r, c), lambda b, t: (0, 0))
    return pl.pallas_call(
        _gla_kernel,
        grid=(BATCH, nt),
        in_specs=[tok(2 * GLA_QK_W), tok(GLA_V_W), tok(GLA_V_W), tok(LR_PAD),
                  const2(LR_PAD, GLA_QK_W), const2(LR_PAD, GLA_QK_W), const2(1, GLA_QK_W),
                  const2(1, GLA_DV)],
        out_specs=tok(GLA_V_W),
        out_shape=jax.ShapeDtypeStruct((TOKENS, GLA_V_W), BF16),
        scratch_shapes=[pltpu.VMEM((GLA_HEADS, GLA_DK, GLA_DV), F32),
                        pltpu.VMEM((TT_GLA, GLA_V_W), F32),
                        pltpu.VMEM((TT_GLA // GLA_CHUNK * GLA_HEADS, GLA_DK, GLA_DV), F32)],
        compiler_params=_cparams(("parallel", "arbitrary")),
        name="gla",
    )(gqk, gv, gr, lr, au_hi, au_lo, ab_row, og_row)


def _attn_kernel(qT_ref, k_ref, vT_ref, lam_ref, og_ref, o_ref,
                 q_scr, s_scr, p_scr, acc_scr, m_scr):
    def q_block(i, carry):
        _attn_q_block(i, qT_ref, k_ref, vT_ref, lam_ref, og_ref, o_ref,
                      q_scr, s_scr, p_scr, acc_scr, m_scr)
        return carry

    lax.fori_loop(0, N_KV, q_block, 0)


def _attn_q_block(i, qT_ref, k_ref, vT_ref, lam_ref, og_ref, o_ref,
                  q_scr, s_scr, p_scr, acc_scr, m_scr):
    qT = qT_ref[0, 0, i]
    rowq = lax.broadcasted_iota(I32, qT.shape, 0)
    zero = jnp.zeros_like(qT)
    q_scr[0] = jnp.where(rowq < DIFF_DH, qT, zero)
    q_scr[1] = jnp.where(rowq >= DIFF_DH, qT, zero)
    m_scr[...] = jnp.full(m_scr.shape, NEG_INF, F32)
    acc_scr[...] = jnp.zeros(acc_scr.shape, F32)
    n_sub = TQ // ATT_SUB

    def fold8(t, op):
        return op(t.reshape(t.shape[0] // SUBLANES, SUBLANES, TQ), axis=0)

    def scores(c, j):
        s_scr[c] = jnp.dot(k_ref[0, 0, j], q_scr[c], preferred_element_type=F32)

    def load_s(c, r, masked):
        s = s_scr[c, r * ATT_SUB:(r + 1) * ATT_SUB, :]
        if masked:
            key_i = lax.broadcasted_iota(I32, (ATT_SUB, TQ), 0) + r * ATT_SUB
            qry_i = lax.broadcasted_iota(I32, (ATT_SUB, TQ), 1)
            s = jnp.where(key_i <= qry_i, s, NEG_INF)
        return s

    def softmax_pv(c, j, masked):
        m8 = fold8(load_s(c, 0, masked), jnp.max)
        for r in range(1, n_sub):
            m8 = jnp.maximum(m8, fold8(load_s(c, r, masked), jnp.max))
        m_old = m_scr[c]
        m_new = jnp.maximum(m_old, jnp.max(m8, axis=0, keepdims=True))
        alpha = jnp.exp2(m_old - m_new)
        for r in range(n_sub):
            p = jnp.exp2(load_s(c, r, masked) - m_new)
            p_scr[c, r * ATT_SUB:(r + 1) * ATT_SUB, :] = p.astype(BF16)
        m_scr[c] = m_new
        acc_scr[c] = acc_scr[c] * alpha + jnp.dot(vT_ref[0, 0, j], p_scr[c],
                                                  preferred_element_type=F32)

    scores(0, 0)

    def body(j, carry):
        scores(1, j)
        softmax_pv(0, j, False)
        scores(0, j + 1)
        softmax_pv(1, j, False)
        return carry

    lax.fori_loop(0, i, body, 0)
    scores(1, i)
    softmax_pv(0, i, True)
    softmax_pv(1, i, True)

    l1 = acc_scr[0, DIFF_DV:DIFF_DV + 1, :]
    l2 = acc_scr[1, DIFF_DV:DIFF_DV + 1, :]

    lam_p = lam_ref[...]
    lam = (jnp.exp(jnp.sum(lam_p[0:1] * lam_p[1:2], axis=1, keepdims=True))
           - jnp.exp(jnp.sum(lam_p[2:3] * lam_p[3:4], axis=1, keepdims=True)) + LAMBDA_INIT)
    oT = acc_scr[0, :DIFF_DV, :] / l1 - lam * (acc_scr[1, :DIFF_DV, :] / l2)
    ms = jnp.mean(oT * oT, axis=0, keepdims=True)
    y = oT * lax.rsqrt(ms + NORM_EPS) * og_ref[...] * (1.0 - LAMBDA_INIT)
    o_ref[pl.ds(pl.multiple_of(i * TQ, TQ), TQ), :] = y.T.astype(BF16)


def _attn_call(qT, kk, vT, lam_p, og_col):
    return pl.pallas_call(
        _attn_kernel,
        grid=(BATCH, DIFF_HEADS),
        in_specs=[
            pl.BlockSpec((1, 1, N_KV, LANES, TQ), lambda b, h: (b, h, 0, 0, 0)),
            pl.BlockSpec((1, 1, N_KV, TQ, LANES), lambda b, h: (b, h, 0, 0, 0)),
            pl.BlockSpec((1, 1, N_KV, V_ROWS, TQ), lambda b, h: (b, h, 0, 0, 0)),
            pl.BlockSpec((4, DIFF_DH), lambda b, h: (0, 0)),
            pl.BlockSpec((DIFF_DV, 1), lambda b, h: (0, 0)),
        ],
        out_specs=pl.BlockSpec((SEQ, DIFF_DV), lambda b, h: (b, h)),
        out_shape=jax.ShapeDtypeStruct((TOKENS, DIFF_V_W), BF16),
        scratch_shapes=[pltpu.VMEM((2, LANES, TQ), BF16),
                        pltpu.VMEM((2, TQ, TQ), F32),
                        pltpu.VMEM((2, TQ, TQ), BF16),
                        pltpu.VMEM((2, V_ROWS, TQ), F32),
                        pltpu.VMEM((2, 1, TQ), F32)],
        compiler_params=_cparams(("parallel", "parallel")),
        name="attn",
    )(qT, kk, vT, lam_p, og_col)


def _merge_kernel(x_ref, oa_ref, ob_ref, sa_ref, sb_ref, mod_ref, g2_ref, wa_ref, wb_ref, wo_ref,
                  wrh_ref, wrl_ref, br_ref, x1_ref, h2_ref, rf_ref, rank_ref, cnt_ref,
                  carry_ref, tri_ref):
    tm = x_ref.shape[0]

    @pl.when(pl.program_id(0) == 0)
    def _():
        carry_ref[...] = jnp.zeros_like(carry_ref)
        ti = lax.broadcasted_iota(I32, (tm, tm), 0)
        tj = lax.broadcasted_iota(I32, (tm, tm), 1)
        tri_ref[...] = jnp.where(ti > tj, 1.0, 0.0).astype(BF16)

    ma = jnp.dot(oa_ref[...], wa_ref[...], preferred_element_type=F32)
    mb = jnp.dot(ob_ref[...], wb_ref[...], preferred_element_type=F32)
    merged = sa_ref[...].astype(F32) * ma + sb_ref[...].astype(F32) * mb
    y = jnp.dot(merged.astype(BF16), wo_ref[...], preferred_element_type=F32)
    gate1 = mod_ref[0, 2:3, :]
    shift2 = mod_ref[0, 3:4, :]
    scale2 = mod_ref[0, 4:5, :]
    x1 = x_ref[...] + gate1 * y
    x1_ref[...] = x1
    ms = jnp.mean(x1 * x1, axis=-1, keepdims=True)
    h2 = (x1 * lax.rsqrt(ms + NORM_EPS) * g2_ref[...]) * (1.0 + scale2) + shift2
    _store_row_tiles(h2_ref, h2)

    h2_hi = h2.astype(BF16)
    h2_lo = (h2 - h2_hi.astype(F32)).astype(BF16)
    logits = (jnp.dot(h2_hi, wrh_ref[...], preferred_element_type=F32)
              + jnp.dot(h2_lo, wrh_ref[...], preferred_element_type=F32)
              + jnp.dot(h2_hi, wrl_ref[...], preferred_element_type=F32)) + br_ref[...]
    lane = lax.broadcasted_iota(I32, (tm, LANES), 1).astype(F32)
    ninf = -jnp.inf
    big = float(LANES)

    def first_argmax(v):
        vmax = jnp.max(v, axis=1, keepdims=True)
        idx = jnp.min(jnp.where(v == vmax, lane, big), axis=1, keepdims=True)
        return vmax, idx

    gl = jnp.where(lane < N_GROUPS, logits, ninf)
    gmax, gidx = first_argmax(gl)
    p_top = 1.0 / jnp.sum(jnp.exp(gl - gmax), axis=1, keepdims=True)
    lo = N_GROUPS + EXPERTS_PER_GROUP * gidx
    el = jnp.where((lane >= lo) & (lane < lo + EXPERTS_PER_GROUP), logits, ninf)
    e1max, e1 = first_argmax(el)
    e2max, e2 = first_argmax(jnp.where(lane == e1, ninf, el))
    t = jnp.exp(e2max - e1max)
    w1 = 1.0 / (1.0 + t)
    w2 = t / (1.0 + t)
    col = lax.broadcasted_iota(I32, (tm, ROUTE_W), 1)
    rf_ref[...] = jnp.where(col == 0, p_top * w1, jnp.where(col == 1, p_top * w2, 0.0))

    x1id = e1 - N_GROUPS
    x2id = e2 - N_GROUPS
    hit1 = lane == x1id
    hit2 = lane == x2id
    onehot = jnp.where(hit1 | hit2, 1.0, 0.0)
    before = jnp.dot(tri_ref[...], onehot.astype(BF16), preferred_element_type=F32) + carry_ref[...]
    r1 = jnp.sum(jnp.where(hit1, before, 0.0), axis=1, keepdims=True)
    r2 = jnp.sum(jnp.where(hit2, before, 0.0), axis=1, keepdims=True)
    carry_ref[...] = carry_ref[...] + jnp.sum(onehot, axis=0, keepdims=True)
    cnt_ref[...] = carry_ref[...]
    cols = jnp.where(lane == 0.0, x1id, jnp.where(lane == 1.0, x2id,
                                                  jnp.where(lane == 2.0, r1,
                                                            jnp.where(lane == 3.0, r2, 0.0))))
    rank_ref[0] = cols.T[0:ROUTE_W, :].astype(I32)


def _merge_call(x2, o_a, o_b, sa, sb, mod, norm2_g, wa, wb, wo, wr_hi, wr_lo, br):
    nb = SEQ // TM_MERGE
    tok = lambda w: pl.BlockSpec((TM_MERGE, w), lambda i: (i, 0))
    const2 = lambda r, c: pl.BlockSpec((r, c), lambda i: (0, 0))
    return pl.pallas_call(
        _merge_kernel,
        grid=(TOKENS // TM_MERGE,),
        in_specs=[tok(D_MODEL), tok(GLA_V_W), tok(DIFF_V_W), tok(D_MODEL), tok(D_MODEL),
                  pl.BlockSpec((1, N_MOD, D_MODEL), lambda i: (i // nb, 0, 0)),
                  const2(1, D_MODEL),
                  const2(GLA_V_W, D_MODEL), const2(DIFF_V_W, D_MODEL), const2(D_MODEL, D_MODEL),
                  const2(D_MODEL, LANES), const2(D_MODEL, LANES), const2(1, LANES)],
        out_specs=(tok(D_MODEL), _row_tile_spec(TM_MERGE, lambda i: (i, 0)),
                   tok(ROUTE_W),
                   pl.BlockSpec((1, ROUTE_W, TM_MERGE), lambda i: (i, 0, 0)),
                   pl.BlockSpec((1, LANES), lambda i: (0, 0))),
        out_shape=(jax.ShapeDtypeStruct((TOKENS, D_MODEL), F32),
                   jax.ShapeDtypeStruct((TOKENS * ROW_TILE_S, LANES), F32),
                   jax.ShapeDtypeStruct((TOKENS, ROUTE_W), F32),
                   jax.ShapeDtypeStruct((TOKENS // TM_MERGE, ROUTE_W, TM_MERGE), I32),
                   jax.ShapeDtypeStruct((1, LANES), F32)),
        scratch_shapes=[pltpu.VMEM((1, LANES), F32), pltpu.VMEM((TM_MERGE, TM_MERGE), BF16)],
        compiler_params=_cparams(("arbitrary",)),
        name="merge",
    )(x2, o_a, o_b, sa, sb, mod, norm2_g, wa, wb, wo, wr_hi, wr_lo, br)


def _plan_kernel(rk_ref, cnt_ref, dest_ref, blk_ref):
    n_tiles, _, tm = rk_ref.shape
    cnt = cnt_ref[...]
    padded = jnp.floor((cnt + (MOE_BLK - 1)) * (1.0 / MOE_BLK)) * MOE_BLK
    lane = lax.broadcasted_iota(I32, (1, LANES), 1)
    seg_end = padded
    step = 1
    while step < N_EXPERTS:
        seg_end = seg_end + jnp.where(lane >= step, pltpu.roll(seg_end, step, 1), 0.0)
        step *= 2
    seg_start = seg_end - padded
    valid_end = seg_start + cnt

    ei = lax.broadcasted_iota(I32, (LANES, LANES), 0)
    ej = lax.broadcasted_iota(I32, (LANES, LANES), 1)

    def to_col(rowv):
        return jnp.sum(jnp.where(ei == ej, rowv, 0.0), axis=1, keepdims=True)

    start_col, end_col, valid_col = to_col(seg_start), to_col(seg_end), to_col(valid_end)

    nb_pad = blk_ref.shape[1]
    e_sub = lax.broadcasted_iota(I32, (LANES, nb_pad), 0)
    b_start = (lax.broadcasted_iota(I32, (1, nb_pad), 1) * MOE_BLK).astype(F32)
    ends_before = jnp.where((e_sub < N_EXPERTS) & (end_col <= b_start), 1.0, 0.0)
    block_e = jnp.minimum(jnp.sum(ends_before, axis=0, keepdims=True), N_EXPERTS - 1.0)
    block_valid_end = jnp.sum(jnp.where(e_sub.astype(F32) == block_e, valid_col, 0.0),
                              axis=0, keepdims=True)
    n_valid = jnp.clip(block_valid_end - b_start, 0.0, float(MOE_BLK))
    n_used = jnp.max(seg_end, axis=1, keepdims=True) * (1.0 / MOE_BLK)
    nonempty = (to_col(cnt) > 0.0) & (e_sub < N_EXPERTS)
    e_subf = e_sub.astype(F32)
    no_next = float(LANES)
    nxt = jnp.min(jnp.where(nonempty & (e_subf > block_e), e_subf, no_next), axis=0, keepdims=True)
    nxt = jnp.where(nxt == no_next, -1.0, nxt)
    seg_idx = jnp.sum(jnp.where(nonempty & (e_subf < block_e), 1.0, 0.0), axis=0, keepdims=True)
    slot = seg_idx - 2.0 * jnp.floor(seg_idx * 0.5)
    blk_ref[...] = jnp.concatenate(
        [block_e, n_valid, jnp.broadcast_to(n_used, (1, nb_pad)), nxt, slot,
         jnp.zeros((ROUTE_W - 5, nb_pad), F32)], axis=0).astype(I32)

    e_tok = lax.broadcasted_iota(I32, (LANES, tm), 0)

    def tile(t, carry):
        rk = rk_ref[t]
        d = [jnp.sum(jnp.where(e_tok == rk[k:k + 1], start_col, 0.0), axis=0, keepdims=True)
             + rk[k + 2:k + 3].astype(F32) for k in range(TOP_K)]
        dest_ref[t] = jnp.concatenate(d + [jnp.zeros((ROUTE_W - TOP_K, tm), F32)], axis=0).astype(I32)
        return carry

    lax.fori_loop(0, n_tiles, tile, 0)


def _plan_call(rank_rows, cnt):
    n_tiles = TOKENS // TM_MERGE
    nb_pad = -(-N_BLOCKS // LANES) * LANES
    dest, blk = pl.pallas_call(
        _plan_kernel,
        out_shape=(jax.ShapeDtypeStruct((n_tiles, ROUTE_W, TM_MERGE), I32),
                   jax.ShapeDtypeStruct((ROUTE_W, nb_pad), I32)),
        compiler_params=pltpu.CompilerParams(vmem_limit_bytes=VMEM_LIMIT),
        name="plan",
    )(rank_rows, cnt)
    dest1 = dest[:, 0, :].reshape(TOKENS)
    dest2 = dest[:, 1, :].reshape(TOKENS)
    tables = tuple(blk[r, :N_BLOCKS] for r in (0, 1, 3, 4))
    return dest1, dest2, tables, blk[2, :1]


def _sc_params():
    return pltpu.CompilerParams(use_tc_tiling_on_sc=True)


def _sc_mesh():
    return plsc.VectorSubcoreMesh(core_axis_name="core", subcore_axis_name="subcore")


def _sc_worker_base(per_worker):
    wid = lax.axis_index("subcore") * SC_CORES + lax.axis_index("core")
    return wid * per_worker


def _sc_scatter_rows(src, dest1, dest2, n_out):
    n = src.shape[0]
    per_worker = n // SC_WORKERS
    assert per_worker * SC_WORKERS == n and per_worker % SC_CHUNK == 0

    def body(src_hbm, d1_hbm, d2_hbm, out_hbm, idx_v, rows_v):
        base = _sc_worker_base(per_worker)

        @pl.loop(0, per_worker // SC_CHUNK)
        def _(j):
            start = pl.multiple_of(base + j * SC_CHUNK, SC_CHUNK)
            pltpu.sync_copy(src_hbm.at[pl.ds(start, SC_CHUNK)], rows_v)
            for d_hbm in (d1_hbm, d2_hbm):
                pltpu.sync_copy(d_hbm.at[pl.ds(start, SC_CHUNK)], idx_v)
                pltpu.sync_copy(rows_v, out_hbm.at[idx_v])

    return pl.kernel(
        body,
        out_type=jax.ShapeDtypeStruct((n_out, ROW_TILE_S, LANES), F32),
        mesh=_sc_mesh(),
        scratch_types=[pltpu.VMEM((SC_CHUNK,), I32),
                       pltpu.VMEM((SC_CHUNK, ROW_TILE_S, LANES), F32)],
        compiler_params=_sc_params(),
        name="sc_dispatch",
    )(src, dest1, dest2)


def _sc_gather_rows(table, idx):
    n = idx.shape[0]
    per_worker = n // SC_WORKERS
    half = SC_CHUNK // 2
    assert per_worker * SC_WORKERS == n and per_worker % SC_CHUNK == 0

    def body(table_hbm, idx_hbm, out_hbm, idx_v, rows_a, rows_b, gsem_a, gsem_b, wsem_a, wsem_b):
        base = pl.multiple_of(_sc_worker_base(per_worker), SC_CHUNK)
        pltpu.sync_copy(idx_hbm.at[pl.ds(base, per_worker)], idx_v)

        @pl.loop(0, per_worker // SC_CHUNK)
        def _(j):
            off_a = pl.multiple_of(j * SC_CHUNK, SC_CHUNK)
            off_b = pl.multiple_of(j * SC_CHUNK + half, half)
            ga = pltpu.async_copy(table_hbm.at[idx_v.at[pl.ds(off_a, half)]], rows_a, gsem_a)
            gb = pltpu.async_copy(table_hbm.at[idx_v.at[pl.ds(off_b, half)]], rows_b, gsem_b)
            ga.wait()
            wa = pltpu.async_copy(rows_a, out_hbm.at[pl.ds(base + off_a, half)], wsem_a)
            gb.wait()
            wb = pltpu.async_copy(rows_b, out_hbm.at[pl.ds(base + off_b, half)], wsem_b)
            wa.wait()
            wb.wait()

    return pl.kernel(
        body,
        out_type=jax.ShapeDtypeStruct((n, ROW_TILE_S, LANES), F32),
        mesh=_sc_mesh(),
        scratch_types=[pltpu.VMEM((per_worker,), I32),
                       pltpu.VMEM((half, ROW_TILE_S, LANES), F32),
                       pltpu.VMEM((half, ROW_TILE_S, LANES), F32),
                       pltpu.SemaphoreType.DMA, pltpu.SemaphoreType.DMA,
                       pltpu.SemaphoreType.DMA, pltpu.SemaphoreType.DMA],
        compiler_params=_sc_params(),
        name="sc_gather",
    )(table, idx)


def _weight_fetch(w_hbm, stage, sem, e, slot):
    return [pltpu.make_async_copy(w.at[e], st.at[slot], sem.at[slot]) for w, st in zip(w_hbm, stage)]


def _expert_kernel(be_ref, nv_ref, nx_ref, sl_ref, nu_ref, xs_ref, w1_hbm, w3_hbm, w2_hbm, y_ref,
                   w1s, w3s, w2s, w1b, w3b, w2b, sem):
    i = pl.program_id(0)
    e = be_ref[i]
    used = i < nu_ref[0]
    first = (i == 0) | (e != be_ref[jnp.maximum(i - 1, 0)])
    w_hbm = (w1_hbm, w3_hbm, w2_hbm)
    stage = (w1s, w3s, w2s)

    @pl.when(used & first)
    def _():
        slot = sl_ref[i]

        @pl.when(i == 0)
        def _():
            for cp in _weight_fetch(w_hbm, stage, sem, e, slot):
                cp.start()

        for cp in _weight_fetch(w_hbm, stage, sem, e, slot):
            cp.wait()
        w1b[...] = w1s[slot].astype(BF16)
        w3b[...] = w3s[slot].astype(BF16)
        w2b[...] = w2s[slot].astype(BF16)

        @pl.when(nx_ref[i] >= 0)
        def _():
            for cp in _weight_fetch(w_hbm, stage, sem, nx_ref[i], 1 - slot):
                cp.start()

    for part in range(MOE_BLK // MOE_SUB):
        row0 = part * MOE_SUB
        live = used & (nv_ref[i] > row0)

        @pl.when(live)
        def _():
            row = lax.broadcasted_iota(I32, (MOE_SUB, D_MODEL), 0) + row0
            xb = jnp.where(row < nv_ref[i], _load_row_tiles(xs_ref, row0, MOE_SUB), 0.0).astype(BF16)
            a = jnp.dot(xb, w1b[...], preferred_element_type=F32)
            g = jnp.dot(xb, w3b[...], preferred_element_type=F32)
            hmid = (a * _sigmoid(a)) * g
            _store_row_tiles(y_ref, jnp.dot(hmid.astype(BF16), w2b[...], preferred_element_type=F32),
                             row0)

        @pl.when(jnp.logical_not(live))
        def _():
            y_ref[pl.ds(row0 * ROW_TILE_S, MOE_SUB * ROW_TILE_S), :] = jnp.zeros(
                (MOE_SUB * ROW_TILE_S, LANES), F32)


def _expert_call(tables, n_used, xs, w1, w3, w2):
    n_tab = len(tables)
    grid_spec = pltpu.PrefetchScalarGridSpec(
        num_scalar_prefetch=n_tab + 1,
        grid=(N_BLOCKS,),
        in_specs=[_row_tile_spec(MOE_BLK, lambda i, *pf: (jnp.minimum(i, pf[n_tab][0] - 1), 0)),
                  pl.BlockSpec(memory_space=pl.ANY),
                  pl.BlockSpec(memory_space=pl.ANY),
                  pl.BlockSpec(memory_space=pl.ANY)],
        out_specs=_row_tile_spec(MOE_BLK, lambda i, *pf: (i, 0)),
        scratch_shapes=[pltpu.VMEM((2, D_MODEL, D_EXPERT), F32),
                        pltpu.VMEM((2, D_MODEL, D_EXPERT), F32),
                        pltpu.VMEM((2, D_EXPERT, D_MODEL), F32),
                        pltpu.VMEM((D_MODEL, D_EXPERT), BF16),
                        pltpu.VMEM((D_MODEL, D_EXPERT), BF16),
                        pltpu.VMEM((D_EXPERT, D_MODEL), BF16),
                        pltpu.SemaphoreType.DMA((2,))],
    )
    return pl.pallas_call(
        _expert_kernel,
        grid_spec=grid_spec,
        out_shape=jax.ShapeDtypeStruct((N_BLOCKS * MOE_BLK * ROW_TILE_S, LANES), F32),
        compiler_params=_cparams(("arbitrary",)),
        name="experts",
    )(*tables, n_used, xs, w1, w3, w2)


def _combine_rows_kernel(x1_ref, rf_ref, mod_ref, ya_ref, yb_ref, *rest):
    o_ref = rest[-1]
    rf = rf_ref[...]
    gate2 = mod_ref[0, 5:6, :]
    moe = rf[:, 0:1] * _load_row_tiles(ya_ref) + rf[:, 1:2] * _load_row_tiles(yb_ref)
    o_ref[...] = x1_ref[...] + gate2 * moe


def _combine_rows_call(x1, rf, mod, yg, part, prev=None):
    nb = SEQ // TM_ROW
    n_half = TOKENS // TM_ROW // 2
    t0 = part * n_half
    in_specs = [pl.BlockSpec((TM_ROW, D_MODEL), lambda i: (t0 + i, 0)),
                pl.BlockSpec((TM_ROW, ROUTE_W), lambda i: (t0 + i, 0)),
                pl.BlockSpec((1, N_MOD, D_MODEL), lambda i: ((t0 + i) // nb, 0, 0)),
                _row_tile_spec(TM_ROW, lambda i: (i, 0)),
                _row_tile_spec(TM_ROW, lambda i: (i + n_half, 0))]
    args = [x1, rf, mod, yg, yg]
    aliases = {}
    if prev is not None:
        in_specs.append(pl.BlockSpec(memory_space=pl.ANY))
        args.append(prev)
        aliases = {len(args) - 1: 0}
    return pl.pallas_call(
        _combine_rows_kernel,
        grid=(n_half,),
        in_specs=in_specs,
        out_specs=pl.BlockSpec((TM_ROW, D_MODEL), lambda i: (t0 + i, 0)),
        out_shape=jax.ShapeDtypeStruct((TOKENS, D_MODEL), F32),
        input_output_aliases=aliases,
        compiler_params=_cparams(("parallel",)),
        name="combine",
    )(*args)


def kernel(x, c, positions, ada_w, ada_b, norm1_g, norm2_g, w_in, gla_alpha_up, gla_alpha_b,
           gla_out_g, diff_q_g, diff_k_g, diff_lq1, diff_lk1, diff_lq2, diff_lk2, diff_out_g,
           w_branch_a, w_branch_b, w_out, router_group_w, router_group_b, router_expert_w,
           router_expert_b, expert_w1, expert_w3, expert_w2):
    assert x.shape == (BATCH, SEQ, D_MODEL) and ada_w.shape[0] == 1
    x2 = x.reshape(TOKENS, D_MODEL)

    w_in_p = _wprep_call(w_in[0])
    au_pad = jnp.zeros((LR_PAD, GLA_QK_W), F32).at[:GLA_GATE_RANK].set(gla_alpha_up[0])
    au_hi = au_pad.astype(BF16)
    au_lo = (au_pad - au_hi.astype(F32)).astype(BF16)
    gid = jnp.arange(DIFF_QK_W) // DIFF_DH
    bd = jnp.where(gid[:, None] == gid[None, :], 1.0 / DIFF_DH, 0.0).astype(BF16)
    qg_row = jnp.tile(diff_q_g[0], DIFF_QK_W // DIFF_DH).reshape(1, DIFF_QK_W)
    kg_row = jnp.tile(diff_k_g[0], DIFF_QK_W // DIFF_DH).reshape(1, DIFF_QK_W)
    inv_freq = ROPE_THETA ** (-jnp.arange(ROPE_HALF, dtype=F32) / ROPE_HALF)
    invf64 = jnp.concatenate([inv_freq, inv_freq, jnp.zeros((DIFF_DH - ROPE_DIM,), F32)])
    invf_row = jnp.tile(invf64, LANES // DIFF_DH).reshape(1, LANES)
    pos_col = positions.reshape(TOKENS, 1)
    lam_p = jnp.concatenate([diff_lq1, diff_lk1, diff_lq2, diff_lk2], axis=0)
    wr = (jnp.zeros((D_MODEL, LANES), F32)
          .at[:, :N_GROUPS].set(router_group_w[0])
          .at[:, N_GROUPS:N_GROUPS + N_EXPERTS].set(router_expert_w[0]))
    wr_hi = wr.astype(BF16)
    wr_lo = (wr - wr_hi.astype(F32)).astype(BF16)
    br = (jnp.zeros((1, LANES), F32)
          .at[0, :N_GROUPS].set(router_group_b[0])
          .at[0, N_GROUPS:N_GROUPS + N_EXPERTS].set(router_expert_b[0]))

    mod = _mod_call(c, ada_w[0], ada_b[0])
    gqk, gv, gr, lr, qT, kk, vT, sa, sb = _in_call(
        x2, mod, norm1_g, w_in_p, bd, qg_row, kg_row, pos_col, invf_row)
    o_a = _gla_call(gqk, gv, gr, lr, au_hi, au_lo, gla_alpha_b, gla_out_g)
    o_b = _attn_call(qT, kk, vT, lam_p, diff_out_g.reshape(DIFF_DV, 1))
    x1, h2, rf, rank, cnt = _merge_call(
        x2, o_a, o_b, sa, sb, mod, norm2_g, w_branch_a[0].astype(BF16),
        w_branch_b[0].astype(BF16), w_out[0].astype(BF16), wr_hi, wr_lo, br)
    dest1, dest2, tables, n_used = _plan_call(rank, cnt)

    xs = _sc_scatter_rows(_as_row_tiles(h2), dest1, dest2, N_BLOCKS * MOE_BLK)
    y = _expert_call(tables, n_used, _as_2d(xs), expert_w1[0], expert_w3[0], expert_w2[0])
    y3 = _as_row_tiles(y)
    half = TOKENS // 2
    out = None
    for part in range(2):
        tok = slice(part * half, (part + 1) * half)
        yg = _sc_gather_rows(y3, jnp.concatenate([dest1[tok], dest2[tok]]))
        out = _combine_rows_call(x1, rf, mod, _as_2d(yg), part, out)
    return out.reshape(BATCH, SEQ, D_MODEL)
```

```python
import math

import jax
import jax.numpy as jnp
from jax import lax
from jax.experimental import pallas as pl
from jax.experimental.pallas import tpu as pltpu
from jax.experimental.pallas import tpu_sc as plsc

F32 = jnp.float32
BF16 = jnp.bfloat16
I32 = jnp.int32

D_MODEL = 1024
BATCH = 4
SEQ = 4096
TOKENS = BATCH * SEQ
N_MOD = 6
NORM_EPS = 1e-6

GLA_HEADS = 4
GLA_DK = 64
GLA_DV = 128
GLA_GATE_RANK = 16
GLA_GATE_TAU = 16.0
GLA_CHUNK = 64
GLA_QK_W = GLA_HEADS * GLA_DK
GLA_V_W = GLA_HEADS * GLA_DV

DIFF_HEADS = 4
DIFF_DH = 64
DIFF_DV = 2 * DIFF_DH
DIFF_QK_W = DIFF_HEADS * 2 * DIFF_DH
DIFF_V_W = DIFF_HEADS * DIFF_DV
ROPE_THETA = 500000.0
ROPE_DIM = DIFF_DH // 4
ROPE_HALF = ROPE_DIM // 2
NEG_INF = -1e30
LAMBDA_INIT = 0.8 - 0.6 * 1.0

N_GROUPS = 4
EXPERTS_PER_GROUP = 8
N_EXPERTS = N_GROUPS * EXPERTS_PER_GROUP
TOP_K = 2
D_EXPERT = 512

LANES = 128
SUBLANES = 8
ROW_TILE_S = D_MODEL // LANES
SC_CORES = 2
SC_SUBCORES = 16
SC_WORKERS = SC_CORES * SC_SUBCORES
SC_CHUNK = 64
LR_PAD = LANES
D_IN_PAD = 2 * GLA_QK_W + 2 * GLA_V_W + 2 * DIFF_QK_W + DIFF_V_W + 2 * D_MODEL + LR_PAD

TM_IN = 512
TQ = 512
V_ROWS = DIFF_DV + 16
ATT_SUB = 32
Q_SCALE = DIFF_DH ** -0.5 * math.log2(math.e)
N_KV = SEQ // TQ
TT_GLA = 512
TM_MERGE = 512
TM_ROW = 256
MOE_BLK = 512
MOE_SUB = 256
N_BLOCKS = (TOKENS * TOP_K + N_EXPERTS * (MOE_BLK - 1) + MOE_BLK - 1) // MOE_BLK
ROUTE_W = 8

VMEM_LIMIT = 56 * 1024 * 1024


def _cparams(sem):
    return pltpu.CompilerParams(dimension_semantics=sem, vmem_limit_bytes=VMEM_LIMIT)


def _sigmoid(x):
    return 1.0 / (1.0 + jnp.exp(-x))


def _row_tile_spec(rows, index_map):
    return pl.BlockSpec((rows * ROW_TILE_S, LANES), index_map)


def _as_row_tiles(a2d):
    return a2d.reshape(a2d.shape[0] // ROW_TILE_S, ROW_TILE_S, LANES)


def _as_2d(a3d):
    return a3d.reshape(a3d.shape[0] * ROW_TILE_S, LANES)


def _store_row_tiles(ref, val, row0=0):
    for s in range(ROW_TILE_S):
        ref[pl.ds(row0 * ROW_TILE_S + s, val.shape[0], stride=ROW_TILE_S), :] = (
            val[:, s * LANES:(s + 1) * LANES])


def _load_row_tiles(ref, row0=0, rows=None):
    rows = ref.shape[0] // ROW_TILE_S if rows is None else rows
    return jnp.concatenate(
        [ref[pl.ds(row0 * ROW_TILE_S + s, rows, stride=ROW_TILE_S), :] for s in range(ROW_TILE_S)],
        axis=1)


def _mod_kernel(ct_ref, w_ref, b_ref, o_ref):
    ct = ct_ref[...]
    ca = ct * _sigmoid(ct)
    w = w_ref[...]
    rows = []
    for b in range(BATCH):
        rows.append(jnp.sum(w * ca[:, b:b + 1], axis=0, keepdims=True) + b_ref[...])
    rows.append(jnp.zeros((8 - BATCH, w.shape[1]), F32))
    o_ref[...] = jnp.concatenate(rows, axis=0)


def _mod_call(c, ada_w, ada_b):
    tn = D_MODEL
    ct = jnp.zeros((D_MODEL, 8), F32).at[:, :BATCH].set(c.T)
    out = pl.pallas_call(
        _mod_kernel,
        grid=(N_MOD,),
        in_specs=[
            pl.BlockSpec((D_MODEL, 8), lambda j: (0, 0)),
            pl.BlockSpec((D_MODEL, tn), lambda j: (0, j)),
            pl.BlockSpec((1, tn), lambda j: (0, j)),
        ],
        out_specs=pl.BlockSpec((8, tn), lambda j: (0, j)),
        out_shape=jax.ShapeDtypeStruct((8, N_MOD * D_MODEL), F32),
        compiler_params=_cparams(("arbitrary",)),
        name="mod",
    )(ct, ada_w, ada_b.reshape(1, N_MOD * D_MODEL))
    return out[:BATCH].reshape(BATCH, N_MOD, D_MODEL)


def _wprep_kernel(w_ref, o_ref):
    w = w_ref[...]
    lr0 = 2 * GLA_QK_W + 2 * GLA_V_W
    rest = D_IN_PAD - LR_PAD - lr0
    o_ref[:, :lr0] = w[:, :lr0].astype(BF16)
    o_ref[:, lr0:lr0 + rest] = w[:, lr0 + GLA_GATE_RANK:].astype(BF16)
    lr = jnp.concatenate([w[:, lr0:lr0 + GLA_GATE_RANK],
                          jnp.zeros((w.shape[0], LR_PAD - GLA_GATE_RANK), F32)], axis=1)
    o_ref[:, lr0 + rest:] = lr.astype(BF16)


def _wprep_call(w_in0):
    d_in = w_in0.shape[1]
    rows = 128
    return pl.pallas_call(
        _wprep_kernel,
        grid=(D_MODEL // rows,),
        in_specs=[pl.BlockSpec((rows, d_in), lambda i: (i, 0))],
        out_specs=pl.BlockSpec((rows, D_IN_PAD), lambda i: (i, 0)),
        out_shape=jax.ShapeDtypeStruct((D_MODEL, D_IN_PAD), BF16),
        compiler_params=_cparams(("parallel",)),
        name="w_prep",
    )(w_in0)


def _in_kernel(x_ref, mod_ref, g1_ref, w_ref, bd_ref, qg_ref, kg_ref, pos_ref, invf_ref,
               gqk_ref, gv_ref, gr_ref, lr_ref, qT_ref, k_ref, vT_ref, sa_ref, sb_ref,
               cos_scr, sin_scr):
    ang = pos_ref[...].astype(F32) * invf_ref[...]
    cos_scr[...] = jnp.cos(ang)
    sin_scr[...] = jnp.sin(ang)

    x = x_ref[...]
    shift1 = mod_ref[0, 0:1, :]
    scale1 = mod_ref[0, 1:2, :]
    ms = jnp.mean(x * x, axis=-1, keepdims=True)
    h = (x * lax.rsqrt(ms + NORM_EPS) * g1_ref[...]) * (1.0 + scale1) + shift1
    hb = h.astype(BF16)

    def proj(c0, c1):
        return jnp.dot(hb, w_ref[:, c0:c1], preferred_element_type=F32)

    tm = x.shape[0]
    cos4 = jnp.concatenate([cos_scr[...]] * DIFF_HEADS, axis=1)
    sin4 = jnp.concatenate([sin_scr[...]] * DIFF_HEADS, axis=1)
    lane = lax.broadcasted_iota(I32, (tm, DIFF_QK_W), 1)
    first_half = (lane % DIFF_DH) < ROPE_HALF
    bd = bd_ref[...]

    def norm_rope(t, gain_row):
        t2 = t * t
        hi = t2.astype(BF16)
        lo = (t2 - hi.astype(F32)).astype(BF16)
        gms = (jnp.dot(hi, bd, preferred_element_type=F32)
               + jnp.dot(lo, bd, preferred_element_type=F32))
        t = t * lax.rsqrt(gms + NORM_EPS) * gain_row
        nxt = pltpu.roll(t, DIFF_QK_W - ROPE_HALF, 1)
        prv = pltpu.roll(t, ROPE_HALF, 1)
        return t * cos4 + jnp.where(first_half, -nxt, prv) * sin4

    dq_raw = proj(1536, 2048)
    dk_raw = proj(2048, 2560)
    gqk_ref[...] = proj(0, 512).astype(BF16)
    dq = norm_rope(dq_raw, qg_ref[...]) * Q_SCALE
    gv_ref[...] = proj(512, 1024).astype(BF16)
    dk = norm_rope(dk_raw, kg_ref[...])
    dv = proj(2560, 3072)
    for hd in range(DIFF_HEADS):
        sl = slice(hd * LANES, (hd + 1) * LANES)
        qT_ref[0, hd, 0] = dq[:, sl].T.astype(BF16)
        k_ref[0, hd, 0] = dk[:, sl].astype(BF16)
    gr_ref[...] = proj(1024, 1536).astype(BF16)
    for hd in range(DIFF_HEADS):
        sl = slice(hd * LANES, (hd + 1) * LANES)
        vT_ref[0, hd, 0] = jnp.concatenate(
            [dv[:, sl].T, jnp.ones((V_ROWS - DIFF_DV, tm), F32)], axis=0).astype(BF16)
    sa_ref[...] = _sigmoid(proj(3072, 4096)).astype(BF16)
    sb_ref[...] = _sigmoid(proj(4096, 5120)).astype(BF16)
    lr_ref[...] = proj(5120, 5248)


def _in_call(x2, mod, norm1_g, w_in_p, bd, qg_row, kg_row, pos_col, invf_row):
    nb = SEQ // TM_IN
    tok_spec = lambda w: pl.BlockSpec((TM_IN, w), lambda i: (i, 0))
    const2 = lambda r, c: pl.BlockSpec((r, c), lambda i: (0, 0))
    out_shapes = (
        jax.ShapeDtypeStruct((TOKENS, 2 * GLA_QK_W), BF16),
        jax.ShapeDtypeStruct((TOKENS, GLA_V_W), BF16),
        jax.ShapeDtypeStruct((TOKENS, GLA_V_W), BF16),
        jax.ShapeDtypeStruct((TOKENS, LR_PAD), F32),
        jax.ShapeDtypeStruct((BATCH, DIFF_HEADS, N_KV, LANES, TQ), BF16),
        jax.ShapeDtypeStruct((BATCH, DIFF_HEADS, N_KV, TQ, LANES), BF16),
        jax.ShapeDtypeStruct((BATCH, DIFF_HEADS, N_KV, V_ROWS, TQ), BF16),
        jax.ShapeDtypeStruct((TOKENS, D_MODEL), BF16),
        jax.ShapeDtypeStruct((TOKENS, D_MODEL), BF16),
    )
    out_specs = (
        tok_spec(2 * GLA_QK_W), tok_spec(GLA_V_W), tok_spec(GLA_V_W), tok_spec(LR_PAD),
        pl.BlockSpec((1, DIFF_HEADS, 1, LANES, TM_IN), lambda i: (i // nb, 0, i % nb, 0, 0)),
        pl.BlockSpec((1, DIFF_HEADS, 1, TM_IN, LANES), lambda i: (i // nb, 0, i % nb, 0, 0)),
        pl.BlockSpec((1, DIFF_HEADS, 1, V_ROWS, TM_IN), lambda i: (i // nb, 0, i % nb, 0, 0)),
        tok_spec(D_MODEL), tok_spec(D_MODEL),
    )
    return pl.pallas_call(
        _in_kernel,
        grid=(TOKENS // TM_IN,),
        in_specs=[
            tok_spec(D_MODEL),
            pl.BlockSpec((1, N_MOD, D_MODEL), lambda i: (i // nb, 0, 0)),
            const2(1, D_MODEL),
            pl.BlockSpec((D_MODEL, D_IN_PAD), lambda i: (0, 0), pipeline_mode=pl.Buffered(1)),
            const2(DIFF_QK_W, DIFF_QK_W),
            const2(1, DIFF_QK_W), const2(1, DIFF_QK_W),
            tok_spec(1),
            const2(1, LANES),
        ],
        out_specs=out_specs,
        out_shape=out_shapes,
        scratch_shapes=[pltpu.VMEM((TM_IN, LANES), F32), pltpu.VMEM((TM_IN, LANES), F32)],
        compiler_params=_cparams(("parallel",)),
        name="in_proj",
    )(x2, mod, norm1_g, w_in_p, bd, qg_row, kg_row, pos_col, invf_row)


def _gla_kernel(qk_ref, v_ref, r_ref, lr_ref, auh_ref, aul_ref, ab_ref, og_ref, o_ref,
                state_ref, oacc_ref, snew_ref):
    tt = qk_ref.shape[0]
    n_chunks = tt // GLA_CHUNK

    @pl.when(pl.program_id(1) == 0)
    def _():
        state_ref[...] = jnp.zeros_like(state_ref)

    lr = lr_ref[...]
    lr_hi = lr.astype(BF16)
    lr_lo = (lr - lr_hi.astype(F32)).astype(BF16)
    z = (jnp.dot(lr_hi, auh_ref[...], preferred_element_type=F32)
         + jnp.dot(lr_lo, auh_ref[...], preferred_element_type=F32)
         + jnp.dot(lr_hi, aul_ref[...], preferred_element_type=F32)) + ab_ref[...]
    g = (jnp.minimum(z, 0.0) - jnp.log(1.0 + jnp.exp(-jnp.abs(z)))) * (1.0 / GLA_GATE_TAU)

    row = lax.broadcasted_iota(I32, (tt, GLA_QK_W), 0) % GLA_CHUNK
    b = g
    step = 1
    while step < GLA_CHUNK:
        b = b + jnp.where(row >= step, pltpu.roll(b, step, 0), 0.0)
        step *= 2

    b_last_rows = [b[c * GLA_CHUNK + GLA_CHUNK - 1:(c + 1) * GLA_CHUNK, :] for c in range(n_chunks)]
    b_last = jnp.concatenate(
        [jnp.broadcast_to(bl, (GLA_CHUNK, GLA_QK_W)) for bl in b_last_rows], axis=0)

    qk = qk_ref[...].astype(F32)
    q = qk[:, :GLA_QK_W] * (GLA_DK ** -0.5)
    k = qk[:, GLA_QK_W:]
    q_in = (q * jnp.exp(b)).astype(BF16)
    k_in = (k * jnp.exp(-b)).astype(BF16)
    k_dec = (k * jnp.exp(b_last - b)).astype(BF16)

    ci = lax.broadcasted_iota(I32, (GLA_CHUNK, GLA_CHUNK), 0)
    cj = lax.broadcasted_iota(I32, (GLA_CHUNK, GLA_CHUNK), 1)
    causal = ci >= cj

    pairs = [(c, hd) for c in range(n_chunks) for hd in range(GLA_HEADS)]

    def rows(c):
        return slice(c * GLA_CHUNK, (c + 1) * GLA_CHUNK)

    def kcols(hd):
        return slice(hd * GLA_DK, (hd + 1) * GLA_DK)

    def vcols(hd):
        return slice(hd * GLA_DV, (hd + 1) * GLA_DV)

    att = {}
    for c, hd in pairs:
        a = lax.dot_general(q_in[rows(c), kcols(hd)], k_in[rows(c), kcols(hd)],
                            (((1,), (1,)), ((), ())), preferred_element_type=F32)
        att[c, hd] = jnp.where(causal, a, 0.0).astype(BF16)
    for c, hd in pairs:
        oacc_ref[rows(c), vcols(hd)] = jnp.dot(att[c, hd], v_ref[rows(c), vcols(hd)],
                                               preferred_element_type=F32)
    for c, hd in pairs:
        snew_ref[c * GLA_HEADS + hd] = lax.dot_general(
            k_dec[rows(c), kcols(hd)], v_ref[rows(c), vcols(hd)],
            (((0,), (0,)), ((), ())), preferred_element_type=F32)

    decay_rows = jnp.exp(jnp.concatenate(
        b_last_rows + [jnp.zeros((LANES - n_chunks, GLA_QK_W), F32)], axis=0))
    decay_cols = decay_rows.T
    states = [state_ref[hd] for hd in range(GLA_HEADS)]
    for c in range(n_chunks):
        for hd in range(GLA_HEADS):
            s_prev = states[hd]
            oacc_ref[rows(c), vcols(hd)] += jnp.dot(
                q_in[rows(c), kcols(hd)], s_prev.astype(BF16), preferred_element_type=F32)
            dcol = decay_cols[kcols(hd), c:c + 1]
            states[hd] = s_prev * dcol + snew_ref[c * GLA_HEADS + hd]
    for hd in range(GLA_HEADS):
        state_ref[hd] = states[hd]

    for hd in range(GLA_HEADS):
        vs = slice(hd * GLA_DV, (hd + 1) * GLA_DV)
        oh = oacc_ref[:, vs]
        ms = jnp.mean(oh * oh, axis=-1, keepdims=True)
        y = oh * lax.rsqrt(ms + NORM_EPS) * og_ref[...]
        r = r_ref[:, vs].astype(F32)
        o_ref[:, vs] = (y * (r * _sigmoid(r))).astype(BF16)


def _gla_call(gqk, gv, gr, lr, au_hi, au_lo, ab_row, og_row):
    nt = SEQ // TT_GLA
    tok = lambda w: pl.BlockSpec((TT_GLA, w), lambda b, t: (b * nt + t, 0))
    const2 = lambda r, c: pl.BlockSpec((r, c), lambda b, t: (0, 0))
    return pl.pallas_call(
        _gla_kernel,
        grid=(BATCH, nt),
        in_specs=[tok(2 * GLA_QK_W), tok(GLA_V_W), tok(GLA_V_W), tok(LR_PAD),
                  const2(LR_PAD, GLA_QK_W), const2(LR_PAD, GLA_QK_W), const2(1, GLA_QK_W),
                  const2(1, GLA_DV)],
        out_specs=tok(GLA_V_W),
        out_shape=jax.ShapeDtypeStruct((TOKENS, GLA_V_W), BF16),
        scratch_shapes=[pltpu.VMEM((GLA_HEADS, GLA_DK, GLA_DV), F32),
                        pltpu.VMEM((TT_GLA, GLA_V_W), F32),
                        pltpu.VMEM((TT_GLA // GLA_CHUNK * GLA_HEADS, GLA_DK, GLA_DV), F32)],
        compiler_params=_cparams(("parallel", "arbitrary")),
        name="gla",
    )(gqk, gv, gr, lr, au_hi, au_lo, ab_row, og_row)


def _attn_kernel(qT_ref, k_ref, vT_ref, lam_ref, og_ref, o_ref,
                 q_scr, s_scr, p_scr, acc_scr, m_scr):
    _attn_load_q(qT_ref, q_scr, 0, 0)
    s_scr[0] = jnp.dot(k_ref[0, 0, 0], q_scr[0, 0], preferred_element_type=F32)

    def q_block(i, carry):
        _attn_q_block(i, qT_ref, k_ref, vT_ref, lam_ref, og_ref, o_ref,
                      q_scr, s_scr, p_scr, acc_scr, m_scr)
        return carry

    lax.fori_loop(0, N_KV, q_block, 0)


def _attn_load_q(qT_ref, q_scr, i, slot):
    qT = qT_ref[0, 0, i]
    rowq = lax.broadcasted_iota(I32, qT.shape, 0)
    zero = jnp.zeros_like(qT)
    q_scr[slot, 0] = jnp.where(rowq < DIFF_DH, qT, zero)
    q_scr[slot, 1] = jnp.where(rowq >= DIFF_DH, qT, zero)


def _attn_q_block(i, qT_ref, k_ref, vT_ref, lam_ref, og_ref, o_ref,
                  q_scr, s_scr, p_scr, acc_scr, m_scr):
    slot = i % 2
    m_scr[...] = jnp.full(m_scr.shape, NEG_INF, F32)
    acc_scr[...] = jnp.zeros(acc_scr.shape, F32)
    n_sub = TQ // ATT_SUB

    def fold8(t, op):
        return op(t.reshape(t.shape[0] // SUBLANES, SUBLANES, TQ), axis=0)

    def scores(c, j):
        s_scr[c] = jnp.dot(k_ref[0, 0, j], q_scr[slot, c], preferred_element_type=F32)

    def load_s(c, r, masked):
        s = s_scr[c, r * ATT_SUB:(r + 1) * ATT_SUB, :]
        if masked:
            key_i = lax.broadcasted_iota(I32, (ATT_SUB, TQ), 0) + r * ATT_SUB
            qry_i = lax.broadcasted_iota(I32, (ATT_SUB, TQ), 1)
            s = jnp.where(key_i <= qry_i, s, NEG_INF)
        return s

    def softmax_pv(c, j, masked):
        m8 = fold8(load_s(c, 0, masked), jnp.max)
        for r in range(1, n_sub):
            m8 = jnp.maximum(m8, fold8(load_s(c, r, masked), jnp.max))
        m_old = m_scr[c]
        m_new = jnp.maximum(m_old, jnp.max(m8, axis=0, keepdims=True))
        alpha = jnp.exp2(m_old - m_new)
        for r in range(n_sub):
            p = jnp.exp2(load_s(c, r, masked) - m_new)
            p_scr[c, r * ATT_SUB:(r + 1) * ATT_SUB, :] = p.astype(BF16)
        m_scr[c] = m_new
        acc_scr[c] = acc_scr[c] * alpha + jnp.dot(vT_ref[0, 0, j], p_scr[c],
                                                  preferred_element_type=F32)

    def body(j, carry):
        scores(1, j)
        softmax_pv(0, j, False)
        scores(0, j + 1)
        softmax_pv(1, j, False)
        return carry

    lax.fori_loop(0, i, body, 0)
    scores(1, i)
    softmax_pv(0, i, True)
    softmax_pv(1, i, True)

    nxt = jnp.minimum(i + 1, N_KV - 1)
    _attn_load_q(qT_ref, q_scr, nxt, 1 - slot)
    s_scr[0] = jnp.dot(k_ref[0, 0, 0], q_scr[1 - slot, 0], preferred_element_type=F32)

    l1 = acc_scr[0, DIFF_DV:DIFF_DV + 1, :]
    l2 = acc_scr[1, DIFF_DV:DIFF_DV + 1, :]

    lam_p = lam_ref[...]
    lam = (jnp.exp(jnp.sum(lam_p[0:1] * lam_p[1:2], axis=1, keepdims=True))
           - jnp.exp(jnp.sum(lam_p[2:3] * lam_p[3:4], axis=1, keepdims=True)) + LAMBDA_INIT)
    oT = acc_scr[0, :DIFF_DV, :] / l1 - lam * (acc_scr[1, :DIFF_DV, :] / l2)
    ms = jnp.mean(oT * oT, axis=0, keepdims=True)
    y = oT * lax.rsqrt(ms + NORM_EPS) * og_ref[...] * (1.0 - LAMBDA_INIT)
    o_ref[pl.ds(pl.multiple_of(i * TQ, TQ), TQ), :] = y.T.astype(BF16)


def _attn_call(qT, kk, vT, lam_p, og_col):
    return pl.pallas_call(
        _attn_kernel,
        grid=(BATCH, DIFF_HEADS),
        in_specs=[
            pl.BlockSpec((1, 1, N_KV, LANES, TQ), lambda b, h: (b, h, 0, 0, 0)),
            pl.BlockSpec((1, 1, N_KV, TQ, LANES), lambda b, h: (b, h, 0, 0, 0)),
            pl.BlockSpec((1, 1, N_KV, V_ROWS, TQ), lambda b, h: (b, h, 0, 0, 0)),
            pl.BlockSpec((4, DIFF_DH), lambda b, h: (0, 0)),
            pl.BlockSpec((DIFF_DV, 1), lambda b, h: (0, 0)),
        ],
        out_specs=pl.BlockSpec((SEQ, DIFF_DV), lambda b, h: (b, h)),
        out_shape=jax.ShapeDtypeStruct((TOKENS, DIFF_V_W), BF16),
        scratch_shapes=[pltpu.VMEM((2, 2, LANES, TQ), BF16),
                        pltpu.VMEM((2, TQ, TQ), F32),
                        pltpu.VMEM((2, TQ, TQ), BF16),
                        pltpu.VMEM((2, V_ROWS, TQ), F32),
                        pltpu.VMEM((2, 1, TQ), F32)],
        compiler_params=_cparams(("parallel", "parallel")),
        name="attn",
    )(qT, kk, vT, lam_p, og_col)


def _merge_kernel(x_ref, oa_ref, ob_ref, sa_ref, sb_ref, mod_ref, g2_ref, wa_ref, wb_ref, wo_ref,
                  wrh_ref, wrl_ref, br_ref, x1_ref, h2_ref, rf_ref, rank_ref, cnt_ref,
                  carry_ref, tri_ref):
    tm = x_ref.shape[0]

    @pl.when(pl.program_id(0) == 0)
    def _():
        carry_ref[...] = jnp.zeros_like(carry_ref)
        ti = lax.broadcasted_iota(I32, (tm, tm), 0)
        tj = lax.broadcasted_iota(I32, (tm, tm), 1)
        tri_ref[...] = jnp.where(ti > tj, 1.0, 0.0).astype(BF16)

    ma = jnp.dot(oa_ref[...], wa_ref[...], preferred_element_type=F32)
    mb = jnp.dot(ob_ref[...], wb_ref[...], preferred_element_type=F32)
    merged = sa_ref[...].astype(F32) * ma + sb_ref[...].astype(F32) * mb
    y = jnp.dot(merged.astype(BF16), wo_ref[...], preferred_element_type=F32)
    gate1 = mod_ref[0, 2:3, :]
    shift2 = mod_ref[0, 3:4, :]
    scale2 = mod_ref[0, 4:5, :]
    x1 = x_ref[...] + gate1 * y
    x1_ref[...] = x1
    ms = jnp.mean(x1 * x1, axis=-1, keepdims=True)
    h2 = (x1 * lax.rsqrt(ms + NORM_EPS) * g2_ref[...]) * (1.0 + scale2) + shift2
    _store_row_tiles(h2_ref, h2)

    h2_hi = h2.astype(BF16)
    h2_lo = (h2 - h2_hi.astype(F32)).astype(BF16)
    logits = (jnp.dot(h2_hi, wrh_ref[...], preferred_element_type=F32)
              + jnp.dot(h2_lo, wrh_ref[...], preferred_element_type=F32)
              + jnp.dot(h2_hi, wrl_ref[...], preferred_element_type=F32)) + br_ref[...]
    lane = lax.broadcasted_iota(I32, (tm, LANES), 1).astype(F32)
    ninf = -jnp.inf
    big = float(LANES)

    def first_argmax(v):
        vmax = jnp.max(v, axis=1, keepdims=True)
        idx = jnp.min(jnp.where(v == vmax, lane, big), axis=1, keepdims=True)
        return vmax, idx

    gl = jnp.where(lane < N_GROUPS, logits, ninf)
    gmax, gidx = first_argmax(gl)
    p_top = 1.0 / jnp.sum(jnp.exp(gl - gmax), axis=1, keepdims=True)
    lo = N_GROUPS + EXPERTS_PER_GROUP * gidx
    el = jnp.where((lane >= lo) & (lane < lo + EXPERTS_PER_GROUP), logits, ninf)
    e1max, e1 = first_argmax(el)
    e2max, e2 = first_argmax(jnp.where(lane == e1, ninf, el))
    t = jnp.exp(e2max - e1max)
    w1 = 1.0 / (1.0 + t)
    w2 = t / (1.0 + t)
    col = lax.broadcasted_iota(I32, (tm, ROUTE_W), 1)
    rf_ref[...] = jnp.where(col == 0, p_top * w1, jnp.where(col == 1, p_top * w2, 0.0))

    x1id = e1 - N_GROUPS
    x2id = e2 - N_GROUPS
    hit1 = lane == x1id
    hit2 = lane == x2id
    onehot = jnp.where(hit1 | hit2, 1.0, 0.0)
    before = jnp.dot(tri_ref[...], onehot.astype(BF16), preferred_element_type=F32) + carry_ref[...]
    r1 = jnp.sum(jnp.where(hit1, before, 0.0), axis=1, keepdims=True)
    r2 = jnp.sum(jnp.where(hit2, before, 0.0), axis=1, keepdims=True)
    carry_ref[...] = carry_ref[...] + jnp.sum(onehot, axis=0, keepdims=True)
    cnt_ref[...] = carry_ref[...]
    cols = jnp.where(lane == 0.0, x1id, jnp.where(lane == 1.0, x2id,
                                                  jnp.where(lane == 2.0, r1,
                                                            jnp.where(lane == 3.0, r2, 0.0))))
    rank_ref[0] = cols.T[0:ROUTE_W, :].astype(I32)


def _merge_call(x2, o_a, o_b, sa, sb, mod, norm2_g, wa, wb, wo, wr_hi, wr_lo, br):
    nb = SEQ // TM_MERGE
    tok = lambda w: pl.BlockSpec((TM_MERGE, w), lambda i: (i, 0))
    const2 = lambda r, c: pl.BlockSpec((r, c), lambda i: (0, 0))
    return pl.pallas_call(
        _merge_kernel,
        grid=(TOKENS // TM_MERGE,),
        in_specs=[tok(D_MODEL), tok(GLA_V_W), tok(DIFF_V_W), tok(D_MODEL), tok(D_MODEL),
                  pl.BlockSpec((1, N_MOD, D_MODEL), lambda i: (i // nb, 0, 0)),
                  const2(1, D_MODEL),
                  const2(GLA_V_W, D_MODEL), const2(DIFF_V_W, D_MODEL), const2(D_MODEL, D_MODEL),
                  const2(D_MODEL, LANES), const2(D_MODEL, LANES), const2(1, LANES)],
        out_specs=(tok(D_MODEL), _row_tile_spec(TM_MERGE, lambda i: (i, 0)),
                   tok(ROUTE_W),
                   pl.BlockSpec((1, ROUTE_W, TM_MERGE), lambda i: (i, 0, 0)),
                   pl.BlockSpec((1, LANES), lambda i: (0, 0))),
        out_shape=(jax.ShapeDtypeStruct((TOKENS, D_MODEL), F32),
                   jax.ShapeDtypeStruct((TOKENS * ROW_TILE_S, LANES), F32),
                   jax.ShapeDtypeStruct((TOKENS, ROUTE_W), F32),
                   jax.ShapeDtypeStruct((TOKENS // TM_MERGE, ROUTE_W, TM_MERGE), I32),
                   jax.ShapeDtypeStruct((1, LANES), F32)),
        scratch_shapes=[pltpu.VMEM((1, LANES), F32), pltpu.VMEM((TM_MERGE, TM_MERGE), BF16)],
        compiler_params=_cparams(("arbitrary",)),
        name="merge",
    )(x2, o_a, o_b, sa, sb, mod, norm2_g, wa, wb, wo, wr_hi, wr_lo, br)


def _plan_kernel(rk_ref, cnt_ref, dest_ref, blk_ref):
    n_tiles, _, tm = rk_ref.shape
    cnt = cnt_ref[...]
    padded = jnp.floor((cnt + (MOE_BLK - 1)) * (1.0 / MOE_BLK)) * MOE_BLK
    lane = lax.broadcasted_iota(I32, (1, LANES), 1)
    seg_end = padded
    step = 1
    while step < N_EXPERTS:
        seg_end = seg_end + jnp.where(lane >= step, pltpu.roll(seg_end, step, 1), 0.0)
        step *= 2
    seg_start = seg_end - padded
    valid_end = seg_start + cnt

    ei = lax.broadcasted_iota(I32, (LANES, LANES), 0)
    ej = lax.broadcasted_iota(I32, (LANES, LANES), 1)

    def to_col(rowv):
        return jnp.sum(jnp.where(ei == ej, rowv, 0.0), axis=1, keepdims=True)

    start_col, end_col, valid_col = to_col(seg_start), to_col(seg_end), to_col(valid_end)

    nb_pad = blk_ref.shape[1]
    e_sub = lax.broadcasted_iota(I32, (LANES, nb_pad), 0)
    b_start = (lax.broadcasted_iota(I32, (1, nb_pad), 1) * MOE_BLK).astype(F32)
    ends_before = jnp.where((e_sub < N_EXPERTS) & (end_col <= b_start), 1.0, 0.0)
    block_e = jnp.minimum(jnp.sum(ends_before, axis=0, keepdims=True), N_EXPERTS - 1.0)
    block_valid_end = jnp.sum(jnp.where(e_sub.astype(F32) == block_e, valid_col, 0.0),
                              axis=0, keepdims=True)
    n_valid = jnp.clip(block_valid_end - b_start, 0.0, float(MOE_BLK))
    n_used = jnp.max(seg_end, axis=1, keepdims=True) * (1.0 / MOE_BLK)
    nonempty = (to_col(cnt) > 0.0) & (e_sub < N_EXPERTS)
    e_subf = e_sub.astype(F32)
    no_next = float(LANES)
    nxt = jnp.min(jnp.where(nonempty & (e_subf > block_e), e_subf, no_next), axis=0, keepdims=True)
    nxt = jnp.where(nxt == no_next, -1.0, nxt)
    seg_idx = jnp.sum(jnp.where(nonempty & (e_subf < block_e), 1.0, 0.0), axis=0, keepdims=True)
    slot = seg_idx - 2.0 * jnp.floor(seg_idx * 0.5)
    blk_ref[...] = jnp.concatenate(
        [block_e, n_valid, jnp.broadcast_to(n_used, (1, nb_pad)), nxt, slot,
         jnp.zeros((ROUTE_W - 5, nb_pad), F32)], axis=0).astype(I32)

    e_tok = lax.broadcasted_iota(I32, (LANES, tm), 0)

    def tile(t, carry):
        rk = rk_ref[t]
        d = [jnp.sum(jnp.where(e_tok == rk[k:k + 1], start_col, 0.0), axis=0, keepdims=True)
             + rk[k + 2:k + 3].astype(F32) for k in range(TOP_K)]
        dest_ref[t] = jnp.concatenate(d + [jnp.zeros((ROUTE_W - TOP_K, tm), F32)], axis=0).astype(I32)
        return carry

    lax.fori_loop(0, n_tiles, tile, 0)


def _plan_call(rank_rows, cnt):
    n_tiles = TOKENS // TM_MERGE
    nb_pad = -(-N_BLOCKS // LANES) * LANES
    dest, blk = pl.pallas_call(
        _plan_kernel,
        out_shape=(jax.ShapeDtypeStruct((n_tiles, ROUTE_W, TM_MERGE), I32),
                   jax.ShapeDtypeStruct((ROUTE_W, nb_pad), I32)),
        compiler_params=pltpu.CompilerParams(vmem_limit_bytes=VMEM_LIMIT),
        name="plan",
    )(rank_rows, cnt)
    dest1 = dest[:, 0, :].reshape(TOKENS)
    dest2 = dest[:, 1, :].reshape(TOKENS)
    tables = tuple(blk[r, :N_BLOCKS] for r in (0, 1, 3, 4))
    return dest1, dest2, tables, blk[2, :1]


def _sc_params():
    return pltpu.CompilerParams(use_tc_tiling_on_sc=True)


def _sc_mesh():
    return plsc.VectorSubcoreMesh(core_axis_name="core", subcore_axis_name="subcore")


def _sc_worker_base(per_worker):
    wid = lax.axis_index("subcore") * SC_CORES + lax.axis_index("core")
    return wid * per_worker


def _sc_scatter_rows(src, dest1, dest2, n_out):
    n = src.shape[0]
    per_worker = n // SC_WORKERS
    assert per_worker * SC_WORKERS == n and per_worker % SC_CHUNK == 0

    def body(src_hbm, d1_hbm, d2_hbm, out_hbm, idx_v, rows_v):
        base = _sc_worker_base(per_worker)

        @pl.loop(0, per_worker // SC_CHUNK)
        def _(j):
            start = pl.multiple_of(base + j * SC_CHUNK, SC_CHUNK)
            pltpu.sync_copy(src_hbm.at[pl.ds(start, SC_CHUNK)], rows_v)
            for d_hbm in (d1_hbm, d2_hbm):
                pltpu.sync_copy(d_hbm.at[pl.ds(start, SC_CHUNK)], idx_v)
                pltpu.sync_copy(rows_v, out_hbm.at[idx_v])

    return pl.kernel(
        body,
        out_type=jax.ShapeDtypeStruct((n_out, ROW_TILE_S, LANES), F32),
        mesh=_sc_mesh(),
        scratch_types=[pltpu.VMEM((SC_CHUNK,), I32),
                       pltpu.VMEM((SC_CHUNK, ROW_TILE_S, LANES), F32)],
        compiler_params=_sc_params(),
        name="sc_dispatch",
    )(src, dest1, dest2)


def _sc_gather_rows(table, idx):
    n = idx.shape[0]
    per_worker = n // SC_WORKERS
    half = SC_CHUNK // 2
    assert per_worker * SC_WORKERS == n and per_worker % SC_CHUNK == 0

    def body(table_hbm, idx_hbm, out_hbm, idx_v, rows_a, rows_b, gsem_a, gsem_b, wsem_a, wsem_b):
        base = pl.multiple_of(_sc_worker_base(per_worker), SC_CHUNK)
        pltpu.sync_copy(idx_hbm.at[pl.ds(base, per_worker)], idx_v)

        @pl.loop(0, per_worker // SC_CHUNK)
        def _(j):
            off_a = pl.multiple_of(j * SC_CHUNK, SC_CHUNK)
            off_b = pl.multiple_of(j * SC_CHUNK + half, half)
            ga = pltpu.async_copy(table_hbm.at[idx_v.at[pl.ds(off_a, half)]], rows_a, gsem_a)
            gb = pltpu.async_copy(table_hbm.at[idx_v.at[pl.ds(off_b, half)]], rows_b, gsem_b)
            ga.wait()
            wa = pltpu.async_copy(rows_a, out_hbm.at[pl.ds(base + off_a, half)], wsem_a)
            gb.wait()
            wb = pltpu.async_copy(rows_b, out_hbm.at[pl.ds(base + off_b, half)], wsem_b)
            wa.wait()
            wb.wait()

    return pl.kernel(
        body,
        out_type=jax.ShapeDtypeStruct((n, ROW_TILE_S, LANES), F32),
        mesh=_sc_mesh(),
        scratch_types=[pltpu.VMEM((per_worker,), I32),
                       pltpu.VMEM((half, ROW_TILE_S, LANES), F32),
                       pltpu.VMEM((half, ROW_TILE_S, LANES), F32),
                       pltpu.SemaphoreType.DMA, pltpu.SemaphoreType.DMA,
                       pltpu.SemaphoreType.DMA, pltpu.SemaphoreType.DMA],
        compiler_params=_sc_params(),
        name="sc_gather",
    )(table, idx)


def _weight_fetch(w_hbm, stage, sem, e, slot):
    return [pltpu.make_async_copy(w.at[e], st.at[slot], sem.at[slot]) for w, st in zip(w_hbm, stage)]


def _expert_kernel(be_ref, nv_ref, nx_ref, sl_ref, nu_ref, xs_ref, w1_hbm, w3_hbm, w2_hbm, y_ref,
                   w1s, w3s, w2s, w1b, w3b, w2b, sem):
    i = pl.program_id(0)
    e = be_ref[i]
    used = i < nu_ref[0]
    first = (i == 0) | (e != be_ref[jnp.maximum(i - 1, 0)])
    w_hbm = (w1_hbm, w3_hbm, w2_hbm)
    stage = (w1s, w3s, w2s)

    @pl.when(used & first)
    def _():
        slot = sl_ref[i]

        @pl.when(i == 0)
        def _():
            for cp in _weight_fetch(w_hbm, stage, sem, e, slot):
                cp.start()

        for cp in _weight_fetch(w_hbm, stage, sem, e, slot):
            cp.wait()
        w1b[...] = w1s[slot].astype(BF16)
        w3b[...] = w3s[slot].astype(BF16)
        w2b[...] = w2s[slot].astype(BF16)

        @pl.when(nx_ref[i] >= 0)
        def _():
            for cp in _weight_fetch(w_hbm, stage, sem, nx_ref[i], 1 - slot):
                cp.start()

    for part in range(MOE_BLK // MOE_SUB):
        row0 = part * MOE_SUB
        live = used & (nv_ref[i] > row0)

        @pl.when(live)
        def _():
            row = lax.broadcasted_iota(I32, (MOE_SUB, D_MODEL), 0) + row0
            xb = jnp.where(row < nv_ref[i], _load_row_tiles(xs_ref, row0, MOE_SUB), 0.0).astype(BF16)
            a = jnp.dot(xb, w1b[...], preferred_element_type=F32)
            g = jnp.dot(xb, w3b[...], preferred_element_type=F32)
            hmid = (a * _sigmoid(a)) * g
            _store_row_tiles(y_ref, jnp.dot(hmid.astype(BF16), w2b[...], preferred_element_type=F32),
                             row0)

        @pl.when(jnp.logical_not(live))
        def _():
            y_ref[pl.ds(row0 * ROW_TILE_S, MOE_SUB * ROW_TILE_S), :] = jnp.zeros(
                (MOE_SUB * ROW_TILE_S, LANES), F32)


def _expert_call(tables, n_used, xs, w1, w3, w2):
    n_tab = len(tables)
    grid_spec = pltpu.PrefetchScalarGridSpec(
        num_scalar_prefetch=n_tab + 1,
        grid=(N_BLOCKS,),
        in_specs=[_row_tile_spec(MOE_BLK, lambda i, *pf: (jnp.minimum(i, pf[n_tab][0] - 1), 0)),
                  pl.BlockSpec(memory_space=pl.ANY),
                  pl.BlockSpec(memory_space=pl.ANY),
                  pl.BlockSpec(memory_space=pl.ANY)],
        out_specs=_row_tile_spec(MOE_BLK, lambda i, *pf: (i, 0)),
        scratch_shapes=[pltpu.VMEM((2, D_MODEL, D_EXPERT), F32),
                        pltpu.VMEM((2, D_MODEL, D_EXPERT), F32),
                        pltpu.VMEM((2, D_EXPERT, D_MODEL), F32),
                        pltpu.VMEM((D_MODEL, D_EXPERT), BF16),
                        pltpu.VMEM((D_MODEL, D_EXPERT), BF16),
                        pltpu.VMEM((D_EXPERT, D_MODEL), BF16),
                        pltpu.SemaphoreType.DMA((2,))],
    )
    return pl.pallas_call(
        _expert_kernel,
        grid_spec=grid_spec,
        out_shape=jax.ShapeDtypeStruct((N_BLOCKS * MOE_BLK * ROW_TILE_S, LANES), F32),
        compiler_params=_cparams(("arbitrary",)),
        name="experts",
    )(*tables, n_used, xs, w1, w3, w2)


def _combine_rows_kernel(x1_ref, rf_ref, mod_ref, ya_ref, yb_ref, *rest):
    o_ref = rest[-1]
    rf = rf_ref[...]
    gate2 = mod_ref[0, 5:6, :]
    moe = rf[:, 0:1] * _load_row_tiles(ya_ref) + rf[:, 1:2] * _load_row_tiles(yb_ref)
    o_ref[...] = x1_ref[...] + gate2 * moe


def _combine_rows_call(x1, rf, mod, yg, part, prev=None):
    nb = SEQ // TM_ROW
    n_half = TOKENS // TM_ROW // 2
    t0 = part * n_half
    in_specs = [pl.BlockSpec((TM_ROW, D_MODEL), lambda i: (t0 + i, 0)),
                pl.BlockSpec((TM_ROW, ROUTE_W), lambda i: (t0 + i, 0)),
                pl.BlockSpec((1, N_MOD, D_MODEL), lambda i: ((t0 + i) // nb, 0, 0)),
                _row_tile_spec(TM_ROW, lambda i: (i, 0)),
                _row_tile_spec(TM_ROW, lambda i: (i + n_half, 0))]
    args = [x1, rf, mod, yg, yg]
    aliases = {}
    if prev is not None:
        in_specs.append(pl.BlockSpec(memory_space=pl.ANY))
        args.append(prev)
        aliases = {len(args) - 1: 0}
    return pl.pallas_call(
        _combine_rows_kernel,
        grid=(n_half,),
        in_specs=in_specs,
        out_specs=pl.BlockSpec((TM_ROW, D_MODEL), lambda i: (t0 + i, 0)),
        out_shape=jax.ShapeDtypeStruct((TOKENS, D_MODEL), F32),
        input_output_aliases=aliases,
        compiler_params=_cparams(("parallel",)),
        name="combine",
    )(*args)


def kernel(x, c, positions, ada_w, ada_b, norm1_g, norm2_g, w_in, gla_alpha_up, gla_alpha_b,
           gla_out_g, diff_q_g, diff_k_g, diff_lq1, diff_lk1, diff_lq2, diff_lk2, diff_out_g,
           w_branch_a, w_branch_b, w_out, router_group_w, router_group_b, router_expert_w,
           router_expert_b, expert_w1, expert_w3, expert_w2):
    assert x.shape == (BATCH, SEQ, D_MODEL) and ada_w.shape[0] == 1
    x2 = x.reshape(TOKENS, D_MODEL)

    w_in_p = _wprep_call(w_in[0])
    au_pad = jnp.zeros((LR_PAD, GLA_QK_W), F32).at[:GLA_GATE_RANK].set(gla_alpha_up[0])
    au_hi = au_pad.astype(BF16)
    au_lo = (au_pad - au_hi.astype(F32)).astype(BF16)
    gid = jnp.arange(DIFF_QK_W) // DIFF_DH
    bd = jnp.where(gid[:, None] == gid[None, :], 1.0 / DIFF_DH, 0.0).astype(BF16)
    qg_row = jnp.tile(diff_q_g[0], DIFF_QK_W // DIFF_DH).reshape(1, DIFF_QK_W)
    kg_row = jnp.tile(diff_k_g[0], DIFF_QK_W // DIFF_DH).reshape(1, DIFF_QK_W)
    inv_freq = ROPE_THETA ** (-jnp.arange(ROPE_HALF, dtype=F32) / ROPE_HALF)
    invf64 = jnp.concatenate([inv_freq, inv_freq, jnp.zeros((DIFF_DH - ROPE_DIM,), F32)])
    invf_row = jnp.tile(invf64, LANES // DIFF_DH).reshape(1, LANES)
    pos_col = positions.reshape(TOKENS, 1)
    lam_p = jnp.concatenate([diff_lq1, diff_lk1, diff_lq2, diff_lk2], axis=0)
    wr = (jnp.zeros((D_MODEL, LANES), F32)
          .at[:, :N_GROUPS].set(router_group_w[0])
          .at[:, N_GROUPS:N_GROUPS + N_EXPERTS].set(router_expert_w[0]))
    wr_hi = wr.astype(BF16)
    wr_lo = (wr - wr_hi.astype(F32)).astype(BF16)
    br = (jnp.zeros((1, LANES), F32)
          .at[0, :N_GROUPS].set(router_group_b[0])
          .at[0, N_GROUPS:N_GROUPS + N_EXPERTS].set(router_expert_b[0]))

    mod = _mod_call(c, ada_w[0], ada_b[0])
    gqk, gv, gr, lr, qT, kk, vT, sa, sb = _in_call(
        x2, mod, norm1_g, w_in_p, bd, qg_row, kg_row, pos_col, invf_row)
    o_a = _gla_call(gqk, gv, gr, lr, au_hi, au_lo, gla_alpha_b, gla_out_g)
    o_b = _attn_call(qT, kk, vT, lam_p, diff_out_g.reshape(DIFF_DV, 1))
    x1, h2, rf, rank, cnt = _merge_call(
        x2, o_a, o_b, sa, sb, mod, norm2_g, w_branch_a[0].astype(BF16),
        w_branch_b[0].astype(BF16), w_out[0].astype(BF16), wr_hi, wr_lo, br)
    dest1, dest2, tables, n_used = _plan_call(rank, cnt)

    xs = _sc_scatter_rows(_as_row_tiles(h2), dest1, dest2, N_BLOCKS * MOE_BLK)
    y = _expert_call(tables, n_used, _as_2d(xs), expert_w1[0], expert_w3[0], expert_w2[0])
    y3 = _as_row_tiles(y)
    half = TOKENS // 2
    out = None
    for part in range(2):
        tok = slice(part * half, (part + 1) * half)
        yg = _sc_gather_rows(y3, jnp.concatenate([dest1[tok], dest2[tok]]))
        out = _combine_rows_call(x1, rf, mod, _as_2d(yg), part, out)
    return out.reshape(BATCH, SEQ, D_MODEL)
```

```python
import math

import jax
import jax.numpy as jnp
from jax import lax
from jax.experimental import pallas as pl
from jax.experimental.pallas import tpu as pltpu
from jax.experimental.pallas import tpu_sc as plsc

F32 = jnp.float32
BF16 = jnp.bfloat16
I32 = jnp.int32

D_MODEL = 1024
BATCH = 4
SEQ = 4096
TOKENS = BATCH * SEQ
N_MOD = 6
NORM_EPS = 1e-6

GLA_HEADS = 4
GLA_DK = 64
GLA_DV = 128
GLA_GATE_RANK = 16
GLA_GATE_TAU = 16.0
GLA_CHUNK = 64
GLA_QK_W = GLA_HEADS * GLA_DK
GLA_V_W = GLA_HEADS * GLA_DV

DIFF_HEADS = 4
DIFF_DH = 64
DIFF_DV = 2 * DIFF_DH
DIFF_QK_W = DIFF_HEADS * 2 * DIFF_DH
DIFF_V_W = DIFF_HEADS * DIFF_DV
ROPE_THETA = 500000.0
ROPE_DIM = DIFF_DH // 4
ROPE_HALF = ROPE_DIM // 2
NEG_INF = -1e30
LAMBDA_INIT = 0.8 - 0.6 * 1.0

N_GROUPS = 4
EXPERTS_PER_GROUP = 8
N_EXPERTS = N_GROUPS * EXPERTS_PER_GROUP
TOP_K = 2
D_EXPERT = 512

LANES = 128
SUBLANES = 8
ROW_TILE_S = D_MODEL // LANES
SC_CORES = 2
SC_SUBCORES = 16
SC_WORKERS = SC_CORES * SC_SUBCORES
SC_CHUNK = 64
LR_PAD = LANES
D_IN_PAD = 2 * GLA_QK_W + 2 * GLA_V_W + 2 * DIFF_QK_W + DIFF_V_W + 2 * D_MODEL + LR_PAD

TM_IN = 512
TQ = 512
V_ROWS = DIFF_DV + 16
ATT_SUB = 32
Q_SCALE = DIFF_DH ** -0.5 * math.log2(math.e)
N_KV = SEQ // TQ
TT_GLA = 512
TM_MERGE = 512
TM_ROW = 256
MOE_BLK = 512
MOE_SUB = 256
N_BLOCKS = (TOKENS * TOP_K + N_EXPERTS * (MOE_BLK - 1) + MOE_BLK - 1) // MOE_BLK
ROUTE_W = 8

VMEM_LIMIT = 56 * 1024 * 1024


def _cparams(sem):
    return pltpu.CompilerParams(dimension_semantics=sem, vmem_limit_bytes=VMEM_LIMIT)


def _sigmoid(x):
    return 1.0 / (1.0 + jnp.exp(-x))


def _row_tile_spec(rows, index_map):
    return pl.BlockSpec((rows * ROW_TILE_S, LANES), index_map)


def _as_row_tiles(a2d):
    return a2d.reshape(a2d.shape[0] // ROW_TILE_S, ROW_TILE_S, LANES)


def _as_2d(a3d):
    return a3d.reshape(a3d.shape[0] * ROW_TILE_S, LANES)


def _store_row_tiles(ref, val, row0=0):
    for s in range(ROW_TILE_S):
        ref[pl.ds(row0 * ROW_TILE_S + s, val.shape[0], stride=ROW_TILE_S), :] = (
            val[:, s * LANES:(s + 1) * LANES])


def _load_row_tiles(ref, row0=0, rows=None):
    rows = ref.shape[0] // ROW_TILE_S if rows is None else rows
    return jnp.concatenate(
        [ref[pl.ds(row0 * ROW_TILE_S + s, rows, stride=ROW_TILE_S), :] for s in range(ROW_TILE_S)],
        axis=1)


def _mod_kernel(ct_ref, w_ref, b_ref, o_ref):
    ct = ct_ref[...]
    ca = ct * _sigmoid(ct)
    w = w_ref[...]
    rows = []
    for b in range(BATCH):
        rows.append(jnp.sum(w * ca[:, b:b + 1], axis=0, keepdims=True) + b_ref[...])
    rows.append(jnp.zeros((8 - BATCH, w.shape[1]), F32))
    o_ref[...] = jnp.concatenate(rows, axis=0)


def _mod_call(c, ada_w, ada_b):
    tn = D_MODEL
    ct = jnp.zeros((D_MODEL, 8), F32).at[:, :BATCH].set(c.T)
    out = pl.pallas_call(
        _mod_kernel,
        grid=(N_MOD,),
        in_specs=[
            pl.BlockSpec((D_MODEL, 8), lambda j: (0, 0)),
            pl.BlockSpec((D_MODEL, tn), lambda j: (0, j)),
            pl.BlockSpec((1, tn), lambda j: (0, j)),
        ],
        out_specs=pl.BlockSpec((8, tn), lambda j: (0, j)),
        out_shape=jax.ShapeDtypeStruct((8, N_MOD * D_MODEL), F32),
        compiler_params=_cparams(("arbitrary",)),
        name="mod",
    )(ct, ada_w, ada_b.reshape(1, N_MOD * D_MODEL))
    return out[:BATCH].reshape(BATCH, N_MOD, D_MODEL)


def _wprep_kernel(w_ref, o_ref):
    w = w_ref[...]
    lr0 = 2 * GLA_QK_W + 2 * GLA_V_W
    rest = D_IN_PAD - LR_PAD - lr0
    o_ref[:, :lr0] = w[:, :lr0].astype(BF16)
    o_ref[:, lr0:lr0 + rest] = w[:, lr0 + GLA_GATE_RANK:].astype(BF16)
    lr = jnp.concatenate([w[:, lr0:lr0 + GLA_GATE_RANK],
                          jnp.zeros((w.shape[0], LR_PAD - GLA_GATE_RANK), F32)], axis=1)
    o_ref[:, lr0 + rest:] = lr.astype(BF16)


def _wprep_call(w_in0):
    d_in = w_in0.shape[1]
    rows = 128
    return pl.pallas_call(
        _wprep_kernel,
        grid=(D_MODEL // rows,),
        in_specs=[pl.BlockSpec((rows, d_in), lambda i: (i, 0))],
        out_specs=pl.BlockSpec((rows, D_IN_PAD), lambda i: (i, 0)),
        out_shape=jax.ShapeDtypeStruct((D_MODEL, D_IN_PAD), BF16),
        compiler_params=_cparams(("parallel",)),
        name="w_prep",
    )(w_in0)


def _in_kernel(x_ref, mod_ref, g1_ref, w_ref, bd_ref, qg_ref, kg_ref, pos_ref, invf_ref,
               gqk_ref, gv_ref, gr_ref, lr_ref, qT_ref, k_ref, vT_ref, sa_ref, sb_ref,
               cos_scr, sin_scr):
    ang = pos_ref[...].astype(F32) * invf_ref[...]
    cos_scr[...] = jnp.cos(ang)
    sin_scr[...] = jnp.sin(ang)

    x = x_ref[...]
    shift1 = mod_ref[0, 0:1, :]
    scale1 = mod_ref[0, 1:2, :]
    ms = jnp.mean(x * x, axis=-1, keepdims=True)
    h = (x * lax.rsqrt(ms + NORM_EPS) * g1_ref[...]) * (1.0 + scale1) + shift1
    hb = h.astype(BF16)

    def proj(c0, c1):
        return jnp.dot(hb, w_ref[:, c0:c1], preferred_element_type=F32)

    tm = x.shape[0]
    cos4 = jnp.concatenate([cos_scr[...]] * DIFF_HEADS, axis=1)
    sin4 = jnp.concatenate([sin_scr[...]] * DIFF_HEADS, axis=1)
    lane = lax.broadcasted_iota(I32, (tm, DIFF_QK_W), 1)
    first_half = (lane % DIFF_DH) < ROPE_HALF
    bd = bd_ref[...]

    def norm_rope(t, gain_row):
        t2 = t * t
        hi = t2.astype(BF16)
        lo = (t2 - hi.astype(F32)).astype(BF16)
        gms = (jnp.dot(hi, bd, preferred_element_type=F32)
               + jnp.dot(lo, bd, preferred_element_type=F32))
        t = t * lax.rsqrt(gms + NORM_EPS) * gain_row
        nxt = pltpu.roll(t, DIFF_QK_W - ROPE_HALF, 1)
        prv = pltpu.roll(t, ROPE_HALF, 1)
        return t * cos4 + jnp.where(first_half, -nxt, prv) * sin4

    dq_raw = proj(1536, 2048)
    dk_raw = proj(2048, 2560)
    gqk_ref[...] = proj(0, 512).astype(BF16)
    dq = norm_rope(dq_raw, qg_ref[...]) * Q_SCALE
    gv_ref[...] = proj(512, 1024).astype(BF16)
    dk = norm_rope(dk_raw, kg_ref[...])
    dv = proj(2560, 3072)
    for hd in range(DIFF_HEADS):
        sl = slice(hd * LANES, (hd + 1) * LANES)
        qT_ref[0, hd, 0] = dq[:, sl].T.astype(BF16)
        k_ref[0, hd, 0] = dk[:, sl].astype(BF16)
    gr_ref[...] = proj(1024, 1536).astype(BF16)
    for hd in range(DIFF_HEADS):
        sl = slice(hd * LANES, (hd + 1) * LANES)
        vT_ref[0, hd, 0] = jnp.concatenate(
            [dv[:, sl].T, jnp.ones((V_ROWS - DIFF_DV, tm), F32)], axis=0).astype(BF16)
    sa_ref[...] = _sigmoid(proj(3072, 4096)).astype(BF16)
    sb_ref[...] = _sigmoid(proj(4096, 5120)).astype(BF16)
    lr_ref[...] = proj(5120, 5248)


def _in_call(x2, mod, norm1_g, w_in_p, bd, qg_row, kg_row, pos_col, invf_row):
    nb = SEQ // TM_IN
    tok_spec = lambda w: pl.BlockSpec((TM_IN, w), lambda i: (i, 0))
    const2 = lambda r, c: pl.BlockSpec((r, c), lambda i: (0, 0))
    out_shapes = (
        jax.ShapeDtypeStruct((TOKENS, 2 * GLA_QK_W), BF16),
        jax.ShapeDtypeStruct((TOKENS, GLA_V_W), BF16),
        jax.ShapeDtypeStruct((TOKENS, GLA_V_W), BF16),
        jax.ShapeDtypeStruct((TOKENS, LR_PAD), F32),
        jax.ShapeDtypeStruct((BATCH, DIFF_HEADS, N_KV, LANES, TQ), BF16),
        jax.ShapeDtypeStruct((BATCH, DIFF_HEADS, N_KV, TQ, LANES), BF16),
        jax.ShapeDtypeStruct((BATCH, DIFF_HEADS, N_KV, V_ROWS, TQ), BF16),
        jax.ShapeDtypeStruct((TOKENS, D_MODEL), BF16),
        jax.ShapeDtypeStruct((TOKENS, D_MODEL), BF16),
    )
    out_specs = (
        tok_spec(2 * GLA_QK_W), tok_spec(GLA_V_W), tok_spec(GLA_V_W), tok_spec(LR_PAD),
        pl.BlockSpec((1, DIFF_HEADS, 1, LANES, TM_IN), lambda i: (i // nb, 0, i % nb, 0, 0)),
        pl.BlockSpec((1, DIFF_HEADS, 1, TM_IN, LANES), lambda i: (i // nb, 0, i % nb, 0, 0)),
        pl.BlockSpec((1, DIFF_HEADS, 1, V_ROWS, TM_IN), lambda i: (i // nb, 0, i % nb, 0, 0)),
        tok_spec(D_MODEL), tok_spec(D_MODEL),
    )
    return pl.pallas_call(
        _in_kernel,
        grid=(TOKENS // TM_IN,),
        in_specs=[
            tok_spec(D_MODEL),
            pl.BlockSpec((1, N_MOD, D_MODEL), lambda i: (i // nb, 0, 0)),
            const2(1, D_MODEL),
            pl.BlockSpec((D_MODEL, D_IN_PAD), lambda i: (0, 0), pipeline_mode=pl.Buffered(1)),
            const2(DIFF_QK_W, DIFF_QK_W),
            const2(1, DIFF_QK_W), const2(1, DIFF_QK_W),
            tok_spec(1),
            const2(1, LANES),
        ],
        out_specs=out_specs,
        out_shape=out_shapes,
        scratch_shapes=[pltpu.VMEM((TM_IN, LANES), F32), pltpu.VMEM((TM_IN, LANES), F32)],
        compiler_params=_cparams(("parallel",)),
        name="in_proj",
    )(x2, mod, norm1_g, w_in_p, bd, qg_row, kg_row, pos_col, invf_row)


def _gla_kernel(qk_ref, v_ref, r_ref, lr_ref, auh_ref, aul_ref, ab_ref, og_ref, o_ref,
                state_ref, oacc_ref, snew_ref):
    tt = qk_ref.shape[0]
    n_chunks = tt // GLA_CHUNK

    @pl.when(pl.program_id(1) == 0)
    def _():
        state_ref[...] = jnp.zeros_like(state_ref)

    lr = lr_ref[...]
    lr_hi = lr.astype(BF16)
    lr_lo = (lr - lr_hi.astype(F32)).astype(BF16)
    z = (jnp.dot(lr_hi, auh_ref[...], preferred_element_type=F32)
         + jnp.dot(lr_lo, auh_ref[...], preferred_element_type=F32)
         + jnp.dot(lr_hi, aul_ref[...], preferred_element_type=F32)) + ab_ref[...]
    g = (jnp.minimum(z, 0.0) - jnp.log(1.0 + jnp.exp(-jnp.abs(z)))) * (1.0 / GLA_GATE_TAU)

    row = lax.broadcasted_iota(I32, (tt, GLA_QK_W), 0) % GLA_CHUNK
    b = g
    step = 1
    while step < GLA_CHUNK:
        b = b + jnp.where(row >= step, pltpu.roll(b, step, 0), 0.0)
        step *= 2

    b_last_rows = [b[c * GLA_CHUNK + GLA_CHUNK - 1:(c + 1) * GLA_CHUNK, :] for c in range(n_chunks)]
    b_last = jnp.concatenate(
        [jnp.broadcast_to(bl, (GLA_CHUNK, GLA_QK_W)) for bl in b_last_rows], axis=0)

    qk = qk_ref[...].astype(F32)
    q = qk[:, :GLA_QK_W] * (GLA_DK ** -0.5)
    k = qk[:, GLA_QK_W:]
    q_in = (q * jnp.exp(b)).astype(BF16)
    k_in = (k * jnp.exp(-b)).astype(BF16)
    k_dec = (k * jnp.exp(b_last - b)).astype(BF16)

    ci = lax.broadcasted_iota(I32, (GLA_CHUNK, GLA_CHUNK), 0)
    cj = lax.broadcasted_iota(I32, (GLA_CHUNK, GLA_CHUNK), 1)
    causal = ci >= cj

    pairs = [(c, hd) for c in range(n_chunks) for hd in range(GLA_HEADS)]

    def rows(c):
        return slice(c * GLA_CHUNK, (c + 1) * GLA_CHUNK)

    def kcols(hd):
        return slice(hd * GLA_DK, (hd + 1) * GLA_DK)

    def vcols(hd):
        return slice(hd * GLA_DV, (hd + 1) * GLA_DV)

    att = {}
    for c, hd in pairs:
        a = lax.dot_general(q_in[rows(c), kcols(hd)], k_in[rows(c), kcols(hd)],
                            (((1,), (1,)), ((), ())), preferred_element_type=F32)
        att[c, hd] = jnp.where(causal, a, 0.0).astype(BF16)
    for c, hd in pairs:
        oacc_ref[rows(c), vcols(hd)] = jnp.dot(att[c, hd], v_ref[rows(c), vcols(hd)],
                                               preferred_element_type=F32)
    for c, hd in pairs:
        snew_ref[c * GLA_HEADS + hd] = lax.dot_general(
            k_dec[rows(c), kcols(hd)], v_ref[rows(c), vcols(hd)],
            (((0,), (0,)), ((), ())), preferred_element_type=F32)

    decay_rows = jnp.exp(jnp.concatenate(
        b_last_rows + [jnp.zeros((LANES - n_chunks, GLA_QK_W), F32)], axis=0))
    decay_cols = decay_rows.T
    states = [state_ref[hd] for hd in range(GLA_HEADS)]
    for c in range(n_chunks):
        for hd in range(GLA_HEADS):
            s_prev = states[hd]
            oacc_ref[rows(c), vcols(hd)] += jnp.dot(
                q_in[rows(c), kcols(hd)], s_prev.astype(BF16), preferred_element_type=F32)
            dcol = decay_cols[kcols(hd), c:c + 1]
            states[hd] = s_prev * dcol + snew_ref[c * GLA_HEADS + hd]
    for hd in range(GLA_HEADS):
        state_ref[hd] = states[hd]

    for hd in range(GLA_HEADS):
        vs = slice(hd * GLA_DV, (hd + 1) * GLA_DV)
        oh = oacc_ref[:, vs]
        ms = jnp.mean(oh * oh, axis=-1, keepdims=True)
        y = oh * lax.rsqrt(ms + NORM_EPS) * og_ref[...]
        r = r_ref[:, vs].astype(F32)
        o_ref[:, vs] = (y * (r * _sigmoid(r))).astype(BF16)


def _gla_call(gqk, gv, gr, lr, au_hi, au_lo, ab_row, og_row):
    nt = SEQ // TT_GLA
    tok = lambda w: pl.BlockSpec((TT_GLA, w), lambda b, t: (b * nt + t, 0))
    const2 = lambda r, c: pl.BlockSpec((r, c), lambda b, t: (0, 0))
    return pl.pallas_call(
        _gla_kernel,
        grid=(BATCH, nt),
        in_specs=[tok(2 * GLA_QK_W), tok(GLA_V_W), tok(GLA_V_W), tok(LR_PAD),
                  const2(LR_PAD, GLA_QK_W), const2(LR_PAD, GLA_QK_W), const2(1, GLA_QK_W),
                  const2(1, GLA_DV)],
        out_specs=tok(GLA_V_W),
        out_shape=jax.ShapeDtypeStruct((TOKENS, GLA_V_W), BF16),
        scratch_shapes=[pltpu.VMEM((GLA_HEADS, GLA_DK, GLA_DV), F32),
                        pltpu.VMEM((TT_GLA, GLA_V_W), F32),
                        pltpu.VMEM((TT_GLA // GLA_CHUNK * GLA_HEADS, GLA_DK, GLA_DV), F32)],
        compiler_params=_cparams(("parallel", "arbitrary")),
        name="gla",
    )(gqk, gv, gr, lr, au_hi, au_lo, ab_row, og_row)


def _attn_kernel(qT_ref, k_ref, vT_ref, lam_ref, og_ref, o_ref,
                 q_scr, s_scr, p_scr, acc_scr, m_scr):
    _attn_load_q(qT_ref, q_scr, 0, 0)
    s_scr[0] = jnp.dot(k_ref[0, 0, 0], q_scr[0, 0], preferred_element_type=F32)

    def q_block(i, carry):
        _attn_q_block(i, qT_ref, k_ref, vT_ref, lam_ref, og_ref, o_ref,
                      q_scr, s_scr, p_scr, acc_scr, m_scr)
        return carry

    lax.fori_loop(0, N_KV, q_block, 0)


def _attn_load_q(qT_ref, q_scr, i, slot):
    qT = qT_ref[0, 0, i]
    rowq = lax.broadcasted_iota(I32, qT.shape, 0)
    zero = jnp.zeros_like(qT)
    q_scr[slot, 0] = jnp.where(rowq < DIFF_DH, qT, zero)
    q_scr[slot, 1] = jnp.where(rowq >= DIFF_DH, qT, zero)


def _attn_q_block(i, qT_ref, k_ref, vT_ref, lam_ref, og_ref, o_ref,
                  q_scr, s_scr, p_scr, acc_scr, m_scr):
    slot = i % 2
    m_scr[...] = jnp.full(m_scr.shape, NEG_INF, F32)
    acc_scr[...] = jnp.zeros(acc_scr.shape, F32)
    n_sub = TQ // ATT_SUB

    def fold8(t, op):
        return op(t.reshape(t.shape[0] // SUBLANES, SUBLANES, TQ), axis=0)

    def scores(c, j):
        s_scr[c] = jnp.dot(k_ref[0, 0, j], q_scr[slot, c], preferred_element_type=F32)

    def load_s(c, r, masked):
        s = s_scr[c, r * ATT_SUB:(r + 1) * ATT_SUB, :]
        if masked:
            key_i = lax.broadcasted_iota(I32, (ATT_SUB, TQ), 0) + r * ATT_SUB
            qry_i = lax.broadcasted_iota(I32, (ATT_SUB, TQ), 1)
            s = jnp.where(key_i <= qry_i, s, NEG_INF)
        return s

    def softmax_pv(c, j, masked):
        m8 = fold8(load_s(c, 0, masked), jnp.max)
        for r in range(1, n_sub):
            m8 = jnp.maximum(m8, fold8(load_s(c, r, masked), jnp.max))
        m_old = m_scr[c]
        m_new = jnp.maximum(m_old, jnp.max(m8, axis=0, keepdims=True))
        alpha = jnp.exp2(m_old - m_new)
        for r in range(n_sub):
            p = jnp.exp2(load_s(c, r, masked) - m_new)
            p_scr[c, r * ATT_SUB:(r + 1) * ATT_SUB, :] = p.astype(BF16)
        m_scr[c] = m_new
        acc_scr[c] = acc_scr[c] * alpha + jnp.dot(vT_ref[0, 0, j], p_scr[c],
                                                  preferred_element_type=F32)

    def body(j, carry):
        scores(1, j)
        softmax_pv(0, j, False)
        scores(0, j + 1)
        softmax_pv(1, j, False)
        return carry

    lax.fori_loop(0, i, body, 0)
    scores(1, i)
    softmax_pv(0, i, True)
    softmax_pv(1, i, True)

    nxt = jnp.minimum(i + 1, N_KV - 1)
    _attn_load_q(qT_ref, q_scr, nxt, 1 - slot)
    s_scr[0] = jnp.dot(k_ref[0, 0, 0], q_scr[1 - slot, 0], preferred_element_type=F32)

    l1 = acc_scr[0, DIFF_DV:DIFF_DV + 1, :]
    l2 = acc_scr[1, DIFF_DV:DIFF_DV + 1, :]

    lam_p = lam_ref[...]
    lam = (jnp.exp(jnp.sum(lam_p[0:1] * lam_p[1:2], axis=1, keepdims=True))
           - jnp.exp(jnp.sum(lam_p[2:3] * lam_p[3:4], axis=1, keepdims=True)) + LAMBDA_INIT)
    oT = acc_scr[0, :DIFF_DV, :] / l1 - lam * (acc_scr[1, :DIFF_DV, :] / l2)
    ms = jnp.mean(oT * oT, axis=0, keepdims=True)
    y = oT * lax.rsqrt(ms + NORM_EPS) * og_ref[...] * (1.0 - LAMBDA_INIT)
    o_ref[pl.ds(pl.multiple_of(i * TQ, TQ), TQ), :] = y.T.astype(BF16)


def _attn_call(qT, kk, vT, lam_p, og_col):
    return pl.pallas_call(
        _attn_kernel,
        grid=(BATCH, DIFF_HEADS),
        in_specs=[
            pl.BlockSpec((1, 1, N_KV, LANES, TQ), lambda b, h: (b, h, 0, 0, 0)),
            pl.BlockSpec((1, 1, N_KV, TQ, LANES), lambda b, h: (b, h, 0, 0, 0)),
            pl.BlockSpec((1, 1, N_KV, V_ROWS, TQ), lambda b, h: (b, h, 0, 0, 0)),
            pl.BlockSpec((4, DIFF_DH), lambda b, h: (0, 0)),
            pl.BlockSpec((DIFF_DV, 1), lambda b, h: (0, 0)),
        ],
        out_specs=pl.BlockSpec((SEQ, DIFF_DV), lambda b, h: (b, h)),
        out_shape=jax.ShapeDtypeStruct((TOKENS, DIFF_V_W), BF16),
        scratch_shapes=[pltpu.VMEM((2, 2, LANES, TQ), BF16),
                        pltpu.VMEM((2, TQ, TQ), F32),
                        pltpu.VMEM((2, TQ, TQ), BF16),
                        pltpu.VMEM((2, V_ROWS, TQ), F32),
                        pltpu.VMEM((2, 1, TQ), F32)],
        compiler_params=_cparams(("parallel", "parallel")),
        name="attn",
    )(qT, kk, vT, lam_p, og_col)


def _merge_kernel(x_ref, oa_ref, ob_ref, sa_ref, sb_ref, mod_ref, g2_ref, wa_ref, wb_ref, wo_ref,
                  wrh_ref, wrl_ref, br_ref, x1_ref, h2_ref, rf_ref, rank_ref, cnt_ref,
                  carry_ref, tri_ref, lg_ref):
    tm = x_ref.shape[0]
    step = pl.program_id(0)

    @pl.when(step == 0)
    def _():
        carry_ref[...] = jnp.zeros_like(carry_ref)
        lg_ref[...] = jnp.zeros_like(lg_ref)
        ti = lax.broadcasted_iota(I32, (tm, tm), 0)
        tj = lax.broadcasted_iota(I32, (tm, tm), 1)
        tri_ref[...] = jnp.where(ti > tj, 1.0, 0.0).astype(BF16)

    ma = jnp.dot(oa_ref[...], wa_ref[...], preferred_element_type=F32)
    mb = jnp.dot(ob_ref[...], wb_ref[...], preferred_element_type=F32)
    merged = sa_ref[...].astype(F32) * ma + sb_ref[...].astype(F32) * mb
    y = jnp.dot(merged.astype(BF16), wo_ref[...], preferred_element_type=F32)
    logits = lg_ref[...]
    lane = lax.broadcasted_iota(I32, (tm, LANES), 1).astype(F32)
    ninf = -jnp.inf
    big = float(LANES)

    def first_argmax(v):
        vmax = jnp.max(v, axis=1, keepdims=True)
        idx = jnp.min(jnp.where(v == vmax, lane, big), axis=1, keepdims=True)
        return vmax, idx

    gl = jnp.where(lane < N_GROUPS, logits, ninf)
    gmax, gidx = first_argmax(gl)
    p_top = 1.0 / jnp.sum(jnp.exp(gl - gmax), axis=1, keepdims=True)
    lo = N_GROUPS + EXPERTS_PER_GROUP * gidx
    el = jnp.where((lane >= lo) & (lane < lo + EXPERTS_PER_GROUP), logits, ninf)
    e1max, e1 = first_argmax(el)
    e2max, e2 = first_argmax(jnp.where(lane == e1, ninf, el))
    t = jnp.exp(e2max - e1max)
    w1 = 1.0 / (1.0 + t)
    w2 = t / (1.0 + t)
    col = lax.broadcasted_iota(I32, (tm, ROUTE_W), 1)
    rf_ref[...] = jnp.where(col == 0, p_top * w1, jnp.where(col == 1, p_top * w2, 0.0))

    x1id = e1 - N_GROUPS
    x2id = e2 - N_GROUPS
    hit1 = lane == x1id
    hit2 = lane == x2id
    onehot = jnp.where(hit1 | hit2, 1.0, 0.0)
    before = jnp.dot(tri_ref[...], onehot.astype(BF16), preferred_element_type=F32) + carry_ref[...]
    r1 = jnp.sum(jnp.where(hit1, before, 0.0), axis=1, keepdims=True)
    r2 = jnp.sum(jnp.where(hit2, before, 0.0), axis=1, keepdims=True)
    real_tile = jnp.where(step > 0, 1.0, 0.0)
    carry_ref[...] = carry_ref[...] + real_tile * jnp.sum(onehot, axis=0, keepdims=True)
    cnt_ref[...] = carry_ref[...]
    cols = jnp.where(lane == 0.0, x1id, jnp.where(lane == 1.0, x2id,
                                                  jnp.where(lane == 2.0, r1,
                                                            jnp.where(lane == 3.0, r2, 0.0))))
    rank_ref[0] = cols.T[0:ROUTE_W, :].astype(I32)

    gate1 = mod_ref[0, 2:3, :]
    shift2 = mod_ref[0, 3:4, :]
    scale2 = mod_ref[0, 4:5, :]
    x1 = x_ref[...] + gate1 * y
    x1_ref[...] = x1
    ms = jnp.mean(x1 * x1, axis=-1, keepdims=True)
    h2 = (x1 * lax.rsqrt(ms + NORM_EPS) * g2_ref[...]) * (1.0 + scale2) + shift2
    _store_row_tiles(h2_ref, h2)
    h2_hi = h2.astype(BF16)
    h2_lo = (h2 - h2_hi.astype(F32)).astype(BF16)
    next_logits = (jnp.dot(h2_hi, wrh_ref[...], preferred_element_type=F32)
                   + jnp.dot(h2_lo, wrh_ref[...], preferred_element_type=F32)
                   + jnp.dot(h2_hi, wrl_ref[...], preferred_element_type=F32)) + br_ref[...]

    lg_ref[...] = next_logits


def _merge_call(x2, o_a, o_b, sa, sb, mod, norm2_g, wa, wb, wo, wr_hi, wr_lo, br):
    nb = SEQ // TM_MERGE
    n_tiles = TOKENS // TM_MERGE
    cur = lambda i: jnp.minimum(i, n_tiles - 1)
    prev = lambda i: jnp.maximum(i - 1, 0)
    tok = lambda w: pl.BlockSpec((TM_MERGE, w), lambda i: (cur(i), 0))
    const2 = lambda r, c: pl.BlockSpec((r, c), lambda i: (0, 0))
    return pl.pallas_call(
        _merge_kernel,
        grid=(n_tiles + 1,),
        in_specs=[tok(D_MODEL), tok(GLA_V_W), tok(DIFF_V_W), tok(D_MODEL), tok(D_MODEL),
                  pl.BlockSpec((1, N_MOD, D_MODEL), lambda i: (cur(i) // nb, 0, 0)),
                  const2(1, D_MODEL),
                  const2(GLA_V_W, D_MODEL), const2(DIFF_V_W, D_MODEL), const2(D_MODEL, D_MODEL),
                  const2(D_MODEL, LANES), const2(D_MODEL, LANES), const2(1, LANES)],
        out_specs=(tok(D_MODEL), _row_tile_spec(TM_MERGE, lambda i: (cur(i), 0)),
                   pl.BlockSpec((TM_MERGE, ROUTE_W), lambda i: (prev(i), 0)),
                   pl.BlockSpec((1, ROUTE_W, TM_MERGE), lambda i: (prev(i), 0, 0)),
                   pl.BlockSpec((1, LANES), lambda i: (0, 0))),
        out_shape=(jax.ShapeDtypeStruct((TOKENS, D_MODEL), F32),
                   jax.ShapeDtypeStruct((TOKENS * ROW_TILE_S, LANES), F32),
                   jax.ShapeDtypeStruct((TOKENS, ROUTE_W), F32),
                   jax.ShapeDtypeStruct((n_tiles, ROUTE_W, TM_MERGE), I32),
                   jax.ShapeDtypeStruct((1, LANES), F32)),
        scratch_shapes=[pltpu.VMEM((1, LANES), F32), pltpu.VMEM((TM_MERGE, TM_MERGE), BF16),
                        pltpu.VMEM((TM_MERGE, LANES), F32)],
        compiler_params=_cparams(("arbitrary",)),
        name="merge",
    )(x2, o_a, o_b, sa, sb, mod, norm2_g, wa, wb, wo, wr_hi, wr_lo, br)


def _plan_kernel(rk_ref, cnt_ref, dest_ref, blk_ref):
    n_tiles, _, tm = rk_ref.shape
    cnt = cnt_ref[...]
    padded = jnp.floor((cnt + (MOE_BLK - 1)) * (1.0 / MOE_BLK)) * MOE_BLK
    lane = lax.broadcasted_iota(I32, (1, LANES), 1)
    seg_end = padded
    step = 1
    while step < N_EXPERTS:
        seg_end = seg_end + jnp.where(lane >= step, pltpu.roll(seg_end, step, 1), 0.0)
        step *= 2
    seg_start = seg_end - padded
    valid_end = seg_start + cnt

    ei = lax.broadcasted_iota(I32, (LANES, LANES), 0)
    ej = lax.broadcasted_iota(I32, (LANES, LANES), 1)

    def to_col(rowv):
        return jnp.sum(jnp.where(ei == ej, rowv, 0.0), axis=1, keepdims=True)

    start_col, end_col, valid_col = to_col(seg_start), to_col(seg_end), to_col(valid_end)

    nb_pad = blk_ref.shape[1]
    e_sub = lax.broadcasted_iota(I32, (LANES, nb_pad), 0)
    b_start = (lax.broadcasted_iota(I32, (1, nb_pad), 1) * MOE_BLK).astype(F32)
    ends_before = jnp.where((e_sub < N_EXPERTS) & (end_col <= b_start), 1.0, 0.0)
    block_e = jnp.minimum(jnp.sum(ends_before, axis=0, keepdims=True), N_EXPERTS - 1.0)
    block_valid_end = jnp.sum(jnp.where(e_sub.astype(F32) == block_e, valid_col, 0.0),
                              axis=0, keepdims=True)
    n_valid = jnp.clip(block_valid_end - b_start, 0.0, float(MOE_BLK))
    n_used = jnp.max(seg_end, axis=1, keepdims=True) * (1.0 / MOE_BLK)
    nonempty = (to_col(cnt) > 0.0) & (e_sub < N_EXPERTS)
    e_subf = e_sub.astype(F32)
    no_next = float(LANES)
    nxt = jnp.min(jnp.where(nonempty & (e_subf > block_e), e_subf, no_next), axis=0, keepdims=True)
    nxt = jnp.where(nxt == no_next, -1.0, nxt)
    seg_idx = jnp.sum(jnp.where(nonempty & (e_subf < block_e), 1.0, 0.0), axis=0, keepdims=True)
    slot = seg_idx - 2.0 * jnp.floor(seg_idx * 0.5)
    blk_ref[...] = jnp.concatenate(
        [block_e, n_valid, jnp.broadcast_to(n_used, (1, nb_pad)), nxt, slot,
         jnp.zeros((ROUTE_W - 5, nb_pad), F32)], axis=0).astype(I32)

    e_tok = lax.broadcasted_iota(I32, (LANES, tm), 0)

    def tile(t, carry):
        rk = rk_ref[t]
        d = [jnp.sum(jnp.where(e_tok == rk[k:k + 1], start_col, 0.0), axis=0, keepdims=True)
             + rk[k + 2:k + 3].astype(F32) for k in range(TOP_K)]
        dest_ref[t] = jnp.concatenate(d + [jnp.zeros((ROUTE_W - TOP_K, tm), F32)], axis=0).astype(I32)
        return carry

    lax.fori_loop(0, n_tiles, tile, 0)


def _plan_call(rank_rows, cnt):
    n_tiles = TOKENS // TM_MERGE
    nb_pad = -(-N_BLOCKS // LANES) * LANES
    dest, blk = pl.pallas_call(
        _plan_kernel,
        out_shape=(jax.ShapeDtypeStruct((n_tiles, ROUTE_W, TM_MERGE), I32),
                   jax.ShapeDtypeStruct((ROUTE_W, nb_pad), I32)),
        compiler_params=pltpu.CompilerParams(vmem_limit_bytes=VMEM_LIMIT),
        name="plan",
    )(rank_rows, cnt)
    dest1 = dest[:, 0, :].reshape(TOKENS)
    dest2 = dest[:, 1, :].reshape(TOKENS)
    tables = tuple(blk[r, :N_BLOCKS] for r in (0, 1, 3, 4))
    return dest1, dest2, tables, blk[2, :1]


def _sc_params():
    return pltpu.CompilerParams(use_tc_tiling_on_sc=True)


def _sc_mesh():
    return plsc.VectorSubcoreMesh(core_axis_name="core", subcore_axis_name="subcore")


def _sc_worker_base(per_worker):
    wid = lax.axis_index("subcore") * SC_CORES + lax.axis_index("core")
    return wid * per_worker


def _sc_scatter_rows(src, dest1, dest2, n_out):
    n = src.shape[0]
    per_worker = n // SC_WORKERS
    assert per_worker * SC_WORKERS == n and per_worker % SC_CHUNK == 0

    def body(src_hbm, d1_hbm, d2_hbm, out_hbm, idx_v, rows_v):
        base = _sc_worker_base(per_worker)

        @pl.loop(0, per_worker // SC_CHUNK)
        def _(j):
            start = pl.multiple_of(base + j * SC_CHUNK, SC_CHUNK)
            pltpu.sync_copy(src_hbm.at[pl.ds(start, SC_CHUNK)], rows_v)
            for d_hbm in (d1_hbm, d2_hbm):
                pltpu.sync_copy(d_hbm.at[pl.ds(start, SC_CHUNK)], idx_v)
                pltpu.sync_copy(rows_v, out_hbm.at[idx_v])

    return pl.kernel(
        body,
        out_type=jax.ShapeDtypeStruct((n_out, ROW_TILE_S, LANES), F32),
        mesh=_sc_mesh(),
        scratch_types=[pltpu.VMEM((SC_CHUNK,), I32),
                       pltpu.VMEM((SC_CHUNK, ROW_TILE_S, LANES), F32)],
        compiler_params=_sc_params(),
        name="sc_dispatch",
    )(src, dest1, dest2)


def _sc_gather_rows(table, idx):
    n = idx.shape[0]
    per_worker = n // SC_WORKERS
    half = SC_CHUNK // 2
    assert per_worker * SC_WORKERS == n and per_worker % SC_CHUNK == 0

    def body(table_hbm, idx_hbm, out_hbm, idx_v, rows_a, rows_b, gsem_a, gsem_b, wsem_a, wsem_b):
        base = pl.multiple_of(_sc_worker_base(per_worker), SC_CHUNK)
        pltpu.sync_copy(idx_hbm.at[pl.ds(base, per_worker)], idx_v)

        @pl.loop(0, per_worker // SC_CHUNK)
        def _(j):
            off_a = pl.multiple_of(j * SC_CHUNK, SC_CHUNK)
            off_b = pl.multiple_of(j * SC_CHUNK + half, half)
            ga = pltpu.async_copy(table_hbm.at[idx_v.at[pl.ds(off_a, half)]], rows_a, gsem_a)
            gb = pltpu.async_copy(table_hbm.at[idx_v.at[pl.ds(off_b, half)]], rows_b, gsem_b)
            ga.wait()
            wa = pltpu.async_copy(rows_a, out_hbm.at[pl.ds(base + off_a, half)], wsem_a)
            gb.wait()
            wb = pltpu.async_copy(rows_b, out_hbm.at[pl.ds(base + off_b, half)], wsem_b)
            wa.wait()
            wb.wait()

    return pl.kernel(
        body,
        out_type=jax.ShapeDtypeStruct((n, ROW_TILE_S, LANES), F32),
        mesh=_sc_mesh(),
        scratch_types=[pltpu.VMEM((per_worker,), I32),
                       pltpu.VMEM((half, ROW_TILE_S, LANES), F32),
                       pltpu.VMEM((half, ROW_TILE_S, LANES), F32),
                       pltpu.SemaphoreType.DMA, pltpu.SemaphoreType.DMA,
                       pltpu.SemaphoreType.DMA, pltpu.SemaphoreType.DMA],
        compiler_params=_sc_params(),
        name="sc_gather",
    )(table, idx)


def _weight_fetch(w_hbm, stage, sem, e, slot):
    return [pltpu.make_async_copy(w.at[e], st.at[slot], sem.at[slot]) for w, st in zip(w_hbm, stage)]


def _expert_kernel(be_ref, nv_ref, nx_ref, sl_ref, nu_ref, xs_ref, w1_hbm, w3_hbm, w2_hbm, y_ref,
                   w1s, w3s, w2s, w1b, w3b, w2b, sem):
    i = pl.program_id(0)
    e = be_ref[i]
    used = i < nu_ref[0]
    first = (i == 0) | (e != be_ref[jnp.maximum(i - 1, 0)])
    w_hbm = (w1_hbm, w3_hbm, w2_hbm)
    stage = (w1s, w3s, w2s)

    @pl.when(used & first)
    def _():
        slot = sl_ref[i]

        @pl.when(i == 0)
        def _():
            for cp in _weight_fetch(w_hbm, stage, sem, e, slot):
                cp.start()

        for cp in _weight_fetch(w_hbm, stage, sem, e, slot):
            cp.wait()
        w1b[...] = w1s[slot].astype(BF16)
        w3b[...] = w3s[slot].astype(BF16)
        w2b[...] = w2s[slot].astype(BF16)

        @pl.when(nx_ref[i] >= 0)
        def _():
            for cp in _weight_fetch(w_hbm, stage, sem, nx_ref[i], 1 - slot):
                cp.start()

    n_parts = MOE_BLK // MOE_SUB

    def load_x(part):
        row0 = part * MOE_SUB
        row = lax.broadcasted_iota(I32, (MOE_SUB, D_MODEL), 0) + row0
        return jnp.where(row < nv_ref[i], _load_row_tiles(xs_ref, row0, MOE_SUB), 0.0).astype(BF16)

    def zero_y(part):
        y_ref[pl.ds(part * MOE_SUB * ROW_TILE_S, MOE_SUB * ROW_TILE_S), :] = jnp.zeros(
            (MOE_SUB * ROW_TILE_S, LANES), F32)

    def run(parts):
        hmid = []
        for part in parts:
            xb = load_x(part)
            a = jnp.dot(xb, w1b[...], preferred_element_type=F32)
            g = jnp.dot(xb, w3b[...], preferred_element_type=F32)
            hmid.append(((a * _sigmoid(a)) * g).astype(BF16))
        for part, hm in zip(parts, hmid):
            _store_row_tiles(y_ref, jnp.dot(hm, w2b[...], preferred_element_type=F32),
                             part * MOE_SUB)

    n_live = jnp.where(used, (nv_ref[i] + (MOE_SUB - 1)) // MOE_SUB, 0)
    for k in range(n_parts + 1):
        @pl.when(n_live == k)
        def _(k=k):
            if k:
                run(list(range(k)))
            for part in range(k, n_parts):
                zero_y(part)


def _expert_call(tables, n_used, xs, w1, w3, w2):
    n_tab = len(tables)
    grid_spec = pltpu.PrefetchScalarGridSpec(
        num_scalar_prefetch=n_tab + 1,
        grid=(N_BLOCKS,),
        in_specs=[_row_tile_spec(MOE_BLK, lambda i, *pf: (jnp.minimum(i, pf[n_tab][0] - 1), 0)),
                  pl.BlockSpec(memory_space=pl.ANY),
                  pl.BlockSpec(memory_space=pl.ANY),
                  pl.BlockSpec(memory_space=pl.ANY)],
        out_specs=_row_tile_spec(MOE_BLK, lambda i, *pf: (i, 0)),
        scratch_shapes=[pltpu.VMEM((2, D_MODEL, D_EXPERT), F32),
                        pltpu.VMEM((2, D_MODEL, D_EXPERT), F32),
                        pltpu.VMEM((2, D_EXPERT, D_MODEL), F32),
                        pltpu.VMEM((D_MODEL, D_EXPERT), BF16),
                        pltpu.VMEM((D_MODEL, D_EXPERT), BF16),
                        pltpu.VMEM((D_EXPERT, D_MODEL), BF16),
                        pltpu.SemaphoreType.DMA((2,))],
    )
    return pl.pallas_call(
        _expert_kernel,
        grid_spec=grid_spec,
        out_shape=jax.ShapeDtypeStruct((N_BLOCKS * MOE_BLK * ROW_TILE_S, LANES), F32),
        compiler_params=_cparams(("arbitrary",)),
        name="experts",
    )(*tables, n_used, xs, w1, w3, w2)


def _combine_rows_kernel(x1_ref, rf_ref, mod_ref, ya_ref, yb_ref, *rest):
    o_ref = rest[-1]
    rf = rf_ref[...]
    gate2 = mod_ref[0, 5:6, :]
    moe = rf[:, 0:1] * _load_row_tiles(ya_ref) + rf[:, 1:2] * _load_row_tiles(yb_ref)
    o_ref[...] = x1_ref[...] + gate2 * moe


def _combine_rows_call(x1, rf, mod, yg, part, prev=None):
    nb = SEQ // TM_ROW
    n_half = TOKENS // TM_ROW // 2
    t0 = part * n_half
    in_specs = [pl.BlockSpec((TM_ROW, D_MODEL), lambda i: (t0 + i, 0)),
                pl.BlockSpec((TM_ROW, ROUTE_W), lambda i: (t0 + i, 0)),
                pl.BlockSpec((1, N_MOD, D_MODEL), lambda i: ((t0 + i) // nb, 0, 0)),
                _row_tile_spec(TM_ROW, lambda i: (i, 0)),
                _row_tile_spec(TM_ROW, lambda i: (i + n_half, 0))]
    args = [x1, rf, mod, yg, yg]
    aliases = {}
    if prev is not None:
        in_specs.append(pl.BlockSpec(memory_space=pl.ANY))
        args.append(prev)
        aliases = {len(args) - 1: 0}
    return pl.pallas_call(
        _combine_rows_kernel,
        grid=(n_half,),
        in_specs=in_specs,
        out_specs=pl.BlockSpec((TM_ROW, D_MODEL), lambda i: (t0 + i, 0)),
        out_shape=jax.ShapeDtypeStruct((TOKENS, D_MODEL), F32),
        input_output_aliases=aliases,
        compiler_params=_cparams(("parallel",)),
        name="combine",
    )(*args)


def kernel(x, c, positions, ada_w, ada_b, norm1_g, norm2_g, w_in, gla_alpha_up, gla_alpha_b,
           gla_out_g, diff_q_g, diff_k_g, diff_lq1, diff_lk1, diff_lq2, diff_lk2, diff_out_g,
           w_branch_a, w_branch_b, w_out, router_group_w, router_group_b, router_expert_w,
           router_expert_b, expert_w1, expert_w3, expert_w2):
    assert x.shape == (BATCH, SEQ, D_MODEL) and ada_w.shape[0] == 1
    x2 = x.reshape(TOKENS, D_MODEL)

    w_in_p = _wprep_call(w_in[0])
    au_pad = jnp.zeros((LR_PAD, GLA_QK_W), F32).at[:GLA_GATE_RANK].set(gla_alpha_up[0])
    au_hi = au_pad.astype(BF16)
    au_lo = (au_pad - au_hi.astype(F32)).astype(BF16)
    gid = jnp.arange(DIFF_QK_W) // DIFF_DH
    bd = jnp.where(gid[:, None] == gid[None, :], 1.0 / DIFF_DH, 0.0).astype(BF16)
    qg_row = jnp.tile(diff_q_g[0], DIFF_QK_W // DIFF_DH).reshape(1, DIFF_QK_W)
    kg_row = jnp.tile(diff_k_g[0], DIFF_QK_W // DIFF_DH).reshape(1, DIFF_QK_W)
    inv_freq = ROPE_THETA ** (-jnp.arange(ROPE_HALF, dtype=F32) / ROPE_HALF)
    invf64 = jnp.concatenate([inv_freq, inv_freq, jnp.zeros((DIFF_DH - ROPE_DIM,), F32)])
    invf_row = jnp.tile(invf64, LANES // DIFF_DH).reshape(1, LANES)
    pos_col = positions.reshape(TOKENS, 1)
    lam_p = jnp.concatenate([diff_lq1, diff_lk1, diff_lq2, diff_lk2], axis=0)
    wr = (jnp.zeros((D_MODEL, LANES), F32)
          .at[:, :N_GROUPS].set(router_group_w[0])
          .at[:, N_GROUPS:N_GROUPS + N_EXPERTS].set(router_expert_w[0]))
    wr_hi = wr.astype(BF16)
    wr_lo = (wr - wr_hi.astype(F32)).astype(BF16)
    br = (jnp.zeros((1, LANES), F32)
          .at[0, :N_GROUPS].set(router_group_b[0])
          .at[0, N_GROUPS:N_GROUPS + N_EXPERTS].set(router_expert_b[0]))

    mod = _mod_call(c, ada_w[0], ada_b[0])
    gqk, gv, gr, lr, qT, kk, vT, sa, sb = _in_call(
        x2, mod, norm1_g, w_in_p, bd, qg_row, kg_row, pos_col, invf_row)
    o_a = _gla_call(gqk, gv, gr, lr, au_hi, au_lo, gla_alpha_b, gla_out_g)
    o_b = _attn_call(qT, kk, vT, lam_p, diff_out_g.reshape(DIFF_DV, 1))
    x1, h2, rf, rank, cnt = _merge_call(
        x2, o_a, o_b, sa, sb, mod, norm2_g, w_branch_a[0].astype(BF16),
        w_branch_b[0].astype(BF16), w_out[0].astype(BF16), wr_hi, wr_lo, br)
    dest1, dest2, tables, n_used = _plan_call(rank, cnt)

    xs = _sc_scatter_rows(_as_row_tiles(h2), dest1, dest2, N_BLOCKS * MOE_BLK)
    y = _expert_call(tables, n_used, _as_2d(xs), expert_w1[0], expert_w3[0], expert_w2[0])
    y3 = _as_row_tiles(y)
    half = TOKENS // 2
    out = None
    for part in range(2):
        tok = slice(part * half, (part + 1) * half)
        yg = _sc_gather_rows(y3, jnp.concatenate([dest1[tok], dest2[tok]]))
        out = _combine_rows_call(x1, rf, mod, _as_2d(yg), part, out)
    return out.reshape(BATCH, SEQ, D_MODEL)
```

```python
import math

import jax
import jax.numpy as jnp
from jax import lax
from jax.experimental import pallas as pl
from jax.experimental.pallas import tpu as pltpu
from jax.experimental.pallas import tpu_sc as plsc

F32 = jnp.float32
BF16 = jnp.bfloat16
I32 = jnp.int32

D_MODEL = 1024
BATCH = 4
SEQ = 4096
TOKENS = BATCH * SEQ
N_MOD = 6
NORM_EPS = 1e-6

GLA_HEADS = 4
GLA_DK = 64
GLA_DV = 128
GLA_GATE_RANK = 16
GLA_GATE_TAU = 16.0
GLA_CHUNK = 64
GLA_QK_W = GLA_HEADS * GLA_DK
GLA_V_W = GLA_HEADS * GLA_DV

DIFF_HEADS = 4
DIFF_DH = 64
DIFF_DV = 2 * DIFF_DH
DIFF_QK_W = DIFF_HEADS * 2 * DIFF_DH
DIFF_V_W = DIFF_HEADS * DIFF_DV
ROPE_THETA = 500000.0
ROPE_DIM = DIFF_DH // 4
ROPE_HALF = ROPE_DIM // 2
NEG_INF = -1e30
LAMBDA_INIT = 0.8 - 0.6 * 1.0

N_GROUPS = 4
EXPERTS_PER_GROUP = 8
N_EXPERTS = N_GROUPS * EXPERTS_PER_GROUP
TOP_K = 2
D_EXPERT = 512

LANES = 128
SUBLANES = 8
ROW_TILE_S = D_MODEL // LANES
SC_CORES = 2
SC_SUBCORES = 16
SC_WORKERS = SC_CORES * SC_SUBCORES
SC_CHUNK = 64
LR_PAD = LANES
D_IN_PAD = 2 * GLA_QK_W + 2 * GLA_V_W + 2 * DIFF_QK_W + DIFF_V_W + 2 * D_MODEL + LR_PAD
WPREP_LR_STEP = D_IN_PAD // LANES - 1

TM_IN = 512
TQ = 512
V_ROWS = DIFF_DV + 16
ATT_SUB = 32
Q_SCALE = DIFF_DH ** -0.5 * math.log2(math.e)
N_KV = SEQ // TQ
TT_GLA = 512
TM_MERGE = 512
TM_ROW = 256
MOE_BLK = 512
MOE_SUB = 256
N_BLOCKS = (TOKENS * TOP_K + N_EXPERTS * (MOE_BLK - 1) + MOE_BLK - 1) // MOE_BLK
ROUTE_W = 8

VMEM_LIMIT = 56 * 1024 * 1024


def _cparams(sem):
    return pltpu.CompilerParams(dimension_semantics=sem, vmem_limit_bytes=VMEM_LIMIT)


def _sigmoid(x):
    return 1.0 / (1.0 + jnp.exp(-x))


def _row_tile_spec(rows, index_map):
    return pl.BlockSpec((rows * ROW_TILE_S, LANES), index_map)


def _as_row_tiles(a2d):
    return a2d.reshape(a2d.shape[0] // ROW_TILE_S, ROW_TILE_S, LANES)


def _as_2d(a3d):
    return a3d.reshape(a3d.shape[0] * ROW_TILE_S, LANES)


def _store_row_tiles(ref, val, row0=0):
    for s in range(ROW_TILE_S):
        ref[pl.ds(row0 * ROW_TILE_S + s, val.shape[0], stride=ROW_TILE_S), :] = (
            val[:, s * LANES:(s + 1) * LANES])


def _load_row_tiles(ref, row0=0, rows=None):
    rows = ref.shape[0] // ROW_TILE_S if rows is None else rows
    return jnp.concatenate(
        [ref[pl.ds(row0 * ROW_TILE_S + s, rows, stride=ROW_TILE_S), :] for s in range(ROW_TILE_S)],
        axis=1)


def _mod_kernel(ct_ref, w_ref, b_ref, o_ref):
    ct = ct_ref[...]
    ca = ct * _sigmoid(ct)
    w = w_ref[...]
    rows = []
    for b in range(BATCH):
        rows.append(jnp.sum(w * ca[:, b:b + 1], axis=0, keepdims=True) + b_ref[...])
    rows.append(jnp.zeros((8 - BATCH, w.shape[1]), F32))
    o_ref[...] = jnp.concatenate(rows, axis=0)


def _mod_call(c, ada_w, ada_b):
    tn = D_MODEL
    ct = jnp.zeros((D_MODEL, 8), F32).at[:, :BATCH].set(c.T)
    out = pl.pallas_call(
        _mod_kernel,
        grid=(N_MOD,),
        in_specs=[
            pl.BlockSpec((D_MODEL, 8), lambda j: (0, 0)),
            pl.BlockSpec((D_MODEL, tn), lambda j: (0, j)),
            pl.BlockSpec((1, tn), lambda j: (0, j)),
        ],
        out_specs=pl.BlockSpec((8, tn), lambda j: (0, j)),
        out_shape=jax.ShapeDtypeStruct((8, N_MOD * D_MODEL), F32),
        compiler_params=_cparams(("arbitrary",)),
        name="mod",
    )(ct, ada_w, ada_b.reshape(1, N_MOD * D_MODEL))
    return out[:BATCH].reshape(BATCH, N_MOD, D_MODEL)


def _wprep_kernel(w_ref, o_ref):
    j = pl.program_id(0)
    t = w_ref[...].T
    lane = lax.broadcasted_iota(I32, t.shape, 1)
    keep = jnp.logical_or(lane < GLA_GATE_RANK, j < WPREP_LR_STEP)
    o_ref[...] = jnp.where(keep, t, 0.0).astype(BF16)


def _wprep_call(w_in0):
    w_t = w_in0.T
    lr0 = 2 * GLA_QK_W + 2 * GLA_V_W

    def src_row(j):
        return jnp.where(j < lr0 // LANES, j * LANES,
                         jnp.where(j < WPREP_LR_STEP, j * LANES + GLA_GATE_RANK, lr0))

    return pl.pallas_call(
        _wprep_kernel,
        grid=(D_IN_PAD // LANES,),
        in_specs=[pl.BlockSpec((pl.Element(LANES), pl.Element(D_MODEL)),
                               lambda j: (pl.multiple_of(src_row(j), SUBLANES), 0))],
        out_specs=pl.BlockSpec((D_MODEL, LANES), lambda j: (0, j)),
        out_shape=jax.ShapeDtypeStruct((D_MODEL, D_IN_PAD), BF16),
        compiler_params=_cparams(("parallel",)),
        name="w_prep",
    )(w_t)


def _in_kernel(x_ref, mod_ref, g1_ref, w_ref, bd_ref, qg_ref, kg_ref, pos_ref, invf_ref,
               gqk_ref, gv_ref, gr_ref, lr_ref, qT_ref, k_ref, vT_ref, sa_ref, sb_ref,
               cos_scr, sin_scr):
    ang = pos_ref[...].astype(F32) * invf_ref[...]
    cos_scr[...] = jnp.cos(ang)
    sin_scr[...] = jnp.sin(ang)

    x = x_ref[...]
    shift1 = mod_ref[0, 0:1, :]
    scale1 = mod_ref[0, 1:2, :]
    ms = jnp.mean(x * x, axis=-1, keepdims=True)
    h = (x * lax.rsqrt(ms + NORM_EPS) * g1_ref[...]) * (1.0 + scale1) + shift1
    hb = h.astype(BF16)

    def proj(c0, c1):
        return jnp.dot(hb, w_ref[:, c0:c1], preferred_element_type=F32)

    tm = x.shape[0]
    cos4 = jnp.concatenate([cos_scr[...]] * DIFF_HEADS, axis=1)
    sin4 = jnp.concatenate([sin_scr[...]] * DIFF_HEADS, axis=1)
    lane = lax.broadcasted_iota(I32, (tm, DIFF_QK_W), 1)
    first_half = (lane % DIFF_DH) < ROPE_HALF
    bd = bd_ref[...]

    def norm_rope(t, gain_row):
        t2 = t * t
        hi = t2.astype(BF16)
        lo = (t2 - hi.astype(F32)).astype(BF16)
        gms = (jnp.dot(hi, bd, preferred_element_type=F32)
               + jnp.dot(lo, bd, preferred_element_type=F32))
        t = t * lax.rsqrt(gms + NORM_EPS) * gain_row
        nxt = pltpu.roll(t, DIFF_QK_W - ROPE_HALF, 1)
        prv = pltpu.roll(t, ROPE_HALF, 1)
        return t * cos4 + jnp.where(first_half, -nxt, prv) * sin4

    dq_raw = proj(1536, 2048)
    dk_raw = proj(2048, 2560)
    gqk_ref[...] = proj(0, 512).astype(BF16)
    dq = norm_rope(dq_raw, qg_ref[...]) * Q_SCALE
    gv_ref[...] = proj(512, 1024).astype(BF16)
    dk = norm_rope(dk_raw, kg_ref[...])
    dv = proj(2560, 3072)
    for hd in range(DIFF_HEADS):
        sl = slice(hd * LANES, (hd + 1) * LANES)
        qT_ref[0, hd, 0] = dq[:, sl].T.astype(BF16)
        k_ref[0, hd, 0] = dk[:, sl].astype(BF16)
    gr_ref[...] = proj(1024, 1536).astype(BF16)
    for hd in range(DIFF_HEADS):
        sl = slice(hd * LANES, (hd + 1) * LANES)
        vT_ref[0, hd, 0] = jnp.concatenate(
            [dv[:, sl].T, jnp.ones((V_ROWS - DIFF_DV, tm), F32)], axis=0).astype(BF16)
    sa_ref[...] = _sigmoid(proj(3072, 4096)).astype(BF16)
    sb_ref[...] = _sigmoid(proj(4096, 5120)).astype(BF16)
    lr_ref[...] = proj(5120, 5248)


def _in_call(x2, mod, norm1_g, w_in_p, bd, qg_row, kg_row, pos_col, invf_row):
    nb = SEQ // TM_IN
    tok_spec = lambda w: pl.BlockSpec((TM_IN, w), lambda i: (i, 0))
    const2 = lambda r, c: pl.BlockSpec((r, c), lambda i: (0, 0))
    out_shapes = (
        jax.ShapeDtypeStruct((TOKENS, 2 * GLA_QK_W), BF16),
        jax.ShapeDtypeStruct((TOKENS, GLA_V_W), BF16),
        jax.ShapeDtypeStruct((TOKENS, GLA_V_W), BF16),
        jax.ShapeDtypeStruct((TOKENS, LR_PAD), F32),
        jax.ShapeDtypeStruct((BATCH, DIFF_HEADS, N_KV, LANES, TQ), BF16),
        jax.ShapeDtypeStruct((BATCH, DIFF_HEADS, N_KV, TQ, LANES), BF16),
        jax.ShapeDtypeStruct((BATCH, DIFF_HEADS, N_KV, V_ROWS, TQ), BF16),
        jax.ShapeDtypeStruct((TOKENS, D_MODEL), BF16),
        jax.ShapeDtypeStruct((TOKENS, D_MODEL), BF16),
    )
    out_specs = (
        tok_spec(2 * GLA_QK_W), tok_spec(GLA_V_W), tok_spec(GLA_V_W), tok_spec(LR_PAD),
        pl.BlockSpec((1, DIFF_HEADS, 1, LANES, TM_IN), lambda i: (i // nb, 0, i % nb, 0, 0)),
        pl.BlockSpec((1, DIFF_HEADS, 1, TM_IN, LANES), lambda i: (i // nb, 0, i % nb, 0, 0)),
        pl.BlockSpec((1, DIFF_HEADS, 1, V_ROWS, TM_IN), lambda i: (i // nb, 0, i % nb, 0, 0)),
        tok_spec(D_MODEL), tok_spec(D_MODEL),
    )
    return pl.pallas_call(
        _in_kernel,
        grid=(TOKENS // TM_IN,),
        in_specs=[
            tok_spec(D_MODEL),
            pl.BlockSpec((1, N_MOD, D_MODEL), lambda i: (i // nb, 0, 0)),
            const2(1, D_MODEL),
            pl.BlockSpec((D_MODEL, D_IN_PAD), lambda i: (0, 0), pipeline_mode=pl.Buffered(1)),
            const2(DIFF_QK_W, DIFF_QK_W),
            const2(1, DIFF_QK_W), const2(1, DIFF_QK_W),
            tok_spec(1),
            const2(1, LANES),
        ],
        out_specs=out_specs,
        out_shape=out_shapes,
        scratch_shapes=[pltpu.VMEM((TM_IN, LANES), F32), pltpu.VMEM((TM_IN, LANES), F32)],
        compiler_params=_cparams(("parallel",)),
        name="in_proj",
    )(x2, mod, norm1_g, w_in_p, bd, qg_row, kg_row, pos_col, invf_row)


def _gla_kernel(qk_ref, v_ref, r_ref, lr_ref, auh_ref, aul_ref, ab_ref, og_ref, o_ref,
                state_ref, oacc_ref, snew_ref):
    tt = qk_ref.shape[0]
    n_chunks = tt // GLA_CHUNK

    @pl.when(pl.program_id(1) == 0)
    def _():
        state_ref[...] = jnp.zeros_like(state_ref)

    lr = lr_ref[...]
    lr_hi = lr.astype(BF16)
    lr_lo = (lr - lr_hi.astype(F32)).astype(BF16)
    z = (jnp.dot(lr_hi, auh_ref[...], preferred_element_type=F32)
         + jnp.dot(lr_lo, auh_ref[...], preferred_element_type=F32)
         + jnp.dot(lr_hi, aul_ref[...], preferred_element_type=F32)) + ab_ref[...]
    g = (jnp.minimum(z, 0.0) - jnp.log(1.0 + jnp.exp(-jnp.abs(z)))) * (1.0 / GLA_GATE_TAU)

    row = lax.broadcasted_iota(I32, (tt, GLA_QK_W), 0) % GLA_CHUNK
    b = g
    step = 1
    while step < GLA_CHUNK:
        b = b + jnp.where(row >= step, pltpu.roll(b, step, 0), 0.0)
        step *= 2

    b_last_rows = [b[c * GLA_CHUNK + GLA_CHUNK - 1:(c + 1) * GLA_CHUNK, :] for c in range(n_chunks)]
    b_last = jnp.concatenate(
        [jnp.broadcast_to(bl, (GLA_CHUNK, GLA_QK_W)) for bl in b_last_rows], axis=0)

    qk = qk_ref[...].astype(F32)
    q = qk[:, :GLA_QK_W] * (GLA_DK ** -0.5)
    k = qk[:, GLA_QK_W:]
    q_in = (q * jnp.exp(b)).astype(BF16)
    k_in = (k * jnp.exp(-b)).astype(BF16)
    k_dec = (k * jnp.exp(b_last - b)).astype(BF16)

    ci = lax.broadcasted_iota(I32, (GLA_CHUNK, GLA_CHUNK), 0)
    cj = lax.broadcasted_iota(I32, (GLA_CHUNK, GLA_CHUNK), 1)
    causal = ci >= cj

    pairs = [(c, hd) for c in range(n_chunks) for hd in range(GLA_HEADS)]

    def rows(c):
        return slice(c * GLA_CHUNK, (c + 1) * GLA_CHUNK)

    def kcols(hd):
        return slice(hd * GLA_DK, (hd + 1) * GLA_DK)

    def vcols(hd):
        return slice(hd * GLA_DV, (hd + 1) * GLA_DV)

    att = {}
    for c, hd in pairs:
        a = lax.dot_general(q_in[rows(c), kcols(hd)], k_in[rows(c), kcols(hd)],
                            (((1,), (1,)), ((), ())), preferred_element_type=F32)
        att[c, hd] = jnp.where(causal, a, 0.0).astype(BF16)
    for c, hd in pairs:
        oacc_ref[rows(c), vcols(hd)] = jnp.dot(att[c, hd], v_ref[rows(c), vcols(hd)],
                                               preferred_element_type=F32)
    for c, hd in pairs:
        snew_ref[c * GLA_HEADS + hd] = lax.dot_general(
            k_dec[rows(c), kcols(hd)], v_ref[rows(c), vcols(hd)],
            (((0,), (0,)), ((), ())), preferred_element_type=F32)

    decay_rows = jnp.exp(jnp.concatenate(
        b_last_rows + [jnp.zeros((LANES - n_chunks, GLA_QK_W), F32)], axis=0))
    decay_cols = decay_rows.T
    states = [state_ref[hd] for hd in range(GLA_HEADS)]
    for c in range(n_chunks):
        for hd in range(GLA_HEADS):
            s_prev = states[hd]
            oacc_ref[rows(c), vcols(hd)] += jnp.dot(
                q_in[rows(c), kcols(hd)], s_prev.astype(BF16), preferred_element_type=F32)
            dcol = decay_cols[kcols(hd), c:c + 1]
            states[hd] = s_prev * dcol + snew_ref[c * GLA_HEADS + hd]
    for hd in range(GLA_HEADS):
        state_ref[hd] = states[hd]

    for hd in range(GLA_HEADS):
        vs = slice(hd * GLA_DV, (hd + 1) * GLA_DV)
        oh = oacc_ref[:, vs]
        ms = jnp.mean(oh * oh, axis=-1, keepdims=True)
        y = oh * lax.rsqrt(ms + NORM_EPS) * og_ref[...]
        r = r_ref[:, vs].astype(F32)
        o_ref[:, vs] = (y * (r * _sigmoid(r))).astype(BF16)


def _gla_call(gqk, gv, gr, lr, au_hi, au_lo, ab_row, og_row):
    nt = SEQ // TT_GLA
    tok = lambda w: pl.BlockSpec((TT_GLA, w), lambda b, t: (b * nt + t, 0))
    const2 = lambda r, c: pl.BlockSpec((r, c), lambda b, t: (0, 0))
    return pl.pallas_call(
        _gla_kernel,
        grid=(BATCH, nt),
        in_specs=[tok(2 * GLA_QK_W), tok(GLA_V_W), tok(GLA_V_W), tok(LR_PAD),
                  const2(LR_PAD, GLA_QK_W), const2(LR_PAD, GLA_QK_W), const2(1, GLA_QK_W),
                  const2(1, GLA_DV)],
        out_specs=tok(GLA_V_W),
        out_shape=jax.ShapeDtypeStruct((TOKENS, GLA_V_W), BF16),
        scratch_shapes=[pltpu.VMEM((GLA_HEADS, GLA_DK, GLA_DV), F32),
                        pltpu.VMEM((TT_GLA, GLA_V_W), F32),
                        pltpu.VMEM((TT_GLA // GLA_CHUNK * GLA_HEADS, GLA_DK, GLA_DV), F32)],
        compiler_params=_cparams(("parallel", "arbitrary")),
        name="gla",
    )(gqk, gv, gr, lr, au_hi, au_lo, ab_row, og_row)


def _attn_kernel(qT_ref, k_ref, vT_ref, lam_ref, og_ref, o_ref,
                 q_scr, s_scr, p_scr, acc_scr, m_scr):
    _attn_load_q(qT_ref, q_scr, 0, 0)
    s_scr[0] = jnp.dot(k_ref[0, 0, 0], q_scr[0, 0], preferred_element_type=F32)

    def q_block(i, carry):
        _attn_q_block(i, qT_ref, k_ref, vT_ref, lam_ref, og_ref, o_ref,
                      q_scr, s_scr, p_scr, acc_scr, m_scr)
        return carry

    lax.fori_loop(0, N_KV, q_block, 0)


def _attn_load_q(qT_ref, q_scr, i, slot):
    qT = qT_ref[0, 0, i]
    rowq = lax.broadcasted_iota(I32, qT.shape, 0)
    zero = jnp.zeros_like(qT)
    q_scr[slot, 0] = jnp.where(rowq < DIFF_DH, qT, zero)
    q_scr[slot, 1] = jnp.where(rowq >= DIFF_DH, qT, zero)


def _attn_q_block(i, qT_ref, k_ref, vT_ref, lam_ref, og_ref, o_ref,
                  q_scr, s_scr, p_scr, acc_scr, m_scr):
    slot = i % 2
    m_scr[...] = jnp.full(m_scr.shape, NEG_INF, F32)
    acc_scr[...] = jnp.zeros(acc_scr.shape, F32)
    n_sub = TQ // ATT_SUB

    def fold8(t, op):
        return op(t.reshape(t.shape[0] // SUBLANES, SUBLANES, TQ), axis=0)

    def scores(c, j):
        s_scr[c] = jnp.dot(k_ref[0, 0, j], q_scr[slot, c], preferred_element_type=F32)

    def load_s(c, r, masked):
        s = s_scr[c, r * ATT_SUB:(r + 1) * ATT_SUB, :]
        if masked:
            key_i = lax.broadcasted_iota(I32, (ATT_SUB, TQ), 0) + r * ATT_SUB
            qry_i = lax.broadcasted_iota(I32, (ATT_SUB, TQ), 1)
            s = jnp.where(key_i <= qry_i, s, NEG_INF)
        return s

    def softmax_pv(c, j, masked):
        m8 = fold8(load_s(c, 0, masked), jnp.max)
        for r in range(1, n_sub):
            m8 = jnp.maximum(m8, fold8(load_s(c, r, masked), jnp.max))
        m_old = m_scr[c]
        m_new = jnp.maximum(m_old, jnp.max(m8, axis=0, keepdims=True))
        alpha = jnp.exp2(m_old - m_new)
        for r in range(n_sub):
            p = jnp.exp2(load_s(c, r, masked) - m_new)
            p_scr[c, r * ATT_SUB:(r + 1) * ATT_SUB, :] = p.astype(BF16)
        m_scr[c] = m_new
        acc_scr[c] = acc_scr[c] * alpha + jnp.dot(vT_ref[0, 0, j], p_scr[c],
                                                  preferred_element_type=F32)

    def body(j, carry):
        scores(1, j)
        softmax_pv(0, j, False)
        scores(0, j + 1)
        softmax_pv(1, j, False)
        return carry

    lax.fori_loop(0, i, body, 0)
    scores(1, i)
    softmax_pv(0, i, True)
    softmax_pv(1, i, True)

    nxt = jnp.minimum(i + 1, N_KV - 1)
    _attn_load_q(qT_ref, q_scr, nxt, 1 - slot)
    s_scr[0] = jnp.dot(k_ref[0, 0, 0], q_scr[1 - slot, 0], preferred_element_type=F32)

    l1 = acc_scr[0, DIFF_DV:DIFF_DV + 1, :]
    l2 = acc_scr[1, DIFF_DV:DIFF_DV + 1, :]

    lam_p = lam_ref[...]
    lam = (jnp.exp(jnp.sum(lam_p[0:1] * lam_p[1:2], axis=1, keepdims=True))
           - jnp.exp(jnp.sum(lam_p[2:3] * lam_p[3:4], axis=1, keepdims=True)) + LAMBDA_INIT)
    oT = acc_scr[0, :DIFF_DV, :] / l1 - lam * (acc_scr[1, :DIFF_DV, :] / l2)
    ms = jnp.mean(oT * oT, axis=0, keepdims=True)
    y = oT * lax.rsqrt(ms + NORM_EPS) * og_ref[...] * (1.0 - LAMBDA_INIT)
    o_ref[pl.ds(pl.multiple_of(i * TQ, TQ), TQ), :] = y.T.astype(BF16)


def _attn_call(qT, kk, vT, lam_p, og_col):
    return pl.pallas_call(
        _attn_kernel,
        grid=(BATCH, DIFF_HEADS),
        in_specs=[
            pl.BlockSpec((1, 1, N_KV, LANES, TQ), lambda b, h: (b, h, 0, 0, 0)),
            pl.BlockSpec((1, 1, N_KV, TQ, LANES), lambda b, h: (b, h, 0, 0, 0)),
            pl.BlockSpec((1, 1, N_KV, V_ROWS, TQ), lambda b, h: (b, h, 0, 0, 0)),
            pl.BlockSpec((4, DIFF_DH), lambda b, h: (0, 0)),
            pl.BlockSpec((DIFF_DV, 1), lambda b, h: (0, 0)),
        ],
        out_specs=pl.BlockSpec((SEQ, DIFF_DV), lambda b, h: (b, h)),
        out_shape=jax.ShapeDtypeStruct((TOKENS, DIFF_V_W), BF16),
        scratch_shapes=[pltpu.VMEM((2, 2, LANES, TQ), BF16),
                        pltpu.VMEM((2, TQ, TQ), F32),
                        pltpu.VMEM((2, TQ, TQ), BF16),
                        pltpu.VMEM((2, V_ROWS, TQ), F32),
                        pltpu.VMEM((2, 1, TQ), F32)],
        compiler_params=_cparams(("parallel", "parallel")),
        name="attn",
    )(qT, kk, vT, lam_p, og_col)


def _merge_kernel(x_ref, oa_ref, ob_ref, sa_ref, sb_ref, mod_ref, g2_ref, wa_ref, wb_ref, wo_ref,
                  wrh_ref, wrl_ref, br_ref, x1_ref, h2_ref, rf_ref, rank_ref, cnt_ref,
                  carry_ref, tri_ref, lg_ref):
    tm = x_ref.shape[0]
    step = pl.program_id(0)

    @pl.when(step == 0)
    def _():
        carry_ref[...] = jnp.zeros_like(carry_ref)
        lg_ref[...] = jnp.zeros_like(lg_ref)
        ti = lax.broadcasted_iota(I32, (tm, tm), 0)
        tj = lax.broadcasted_iota(I32, (tm, tm), 1)
        tri_ref[...] = jnp.where(ti > tj, 1.0, 0.0).astype(BF16)

    ma = jnp.dot(oa_ref[...], wa_ref[...], preferred_element_type=F32)
    mb = jnp.dot(ob_ref[...], wb_ref[...], preferred_element_type=F32)
    merged = sa_ref[...].astype(F32) * ma + sb_ref[...].astype(F32) * mb
    y = jnp.dot(merged.astype(BF16), wo_ref[...], preferred_element_type=F32)
    logits = lg_ref[...]
    lane = lax.broadcasted_iota(I32, (tm, LANES), 1).astype(F32)
    ninf = -jnp.inf
    big = float(LANES)

    def first_argmax(v):
        vmax = jnp.max(v, axis=1, keepdims=True)
        idx = jnp.min(jnp.where(v == vmax, lane, big), axis=1, keepdims=True)
        return vmax, idx

    gl = jnp.where(lane < N_GROUPS, logits, ninf)
    gmax, gidx = first_argmax(gl)
    p_top = 1.0 / jnp.sum(jnp.exp(gl - gmax), axis=1, keepdims=True)
    lo = N_GROUPS + EXPERTS_PER_GROUP * gidx
    el = jnp.where((lane >= lo) & (lane < lo + EXPERTS_PER_GROUP), logits, ninf)
    e1max, e1 = first_argmax(el)
    e2max, e2 = first_argmax(jnp.where(lane == e1, ninf, el))
    t = jnp.exp(e2max - e1max)
    w1 = 1.0 / (1.0 + t)
    w2 = t / (1.0 + t)
    col = lax.broadcasted_iota(I32, (tm, ROUTE_W), 1)
    rf_ref[...] = jnp.where(col == 0, p_top * w1, jnp.where(col == 1, p_top * w2, 0.0))

    x1id = e1 - N_GROUPS
    x2id = e2 - N_GROUPS
    hit1 = lane == x1id
    hit2 = lane == x2id
    onehot = jnp.where(hit1 | hit2, 1.0, 0.0)
    before = jnp.dot(tri_ref[...], onehot.astype(BF16), preferred_element_type=F32) + carry_ref[...]
    r1 = jnp.sum(jnp.where(hit1, before, 0.0), axis=1, keepdims=True)
    r2 = jnp.sum(jnp.where(hit2, before, 0.0), axis=1, keepdims=True)
    real_tile = jnp.where(step > 0, 1.0, 0.0)
    carry_ref[...] = carry_ref[...] + real_tile * jnp.sum(onehot, axis=0, keepdims=True)
    cnt_ref[...] = carry_ref[...]
    cols = jnp.where(lane == 0.0, x1id, jnp.where(lane == 1.0, x2id,
                                                  jnp.where(lane == 2.0, r1,
                                                            jnp.where(lane == 3.0, r2, 0.0))))
    rank_ref[0] = cols.T[0:ROUTE_W, :].astype(I32)

    gate1 = mod_ref[0, 2:3, :]
    shift2 = mod_ref[0, 3:4, :]
    scale2 = mod_ref[0, 4:5, :]
    x1 = x_ref[...] + gate1 * y
    x1_ref[...] = x1
    ms = jnp.mean(x1 * x1, axis=-1, keepdims=True)
    h2 = (x1 * lax.rsqrt(ms + NORM_EPS) * g2_ref[...]) * (1.0 + scale2) + shift2
    _store_row_tiles(h2_ref, h2)
    h2_hi = h2.astype(BF16)
    h2_lo = (h2 - h2_hi.astype(F32)).astype(BF16)
    next_logits = (jnp.dot(h2_hi, wrh_ref[...], preferred_element_type=F32)
                   + jnp.dot(h2_lo, wrh_ref[...], preferred_element_type=F32)
                   + jnp.dot(h2_hi, wrl_ref[...], preferred_element_type=F32)) + br_ref[...]

    lg_ref[...] = next_logits


def _merge_call(x2, o_a, o_b, sa, sb, mod, norm2_g, wa, wb, wo, wr_hi, wr_lo, br):
    nb = SEQ // TM_MERGE
    n_tiles = TOKENS // TM_MERGE
    cur = lambda i: jnp.minimum(i, n_tiles - 1)
    prev = lambda i: jnp.maximum(i - 1, 0)
    tok = lambda w: pl.BlockSpec((TM_MERGE, w), lambda i: (cur(i), 0))
    const2 = lambda r, c: pl.BlockSpec((r, c), lambda i: (0, 0))
    return pl.pallas_call(
        _merge_kernel,
        grid=(n_tiles + 1,),
        in_specs=[tok(D_MODEL), tok(GLA_V_W), tok(DIFF_V_W), tok(D_MODEL), tok(D_MODEL),
                  pl.BlockSpec((1, N_MOD, D_MODEL), lambda i: (cur(i) // nb, 0, 0)),
                  const2(1, D_MODEL),
                  const2(GLA_V_W, D_MODEL), const2(DIFF_V_W, D_MODEL), const2(D_MODEL, D_MODEL),
                  const2(D_MODEL, LANES), const2(D_MODEL, LANES), const2(1, LANES)],
        out_specs=(tok(D_MODEL), _row_tile_spec(TM_MERGE, lambda i: (cur(i), 0)),
                   pl.BlockSpec((TM_MERGE, ROUTE_W), lambda i: (prev(i), 0)),
                   pl.BlockSpec((1, ROUTE_W, TM_MERGE), lambda i: (prev(i), 0, 0)),
                   pl.BlockSpec((1, LANES), lambda i: (0, 0))),
        out_shape=(jax.ShapeDtypeStruct((TOKENS, D_MODEL), F32),
                   jax.ShapeDtypeStruct((TOKENS * ROW_TILE_S, LANES), F32),
                   jax.ShapeDtypeStruct((TOKENS, ROUTE_W), F32),
                   jax.ShapeDtypeStruct((n_tiles, ROUTE_W, TM_MERGE), I32),
                   jax.ShapeDtypeStruct((1, LANES), F32)),
        scratch_shapes=[pltpu.VMEM((1, LANES), F32), pltpu.VMEM((TM_MERGE, TM_MERGE), BF16),
                        pltpu.VMEM((TM_MERGE, LANES), F32)],
        compiler_params=_cparams(("arbitrary",)),
        name="merge",
    )(x2, o_a, o_b, sa, sb, mod, norm2_g, wa, wb, wo, wr_hi, wr_lo, br)


def _plan_kernel(rk_ref, cnt_ref, dest_ref, blk_ref):
    n_tiles, _, tm = rk_ref.shape
    cnt = cnt_ref[...]
    padded = jnp.floor((cnt + (MOE_BLK - 1)) * (1.0 / MOE_BLK)) * MOE_BLK
    lane = lax.broadcasted_iota(I32, (1, LANES), 1)
    seg_end = padded
    step = 1
    while step < N_EXPERTS:
        seg_end = seg_end + jnp.where(lane >= step, pltpu.roll(seg_end, step, 1), 0.0)
        step *= 2
    seg_start = seg_end - padded
    valid_end = seg_start + cnt

    ei = lax.broadcasted_iota(I32, (LANES, LANES), 0)
    ej = lax.broadcasted_iota(I32, (LANES, LANES), 1)

    def to_col(rowv):
        return jnp.sum(jnp.where(ei == ej, rowv, 0.0), axis=1, keepdims=True)

    start_col, end_col, valid_col = to_col(seg_start), to_col(seg_end), to_col(valid_end)

    nb_pad = blk_ref.shape[1]
    e_sub = lax.broadcasted_iota(I32, (LANES, nb_pad), 0)
    b_start = (lax.broadcasted_iota(I32, (1, nb_pad), 1) * MOE_BLK).astype(F32)
    ends_before = jnp.where((e_sub < N_EXPERTS) & (end_col <= b_start), 1.0, 0.0)
    block_e = jnp.minimum(jnp.sum(ends_before, axis=0, keepdims=True), N_EXPERTS - 1.0)
    block_valid_end = jnp.sum(jnp.where(e_sub.astype(F32) == block_e, valid_col, 0.0),
                              axis=0, keepdims=True)
    n_valid = jnp.clip(block_valid_end - b_start, 0.0, float(MOE_BLK))
    n_used = jnp.max(seg_end, axis=1, keepdims=True) * (1.0 / MOE_BLK)
    nonempty = (to_col(cnt) > 0.0) & (e_sub < N_EXPERTS)
    e_subf = e_sub.astype(F32)
    no_next = float(LANES)
    nxt = jnp.min(jnp.where(nonempty & (e_subf > block_e), e_subf, no_next), axis=0, keepdims=True)
    nxt = jnp.where(nxt == no_next, -1.0, nxt)
    seg_idx = jnp.sum(jnp.where(nonempty & (e_subf < block_e), 1.0, 0.0), axis=0, keepdims=True)
    slot = seg_idx - 2.0 * jnp.floor(seg_idx * 0.5)
    blk_ref[...] = jnp.concatenate(
        [block_e, n_valid, jnp.broadcast_to(n_used, (1, nb_pad)), nxt, slot,
         jnp.zeros((ROUTE_W - 5, nb_pad), F32)], axis=0).astype(I32)

    e_tok = lax.broadcasted_iota(I32, (LANES, tm), 0)

    def tile(t, carry):
        rk = rk_ref[t]
        d = [jnp.sum(jnp.where(e_tok == rk[k:k + 1], start_col, 0.0), axis=0, keepdims=True)
             + rk[k + 2:k + 3].astype(F32) for k in range(TOP_K)]
        dest_ref[t] = jnp.concatenate(d + [jnp.zeros((ROUTE_W - TOP_K, tm), F32)], axis=0).astype(I32)
        return carry

    lax.fori_loop(0, n_tiles, tile, 0)


def _plan_call(rank_rows, cnt):
    n_tiles = TOKENS // TM_MERGE
    nb_pad = -(-N_BLOCKS // LANES) * LANES
    dest, blk = pl.pallas_call(
        _plan_kernel,
        out_shape=(jax.ShapeDtypeStruct((n_tiles, ROUTE_W, TM_MERGE), I32),
                   jax.ShapeDtypeStruct((ROUTE_W, nb_pad), I32)),
        compiler_params=pltpu.CompilerParams(vmem_limit_bytes=VMEM_LIMIT),
        name="plan",
    )(rank_rows, cnt)
    dest1 = dest[:, 0, :].reshape(TOKENS)
    dest2 = dest[:, 1, :].reshape(TOKENS)
    tables = tuple(blk[r, :N_BLOCKS] for r in (0, 1, 3, 4))
    return dest1, dest2, tables, blk[2, :1]


def _sc_params():
    return pltpu.CompilerParams(use_tc_tiling_on_sc=True)


def _sc_mesh():
    return plsc.VectorSubcoreMesh(core_axis_name="core", subcore_axis_name="subcore")


def _sc_worker_base(per_worker):
    wid = lax.axis_index("subcore") * SC_CORES + lax.axis_index("core")
    return wid * per_worker


def _sc_scatter_rows(src, dest1, dest2, n_out):
    n = src.shape[0]
    per_worker = n // SC_WORKERS
    assert per_worker * SC_WORKERS == n and per_worker % SC_CHUNK == 0

    def body(src_hbm, d1_hbm, d2_hbm, out_hbm, idx_v, rows_v):
        base = _sc_worker_base(per_worker)

        @pl.loop(0, per_worker // SC_CHUNK)
        def _(j):
            start = pl.multiple_of(base + j * SC_CHUNK, SC_CHUNK)
            pltpu.sync_copy(src_hbm.at[pl.ds(start, SC_CHUNK)], rows_v)
            for d_hbm in (d1_hbm, d2_hbm):
                pltpu.sync_copy(d_hbm.at[pl.ds(start, SC_CHUNK)], idx_v)
                pltpu.sync_copy(rows_v, out_hbm.at[idx_v])

    return pl.kernel(
        body,
        out_type=jax.ShapeDtypeStruct((n_out, ROW_TILE_S, LANES), F32),
        mesh=_sc_mesh(),
        scratch_types=[pltpu.VMEM((SC_CHUNK,), I32),
                       pltpu.VMEM((SC_CHUNK, ROW_TILE_S, LANES), F32)],
        compiler_params=_sc_params(),
        name="sc_dispatch",
    )(src, dest1, dest2)


def _sc_gather_rows(table, idx):
    n = idx.shape[0]
    per_worker = n // SC_WORKERS
    half = SC_CHUNK // 2
    assert per_worker * SC_WORKERS == n and per_worker % SC_CHUNK == 0

    def body(table_hbm, idx_hbm, out_hbm, idx_v, rows_a, rows_b, gsem_a, gsem_b, wsem_a, wsem_b):
        base = pl.multiple_of(_sc_worker_base(per_worker), SC_CHUNK)
        pltpu.sync_copy(idx_hbm.at[pl.ds(base, per_worker)], idx_v)

        @pl.loop(0, per_worker // SC_CHUNK)
        def _(j):
            off_a = pl.multiple_of(j * SC_CHUNK, SC_CHUNK)
            off_b = pl.multiple_of(j * SC_CHUNK + half, half)
            ga = pltpu.async_copy(table_hbm.at[idx_v.at[pl.ds(off_a, half)]], rows_a, gsem_a)
            gb = pltpu.async_copy(table_hbm.at[idx_v.at[pl.ds(off_b, half)]], rows_b, gsem_b)
            ga.wait()
            wa = pltpu.async_copy(rows_a, out_hbm.at[pl.ds(base + off_a, half)], wsem_a)
            gb.wait()
            wb = pltpu.async_copy(rows_b, out_hbm.at[pl.ds(base + off_b, half)], wsem_b)
            wa.wait()
            wb.wait()

    return pl.kernel(
        body,
        out_type=jax.ShapeDtypeStruct((n, ROW_TILE_S, LANES), F32),
        mesh=_sc_mesh(),
        scratch_types=[pltpu.VMEM((per_worker,), I32),
                       pltpu.VMEM((half, ROW_TILE_S, LANES), F32),
                       pltpu.VMEM((half, ROW_TILE_S, LANES), F32),
                       pltpu.SemaphoreType.DMA, pltpu.SemaphoreType.DMA,
                       pltpu.SemaphoreType.DMA, pltpu.SemaphoreType.DMA],
        compiler_params=_sc_params(),
        name="sc_gather",
    )(table, idx)


def _weight_fetch(w_hbm, stage, sem, e, slot):
    return [pltpu.make_async_copy(w.at[e], st.at[slot], sem.at[slot]) for w, st in zip(w_hbm, stage)]


def _expert_kernel(be_ref, nv_ref, nx_ref, sl_ref, nu_ref, xs_ref, w1_hbm, w3_hbm, w2_hbm, y_ref,
                   w1s, w3s, w2s, w1b, w3b, w2b, sem):
    i = pl.program_id(0)
    e = be_ref[i]
    used = i < nu_ref[0]
    first = (i == 0) | (e != be_ref[jnp.maximum(i - 1, 0)])
    w_hbm = (w1_hbm, w3_hbm, w2_hbm)
    stage = (w1s, w3s, w2s)

    @pl.when(used & first)
    def _():
        slot = sl_ref[i]

        @pl.when(i == 0)
        def _():
            for cp in _weight_fetch(w_hbm, stage, sem, e, slot):
                cp.start()

        for cp in _weight_fetch(w_hbm, stage, sem, e, slot):
            cp.wait()
        w1b[...] = w1s[slot].astype(BF16)
        w3b[...] = w3s[slot].astype(BF16)
        w2b[...] = w2s[slot].astype(BF16)

        @pl.when(nx_ref[i] >= 0)
        def _():
            for cp in _weight_fetch(w_hbm, stage, sem, nx_ref[i], 1 - slot):
                cp.start()

    n_parts = MOE_BLK // MOE_SUB

    def load_x(part):
        row0 = part * MOE_SUB
        row = lax.broadcasted_iota(I32, (MOE_SUB, D_MODEL), 0) + row0
        return jnp.where(row < nv_ref[i], _load_row_tiles(xs_ref, row0, MOE_SUB), 0.0).astype(BF16)

    def zero_y(part):
        y_ref[pl.ds(part * MOE_SUB * ROW_TILE_S, MOE_SUB * ROW_TILE_S), :] = jnp.zeros(
            (MOE_SUB * ROW_TILE_S, LANES), F32)

    def run(parts):
        hmid = []
        for part in parts:
            xb = load_x(part)
            a = jnp.dot(xb, w1b[...], preferred_element_type=F32)
            g = jnp.dot(xb, w3b[...], preferred_element_type=F32)
            hmid.append(((a * _sigmoid(a)) * g).astype(BF16))
        for part, hm in zip(parts, hmid):
            _store_row_tiles(y_ref, jnp.dot(hm, w2b[...], preferred_element_type=F32),
                             part * MOE_SUB)

    n_live = jnp.where(used, (nv_ref[i] + (MOE_SUB - 1)) // MOE_SUB, 0)
    for k in range(n_parts + 1):
        @pl.when(n_live == k)
        def _(k=k):
            if k:
                run(list(range(k)))
            for part in range(k, n_parts):
                zero_y(part)


def _expert_call(tables, n_used, xs, w1, w3, w2):
    n_tab = len(tables)
    grid_spec = pltpu.PrefetchScalarGridSpec(
        num_scalar_prefetch=n_tab + 1,
        grid=(N_BLOCKS,),
        in_specs=[_row_tile_spec(MOE_BLK, lambda i, *pf: (jnp.minimum(i, pf[n_tab][0] - 1), 0)),
                  pl.BlockSpec(memory_space=pl.ANY),
                  pl.BlockSpec(memory_space=pl.ANY),
                  pl.BlockSpec(memory_space=pl.ANY)],
        out_specs=_row_tile_spec(MOE_BLK, lambda i, *pf: (i, 0)),
        scratch_shapes=[pltpu.VMEM((2, D_MODEL, D_EXPERT), F32),
                        pltpu.VMEM((2, D_MODEL, D_EXPERT), F32),
                        pltpu.VMEM((2, D_EXPERT, D_MODEL), F32),
                        pltpu.VMEM((D_MODEL, D_EXPERT), BF16),
                        pltpu.VMEM((D_MODEL, D_EXPERT), BF16),
                        pltpu.VMEM((D_EXPERT, D_MODEL), BF16),
                        pltpu.SemaphoreType.DMA((2,))],
    )
    return pl.pallas_call(
        _expert_kernel,
        grid_spec=grid_spec,
        out_shape=jax.ShapeDtypeStruct((N_BLOCKS * MOE_BLK * ROW_TILE_S, LANES), F32),
        compiler_params=_cparams(("arbitrary",)),
        name="experts",
    )(*tables, n_used, xs, w1, w3, w2)


def _combine_rows_kernel(x1_ref, rf_ref, mod_ref, ya_ref, yb_ref, *rest):
    o_ref = rest[-1]
    rf = rf_ref[...]
    gate2 = mod_ref[0, 5:6, :]
    moe = rf[:, 0:1] * _load_row_tiles(ya_ref) + rf[:, 1:2] * _load_row_tiles(yb_ref)
    o_ref[...] = x1_ref[...] + gate2 * moe


def _combine_rows_call(x1, rf, mod, yg, part, prev=None):
    nb = SEQ // TM_ROW
    n_half = TOKENS // TM_ROW // 2
    t0 = part * n_half
    in_specs = [pl.BlockSpec((TM_ROW, D_MODEL), lambda i: (t0 + i, 0)),
                pl.BlockSpec((TM_ROW, ROUTE_W), lambda i: (t0 + i, 0)),
                pl.BlockSpec((1, N_MOD, D_MODEL), lambda i: ((t0 + i) // nb, 0, 0)),
                _row_tile_spec(TM_ROW, lambda i: (i, 0)),
                _row_tile_spec(TM_ROW, lambda i: (i + n_half, 0))]
    args = [x1, rf, mod, yg, yg]
    aliases = {}
    if prev is not None:
        in_specs.append(pl.BlockSpec(memory_space=pl.ANY))
        args.append(prev)
        aliases = {len(args) - 1: 0}
    return pl.pallas_call(
        _combine_rows_kernel,
        grid=(n_half,),
        in_specs=in_specs,
        out_specs=pl.BlockSpec((TM_ROW, D_MODEL), lambda i: (t0 + i, 0)),
        out_shape=jax.ShapeDtypeStruct((TOKENS, D_MODEL), F32),
        input_output_aliases=aliases,
        compiler_params=_cparams(("parallel",)),
        name="combine",
    )(*args)


def kernel(x, c, positions, ada_w, ada_b, norm1_g, norm2_g, w_in, gla_alpha_up, gla_alpha_b,
           gla_out_g, diff_q_g, diff_k_g, diff_lq1, diff_lk1, diff_lq2, diff_lk2, diff_out_g,
           w_branch_a, w_branch_b, w_out, router_group_w, router_group_b, router_expert_w,
           router_expert_b, expert_w1, expert_w3, expert_w2):
    assert x.shape == (BATCH, SEQ, D_MODEL) and ada_w.shape[0] == 1
    x2 = x.reshape(TOKENS, D_MODEL)

    w_in_p = _wprep_call(w_in[0])
    au_pad = jnp.zeros((LR_PAD, GLA_QK_W), F32).at[:GLA_GATE_RANK].set(gla_alpha_up[0])
    au_hi = au_pad.astype(BF16)
    au_lo = (au_pad - au_hi.astype(F32)).astype(BF16)
    gid = jnp.arange(DIFF_QK_W) // DIFF_DH
    bd = jnp.where(gid[:, None] == gid[None, :], 1.0 / DIFF_DH, 0.0).astype(BF16)
    qg_row = jnp.tile(diff_q_g[0], DIFF_QK_W // DIFF_DH).reshape(1, DIFF_QK_W)
    kg_row = jnp.tile(diff_k_g[0], DIFF_QK_W // DIFF_DH).reshape(1, DIFF_QK_W)
    inv_freq = ROPE_THETA ** (-jnp.arange(ROPE_HALF, dtype=F32) / ROPE_HALF)
    invf64 = jnp.concatenate([inv_freq, inv_freq, jnp.zeros((DIFF_DH - ROPE_DIM,), F32)])
    invf_row = jnp.tile(invf64, LANES // DIFF_DH).reshape(1, LANES)
    pos_col = positions.reshape(TOKENS, 1)
    lam_p = jnp.concatenate([diff_lq1, diff_lk1, diff_lq2, diff_lk2], axis=0)
    wr = (jnp.zeros((D_MODEL, LANES), F32)
          .at[:, :N_GROUPS].set(router_group_w[0])
          .at[:, N_GROUPS:N_GROUPS + N_EXPERTS].set(router_expert_w[0]))
    wr_hi = wr.astype(BF16)
    wr_lo = (wr - wr_hi.astype(F32)).astype(BF16)
    br = (jnp.zeros((1, LANES), F32)
          .at[0, :N_GROUPS].set(router_group_b[0])
          .at[0, N_GROUPS:N_GROUPS + N_EXPERTS].set(router_expert_b[0]))

    mod = _mod_call(c, ada_w[0], ada_b[0])
    gqk, gv, gr, lr, qT, kk, vT, sa, sb = _in_call(
        x2, mod, norm1_g, w_in_p, bd, qg_row, kg_row, pos_col, invf_row)
    o_a = _gla_call(gqk, gv, gr, lr, au_hi, au_lo, gla_alpha_b, gla_out_g)
    o_b = _attn_call(qT, kk, vT, lam_p, diff_out_g.reshape(DIFF_DV, 1))
    x1, h2, rf, rank, cnt = _merge_call(
        x2, o_a, o_b, sa, sb, mod, norm2_g, w_branch_a[0].astype(BF16),
        w_branch_b[0].astype(BF16), w_out[0].astype(BF16), wr_hi, wr_lo, br)
    dest1, dest2, tables, n_used = _plan_call(rank, cnt)

    xs = _sc_scatter_rows(_as_row_tiles(h2), dest1, dest2, N_BLOCKS * MOE_BLK)
    y = _expert_call(tables, n_used, _as_2d(xs), expert_w1[0], expert_w3[0], expert_w2[0])
    y3 = _as_row_tiles(y)
    half = TOKENS // 2
    out = None
    for part in range(2):
        tok = slice(part * half, (part + 1) * half)
        yg = _sc_gather_rows(y3, jnp.concatenate([dest1[tok], dest2[tok]]))
        out = _combine_rows_call(x1, rf, mod, _as_2d(yg), part, out)
    return out.reshape(BATCH, SEQ, D_MODEL)
```

```python
import math

import jax
import jax.numpy as jnp
from jax import lax
from jax.experimental import pallas as pl
from jax.experimental.pallas import tpu as pltpu
from jax.experimental.pallas import tpu_sc as plsc

F32 = jnp.float32
BF16 = jnp.bfloat16
I32 = jnp.int32

D_MODEL = 1024
BATCH = 4
SEQ = 4096
TOKENS = BATCH * SEQ
N_MOD = 6
NORM_EPS = 1e-6

GLA_HEADS = 4
GLA_DK = 64
GLA_DV = 128
GLA_GATE_RANK = 16
GLA_GATE_TAU = 16.0
GLA_CHUNK = 64
GLA_QK_W = GLA_HEADS * GLA_DK
GLA_V_W = GLA_HEADS * GLA_DV

DIFF_HEADS = 4
DIFF_DH = 64
DIFF_DV = 2 * DIFF_DH
DIFF_QK_W = DIFF_HEADS * 2 * DIFF_DH
DIFF_V_W = DIFF_HEADS * DIFF_DV
ROPE_THETA = 500000.0
ROPE_DIM = DIFF_DH // 4
ROPE_HALF = ROPE_DIM // 2
NEG_INF = -1e30
LAMBDA_INIT = 0.8 - 0.6 * 1.0

N_GROUPS = 4
EXPERTS_PER_GROUP = 8
N_EXPERTS = N_GROUPS * EXPERTS_PER_GROUP
TOP_K = 2
D_EXPERT = 512

LANES = 128
SUBLANES = 8
ROW_TILE_S = D_MODEL // LANES
SC_CORES = 2
SC_SUBCORES = 16
SC_WORKERS = SC_CORES * SC_SUBCORES
SC_CHUNK = 64
LR_PAD = LANES
D_IN_PAD = 2 * GLA_QK_W + 2 * GLA_V_W + 2 * DIFF_QK_W + DIFF_V_W + 2 * D_MODEL + LR_PAD
WPREP_COLS = 512
WPREP_LR_STEP = (D_IN_PAD - LR_PAD) // WPREP_COLS
D_IN_ALLOC = (WPREP_LR_STEP + 1) * WPREP_COLS

TM_IN = 512
TQ = 512
V_ROWS = DIFF_DV + 16
ATT_SUB = 32
Q_SCALE = DIFF_DH ** -0.5 * math.log2(math.e)
N_KV = SEQ // TQ
TT_GLA = 512
TM_MERGE = 512
TM_ROW = 512
MOE_BLK = 512
MOE_SUB = 256
N_BLOCKS = (TOKENS * TOP_K + N_EXPERTS * (MOE_BLK - 1) + MOE_BLK - 1) // MOE_BLK
ROUTE_W = 8

VMEM_LIMIT = 56 * 1024 * 1024


def _cparams(sem):
    return pltpu.CompilerParams(dimension_semantics=sem, vmem_limit_bytes=VMEM_LIMIT)


def _sigmoid(x):
    return 1.0 / (1.0 + jnp.exp(-x))


def _row_tile_spec(rows, index_map):
    return pl.BlockSpec((rows * ROW_TILE_S, LANES), index_map)


def _as_row_tiles(a2d):
    return a2d.reshape(a2d.shape[0] // ROW_TILE_S, ROW_TILE_S, LANES)


def _as_2d(a3d):
    return a3d.reshape(a3d.shape[0] * ROW_TILE_S, LANES)


def _store_row_tiles(ref, val, row0=0):
    for s in range(ROW_TILE_S):
        ref[pl.ds(row0 * ROW_TILE_S + s, val.shape[0], stride=ROW_TILE_S), :] = (
            val[:, s * LANES:(s + 1) * LANES])


def _load_row_tiles(ref, row0=0, rows=None):
    rows = ref.shape[0] // ROW_TILE_S if rows is None else rows
    return jnp.concatenate(
        [ref[pl.ds(row0 * ROW_TILE_S + s, rows, stride=ROW_TILE_S), :] for s in range(ROW_TILE_S)],
        axis=1)


def _mod_kernel(ct_ref, w_ref, b_ref, o_ref):
    ct = ct_ref[...]
    ca = ct * _sigmoid(ct)
    w = w_ref[...]
    rows = []
    for b in range(BATCH):
        rows.append(jnp.sum(w * ca[:, b:b + 1], axis=0, keepdims=True) + b_ref[...])
    rows.append(jnp.zeros((8 - BATCH, w.shape[1]), F32))
    o_ref[...] = jnp.concatenate(rows, axis=0)


def _mod_call(c, ada_w, ada_b):
    tn = D_MODEL
    ct = jnp.zeros((D_MODEL, 8), F32).at[:, :BATCH].set(c.T)
    out = pl.pallas_call(
        _mod_kernel,
        grid=(N_MOD,),
        in_specs=[
            pl.BlockSpec((D_MODEL, 8), lambda j: (0, 0)),
            pl.BlockSpec((D_MODEL, tn), lambda j: (0, j)),
            pl.BlockSpec((1, tn), lambda j: (0, j)),
        ],
        out_specs=pl.BlockSpec((8, tn), lambda j: (0, j)),
        out_shape=jax.ShapeDtypeStruct((8, N_MOD * D_MODEL), F32),
        compiler_params=_cparams(("arbitrary",)),
        name="mod",
    )(ct, ada_w, ada_b.reshape(1, N_MOD * D_MODEL))
    return out[:BATCH].reshape(BATCH, N_MOD, D_MODEL)


def _wprep_kernel(w_ref, o_ref):
    j = pl.program_id(0)
    t = w_ref[...].T
    lane = lax.broadcasted_iota(I32, t.shape, 1)
    keep = jnp.logical_or(lane < GLA_GATE_RANK, j < WPREP_LR_STEP)
    o_ref[...] = jnp.where(keep, t, 0.0).astype(BF16)


def _wprep_call(w_in0):
    w_t = w_in0.T
    lr0 = 2 * GLA_QK_W + 2 * GLA_V_W
    assert lr0 % WPREP_COLS == 0 and (D_IN_PAD - LR_PAD) % WPREP_COLS == 0

    def src_row(j):
        return jnp.where(j < lr0 // WPREP_COLS, j * WPREP_COLS,
                         jnp.where(j < WPREP_LR_STEP, j * WPREP_COLS + GLA_GATE_RANK, lr0))

    return pl.pallas_call(
        _wprep_kernel,
        grid=(WPREP_LR_STEP + 1,),
        in_specs=[pl.BlockSpec((pl.Element(WPREP_COLS), pl.Element(D_MODEL)),
                               lambda j: (pl.multiple_of(src_row(j), SUBLANES), 0))],
        out_specs=pl.BlockSpec((D_MODEL, WPREP_COLS), lambda j: (0, j)),
        out_shape=jax.ShapeDtypeStruct((D_MODEL, D_IN_ALLOC), BF16),
        compiler_params=_cparams(("parallel",)),
        name="w_prep",
    )(w_t)


def _in_kernel(x_ref, mod_ref, g1_ref, w_ref, bd_ref, qg_ref, kg_ref, pos_ref, invf_ref,
               gqk_ref, gv_ref, gr_ref, lr_ref, qT_ref, k_ref, vT_ref, sa_ref, sb_ref,
               cos_scr, sin_scr):
    ang = pos_ref[...].astype(F32) * invf_ref[...]
    cos_scr[...] = jnp.cos(ang)
    sin_scr[...] = jnp.sin(ang)

    x = x_ref[...]
    shift1 = mod_ref[0, 0:1, :]
    scale1 = mod_ref[0, 1:2, :]
    ms = jnp.mean(x * x, axis=-1, keepdims=True)
    h = (x * lax.rsqrt(ms + NORM_EPS) * g1_ref[...]) * (1.0 + scale1) + shift1
    hb = h.astype(BF16)

    def proj(c0, c1):
        return jnp.dot(hb, w_ref[:, c0:c1], preferred_element_type=F32)

    tm = x.shape[0]
    cos4 = jnp.concatenate([cos_scr[...]] * DIFF_HEADS, axis=1)
    sin4 = jnp.concatenate([sin_scr[...]] * DIFF_HEADS, axis=1)
    lane = lax.broadcasted_iota(I32, (tm, DIFF_QK_W), 1)
    first_half = (lane % DIFF_DH) < ROPE_HALF
    bd = bd_ref[...]

    def norm_rope(t, gain_row):
        t2 = t * t
        hi = t2.astype(BF16)
        lo = (t2 - hi.astype(F32)).astype(BF16)
        gms = (jnp.dot(hi, bd, preferred_element_type=F32)
               + jnp.dot(lo, bd, preferred_element_type=F32))
        t = t * lax.rsqrt(gms + NORM_EPS) * gain_row
        nxt = pltpu.roll(t, DIFF_QK_W - ROPE_HALF, 1)
        prv = pltpu.roll(t, ROPE_HALF, 1)
        return t * cos4 + jnp.where(first_half, -nxt, prv) * sin4

    dq_raw = proj(1536, 2048)
    dk_raw = proj(2048, 2560)
    gqk_ref[...] = proj(0, 512).astype(BF16)
    dq = norm_rope(dq_raw, qg_ref[...]) * Q_SCALE
    gv_ref[...] = proj(512, 1024).astype(BF16)
    dk = norm_rope(dk_raw, kg_ref[...])
    dv = proj(2560, 3072)
    for hd in range(DIFF_HEADS):
        sl = slice(hd * LANES, (hd + 1) * LANES)
        qT_ref[0, hd, 0] = dq[:, sl].T.astype(BF16)
        k_ref[0, hd, 0] = dk[:, sl].astype(BF16)
    gr_ref[...] = proj(1024, 1536).astype(BF16)
    for hd in range(DIFF_HEADS):
        sl = slice(hd * LANES, (hd + 1) * LANES)
        vT_ref[0, hd, 0] = jnp.concatenate(
            [dv[:, sl].T, jnp.ones((V_ROWS - DIFF_DV, tm), F32)], axis=0).astype(BF16)
    sa_ref[...] = _sigmoid(proj(3072, 4096)).astype(BF16)
    sb_ref[...] = _sigmoid(proj(4096, 5120)).astype(BF16)
    lr_ref[...] = proj(5120, 5248)


def _in_call(x2, mod, norm1_g, w_in_p, bd, qg_row, kg_row, pos_col, invf_row):
    nb = SEQ // TM_IN
    tok_spec = lambda w: pl.BlockSpec((TM_IN, w), lambda i: (i, 0))
    const2 = lambda r, c: pl.BlockSpec((r, c), lambda i: (0, 0))
    out_shapes = (
        jax.ShapeDtypeStruct((TOKENS, 2 * GLA_QK_W), BF16),
        jax.ShapeDtypeStruct((TOKENS, GLA_V_W), BF16),
        jax.ShapeDtypeStruct((TOKENS, GLA_V_W), BF16),
        jax.ShapeDtypeStruct((TOKENS, LR_PAD), F32),
        jax.ShapeDtypeStruct((BATCH, DIFF_HEADS, N_KV, LANES, TQ), BF16),
        jax.ShapeDtypeStruct((BATCH, DIFF_HEADS, N_KV, TQ, LANES), BF16),
        jax.ShapeDtypeStruct((BATCH, DIFF_HEADS, N_KV, V_ROWS, TQ), BF16),
        jax.ShapeDtypeStruct((TOKENS, D_MODEL), BF16),
        jax.ShapeDtypeStruct((TOKENS, D_MODEL), BF16),
    )
    out_specs = (
        tok_spec(2 * GLA_QK_W), tok_spec(GLA_V_W), tok_spec(GLA_V_W), tok_spec(LR_PAD),
        pl.BlockSpec((1, DIFF_HEADS, 1, LANES, TM_IN), lambda i: (i // nb, 0, i % nb, 0, 0)),
        pl.BlockSpec((1, DIFF_HEADS, 1, TM_IN, LANES), lambda i: (i // nb, 0, i % nb, 0, 0)),
        pl.BlockSpec((1, DIFF_HEADS, 1, V_ROWS, TM_IN), lambda i: (i // nb, 0, i % nb, 0, 0)),
        tok_spec(D_MODEL), tok_spec(D_MODEL),
    )
    return pl.pallas_call(
        _in_kernel,
        grid=(TOKENS // TM_IN,),
        in_specs=[
            tok_spec(D_MODEL),
            pl.BlockSpec((1, N_MOD, D_MODEL), lambda i: (i // nb, 0, 0)),
            const2(1, D_MODEL),
            pl.BlockSpec((D_MODEL, D_IN_ALLOC), lambda i: (0, 0), pipeline_mode=pl.Buffered(1)),
            const2(DIFF_QK_W, DIFF_QK_W),
            const2(1, DIFF_QK_W), const2(1, DIFF_QK_W),
            tok_spec(1),
            const2(1, LANES),
        ],
        out_specs=out_specs,
        out_shape=out_shapes,
        scratch_shapes=[pltpu.VMEM((TM_IN, LANES), F32), pltpu.VMEM((TM_IN, LANES), F32)],
        compiler_params=_cparams(("parallel",)),
        name="in_proj",
    )(x2, mod, norm1_g, w_in_p, bd, qg_row, kg_row, pos_col, invf_row)


def _gla_kernel(qk_ref, v_ref, r_ref, lr_ref, auh_ref, aul_ref, ab_ref, og_ref, o_ref,
                state_ref, oacc_ref, snew_ref):
    tt = qk_ref.shape[0]
    n_chunks = tt // GLA_CHUNK

    @pl.when(pl.program_id(1) == 0)
    def _():
        state_ref[...] = jnp.zeros_like(state_ref)

    lr = lr_ref[...]
    lr_hi = lr.astype(BF16)
    lr_lo = (lr - lr_hi.astype(F32)).astype(BF16)
    z = (jnp.dot(lr_hi, auh_ref[...], preferred_element_type=F32)
         + jnp.dot(lr_lo, auh_ref[...], preferred_element_type=F32)
         + jnp.dot(lr_hi, aul_ref[...], preferred_element_type=F32)) + ab_ref[...]
    g = (jnp.minimum(z, 0.0) - jnp.log(1.0 + jnp.exp(-jnp.abs(z)))) * (1.0 / GLA_GATE_TAU)

    row = lax.broadcasted_iota(I32, (tt, GLA_QK_W), 0) % GLA_CHUNK
    b = g
    step = 1
    while step < GLA_CHUNK:
        b = b + jnp.where(row >= step, pltpu.roll(b, step, 0), 0.0)
        step *= 2

    b_last_rows = [b[c * GLA_CHUNK + GLA_CHUNK - 1:(c + 1) * GLA_CHUNK, :] for c in range(n_chunks)]
    b_last = jnp.concatenate(
        [jnp.broadcast_to(bl, (GLA_CHUNK, GLA_QK_W)) for bl in b_last_rows], axis=0)

    qk = qk_ref[...].astype(F32)
    q = qk[:, :GLA_QK_W] * (GLA_DK ** -0.5)
    k = qk[:, GLA_QK_W:]
    q_in = (q * jnp.exp(b)).astype(BF16)
    k_in = (k * jnp.exp(-b)).astype(BF16)
    k_dec = (k * jnp.exp(b_last - b)).astype(BF16)

    ci = lax.broadcasted_iota(I32, (GLA_CHUNK, GLA_CHUNK), 0)
    cj = lax.broadcasted_iota(I32, (GLA_CHUNK, GLA_CHUNK), 1)
    causal = ci >= cj

    pairs = [(c, hd) for c in range(n_chunks) for hd in range(GLA_HEADS)]

    def rows(c):
        return slice(c * GLA_CHUNK, (c + 1) * GLA_CHUNK)

    def kcols(hd):
        return slice(hd * GLA_DK, (hd + 1) * GLA_DK)

    def vcols(hd):
        return slice(hd * GLA_DV, (hd + 1) * GLA_DV)

    att = {}
    for c, hd in pairs:
        a = lax.dot_general(q_in[rows(c), kcols(hd)], k_in[rows(c), kcols(hd)],
                            (((1,), (1,)), ((), ())), preferred_element_type=F32)
        att[c, hd] = jnp.where(causal, a, 0.0).astype(BF16)
    for c, hd in pairs:
        oacc_ref[rows(c), vcols(hd)] = jnp.dot(att[c, hd], v_ref[rows(c), vcols(hd)],
                                               preferred_element_type=F32)
    for c, hd in pairs:
        snew_ref[c * GLA_HEADS + hd] = lax.dot_general(
            k_dec[rows(c), kcols(hd)], v_ref[rows(c), vcols(hd)],
            (((0,), (0,)), ((), ())), preferred_element_type=F32)

    decay_rows = jnp.exp(jnp.concatenate(
        b_last_rows + [jnp.zeros((LANES - n_chunks, GLA_QK_W), F32)], axis=0))
    decay_cols = decay_rows.T
    states = [state_ref[hd] for hd in range(GLA_HEADS)]
    for c in range(n_chunks):
        for hd in range(GLA_HEADS):
            s_prev = states[hd]
            oacc_ref[rows(c), vcols(hd)] += jnp.dot(
                q_in[rows(c), kcols(hd)], s_prev.astype(BF16), preferred_element_type=F32)
            dcol = decay_cols[kcols(hd), c:c + 1]
            states[hd] = s_prev * dcol + snew_ref[c * GLA_HEADS + hd]
    for hd in range(GLA_HEADS):
        state_ref[hd] = states[hd]

    for hd in range(GLA_HEADS):
        vs = slice(hd * GLA_DV, (hd + 1) * GLA_DV)
        oh = oacc_ref[:, vs]
        ms = jnp.mean(oh * oh, axis=-1, keepdims=True)
        y = oh * lax.rsqrt(ms + NORM_EPS) * og_ref[...]
        r = r_ref[:, vs].astype(F32)
        o_ref[:, vs] = (y * (r * _sigmoid(r))).astype(BF16)


def _gla_call(gqk, gv, gr, lr, au_hi, au_lo, ab_row, og_row):
    nt = SEQ // TT_GLA
    tok = lambda w: pl.BlockSpec((TT_GLA, w), lambda b, t: (b * nt + t, 0))
    const2 = lambda r, c: pl.BlockSpec((r, c), lambda b, t: (0, 0))
    return pl.pallas_call(
        _gla_kernel,
        grid=(BATCH, nt),
        in_specs=[tok(2 * GLA_QK_W), tok(GLA_V_W), tok(GLA_V_W), tok(LR_PAD),
                  const2(LR_PAD, GLA_QK_W), const2(LR_PAD, GLA_QK_W), const2(1, GLA_QK_W),
                  const2(1, GLA_DV)],
        out_specs=tok(GLA_V_W),
        out_shape=jax.ShapeDtypeStruct((TOKENS, GLA_V_W), BF16),
        scratch_shapes=[pltpu.VMEM((GLA_HEADS, GLA_DK, GLA_DV), F32),
                        pltpu.VMEM((TT_GLA, GLA_V_W), F32),
                        pltpu.VMEM((TT_GLA // GLA_CHUNK * GLA_HEADS, GLA_DK, GLA_DV), F32)],
        compiler_params=_cparams(("parallel", "arbitrary")),
        name="gla",
    )(gqk, gv, gr, lr, au_hi, au_lo, ab_row, og_row)


def _attn_kernel(qT_ref, k_ref, vT_ref, lam_ref, og_ref, o_ref,
                 q_scr, s_scr, p_scr, acc_scr, m_scr):
    _attn_load_q(qT_ref, q_scr, 0, 0)
    s_scr[0] = jnp.dot(k_ref[0, 0, 0], q_scr[0, 0], preferred_element_type=F32)

    def q_block(i, carry):
        _attn_q_block(i, qT_ref, k_ref, vT_ref, lam_ref, og_ref, o_ref,
                      q_scr, s_scr, p_scr, acc_scr, m_scr)
        return carry

    lax.fori_loop(0, N_KV, q_block, 0)


def _attn_load_q(qT_ref, q_scr, i, slot):
    qT = qT_ref[0, 0, i]
    rowq = lax.broadcasted_iota(I32, qT.shape, 0)
    zero = jnp.zeros_like(qT)
    q_scr[slot, 0] = jnp.where(rowq < DIFF_DH, qT, zero)
    q_scr[slot, 1] = jnp.where(rowq >= DIFF_DH, qT, zero)


def _attn_q_block(i, qT_ref, k_ref, vT_ref, lam_ref, og_ref, o_ref,
                  q_scr, s_scr, p_scr, acc_scr, m_scr):
    slot = i % 2
    m_scr[...] = jnp.full(m_scr.shape, NEG_INF, F32)
    acc_scr[...] = jnp.zeros(acc_scr.shape, F32)
    n_sub = TQ // ATT_SUB

    def fold8(t, op):
        return op(t.reshape(t.shape[0] // SUBLANES, SUBLANES, TQ), axis=0)

    def scores(c, j):
        s_scr[c] = jnp.dot(k_ref[0, 0, j], q_scr[slot, c], preferred_element_type=F32)

    def load_s(c, r, masked):
        s = s_scr[c, r * ATT_SUB:(r + 1) * ATT_SUB, :]
        if masked:
            key_i = lax.broadcasted_iota(I32, (ATT_SUB, TQ), 0) + r * ATT_SUB
            qry_i = lax.broadcasted_iota(I32, (ATT_SUB, TQ), 1)
            s = jnp.where(key_i <= qry_i, s, NEG_INF)
        return s

    def softmax_pv(c, j, masked):
        m8 = fold8(load_s(c, 0, masked), jnp.max)
        for r in range(1, n_sub):
            m8 = jnp.maximum(m8, fold8(load_s(c, r, masked), jnp.max))
        m_old = m_scr[c]
        m_new = jnp.maximum(m_old, jnp.max(m8, axis=0, keepdims=True))
        alpha = jnp.exp2(m_old - m_new)
        for r in range(n_sub):
            p = jnp.exp2(load_s(c, r, masked) - m_new)
            p_scr[c, r * ATT_SUB:(r + 1) * ATT_SUB, :] = p.astype(BF16)
        m_scr[c] = m_new
        acc_scr[c] = acc_scr[c] * alpha + jnp.dot(vT_ref[0, 0, j], p_scr[c],
                                                  preferred_element_type=F32)

    def body(j, carry):
        scores(1, j)
        softmax_pv(0, j, False)
        scores(0, j + 1)
        softmax_pv(1, j, False)
        return carry

    lax.fori_loop(0, i, body, 0)
    scores(1, i)
    softmax_pv(0, i, True)
    softmax_pv(1, i, True)

    nxt = jnp.minimum(i + 1, N_KV - 1)
    _attn_load_q(qT_ref, q_scr, nxt, 1 - slot)
    s_scr[0] = jnp.dot(k_ref[0, 0, 0], q_scr[1 - slot, 0], preferred_element_type=F32)

    l1 = acc_scr[0, DIFF_DV:DIFF_DV + 1, :]
    l2 = acc_scr[1, DIFF_DV:DIFF_DV + 1, :]

    lam_p = lam_ref[...]
    lam = (jnp.exp(jnp.sum(lam_p[0:1] * lam_p[1:2], axis=1, keepdims=True))
           - jnp.exp(jnp.sum(lam_p[2:3] * lam_p[3:4], axis=1, keepdims=True)) + LAMBDA_INIT)
    oT = acc_scr[0, :DIFF_DV, :] / l1 - lam * (acc_scr[1, :DIFF_DV, :] / l2)
    ms = jnp.mean(oT * oT, axis=0, keepdims=True)
    y = oT * lax.rsqrt(ms + NORM_EPS) * og_ref[...] * (1.0 - LAMBDA_INIT)
    o_ref[pl.ds(pl.multiple_of(i * TQ, TQ), TQ), :] = y.T.astype(BF16)


def _attn_call(qT, kk, vT, lam_p, og_col):
    return pl.pallas_call(
        _attn_kernel,
        grid=(BATCH, DIFF_HEADS),
        in_specs=[
            pl.BlockSpec((1, 1, N_KV, LANES, TQ), lambda b, h: (b, h, 0, 0, 0)),
            pl.BlockSpec((1, 1, N_KV, TQ, LANES), lambda b, h: (b, h, 0, 0, 0)),
            pl.BlockSpec((1, 1, N_KV, V_ROWS, TQ), lambda b, h: (b, h, 0, 0, 0)),
            pl.BlockSpec((4, DIFF_DH), lambda b, h: (0, 0)),
            pl.BlockSpec((DIFF_DV, 1), lambda b, h: (0, 0)),
        ],
        out_specs=pl.BlockSpec((SEQ, DIFF_DV), lambda b, h: (b, h)),
        out_shape=jax.ShapeDtypeStruct((TOKENS, DIFF_V_W), BF16),
        scratch_shapes=[pltpu.VMEM((2, 2, LANES, TQ), BF16),
                        pltpu.VMEM((2, TQ, TQ), F32),
                        pltpu.VMEM((2, TQ, TQ), BF16),
                        pltpu.VMEM((2, V_ROWS, TQ), F32),
                        pltpu.VMEM((2, 1, TQ), F32)],
        compiler_params=_cparams(("parallel", "parallel")),
        name="attn",
    )(qT, kk, vT, lam_p, og_col)


def _merge_kernel(x_ref, oa_ref, ob_ref, sa_ref, sb_ref, mod_ref, g2_ref, wa_ref, wb_ref, wo_ref,
                  wrh_ref, wrl_ref, br_ref, x1_ref, h2_ref, rf_ref, rank_ref, cnt_ref,
                  carry_ref, tri_ref, lg_ref):
    tm = x_ref.shape[0]
    step = pl.program_id(0)

    @pl.when(step == 0)
    def _():
        carry_ref[...] = jnp.zeros_like(carry_ref)
        lg_ref[...] = jnp.zeros_like(lg_ref)
        ti = lax.broadcasted_iota(I32, (tm, tm), 0)
        tj = lax.broadcasted_iota(I32, (tm, tm), 1)
        tri_ref[...] = jnp.where(ti > tj, 1.0, 0.0).astype(BF16)

    ma = jnp.dot(oa_ref[...], wa_ref[...], preferred_element_type=F32)
    mb = jnp.dot(ob_ref[...], wb_ref[...], preferred_element_type=F32)
    merged = sa_ref[...].astype(F32) * ma + sb_ref[...].astype(F32) * mb
    y = jnp.dot(merged.astype(BF16), wo_ref[...], preferred_element_type=F32)
    logits = lg_ref[...]
    lane = lax.broadcasted_iota(I32, (tm, LANES), 1).astype(F32)
    ninf = -jnp.inf
    big = float(LANES)

    def first_argmax(v):
        vmax = jnp.max(v, axis=1, keepdims=True)
        idx = jnp.min(jnp.where(v == vmax, lane, big), axis=1, keepdims=True)
        return vmax, idx

    gl = jnp.where(lane < N_GROUPS, logits, ninf)
    gmax, gidx = first_argmax(gl)
    p_top = 1.0 / jnp.sum(jnp.exp(gl - gmax), axis=1, keepdims=True)
    lo = N_GROUPS + EXPERTS_PER_GROUP * gidx
    el = jnp.where((lane >= lo) & (lane < lo + EXPERTS_PER_GROUP), logits, ninf)
    e1max, e1 = first_argmax(el)
    e2max, e2 = first_argmax(jnp.where(lane == e1, ninf, el))
    t = jnp.exp(e2max - e1max)
    w1 = 1.0 / (1.0 + t)
    w2 = t / (1.0 + t)
    col = lax.broadcasted_iota(I32, (tm, ROUTE_W), 1)
    rf_ref[...] = jnp.where(col == 0, p_top * w1, jnp.where(col == 1, p_top * w2, 0.0))

    x1id = e1 - N_GROUPS
    x2id = e2 - N_GROUPS
    hit1 = lane == x1id
    hit2 = lane == x2id
    onehot = jnp.where(hit1 | hit2, 1.0, 0.0)
    before = jnp.dot(tri_ref[...], onehot.astype(BF16), preferred_element_type=F32) + carry_ref[...]
    r1 = jnp.sum(jnp.where(hit1, before, 0.0), axis=1, keepdims=True)
    r2 = jnp.sum(jnp.where(hit2, before, 0.0), axis=1, keepdims=True)
    real_tile = jnp.where(step > 0, 1.0, 0.0)
    carry_ref[...] = carry_ref[...] + real_tile * jnp.sum(onehot, axis=0, keepdims=True)
    cnt_ref[...] = carry_ref[...]
    cols = jnp.where(lane == 0.0, x1id, jnp.where(lane == 1.0, x2id,
                                                  jnp.where(lane == 2.0, r1,
                                                            jnp.where(lane == 3.0, r2, 0.0))))
    rank_ref[0] = cols.T[0:ROUTE_W, :].astype(I32)

    gate1 = mod_ref[0, 2:3, :]
    shift2 = mod_ref[0, 3:4, :]
    scale2 = mod_ref[0, 4:5, :]
    x1 = x_ref[...] + gate1 * y
    x1_ref[...] = x1
    ms = jnp.mean(x1 * x1, axis=-1, keepdims=True)
    h2 = (x1 * lax.rsqrt(ms + NORM_EPS) * g2_ref[...]) * (1.0 + scale2) + shift2
    _store_row_tiles(h2_ref, h2)
    h2_hi = h2.astype(BF16)
    h2_lo = (h2 - h2_hi.astype(F32)).astype(BF16)
    next_logits = (jnp.dot(h2_hi, wrh_ref[...], preferred_element_type=F32)
                   + jnp.dot(h2_lo, wrh_ref[...], preferred_element_type=F32)
                   + jnp.dot(h2_hi, wrl_ref[...], preferred_element_type=F32)) + br_ref[...]

    lg_ref[...] = next_logits


def _merge_call(x2, o_a, o_b, sa, sb, mod, norm2_g, wa, wb, wo, wr_hi, wr_lo, br):
    nb = SEQ // TM_MERGE
    n_tiles = TOKENS // TM_MERGE
    cur = lambda i: jnp.minimum(i, n_tiles - 1)
    prev = lambda i: jnp.maximum(i - 1, 0)
    tok = lambda w: pl.BlockSpec((TM_MERGE, w), lambda i: (cur(i), 0))
    const2 = lambda r, c: pl.BlockSpec((r, c), lambda i: (0, 0))
    return pl.pallas_call(
        _merge_kernel,
        grid=(n_tiles + 1,),
        in_specs=[tok(D_MODEL), tok(GLA_V_W), tok(DIFF_V_W), tok(D_MODEL), tok(D_MODEL),
                  pl.BlockSpec((1, N_MOD, D_MODEL), lambda i: (cur(i) // nb, 0, 0)),
                  const2(1, D_MODEL),
                  const2(GLA_V_W, D_MODEL), const2(DIFF_V_W, D_MODEL), const2(D_MODEL, D_MODEL),
                  const2(D_MODEL, LANES), const2(D_MODEL, LANES), const2(1, LANES)],
        out_specs=(tok(D_MODEL), _row_tile_spec(TM_MERGE, lambda i: (cur(i), 0)),
                   pl.BlockSpec((TM_MERGE, ROUTE_W), lambda i: (prev(i), 0)),
                   pl.BlockSpec((1, ROUTE_W, TM_MERGE), lambda i: (prev(i), 0, 0)),
                   pl.BlockSpec((1, LANES), lambda i: (0, 0))),
        out_shape=(jax.ShapeDtypeStruct((TOKENS, D_MODEL), F32),
                   jax.ShapeDtypeStruct((TOKENS * ROW_TILE_S, LANES), F32),
                   jax.ShapeDtypeStruct((TOKENS, ROUTE_W), F32),
                   jax.ShapeDtypeStruct((n_tiles, ROUTE_W, TM_MERGE), I32),
                   jax.ShapeDtypeStruct((1, LANES), F32)),
        scratch_shapes=[pltpu.VMEM((1, LANES), F32), pltpu.VMEM((TM_MERGE, TM_MERGE), BF16),
                        pltpu.VMEM((TM_MERGE, LANES), F32)],
        compiler_params=_cparams(("arbitrary",)),
        name="merge",
    )(x2, o_a, o_b, sa, sb, mod, norm2_g, wa, wb, wo, wr_hi, wr_lo, br)


def _plan_kernel(rk_ref, cnt_ref, dest_ref, blk_ref):
    n_tiles, _, tm = rk_ref.shape
    cnt = cnt_ref[...]
    padded = jnp.floor((cnt + (MOE_BLK - 1)) * (1.0 / MOE_BLK)) * MOE_BLK
    lane = lax.broadcasted_iota(I32, (1, LANES), 1)
    seg_end = padded
    step = 1
    while step < N_EXPERTS:
        seg_end = seg_end + jnp.where(lane >= step, pltpu.roll(seg_end, step, 1), 0.0)
        step *= 2
    seg_start = seg_end - padded
    valid_end = seg_start + cnt

    ei = lax.broadcasted_iota(I32, (LANES, LANES), 0)
    ej = lax.broadcasted_iota(I32, (LANES, LANES), 1)

    def to_col(rowv):
        return jnp.sum(jnp.where(ei == ej, rowv, 0.0), axis=1, keepdims=True)

    start_col, end_col, valid_col = to_col(seg_start), to_col(seg_end), to_col(valid_end)

    nb_pad = blk_ref.shape[1]
    e_sub = lax.broadcasted_iota(I32, (LANES, nb_pad), 0)
    b_start = (lax.broadcasted_iota(I32, (1, nb_pad), 1) * MOE_BLK).astype(F32)
    ends_before = jnp.where((e_sub < N_EXPERTS) & (end_col <= b_start), 1.0, 0.0)
    block_e = jnp.minimum(jnp.sum(ends_before, axis=0, keepdims=True), N_EXPERTS - 1.0)
    block_valid_end = jnp.sum(jnp.where(e_sub.astype(F32) == block_e, valid_col, 0.0),
                              axis=0, keepdims=True)
    n_valid = jnp.clip(block_valid_end - b_start, 0.0, float(MOE_BLK))
    n_used = jnp.max(seg_end, axis=1, keepdims=True) * (1.0 / MOE_BLK)
    nonempty = (to_col(cnt) > 0.0) & (e_sub < N_EXPERTS)
    e_subf = e_sub.astype(F32)
    no_next = float(LANES)
    nxt = jnp.min(jnp.where(nonempty & (e_subf > block_e), e_subf, no_next), axis=0, keepdims=True)
    nxt = jnp.where(nxt == no_next, -1.0, nxt)
    seg_idx = jnp.sum(jnp.where(nonempty & (e_subf < block_e), 1.0, 0.0), axis=0, keepdims=True)
    slot = seg_idx - 2.0 * jnp.floor(seg_idx * 0.5)
    blk_ref[...] = jnp.concatenate(
        [block_e, n_valid, jnp.broadcast_to(n_used, (1, nb_pad)), nxt, slot,
         jnp.zeros((ROUTE_W - 5, nb_pad), F32)], axis=0).astype(I32)

    e_tok = lax.broadcasted_iota(I32, (LANES, tm), 0)

    def tile(t, carry):
        rk = rk_ref[t]
        d = [jnp.sum(jnp.where(e_tok == rk[k:k + 1], start_col, 0.0), axis=0, keepdims=True)
             + rk[k + 2:k + 3].astype(F32) for k in range(TOP_K)]
        dest_ref[t] = jnp.concatenate(d + [jnp.zeros((ROUTE_W - TOP_K, tm), F32)], axis=0).astype(I32)
        return carry

    lax.fori_loop(0, n_tiles, tile, 0)


def _plan_call(rank_rows, cnt):
    n_tiles = TOKENS // TM_MERGE
    nb_pad = -(-N_BLOCKS // LANES) * LANES
    dest, blk = pl.pallas_call(
        _plan_kernel,
        out_shape=(jax.ShapeDtypeStruct((n_tiles, ROUTE_W, TM_MERGE), I32),
                   jax.ShapeDtypeStruct((ROUTE_W, nb_pad), I32)),
        compiler_params=pltpu.CompilerParams(vmem_limit_bytes=VMEM_LIMIT),
        name="plan",
    )(rank_rows, cnt)
    dest1 = dest[:, 0, :].reshape(TOKENS)
    dest2 = dest[:, 1, :].reshape(TOKENS)
    tables = tuple(blk[r, :N_BLOCKS] for r in (0, 1, 3, 4))
    return dest1, dest2, tables, blk[2, :1]


def _sc_params():
    return pltpu.CompilerParams(use_tc_tiling_on_sc=True)


def _sc_mesh():
    return plsc.VectorSubcoreMesh(core_axis_name="core", subcore_axis_name="subcore")


def _sc_worker_base(per_worker):
    wid = lax.axis_index("subcore") * SC_CORES + lax.axis_index("core")
    return wid * per_worker


def _sc_scatter_rows(src, dest1, dest2, n_out):
    n = src.shape[0]
    per_worker = n // SC_WORKERS
    assert per_worker * SC_WORKERS == n and per_worker % SC_CHUNK == 0

    def body(src_hbm, d1_hbm, d2_hbm, out_hbm, idx_v, rows_v):
        base = _sc_worker_base(per_worker)

        @pl.loop(0, per_worker // SC_CHUNK)
        def _(j):
            start = pl.multiple_of(base + j * SC_CHUNK, SC_CHUNK)
            pltpu.sync_copy(src_hbm.at[pl.ds(start, SC_CHUNK)], rows_v)
            for d_hbm in (d1_hbm, d2_hbm):
                pltpu.sync_copy(d_hbm.at[pl.ds(start, SC_CHUNK)], idx_v)
                pltpu.sync_copy(rows_v, out_hbm.at[idx_v])

    return pl.kernel(
        body,
        out_type=jax.ShapeDtypeStruct((n_out, ROW_TILE_S, LANES), F32),
        mesh=_sc_mesh(),
        scratch_types=[pltpu.VMEM((SC_CHUNK,), I32),
                       pltpu.VMEM((SC_CHUNK, ROW_TILE_S, LANES), F32)],
        compiler_params=_sc_params(),
        name="sc_dispatch",
    )(src, dest1, dest2)


def _sc_gather_rows(table, idx):
    n = idx.shape[0]
    per_worker = n // SC_WORKERS
    half = SC_CHUNK // 2
    assert per_worker * SC_WORKERS == n and per_worker % SC_CHUNK == 0

    def body(table_hbm, idx_hbm, out_hbm, idx_v, rows_a, rows_b, gsem_a, gsem_b, wsem_a, wsem_b):
        base = pl.multiple_of(_sc_worker_base(per_worker), SC_CHUNK)
        pltpu.sync_copy(idx_hbm.at[pl.ds(base, per_worker)], idx_v)

        @pl.loop(0, per_worker // SC_CHUNK)
        def _(j):
            off_a = pl.multiple_of(j * SC_CHUNK, SC_CHUNK)
            off_b = pl.multiple_of(j * SC_CHUNK + half, half)
            ga = pltpu.async_copy(table_hbm.at[idx_v.at[pl.ds(off_a, half)]], rows_a, gsem_a)
            gb = pltpu.async_copy(table_hbm.at[idx_v.at[pl.ds(off_b, half)]], rows_b, gsem_b)
            ga.wait()
            wa = pltpu.async_copy(rows_a, out_hbm.at[pl.ds(base + off_a, half)], wsem_a)
            gb.wait()
            wb = pltpu.async_copy(rows_b, out_hbm.at[pl.ds(base + off_b, half)], wsem_b)
            wa.wait()
            wb.wait()

    return pl.kernel(
        body,
        out_type=jax.ShapeDtypeStruct((n, ROW_TILE_S, LANES), F32),
        mesh=_sc_mesh(),
        scratch_types=[pltpu.VMEM((per_worker,), I32),
                       pltpu.VMEM((half, ROW_TILE_S, LANES), F32),
                       pltpu.VMEM((half, ROW_TILE_S, LANES), F32),
                       pltpu.SemaphoreType.DMA, pltpu.SemaphoreType.DMA,
                       pltpu.SemaphoreType.DMA, pltpu.SemaphoreType.DMA],
        compiler_params=_sc_params(),
        name="sc_gather",
    )(table, idx)


def _weight_fetch(w_hbm, stage, sem, e, slot):
    return [pltpu.make_async_copy(w.at[e], st.at[slot], sem.at[slot]) for w, st in zip(w_hbm, stage)]


def _expert_kernel(be_ref, nv_ref, nx_ref, sl_ref, nu_ref, xs_ref, w1_hbm, w3_hbm, w2_hbm, y_ref,
                   w1s, w3s, w2s, w1b, w3b, w2b, sem):
    i = pl.program_id(0)
    e = be_ref[i]
    used = i < nu_ref[0]
    first = (i == 0) | (e != be_ref[jnp.maximum(i - 1, 0)])
    w_hbm = (w1_hbm, w3_hbm, w2_hbm)
    stage = (w1s, w3s, w2s)

    @pl.when(used & first)
    def _():
        slot = sl_ref[i]

        @pl.when(i == 0)
        def _():
            for cp in _weight_fetch(w_hbm, stage, sem, e, slot):
                cp.start()

        for cp in _weight_fetch(w_hbm, stage, sem, e, slot):
            cp.wait()
        w1b[...] = w1s[slot].astype(BF16)
        w3b[...] = w3s[slot].astype(BF16)
        w2b[...] = w2s[slot].astype(BF16)

        @pl.when(nx_ref[i] >= 0)
        def _():
            for cp in _weight_fetch(w_hbm, stage, sem, nx_ref[i], 1 - slot):
                cp.start()

    n_parts = MOE_BLK // MOE_SUB

    def load_x(part):
        row0 = part * MOE_SUB
        row = lax.broadcasted_iota(I32, (MOE_SUB, D_MODEL), 0) + row0
        return jnp.where(row < nv_ref[i], _load_row_tiles(xs_ref, row0, MOE_SUB), 0.0).astype(BF16)

    def zero_y(part):
        y_ref[pl.ds(part * MOE_SUB * ROW_TILE_S, MOE_SUB * ROW_TILE_S), :] = jnp.zeros(
            (MOE_SUB * ROW_TILE_S, LANES), F32)

    def run(parts):
        hmid = []
        for part in parts:
            xb = load_x(part)
            a = jnp.dot(xb, w1b[...], preferred_element_type=F32)
            g = jnp.dot(xb, w3b[...], preferred_element_type=F32)
            hmid.append(((a * _sigmoid(a)) * g).astype(BF16))
        for part, hm in zip(parts, hmid):
            _store_row_tiles(y_ref, jnp.dot(hm, w2b[...], preferred_element_type=F32),
                             part * MOE_SUB)

    n_live = jnp.where(used, (nv_ref[i] + (MOE_SUB - 1)) // MOE_SUB, 0)
    for k in range(n_parts + 1):
        @pl.when(n_live == k)
        def _(k=k):
            if k:
                run(list(range(k)))
            for part in range(k, n_parts):
                zero_y(part)


def _expert_call(tables, n_used, xs, w1, w3, w2):
    n_tab = len(tables)
    grid_spec = pltpu.PrefetchScalarGridSpec(
        num_scalar_prefetch=n_tab + 1,
        grid=(N_BLOCKS,),
        in_specs=[_row_tile_spec(MOE_BLK, lambda i, *pf: (jnp.minimum(i, pf[n_tab][0] - 1), 0)),
                  pl.BlockSpec(memory_space=pl.ANY),
                  pl.BlockSpec(memory_space=pl.ANY),
                  pl.BlockSpec(memory_space=pl.ANY)],
        out_specs=_row_tile_spec(MOE_BLK, lambda i, *pf: (i, 0)),
        scratch_shapes=[pltpu.VMEM((2, D_MODEL, D_EXPERT), F32),
                        pltpu.VMEM((2, D_MODEL, D_EXPERT), F32),
                        pltpu.VMEM((2, D_EXPERT, D_MODEL), F32),
                        pltpu.VMEM((D_MODEL, D_EXPERT), BF16),
                        pltpu.VMEM((D_MODEL, D_EXPERT), BF16),
                        pltpu.VMEM((D_EXPERT, D_MODEL), BF16),
                        pltpu.SemaphoreType.DMA((2,))],
    )
    return pl.pallas_call(
        _expert_kernel,
        grid_spec=grid_spec,
        out_shape=jax.ShapeDtypeStruct((N_BLOCKS * MOE_BLK * ROW_TILE_S, LANES), F32),
        compiler_params=_cparams(("arbitrary",)),
        name="experts",
    )(*tables, n_used, xs, w1, w3, w2)


def _combine_rows_kernel(x1_ref, rf_ref, mod_ref, ya_ref, yb_ref, *rest):
    o_ref = rest[-1]
    rf = rf_ref[...]
    gate2 = mod_ref[0, 5:6, :]
    moe = rf[:, 0:1] * _load_row_tiles(ya_ref) + rf[:, 1:2] * _load_row_tiles(yb_ref)
    o_ref[...] = x1_ref[...] + gate2 * moe


def _combine_rows_call(x1, rf, mod, yg, part, prev=None):
    nb = SEQ // TM_ROW
    n_half = TOKENS // TM_ROW // 2
    t0 = part * n_half
    in_specs = [pl.BlockSpec((TM_ROW, D_MODEL), lambda i: (t0 + i, 0)),
                pl.BlockSpec((TM_ROW, ROUTE_W), lambda i: (t0 + i, 0)),
                pl.BlockSpec((1, N_MOD, D_MODEL), lambda i: ((t0 + i) // nb, 0, 0)),
                _row_tile_spec(TM_ROW, lambda i: (i, 0)),
                _row_tile_spec(TM_ROW, lambda i: (i + n_half, 0))]
    args = [x1, rf, mod, yg, yg]
    aliases = {}
    if prev is not None:
        in_specs.append(pl.BlockSpec(memory_space=pl.ANY))
        args.append(prev)
        aliases = {len(args) - 1: 0}
    return pl.pallas_call(
        _combine_rows_kernel,
        grid=(n_half,),
        in_specs=in_specs,
        out_specs=pl.BlockSpec((TM_ROW, D_MODEL), lambda i: (t0 + i, 0)),
        out_shape=jax.ShapeDtypeStruct((TOKENS, D_MODEL), F32),
        input_output_aliases=aliases,
        compiler_params=_cparams(("parallel",)),
        name="combine",
    )(*args)


def kernel(x, c, positions, ada_w, ada_b, norm1_g, norm2_g, w_in, gla_alpha_up, gla_alpha_b,
           gla_out_g, diff_q_g, diff_k_g, diff_lq1, diff_lk1, diff_lq2, diff_lk2, diff_out_g,
           w_branch_a, w_branch_b, w_out, router_group_w, router_group_b, router_expert_w,
           router_expert_b, expert_w1, expert_w3, expert_w2):
    assert x.shape == (BATCH, SEQ, D_MODEL) and ada_w.shape[0] == 1
    x2 = x.reshape(TOKENS, D_MODEL)

    w_in_p = _wprep_call(w_in[0])
    au_pad = jnp.zeros((LR_PAD, GLA_QK_W), F32).at[:GLA_GATE_RANK].set(gla_alpha_up[0])
    au_hi = au_pad.astype(BF16)
    au_lo = (au_pad - au_hi.astype(F32)).astype(BF16)
    gid = jnp.arange(DIFF_QK_W) // DIFF_DH
    bd = jnp.where(gid[:, None] == gid[None, :], 1.0 / DIFF_DH, 0.0).astype(BF16)
    qg_row = jnp.tile(diff_q_g[0], DIFF_QK_W // DIFF_DH).reshape(1, DIFF_QK_W)
    kg_row = jnp.tile(diff_k_g[0], DIFF_QK_W // DIFF_DH).reshape(1, DIFF_QK_W)
    inv_freq = ROPE_THETA ** (-jnp.arange(ROPE_HALF, dtype=F32) / ROPE_HALF)
    invf64 = jnp.concatenate([inv_freq, inv_freq, jnp.zeros((DIFF_DH - ROPE_DIM,), F32)])
    invf_row = jnp.tile(invf64, LANES // DIFF_DH).reshape(1, LANES)
    pos_col = positions.reshape(TOKENS, 1)
    lam_p = jnp.concatenate([diff_lq1, diff_lk1, diff_lq2, diff_lk2], axis=0)
    wr = (jnp.zeros((D_MODEL, LANES), F32)
          .at[:, :N_GROUPS].set(router_group_w[0])
          .at[:, N_GROUPS:N_GROUPS + N_EXPERTS].set(router_expert_w[0]))
    wr_hi = wr.astype(BF16)
    wr_lo = (wr - wr_hi.astype(F32)).astype(BF16)
    br = (jnp.zeros((1, LANES), F32)
          .at[0, :N_GROUPS].set(router_group_b[0])
          .at[0, N_GROUPS:N_GROUPS + N_EXPERTS].set(router_expert_b[0]))

    mod = _mod_call(c, ada_w[0], ada_b[0])
    gqk, gv, gr, lr, qT, kk, vT, sa, sb = _in_call(
        x2, mod, norm1_g, w_in_p, bd, qg_row, kg_row, pos_col, invf_row)
    o_a = _gla_call(gqk, gv, gr, lr, au_hi, au_lo, gla_alpha_b, gla_out_g)
    o_b = _attn_call(qT, kk, vT, lam_p, diff_out_g.reshape(DIFF_DV, 1))
    x1, h2, rf, rank, cnt = _merge_call(
        x2, o_a, o_b, sa, sb, mod, norm2_g, w_branch_a[0].astype(BF16),
        w_branch_b[0].astype(BF16), w_out[0].astype(BF16), wr_hi, wr_lo, br)
    dest1, dest2, tables, n_used = _plan_call(rank, cnt)

    xs = _sc_scatter_rows(_as_row_tiles(h2), dest1, dest2, N_BLOCKS * MOE_BLK)
    y = _expert_call(tables, n_used, _as_2d(xs), expert_w1[0], expert_w3[0], expert_w2[0])
    y3 = _as_row_tiles(y)
    half = TOKENS // 2
    out = None
    for part in range(2):
        tok = slice(part * half, (part + 1) * half)
        yg = _sc_gather_rows(y3, jnp.concatenate([dest1[tok], dest2[tok]]))
        out = _combine_rows_call(x1, rf, mod, _as_2d(yg), part, out)
    return out.reshape(BATCH, SEQ, D_MODEL)
```

```python
import math

import jax
import jax.numpy as jnp
from jax import lax
from jax.experimental import pallas as pl
from jax.experimental.pallas import tpu as pltpu
from jax.experimental.pallas import tpu_sc as plsc

F32 = jnp.float32
BF16 = jnp.bfloat16
I32 = jnp.int32

D_MODEL = 1024
BATCH = 4
SEQ = 4096
TOKENS = BATCH * SEQ
N_MOD = 6
NORM_EPS = 1e-6

GLA_HEADS = 4
GLA_DK = 64
GLA_DV = 128
GLA_GATE_RANK = 16
GLA_GATE_TAU = 16.0
GLA_CHUNK = 64
GLA_QK_W = GLA_HEADS * GLA_DK
GLA_V_W = GLA_HEADS * GLA_DV

DIFF_HEADS = 4
DIFF_DH = 64
DIFF_DV = 2 * DIFF_DH
DIFF_QK_W = DIFF_HEADS * 2 * DIFF_DH
DIFF_V_W = DIFF_HEADS * DIFF_DV
ROPE_THETA = 500000.0
ROPE_DIM = DIFF_DH // 4
ROPE_HALF = ROPE_DIM // 2
NEG_INF = -1e30
LAMBDA_INIT = 0.8 - 0.6 * 1.0

N_GROUPS = 4
EXPERTS_PER_GROUP = 8
N_EXPERTS = N_GROUPS * EXPERTS_PER_GROUP
TOP_K = 2
D_EXPERT = 512

LANES = 128
SUBLANES = 8
ROW_TILE_S = D_MODEL // LANES
SC_CORES = 2
SC_SUBCORES = 16
SC_WORKERS = SC_CORES * SC_SUBCORES
SC_CHUNK = 64
LR_PAD = LANES
D_IN_PAD = 2 * GLA_QK_W + 2 * GLA_V_W + 2 * DIFF_QK_W + DIFF_V_W + 2 * D_MODEL + LR_PAD
WPREP_COLS = 512
WPREP_LR_STEP = (D_IN_PAD - LR_PAD) // WPREP_COLS
D_IN_ALLOC = (WPREP_LR_STEP + 1) * WPREP_COLS

TM_IN = 512
TQ = 512
V_ROWS = DIFF_DV + 16
ATT_SUB = 32
Q_SCALE = DIFF_DH ** -0.5 * math.log2(math.e)
N_KV = SEQ // TQ
TT_GLA = 512
TM_MERGE = 512
TM_ROW = 512
MOE_BLK = 512
MOE_SUB = 256
N_BLOCKS = (TOKENS * TOP_K + N_EXPERTS * (MOE_BLK - 1) + MOE_BLK - 1) // MOE_BLK
ROUTE_W = 8

VMEM_LIMIT = 56 * 1024 * 1024


def _cparams(sem):
    return pltpu.CompilerParams(dimension_semantics=sem, vmem_limit_bytes=VMEM_LIMIT)


def _sigmoid(x):
    return 1.0 / (1.0 + jnp.exp(-x))


def _row_tile_spec(rows, index_map):
    return pl.BlockSpec((rows * ROW_TILE_S, LANES), index_map)


def _as_row_tiles(a2d):
    return a2d.reshape(a2d.shape[0] // ROW_TILE_S, ROW_TILE_S, LANES)


def _as_2d(a3d):
    return a3d.reshape(a3d.shape[0] * ROW_TILE_S, LANES)


def _store_row_tiles(ref, val, row0=0):
    for s in range(ROW_TILE_S):
        ref[pl.ds(row0 * ROW_TILE_S + s, val.shape[0], stride=ROW_TILE_S), :] = (
            val[:, s * LANES:(s + 1) * LANES])


def _load_row_tiles(ref, row0=0, rows=None):
    rows = ref.shape[0] // ROW_TILE_S if rows is None else rows
    return jnp.concatenate(
        [ref[pl.ds(row0 * ROW_TILE_S + s, rows, stride=ROW_TILE_S), :] for s in range(ROW_TILE_S)],
        axis=1)


def _mod_kernel(ct_ref, w_ref, b_ref, o_ref):
    ct = ct_ref[...]
    ca = ct * _sigmoid(ct)
    w = w_ref[...]
    rows = []
    for b in range(BATCH):
        rows.append(jnp.sum(w * ca[:, b:b + 1], axis=0, keepdims=True) + b_ref[...])
    rows.append(jnp.zeros((8 - BATCH, w.shape[1]), F32))
    o_ref[...] = jnp.concatenate(rows, axis=0)


def _mod_call(c, ada_w, ada_b):
    tn = D_MODEL
    ct = jnp.zeros((D_MODEL, 8), F32).at[:, :BATCH].set(c.T)
    out = pl.pallas_call(
        _mod_kernel,
        grid=(N_MOD,),
        in_specs=[
            pl.BlockSpec((D_MODEL, 8), lambda j: (0, 0)),
            pl.BlockSpec((D_MODEL, tn), lambda j: (0, j)),
            pl.BlockSpec((1, tn), lambda j: (0, j)),
        ],
        out_specs=pl.BlockSpec((8, tn), lambda j: (0, j)),
        out_shape=jax.ShapeDtypeStruct((8, N_MOD * D_MODEL), F32),
        compiler_params=_cparams(("arbitrary",)),
        name="mod",
    )(ct, ada_w, ada_b.reshape(1, N_MOD * D_MODEL))
    return out[:BATCH].reshape(BATCH, N_MOD, D_MODEL)


def _wprep_kernel(w_ref, o_ref):
    j = pl.program_id(0)
    t = w_ref[...].T
    lane = lax.broadcasted_iota(I32, t.shape, 1)
    @pl.when(j < WPREP_LR_STEP)
    def _():
        o_ref[...] = t.astype(BF16)

    @pl.when(j == WPREP_LR_STEP)
    def _():
        rep = jnp.where(lane < GLA_GATE_RANK, t,
                        jnp.where(lane < 2 * GLA_GATE_RANK, pltpu.roll(t, GLA_GATE_RANK, 1),
                                  jnp.where(lane < 3 * GLA_GATE_RANK,
                                            pltpu.roll(t, 2 * GLA_GATE_RANK, 1), 0.0)))
        o_ref[...] = rep.astype(BF16)


def _wprep_call(w_in0):
    w_t = w_in0.T
    lr0 = 2 * GLA_QK_W + 2 * GLA_V_W
    assert lr0 % WPREP_COLS == 0 and (D_IN_PAD - LR_PAD) % WPREP_COLS == 0

    def src_row(j):
        return jnp.where(j < lr0 // WPREP_COLS, j * WPREP_COLS,
                         jnp.where(j < WPREP_LR_STEP, j * WPREP_COLS + GLA_GATE_RANK, lr0))

    return pl.pallas_call(
        _wprep_kernel,
        grid=(WPREP_LR_STEP + 1,),
        in_specs=[pl.BlockSpec((pl.Element(WPREP_COLS), pl.Element(D_MODEL)),
                               lambda j: (pl.multiple_of(src_row(j), SUBLANES), 0))],
        out_specs=pl.BlockSpec((D_MODEL, WPREP_COLS), lambda j: (0, j)),
        out_shape=jax.ShapeDtypeStruct((D_MODEL, D_IN_ALLOC), BF16),
        compiler_params=_cparams(("parallel",)),
        name="w_prep",
    )(w_t)


def _in_kernel(x_ref, mod_ref, g1_ref, w_ref, bd_ref, qg_ref, kg_ref, pos_ref, invf_ref,
               gqk_ref, gv_ref, gr_ref, lr_ref, qT_ref, k_ref, vT_ref, sa_ref, sb_ref,
               cos_scr, sin_scr):
    ang = pos_ref[...].astype(F32) * invf_ref[...]
    cos_scr[...] = jnp.cos(ang)
    sin_scr[...] = jnp.sin(ang)

    x = x_ref[...]
    shift1 = mod_ref[0, 0:1, :]
    scale1 = mod_ref[0, 1:2, :]
    ms = jnp.mean(x * x, axis=-1, keepdims=True)
    h = (x * lax.rsqrt(ms + NORM_EPS) * g1_ref[...]) * (1.0 + scale1) + shift1
    hb = h.astype(BF16)

    def proj(c0, c1):
        return jnp.dot(hb, w_ref[:, c0:c1], preferred_element_type=F32)

    tm = x.shape[0]
    cos4 = jnp.concatenate([cos_scr[...]] * DIFF_HEADS, axis=1)
    sin4 = jnp.concatenate([sin_scr[...]] * DIFF_HEADS, axis=1)
    lane = lax.broadcasted_iota(I32, (tm, DIFF_QK_W), 1)
    first_half = (lane % DIFF_DH) < ROPE_HALF
    bd = bd_ref[...]

    def norm_rope(t, gain_row):
        t2 = t * t
        hi = t2.astype(BF16)
        lo = (t2 - hi.astype(F32)).astype(BF16)
        gms = (jnp.dot(hi, bd, preferred_element_type=F32)
               + jnp.dot(lo, bd, preferred_element_type=F32))
        t = t * lax.rsqrt(gms + NORM_EPS) * gain_row
        nxt = pltpu.roll(t, DIFF_QK_W - ROPE_HALF, 1)
        prv = pltpu.roll(t, ROPE_HALF, 1)
        return t * cos4 + jnp.where(first_half, -nxt, prv) * sin4

    dq_raw = proj(1536, 2048)
    dk_raw = proj(2048, 2560)
    gqk_ref[...] = proj(0, 512).astype(BF16)
    dq = norm_rope(dq_raw, qg_ref[...]) * Q_SCALE
    gv_ref[...] = proj(512, 1024).astype(BF16)
    dk = norm_rope(dk_raw, kg_ref[...])
    dv = proj(2560, 3072)
    for hd in range(DIFF_HEADS):
        sl = slice(hd * LANES, (hd + 1) * LANES)
        qT_ref[0, hd, 0] = dq[:, sl].T.astype(BF16)
        k_ref[0, hd, 0] = dk[:, sl].astype(BF16)
    gr_ref[...] = proj(1024, 1536).astype(BF16)
    for hd in range(DIFF_HEADS):
        sl = slice(hd * LANES, (hd + 1) * LANES)
        vT_ref[0, hd, 0] = jnp.concatenate(
            [dv[:, sl].T, jnp.ones((V_ROWS - DIFF_DV, tm), F32)], axis=0).astype(BF16)
    sa_ref[...] = _sigmoid(proj(3072, 4096)).astype(BF16)
    sb_ref[...] = _sigmoid(proj(4096, 5120)).astype(BF16)
    lr_ref[...] = proj(5120, 5248)


def _in_call(x2, mod, norm1_g, w_in_p, bd, qg_row, kg_row, pos_col, invf_row):
    nb = SEQ // TM_IN
    tok_spec = lambda w: pl.BlockSpec((TM_IN, w), lambda i: (i, 0))
    const2 = lambda r, c: pl.BlockSpec((r, c), lambda i: (0, 0))
    out_shapes = (
        jax.ShapeDtypeStruct((TOKENS, 2 * GLA_QK_W), BF16),
        jax.ShapeDtypeStruct((TOKENS, GLA_V_W), BF16),
        jax.ShapeDtypeStruct((TOKENS, GLA_V_W), BF16),
        jax.ShapeDtypeStruct((TOKENS, LR_PAD), F32),
        jax.ShapeDtypeStruct((BATCH, DIFF_HEADS, N_KV, LANES, TQ), BF16),
        jax.ShapeDtypeStruct((BATCH, DIFF_HEADS, N_KV, TQ, LANES), BF16),
        jax.ShapeDtypeStruct((BATCH, DIFF_HEADS, N_KV, V_ROWS, TQ), BF16),
        jax.ShapeDtypeStruct((TOKENS, D_MODEL), BF16),
        jax.ShapeDtypeStruct((TOKENS, D_MODEL), BF16),
    )
    out_specs = (
        tok_spec(2 * GLA_QK_W), tok_spec(GLA_V_W), tok_spec(GLA_V_W), tok_spec(LR_PAD),
        pl.BlockSpec((1, DIFF_HEADS, 1, LANES, TM_IN), lambda i: (i // nb, 0, i % nb, 0, 0)),
        pl.BlockSpec((1, DIFF_HEADS, 1, TM_IN, LANES), lambda i: (i // nb, 0, i % nb, 0, 0)),
        pl.BlockSpec((1, DIFF_HEADS, 1, V_ROWS, TM_IN), lambda i: (i // nb, 0, i % nb, 0, 0)),
        tok_spec(D_MODEL), tok_spec(D_MODEL),
    )
    return pl.pallas_call(
        _in_kernel,
        grid=(TOKENS // TM_IN,),
        in_specs=[
            tok_spec(D_MODEL),
            pl.BlockSpec((1, N_MOD, D_MODEL), lambda i: (i // nb, 0, 0)),
            const2(1, D_MODEL),
            pl.BlockSpec((D_MODEL, D_IN_ALLOC), lambda i: (0, 0), pipeline_mode=pl.Buffered(1)),
            const2(DIFF_QK_W, DIFF_QK_W),
            const2(1, DIFF_QK_W), const2(1, DIFF_QK_W),
            tok_spec(1),
            const2(1, LANES),
        ],
        out_specs=out_specs,
        out_shape=out_shapes,
        scratch_shapes=[pltpu.VMEM((TM_IN, LANES), F32), pltpu.VMEM((TM_IN, LANES), F32)],
        compiler_params=_cparams(("parallel",)),
        name="in_proj",
    )(x2, mod, norm1_g, w_in_p, bd, qg_row, kg_row, pos_col, invf_row)


def _gla_kernel(qk_ref, v_ref, r_ref, lr_ref, au_ref, ab_ref, og_ref, o_ref,
                state_ref, oacc_ref, snew_ref):
    tt = qk_ref.shape[0]
    n_chunks = tt // GLA_CHUNK

    @pl.when(pl.program_id(1) == 0)
    def _():
        state_ref[...] = jnp.zeros_like(state_ref)

    lr = lr_ref[...]
    lr_hi = lr.astype(BF16)
    lr_lo = (lr - lr_hi.astype(F32)).astype(BF16)
    lr_lane = lax.broadcasted_iota(I32, lr.shape, 1)
    in_lo_group = (lr_lane >= GLA_GATE_RANK) & (lr_lane < 2 * GLA_GATE_RANK)
    stacked = jnp.where(in_lo_group, lr_lo, lr_hi)
    z = jnp.dot(stacked, au_ref[...], preferred_element_type=F32) + ab_ref[...]
    g = (jnp.minimum(z, 0.0) - jnp.log(1.0 + jnp.exp(-jnp.abs(z)))) * (1.0 / GLA_GATE_TAU)

    row = lax.broadcasted_iota(I32, (tt, GLA_QK_W), 0) % GLA_CHUNK
    b = g
    step = 1
    while step < GLA_CHUNK:
        b = b + jnp.where(row >= step, pltpu.roll(b, step, 0), 0.0)
        step *= 2

    b_last_rows = [b[c * GLA_CHUNK + GLA_CHUNK - 1:(c + 1) * GLA_CHUNK, :] for c in range(n_chunks)]
    b_last = jnp.concatenate(
        [jnp.broadcast_to(bl, (GLA_CHUNK, GLA_QK_W)) for bl in b_last_rows], axis=0)

    qk = qk_ref[...].astype(F32)
    q = qk[:, :GLA_QK_W] * (GLA_DK ** -0.5)
    k = qk[:, GLA_QK_W:]
    q_in = (q * jnp.exp(b)).astype(BF16)
    k_in = (k * jnp.exp(-b)).astype(BF16)
    k_dec = (k * jnp.exp(b_last - b)).astype(BF16)

    ci = lax.broadcasted_iota(I32, (GLA_CHUNK, GLA_CHUNK), 0)
    cj = lax.broadcasted_iota(I32, (GLA_CHUNK, GLA_CHUNK), 1)
    causal = ci >= cj

    pairs = [(c, hd) for c in range(n_chunks) for hd in range(GLA_HEADS)]

    def rows(c):
        return slice(c * GLA_CHUNK, (c + 1) * GLA_CHUNK)

    def kcols(hd):
        return slice(hd * GLA_DK, (hd + 1) * GLA_DK)

    def vcols(hd):
        return slice(hd * GLA_DV, (hd + 1) * GLA_DV)

    att = {}
    for c, hd in pairs:
        a = lax.dot_general(q_in[rows(c), kcols(hd)], k_in[rows(c), kcols(hd)],
                            (((1,), (1,)), ((), ())), preferred_element_type=F32)
        att[c, hd] = jnp.where(causal, a, 0.0).astype(BF16)
    for c, hd in pairs:
        oacc_ref[rows(c), vcols(hd)] = jnp.dot(att[c, hd], v_ref[rows(c), vcols(hd)],
                                               preferred_element_type=F32)
    for c, hd in pairs:
        snew_ref[c * GLA_HEADS + hd] = lax.dot_general(
            k_dec[rows(c), kcols(hd)], v_ref[rows(c), vcols(hd)],
            (((0,), (0,)), ((), ())), preferred_element_type=F32)

    decay_rows = jnp.exp(jnp.concatenate(
        b_last_rows + [jnp.zeros((LANES - n_chunks, GLA_QK_W), F32)], axis=0))
    decay_cols = decay_rows.T
    states = [state_ref[hd] for hd in range(GLA_HEADS)]
    for c in range(n_chunks):
        for hd in range(GLA_HEADS):
            s_prev = states[hd]
            oacc_ref[rows(c), vcols(hd)] += jnp.dot(
                q_in[rows(c), kcols(hd)], s_prev.astype(BF16), preferred_element_type=F32)
            dcol = decay_cols[kcols(hd), c:c + 1]
            states[hd] = s_prev * dcol + snew_ref[c * GLA_HEADS + hd]
    for hd in range(GLA_HEADS):
        state_ref[hd] = states[hd]

    for hd in range(GLA_HEADS):
        vs = slice(hd * GLA_DV, (hd + 1) * GLA_DV)
        oh = oacc_ref[:, vs]
        ms = jnp.mean(oh * oh, axis=-1, keepdims=True)
        y = oh * lax.rsqrt(ms + NORM_EPS) * og_ref[...]
        r = r_ref[:, vs].astype(F32)
        o_ref[:, vs] = (y * (r * _sigmoid(r))).astype(BF16)


def _gla_call(gqk, gv, gr, lr, au_stack, ab_row, og_row):
    nt = SEQ // TT_GLA
    tok = lambda w: pl.BlockSpec((TT_GLA, w), lambda b, t: (b * nt + t, 0))
    const2 = lambda r, c: pl.BlockSpec((r, c), lambda b, t: (0, 0))
    return pl.pallas_call(
        _gla_kernel,
        grid=(BATCH, nt),
        in_specs=[tok(2 * GLA_QK_W), tok(GLA_V_W), tok(GLA_V_W), tok(LR_PAD),
                  const2(LR_PAD, GLA_QK_W), const2(1, GLA_QK_W), const2(1, GLA_DV)],
        out_specs=tok(GLA_V_W),
        out_shape=jax.ShapeDtypeStruct((TOKENS, GLA_V_W), BF16),
        scratch_shapes=[pltpu.VMEM((GLA_HEADS, GLA_DK, GLA_DV), F32),
                        pltpu.VMEM((TT_GLA, GLA_V_W), F32),
                        pltpu.VMEM((TT_GLA // GLA_CHUNK * GLA_HEADS, GLA_DK, GLA_DV), F32)],
        compiler_params=_cparams(("parallel", "arbitrary")),
        name="gla",
    )(gqk, gv, gr, lr, au_stack, ab_row, og_row)


def _attn_kernel(qT_ref, k_ref, vT_ref, lam_ref, og_ref, o_ref,
                 q_scr, s_scr, p_scr, acc_scr, m_scr):
    _attn_load_q(qT_ref, q_scr, 0, 0)
    s_scr[0] = jnp.dot(k_ref[0, 0, 0], q_scr[0, 0], preferred_element_type=F32)

    def q_block(i, carry):
        _attn_q_block(i, qT_ref, k_ref, vT_ref, lam_ref, og_ref, o_ref,
                      q_scr, s_scr, p_scr, acc_scr, m_scr)
        return carry

    lax.fori_loop(0, N_KV, q_block, 0)


def _attn_load_q(qT_ref, q_scr, i, slot):
    qT = qT_ref[0, 0, i]
    rowq = lax.broadcasted_iota(I32, qT.shape, 0)
    zero = jnp.zeros_like(qT)
    q_scr[slot, 0] = jnp.where(rowq < DIFF_DH, qT, zero)
    q_scr[slot, 1] = jnp.where(rowq >= DIFF_DH, qT, zero)


def _attn_q_block(i, qT_ref, k_ref, vT_ref, lam_ref, og_ref, o_ref,
                  q_scr, s_scr, p_scr, acc_scr, m_scr):
    slot = i % 2
    m_scr[...] = jnp.full(m_scr.shape, NEG_INF, F32)
    acc_scr[...] = jnp.zeros(acc_scr.shape, F32)
    n_sub = TQ // ATT_SUB

    def fold8(t, op):
        return op(t.reshape(t.shape[0] // SUBLANES, SUBLANES, TQ), axis=0)

    def scores(c, j):
        s_scr[c] = jnp.dot(k_ref[0, 0, j], q_scr[slot, c], preferred_element_type=F32)

    def load_s(c, r, masked):
        s = s_scr[c, r * ATT_SUB:(r + 1) * ATT_SUB, :]
        if masked:
            key_i = lax.broadcasted_iota(I32, (ATT_SUB, TQ), 0) + r * ATT_SUB
            qry_i = lax.broadcasted_iota(I32, (ATT_SUB, TQ), 1)
            s = jnp.where(key_i <= qry_i, s, NEG_INF)
        return s

    def softmax_pv(c, j, masked):
        m8 = fold8(load_s(c, 0, masked), jnp.max)
        for r in range(1, n_sub):
            m8 = jnp.maximum(m8, fold8(load_s(c, r, masked), jnp.max))
        m_old = m_scr[c]
        m_new = jnp.maximum(m_old, jnp.max(m8, axis=0, keepdims=True))
        alpha = jnp.exp2(m_old - m_new)
        for r in range(n_sub):
            p = jnp.exp2(load_s(c, r, masked) - m_new)
            p_scr[c, r * ATT_SUB:(r + 1) * ATT_SUB, :] = p.astype(BF16)
        m_scr[c] = m_new
        acc_scr[c] = acc_scr[c] * alpha + jnp.dot(vT_ref[0, 0, j], p_scr[c],
                                                  preferred_element_type=F32)

    def body(j, carry):
        scores(1, j)
        softmax_pv(0, j, False)
        scores(0, j + 1)
        softmax_pv(1, j, False)
        return carry

    lax.fori_loop(0, i, body, 0)
    scores(1, i)
    softmax_pv(0, i, True)
    softmax_pv(1, i, True)

    nxt = jnp.minimum(i + 1, N_KV - 1)
    _attn_load_q(qT_ref, q_scr, nxt, 1 - slot)
    s_scr[0] = jnp.dot(k_ref[0, 0, 0], q_scr[1 - slot, 0], preferred_element_type=F32)

    l1 = acc_scr[0, DIFF_DV:DIFF_DV + 1, :]
    l2 = acc_scr[1, DIFF_DV:DIFF_DV + 1, :]

    lam_p = lam_ref[...]
    lam = (jnp.exp(jnp.sum(lam_p[0:1] * lam_p[1:2], axis=1, keepdims=True))
           - jnp.exp(jnp.sum(lam_p[2:3] * lam_p[3:4], axis=1, keepdims=True)) + LAMBDA_INIT)
    oT = acc_scr[0, :DIFF_DV, :] / l1 - lam * (acc_scr[1, :DIFF_DV, :] / l2)
    ms = jnp.mean(oT * oT, axis=0, keepdims=True)
    y = oT * lax.rsqrt(ms + NORM_EPS) * og_ref[...] * (1.0 - LAMBDA_INIT)
    o_ref[pl.ds(pl.multiple_of(i * TQ, TQ), TQ), :] = y.T.astype(BF16)


def _attn_call(qT, kk, vT, lam_p, og_col):
    return pl.pallas_call(
        _attn_kernel,
        grid=(BATCH, DIFF_HEADS),
        in_specs=[
            pl.BlockSpec((1, 1, N_KV, LANES, TQ), lambda b, h: (b, h, 0, 0, 0)),
            pl.BlockSpec((1, 1, N_KV, TQ, LANES), lambda b, h: (b, h, 0, 0, 0)),
            pl.BlockSpec((1, 1, N_KV, V_ROWS, TQ), lambda b, h: (b, h, 0, 0, 0)),
            pl.BlockSpec((4, DIFF_DH), lambda b, h: (0, 0)),
            pl.BlockSpec((DIFF_DV, 1), lambda b, h: (0, 0)),
        ],
        out_specs=pl.BlockSpec((SEQ, DIFF_DV), lambda b, h: (b, h)),
        out_shape=jax.ShapeDtypeStruct((TOKENS, DIFF_V_W), BF16),
        scratch_shapes=[pltpu.VMEM((2, 2, LANES, TQ), BF16),
                        pltpu.VMEM((2, TQ, TQ), F32),
                        pltpu.VMEM((2, TQ, TQ), BF16),
                        pltpu.VMEM((2, V_ROWS, TQ), F32),
                        pltpu.VMEM((2, 1, TQ), F32)],
        compiler_params=_cparams(("parallel", "parallel")),
        name="attn",
    )(qT, kk, vT, lam_p, og_col)


def _merge_kernel(x_ref, oa_ref, ob_ref, sa_ref, sb_ref, mod_ref, g2_ref, wa_ref, wb_ref, wo_ref,
                  wrh_ref, wrl_ref, br_ref, x1_ref, h2_ref, rf_ref, rank_ref, cnt_ref,
                  carry_ref, tri_ref, lg_ref):
    tm = x_ref.shape[0]
    step = pl.program_id(0)

    @pl.when(step == 0)
    def _():
        carry_ref[...] = jnp.zeros_like(carry_ref)
        lg_ref[...] = jnp.zeros_like(lg_ref)
        ti = lax.broadcasted_iota(I32, (tm, tm), 0)
        tj = lax.broadcasted_iota(I32, (tm, tm), 1)
        tri_ref[...] = jnp.where(ti > tj, 1.0, 0.0).astype(BF16)

    ma = jnp.dot(oa_ref[...], wa_ref[...], preferred_element_type=F32)
    mb = jnp.dot(ob_ref[...], wb_ref[...], preferred_element_type=F32)
    merged = sa_ref[...].astype(F32) * ma + sb_ref[...].astype(F32) * mb
    y = jnp.dot(merged.astype(BF16), wo_ref[...], preferred_element_type=F32)
    logits = lg_ref[...]
    lane = lax.broadcasted_iota(I32, (tm, LANES), 1).astype(F32)
    ninf = -jnp.inf
    big = float(LANES)

    def first_argmax(v):
        vmax = jnp.max(v, axis=1, keepdims=True)
        idx = jnp.min(jnp.where(v == vmax, lane, big), axis=1, keepdims=True)
        return vmax, idx

    gl = jnp.where(lane < N_GROUPS, logits, ninf)
    gmax, gidx = first_argmax(gl)
    p_top = 1.0 / jnp.sum(jnp.exp(gl - gmax), axis=1, keepdims=True)
    lo = N_GROUPS + EXPERTS_PER_GROUP * gidx
    el = jnp.where((lane >= lo) & (lane < lo + EXPERTS_PER_GROUP), logits, ninf)
    e1max, e1 = first_argmax(el)
    e2max, e2 = first_argmax(jnp.where(lane == e1, ninf, el))
    t = jnp.exp(e2max - e1max)
    w1 = 1.0 / (1.0 + t)
    w2 = t / (1.0 + t)
    col = lax.broadcasted_iota(I32, (tm, ROUTE_W), 1)
    rf_ref[...] = jnp.where(col == 0, p_top * w1, jnp.where(col == 1, p_top * w2, 0.0))

    x1id = e1 - N_GROUPS
    x2id = e2 - N_GROUPS
    hit1 = lane == x1id
    hit2 = lane == x2id
    onehot = jnp.where(hit1 | hit2, 1.0, 0.0)
    before = jnp.dot(tri_ref[...], onehot.astype(BF16), preferred_element_type=F32) + carry_ref[...]
    r1 = jnp.sum(jnp.where(hit1, before, 0.0), axis=1, keepdims=True)
    r2 = jnp.sum(jnp.where(hit2, before, 0.0), axis=1, keepdims=True)
    real_tile = jnp.where(step > 0, 1.0, 0.0)
    carry_ref[...] = carry_ref[...] + real_tile * jnp.sum(onehot, axis=0, keepdims=True)
    cnt_ref[...] = carry_ref[...]
    cols = jnp.where(lane == 0.0, x1id, jnp.where(lane == 1.0, x2id,
                                                  jnp.where(lane == 2.0, r1,
                                                            jnp.where(lane == 3.0, r2, 0.0))))
    rank_ref[0] = cols.T[0:ROUTE_W, :].astype(I32)

    gate1 = mod_ref[0, 2:3, :]
    shift2 = mod_ref[0, 3:4, :]
    scale2 = mod_ref[0, 4:5, :]
    x1 = x_ref[...] + gate1 * y
    x1_ref[...] = x1
    ms = jnp.mean(x1 * x1, axis=-1, keepdims=True)
    h2 = (x1 * lax.rsqrt(ms + NORM_EPS) * g2_ref[...]) * (1.0 + scale2) + shift2
    _store_row_tiles(h2_ref, h2)
    h2_hi = h2.astype(BF16)
    h2_lo = (h2 - h2_hi.astype(F32)).astype(BF16)
    next_logits = (jnp.dot(h2_hi, wrh_ref[...], preferred_element_type=F32)
                   + jnp.dot(h2_lo, wrh_ref[...], preferred_element_type=F32)
                   + jnp.dot(h2_hi, wrl_ref[...], preferred_element_type=F32)) + br_ref[...]

    lg_ref[...] = next_logits


def _merge_call(x2, o_a, o_b, sa, sb, mod, norm2_g, wa, wb, wo, wr_hi, wr_lo, br):
    nb = SEQ // TM_MERGE
    n_tiles = TOKENS // TM_MERGE
    cur = lambda i: jnp.minimum(i, n_tiles - 1)
    prev = lambda i: jnp.maximum(i - 1, 0)
    tok = lambda w: pl.BlockSpec((TM_MERGE, w), lambda i: (cur(i), 0))
    const2 = lambda r, c: pl.BlockSpec((r, c), lambda i: (0, 0))
    return pl.pallas_call(
        _merge_kernel,
        grid=(n_tiles + 1,),
        in_specs=[tok(D_MODEL), tok(GLA_V_W), tok(DIFF_V_W), tok(D_MODEL), tok(D_MODEL),
                  pl.BlockSpec((1, N_MOD, D_MODEL), lambda i: (cur(i) // nb, 0, 0)),
                  const2(1, D_MODEL),
                  const2(GLA_V_W, D_MODEL), const2(DIFF_V_W, D_MODEL), const2(D_MODEL, D_MODEL),
                  const2(D_MODEL, LANES), const2(D_MODEL, LANES), const2(1, LANES)],
        out_specs=(tok(D_MODEL), _row_tile_spec(TM_MERGE, lambda i: (cur(i), 0)),
                   pl.BlockSpec((TM_MERGE, ROUTE_W), lambda i: (prev(i), 0)),
                   pl.BlockSpec((1, ROUTE_W, TM_MERGE), lambda i: (prev(i), 0, 0)),
                   pl.BlockSpec((1, LANES), lambda i: (0, 0))),
        out_shape=(jax.ShapeDtypeStruct((TOKENS, D_MODEL), F32),
                   jax.ShapeDtypeStruct((TOKENS * ROW_TILE_S, LANES), F32),
                   jax.ShapeDtypeStruct((TOKENS, ROUTE_W), F32),
                   jax.ShapeDtypeStruct((n_tiles, ROUTE_W, TM_MERGE), I32),
                   jax.ShapeDtypeStruct((1, LANES), F32)),
        scratch_shapes=[pltpu.VMEM((1, LANES), F32), pltpu.VMEM((TM_MERGE, TM_MERGE), BF16),
                        pltpu.VMEM((TM_MERGE, LANES), F32)],
        compiler_params=_cparams(("arbitrary",)),
        name="merge",
    )(x2, o_a, o_b, sa, sb, mod, norm2_g, wa, wb, wo, wr_hi, wr_lo, br)


def _plan_kernel(rk_ref, cnt_ref, dest_ref, blk_ref):
    n_tiles, _, tm = rk_ref.shape
    cnt = cnt_ref[...]
    padded = jnp.floor((cnt + (MOE_BLK - 1)) * (1.0 / MOE_BLK)) * MOE_BLK
    lane = lax.broadcasted_iota(I32, (1, LANES), 1)
    seg_end = padded
    step = 1
    while step < N_EXPERTS:
        seg_end = seg_end + jnp.where(lane >= step, pltpu.roll(seg_end, step, 1), 0.0)
        step *= 2
    seg_start = seg_end - padded
    valid_end = seg_start + cnt

    ei = lax.broadcasted_iota(I32, (LANES, LANES), 0)
    ej = lax.broadcasted_iota(I32, (LANES, LANES), 1)

    def to_col(rowv):
        return jnp.sum(jnp.where(ei == ej, rowv, 0.0), axis=1, keepdims=True)

    start_col, end_col, valid_col = to_col(seg_start), to_col(seg_end), to_col(valid_end)

    nb_pad = blk_ref.shape[1]
    e_sub = lax.broadcasted_iota(I32, (LANES, nb_pad), 0)
    b_start = (lax.broadcasted_iota(I32, (1, nb_pad), 1) * MOE_BLK).astype(F32)
    ends_before = jnp.where((e_sub < N_EXPERTS) & (end_col <= b_start), 1.0, 0.0)
    block_e = jnp.minimum(jnp.sum(ends_before, axis=0, keepdims=True), N_EXPERTS - 1.0)
    block_valid_end = jnp.sum(jnp.where(e_sub.astype(F32) == block_e, valid_col, 0.0),
                              axis=0, keepdims=True)
    n_valid = jnp.clip(block_valid_end - b_start, 0.0, float(MOE_BLK))
    n_used = jnp.max(seg_end, axis=1, keepdims=True) * (1.0 / MOE_BLK)
    nonempty = (to_col(cnt) > 0.0) & (e_sub < N_EXPERTS)
    e_subf = e_sub.astype(F32)
    no_next = float(LANES)
    nxt = jnp.min(jnp.where(nonempty & (e_subf > block_e), e_subf, no_next), axis=0, keepdims=True)
    nxt = jnp.where(nxt == no_next, -1.0, nxt)
    seg_idx = jnp.sum(jnp.where(nonempty & (e_subf < block_e), 1.0, 0.0), axis=0, keepdims=True)
    slot = seg_idx - 2.0 * jnp.floor(seg_idx * 0.5)
    blk_ref[...] = jnp.concatenate(
        [block_e, n_valid, jnp.broadcast_to(n_used, (1, nb_pad)), nxt, slot,
         jnp.zeros((ROUTE_W - 5, nb_pad), F32)], axis=0).astype(I32)

    e_tok = lax.broadcasted_iota(I32, (LANES, tm), 0)

    def tile(t, carry):
        rk = rk_ref[t]
        d = [jnp.sum(jnp.where(e_tok == rk[k:k + 1], start_col, 0.0), axis=0, keepdims=True)
             + rk[k + 2:k + 3].astype(F32) for k in range(TOP_K)]
        dest_ref[t] = jnp.concatenate(d + [jnp.zeros((ROUTE_W - TOP_K, tm), F32)], axis=0).astype(I32)
        return carry

    lax.fori_loop(0, n_tiles, tile, 0)


def _plan_call(rank_rows, cnt):
    n_tiles = TOKENS // TM_MERGE
    nb_pad = -(-N_BLOCKS // LANES) * LANES
    dest, blk = pl.pallas_call(
        _plan_kernel,
        out_shape=(jax.ShapeDtypeStruct((n_tiles, ROUTE_W, TM_MERGE), I32),
                   jax.ShapeDtypeStruct((ROUTE_W, nb_pad), I32)),
        compiler_params=pltpu.CompilerParams(vmem_limit_bytes=VMEM_LIMIT),
        name="plan",
    )(rank_rows, cnt)
    dest1 = dest[:, 0, :].reshape(TOKENS)
    dest2 = dest[:, 1, :].reshape(TOKENS)
    tables = tuple(blk[r, :N_BLOCKS] for r in (0, 1, 3, 4))
    return dest1, dest2, tables, blk[2, :1]


def _sc_params():
    return pltpu.CompilerParams(use_tc_tiling_on_sc=True)


def _sc_mesh():
    return plsc.VectorSubcoreMesh(core_axis_name="core", subcore_axis_name="subcore")


def _sc_worker_base(per_worker):
    wid = lax.axis_index("subcore") * SC_CORES + lax.axis_index("core")
    return wid * per_worker


def _sc_scatter_rows(src, dest1, dest2, n_out):
    n = src.shape[0]
    per_worker = n // SC_WORKERS
    assert per_worker * SC_WORKERS == n and per_worker % SC_CHUNK == 0

    def body(src_hbm, d1_hbm, d2_hbm, out_hbm, idx_v, rows_v):
        base = _sc_worker_base(per_worker)

        @pl.loop(0, per_worker // SC_CHUNK)
        def _(j):
            start = pl.multiple_of(base + j * SC_CHUNK, SC_CHUNK)
            pltpu.sync_copy(src_hbm.at[pl.ds(start, SC_CHUNK)], rows_v)
            for d_hbm in (d1_hbm, d2_hbm):
                pltpu.sync_copy(d_hbm.at[pl.ds(start, SC_CHUNK)], idx_v)
                pltpu.sync_copy(rows_v, out_hbm.at[idx_v])

    return pl.kernel(
        body,
        out_type=jax.ShapeDtypeStruct((n_out, ROW_TILE_S, LANES), F32),
        mesh=_sc_mesh(),
        scratch_types=[pltpu.VMEM((SC_CHUNK,), I32),
                       pltpu.VMEM((SC_CHUNK, ROW_TILE_S, LANES), F32)],
        compiler_params=_sc_params(),
        name="sc_dispatch",
    )(src, dest1, dest2)


def _sc_gather_rows(table, idx):
    n = idx.shape[0]
    per_worker = n // SC_WORKERS
    half = SC_CHUNK // 2
    assert per_worker * SC_WORKERS == n and per_worker % SC_CHUNK == 0

    def body(table_hbm, idx_hbm, out_hbm, idx_v, rows_a, rows_b, gsem_a, gsem_b, wsem_a, wsem_b):
        base = pl.multiple_of(_sc_worker_base(per_worker), SC_CHUNK)
        pltpu.sync_copy(idx_hbm.at[pl.ds(base, per_worker)], idx_v)

        @pl.loop(0, per_worker // SC_CHUNK)
        def _(j):
            off_a = pl.multiple_of(j * SC_CHUNK, SC_CHUNK)
            off_b = pl.multiple_of(j * SC_CHUNK + half, half)
            ga = pltpu.async_copy(table_hbm.at[idx_v.at[pl.ds(off_a, half)]], rows_a, gsem_a)
            gb = pltpu.async_copy(table_hbm.at[idx_v.at[pl.ds(off_b, half)]], rows_b, gsem_b)
            ga.wait()
            wa = pltpu.async_copy(rows_a, out_hbm.at[pl.ds(base + off_a, half)], wsem_a)
            gb.wait()
            wb = pltpu.async_copy(rows_b, out_hbm.at[pl.ds(base + off_b, half)], wsem_b)
            wa.wait()
            wb.wait()

    return pl.kernel(
        body,
        out_type=jax.ShapeDtypeStruct((n, ROW_TILE_S, LANES), F32),
        mesh=_sc_mesh(),
        scratch_types=[pltpu.VMEM((per_worker,), I32),
                       pltpu.VMEM((half, ROW_TILE_S, LANES), F32),
                       pltpu.VMEM((half, ROW_TILE_S, LANES), F32),
                       pltpu.SemaphoreType.DMA, pltpu.SemaphoreType.DMA,
                       pltpu.SemaphoreType.DMA, pltpu.SemaphoreType.DMA],
        compiler_params=_sc_params(),
        name="sc_gather",
    )(table, idx)


def _weight_fetch(w_hbm, stage, sem, e, slot):
    return [pltpu.make_async_copy(w.at[e], st.at[slot], sem.at[slot]) for w, st in zip(w_hbm, stage)]


def _expert_kernel(be_ref, nv_ref, nx_ref, sl_ref, nu_ref, xs_ref, w1_hbm, w3_hbm, w2_hbm, y_ref,
                   w1s, w3s, w2s, w1b, w3b, w2b, sem):
    i = pl.program_id(0)
    e = be_ref[i]
    used = i < nu_ref[0]
    first = (i == 0) | (e != be_ref[jnp.maximum(i - 1, 0)])
    w_hbm = (w1_hbm, w3_hbm, w2_hbm)
    stage = (w1s, w3s, w2s)

    @pl.when(used & first)
    def _():
        slot = sl_ref[i]

        @pl.when(i == 0)
        def _():
            for cp in _weight_fetch(w_hbm, stage, sem, e, slot):
                cp.start()

        for cp in _weight_fetch(w_hbm, stage, sem, e, slot):
            cp.wait()
        w1b[...] = w1s[slot].astype(BF16)
        w3b[...] = w3s[slot].astype(BF16)
        w2b[...] = w2s[slot].astype(BF16)

        @pl.when(nx_ref[i] >= 0)
        def _():
            for cp in _weight_fetch(w_hbm, stage, sem, nx_ref[i], 1 - slot):
                cp.start()

    n_parts = MOE_BLK // MOE_SUB

    def load_x(part):
        row0 = part * MOE_SUB
        row = lax.broadcasted_iota(I32, (MOE_SUB, D_MODEL), 0) + row0
        return jnp.where(row < nv_ref[i], _load_row_tiles(xs_ref, row0, MOE_SUB), 0.0).astype(BF16)

    def zero_y(part):
        y_ref[pl.ds(part * MOE_SUB * ROW_TILE_S, MOE_SUB * ROW_TILE_S), :] = jnp.zeros(
            (MOE_SUB * ROW_TILE_S, LANES), F32)

    def run(parts):
        hmid = []
        for part in parts:
            xb = load_x(part)
            a = jnp.dot(xb, w1b[...], preferred_element_type=F32)
            g = jnp.dot(xb, w3b[...], preferred_element_type=F32)
            hmid.append(((a * _sigmoid(a)) * g).astype(BF16))
        for part, hm in zip(parts, hmid):
            _store_row_tiles(y_ref, jnp.dot(hm, w2b[...], preferred_element_type=F32),
                             part * MOE_SUB)

    n_live = jnp.where(used, (nv_ref[i] + (MOE_SUB - 1)) // MOE_SUB, 0)
    for k in range(n_parts + 1):
        @pl.when(n_live == k)
        def _(k=k):
            if k:
                run(list(range(k)))
            for part in range(k, n_parts):
                zero_y(part)


def _expert_call(tables, n_used, xs, w1, w3, w2):
    n_tab = len(tables)
    grid_spec = pltpu.PrefetchScalarGridSpec(
        num_scalar_prefetch=n_tab + 1,
        grid=(N_BLOCKS,),
        in_specs=[_row_tile_spec(MOE_BLK, lambda i, *pf: (jnp.minimum(i, pf[n_tab][0] - 1), 0)),
                  pl.BlockSpec(memory_space=pl.ANY),
                  pl.BlockSpec(memory_space=pl.ANY),
                  pl.BlockSpec(memory_space=pl.ANY)],
        out_specs=_row_tile_spec(MOE_BLK, lambda i, *pf: (i, 0)),
        scratch_shapes=[pltpu.VMEM((2, D_MODEL, D_EXPERT), F32),
                        pltpu.VMEM((2, D_MODEL, D_EXPERT), F32),
                        pltpu.VMEM((2, D_EXPERT, D_MODEL), F32),
                        pltpu.VMEM((D_MODEL, D_EXPERT), BF16),
                        pltpu.VMEM((D_MODEL, D_EXPERT), BF16),
                        pltpu.VMEM((D_EXPERT, D_MODEL), BF16),
                        pltpu.SemaphoreType.DMA((2,))],
    )
    return pl.pallas_call(
        _expert_kernel,
        grid_spec=grid_spec,
        out_shape=jax.ShapeDtypeStruct((N_BLOCKS * MOE_BLK * ROW_TILE_S, LANES), F32),
        compiler_params=_cparams(("arbitrary",)),
        name="experts",
    )(*tables, n_used, xs, w1, w3, w2)


def _combine_rows_kernel(x1_ref, rf_ref, mod_ref, ya_ref, yb_ref, *rest):
    o_ref = rest[-1]
    rf = rf_ref[...]
    gate2 = mod_ref[0, 5:6, :]
    moe = rf[:, 0:1] * _load_row_tiles(ya_ref) + rf[:, 1:2] * _load_row_tiles(yb_ref)
    o_ref[...] = x1_ref[...] + gate2 * moe


def _combine_rows_call(x1, rf, mod, yg, part, prev=None):
    nb = SEQ // TM_ROW
    n_half = TOKENS // TM_ROW // 2
    t0 = part * n_half
    in_specs = [pl.BlockSpec((TM_ROW, D_MODEL), lambda i: (t0 + i, 0)),
                pl.BlockSpec((TM_ROW, ROUTE_W), lambda i: (t0 + i, 0)),
                pl.BlockSpec((1, N_MOD, D_MODEL), lambda i: ((t0 + i) // nb, 0, 0)),
                _row_tile_spec(TM_ROW, lambda i: (i, 0)),
                _row_tile_spec(TM_ROW, lambda i: (i + n_half, 0))]
    args = [x1, rf, mod, yg, yg]
    aliases = {}
    if prev is not None:
        in_specs.append(pl.BlockSpec(memory_space=pl.ANY))
        args.append(prev)
        aliases = {len(args) - 1: 0}
    return pl.pallas_call(
        _combine_rows_kernel,
        grid=(n_half,),
        in_specs=in_specs,
        out_specs=pl.BlockSpec((TM_ROW, D_MODEL), lambda i: (t0 + i, 0)),
        out_shape=jax.ShapeDtypeStruct((TOKENS, D_MODEL), F32),
        input_output_aliases=aliases,
        compiler_params=_cparams(("parallel",)),
        name="combine",
    )(*args)


def kernel(x, c, positions, ada_w, ada_b, norm1_g, norm2_g, w_in, gla_alpha_up, gla_alpha_b,
           gla_out_g, diff_q_g, diff_k_g, diff_lq1, diff_lk1, diff_lq2, diff_lk2, diff_out_g,
           w_branch_a, w_branch_b, w_out, router_group_w, router_group_b, router_expert_w,
           router_expert_b, expert_w1, expert_w3, expert_w2):
    assert x.shape == (BATCH, SEQ, D_MODEL) and ada_w.shape[0] == 1
    x2 = x.reshape(TOKENS, D_MODEL)

    w_in_p = _wprep_call(w_in[0])
    au_hi = gla_alpha_up[0].astype(BF16)
    au_lo = (gla_alpha_up[0] - au_hi.astype(F32)).astype(BF16)
    au_stack = jnp.concatenate(
        [au_hi, au_hi, au_lo, jnp.zeros((LR_PAD - 3 * GLA_GATE_RANK, GLA_QK_W), BF16)], axis=0)
    gid = jnp.arange(DIFF_QK_W) // DIFF_DH
    bd = jnp.where(gid[:, None] == gid[None, :], 1.0 / DIFF_DH, 0.0).astype(BF16)
    qg_row = jnp.tile(diff_q_g[0], DIFF_QK_W // DIFF_DH).reshape(1, DIFF_QK_W)
    kg_row = jnp.tile(diff_k_g[0], DIFF_QK_W // DIFF_DH).reshape(1, DIFF_QK_W)
    inv_freq = ROPE_THETA ** (-jnp.arange(ROPE_HALF, dtype=F32) / ROPE_HALF)
    invf64 = jnp.concatenate([inv_freq, inv_freq, jnp.zeros((DIFF_DH - ROPE_DIM,), F32)])
    invf_row = jnp.tile(invf64, LANES // DIFF_DH).reshape(1, LANES)
    pos_col = positions.reshape(TOKENS, 1)
    lam_p = jnp.concatenate([diff_lq1, diff_lk1, diff_lq2, diff_lk2], axis=0)
    wr = (jnp.zeros((D_MODEL, LANES), F32)
          .at[:, :N_GROUPS].set(router_group_w[0])
          .at[:, N_GROUPS:N_GROUPS + N_EXPERTS].set(router_expert_w[0]))
    wr_hi = wr.astype(BF16)
    wr_lo = (wr - wr_hi.astype(F32)).astype(BF16)
    br = (jnp.zeros((1, LANES), F32)
          .at[0, :N_GROUPS].set(router_group_b[0])
          .at[0, N_GROUPS:N_GROUPS + N_EXPERTS].set(router_expert_b[0]))

    mod = _mod_call(c, ada_w[0], ada_b[0])
    gqk, gv, gr, lr, qT, kk, vT, sa, sb = _in_call(
        x2, mod, norm1_g, w_in_p, bd, qg_row, kg_row, pos_col, invf_row)
    o_a = _gla_call(gqk, gv, gr, lr, au_stack, gla_alpha_b, gla_out_g)
    o_b = _attn_call(qT, kk, vT, lam_p, diff_out_g.reshape(DIFF_DV, 1))
    x1, h2, rf, rank, cnt = _merge_call(
        x2, o_a, o_b, sa, sb, mod, norm2_g, w_branch_a[0].astype(BF16),
        w_branch_b[0].astype(BF16), w_out[0].astype(BF16), wr_hi, wr_lo, br)
    dest1, dest2, tables, n_used = _plan_call(rank, cnt)

    xs = _sc_scatter_rows(_as_row_tiles(h2), dest1, dest2, N_BLOCKS * MOE_BLK)
    y = _expert_call(tables, n_used, _as_2d(xs), expert_w1[0], expert_w3[0], expert_w2[0])
    y3 = _as_row_tiles(y)
    half = TOKENS // 2
    out = None
    for part in range(2):
        tok = slice(part * half, (part + 1) * half)
        yg = _sc_gather_rows(y3, jnp.concatenate([dest1[tok], dest2[tok]]))
        out = _combine_rows_call(x1, rf, mod, _as_2d(yg), part, out)
    return out.reshape(BATCH, SEQ, D_MODEL)
```

```python
import math

import jax
import jax.numpy as jnp
from jax import lax
from jax.experimental import pallas as pl
from jax.experimental.pallas import tpu as pltpu
from jax.experimental.pallas import tpu_sc as plsc

F32 = jnp.float32
BF16 = jnp.bfloat16
I32 = jnp.int32

D_MODEL = 1024
BATCH = 4
SEQ = 4096
TOKENS = BATCH * SEQ
N_MOD = 6
NORM_EPS = 1e-6

GLA_HEADS = 4
GLA_DK = 64
GLA_DV = 128
GLA_GATE_RANK = 16
GLA_GATE_TAU = 16.0
GLA_CHUNK = 64
GLA_QK_W = GLA_HEADS * GLA_DK
GLA_V_W = GLA_HEADS * GLA_DV

DIFF_HEADS = 4
DIFF_DH = 64
DIFF_DV = 2 * DIFF_DH
DIFF_QK_W = DIFF_HEADS * 2 * DIFF_DH
DIFF_V_W = DIFF_HEADS * DIFF_DV
ROPE_THETA = 500000.0
ROPE_DIM = DIFF_DH // 4
ROPE_HALF = ROPE_DIM // 2
NEG_INF = -1e30
LAMBDA_INIT = 0.8 - 0.6 * 1.0

N_GROUPS = 4
EXPERTS_PER_GROUP = 8
N_EXPERTS = N_GROUPS * EXPERTS_PER_GROUP
TOP_K = 2
D_EXPERT = 512

LANES = 128
SUBLANES = 8
ROW_TILE_S = D_MODEL // LANES
SC_CORES = 2
SC_SUBCORES = 16
SC_WORKERS = SC_CORES * SC_SUBCORES
SC_CHUNK = 64
LR_PAD = LANES
D_IN_PAD = 2 * GLA_QK_W + 2 * GLA_V_W + 2 * DIFF_QK_W + DIFF_V_W + 2 * D_MODEL + LR_PAD
IN_COLS = {}
_col = 0
for _name, _width in (("gla_qk", 2 * GLA_QK_W), ("gla_v", GLA_V_W), ("gla_r", GLA_V_W),
                      ("diff_q", DIFF_QK_W), ("diff_k", DIFF_QK_W), ("diff_v", DIFF_V_W),
                      ("gate_a", D_MODEL), ("gate_b", D_MODEL), ("gla_lr", LR_PAD)):
    IN_COLS[_name] = (_col, _col + _width)
    _col += _width
assert _col == D_IN_PAD
WPREP_COLS = 512
WPREP_LR_STEP = (D_IN_PAD - LR_PAD) // WPREP_COLS
D_IN_ALLOC = (WPREP_LR_STEP + 1) * WPREP_COLS

TM_IN = 512
TQ = 512
V_ROWS = DIFF_DV + 16
ATT_SUB = 32
Q_SCALE = DIFF_DH ** -0.5 * math.log2(math.e)
N_KV = SEQ // TQ
TT_GLA = 512
TM_MERGE = 512
TM_ROW = 512
MOE_BLK = 512
MOE_SUB = 256
N_BLOCKS = (TOKENS * TOP_K + N_EXPERTS * (MOE_BLK - 1) + MOE_BLK - 1) // MOE_BLK
ROUTE_W = 8

VMEM_LIMIT = 56 * 1024 * 1024


def _cparams(sem):
    return pltpu.CompilerParams(dimension_semantics=sem, vmem_limit_bytes=VMEM_LIMIT)


def _sigmoid(x):
    return 1.0 / (1.0 + jnp.exp(-x))


def _row_tile_spec(rows, index_map):
    return pl.BlockSpec((rows * ROW_TILE_S, LANES), index_map)


def _as_row_tiles(a2d):
    return a2d.reshape(a2d.shape[0] // ROW_TILE_S, ROW_TILE_S, LANES)


def _as_2d(a3d):
    return a3d.reshape(a3d.shape[0] * ROW_TILE_S, LANES)


def _store_row_tiles(ref, val, row0=0):
    for s in range(ROW_TILE_S):
        ref[pl.ds(row0 * ROW_TILE_S + s, val.shape[0], stride=ROW_TILE_S), :] = (
            val[:, s * LANES:(s + 1) * LANES])


def _load_row_tiles(ref, row0=0, rows=None):
    rows = ref.shape[0] // ROW_TILE_S if rows is None else rows
    return jnp.concatenate(
        [ref[pl.ds(row0 * ROW_TILE_S + s, rows, stride=ROW_TILE_S), :] for s in range(ROW_TILE_S)],
        axis=1)


def _mod_kernel(ct_ref, w_ref, b_ref, o_ref):
    ct = ct_ref[...]
    ca = ct * _sigmoid(ct)
    w = w_ref[...]
    rows = []
    for b in range(BATCH):
        rows.append(jnp.sum(w * ca[:, b:b + 1], axis=0, keepdims=True) + b_ref[...])
    rows.append(jnp.zeros((8 - BATCH, w.shape[1]), F32))
    o_ref[...] = jnp.concatenate(rows, axis=0)


def _mod_call(c, ada_w, ada_b):
    tn = D_MODEL
    ct = jnp.zeros((D_MODEL, 8), F32).at[:, :BATCH].set(c.T)
    out = pl.pallas_call(
        _mod_kernel,
        grid=(N_MOD,),
        in_specs=[
            pl.BlockSpec((D_MODEL, 8), lambda j: (0, 0)),
            pl.BlockSpec((D_MODEL, tn), lambda j: (0, j)),
            pl.BlockSpec((1, tn), lambda j: (0, j)),
        ],
        out_specs=pl.BlockSpec((8, tn), lambda j: (0, j)),
        out_shape=jax.ShapeDtypeStruct((8, N_MOD * D_MODEL), F32),
        compiler_params=_cparams(("arbitrary",)),
        name="mod",
    )(ct, ada_w, ada_b.reshape(1, N_MOD * D_MODEL))
    return out[:BATCH].reshape(BATCH, N_MOD, D_MODEL)


def _wprep_kernel(w_ref, o_ref):
    j = pl.program_id(0)
    t = w_ref[...].T
    lane = lax.broadcasted_iota(I32, t.shape, 1)
    @pl.when(j < WPREP_LR_STEP)
    def _():
        o_ref[...] = t.astype(BF16)

    @pl.when(j == WPREP_LR_STEP)
    def _():
        rep = jnp.where(lane < GLA_GATE_RANK, t,
                        jnp.where(lane < 2 * GLA_GATE_RANK, pltpu.roll(t, GLA_GATE_RANK, 1),
                                  jnp.where(lane < 3 * GLA_GATE_RANK,
                                            pltpu.roll(t, 2 * GLA_GATE_RANK, 1), 0.0)))
        o_ref[...] = rep.astype(BF16)


def _wprep_call(w_in0):
    w_t = w_in0.T
    lr0 = 2 * GLA_QK_W + 2 * GLA_V_W
    assert lr0 % WPREP_COLS == 0 and (D_IN_PAD - LR_PAD) % WPREP_COLS == 0

    def src_row(j):
        return jnp.where(j < lr0 // WPREP_COLS, j * WPREP_COLS,
                         jnp.where(j < WPREP_LR_STEP, j * WPREP_COLS + GLA_GATE_RANK, lr0))

    return pl.pallas_call(
        _wprep_kernel,
        grid=(WPREP_LR_STEP + 1,),
        in_specs=[pl.BlockSpec((pl.Element(WPREP_COLS), pl.Element(D_MODEL)),
                               lambda j: (pl.multiple_of(src_row(j), SUBLANES), 0))],
        out_specs=pl.BlockSpec((D_MODEL, WPREP_COLS), lambda j: (0, j)),
        out_shape=jax.ShapeDtypeStruct((D_MODEL, D_IN_ALLOC), BF16),
        compiler_params=_cparams(("parallel",)),
        name="w_prep",
    )(w_t)


def _in_kernel(x_ref, mod_ref, g1_ref, w_ref, bd_ref, qg_ref, kg_ref, pos_ref, invf_ref,
               gqk_ref, gv_ref, gr_ref, lr_ref, qT_ref, k_ref, vT_ref, sa_ref, sb_ref,
               cos_scr, sin_scr):
    ang = pos_ref[...].astype(F32) * invf_ref[...]
    cos_scr[...] = jnp.cos(ang)
    sin_scr[...] = jnp.sin(ang)

    x = x_ref[...]
    shift1 = mod_ref[0, 0:1, :]
    scale1 = mod_ref[0, 1:2, :]
    ms = jnp.mean(x * x, axis=-1, keepdims=True)
    h = (x * lax.rsqrt(ms + NORM_EPS) * g1_ref[...]) * (1.0 + scale1) + shift1
    hb = h.astype(BF16)

    def proj(name):
        c0, c1 = IN_COLS[name]
        return jnp.dot(hb, w_ref[:, c0:c1], preferred_element_type=F32)

    tm = x.shape[0]
    cos4 = jnp.concatenate([cos_scr[...]] * DIFF_HEADS, axis=1)
    sin4 = jnp.concatenate([sin_scr[...]] * DIFF_HEADS, axis=1)
    lane = lax.broadcasted_iota(I32, (tm, DIFF_QK_W), 1)
    first_half = (lane % DIFF_DH) < ROPE_HALF
    bd = bd_ref[...]

    def norm_rope(t, gain_row):
        t2 = t * t
        hi = t2.astype(BF16)
        lo = (t2 - hi.astype(F32)).astype(BF16)
        gms = (jnp.dot(hi, bd, preferred_element_type=F32)
               + jnp.dot(lo, bd, preferred_element_type=F32))
        t = t * lax.rsqrt(gms + NORM_EPS) * gain_row
        nxt = pltpu.roll(t, DIFF_QK_W - ROPE_HALF, 1)
        prv = pltpu.roll(t, ROPE_HALF, 1)
        return t * cos4 + jnp.where(first_half, -nxt, prv) * sin4

    dq_raw = proj("diff_q")
    dk_raw = proj("diff_k")
    gqk_ref[...] = proj("gla_qk").astype(BF16)
    dq = norm_rope(dq_raw, qg_ref[...]) * Q_SCALE
    gv_ref[...] = proj("gla_v").astype(BF16)
    dk = norm_rope(dk_raw, kg_ref[...])
    dv = proj("diff_v")
    for hd in range(DIFF_HEADS):
        sl = slice(hd * LANES, (hd + 1) * LANES)
        qT_ref[0, hd, 0] = dq[:, sl].T.astype(BF16)
        k_ref[0, hd, 0] = dk[:, sl].astype(BF16)
    gr_ref[...] = proj("gla_r").astype(BF16)
    for hd in range(DIFF_HEADS):
        sl = slice(hd * LANES, (hd + 1) * LANES)
        vT_ref[0, hd, 0] = jnp.concatenate(
            [dv[:, sl].T, jnp.ones((V_ROWS - DIFF_DV, tm), F32)], axis=0).astype(BF16)
    sa_ref[...] = _sigmoid(proj("gate_a")).astype(BF16)
    sb_ref[...] = _sigmoid(proj("gate_b")).astype(BF16)
    lr_ref[...] = proj("gla_lr")


def _in_call(x2, mod, norm1_g, w_in_p, bd, qg_row, kg_row, pos_col, invf_row):
    nb = SEQ // TM_IN
    tok_spec = lambda w: pl.BlockSpec((TM_IN, w), lambda i: (i, 0))
    const2 = lambda r, c: pl.BlockSpec((r, c), lambda i: (0, 0))
    out_shapes = (
        jax.ShapeDtypeStruct((TOKENS, 2 * GLA_QK_W), BF16),
        jax.ShapeDtypeStruct((TOKENS, GLA_V_W), BF16),
        jax.ShapeDtypeStruct((TOKENS, GLA_V_W), BF16),
        jax.ShapeDtypeStruct((TOKENS, LR_PAD), F32),
        jax.ShapeDtypeStruct((BATCH, DIFF_HEADS, N_KV, LANES, TQ), BF16),
        jax.ShapeDtypeStruct((BATCH, DIFF_HEADS, N_KV, TQ, LANES), BF16),
        jax.ShapeDtypeStruct((BATCH, DIFF_HEADS, N_KV, V_ROWS, TQ), BF16),
        jax.ShapeDtypeStruct((TOKENS, D_MODEL), BF16),
        jax.ShapeDtypeStruct((TOKENS, D_MODEL), BF16),
    )
    out_specs = (
        tok_spec(2 * GLA_QK_W), tok_spec(GLA_V_W), tok_spec(GLA_V_W), tok_spec(LR_PAD),
        pl.BlockSpec((1, DIFF_HEADS, 1, LANES, TM_IN), lambda i: (i // nb, 0, i % nb, 0, 0)),
        pl.BlockSpec((1, DIFF_HEADS, 1, TM_IN, LANES), lambda i: (i // nb, 0, i % nb, 0, 0)),
        pl.BlockSpec((1, DIFF_HEADS, 1, V_ROWS, TM_IN), lambda i: (i // nb, 0, i % nb, 0, 0)),
        tok_spec(D_MODEL), tok_spec(D_MODEL),
    )
    return pl.pallas_call(
        _in_kernel,
        grid=(TOKENS // TM_IN,),
        in_specs=[
            tok_spec(D_MODEL),
            pl.BlockSpec((1, N_MOD, D_MODEL), lambda i: (i // nb, 0, 0)),
            const2(1, D_MODEL),
            pl.BlockSpec((D_MODEL, D_IN_ALLOC), lambda i: (0, 0), pipeline_mode=pl.Buffered(1)),
            const2(DIFF_QK_W, DIFF_QK_W),
            const2(1, DIFF_QK_W), const2(1, DIFF_QK_W),
            tok_spec(1),
            const2(1, LANES),
        ],
        out_specs=out_specs,
        out_shape=out_shapes,
        scratch_shapes=[pltpu.VMEM((TM_IN, LANES), F32), pltpu.VMEM((TM_IN, LANES), F32)],
        compiler_params=_cparams(("parallel",)),
        name="in_proj",
    )(x2, mod, norm1_g, w_in_p, bd, qg_row, kg_row, pos_col, invf_row)


def _gla_kernel(qk_ref, v_ref, r_ref, lr_ref, au_ref, ab_ref, og_ref, o_ref,
                state_ref, oacc_ref, snew_ref):
    tt = qk_ref.shape[0]
    n_chunks = tt // GLA_CHUNK

    @pl.when(pl.program_id(1) == 0)
    def _():
        state_ref[...] = jnp.zeros_like(state_ref)

    lr = lr_ref[...]
    lr_hi = lr.astype(BF16)
    lr_lo = (lr - lr_hi.astype(F32)).astype(BF16)
    lr_lane = lax.broadcasted_iota(I32, lr.shape, 1)
    in_lo_group = (lr_lane >= GLA_GATE_RANK) & (lr_lane < 2 * GLA_GATE_RANK)
    stacked = jnp.where(in_lo_group, lr_lo, lr_hi)
    z = jnp.dot(stacked, au_ref[...], preferred_element_type=F32) + ab_ref[...]
    g = (jnp.minimum(z, 0.0) - jnp.log(1.0 + jnp.exp(-jnp.abs(z)))) * (1.0 / GLA_GATE_TAU)

    row = lax.broadcasted_iota(I32, (tt, GLA_QK_W), 0) % GLA_CHUNK
    b = g
    step = 1
    while step < GLA_CHUNK:
        b = b + jnp.where(row >= step, pltpu.roll(b, step, 0), 0.0)
        step *= 2

    b_last_rows = [b[c * GLA_CHUNK + GLA_CHUNK - 1:(c + 1) * GLA_CHUNK, :] for c in range(n_chunks)]
    b_last = jnp.concatenate(
        [jnp.broadcast_to(bl, (GLA_CHUNK, GLA_QK_W)) for bl in b_last_rows], axis=0)

    qk = qk_ref[...].astype(F32)
    q = qk[:, :GLA_QK_W] * (GLA_DK ** -0.5)
    k = qk[:, GLA_QK_W:]
    q_in = (q * jnp.exp(b)).astype(BF16)
    k_in = (k * jnp.exp(-b)).astype(BF16)
    k_dec = (k * jnp.exp(b_last - b)).astype(BF16)

    ci = lax.broadcasted_iota(I32, (GLA_CHUNK, GLA_CHUNK), 0)
    cj = lax.broadcasted_iota(I32, (GLA_CHUNK, GLA_CHUNK), 1)
    causal = ci >= cj

    pairs = [(c, hd) for c in range(n_chunks) for hd in range(GLA_HEADS)]

    def rows(c):
        return slice(c * GLA_CHUNK, (c + 1) * GLA_CHUNK)

    def kcols(hd):
        return slice(hd * GLA_DK, (hd + 1) * GLA_DK)

    def vcols(hd):
        return slice(hd * GLA_DV, (hd + 1) * GLA_DV)

    att = {}
    for c, hd in pairs:
        a = lax.dot_general(q_in[rows(c), kcols(hd)], k_in[rows(c), kcols(hd)],
                            (((1,), (1,)), ((), ())), preferred_element_type=F32)
        att[c, hd] = jnp.where(causal, a, 0.0).astype(BF16)
    for c, hd in pairs:
        oacc_ref[rows(c), vcols(hd)] = jnp.dot(att[c, hd], v_ref[rows(c), vcols(hd)],
                                               preferred_element_type=F32)
    for c, hd in pairs:
        snew_ref[c * GLA_HEADS + hd] = lax.dot_general(
            k_dec[rows(c), kcols(hd)], v_ref[rows(c), vcols(hd)],
            (((0,), (0,)), ((), ())), preferred_element_type=F32)

    decay_rows = jnp.exp(jnp.concatenate(
        b_last_rows + [jnp.zeros((LANES - n_chunks, GLA_QK_W), F32)], axis=0))
    decay_cols = decay_rows.T
    states = [state_ref[hd] for hd in range(GLA_HEADS)]
    for c in range(n_chunks):
        for hd in range(GLA_HEADS):
            s_prev = states[hd]
            oacc_ref[rows(c), vcols(hd)] += jnp.dot(
                q_in[rows(c), kcols(hd)], s_prev.astype(BF16), preferred_element_type=F32)
            dcol = decay_cols[kcols(hd), c:c + 1]
            states[hd] = s_prev * dcol + snew_ref[c * GLA_HEADS + hd]
    for hd in range(GLA_HEADS):
        state_ref[hd] = states[hd]

    for hd in range(GLA_HEADS):
        vs = slice(hd * GLA_DV, (hd + 1) * GLA_DV)
        oh = oacc_ref[:, vs]
        ms = jnp.mean(oh * oh, axis=-1, keepdims=True)
        y = oh * lax.rsqrt(ms + NORM_EPS) * og_ref[...]
        r = r_ref[:, vs].astype(F32)
        o_ref[:, vs] = (y * (r * _sigmoid(r))).astype(BF16)


def _gla_call(gqk, gv, gr, lr, au_stack, ab_row, og_row):
    nt = SEQ // TT_GLA
    tok = lambda w: pl.BlockSpec((TT_GLA, w), lambda b, t: (b * nt + t, 0))
    const2 = lambda r, c: pl.BlockSpec((r, c), lambda b, t: (0, 0))
    return pl.pallas_call(
        _gla_kernel,
        grid=(BATCH, nt),
        in_specs=[tok(2 * GLA_QK_W), tok(GLA_V_W), tok(GLA_V_W), tok(LR_PAD),
                  const2(LR_PAD, GLA_QK_W), const2(1, GLA_QK_W), const2(1, GLA_DV)],
        out_specs=tok(GLA_V_W),
        out_shape=jax.ShapeDtypeStruct((TOKENS, GLA_V_W), BF16),
        scratch_shapes=[pltpu.VMEM((GLA_HEADS, GLA_DK, GLA_DV), F32),
                        pltpu.VMEM((TT_GLA, GLA_V_W), F32),
                        pltpu.VMEM((TT_GLA // GLA_CHUNK * GLA_HEADS, GLA_DK, GLA_DV), F32)],
        compiler_params=_cparams(("parallel", "arbitrary")),
        name="gla",
    )(gqk, gv, gr, lr, au_stack, ab_row, og_row)


def _attn_kernel(qT_ref, k_ref, vT_ref, lam_ref, og_ref, o_ref,
                 q_scr, s_scr, p_scr, acc_scr, m_scr):
    _attn_load_q(qT_ref, q_scr, 0, 0)
    s_scr[0] = jnp.dot(k_ref[0, 0, 0], q_scr[0, 0], preferred_element_type=F32)

    def q_block(i, carry):
        _attn_q_block(i, qT_ref, k_ref, vT_ref, lam_ref, og_ref, o_ref,
                      q_scr, s_scr, p_scr, acc_scr, m_scr)
        return carry

    lax.fori_loop(0, N_KV, q_block, 0)


def _attn_load_q(qT_ref, q_scr, i, slot):
    qT = qT_ref[0, 0, i]
    rowq = lax.broadcasted_iota(I32, qT.shape, 0)
    zero = jnp.zeros_like(qT)
    q_scr[slot, 0] = jnp.where(rowq < DIFF_DH, qT, zero)
    q_scr[slot, 1] = jnp.where(rowq >= DIFF_DH, qT, zero)


def _attn_q_block(i, qT_ref, k_ref, vT_ref, lam_ref, og_ref, o_ref,
                  q_scr, s_scr, p_scr, acc_scr, m_scr):
    slot = i % 2
    m_scr[...] = jnp.full(m_scr.shape, NEG_INF, F32)
    acc_scr[...] = jnp.zeros(acc_scr.shape, F32)
    n_sub = TQ // ATT_SUB

    def fold8(t, op):
        return op(t.reshape(t.shape[0] // SUBLANES, SUBLANES, TQ), axis=0)

    def scores(c, j):
        s_scr[c] = jnp.dot(k_ref[0, 0, j], q_scr[slot, c], preferred_element_type=F32)

    def load_s(c, r, masked):
        s = s_scr[c, r * ATT_SUB:(r + 1) * ATT_SUB, :]
        if masked:
            key_i = lax.broadcasted_iota(I32, (ATT_SUB, TQ), 0) + r * ATT_SUB
            qry_i = lax.broadcasted_iota(I32, (ATT_SUB, TQ), 1)
            s = jnp.where(key_i <= qry_i, s, NEG_INF)
        return s

    def softmax_pv(c, j, masked):
        m8 = fold8(load_s(c, 0, masked), jnp.max)
        for r in range(1, n_sub):
            m8 = jnp.maximum(m8, fold8(load_s(c, r, masked), jnp.max))
        m_old = m_scr[c]
        m_new = jnp.maximum(m_old, jnp.max(m8, axis=0, keepdims=True))
        alpha = jnp.exp2(m_old - m_new)
        for r in range(n_sub):
            p = jnp.exp2(load_s(c, r, masked) - m_new)
            p_scr[c, r * ATT_SUB:(r + 1) * ATT_SUB, :] = p.astype(BF16)
        m_scr[c] = m_new
        acc_scr[c] = acc_scr[c] * alpha + jnp.dot(vT_ref[0, 0, j], p_scr[c],
                                                  preferred_element_type=F32)

    def body(j, carry):
        scores(1, j)
        softmax_pv(0, j, False)
        scores(0, j + 1)
        softmax_pv(1, j, False)
        return carry

    lax.fori_loop(0, i, body, 0)
    scores(1, i)
    softmax_pv(0, i, True)
    softmax_pv(1, i, True)

    nxt = jnp.minimum(i + 1, N_KV - 1)
    _attn_load_q(qT_ref, q_scr, nxt, 1 - slot)
    s_scr[0] = jnp.dot(k_ref[0, 0, 0], q_scr[1 - slot, 0], preferred_element_type=F32)

    l1 = acc_scr[0, DIFF_DV:DIFF_DV + 1, :]
    l2 = acc_scr[1, DIFF_DV:DIFF_DV + 1, :]

    lam_p = lam_ref[...]
    lam = (jnp.exp(jnp.sum(lam_p[0:1] * lam_p[1:2], axis=1, keepdims=True))
           - jnp.exp(jnp.sum(lam_p[2:3] * lam_p[3:4], axis=1, keepdims=True)) + LAMBDA_INIT)
    oT = acc_scr[0, :DIFF_DV, :] / l1 - lam * (acc_scr[1, :DIFF_DV, :] / l2)
    ms = jnp.mean(oT * oT, axis=0, keepdims=True)
    y = oT * lax.rsqrt(ms + NORM_EPS) * og_ref[...] * (1.0 - LAMBDA_INIT)
    o_ref[pl.ds(pl.multiple_of(i * TQ, TQ), TQ), :] = y.T.astype(BF16)


def _attn_call(qT, kk, vT, lam_p, og_col):
    return pl.pallas_call(
        _attn_kernel,
        grid=(BATCH, DIFF_HEADS),
        in_specs=[
            pl.BlockSpec((1, 1, N_KV, LANES, TQ), lambda b, h: (b, h, 0, 0, 0)),
            pl.BlockSpec((1, 1, N_KV, TQ, LANES), lambda b, h: (b, h, 0, 0, 0)),
            pl.BlockSpec((1, 1, N_KV, V_ROWS, TQ), lambda b, h: (b, h, 0, 0, 0)),
            pl.BlockSpec((4, DIFF_DH), lambda b, h: (0, 0)),
            pl.BlockSpec((DIFF_DV, 1), lambda b, h: (0, 0)),
        ],
        out_specs=pl.BlockSpec((SEQ, DIFF_DV), lambda b, h: (b, h)),
        out_shape=jax.ShapeDtypeStruct((TOKENS, DIFF_V_W), BF16),
        scratch_shapes=[pltpu.VMEM((2, 2, LANES, TQ), BF16),
                        pltpu.VMEM((2, TQ, TQ), F32),
                        pltpu.VMEM((2, TQ, TQ), BF16),
                        pltpu.VMEM((2, V_ROWS, TQ), F32),
                        pltpu.VMEM((2, 1, TQ), F32)],
        compiler_params=_cparams(("parallel", "parallel")),
        name="attn",
    )(qT, kk, vT, lam_p, og_col)


def _merge_kernel(x_ref, oa_ref, ob_ref, sa_ref, sb_ref, mod_ref, g2_ref, wa_ref, wb_ref, wo_ref,
                  wr2_ref, br_ref, x1_ref, h2_ref, rf_ref, rank_ref, cnt_ref,
                  carry_ref, tri_ref, lg_ref):
    tm = x_ref.shape[0]
    step = pl.program_id(0)

    @pl.when(step == 0)
    def _():
        carry_ref[...] = jnp.zeros_like(carry_ref)
        lg_ref[...] = jnp.zeros_like(lg_ref)
        ti = lax.broadcasted_iota(I32, (tm, tm), 0)
        tj = lax.broadcasted_iota(I32, (tm, tm), 1)
        tri_ref[...] = jnp.where(ti > tj, 1.0, 0.0).astype(BF16)

    ma = jnp.dot(oa_ref[...], wa_ref[...], preferred_element_type=F32)
    mb = jnp.dot(ob_ref[...], wb_ref[...], preferred_element_type=F32)
    merged = sa_ref[...].astype(F32) * ma + sb_ref[...].astype(F32) * mb
    y = jnp.dot(merged.astype(BF16), wo_ref[...], preferred_element_type=F32)
    logits = lg_ref[...]
    lane = lax.broadcasted_iota(I32, (tm, LANES), 1).astype(F32)
    ninf = -jnp.inf
    big = float(LANES)

    def first_argmax(v):
        vmax = jnp.max(v, axis=1, keepdims=True)
        idx = jnp.min(jnp.where(v == vmax, lane, big), axis=1, keepdims=True)
        return vmax, idx

    gl = jnp.where(lane < N_GROUPS, logits, ninf)
    gmax, gidx = first_argmax(gl)
    p_top = 1.0 / jnp.sum(jnp.exp(gl - gmax), axis=1, keepdims=True)
    lo = N_GROUPS + EXPERTS_PER_GROUP * gidx
    el = jnp.where((lane >= lo) & (lane < lo + EXPERTS_PER_GROUP), logits, ninf)
    e1max, e1 = first_argmax(el)
    e2max, e2 = first_argmax(jnp.where(lane == e1, ninf, el))
    t = jnp.exp(e2max - e1max)
    w1 = 1.0 / (1.0 + t)
    w2 = t / (1.0 + t)
    col = lax.broadcasted_iota(I32, (tm, ROUTE_W), 1)
    rf_ref[...] = jnp.where(col == 0, p_top * w1, jnp.where(col == 1, p_top * w2, 0.0))

    x1id = e1 - N_GROUPS
    x2id = e2 - N_GROUPS
    hit1 = lane == x1id
    hit2 = lane == x2id
    onehot = jnp.where(hit1 | hit2, 1.0, 0.0)
    before = jnp.dot(tri_ref[...], onehot.astype(BF16), preferred_element_type=F32) + carry_ref[...]
    r1 = jnp.sum(jnp.where(hit1, before, 0.0), axis=1, keepdims=True)
    r2 = jnp.sum(jnp.where(hit2, before, 0.0), axis=1, keepdims=True)
    real_tile = jnp.where(step > 0, 1.0, 0.0)
    carry_ref[...] = carry_ref[...] + real_tile * jnp.sum(onehot, axis=0, keepdims=True)
    cnt_ref[...] = carry_ref[...]
    cols = jnp.where(lane == 0.0, x1id, jnp.where(lane == 1.0, x2id,
                                                  jnp.where(lane == 2.0, r1,
                                                            jnp.where(lane == 3.0, r2, 0.0))))
    rank_ref[0] = cols.T[0:ROUTE_W, :].astype(I32)

    gate1 = mod_ref[0, 2:3, :]
    shift2 = mod_ref[0, 3:4, :]
    scale2 = mod_ref[0, 4:5, :]
    x1 = x_ref[...] + gate1 * y
    x1_ref[...] = x1
    ms = jnp.mean(x1 * x1, axis=-1, keepdims=True)
    h2 = (x1 * lax.rsqrt(ms + NORM_EPS) * g2_ref[...]) * (1.0 + scale2) + shift2
    _store_row_tiles(h2_ref, h2)
    h2_hi = h2.astype(BF16)
    h2_lo = (h2 - h2_hi.astype(F32)).astype(BF16)
    hi_both = jnp.dot(h2_hi, wr2_ref[...], preferred_element_type=F32)
    next_logits = (hi_both[:, :LANES] + hi_both[:, LANES:]
                   + jnp.dot(h2_lo, wr2_ref[:, :LANES], preferred_element_type=F32)) + br_ref[...]

    lg_ref[...] = next_logits


def _merge_call(x2, o_a, o_b, sa, sb, mod, norm2_g, wa, wb, wo, wr2, br):
    nb = SEQ // TM_MERGE
    n_tiles = TOKENS // TM_MERGE
    cur = lambda i: jnp.minimum(i, n_tiles - 1)
    prev = lambda i: jnp.maximum(i - 1, 0)
    tok = lambda w: pl.BlockSpec((TM_MERGE, w), lambda i: (cur(i), 0))
    const2 = lambda r, c: pl.BlockSpec((r, c), lambda i: (0, 0))
    return pl.pallas_call(
        _merge_kernel,
        grid=(n_tiles + 1,),
        in_specs=[tok(D_MODEL), tok(GLA_V_W), tok(DIFF_V_W), tok(D_MODEL), tok(D_MODEL),
                  pl.BlockSpec((1, N_MOD, D_MODEL), lambda i: (cur(i) // nb, 0, 0)),
                  const2(1, D_MODEL),
                  const2(GLA_V_W, D_MODEL), const2(DIFF_V_W, D_MODEL), const2(D_MODEL, D_MODEL),
                  const2(D_MODEL, 2 * LANES), const2(1, LANES)],
        out_specs=(tok(D_MODEL), _row_tile_spec(TM_MERGE, lambda i: (cur(i), 0)),
                   pl.BlockSpec((TM_MERGE, ROUTE_W), lambda i: (prev(i), 0)),
                   pl.BlockSpec((1, ROUTE_W, TM_MERGE), lambda i: (prev(i), 0, 0)),
                   pl.BlockSpec((1, LANES), lambda i: (0, 0))),
        out_shape=(jax.ShapeDtypeStruct((TOKENS, D_MODEL), F32),
                   jax.ShapeDtypeStruct((TOKENS * ROW_TILE_S, LANES), F32),
                   jax.ShapeDtypeStruct((TOKENS, ROUTE_W), F32),
                   jax.ShapeDtypeStruct((n_tiles, ROUTE_W, TM_MERGE), I32),
                   jax.ShapeDtypeStruct((1, LANES), F32)),
        scratch_shapes=[pltpu.VMEM((1, LANES), F32), pltpu.VMEM((TM_MERGE, TM_MERGE), BF16),
                        pltpu.VMEM((TM_MERGE, LANES), F32)],
        compiler_params=_cparams(("arbitrary",)),
        name="merge",
    )(x2, o_a, o_b, sa, sb, mod, norm2_g, wa, wb, wo, wr2, br)


def _plan_kernel(rk_ref, cnt_ref, dest_ref, blk_ref):
    n_tiles, _, tm = rk_ref.shape
    cnt = cnt_ref[...]
    padded = jnp.floor((cnt + (MOE_BLK - 1)) * (1.0 / MOE_BLK)) * MOE_BLK
    lane = lax.broadcasted_iota(I32, (1, LANES), 1)
    seg_end = padded
    step = 1
    while step < N_EXPERTS:
        seg_end = seg_end + jnp.where(lane >= step, pltpu.roll(seg_end, step, 1), 0.0)
        step *= 2
    seg_start = seg_end - padded
    valid_end = seg_start + cnt

    ei = lax.broadcasted_iota(I32, (LANES, LANES), 0)
    ej = lax.broadcasted_iota(I32, (LANES, LANES), 1)

    def to_col(rowv):
        return jnp.sum(jnp.where(ei == ej, rowv, 0.0), axis=1, keepdims=True)

    start_col, end_col, valid_col = to_col(seg_start), to_col(seg_end), to_col(valid_end)

    nb_pad = blk_ref.shape[1]
    e_sub = lax.broadcasted_iota(I32, (LANES, nb_pad), 0)
    b_start = (lax.broadcasted_iota(I32, (1, nb_pad), 1) * MOE_BLK).astype(F32)
    ends_before = jnp.where((e_sub < N_EXPERTS) & (end_col <= b_start), 1.0, 0.0)
    block_e = jnp.minimum(jnp.sum(ends_before, axis=0, keepdims=True), N_EXPERTS - 1.0)
    block_valid_end = jnp.sum(jnp.where(e_sub.astype(F32) == block_e, valid_col, 0.0),
                              axis=0, keepdims=True)
    n_valid = jnp.clip(block_valid_end - b_start, 0.0, float(MOE_BLK))
    n_used = jnp.max(seg_end, axis=1, keepdims=True) * (1.0 / MOE_BLK)
    nonempty = (to_col(cnt) > 0.0) & (e_sub < N_EXPERTS)
    e_subf = e_sub.astype(F32)
    no_next = float(LANES)
    nxt = jnp.min(jnp.where(nonempty & (e_subf > block_e), e_subf, no_next), axis=0, keepdims=True)
    nxt = jnp.where(nxt == no_next, -1.0, nxt)
    seg_idx = jnp.sum(jnp.where(nonempty & (e_subf < block_e), 1.0, 0.0), axis=0, keepdims=True)
    slot = seg_idx - 2.0 * jnp.floor(seg_idx * 0.5)
    blk_ref[...] = jnp.concatenate(
        [block_e, n_valid, jnp.broadcast_to(n_used, (1, nb_pad)), nxt, slot,
         jnp.zeros((ROUTE_W - 5, nb_pad), F32)], axis=0).astype(I32)

    e_tok = lax.broadcasted_iota(I32, (LANES, tm), 0)

    def tile(t, carry):
        rk = rk_ref[t]
        d = [jnp.sum(jnp.where(e_tok == rk[k:k + 1], start_col, 0.0), axis=0, keepdims=True)
             + rk[k + 2:k + 3].astype(F32) for k in range(TOP_K)]
        dest_ref[t] = jnp.concatenate(d + [jnp.zeros((ROUTE_W - TOP_K, tm), F32)], axis=0).astype(I32)
        return carry

    lax.fori_loop(0, n_tiles, tile, 0)


def _plan_call(rank_rows, cnt):
    n_tiles = TOKENS // TM_MERGE
    nb_pad = -(-N_BLOCKS // LANES) * LANES
    dest, blk = pl.pallas_call(
        _plan_kernel,
        out_shape=(jax.ShapeDtypeStruct((n_tiles, ROUTE_W, TM_MERGE), I32),
                   jax.ShapeDtypeStruct((ROUTE_W, nb_pad), I32)),
        compiler_params=pltpu.CompilerParams(vmem_limit_bytes=VMEM_LIMIT),
        name="plan",
    )(rank_rows, cnt)
    dest1 = dest[:, 0, :].reshape(TOKENS)
    dest2 = dest[:, 1, :].reshape(TOKENS)
    tables = tuple(blk[r, :N_BLOCKS] for r in (0, 1, 3, 4))
    return dest1, dest2, tables, blk[2, :1]


def _sc_params():
    return pltpu.CompilerParams(use_tc_tiling_on_sc=True)


def _sc_mesh():
    return plsc.VectorSubcoreMesh(core_axis_name="core", subcore_axis_name="subcore")


def _sc_worker_base(per_worker):
    wid = lax.axis_index("subcore") * SC_CORES + lax.axis_index("core")
    return wid * per_worker


def _sc_scatter_rows(src, dest1, dest2, n_out):
    n = src.shape[0]
    per_worker = n // SC_WORKERS
    assert per_worker * SC_WORKERS == n and per_worker % SC_CHUNK == 0

    def body(src_hbm, d1_hbm, d2_hbm, out_hbm, idx_v, rows_v):
        base = _sc_worker_base(per_worker)

        @pl.loop(0, per_worker // SC_CHUNK)
        def _(j):
            start = pl.multiple_of(base + j * SC_CHUNK, SC_CHUNK)
            pltpu.sync_copy(src_hbm.at[pl.ds(start, SC_CHUNK)], rows_v)
            for d_hbm in (d1_hbm, d2_hbm):
                pltpu.sync_copy(d_hbm.at[pl.ds(start, SC_CHUNK)], idx_v)
                pltpu.sync_copy(rows_v, out_hbm.at[idx_v])

    return pl.kernel(
        body,
        out_type=jax.ShapeDtypeStruct((n_out, ROW_TILE_S, LANES), F32),
        mesh=_sc_mesh(),
        scratch_types=[pltpu.VMEM((SC_CHUNK,), I32),
                       pltpu.VMEM((SC_CHUNK, ROW_TILE_S, LANES), F32)],
        compiler_params=_sc_params(),
        name="sc_dispatch",
    )(src, dest1, dest2)


def _sc_gather_rows(table, idx):
    n = idx.shape[0]
    per_worker = n // SC_WORKERS
    half = SC_CHUNK // 2
    assert per_worker * SC_WORKERS == n and per_worker % SC_CHUNK == 0

    def body(table_hbm, idx_hbm, out_hbm, idx_v, rows_a, rows_b, gsem_a, gsem_b, wsem_a, wsem_b):
        base = pl.multiple_of(_sc_worker_base(per_worker), SC_CHUNK)
        pltpu.sync_copy(idx_hbm.at[pl.ds(base, per_worker)], idx_v)

        @pl.loop(0, per_worker // SC_CHUNK)
        def _(j):
            off_a = pl.multiple_of(j * SC_CHUNK, SC_CHUNK)
            off_b = pl.multiple_of(j * SC_CHUNK + half, half)
            ga = pltpu.async_copy(table_hbm.at[idx_v.at[pl.ds(off_a, half)]], rows_a, gsem_a)
            gb = pltpu.async_copy(table_hbm.at[idx_v.at[pl.ds(off_b, half)]], rows_b, gsem_b)
            ga.wait()
            wa = pltpu.async_copy(rows_a, out_hbm.at[pl.ds(base + off_a, half)], wsem_a)
            gb.wait()
            wb = pltpu.async_copy(rows_b, out_hbm.at[pl.ds(base + off_b, half)], wsem_b)
            wa.wait()
            wb.wait()

    return pl.kernel(
        body,
        out_type=jax.ShapeDtypeStruct((n, ROW_TILE_S, LANES), F32),
        mesh=_sc_mesh(),
        scratch_types=[pltpu.VMEM((per_worker,), I32),
                       pltpu.VMEM((half, ROW_TILE_S, LANES), F32),
                       pltpu.VMEM((half, ROW_TILE_S, LANES), F32),
                       pltpu.SemaphoreType.DMA, pltpu.SemaphoreType.DMA,
                       pltpu.SemaphoreType.DMA, pltpu.SemaphoreType.DMA],
        compiler_params=_sc_params(),
        name="sc_gather",
    )(table, idx)


def _weight_fetch(w_hbm, stage, sem, e, slot):
    return [pltpu.make_async_copy(w.at[e], st.at[slot], sem.at[slot]) for w, st in zip(w_hbm, stage)]


def _expert_kernel(be_ref, nv_ref, nx_ref, sl_ref, nu_ref, xs_ref, w1_hbm, w3_hbm, w2_hbm, y_ref,
                   w1s, w3s, w2s, w1b, w3b, w2b, sem):
    i = pl.program_id(0)
    e = be_ref[i]
    used = i < nu_ref[0]
    first = (i == 0) | (e != be_ref[jnp.maximum(i - 1, 0)])
    w_hbm = (w1_hbm, w3_hbm, w2_hbm)
    stage = (w1s, w3s, w2s)

    @pl.when(used & first)
    def _():
        slot = sl_ref[i]

        @pl.when(i == 0)
        def _():
            for cp in _weight_fetch(w_hbm, stage, sem, e, slot):
                cp.start()

        for cp in _weight_fetch(w_hbm, stage, sem, e, slot):
            cp.wait()
        w1b[...] = w1s[slot].astype(BF16)
        w3b[...] = w3s[slot].astype(BF16)
        w2b[...] = w2s[slot].astype(BF16)

        @pl.when(nx_ref[i] >= 0)
        def _():
            for cp in _weight_fetch(w_hbm, stage, sem, nx_ref[i], 1 - slot):
                cp.start()

    n_parts = MOE_BLK // MOE_SUB

    def load_x(part):
        row0 = part * MOE_SUB
        row = lax.broadcasted_iota(I32, (MOE_SUB, D_MODEL), 0) + row0
        return jnp.where(row < nv_ref[i], _load_row_tiles(xs_ref, row0, MOE_SUB), 0.0).astype(BF16)

    def zero_y(part):
        y_ref[pl.ds(part * MOE_SUB * ROW_TILE_S, MOE_SUB * ROW_TILE_S), :] = jnp.zeros(
            (MOE_SUB * ROW_TILE_S, LANES), F32)

    def run(parts):
        hmid = []
        for part in parts:
            xb = load_x(part)
            a = jnp.dot(xb, w1b[...], preferred_element_type=F32)
            g = jnp.dot(xb, w3b[...], preferred_element_type=F32)
            hmid.append(((a * _sigmoid(a)) * g).astype(BF16))
        for part, hm in zip(parts, hmid):
            _store_row_tiles(y_ref, jnp.dot(hm, w2b[...], preferred_element_type=F32),
                             part * MOE_SUB)

    n_live = jnp.where(used, (nv_ref[i] + (MOE_SUB - 1)) // MOE_SUB, 0)
    for k in range(n_parts + 1):
        @pl.when(n_live == k)
        def _(k=k):
            if k:
                run(list(range(k)))
            for part in range(k, n_parts):
                zero_y(part)


def _expert_call(tables, n_used, xs, w1, w3, w2):
    n_tab = len(tables)
    grid_spec = pltpu.PrefetchScalarGridSpec(
        num_scalar_prefetch=n_tab + 1,
        grid=(N_BLOCKS,),
        in_specs=[_row_tile_spec(MOE_BLK, lambda i, *pf: (jnp.minimum(i, pf[n_tab][0] - 1), 0)),
                  pl.BlockSpec(memory_space=pl.ANY),
                  pl.BlockSpec(memory_space=pl.ANY),
                  pl.BlockSpec(memory_space=pl.ANY)],
        out_specs=_row_tile_spec(MOE_BLK, lambda i, *pf: (i, 0)),
        scratch_shapes=[pltpu.VMEM((2, D_MODEL, D_EXPERT), F32),
                        pltpu.VMEM((2, D_MODEL, D_EXPERT), F32),
                        pltpu.VMEM((2, D_EXPERT, D_MODEL), F32),
                        pltpu.VMEM((D_MODEL, D_EXPERT), BF16),
                        pltpu.VMEM((D_MODEL, D_EXPERT), BF16),
                        pltpu.VMEM((D_EXPERT, D_MODEL), BF16),
                        pltpu.SemaphoreType.DMA((2,))],
    )
    return pl.pallas_call(
        _expert_kernel,
        grid_spec=grid_spec,
        out_shape=jax.ShapeDtypeStruct((N_BLOCKS * MOE_BLK * ROW_TILE_S, LANES), F32),
        compiler_params=_cparams(("arbitrary",)),
        name="experts",
    )(*tables, n_used, xs, w1, w3, w2)


def _combine_rows_kernel(x1_ref, rf_ref, mod_ref, ya_ref, yb_ref, *rest):
    o_ref = rest[-1]
    rf = rf_ref[...]
    gate2 = mod_ref[0, 5:6, :]
    moe = rf[:, 0:1] * _load_row_tiles(ya_ref) + rf[:, 1:2] * _load_row_tiles(yb_ref)
    o_ref[...] = x1_ref[...] + gate2 * moe


def _combine_rows_call(x1, rf, mod, yg, part, prev=None):
    nb = SEQ // TM_ROW
    n_half = TOKENS // TM_ROW // 2
    t0 = part * n_half
    in_specs = [pl.BlockSpec((TM_ROW, D_MODEL), lambda i: (t0 + i, 0)),
                pl.BlockSpec((TM_ROW, ROUTE_W), lambda i: (t0 + i, 0)),
                pl.BlockSpec((1, N_MOD, D_MODEL), lambda i: ((t0 + i) // nb, 0, 0)),
                _row_tile_spec(TM_ROW, lambda i: (i, 0)),
                _row_tile_spec(TM_ROW, lambda i: (i + n_half, 0))]
    args = [x1, rf, mod, yg, yg]
    aliases = {}
    if prev is not None:
        in_specs.append(pl.BlockSpec(memory_space=pl.ANY))
        args.append(prev)
        aliases = {len(args) - 1: 0}
    return pl.pallas_call(
        _combine_rows_kernel,
        grid=(n_half,),
        in_specs=in_specs,
        out_specs=pl.BlockSpec((TM_ROW, D_MODEL), lambda i: (t0 + i, 0)),
        out_shape=jax.ShapeDtypeStruct((TOKENS, D_MODEL), F32),
        input_output_aliases=aliases,
        compiler_params=_cparams(("parallel",)),
        name="combine",
    )(*args)


def kernel(x, c, positions, ada_w, ada_b, norm1_g, norm2_g, w_in, gla_alpha_up, gla_alpha_b,
           gla_out_g, diff_q_g, diff_k_g, diff_lq1, diff_lk1, diff_lq2, diff_lk2, diff_out_g,
           w_branch_a, w_branch_b, w_out, router_group_w, router_group_b, router_expert_w,
           router_expert_b, expert_w1, expert_w3, expert_w2):
    assert x.shape == (BATCH, SEQ, D_MODEL) and ada_w.shape[0] == 1
    x2 = x.reshape(TOKENS, D_MODEL)

    w_in_p = _wprep_call(w_in[0])
    au_hi = gla_alpha_up[0].astype(BF16)
    au_lo = (gla_alpha_up[0] - au_hi.astype(F32)).astype(BF16)
    au_stack = jnp.concatenate(
        [au_hi, au_hi, au_lo, jnp.zeros((LR_PAD - 3 * GLA_GATE_RANK, GLA_QK_W), BF16)], axis=0)
    gid = jnp.arange(DIFF_QK_W) // DIFF_DH
    bd = jnp.where(gid[:, None] == gid[None, :], 1.0 / DIFF_DH, 0.0).astype(BF16)
    qg_row = jnp.tile(diff_q_g[0], DIFF_QK_W // DIFF_DH).reshape(1, DIFF_QK_W)
    kg_row = jnp.tile(diff_k_g[0], DIFF_QK_W // DIFF_DH).reshape(1, DIFF_QK_W)
    inv_freq = ROPE_THETA ** (-jnp.arange(ROPE_HALF, dtype=F32) / ROPE_HALF)
    invf64 = jnp.concatenate([inv_freq, inv_freq, jnp.zeros((DIFF_DH - ROPE_DIM,), F32)])
    invf_row = jnp.tile(invf64, LANES // DIFF_DH).reshape(1, LANES)
    pos_col = positions.reshape(TOKENS, 1)
    lam_p = jnp.concatenate([diff_lq1, diff_lk1, diff_lq2, diff_lk2], axis=0)
    wr = (jnp.zeros((D_MODEL, LANES), F32)
          .at[:, :N_GROUPS].set(router_group_w[0])
          .at[:, N_GROUPS:N_GROUPS + N_EXPERTS].set(router_expert_w[0]))
    wr_hi = wr.astype(BF16)
    wr2 = jnp.concatenate([wr_hi, (wr - wr_hi.astype(F32)).astype(BF16)], axis=1)
    br = (jnp.zeros((1, LANES), F32)
          .at[0, :N_GROUPS].set(router_group_b[0])
          .at[0, N_GROUPS:N_GROUPS + N_EXPERTS].set(router_expert_b[0]))

    mod = _mod_call(c, ada_w[0], ada_b[0])
    gqk, gv, gr, lr, qT, kk, vT, sa, sb = _in_call(
        x2, mod, norm1_g, w_in_p, bd, qg_row, kg_row, pos_col, invf_row)
    o_a = _gla_call(gqk, gv, gr, lr, au_stack, gla_alpha_b, gla_out_g)
    o_b = _attn_call(qT, kk, vT, lam_p, diff_out_g.reshape(DIFF_DV, 1))
    x1, h2, rf, rank, cnt = _merge_call(
        x2, o_a, o_b, sa, sb, mod, norm2_g, w_branch_a[0].astype(BF16),
        w_branch_b[0].astype(BF16), w_out[0].astype(BF16), wr2, br)
    dest1, dest2, tables, n_used = _plan_call(rank, cnt)

    xs = _sc_scatter_rows(_as_row_tiles(h2), dest1, dest2, N_BLOCKS * MOE_BLK)
    y = _expert_call(tables, n_used, _as_2d(xs), expert_w1[0], expert_w3[0], expert_w2[0])
    y3 = _as_row_tiles(y)
    half = TOKENS // 2
    out = None
    for part in range(2):
        tok = slice(part * half, (part + 1) * half)
        yg = _sc_gather_rows(y3, jnp.concatenate([dest1[tok], dest2[tok]]))
        out = _combine_rows_call(x1, rf, mod, _as_2d(yg), part, out)
    return out.reshape(BATCH, SEQ, D_MODEL)
```

```python
import math

import jax
import jax.numpy as jnp
from jax import lax
from jax.experimental import pallas as pl
from jax.experimental.pallas import tpu as pltpu
from jax.experimental.pallas import tpu_sc as plsc

F32 = jnp.float32
BF16 = jnp.bfloat16
I32 = jnp.int32

D_MODEL = 1024
BATCH = 4
SEQ = 4096
TOKENS = BATCH * SEQ
N_MOD = 6
NORM_EPS = 1e-6

GLA_HEADS = 4
GLA_DK = 64
GLA_DV = 128
GLA_GATE_RANK = 16
GLA_GATE_TAU = 16.0
GLA_CHUNK = 64
GLA_QK_W = GLA_HEADS * GLA_DK
GLA_V_W = GLA_HEADS * GLA_DV

DIFF_HEADS = 4
DIFF_DH = 64
DIFF_DV = 2 * DIFF_DH
DIFF_QK_W = DIFF_HEADS * 2 * DIFF_DH
DIFF_V_W = DIFF_HEADS * DIFF_DV
ROPE_THETA = 500000.0
ROPE_DIM = DIFF_DH // 4
ROPE_HALF = ROPE_DIM // 2
NEG_INF = -1e30
LAMBDA_INIT = 0.8 - 0.6 * 1.0

N_GROUPS = 4
EXPERTS_PER_GROUP = 8
N_EXPERTS = N_GROUPS * EXPERTS_PER_GROUP
TOP_K = 2
D_EXPERT = 512

LANES = 128
SUBLANES = 8
ROW_TILE_S = D_MODEL // LANES
SC_CORES = 2
SC_SUBCORES = 16
SC_WORKERS = SC_CORES * SC_SUBCORES
SC_CHUNK = 64
LR_PAD = LANES
D_IN_PAD = 2 * GLA_QK_W + 2 * GLA_V_W + 2 * DIFF_QK_W + DIFF_V_W + 2 * D_MODEL + LR_PAD
IN_COLS = {}
_col = 0
for _name, _width in (("gla_qk", 2 * GLA_QK_W), ("gla_v", GLA_V_W), ("gla_r", GLA_V_W),
                      ("diff_q", DIFF_QK_W), ("diff_k", DIFF_QK_W), ("diff_v", DIFF_V_W),
                      ("gate_a", D_MODEL), ("gate_b", D_MODEL), ("gla_lr", LR_PAD)):
    IN_COLS[_name] = (_col, _col + _width)
    _col += _width
assert _col == D_IN_PAD
WPREP_COLS = 512
WPREP_LR_STEP = (D_IN_PAD - LR_PAD) // WPREP_COLS
D_IN_ALLOC = (WPREP_LR_STEP + 1) * WPREP_COLS

TM_IN = 512
TQ = 512
V_ROWS = DIFF_DV + 16
NORM_BLK = 256
ATT_SUB = 32
Q_SCALE = DIFF_DH ** -0.5 * math.log2(math.e)
N_KV = SEQ // TQ
TT_GLA = 512
TM_MERGE = 512
TM_ROW = 512
MOE_BLK = 512
MOE_SUB = 256
N_BLOCKS = (TOKENS * TOP_K + N_EXPERTS * (MOE_BLK - 1) + MOE_BLK - 1) // MOE_BLK
ROUTE_W = 8

VMEM_LIMIT = 56 * 1024 * 1024


def _cparams(sem):
    return pltpu.CompilerParams(dimension_semantics=sem, vmem_limit_bytes=VMEM_LIMIT)


def _sigmoid(x):
    return 1.0 / (1.0 + jnp.exp(-x))


def _row_tile_spec(rows, index_map):
    return pl.BlockSpec((rows * ROW_TILE_S, LANES), index_map)


def _as_row_tiles(a2d):
    return a2d.reshape(a2d.shape[0] // ROW_TILE_S, ROW_TILE_S, LANES)


def _as_2d(a3d):
    return a3d.reshape(a3d.shape[0] * ROW_TILE_S, LANES)


def _store_row_tiles(ref, val, row0=0):
    for s in range(ROW_TILE_S):
        ref[pl.ds(row0 * ROW_TILE_S + s, val.shape[0], stride=ROW_TILE_S), :] = (
            val[:, s * LANES:(s + 1) * LANES])


def _load_row_tiles(ref, row0=0, rows=None):
    rows = ref.shape[0] // ROW_TILE_S if rows is None else rows
    return jnp.concatenate(
        [ref[pl.ds(row0 * ROW_TILE_S + s, rows, stride=ROW_TILE_S), :] for s in range(ROW_TILE_S)],
        axis=1)


def _mod_kernel(ct_ref, w_ref, b_ref, o_ref):
    ct = ct_ref[...]
    ca = ct * _sigmoid(ct)
    w = w_ref[...]
    rows = []
    for b in range(BATCH):
        rows.append(jnp.sum(w * ca[:, b:b + 1], axis=0, keepdims=True) + b_ref[...])
    rows.append(jnp.zeros((8 - BATCH, w.shape[1]), F32))
    o_ref[...] = jnp.concatenate(rows, axis=0)


def _mod_call(c, ada_w, ada_b):
    tn = D_MODEL
    ct = jnp.zeros((D_MODEL, 8), F32).at[:, :BATCH].set(c.T)
    out = pl.pallas_call(
        _mod_kernel,
        grid=(N_MOD,),
        in_specs=[
            pl.BlockSpec((D_MODEL, 8), lambda j: (0, 0)),
            pl.BlockSpec((D_MODEL, tn), lambda j: (0, j)),
            pl.BlockSpec((1, tn), lambda j: (0, j)),
        ],
        out_specs=pl.BlockSpec((8, tn), lambda j: (0, j)),
        out_shape=jax.ShapeDtypeStruct((8, N_MOD * D_MODEL), F32),
        compiler_params=_cparams(("arbitrary",)),
        name="mod",
    )(ct, ada_w, ada_b.reshape(1, N_MOD * D_MODEL))
    return out[:BATCH].reshape(BATCH, N_MOD, D_MODEL)


def _wprep_kernel(w_ref, o_ref):
    j = pl.program_id(0)
    t = w_ref[...].T
    lane = lax.broadcasted_iota(I32, t.shape, 1)
    @pl.when(j < WPREP_LR_STEP)
    def _():
        o_ref[...] = t.astype(BF16)

    @pl.when(j == WPREP_LR_STEP)
    def _():
        rep = jnp.where(lane < GLA_GATE_RANK, t,
                        jnp.where(lane < 2 * GLA_GATE_RANK, pltpu.roll(t, GLA_GATE_RANK, 1),
                                  jnp.where(lane < 3 * GLA_GATE_RANK,
                                            pltpu.roll(t, 2 * GLA_GATE_RANK, 1), 0.0)))
        o_ref[...] = rep.astype(BF16)


def _wprep_call(w_in0):
    w_t = w_in0.T
    lr0 = 2 * GLA_QK_W + 2 * GLA_V_W
    assert lr0 % WPREP_COLS == 0 and (D_IN_PAD - LR_PAD) % WPREP_COLS == 0

    def src_row(j):
        return jnp.where(j < lr0 // WPREP_COLS, j * WPREP_COLS,
                         jnp.where(j < WPREP_LR_STEP, j * WPREP_COLS + GLA_GATE_RANK, lr0))

    return pl.pallas_call(
        _wprep_kernel,
        grid=(WPREP_LR_STEP + 1,),
        in_specs=[pl.BlockSpec((pl.Element(WPREP_COLS), pl.Element(D_MODEL)),
                               lambda j: (pl.multiple_of(src_row(j), SUBLANES), 0))],
        out_specs=pl.BlockSpec((D_MODEL, WPREP_COLS), lambda j: (0, j)),
        out_shape=jax.ShapeDtypeStruct((D_MODEL, D_IN_ALLOC), BF16),
        compiler_params=_cparams(("parallel",)),
        name="w_prep",
    )(w_t)


def _in_kernel(x_ref, mod_ref, g1_ref, w_ref, bd_ref, qg_ref, kg_ref, pos_ref, invf_ref,
               gqk_ref, gv_ref, gr_ref, lr_ref, qT_ref, k_ref, vT_ref, sa_ref, sb_ref,
               cos_scr, sin_scr):
    ang = pos_ref[...].astype(F32) * invf_ref[...]
    cos_scr[...] = jnp.cos(ang)
    sin_scr[...] = jnp.sin(ang)

    x = x_ref[...]
    shift1 = mod_ref[0, 0:1, :]
    scale1 = mod_ref[0, 1:2, :]
    ms = jnp.mean(x * x, axis=-1, keepdims=True)
    h = (x * lax.rsqrt(ms + NORM_EPS) * g1_ref[...]) * (1.0 + scale1) + shift1
    hb = h.astype(BF16)

    def proj(name):
        c0, c1 = IN_COLS[name]
        return jnp.dot(hb, w_ref[:, c0:c1], preferred_element_type=F32)

    tm = x.shape[0]
    cos4 = jnp.concatenate([cos_scr[...]] * DIFF_HEADS, axis=1)
    sin4 = jnp.concatenate([sin_scr[...]] * DIFF_HEADS, axis=1)
    lane = lax.broadcasted_iota(I32, (tm, DIFF_QK_W), 1)
    first_half = (lane % DIFF_DH) < ROPE_HALF
    bd = bd_ref[...]

    def norm_rope(t, gain_row):
        t2 = t * t
        hi = t2.astype(BF16)
        lo = (t2 - hi.astype(F32)).astype(BF16)
        hw = bd.shape[0]
        gms = jnp.concatenate(
            [jnp.dot(hi[:, c0:c0 + hw], bd, preferred_element_type=F32)
             + jnp.dot(lo[:, c0:c0 + hw], bd, preferred_element_type=F32)
             for c0 in range(0, DIFF_QK_W, hw)], axis=1)
        t = t * lax.rsqrt(gms + NORM_EPS) * gain_row
        nxt = pltpu.roll(t, DIFF_QK_W - ROPE_HALF, 1)
        prv = pltpu.roll(t, ROPE_HALF, 1)
        return t * cos4 + jnp.where(first_half, -nxt, prv) * sin4

    dq_raw = proj("diff_q")
    dk_raw = proj("diff_k")
    gqk_ref[...] = proj("gla_qk").astype(BF16)
    dq = norm_rope(dq_raw, qg_ref[...]) * Q_SCALE
    gv_ref[...] = proj("gla_v").astype(BF16)
    dk = norm_rope(dk_raw, kg_ref[...])
    dv = proj("diff_v")
    for hd in range(DIFF_HEADS):
        sl = slice(hd * LANES, (hd + 1) * LANES)
        qT_ref[0, hd, 0] = dq[:, sl].T.astype(BF16)
        k_ref[0, hd, 0] = dk[:, sl].astype(BF16)
    gr_ref[...] = proj("gla_r").astype(BF16)
    for hd in range(DIFF_HEADS):
        sl = slice(hd * LANES, (hd + 1) * LANES)
        vT_ref[0, hd, 0] = jnp.concatenate(
            [dv[:, sl].T, jnp.ones((V_ROWS - DIFF_DV, tm), F32)], axis=0).astype(BF16)
    sa_ref[...] = _sigmoid(proj("gate_a")).astype(BF16)
    sb_ref[...] = _sigmoid(proj("gate_b")).astype(BF16)
    lr_ref[...] = proj("gla_lr")


def _in_call(x2, mod, norm1_g, w_in_p, bd, qg_row, kg_row, pos_col, invf_row):
    nb = SEQ // TM_IN
    tok_spec = lambda w: pl.BlockSpec((TM_IN, w), lambda i: (i, 0))
    const2 = lambda r, c: pl.BlockSpec((r, c), lambda i: (0, 0))
    out_shapes = (
        jax.ShapeDtypeStruct((TOKENS, 2 * GLA_QK_W), BF16),
        jax.ShapeDtypeStruct((TOKENS, GLA_V_W), BF16),
        jax.ShapeDtypeStruct((TOKENS, GLA_V_W), BF16),
        jax.ShapeDtypeStruct((TOKENS, LR_PAD), F32),
        jax.ShapeDtypeStruct((BATCH, DIFF_HEADS, N_KV, LANES, TQ), BF16),
        jax.ShapeDtypeStruct((BATCH, DIFF_HEADS, N_KV, TQ, LANES), BF16),
        jax.ShapeDtypeStruct((BATCH, DIFF_HEADS, N_KV, V_ROWS, TQ), BF16),
        jax.ShapeDtypeStruct((TOKENS, D_MODEL), BF16),
        jax.ShapeDtypeStruct((TOKENS, D_MODEL), BF16),
    )
    out_specs = (
        tok_spec(2 * GLA_QK_W), tok_spec(GLA_V_W), tok_spec(GLA_V_W), tok_spec(LR_PAD),
        pl.BlockSpec((1, DIFF_HEADS, 1, LANES, TM_IN), lambda i: (i // nb, 0, i % nb, 0, 0)),
        pl.BlockSpec((1, DIFF_HEADS, 1, TM_IN, LANES), lambda i: (i // nb, 0, i % nb, 0, 0)),
        pl.BlockSpec((1, DIFF_HEADS, 1, V_ROWS, TM_IN), lambda i: (i // nb, 0, i % nb, 0, 0)),
        tok_spec(D_MODEL), tok_spec(D_MODEL),
    )
    return pl.pallas_call(
        _in_kernel,
        grid=(TOKENS // TM_IN,),
        in_specs=[
            tok_spec(D_MODEL),
            pl.BlockSpec((1, N_MOD, D_MODEL), lambda i: (i // nb, 0, 0)),
            const2(1, D_MODEL),
            pl.BlockSpec((D_MODEL, D_IN_ALLOC), lambda i: (0, 0), pipeline_mode=pl.Buffered(1)),
            const2(NORM_BLK, NORM_BLK),
            const2(1, DIFF_QK_W), const2(1, DIFF_QK_W),
            tok_spec(1),
            const2(1, LANES),
        ],
        out_specs=out_specs,
        out_shape=out_shapes,
        scratch_shapes=[pltpu.VMEM((TM_IN, LANES), F32), pltpu.VMEM((TM_IN, LANES), F32)],
        compiler_params=_cparams(("parallel",)),
        name="in_proj",
    )(x2, mod, norm1_g, w_in_p, bd, qg_row, kg_row, pos_col, invf_row)


def _gla_kernel(qk_ref, v_ref, r_ref, lr_ref, au_ref, ab_ref, og_ref, o_ref,
                state_ref, oacc_ref, snew_ref):
    tt = qk_ref.shape[0]
    n_chunks = tt // GLA_CHUNK

    @pl.when(pl.program_id(1) == 0)
    def _():
        state_ref[...] = jnp.zeros_like(state_ref)

    lr = lr_ref[...]
    lr_hi = lr.astype(BF16)
    lr_lo = (lr - lr_hi.astype(F32)).astype(BF16)
    lr_lane = lax.broadcasted_iota(I32, lr.shape, 1)
    in_lo_group = (lr_lane >= GLA_GATE_RANK) & (lr_lane < 2 * GLA_GATE_RANK)
    stacked = jnp.where(in_lo_group, lr_lo, lr_hi)
    z = jnp.dot(stacked, au_ref[...], preferred_element_type=F32) + ab_ref[...]
    g = (jnp.minimum(z, 0.0) - jnp.log(1.0 + jnp.exp(-jnp.abs(z)))) * (1.0 / GLA_GATE_TAU)

    row = lax.broadcasted_iota(I32, (tt, GLA_QK_W), 0) % GLA_CHUNK
    b = g
    step = 1
    while step < GLA_CHUNK:
        b = b + jnp.where(row >= step, pltpu.roll(b, step, 0), 0.0)
        step *= 2

    b_last_rows = [b[c * GLA_CHUNK + GLA_CHUNK - 1:(c + 1) * GLA_CHUNK, :] for c in range(n_chunks)]
    b_last = jnp.concatenate(
        [jnp.broadcast_to(bl, (GLA_CHUNK, GLA_QK_W)) for bl in b_last_rows], axis=0)

    qk = qk_ref[...].astype(F32)
    q = qk[:, :GLA_QK_W] * (GLA_DK ** -0.5)
    k = qk[:, GLA_QK_W:]
    q_in = (q * jnp.exp(b)).astype(BF16)
    k_in = (k * jnp.exp(-b)).astype(BF16)
    k_dec = (k * jnp.exp(b_last - b)).astype(BF16)

    ci = lax.broadcasted_iota(I32, (GLA_CHUNK, GLA_CHUNK), 0)
    cj = lax.broadcasted_iota(I32, (GLA_CHUNK, GLA_CHUNK), 1)
    causal = ci >= cj

    pairs = [(c, hd) for c in range(n_chunks) for hd in range(GLA_HEADS)]

    def rows(c):
        return slice(c * GLA_CHUNK, (c + 1) * GLA_CHUNK)

    def kcols(hd):
        return slice(hd * GLA_DK, (hd + 1) * GLA_DK)

    def vcols(hd):
        return slice(hd * GLA_DV, (hd + 1) * GLA_DV)

    att = {}
    for c, hd in pairs:
        a = lax.dot_general(q_in[rows(c), kcols(hd)], k_in[rows(c), kcols(hd)],
                            (((1,), (1,)), ((), ())), preferred_element_type=F32)
        att[c, hd] = jnp.where(causal, a, 0.0).astype(BF16)
    for c, hd in pairs:
        oacc_ref[rows(c), vcols(hd)] = jnp.dot(att[c, hd], v_ref[rows(c), vcols(hd)],
                                               preferred_element_type=F32)
    for c, hd in pairs:
        snew_ref[c * GLA_HEADS + hd] = lax.dot_general(
            k_dec[rows(c), kcols(hd)], v_ref[rows(c), vcols(hd)],
            (((0,), (0,)), ((), ())), preferred_element_type=F32)

    decay_rows = jnp.exp(jnp.concatenate(
        b_last_rows + [jnp.zeros((LANES - n_chunks, GLA_QK_W), F32)], axis=0))
    decay_cols = decay_rows.T
    states = [state_ref[hd] for hd in range(GLA_HEADS)]
    for c in range(n_chunks):
        for hd in range(GLA_HEADS):
            s_prev = states[hd]
            oacc_ref[rows(c), vcols(hd)] += jnp.dot(
                q_in[rows(c), kcols(hd)], s_prev.astype(BF16), preferred_element_type=F32)
            dcol = decay_cols[kcols(hd), c:c + 1]
            states[hd] = s_prev * dcol + snew_ref[c * GLA_HEADS + hd]
    for hd in range(GLA_HEADS):
        state_ref[hd] = states[hd]

    for hd in range(GLA_HEADS):
        vs = slice(hd * GLA_DV, (hd + 1) * GLA_DV)
        oh = oacc_ref[:, vs]
        ms = jnp.mean(oh * oh, axis=-1, keepdims=True)
        y = oh * lax.rsqrt(ms + NORM_EPS) * og_ref[...]
        r = r_ref[:, vs].astype(F32)
        o_ref[:, vs] = (y * (r * _sigmoid(r))).astype(BF16)


def _gla_call(gqk, gv, gr, lr, au_stack, ab_row, og_row):
    nt = SEQ // TT_GLA
    tok = lambda w: pl.BlockSpec((TT_GLA, w), lambda b, t: (b * nt + t, 0))
    const2 = lambda r, c: pl.BlockSpec((r, c), lambda b, t: (0, 0))
    return pl.pallas_call(
        _gla_kernel,
        grid=(BATCH, nt),
        in_specs=[tok(2 * GLA_QK_W), tok(GLA_V_W), tok(GLA_V_W), tok(LR_PAD),
                  const2(LR_PAD, GLA_QK_W), const2(1, GLA_QK_W), const2(1, GLA_DV)],
        out_specs=tok(GLA_V_W),
        out_shape=jax.ShapeDtypeStruct((TOKENS, GLA_V_W), BF16),
        scratch_shapes=[pltpu.VMEM((GLA_HEADS, GLA_DK, GLA_DV), F32),
                        pltpu.VMEM((TT_GLA, GLA_V_W), F32),
                        pltpu.VMEM((TT_GLA // GLA_CHUNK * GLA_HEADS, GLA_DK, GLA_DV), F32)],
        compiler_params=_cparams(("parallel", "arbitrary")),
        name="gla",
    )(gqk, gv, gr, lr, au_stack, ab_row, og_row)


def _attn_kernel(qT_ref, k_ref, vT_ref, lam_ref, og_ref, o_ref,
                 q_scr, s_scr, p_scr, acc_scr, m_scr):
    _attn_load_q(qT_ref, q_scr, 0, 0)
    s_scr[0] = jnp.dot(k_ref[0, 0, 0], q_scr[0, 0], preferred_element_type=F32)

    def q_block(i, carry):
        _attn_q_block(i, qT_ref, k_ref, vT_ref, lam_ref, og_ref, o_ref,
                      q_scr, s_scr, p_scr, acc_scr, m_scr)
        return carry

    lax.fori_loop(0, N_KV, q_block, 0)


def _attn_load_q(qT_ref, q_scr, i, slot):
    qT = qT_ref[0, 0, i]
    rowq = lax.broadcasted_iota(I32, qT.shape, 0)
    zero = jnp.zeros_like(qT)
    q_scr[slot, 0] = jnp.where(rowq < DIFF_DH, qT, zero)
    q_scr[slot, 1] = jnp.where(rowq >= DIFF_DH, qT, zero)


def _attn_q_block(i, qT_ref, k_ref, vT_ref, lam_ref, og_ref, o_ref,
                  q_scr, s_scr, p_scr, acc_scr, m_scr):
    slot = i % 2
    m_scr[...] = jnp.full(m_scr.shape, NEG_INF, F32)
    acc_scr[...] = jnp.zeros(acc_scr.shape, F32)
    n_sub = TQ // ATT_SUB

    def fold8(t, op):
        return op(t.reshape(t.shape[0] // SUBLANES, SUBLANES, TQ), axis=0)

    def scores(c, j):
        s_scr[c] = jnp.dot(k_ref[0, 0, j], q_scr[slot, c], preferred_element_type=F32)

    def load_s(c, r, masked):
        s = s_scr[c, r * ATT_SUB:(r + 1) * ATT_SUB, :]
        if masked:
            key_i = lax.broadcasted_iota(I32, (ATT_SUB, TQ), 0) + r * ATT_SUB
            qry_i = lax.broadcasted_iota(I32, (ATT_SUB, TQ), 1)
            s = jnp.where(key_i <= qry_i, s, NEG_INF)
        return s

    def softmax_pv(c, j, masked):
        m8 = fold8(load_s(c, 0, masked), jnp.max)
        for r in range(1, n_sub):
            m8 = jnp.maximum(m8, fold8(load_s(c, r, masked), jnp.max))
        m_old = m_scr[c]
        m_new = jnp.maximum(m_old, jnp.max(m8, axis=0, keepdims=True))
        alpha = jnp.exp2(m_old - m_new)
        for r in range(n_sub):
            p = jnp.exp2(load_s(c, r, masked) - m_new)
            p_scr[c, r * ATT_SUB:(r + 1) * ATT_SUB, :] = p.astype(BF16)
        m_scr[c] = m_new
        acc_scr[c] = acc_scr[c] * alpha + jnp.dot(vT_ref[0, 0, j], p_scr[c],
                                                  preferred_element_type=F32)

    def body(j, carry):
        scores(1, j)
        softmax_pv(0, j, False)
        scores(0, j + 1)
        softmax_pv(1, j, False)
        return carry

    lax.fori_loop(0, i, body, 0)
    scores(1, i)
    softmax_pv(0, i, True)
    softmax_pv(1, i, True)

    nxt = jnp.minimum(i + 1, N_KV - 1)
    _attn_load_q(qT_ref, q_scr, nxt, 1 - slot)
    s_scr[0] = jnp.dot(k_ref[0, 0, 0], q_scr[1 - slot, 0], preferred_element_type=F32)

    l1 = acc_scr[0, DIFF_DV:DIFF_DV + 1, :]
    l2 = acc_scr[1, DIFF_DV:DIFF_DV + 1, :]

    lam_p = lam_ref[...]
    lam = (jnp.exp(jnp.sum(lam_p[0:1] * lam_p[1:2], axis=1, keepdims=True))
           - jnp.exp(jnp.sum(lam_p[2:3] * lam_p[3:4], axis=1, keepdims=True)) + LAMBDA_INIT)
    oT = acc_scr[0, :DIFF_DV, :] / l1 - lam * (acc_scr[1, :DIFF_DV, :] / l2)
    ms = jnp.mean(oT * oT, axis=0, keepdims=True)
    y = oT * lax.rsqrt(ms + NORM_EPS) * og_ref[...] * (1.0 - LAMBDA_INIT)
    o_ref[pl.ds(pl.multiple_of(i * TQ, TQ), TQ), :] = y.T.astype(BF16)


def _attn_call(qT, kk, vT, lam_p, og_col):
    return pl.pallas_call(
        _attn_kernel,
        grid=(BATCH, DIFF_HEADS),
        in_specs=[
            pl.BlockSpec((1, 1, N_KV, LANES, TQ), lambda b, h: (b, h, 0, 0, 0)),
            pl.BlockSpec((1, 1, N_KV, TQ, LANES), lambda b, h: (b, h, 0, 0, 0)),
            pl.BlockSpec((1, 1, N_KV, V_ROWS, TQ), lambda b, h: (b, h, 0, 0, 0)),
            pl.BlockSpec((4, DIFF_DH), lambda b, h: (0, 0)),
            pl.BlockSpec((DIFF_DV, 1), lambda b, h: (0, 0)),
        ],
        out_specs=pl.BlockSpec((SEQ, DIFF_DV), lambda b, h: (b, h)),
        out_shape=jax.ShapeDtypeStruct((TOKENS, DIFF_V_W), BF16),
        scratch_shapes=[pltpu.VMEM((2, 2, LANES, TQ), BF16),
                        pltpu.VMEM((2, TQ, TQ), F32),
                        pltpu.VMEM((2, TQ, TQ), BF16),
                        pltpu.VMEM((2, V_ROWS, TQ), F32),
                        pltpu.VMEM((2, 1, TQ), F32)],
        compiler_params=_cparams(("parallel", "parallel")),
        name="attn",
    )(qT, kk, vT, lam_p, og_col)


def _merge_kernel(x_ref, oa_ref, ob_ref, sa_ref, sb_ref, mod_ref, g2_ref, wa_ref, wb_ref, wo_ref,
                  wr2_ref, br_ref, x1_ref, h2_ref, rf_ref, rank_ref, cnt_ref,
                  carry_ref, tri_ref, lg_ref):
    tm = x_ref.shape[0]
    step = pl.program_id(0)

    @pl.when(step == 0)
    def _():
        carry_ref[...] = jnp.zeros_like(carry_ref)
        lg_ref[...] = jnp.zeros_like(lg_ref)
        ti = lax.broadcasted_iota(I32, (tm, tm), 0)
        tj = lax.broadcasted_iota(I32, (tm, tm), 1)
        tri_ref[...] = jnp.where(ti > tj, 1.0, 0.0).astype(BF16)

    ma = jnp.dot(oa_ref[...], wa_ref[...], preferred_element_type=F32)
    mb = jnp.dot(ob_ref[...], wb_ref[...], preferred_element_type=F32)
    merged = sa_ref[...].astype(F32) * ma + sb_ref[...].astype(F32) * mb
    y = jnp.dot(merged.astype(BF16), wo_ref[...], preferred_element_type=F32)
    logits = lg_ref[...]
    lane = lax.broadcasted_iota(I32, (tm, LANES), 1).astype(F32)
    ninf = -jnp.inf
    big = float(LANES)

    def first_argmax(v):
        vmax = jnp.max(v, axis=1, keepdims=True)
        idx = jnp.min(jnp.where(v == vmax, lane, big), axis=1, keepdims=True)
        return vmax, idx

    gl = jnp.where(lane < N_GROUPS, logits, ninf)
    gmax, gidx = first_argmax(gl)
    p_top = 1.0 / jnp.sum(jnp.exp(gl - gmax), axis=1, keepdims=True)
    lo = N_GROUPS + EXPERTS_PER_GROUP * gidx
    el = jnp.where((lane >= lo) & (lane < lo + EXPERTS_PER_GROUP), logits, ninf)
    e1max, e1 = first_argmax(el)
    e2max, e2 = first_argmax(jnp.where(lane == e1, ninf, el))
    t = jnp.exp(e2max - e1max)
    w1 = 1.0 / (1.0 + t)
    w2 = t / (1.0 + t)
    col = lax.broadcasted_iota(I32, (tm, ROUTE_W), 1)
    rf_ref[...] = jnp.where(col == 0, p_top * w1, jnp.where(col == 1, p_top * w2, 0.0))

    x1id = e1 - N_GROUPS
    x2id = e2 - N_GROUPS
    hit1 = lane == x1id
    hit2 = lane == x2id
    onehot = jnp.where(hit1 | hit2, 1.0, 0.0)
    before = jnp.dot(tri_ref[...], onehot.astype(BF16), preferred_element_type=F32) + carry_ref[...]
    r1 = jnp.sum(jnp.where(hit1, before, 0.0), axis=1, keepdims=True)
    r2 = jnp.sum(jnp.where(hit2, before, 0.0), axis=1, keepdims=True)
    real_tile = jnp.where(step > 0, 1.0, 0.0)
    carry_ref[...] = carry_ref[...] + real_tile * jnp.sum(onehot, axis=0, keepdims=True)
    cnt_ref[...] = carry_ref[...]
    cols = jnp.where(lane == 0.0, x1id, jnp.where(lane == 1.0, x2id,
                                                  jnp.where(lane == 2.0, r1,
                                                            jnp.where(lane == 3.0, r2, 0.0))))
    rank_ref[0] = cols.T[0:ROUTE_W, :].astype(I32)

    gate1 = mod_ref[0, 2:3, :]
    shift2 = mod_ref[0, 3:4, :]
    scale2 = mod_ref[0, 4:5, :]
    x1 = x_ref[...] + gate1 * y
    x1_ref[...] = x1
    ms = jnp.mean(x1 * x1, axis=-1, keepdims=True)
    h2 = (x1 * lax.rsqrt(ms + NORM_EPS) * g2_ref[...]) * (1.0 + scale2) + shift2
    _store_row_tiles(h2_ref, h2)
    h2_hi = h2.astype(BF16)
    h2_lo = (h2 - h2_hi.astype(F32)).astype(BF16)
    hi_both = jnp.dot(h2_hi, wr2_ref[...], preferred_element_type=F32)
    next_logits = (hi_both[:, :LANES] + hi_both[:, LANES:]
                   + jnp.dot(h2_lo, wr2_ref[:, :LANES], preferred_element_type=F32)) + br_ref[...]

    lg_ref[...] = next_logits


def _merge_call(x2, o_a, o_b, sa, sb, mod, norm2_g, wa, wb, wo, wr2, br):
    nb = SEQ // TM_MERGE
    n_tiles = TOKENS // TM_MERGE
    cur = lambda i: jnp.minimum(i, n_tiles - 1)
    prev = lambda i: jnp.maximum(i - 1, 0)
    tok = lambda w: pl.BlockSpec((TM_MERGE, w), lambda i: (cur(i), 0))
    const2 = lambda r, c: pl.BlockSpec((r, c), lambda i: (0, 0))
    return pl.pallas_call(
        _merge_kernel,
        grid=(n_tiles + 1,),
        in_specs=[tok(D_MODEL), tok(GLA_V_W), tok(DIFF_V_W), tok(D_MODEL), tok(D_MODEL),
                  pl.BlockSpec((1, N_MOD, D_MODEL), lambda i: (cur(i) // nb, 0, 0)),
                  const2(1, D_MODEL),
                  const2(GLA_V_W, D_MODEL), const2(DIFF_V_W, D_MODEL), const2(D_MODEL, D_MODEL),
                  const2(D_MODEL, 2 * LANES), const2(1, LANES)],
        out_specs=(tok(D_MODEL), _row_tile_spec(TM_MERGE, lambda i: (cur(i), 0)),
                   pl.BlockSpec((TM_MERGE, ROUTE_W), lambda i: (prev(i), 0)),
                   pl.BlockSpec((1, ROUTE_W, TM_MERGE), lambda i: (prev(i), 0, 0)),
                   pl.BlockSpec((1, LANES), lambda i: (0, 0))),
        out_shape=(jax.ShapeDtypeStruct((TOKENS, D_MODEL), F32),
                   jax.ShapeDtypeStruct((TOKENS * ROW_TILE_S, LANES), F32),
                   jax.ShapeDtypeStruct((TOKENS, ROUTE_W), F32),
                   jax.ShapeDtypeStruct((n_tiles, ROUTE_W, TM_MERGE), I32),
                   jax.ShapeDtypeStruct((1, LANES), F32)),
        scratch_shapes=[pltpu.VMEM((1, LANES), F32), pltpu.VMEM((TM_MERGE, TM_MERGE), BF16),
                        pltpu.VMEM((TM_MERGE, LANES), F32)],
        compiler_params=_cparams(("arbitrary",)),
        name="merge",
    )(x2, o_a, o_b, sa, sb, mod, norm2_g, wa, wb, wo, wr2, br)


def _plan_kernel(rk_ref, cnt_ref, dest_ref, blk_ref):
    n_tiles, _, tm = rk_ref.shape
    cnt = cnt_ref[...]
    padded = jnp.floor((cnt + (MOE_BLK - 1)) * (1.0 / MOE_BLK)) * MOE_BLK
    lane = lax.broadcasted_iota(I32, (1, LANES), 1)
    seg_end = padded
    step = 1
    while step < N_EXPERTS:
        seg_end = seg_end + jnp.where(lane >= step, pltpu.roll(seg_end, step, 1), 0.0)
        step *= 2
    seg_start = seg_end - padded
    valid_end = seg_start + cnt

    ei = lax.broadcasted_iota(I32, (LANES, LANES), 0)
    ej = lax.broadcasted_iota(I32, (LANES, LANES), 1)

    def to_col(rowv):
        return jnp.sum(jnp.where(ei == ej, rowv, 0.0), axis=1, keepdims=True)

    start_col, end_col, valid_col = to_col(seg_start), to_col(seg_end), to_col(valid_end)

    nb_pad = blk_ref.shape[1]
    e_sub = lax.broadcasted_iota(I32, (LANES, nb_pad), 0)
    b_start = (lax.broadcasted_iota(I32, (1, nb_pad), 1) * MOE_BLK).astype(F32)
    ends_before = jnp.where((e_sub < N_EXPERTS) & (end_col <= b_start), 1.0, 0.0)
    block_e = jnp.minimum(jnp.sum(ends_before, axis=0, keepdims=True), N_EXPERTS - 1.0)
    block_valid_end = jnp.sum(jnp.where(e_sub.astype(F32) == block_e, valid_col, 0.0),
                              axis=0, keepdims=True)
    n_valid = jnp.clip(block_valid_end - b_start, 0.0, float(MOE_BLK))
    n_used = jnp.max(seg_end, axis=1, keepdims=True) * (1.0 / MOE_BLK)
    nonempty = (to_col(cnt) > 0.0) & (e_sub < N_EXPERTS)
    e_subf = e_sub.astype(F32)
    no_next = float(LANES)
    nxt = jnp.min(jnp.where(nonempty & (e_subf > block_e), e_subf, no_next), axis=0, keepdims=True)
    nxt = jnp.where(nxt == no_next, -1.0, nxt)
    seg_idx = jnp.sum(jnp.where(nonempty & (e_subf < block_e), 1.0, 0.0), axis=0, keepdims=True)
    slot = seg_idx - 2.0 * jnp.floor(seg_idx * 0.5)
    blk_ref[...] = jnp.concatenate(
        [block_e, n_valid, jnp.broadcast_to(n_used, (1, nb_pad)), nxt, slot,
         jnp.zeros((ROUTE_W - 5, nb_pad), F32)], axis=0).astype(I32)

    e_tok = lax.broadcasted_iota(I32, (LANES, tm), 0)

    def tile(t, carry):
        rk = rk_ref[t]
        d = [jnp.sum(jnp.where(e_tok == rk[k:k + 1], start_col, 0.0), axis=0, keepdims=True)
             + rk[k + 2:k + 3].astype(F32) for k in range(TOP_K)]
        dest_ref[t] = jnp.concatenate(d + [jnp.zeros((ROUTE_W - TOP_K, tm), F32)], axis=0).astype(I32)
        return carry

    lax.fori_loop(0, n_tiles, tile, 0)


def _plan_call(rank_rows, cnt):
    n_tiles = TOKENS // TM_MERGE
    nb_pad = -(-N_BLOCKS // LANES) * LANES
    dest, blk = pl.pallas_call(
        _plan_kernel,
        out_shape=(jax.ShapeDtypeStruct((n_tiles, ROUTE_W, TM_MERGE), I32),
                   jax.ShapeDtypeStruct((ROUTE_W, nb_pad), I32)),
        compiler_params=pltpu.CompilerParams(vmem_limit_bytes=VMEM_LIMIT),
        name="plan",
    )(rank_rows, cnt)
    dest1 = dest[:, 0, :].reshape(TOKENS)
    dest2 = dest[:, 1, :].reshape(TOKENS)
    tables = tuple(blk[r, :N_BLOCKS] for r in (0, 1, 3, 4))
    return dest1, dest2, tables, blk[2, :1]


def _sc_params():
    return pltpu.CompilerParams(use_tc_tiling_on_sc=True)


def _sc_mesh():
    return plsc.VectorSubcoreMesh(core_axis_name="core", subcore_axis_name="subcore")


def _sc_worker_base(per_worker):
    wid = lax.axis_index("subcore") * SC_CORES + lax.axis_index("core")
    return wid * per_worker


def _sc_scatter_rows(src, dest1, dest2, n_out):
    n = src.shape[0]
    per_worker = n // SC_WORKERS
    assert per_worker * SC_WORKERS == n and per_worker % SC_CHUNK == 0

    def body(src_hbm, d1_hbm, d2_hbm, out_hbm, idx_v, rows_v):
        base = _sc_worker_base(per_worker)

        @pl.loop(0, per_worker // SC_CHUNK)
        def _(j):
            start = pl.multiple_of(base + j * SC_CHUNK, SC_CHUNK)
            pltpu.sync_copy(src_hbm.at[pl.ds(start, SC_CHUNK)], rows_v)
            for d_hbm in (d1_hbm, d2_hbm):
                pltpu.sync_copy(d_hbm.at[pl.ds(start, SC_CHUNK)], idx_v)
                pltpu.sync_copy(rows_v, out_hbm.at[idx_v])

    return pl.kernel(
        body,
        out_type=jax.ShapeDtypeStruct((n_out, ROW_TILE_S, LANES), F32),
        mesh=_sc_mesh(),
        scratch_types=[pltpu.VMEM((SC_CHUNK,), I32),
                       pltpu.VMEM((SC_CHUNK, ROW_TILE_S, LANES), F32)],
        compiler_params=_sc_params(),
        name="sc_dispatch",
    )(src, dest1, dest2)


def _sc_gather_rows(table, idx):
    n = idx.shape[0]
    per_worker = n // SC_WORKERS
    half = SC_CHUNK // 2
    assert per_worker * SC_WORKERS == n and per_worker % SC_CHUNK == 0

    def body(table_hbm, idx_hbm, out_hbm, idx_v, rows_a, rows_b, gsem_a, gsem_b, wsem_a, wsem_b):
        base = pl.multiple_of(_sc_worker_base(per_worker), SC_CHUNK)
        pltpu.sync_copy(idx_hbm.at[pl.ds(base, per_worker)], idx_v)

        @pl.loop(0, per_worker // SC_CHUNK)
        def _(j):
            off_a = pl.multiple_of(j * SC_CHUNK, SC_CHUNK)
            off_b = pl.multiple_of(j * SC_CHUNK + half, half)
            ga = pltpu.async_copy(table_hbm.at[idx_v.at[pl.ds(off_a, half)]], rows_a, gsem_a)
            gb = pltpu.async_copy(table_hbm.at[idx_v.at[pl.ds(off_b, half)]], rows_b, gsem_b)
            ga.wait()
            wa = pltpu.async_copy(rows_a, out_hbm.at[pl.ds(base + off_a, half)], wsem_a)
            gb.wait()
            wb = pltpu.async_copy(rows_b, out_hbm.at[pl.ds(base + off_b, half)], wsem_b)
            wa.wait()
            wb.wait()

    return pl.kernel(
        body,
        out_type=jax.ShapeDtypeStruct((n, ROW_TILE_S, LANES), F32),
        mesh=_sc_mesh(),
        scratch_types=[pltpu.VMEM((per_worker,), I32),
                       pltpu.VMEM((half, ROW_TILE_S, LANES), F32),
                       pltpu.VMEM((half, ROW_TILE_S, LANES), F32),
                       pltpu.SemaphoreType.DMA, pltpu.SemaphoreType.DMA,
                       pltpu.SemaphoreType.DMA, pltpu.SemaphoreType.DMA],
        compiler_params=_sc_params(),
        name="sc_gather",
    )(table, idx)


def _weight_fetch(w_hbm, stage, sem, e, slot):
    return [pltpu.make_async_copy(w.at[e], st.at[slot], sem.at[slot]) for w, st in zip(w_hbm, stage)]


def _expert_kernel(be_ref, nv_ref, nx_ref, sl_ref, nu_ref, xs_ref, w1_hbm, w3_hbm, w2_hbm, y_ref,
                   w1s, w3s, w2s, w1b, w3b, w2b, sem):
    i = pl.program_id(0)
    e = be_ref[i]
    used = i < nu_ref[0]
    first = (i == 0) | (e != be_ref[jnp.maximum(i - 1, 0)])
    w_hbm = (w1_hbm, w3_hbm, w2_hbm)
    stage = (w1s, w3s, w2s)

    @pl.when(used & first)
    def _():
        slot = sl_ref[i]

        @pl.when(i == 0)
        def _():
            for cp in _weight_fetch(w_hbm, stage, sem, e, slot):
                cp.start()

        for cp in _weight_fetch(w_hbm, stage, sem, e, slot):
            cp.wait()
        w1b[...] = w1s[slot].astype(BF16)
        w3b[...] = w3s[slot].astype(BF16)
        w2b[...] = w2s[slot].astype(BF16)

        @pl.when(nx_ref[i] >= 0)
        def _():
            for cp in _weight_fetch(w_hbm, stage, sem, nx_ref[i], 1 - slot):
                cp.start()

    n_parts = MOE_BLK // MOE_SUB

    def load_x(part):
        row0 = part * MOE_SUB
        row = lax.broadcasted_iota(I32, (MOE_SUB, D_MODEL), 0) + row0
        return jnp.where(row < nv_ref[i], _load_row_tiles(xs_ref, row0, MOE_SUB), 0.0).astype(BF16)

    def zero_y(part):
        y_ref[pl.ds(part * MOE_SUB * ROW_TILE_S, MOE_SUB * ROW_TILE_S), :] = jnp.zeros(
            (MOE_SUB * ROW_TILE_S, LANES), F32)

    def run(parts):
        hmid = []
        for part in parts:
            xb = load_x(part)
            a = jnp.dot(xb, w1b[...], preferred_element_type=F32)
            g = jnp.dot(xb, w3b[...], preferred_element_type=F32)
            hmid.append(((a * _sigmoid(a)) * g).astype(BF16))
        for part, hm in zip(parts, hmid):
            _store_row_tiles(y_ref, jnp.dot(hm, w2b[...], preferred_element_type=F32),
                             part * MOE_SUB)

    n_live = jnp.where(used, (nv_ref[i] + (MOE_SUB - 1)) // MOE_SUB, 0)
    for k in range(n_parts + 1):
        @pl.when(n_live == k)
        def _(k=k):
            if k:
                run(list(range(k)))
            for part in range(k, n_parts):
                zero_y(part)


def _expert_call(tables, n_used, xs, w1, w3, w2):
    n_tab = len(tables)
    grid_spec = pltpu.PrefetchScalarGridSpec(
        num_scalar_prefetch=n_tab + 1,
        grid=(N_BLOCKS,),
        in_specs=[_row_tile_spec(MOE_BLK, lambda i, *pf: (jnp.minimum(i, pf[n_tab][0] - 1), 0)),
                  pl.BlockSpec(memory_space=pl.ANY),
                  pl.BlockSpec(memory_space=pl.ANY),
                  pl.BlockSpec(memory_space=pl.ANY)],
        out_specs=_row_tile_spec(MOE_BLK, lambda i, *pf: (i, 0)),
        scratch_shapes=[pltpu.VMEM((2, D_MODEL, D_EXPERT), F32),
                        pltpu.VMEM((2, D_MODEL, D_EXPERT), F32),
                        pltpu.VMEM((2, D_EXPERT, D_MODEL), F32),
                        pltpu.VMEM((D_MODEL, D_EXPERT), BF16),
                        pltpu.VMEM((D_MODEL, D_EXPERT), BF16),
                        pltpu.VMEM((D_EXPERT, D_MODEL), BF16),
                        pltpu.SemaphoreType.DMA((2,))],
    )
    return pl.pallas_call(
        _expert_kernel,
        grid_spec=grid_spec,
        out_shape=jax.ShapeDtypeStruct((N_BLOCKS * MOE_BLK * ROW_TILE_S, LANES), F32),
        compiler_params=_cparams(("arbitrary",)),
        name="experts",
    )(*tables, n_used, xs, w1, w3, w2)


def _combine_rows_kernel(x1_ref, rf_ref, mod_ref, ya_ref, yb_ref, *rest):
    o_ref = rest[-1]
    rf = rf_ref[...]
    gate2 = mod_ref[0, 5:6, :]
    moe = rf[:, 0:1] * _load_row_tiles(ya_ref) + rf[:, 1:2] * _load_row_tiles(yb_ref)
    o_ref[...] = x1_ref[...] + gate2 * moe


def _combine_rows_call(x1, rf, mod, yg, part, prev=None):
    nb = SEQ // TM_ROW
    n_half = TOKENS // TM_ROW // 2
    t0 = part * n_half
    in_specs = [pl.BlockSpec((TM_ROW, D_MODEL), lambda i: (t0 + i, 0)),
                pl.BlockSpec((TM_ROW, ROUTE_W), lambda i: (t0 + i, 0)),
                pl.BlockSpec((1, N_MOD, D_MODEL), lambda i: ((t0 + i) // nb, 0, 0)),
                _row_tile_spec(TM_ROW, lambda i: (i, 0)),
                _row_tile_spec(TM_ROW, lambda i: (i + n_half, 0))]
    args = [x1, rf, mod, yg, yg]
    aliases = {}
    if prev is not None:
        in_specs.append(pl.BlockSpec(memory_space=pl.ANY))
        args.append(prev)
        aliases = {len(args) - 1: 0}
    return pl.pallas_call(
        _combine_rows_kernel,
        grid=(n_half,),
        in_specs=in_specs,
        out_specs=pl.BlockSpec((TM_ROW, D_MODEL), lambda i: (t0 + i, 0)),
        out_shape=jax.ShapeDtypeStruct((TOKENS, D_MODEL), F32),
        input_output_aliases=aliases,
        compiler_params=_cparams(("parallel",)),
        name="combine",
    )(*args)


def kernel(x, c, positions, ada_w, ada_b, norm1_g, norm2_g, w_in, gla_alpha_up, gla_alpha_b,
           gla_out_g, diff_q_g, diff_k_g, diff_lq1, diff_lk1, diff_lq2, diff_lk2, diff_out_g,
           w_branch_a, w_branch_b, w_out, router_group_w, router_group_b, router_expert_w,
           router_expert_b, expert_w1, expert_w3, expert_w2):
    assert x.shape == (BATCH, SEQ, D_MODEL) and ada_w.shape[0] == 1
    x2 = x.reshape(TOKENS, D_MODEL)

    w_in_p = _wprep_call(w_in[0])
    au_hi = gla_alpha_up[0].astype(BF16)
    au_lo = (gla_alpha_up[0] - au_hi.astype(F32)).astype(BF16)
    au_stack = jnp.concatenate(
        [au_hi, au_hi, au_lo, jnp.zeros((LR_PAD - 3 * GLA_GATE_RANK, GLA_QK_W), BF16)], axis=0)
    gid = jnp.arange(NORM_BLK) // DIFF_DH
    bd = jnp.where(gid[:, None] == gid[None, :], 1.0 / DIFF_DH, 0.0).astype(BF16)
    qg_row = jnp.tile(diff_q_g[0], DIFF_QK_W // DIFF_DH).reshape(1, DIFF_QK_W)
    kg_row = jnp.tile(diff_k_g[0], DIFF_QK_W // DIFF_DH).reshape(1, DIFF_QK_W)
    inv_freq = ROPE_THETA ** (-jnp.arange(ROPE_HALF, dtype=F32) / ROPE_HALF)
    invf64 = jnp.concatenate([inv_freq, inv_freq, jnp.zeros((DIFF_DH - ROPE_DIM,), F32)])
    invf_row = jnp.tile(invf64, LANES // DIFF_DH).reshape(1, LANES)
    pos_col = positions.reshape(TOKENS, 1)
    lam_p = jnp.concatenate([diff_lq1, diff_lk1, diff_lq2, diff_lk2], axis=0)
    wr = (jnp.zeros((D_MODEL, LANES), F32)
          .at[:, :N_GROUPS].set(router_group_w[0])
          .at[:, N_GROUPS:N_GROUPS + N_EXPERTS].set(router_expert_w[0]))
    wr_hi = wr.astype(BF16)
    wr2 = jnp.concatenate([wr_hi, (wr - wr_hi.astype(F32)).astype(BF16)], axis=1)
    br = (jnp.zeros((1, LANES), F32)
          .at[0, :N_GROUPS].set(router_group_b[0])
          .at[0, N_GROUPS:N_GROUPS + N_EXPERTS].set(router_expert_b[0]))

    mod = _mod_call(c, ada_w[0], ada_b[0])
    gqk, gv, gr, lr, qT, kk, vT, sa, sb = _in_call(
        x2, mod, norm1_g, w_in_p, bd, qg_row, kg_row, pos_col, invf_row)
    o_a = _gla_call(gqk, gv, gr, lr, au_stack, gla_alpha_b, gla_out_g)
    o_b = _attn_call(qT, kk, vT, lam_p, diff_out_g.reshape(DIFF_DV, 1))
    x1, h2, rf, rank, cnt = _merge_call(
        x2, o_a, o_b, sa, sb, mod, norm2_g, w_branch_a[0].astype(BF16),
        w_branch_b[0].astype(BF16), w_out[0].astype(BF16), wr2, br)
    dest1, dest2, tables, n_used = _plan_call(rank, cnt)

    xs = _sc_scatter_rows(_as_row_tiles(h2), dest1, dest2, N_BLOCKS * MOE_BLK)
    y = _expert_call(tables, n_used, _as_2d(xs), expert_w1[0], expert_w3[0], expert_w2[0])
    y3 = _as_row_tiles(y)
    half = TOKENS // 2
    out = None
    for part in range(2):
        tok = slice(part * half, (part + 1) * half)
        yg = _sc_gather_rows(y3, jnp.concatenate([dest1[tok], dest2[tok]]))
        out = _combine_rows_call(x1, rf, mod, _as_2d(yg), part, out)
    return out.reshape(BATCH, SEQ, D_MODEL)
```

```python
import math

import jax
import jax.numpy as jnp
from jax import lax
from jax.experimental import pallas as pl
from jax.experimental.pallas import tpu as pltpu
from jax.experimental.pallas import tpu_sc as plsc

F32 = jnp.float32
BF16 = jnp.bfloat16
I32 = jnp.int32

D_MODEL = 1024
BATCH = 4
SEQ = 4096
TOKENS = BATCH * SEQ
N_MOD = 6
NORM_EPS = 1e-6

GLA_HEADS = 4
GLA_DK = 64
GLA_DV = 128
GLA_GATE_RANK = 16
GLA_GATE_TAU = 16.0
GLA_CHUNK = 64
GLA_QK_W = GLA_HEADS * GLA_DK
GLA_V_W = GLA_HEADS * GLA_DV

DIFF_HEADS = 4
DIFF_DH = 64
DIFF_DV = 2 * DIFF_DH
DIFF_QK_W = DIFF_HEADS * 2 * DIFF_DH
DIFF_V_W = DIFF_HEADS * DIFF_DV
ROPE_THETA = 500000.0
ROPE_DIM = DIFF_DH // 4
ROPE_HALF = ROPE_DIM // 2
NEG_INF = -1e30
LAMBDA_INIT = 0.8 - 0.6 * 1.0

N_GROUPS = 4
EXPERTS_PER_GROUP = 8
N_EXPERTS = N_GROUPS * EXPERTS_PER_GROUP
TOP_K = 2
D_EXPERT = 512

LANES = 128
SUBLANES = 8
ROW_TILE_S = D_MODEL // LANES
SC_CORES = 2
SC_SUBCORES = 16
SC_WORKERS = SC_CORES * SC_SUBCORES
SC_CHUNK = 64
LR_PAD = LANES
D_IN_PAD = 2 * GLA_QK_W + 2 * GLA_V_W + 2 * DIFF_QK_W + DIFF_V_W + 2 * D_MODEL + LR_PAD
IN_COLS = {}
_col = 0
for _name, _width in (("gla_qk", 2 * GLA_QK_W), ("gla_v", GLA_V_W), ("gla_r", GLA_V_W),
                      ("diff_q", DIFF_QK_W), ("diff_k", DIFF_QK_W), ("diff_v", DIFF_V_W),
                      ("gate_a", D_MODEL), ("gate_b", D_MODEL), ("gla_lr", LR_PAD)):
    IN_COLS[_name] = (_col, _col + _width)
    _col += _width
assert _col == D_IN_PAD
WPREP_COLS = 512
WPREP_LR_STEP = (D_IN_PAD - LR_PAD) // WPREP_COLS
D_IN_ALLOC = (WPREP_LR_STEP + 1) * WPREP_COLS

TM_IN = 512
TQ = 512
V_ROWS = DIFF_DV + 16
NORM_BLK = 256
ATT_SUB = 32
Q_SCALE = DIFF_DH ** -0.5 * math.log2(math.e)
N_KV = SEQ // TQ
TT_GLA = 512
TM_MERGE = 512
TM_ROW = 512
MOE_BLK = 512
MOE_SUB = 256
N_BLOCKS = (TOKENS * TOP_K + N_EXPERTS * (MOE_BLK - 1) + MOE_BLK - 1) // MOE_BLK
ROUTE_W = 8

VMEM_LIMIT = 56 * 1024 * 1024


def _cparams(sem):
    return pltpu.CompilerParams(dimension_semantics=sem, vmem_limit_bytes=VMEM_LIMIT)


def _sigmoid(x):
    return 1.0 / (1.0 + jnp.exp(-x))


def _row_tile_spec(rows, index_map):
    return pl.BlockSpec((rows * ROW_TILE_S, LANES), index_map)


def _as_row_tiles(a2d):
    return a2d.reshape(a2d.shape[0] // ROW_TILE_S, ROW_TILE_S, LANES)


def _as_2d(a3d):
    return a3d.reshape(a3d.shape[0] * ROW_TILE_S, LANES)


def _store_row_tiles(ref, val, row0=0):
    for s in range(ROW_TILE_S):
        ref[pl.ds(row0 * ROW_TILE_S + s, val.shape[0], stride=ROW_TILE_S), :] = (
            val[:, s * LANES:(s + 1) * LANES])


def _load_row_tiles(ref, row0=0, rows=None):
    rows = ref.shape[0] // ROW_TILE_S if rows is None else rows
    return jnp.concatenate(
        [ref[pl.ds(row0 * ROW_TILE_S + s, rows, stride=ROW_TILE_S), :] for s in range(ROW_TILE_S)],
        axis=1)


def _mod_kernel(ct_ref, w_ref, b_ref, o_ref):
    ct = ct_ref[...]
    ca = ct * _sigmoid(ct)
    w = w_ref[...]
    rows = []
    for b in range(BATCH):
        rows.append(jnp.sum(w * ca[:, b:b + 1], axis=0, keepdims=True) + b_ref[...])
    rows.append(jnp.zeros((8 - BATCH, w.shape[1]), F32))
    o_ref[...] = jnp.concatenate(rows, axis=0)


def _mod_call(c, ada_w, ada_b):
    tn = D_MODEL
    ct = jnp.zeros((D_MODEL, 8), F32).at[:, :BATCH].set(c.T)
    out = pl.pallas_call(
        _mod_kernel,
        grid=(N_MOD,),
        in_specs=[
            pl.BlockSpec((D_MODEL, 8), lambda j: (0, 0)),
            pl.BlockSpec((D_MODEL, tn), lambda j: (0, j)),
            pl.BlockSpec((1, tn), lambda j: (0, j)),
        ],
        out_specs=pl.BlockSpec((8, tn), lambda j: (0, j)),
        out_shape=jax.ShapeDtypeStruct((8, N_MOD * D_MODEL), F32),
        compiler_params=_cparams(("arbitrary",)),
        name="mod",
    )(ct, ada_w, ada_b.reshape(1, N_MOD * D_MODEL))
    return out[:BATCH].reshape(BATCH, N_MOD, D_MODEL)


def _wprep_kernel(w_ref, o_ref):
    j = pl.program_id(0)
    t = w_ref[...].T
    lane = lax.broadcasted_iota(I32, t.shape, 1)
    @pl.when(j < WPREP_LR_STEP)
    def _():
        o_ref[...] = t.astype(BF16)

    @pl.when(j == WPREP_LR_STEP)
    def _():
        rep = jnp.where(lane < GLA_GATE_RANK, t,
                        jnp.where(lane < 2 * GLA_GATE_RANK, pltpu.roll(t, GLA_GATE_RANK, 1),
                                  jnp.where(lane < 3 * GLA_GATE_RANK,
                                            pltpu.roll(t, 2 * GLA_GATE_RANK, 1), 0.0)))
        o_ref[...] = rep.astype(BF16)


def _wprep_call(w_in0):
    w_t = w_in0.T
    lr0 = 2 * GLA_QK_W + 2 * GLA_V_W
    assert lr0 % WPREP_COLS == 0 and (D_IN_PAD - LR_PAD) % WPREP_COLS == 0

    def src_row(j):
        return jnp.where(j < lr0 // WPREP_COLS, j * WPREP_COLS,
                         jnp.where(j < WPREP_LR_STEP, j * WPREP_COLS + GLA_GATE_RANK, lr0))

    return pl.pallas_call(
        _wprep_kernel,
        grid=(WPREP_LR_STEP + 1,),
        in_specs=[pl.BlockSpec((pl.Element(WPREP_COLS), pl.Element(D_MODEL)),
                               lambda j: (pl.multiple_of(src_row(j), SUBLANES), 0))],
        out_specs=pl.BlockSpec((D_MODEL, WPREP_COLS), lambda j: (0, j)),
        out_shape=jax.ShapeDtypeStruct((D_MODEL, D_IN_ALLOC), BF16),
        compiler_params=_cparams(("parallel",)),
        name="w_prep",
    )(w_t)


def _in_kernel(x_ref, mod_ref, g1_ref, w_ref, bd_ref, qg_ref, kg_ref, pos_ref, invf_ref,
               gqk_ref, gv_ref, gr_ref, lr_ref, qT_ref, k_ref, vT_ref, sa_ref, sb_ref,
               cos_scr, sin_scr):
    ang = pos_ref[...].astype(F32) * invf_ref[...]
    cos_scr[...] = jnp.cos(ang)
    sin_scr[...] = jnp.sin(ang)

    x = x_ref[...]
    shift1 = mod_ref[0, 0:1, :]
    scale1 = mod_ref[0, 1:2, :]
    ms = jnp.mean(x * x, axis=-1, keepdims=True)
    h = (x * lax.rsqrt(ms + NORM_EPS) * g1_ref[...]) * (1.0 + scale1) + shift1
    hb = h.astype(BF16)

    def proj(name):
        c0, c1 = IN_COLS[name]
        return jnp.dot(hb, w_ref[:, c0:c1], preferred_element_type=F32)

    tm = x.shape[0]
    cos4 = jnp.concatenate([cos_scr[...]] * DIFF_HEADS, axis=1)
    sin4 = jnp.concatenate([sin_scr[...]] * DIFF_HEADS, axis=1)
    lane = lax.broadcasted_iota(I32, (tm, DIFF_QK_W), 1)
    first_half = (lane % DIFF_DH) < ROPE_HALF
    bd = bd_ref[...]

    def norm_rope(t, gain_row):
        t2 = t * t
        hi = t2.astype(BF16)
        lo = (t2 - hi.astype(F32)).astype(BF16)
        hw = bd.shape[0]
        gms = jnp.concatenate(
            [jnp.dot(hi[:, c0:c0 + hw], bd, preferred_element_type=F32)
             + jnp.dot(lo[:, c0:c0 + hw], bd, preferred_element_type=F32)
             for c0 in range(0, DIFF_QK_W, hw)], axis=1)
        t = t * lax.rsqrt(gms + NORM_EPS) * gain_row
        nxt = pltpu.roll(t, DIFF_QK_W - ROPE_HALF, 1)
        prv = pltpu.roll(t, ROPE_HALF, 1)
        return t * cos4 + jnp.where(first_half, -nxt, prv) * sin4

    dq_raw = proj("diff_q")
    dk_raw = proj("diff_k")
    gqk_ref[...] = proj("gla_qk").astype(BF16)
    dq = norm_rope(dq_raw, qg_ref[...]) * Q_SCALE
    gv_ref[...] = proj("gla_v").astype(BF16)
    dk = norm_rope(dk_raw, kg_ref[...])
    dv = proj("diff_v")
    for hd in range(DIFF_HEADS):
        sl = slice(hd * LANES, (hd + 1) * LANES)
        qT_ref[0, hd, 0] = dq[:, sl].T.astype(BF16)
        k_ref[0, hd, 0] = dk[:, sl].astype(BF16)
    gr_ref[...] = proj("gla_r").astype(BF16)
    for hd in range(DIFF_HEADS):
        sl = slice(hd * LANES, (hd + 1) * LANES)
        vT_ref[0, hd, 0] = jnp.concatenate(
            [dv[:, sl].T, jnp.ones((V_ROWS - DIFF_DV, tm), F32)], axis=0).astype(BF16)
    sa_ref[...] = _sigmoid(proj("gate_a")).astype(BF16)
    sb_ref[...] = _sigmoid(proj("gate_b")).astype(BF16)
    lr_ref[...] = proj("gla_lr")


def _in_call(x2, mod, norm1_g, w_in_p, bd, qg_row, kg_row, pos_col, invf_row):
    nb = SEQ // TM_IN
    tok_spec = lambda w: pl.BlockSpec((TM_IN, w), lambda i: (i, 0))
    const2 = lambda r, c: pl.BlockSpec((r, c), lambda i: (0, 0))
    out_shapes = (
        jax.ShapeDtypeStruct((TOKENS, 2 * GLA_QK_W), BF16),
        jax.ShapeDtypeStruct((TOKENS, GLA_V_W), BF16),
        jax.ShapeDtypeStruct((TOKENS, GLA_V_W), BF16),
        jax.ShapeDtypeStruct((TOKENS, LR_PAD), F32),
        jax.ShapeDtypeStruct((BATCH, DIFF_HEADS, N_KV, LANES, TQ), BF16),
        jax.ShapeDtypeStruct((BATCH, DIFF_HEADS, N_KV, TQ, LANES), BF16),
        jax.ShapeDtypeStruct((BATCH, DIFF_HEADS, N_KV, V_ROWS, TQ), BF16),
        jax.ShapeDtypeStruct((TOKENS, D_MODEL), BF16),
        jax.ShapeDtypeStruct((TOKENS, D_MODEL), BF16),
    )
    out_specs = (
        tok_spec(2 * GLA_QK_W), tok_spec(GLA_V_W), tok_spec(GLA_V_W), tok_spec(LR_PAD),
        pl.BlockSpec((1, DIFF_HEADS, 1, LANES, TM_IN), lambda i: (i // nb, 0, i % nb, 0, 0)),
        pl.BlockSpec((1, DIFF_HEADS, 1, TM_IN, LANES), lambda i: (i // nb, 0, i % nb, 0, 0)),
        pl.BlockSpec((1, DIFF_HEADS, 1, V_ROWS, TM_IN), lambda i: (i // nb, 0, i % nb, 0, 0)),
        tok_spec(D_MODEL), tok_spec(D_MODEL),
    )
    return pl.pallas_call(
        _in_kernel,
        grid=(TOKENS // TM_IN,),
        in_specs=[
            tok_spec(D_MODEL),
            pl.BlockSpec((1, N_MOD, D_MODEL), lambda i: (i // nb, 0, 0)),
            const2(1, D_MODEL),
            pl.BlockSpec((D_MODEL, D_IN_ALLOC), lambda i: (0, 0), pipeline_mode=pl.Buffered(1)),
            const2(NORM_BLK, NORM_BLK),
            const2(1, DIFF_QK_W), const2(1, DIFF_QK_W),
            tok_spec(1),
            const2(1, LANES),
        ],
        out_specs=out_specs,
        out_shape=out_shapes,
        scratch_shapes=[pltpu.VMEM((TM_IN, LANES), F32), pltpu.VMEM((TM_IN, LANES), F32)],
        compiler_params=_cparams(("parallel",)),
        name="in_proj",
    )(x2, mod, norm1_g, w_in_p, bd, qg_row, kg_row, pos_col, invf_row)


def _gla_kernel(qk_ref, v_ref, r_ref, lr_ref, au_ref, ab_ref, og_ref, o_ref,
                state_ref, oacc_ref, snew_ref):
    tt = qk_ref.shape[0]
    n_chunks = tt // GLA_CHUNK

    @pl.when(pl.program_id(1) == 0)
    def _():
        state_ref[...] = jnp.zeros_like(state_ref)

    lr = lr_ref[...]
    lr_hi = lr.astype(BF16)
    lr_lo = (lr - lr_hi.astype(F32)).astype(BF16)
    lr_lane = lax.broadcasted_iota(I32, lr.shape, 1)
    in_lo_group = (lr_lane >= GLA_GATE_RANK) & (lr_lane < 2 * GLA_GATE_RANK)
    stacked = jnp.where(in_lo_group, lr_lo, lr_hi)
    z = jnp.dot(stacked, au_ref[...], preferred_element_type=F32) + ab_ref[...]
    g = (jnp.minimum(z, 0.0) - jnp.log(1.0 + jnp.exp(-jnp.abs(z)))) * (1.0 / GLA_GATE_TAU)

    row = lax.broadcasted_iota(I32, (tt, GLA_QK_W), 0) % GLA_CHUNK
    b = g
    step = 1
    while step < GLA_CHUNK:
        b = b + jnp.where(row >= step, pltpu.roll(b, step, 0), 0.0)
        step *= 2

    b_last_rows = [b[c * GLA_CHUNK + GLA_CHUNK - 1:(c + 1) * GLA_CHUNK, :] for c in range(n_chunks)]
    b_last = jnp.concatenate(
        [jnp.broadcast_to(bl, (GLA_CHUNK, GLA_QK_W)) for bl in b_last_rows], axis=0)

    qk = qk_ref[...].astype(F32)
    q = qk[:, :GLA_QK_W] * (GLA_DK ** -0.5)
    k = qk[:, GLA_QK_W:]
    q_in = (q * jnp.exp(b)).astype(BF16)
    k_in = (k * jnp.exp(-b)).astype(BF16)
    k_dec = (k * jnp.exp(b_last - b)).astype(BF16)

    ci = lax.broadcasted_iota(I32, (GLA_CHUNK, GLA_CHUNK), 0)
    cj = lax.broadcasted_iota(I32, (GLA_CHUNK, GLA_CHUNK), 1)
    causal = ci >= cj

    pairs = [(c, hd) for c in range(n_chunks) for hd in range(GLA_HEADS)]

    def rows(c):
        return slice(c * GLA_CHUNK, (c + 1) * GLA_CHUNK)

    def kcols(hd):
        return slice(hd * GLA_DK, (hd + 1) * GLA_DK)

    def vcols(hd):
        return slice(hd * GLA_DV, (hd + 1) * GLA_DV)

    att = {}
    for c, hd in pairs:
        a = lax.dot_general(q_in[rows(c), kcols(hd)], k_in[rows(c), kcols(hd)],
                            (((1,), (1,)), ((), ())), preferred_element_type=F32)
        att[c, hd] = jnp.where(causal, a, 0.0).astype(BF16)
    for c, hd in pairs:
        oacc_ref[rows(c), vcols(hd)] = jnp.dot(att[c, hd], v_ref[rows(c), vcols(hd)],
                                               preferred_element_type=F32)
    for c, hd in pairs:
        snew_ref[c * GLA_HEADS + hd] = lax.dot_general(
            k_dec[rows(c), kcols(hd)], v_ref[rows(c), vcols(hd)],
            (((0,), (0,)), ((), ())), preferred_element_type=F32)

    decay_rows = jnp.exp(jnp.concatenate(
        b_last_rows + [jnp.zeros((LANES - n_chunks, GLA_QK_W), F32)], axis=0))
    decay_cols = decay_rows.T
    states = [state_ref[hd] for hd in range(GLA_HEADS)]
    for c in range(n_chunks):
        for hd in range(GLA_HEADS):
            s_prev = states[hd]
            oacc_ref[rows(c), vcols(hd)] += jnp.dot(
                q_in[rows(c), kcols(hd)], s_prev.astype(BF16), preferred_element_type=F32)
            dcol = decay_cols[kcols(hd), c:c + 1]
            states[hd] = s_prev * dcol + snew_ref[c * GLA_HEADS + hd]
    for hd in range(GLA_HEADS):
        state_ref[hd] = states[hd]

    for hd in range(GLA_HEADS):
        vs = slice(hd * GLA_DV, (hd + 1) * GLA_DV)
        oh = oacc_ref[:, vs]
        ms = jnp.mean(oh * oh, axis=-1, keepdims=True)
        y = oh * lax.rsqrt(ms + NORM_EPS) * og_ref[...]
        r = r_ref[:, vs].astype(F32)
        o_ref[:, vs] = (y * (r * _sigmoid(r))).astype(BF16)


def _gla_call(gqk, gv, gr, lr, au_stack, ab_row, og_row):
    nt = SEQ // TT_GLA
    tok = lambda w: pl.BlockSpec((TT_GLA, w), lambda b, t: (b * nt + t, 0))
    const2 = lambda r, c: pl.BlockSpec((r, c), lambda b, t: (0, 0))
    return pl.pallas_call(
        _gla_kernel,
        grid=(BATCH, nt),
        in_specs=[tok(2 * GLA_QK_W), tok(GLA_V_W), tok(GLA_V_W), tok(LR_PAD),
                  const2(LR_PAD, GLA_QK_W), const2(1, GLA_QK_W), const2(1, GLA_DV)],
        out_specs=tok(GLA_V_W),
        out_shape=jax.ShapeDtypeStruct((TOKENS, GLA_V_W), BF16),
        scratch_shapes=[pltpu.VMEM((GLA_HEADS, GLA_DK, GLA_DV), F32),
                        pltpu.VMEM((TT_GLA, GLA_V_W), F32),
                        pltpu.VMEM((TT_GLA // GLA_CHUNK * GLA_HEADS, GLA_DK, GLA_DV), F32)],
        compiler_params=_cparams(("parallel", "arbitrary")),
        name="gla",
    )(gqk, gv, gr, lr, au_stack, ab_row, og_row)


def _attn_kernel(qT_ref, k_ref, vT_ref, lam_ref, og_ref, o_ref,
                 q_scr, s_scr, p_scr, acc_scr, m_scr):
    _attn_load_q(qT_ref, q_scr, 0, 0)
    s_scr[0] = jnp.dot(k_ref[0, 0, 0], q_scr[0, 0], preferred_element_type=F32)

    def q_block(i, carry):
        _attn_q_block(i, qT_ref, k_ref, vT_ref, lam_ref, og_ref, o_ref,
                      q_scr, s_scr, p_scr, acc_scr, m_scr)
        return carry

    lax.fori_loop(0, N_KV, q_block, 0)


def _attn_load_q(qT_ref, q_scr, i, slot):
    qT = qT_ref[0, 0, i]
    rowq = lax.broadcasted_iota(I32, qT.shape, 0)
    zero = jnp.zeros_like(qT)
    q_scr[slot, 0] = jnp.where(rowq < DIFF_DH, qT, zero)
    q_scr[slot, 1] = jnp.where(rowq >= DIFF_DH, qT, zero)


def _attn_q_block(i, qT_ref, k_ref, vT_ref, lam_ref, og_ref, o_ref,
                  q_scr, s_scr, p_scr, acc_scr, m_scr):
    slot = i % 2
    m_scr[...] = jnp.full(m_scr.shape, NEG_INF, F32)
    acc_scr[...] = jnp.zeros(acc_scr.shape, F32)
    n_sub = TQ // ATT_SUB

    def fold8(t, op):
        return op(t.reshape(t.shape[0] // SUBLANES, SUBLANES, TQ), axis=0)

    def scores(c, j):
        s_scr[c] = jnp.dot(k_ref[0, 0, j], q_scr[slot, c], preferred_element_type=F32)

    def load_s(c, r, masked):
        s = s_scr[c, r * ATT_SUB:(r + 1) * ATT_SUB, :]
        if masked:
            key_i = lax.broadcasted_iota(I32, (ATT_SUB, TQ), 0) + r * ATT_SUB
            qry_i = lax.broadcasted_iota(I32, (ATT_SUB, TQ), 1)
            s = jnp.where(key_i <= qry_i, s, NEG_INF)
        return s

    def softmax_pv(c, j, masked):
        m8 = fold8(load_s(c, 0, masked), jnp.max)
        for r in range(1, n_sub):
            m8 = jnp.maximum(m8, fold8(load_s(c, r, masked), jnp.max))
        m_old = m_scr[c]
        m_new = jnp.maximum(m_old, jnp.max(m8, axis=0, keepdims=True))
        alpha = jnp.exp2(m_old - m_new)
        for r in range(n_sub):
            p = jnp.exp2(load_s(c, r, masked) - m_new)
            p_scr[c, r * ATT_SUB:(r + 1) * ATT_SUB, :] = p.astype(BF16)
        m_scr[c] = m_new
        acc_scr[c] = acc_scr[c] * alpha + jnp.dot(vT_ref[0, 0, j], p_scr[c],
                                                  preferred_element_type=F32)

    def body(j, carry):
        scores(1, j)
        softmax_pv(0, j, False)
        scores(0, j + 1)
        softmax_pv(1, j, False)
        return carry

    lax.fori_loop(0, i, body, 0)
    scores(1, i)
    softmax_pv(0, i, True)
    softmax_pv(1, i, True)

    nxt = jnp.minimum(i + 1, N_KV - 1)
    _attn_load_q(qT_ref, q_scr, nxt, 1 - slot)
    s_scr[0] = jnp.dot(k_ref[0, 0, 0], q_scr[1 - slot, 0], preferred_element_type=F32)

    l1 = acc_scr[0, DIFF_DV:DIFF_DV + 1, :]
    l2 = acc_scr[1, DIFF_DV:DIFF_DV + 1, :]

    lam_p = lam_ref[...]
    lam = (jnp.exp(jnp.sum(lam_p[0:1] * lam_p[1:2], axis=1, keepdims=True))
           - jnp.exp(jnp.sum(lam_p[2:3] * lam_p[3:4], axis=1, keepdims=True)) + LAMBDA_INIT)
    oT = acc_scr[0, :DIFF_DV, :] / l1 - lam * (acc_scr[1, :DIFF_DV, :] / l2)
    ms = jnp.mean(oT * oT, axis=0, keepdims=True)
    y = oT * lax.rsqrt(ms + NORM_EPS) * og_ref[...] * (1.0 - LAMBDA_INIT)
    o_ref[pl.ds(pl.multiple_of(i * TQ, TQ), TQ), :] = y.T.astype(BF16)


def _attn_call(qT, kk, vT, lam_p, og_col):
    return pl.pallas_call(
        _attn_kernel,
        grid=(BATCH, DIFF_HEADS),
        in_specs=[
            pl.BlockSpec((1, 1, N_KV, LANES, TQ), lambda b, h: (b, h, 0, 0, 0)),
            pl.BlockSpec((1, 1, N_KV, TQ, LANES), lambda b, h: (b, h, 0, 0, 0)),
            pl.BlockSpec((1, 1, N_KV, V_ROWS, TQ), lambda b, h: (b, h, 0, 0, 0)),
            pl.BlockSpec((4, DIFF_DH), lambda b, h: (0, 0)),
            pl.BlockSpec((DIFF_DV, 1), lambda b, h: (0, 0)),
        ],
        out_specs=pl.BlockSpec((SEQ, DIFF_DV), lambda b, h: (b, h)),
        out_shape=jax.ShapeDtypeStruct((TOKENS, DIFF_V_W), BF16),
        scratch_shapes=[pltpu.VMEM((2, 2, LANES, TQ), BF16),
                        pltpu.VMEM((2, TQ, TQ), F32),
                        pltpu.VMEM((2, TQ, TQ), BF16),
                        pltpu.VMEM((2, V_ROWS, TQ), F32),
                        pltpu.VMEM((2, 1, TQ), F32)],
        compiler_params=_cparams(("parallel", "parallel")),
        name="attn",
    )(qT, kk, vT, lam_p, og_col)


def _merge_kernel(x_ref, oa_ref, ob_ref, sa_ref, sb_ref, mod_ref, g2_ref, wa_ref, wb_ref, wo_ref,
                  wr2_ref, br_ref, x1_ref, h2_ref, rf_ref, rank_ref, cnt_ref,
                  carry_ref, tri_ref, lg_ref):
    tm = x_ref.shape[0]
    step = pl.program_id(0)

    @pl.when(step == 0)
    def _():
        carry_ref[...] = jnp.zeros_like(carry_ref)
        lg_ref[...] = jnp.zeros_like(lg_ref)
        ti = lax.broadcasted_iota(I32, (tm, tm), 0)
        tj = lax.broadcasted_iota(I32, (tm, tm), 1)
        tri_ref[...] = jnp.where(ti > tj, 1.0, 0.0).astype(BF16)

    ma = jnp.dot(oa_ref[...], wa_ref[...], preferred_element_type=F32)
    mb = jnp.dot(ob_ref[...], wb_ref[...], preferred_element_type=F32)
    merged = sa_ref[...].astype(F32) * ma + sb_ref[...].astype(F32) * mb
    y = jnp.dot(merged.astype(BF16), wo_ref[...], preferred_element_type=F32)
    logits = lg_ref[...]
    lane = lax.broadcasted_iota(I32, (tm, LANES), 1).astype(F32)
    ninf = -jnp.inf
    big = float(LANES)

    def first_argmax(v):
        vmax = jnp.max(v, axis=1, keepdims=True)
        idx = jnp.min(jnp.where(v == vmax, lane, big), axis=1, keepdims=True)
        return vmax, idx

    gl = jnp.where(lane < N_GROUPS, logits, ninf)
    gmax, gidx = first_argmax(gl)
    p_top = 1.0 / jnp.sum(jnp.exp(gl - gmax), axis=1, keepdims=True)
    lo = N_GROUPS + EXPERTS_PER_GROUP * gidx
    el = jnp.where((lane >= lo) & (lane < lo + EXPERTS_PER_GROUP), logits, ninf)
    e1max, e1 = first_argmax(el)
    e2max, e2 = first_argmax(jnp.where(lane == e1, ninf, el))
    t = jnp.exp(e2max - e1max)
    w1 = 1.0 / (1.0 + t)
    w2 = t / (1.0 + t)
    col = lax.broadcasted_iota(I32, (tm, ROUTE_W), 1)
    rf_ref[...] = jnp.where(col == 0, p_top * w1, jnp.where(col == 1, p_top * w2, 0.0))

    x1id = e1 - N_GROUPS
    x2id = e2 - N_GROUPS
    hit1 = lane == x1id
    hit2 = lane == x2id
    onehot = jnp.where(hit1 | hit2, 1.0, 0.0)
    before = jnp.dot(tri_ref[...], onehot.astype(BF16), preferred_element_type=F32) + carry_ref[...]
    r1 = jnp.sum(jnp.where(hit1, before, 0.0), axis=1, keepdims=True)
    r2 = jnp.sum(jnp.where(hit2, before, 0.0), axis=1, keepdims=True)
    real_tile = jnp.where(step > 0, 1.0, 0.0)
    carry_ref[...] = carry_ref[...] + real_tile * jnp.sum(onehot, axis=0, keepdims=True)
    cnt_ref[...] = carry_ref[...]
    cols = jnp.where(lane == 0.0, x1id, jnp.where(lane == 1.0, x2id,
                                                  jnp.where(lane == 2.0, r1,
                                                            jnp.where(lane == 3.0, r2, 0.0))))
    rank_ref[0] = cols.T[0:ROUTE_W, :].astype(I32)

    gate1 = mod_ref[0, 2:3, :]
    shift2 = mod_ref[0, 3:4, :]
    scale2 = mod_ref[0, 4:5, :]
    x1 = x_ref[...] + gate1 * y
    x1_ref[...] = x1
    ms = jnp.mean(x1 * x1, axis=-1, keepdims=True)
    h2 = (x1 * lax.rsqrt(ms + NORM_EPS) * g2_ref[...]) * (1.0 + scale2) + shift2
    _store_row_tiles(h2_ref, h2)
    h2_hi = h2.astype(BF16)
    h2_lo = (h2 - h2_hi.astype(F32)).astype(BF16)
    hi_both = jnp.dot(h2_hi, wr2_ref[...], preferred_element_type=F32)
    next_logits = (hi_both[:, :LANES] + hi_both[:, LANES:]
                   + jnp.dot(h2_lo, wr2_ref[:, :LANES], preferred_element_type=F32)) + br_ref[...]

    lg_ref[...] = next_logits


def _merge_call(x2, o_a, o_b, sa, sb, mod, norm2_g, wa, wb, wo, wr2, br):
    nb = SEQ // TM_MERGE
    n_tiles = TOKENS // TM_MERGE
    cur = lambda i: jnp.minimum(i, n_tiles - 1)
    prev = lambda i: jnp.maximum(i - 1, 0)
    tok = lambda w: pl.BlockSpec((TM_MERGE, w), lambda i: (cur(i), 0))
    const2 = lambda r, c: pl.BlockSpec((r, c), lambda i: (0, 0))
    return pl.pallas_call(
        _merge_kernel,
        grid=(n_tiles + 1,),
        in_specs=[tok(D_MODEL), tok(GLA_V_W), tok(DIFF_V_W), tok(D_MODEL), tok(D_MODEL),
                  pl.BlockSpec((1, N_MOD, D_MODEL), lambda i: (cur(i) // nb, 0, 0)),
                  const2(1, D_MODEL),
                  const2(GLA_V_W, D_MODEL), const2(DIFF_V_W, D_MODEL), const2(D_MODEL, D_MODEL),
                  const2(D_MODEL, 2 * LANES), const2(1, LANES)],
        out_specs=(tok(D_MODEL), _row_tile_spec(TM_MERGE, lambda i: (cur(i), 0)),
                   pl.BlockSpec((TM_MERGE, ROUTE_W), lambda i: (prev(i), 0)),
                   pl.BlockSpec((1, ROUTE_W, TM_MERGE), lambda i: (prev(i), 0, 0)),
                   pl.BlockSpec((1, LANES), lambda i: (0, 0))),
        out_shape=(jax.ShapeDtypeStruct((TOKENS, D_MODEL), F32),
                   jax.ShapeDtypeStruct((TOKENS * ROW_TILE_S, LANES), F32),
                   jax.ShapeDtypeStruct((TOKENS, ROUTE_W), F32),
                   jax.ShapeDtypeStruct((n_tiles, ROUTE_W, TM_MERGE), I32),
                   jax.ShapeDtypeStruct((1, LANES), F32)),
        scratch_shapes=[pltpu.VMEM((1, LANES), F32), pltpu.VMEM((TM_MERGE, TM_MERGE), BF16),
                        pltpu.VMEM((TM_MERGE, LANES), F32)],
        compiler_params=_cparams(("arbitrary",)),
        name="merge",
    )(x2, o_a, o_b, sa, sb, mod, norm2_g, wa, wb, wo, wr2, br)


def _plan_kernel(rk_ref, cnt_ref, dest_ref, blk_ref):
    n_tiles, _, tm = rk_ref.shape
    cnt = cnt_ref[...]
    padded = jnp.floor((cnt + (MOE_BLK - 1)) * (1.0 / MOE_BLK)) * MOE_BLK
    lane = lax.broadcasted_iota(I32, (1, LANES), 1)
    seg_end = padded
    step = 1
    while step < N_EXPERTS:
        seg_end = seg_end + jnp.where(lane >= step, pltpu.roll(seg_end, step, 1), 0.0)
        step *= 2
    seg_start = seg_end - padded
    valid_end = seg_start + cnt

    ei = lax.broadcasted_iota(I32, (LANES, LANES), 0)
    ej = lax.broadcasted_iota(I32, (LANES, LANES), 1)

    def to_col(rowv):
        return jnp.sum(jnp.where(ei == ej, rowv, 0.0), axis=1, keepdims=True)

    start_col, end_col, valid_col = to_col(seg_start), to_col(seg_end), to_col(valid_end)

    nb_pad = blk_ref.shape[1]
    e_sub = lax.broadcasted_iota(I32, (LANES, nb_pad), 0)
    b_start = (lax.broadcasted_iota(I32, (1, nb_pad), 1) * MOE_BLK).astype(F32)
    ends_before = jnp.where((e_sub < N_EXPERTS) & (end_col <= b_start), 1.0, 0.0)
    block_e = jnp.minimum(jnp.sum(ends_before, axis=0, keepdims=True), N_EXPERTS - 1.0)
    block_valid_end = jnp.sum(jnp.where(e_sub.astype(F32) == block_e, valid_col, 0.0),
                              axis=0, keepdims=True)
    n_valid = jnp.clip(block_valid_end - b_start, 0.0, float(MOE_BLK))
    n_used = jnp.max(seg_end, axis=1, keepdims=True) * (1.0 / MOE_BLK)
    nonempty = (to_col(cnt) > 0.0) & (e_sub < N_EXPERTS)
    e_subf = e_sub.astype(F32)
    no_next = float(LANES)
    nxt = jnp.min(jnp.where(nonempty & (e_subf > block_e), e_subf, no_next), axis=0, keepdims=True)
    nxt = jnp.where(nxt == no_next, -1.0, nxt)
    seg_idx = jnp.sum(jnp.where(nonempty & (e_subf < block_e), 1.0, 0.0), axis=0, keepdims=True)
    slot = seg_idx - 2.0 * jnp.floor(seg_idx * 0.5)
    blk_ref[...] = jnp.concatenate(
        [block_e, n_valid, jnp.broadcast_to(n_used, (1, nb_pad)), nxt, slot,
         jnp.zeros((ROUTE_W - 5, nb_pad), F32)], axis=0).astype(I32)

    e_tok = lax.broadcasted_iota(I32, (LANES, tm), 0)

    def tile(t, carry):
        rk = rk_ref[t]
        d = [jnp.sum(jnp.where(e_tok == rk[k:k + 1], start_col, 0.0), axis=0, keepdims=True)
             + rk[k + 2:k + 3].astype(F32) for k in range(TOP_K)]
        dest_ref[t] = jnp.concatenate(d + [jnp.zeros((ROUTE_W - TOP_K, tm), F32)], axis=0).astype(I32)
        return carry

    lax.fori_loop(0, n_tiles, tile, 0)


def _plan_call(rank_rows, cnt):
    n_tiles = TOKENS // TM_MERGE
    nb_pad = -(-N_BLOCKS // LANES) * LANES
    dest, blk = pl.pallas_call(
        _plan_kernel,
        out_shape=(jax.ShapeDtypeStruct((n_tiles, ROUTE_W, TM_MERGE), I32),
                   jax.ShapeDtypeStruct((ROUTE_W, nb_pad), I32)),
        compiler_params=pltpu.CompilerParams(vmem_limit_bytes=VMEM_LIMIT),
        name="plan",
    )(rank_rows, cnt)
    dest1 = dest[:, 0, :].reshape(TOKENS)
    dest2 = dest[:, 1, :].reshape(TOKENS)
    tables = tuple(blk[r, :N_BLOCKS] for r in (0, 1, 3, 4))
    return dest1, dest2, tables, blk[2, :1]


def _sc_params():
    return pltpu.CompilerParams(use_tc_tiling_on_sc=True)


def _sc_mesh():
    return plsc.VectorSubcoreMesh(core_axis_name="core", subcore_axis_name="subcore")


def _sc_worker_base(per_worker):
    wid = lax.axis_index("subcore") * SC_CORES + lax.axis_index("core")
    return wid * per_worker


def _sc_scatter_rows(src, dest1, dest2, n_out):
    n = src.shape[0]
    per_worker = n // SC_WORKERS
    assert per_worker * SC_WORKERS == n and per_worker % SC_CHUNK == 0

    def body(src_hbm, d1_hbm, d2_hbm, out_hbm, idx1_v, idx2_v, rows_v, lsem, ssem1, ssem2):
        base = _sc_worker_base(per_worker)

        @pl.loop(0, per_worker // SC_CHUNK)
        def _(j):
            start = pl.multiple_of(base + j * SC_CHUNK, SC_CHUNK)
            rows_in = pltpu.async_copy(src_hbm.at[pl.ds(start, SC_CHUNK)], rows_v, lsem)
            pltpu.sync_copy(d1_hbm.at[pl.ds(start, SC_CHUNK)], idx1_v)
            pltpu.sync_copy(d2_hbm.at[pl.ds(start, SC_CHUNK)], idx2_v)
            rows_in.wait()
            out1 = pltpu.async_copy(rows_v, out_hbm.at[idx1_v], ssem1)
            out2 = pltpu.async_copy(rows_v, out_hbm.at[idx2_v], ssem2)
            out1.wait()
            out2.wait()

    return pl.kernel(
        body,
        out_type=jax.ShapeDtypeStruct((n_out, ROW_TILE_S, LANES), F32),
        mesh=_sc_mesh(),
        scratch_types=[pltpu.VMEM((SC_CHUNK,), I32), pltpu.VMEM((SC_CHUNK,), I32),
                       pltpu.VMEM((SC_CHUNK, ROW_TILE_S, LANES), F32),
                       pltpu.SemaphoreType.DMA, pltpu.SemaphoreType.DMA, pltpu.SemaphoreType.DMA],
        compiler_params=_sc_params(),
        name="sc_dispatch",
    )(src, dest1, dest2)


def _sc_gather_rows(table, idx):
    n = idx.shape[0]
    per_worker = n // SC_WORKERS
    half = SC_CHUNK // 2
    assert per_worker * SC_WORKERS == n and per_worker % SC_CHUNK == 0

    def body(table_hbm, idx_hbm, out_hbm, idx_v, rows_a, rows_b, gsem_a, gsem_b, wsem_a, wsem_b):
        base = pl.multiple_of(_sc_worker_base(per_worker), SC_CHUNK)
        pltpu.sync_copy(idx_hbm.at[pl.ds(base, per_worker)], idx_v)

        @pl.loop(0, per_worker // SC_CHUNK)
        def _(j):
            off_a = pl.multiple_of(j * SC_CHUNK, SC_CHUNK)
            off_b = pl.multiple_of(j * SC_CHUNK + half, half)
            ga = pltpu.async_copy(table_hbm.at[idx_v.at[pl.ds(off_a, half)]], rows_a, gsem_a)
            gb = pltpu.async_copy(table_hbm.at[idx_v.at[pl.ds(off_b, half)]], rows_b, gsem_b)
            ga.wait()
            wa = pltpu.async_copy(rows_a, out_hbm.at[pl.ds(base + off_a, half)], wsem_a)
            gb.wait()
            wb = pltpu.async_copy(rows_b, out_hbm.at[pl.ds(base + off_b, half)], wsem_b)
            wa.wait()
            wb.wait()

    return pl.kernel(
        body,
        out_type=jax.ShapeDtypeStruct((n, ROW_TILE_S, LANES), F32),
        mesh=_sc_mesh(),
        scratch_types=[pltpu.VMEM((per_worker,), I32),
                       pltpu.VMEM((half, ROW_TILE_S, LANES), F32),
                       pltpu.VMEM((half, ROW_TILE_S, LANES), F32),
                       pltpu.SemaphoreType.DMA, pltpu.SemaphoreType.DMA,
                       pltpu.SemaphoreType.DMA, pltpu.SemaphoreType.DMA],
        compiler_params=_sc_params(),
        name="sc_gather",
    )(table, idx)


def _weight_fetch(w_hbm, stage, sem, e, slot):
    return [pltpu.make_async_copy(w.at[e], st.at[slot], sem.at[slot]) for w, st in zip(w_hbm, stage)]


def _expert_kernel(be_ref, nv_ref, nx_ref, sl_ref, nu_ref, xs_ref, w1_hbm, w3_hbm, w2_hbm, y_ref,
                   w1s, w3s, w2s, w1b, w3b, w2b, sem):
    i = pl.program_id(0)
    e = be_ref[i]
    used = i < nu_ref[0]
    first = (i == 0) | (e != be_ref[jnp.maximum(i - 1, 0)])
    w_hbm = (w1_hbm, w3_hbm, w2_hbm)
    stage = (w1s, w3s, w2s)

    @pl.when(used & first)
    def _():
        slot = sl_ref[i]

        @pl.when(i == 0)
        def _():
            for cp in _weight_fetch(w_hbm, stage, sem, e, slot):
                cp.start()

        for cp in _weight_fetch(w_hbm, stage, sem, e, slot):
            cp.wait()
        w1b[...] = w1s[slot].astype(BF16)
        w3b[...] = w3s[slot].astype(BF16)
        w2b[...] = w2s[slot].astype(BF16)

        @pl.when(nx_ref[i] >= 0)
        def _():
            for cp in _weight_fetch(w_hbm, stage, sem, nx_ref[i], 1 - slot):
                cp.start()

    n_parts = MOE_BLK // MOE_SUB

    def load_x(part):
        row0 = part * MOE_SUB
        row = lax.broadcasted_iota(I32, (MOE_SUB, D_MODEL), 0) + row0
        return jnp.where(row < nv_ref[i], _load_row_tiles(xs_ref, row0, MOE_SUB), 0.0).astype(BF16)

    def zero_y(part):
        y_ref[pl.ds(part * MOE_SUB * ROW_TILE_S, MOE_SUB * ROW_TILE_S), :] = jnp.zeros(
            (MOE_SUB * ROW_TILE_S, LANES), F32)

    def run(parts):
        hmid = []
        for part in parts:
            xb = load_x(part)
            a = jnp.dot(xb, w1b[...], preferred_element_type=F32)
            g = jnp.dot(xb, w3b[...], preferred_element_type=F32)
            hmid.append(((a * _sigmoid(a)) * g).astype(BF16))
        for part, hm in zip(parts, hmid):
            _store_row_tiles(y_ref, jnp.dot(hm, w2b[...], preferred_element_type=F32),
                             part * MOE_SUB)

    n_live = jnp.where(used, (nv_ref[i] + (MOE_SUB - 1)) // MOE_SUB, 0)
    for k in range(n_parts + 1):
        @pl.when(n_live == k)
        def _(k=k):
            if k:
                run(list(range(k)))
            for part in range(k, n_parts):
                zero_y(part)


def _expert_call(tables, n_used, xs, w1, w3, w2):
    n_tab = len(tables)
    grid_spec = pltpu.PrefetchScalarGridSpec(
        num_scalar_prefetch=n_tab + 1,
        grid=(N_BLOCKS,),
        in_specs=[_row_tile_spec(MOE_BLK, lambda i, *pf: (jnp.minimum(i, pf[n_tab][0] - 1), 0)),
                  pl.BlockSpec(memory_space=pl.ANY),
                  pl.BlockSpec(memory_space=pl.ANY),
                  pl.BlockSpec(memory_space=pl.ANY)],
        out_specs=_row_tile_spec(MOE_BLK, lambda i, *pf: (i, 0)),
        scratch_shapes=[pltpu.VMEM((2, D_MODEL, D_EXPERT), F32),
                        pltpu.VMEM((2, D_MODEL, D_EXPERT), F32),
                        pltpu.VMEM((2, D_EXPERT, D_MODEL), F32),
                        pltpu.VMEM((D_MODEL, D_EXPERT), BF16),
                        pltpu.VMEM((D_MODEL, D_EXPERT), BF16),
                        pltpu.VMEM((D_EXPERT, D_MODEL), BF16),
                        pltpu.SemaphoreType.DMA((2,))],
    )
    return pl.pallas_call(
        _expert_kernel,
        grid_spec=grid_spec,
        out_shape=jax.ShapeDtypeStruct((N_BLOCKS * MOE_BLK * ROW_TILE_S, LANES), F32),
        compiler_params=_cparams(("arbitrary",)),
        name="experts",
    )(*tables, n_used, xs, w1, w3, w2)


def _combine_rows_kernel(x1_ref, rf_ref, mod_ref, ya_ref, yb_ref, *rest):
    o_ref = rest[-1]
    rf = rf_ref[...]
    gate2 = mod_ref[0, 5:6, :]
    moe = rf[:, 0:1] * _load_row_tiles(ya_ref) + rf[:, 1:2] * _load_row_tiles(yb_ref)
    o_ref[...] = x1_ref[...] + gate2 * moe


def _combine_rows_call(x1, rf, mod, yg, part, prev=None):
    nb = SEQ // TM_ROW
    n_half = TOKENS // TM_ROW // 2
    t0 = part * n_half
    in_specs = [pl.BlockSpec((TM_ROW, D_MODEL), lambda i: (t0 + i, 0)),
                pl.BlockSpec((TM_ROW, ROUTE_W), lambda i: (t0 + i, 0)),
                pl.BlockSpec((1, N_MOD, D_MODEL), lambda i: ((t0 + i) // nb, 0, 0)),
                _row_tile_spec(TM_ROW, lambda i: (i, 0)),
                _row_tile_spec(TM_ROW, lambda i: (i + n_half, 0))]
    args = [x1, rf, mod, yg, yg]
    aliases = {}
    if prev is not None:
        in_specs.append(pl.BlockSpec(memory_space=pl.ANY))
        args.append(prev)
        aliases = {len(args) - 1: 0}
    return pl.pallas_call(
        _combine_rows_kernel,
        grid=(n_half,),
        in_specs=in_specs,
        out_specs=pl.BlockSpec((TM_ROW, D_MODEL), lambda i: (t0 + i, 0)),
        out_shape=jax.ShapeDtypeStruct((TOKENS, D_MODEL), F32),
        input_output_aliases=aliases,
        compiler_params=_cparams(("parallel",)),
        name="combine",
    )(*args)


def kernel(x, c, positions, ada_w, ada_b, norm1_g, norm2_g, w_in, gla_alpha_up, gla_alpha_b,
           gla_out_g, diff_q_g, diff_k_g, diff_lq1, diff_lk1, diff_lq2, diff_lk2, diff_out_g,
           w_branch_a, w_branch_b, w_out, router_group_w, router_group_b, router_expert_w,
           router_expert_b, expert_w1, expert_w3, expert_w2):
    assert x.shape == (BATCH, SEQ, D_MODEL) and ada_w.shape[0] == 1
    x2 = x.reshape(TOKENS, D_MODEL)

    w_in_p = _wprep_call(w_in[0])
    au_hi = gla_alpha_up[0].astype(BF16)
    au_lo = (gla_alpha_up[0] - au_hi.astype(F32)).astype(BF16)
    au_stack = jnp.concatenate(
        [au_hi, au_hi, au_lo, jnp.zeros((LR_PAD - 3 * GLA_GATE_RANK, GLA_QK_W), BF16)], axis=0)
    gid = jnp.arange(NORM_BLK) // DIFF_DH
    bd = jnp.where(gid[:, None] == gid[None, :], 1.0 / DIFF_DH, 0.0).astype(BF16)
    qg_row = jnp.tile(diff_q_g[0], DIFF_QK_W // DIFF_DH).reshape(1, DIFF_QK_W)
    kg_row = jnp.tile(diff_k_g[0], DIFF_QK_W // DIFF_DH).reshape(1, DIFF_QK_W)
    inv_freq = ROPE_THETA ** (-jnp.arange(ROPE_HALF, dtype=F32) / ROPE_HALF)
    invf64 = jnp.concatenate([inv_freq, inv_freq, jnp.zeros((DIFF_DH - ROPE_DIM,), F32)])
    invf_row = jnp.tile(invf64, LANES // DIFF_DH).reshape(1, LANES)
    pos_col = positions.reshape(TOKENS, 1)
    lam_p = jnp.concatenate([diff_lq1, diff_lk1, diff_lq2, diff_lk2], axis=0)
    wr = (jnp.zeros((D_MODEL, LANES), F32)
          .at[:, :N_GROUPS].set(router_group_w[0])
          .at[:, N_GROUPS:N_GROUPS + N_EXPERTS].set(router_expert_w[0]))
    wr_hi = wr.astype(BF16)
    wr2 = jnp.concatenate([wr_hi, (wr - wr_hi.astype(F32)).astype(BF16)], axis=1)
    br = (jnp.zeros((1, LANES), F32)
          .at[0, :N_GROUPS].set(router_group_b[0])
          .at[0, N_GROUPS:N_GROUPS + N_EXPERTS].set(router_expert_b[0]))

    mod = _mod_call(c, ada_w[0], ada_b[0])
    gqk, gv, gr, lr, qT, kk, vT, sa, sb = _in_call(
        x2, mod, norm1_g, w_in_p, bd, qg_row, kg_row, pos_col, invf_row)
    o_a = _gla_call(gqk, gv, gr, lr, au_stack, gla_alpha_b, gla_out_g)
    o_b = _attn_call(qT, kk, vT, lam_p, diff_out_g.reshape(DIFF_DV, 1))
    x1, h2, rf, rank, cnt = _merge_call(
        x2, o_a, o_b, sa, sb, mod, norm2_g, w_branch_a[0].astype(BF16),
        w_branch_b[0].astype(BF16), w_out[0].astype(BF16), wr2, br)
    dest1, dest2, tables, n_used = _plan_call(rank, cnt)

    xs = _sc_scatter_rows(_as_row_tiles(h2), dest1, dest2, N_BLOCKS * MOE_BLK)
    y = _expert_call(tables, n_used, _as_2d(xs), expert_w1[0], expert_w3[0], expert_w2[0])
    y3 = _as_row_tiles(y)
    half = TOKENS // 2
    out = None
    for part in range(2):
        tok = slice(part * half, (part + 1) * half)
        yg = _sc_gather_rows(y3, jnp.concatenate([dest1[tok], dest2[tok]]))
        out = _combine_rows_call(x1, rf, mod, _as_2d(yg), part, out)
    return out.reshape(BATCH, SEQ, D_MODEL)
```

```python
import math

import jax
import jax.numpy as jnp
from jax import lax
from jax.experimental import pallas as pl
from jax.experimental.pallas import tpu as pltpu
from jax.experimental.pallas import tpu_sc as plsc

F32 = jnp.float32
BF16 = jnp.bfloat16
I32 = jnp.int32

D_MODEL = 1024
BATCH = 4
SEQ = 4096
TOKENS = BATCH * SEQ
N_MOD = 6
NORM_EPS = 1e-6

GLA_HEADS = 4
GLA_DK = 64
GLA_DV = 128
GLA_GATE_RANK = 16
GLA_GATE_TAU = 16.0
GLA_CHUNK = 64
GLA_QK_W = GLA_HEADS * GLA_DK
GLA_V_W = GLA_HEADS * GLA_DV

DIFF_HEADS = 4
DIFF_DH = 64
DIFF_DV = 2 * DIFF_DH
DIFF_QK_W = DIFF_HEADS * 2 * DIFF_DH
DIFF_V_W = DIFF_HEADS * DIFF_DV
ROPE_THETA = 500000.0
ROPE_DIM = DIFF_DH // 4
ROPE_HALF = ROPE_DIM // 2
NEG_INF = -1e30
LAMBDA_INIT = 0.8 - 0.6 * 1.0

N_GROUPS = 4
EXPERTS_PER_GROUP = 8
N_EXPERTS = N_GROUPS * EXPERTS_PER_GROUP
TOP_K = 2
D_EXPERT = 512

LANES = 128
SUBLANES = 8
ROW_TILE_S = D_MODEL // LANES
SC_CORES = 2
SC_SUBCORES = 16
SC_WORKERS = SC_CORES * SC_SUBCORES
SC_CHUNK = 64
LR_PAD = LANES
D_IN_PAD = 2 * GLA_QK_W + 2 * GLA_V_W + 2 * DIFF_QK_W + DIFF_V_W + 2 * D_MODEL + LR_PAD
IN_COLS = {}
_col = 0
for _name, _width in (("gla_qk", 2 * GLA_QK_W), ("gla_v", GLA_V_W), ("gla_r", GLA_V_W),
                      ("diff_q", DIFF_QK_W), ("diff_k", DIFF_QK_W), ("diff_v", DIFF_V_W),
                      ("gate_a", D_MODEL), ("gate_b", D_MODEL), ("gla_lr", LR_PAD)):
    IN_COLS[_name] = (_col, _col + _width)
    _col += _width
assert _col == D_IN_PAD
WPREP_COLS = 512
WPREP_LR_STEP = (D_IN_PAD - LR_PAD) // WPREP_COLS
D_IN_ALLOC = (WPREP_LR_STEP + 1) * WPREP_COLS

TM_IN = 512
TQ = 512
V_ROWS = DIFF_DV + 16
NORM_BLK = 256
ATT_SUB = 32
Q_SCALE = DIFF_DH ** -0.5 * math.log2(math.e)
N_KV = SEQ // TQ
TT_GLA = 512
TM_MERGE = 512
TM_ROW = 512
MOE_BLK = 512
MOE_SUB = 256
N_BLOCKS = (TOKENS * TOP_K + N_EXPERTS * (MOE_BLK - 1) + MOE_BLK - 1) // MOE_BLK
ROUTE_W = 8

VMEM_LIMIT = 56 * 1024 * 1024


def _cparams(sem):
    return pltpu.CompilerParams(dimension_semantics=sem, vmem_limit_bytes=VMEM_LIMIT)


def _sigmoid(x):
    return 1.0 / (1.0 + jnp.exp(-x))


def _row_tile_spec(rows, index_map):
    return pl.BlockSpec((rows * ROW_TILE_S, LANES), index_map)


def _as_row_tiles(a2d):
    return a2d.reshape(a2d.shape[0] // ROW_TILE_S, ROW_TILE_S, LANES)


def _as_2d(a3d):
    return a3d.reshape(a3d.shape[0] * ROW_TILE_S, LANES)


def _store_row_tiles(ref, val, row0=0):
    for s in range(ROW_TILE_S):
        ref[pl.ds(row0 * ROW_TILE_S + s, val.shape[0], stride=ROW_TILE_S), :] = (
            val[:, s * LANES:(s + 1) * LANES])


def _load_row_tiles(ref, row0=0, rows=None):
    rows = ref.shape[0] // ROW_TILE_S if rows is None else rows
    return jnp.concatenate(
        [ref[pl.ds(row0 * ROW_TILE_S + s, rows, stride=ROW_TILE_S), :] for s in range(ROW_TILE_S)],
        axis=1)


def _mod_kernel(ct_ref, w_ref, b_ref, o_ref):
    ct = ct_ref[...]
    ca = ct * _sigmoid(ct)
    w = w_ref[...]
    rows = []
    for b in range(BATCH):
        rows.append(jnp.sum(w * ca[:, b:b + 1], axis=0, keepdims=True) + b_ref[...])
    rows.append(jnp.zeros((8 - BATCH, w.shape[1]), F32))
    o_ref[...] = jnp.concatenate(rows, axis=0)


def _mod_call(c, ada_w, ada_b):
    tn = D_MODEL
    ct = jnp.zeros((D_MODEL, 8), F32).at[:, :BATCH].set(c.T)
    out = pl.pallas_call(
        _mod_kernel,
        grid=(N_MOD,),
        in_specs=[
            pl.BlockSpec((D_MODEL, 8), lambda j: (0, 0)),
            pl.BlockSpec((D_MODEL, tn), lambda j: (0, j)),
            pl.BlockSpec((1, tn), lambda j: (0, j)),
        ],
        out_specs=pl.BlockSpec((8, tn), lambda j: (0, j)),
        out_shape=jax.ShapeDtypeStruct((8, N_MOD * D_MODEL), F32),
        compiler_params=_cparams(("arbitrary",)),
        name="mod",
    )(ct, ada_w, ada_b.reshape(1, N_MOD * D_MODEL))
    return out[:BATCH].reshape(BATCH, N_MOD, D_MODEL)


def _wprep_kernel(w_ref, o_ref):
    j = pl.program_id(0)
    t = w_ref[...].T
    lane = lax.broadcasted_iota(I32, t.shape, 1)
    @pl.when(j < WPREP_LR_STEP)
    def _():
        o_ref[...] = t.astype(BF16)

    @pl.when(j == WPREP_LR_STEP)
    def _():
        rep = jnp.where(lane < GLA_GATE_RANK, t,
                        jnp.where(lane < 2 * GLA_GATE_RANK, pltpu.roll(t, GLA_GATE_RANK, 1),
                                  jnp.where(lane < 3 * GLA_GATE_RANK,
                                            pltpu.roll(t, 2 * GLA_GATE_RANK, 1), 0.0)))
        o_ref[...] = rep.astype(BF16)


def _wprep_call(w_in0):
    w_t = w_in0.T
    lr0 = 2 * GLA_QK_W + 2 * GLA_V_W
    assert lr0 % WPREP_COLS == 0 and (D_IN_PAD - LR_PAD) % WPREP_COLS == 0

    def src_row(j):
        return jnp.where(j < lr0 // WPREP_COLS, j * WPREP_COLS,
                         jnp.where(j < WPREP_LR_STEP, j * WPREP_COLS + GLA_GATE_RANK, lr0))

    return pl.pallas_call(
        _wprep_kernel,
        grid=(WPREP_LR_STEP + 1,),
        in_specs=[pl.BlockSpec((pl.Element(WPREP_COLS), pl.Element(D_MODEL)),
                               lambda j: (pl.multiple_of(src_row(j), SUBLANES), 0))],
        out_specs=pl.BlockSpec((D_MODEL, WPREP_COLS), lambda j: (0, j)),
        out_shape=jax.ShapeDtypeStruct((D_MODEL, D_IN_ALLOC), BF16),
        compiler_params=_cparams(("parallel",)),
        name="w_prep",
    )(w_t)


def _in_kernel(x_ref, mod_ref, g1_ref, w_ref, bd_ref, qg_ref, kg_ref, pos_ref, invf_ref,
               gqk_ref, gv_ref, gr_ref, lr_ref, qT_ref, k_ref, vT_ref, sa_ref, sb_ref,
               cos_scr, sin_scr):
    ang = pos_ref[...].astype(F32) * invf_ref[...]
    cos_scr[...] = jnp.cos(ang)
    sin_scr[...] = jnp.sin(ang)

    x = x_ref[...]
    shift1 = mod_ref[0, 0:1, :]
    scale1 = mod_ref[0, 1:2, :]
    ms = jnp.mean(x * x, axis=-1, keepdims=True)
    h = (x * lax.rsqrt(ms + NORM_EPS) * g1_ref[...]) * (1.0 + scale1) + shift1
    hb = h.astype(BF16)

    def proj(name):
        c0, c1 = IN_COLS[name]
        return jnp.dot(hb, w_ref[:, c0:c1], preferred_element_type=F32)

    tm = x.shape[0]
    cos4 = jnp.concatenate([cos_scr[...]] * DIFF_HEADS, axis=1)
    sin4 = jnp.concatenate([sin_scr[...]] * DIFF_HEADS, axis=1)
    lane = lax.broadcasted_iota(I32, (tm, DIFF_QK_W), 1)
    first_half = (lane % DIFF_DH) < ROPE_HALF
    bd = bd_ref[...]

    def norm_rope(t, gain_row):
        t2 = t * t
        hi = t2.astype(BF16)
        lo = (t2 - hi.astype(F32)).astype(BF16)
        hw = bd.shape[0]
        gms = jnp.concatenate(
            [jnp.dot(hi[:, c0:c0 + hw], bd, preferred_element_type=F32)
             + jnp.dot(lo[:, c0:c0 + hw], bd, preferred_element_type=F32)
             for c0 in range(0, DIFF_QK_W, hw)], axis=1)
        t = t * lax.rsqrt(gms + NORM_EPS) * gain_row
        nxt = pltpu.roll(t, DIFF_QK_W - ROPE_HALF, 1)
        prv = pltpu.roll(t, ROPE_HALF, 1)
        return t * cos4 + jnp.where(first_half, -nxt, prv) * sin4

    dq_raw = proj("diff_q")
    dk_raw = proj("diff_k")
    gqk_ref[...] = proj("gla_qk").astype(BF16)
    dq = norm_rope(dq_raw, qg_ref[...]) * Q_SCALE
    gv_ref[...] = proj("gla_v").astype(BF16)
    dk = norm_rope(dk_raw, kg_ref[...])
    dv = proj("diff_v")
    for hd in range(DIFF_HEADS):
        sl = slice(hd * LANES, (hd + 1) * LANES)
        qT_ref[0, hd, 0] = dq[:, sl].T.astype(BF16)
        k_ref[0, hd, 0] = dk[:, sl].astype(BF16)
    gr_ref[...] = proj("gla_r").astype(BF16)
    for hd in range(DIFF_HEADS):
        sl = slice(hd * LANES, (hd + 1) * LANES)
        vT_ref[0, hd, 0] = jnp.concatenate(
            [dv[:, sl].T, jnp.ones((V_ROWS - DIFF_DV, tm), F32)], axis=0).astype(BF16)
    sa_ref[...] = _sigmoid(proj("gate_a")).astype(BF16)
    sb_ref[...] = _sigmoid(proj("gate_b")).astype(BF16)
    lr_ref[...] = proj("gla_lr")


def _in_call(x2, mod, norm1_g, w_in_p, bd, qg_row, kg_row, pos_col, invf_row):
    nb = SEQ // TM_IN
    tok_spec = lambda w: pl.BlockSpec((TM_IN, w), lambda i: (i, 0))
    const2 = lambda r, c: pl.BlockSpec((r, c), lambda i: (0, 0))
    out_shapes = (
        jax.ShapeDtypeStruct((TOKENS, 2 * GLA_QK_W), BF16),
        jax.ShapeDtypeStruct((TOKENS, GLA_V_W), BF16),
        jax.ShapeDtypeStruct((TOKENS, GLA_V_W), BF16),
        jax.ShapeDtypeStruct((TOKENS, LR_PAD), F32),
        jax.ShapeDtypeStruct((BATCH, DIFF_HEADS, N_KV, LANES, TQ), BF16),
        jax.ShapeDtypeStruct((BATCH, DIFF_HEADS, N_KV, TQ, LANES), BF16),
        jax.ShapeDtypeStruct((BATCH, DIFF_HEADS, N_KV, V_ROWS, TQ), BF16),
        jax.ShapeDtypeStruct((TOKENS, D_MODEL), BF16),
        jax.ShapeDtypeStruct((TOKENS, D_MODEL), BF16),
    )
    out_specs = (
        tok_spec(2 * GLA_QK_W), tok_spec(GLA_V_W), tok_spec(GLA_V_W), tok_spec(LR_PAD),
        pl.BlockSpec((1, DIFF_HEADS, 1, LANES, TM_IN), lambda i: (i // nb, 0, i % nb, 0, 0)),
        pl.BlockSpec((1, DIFF_HEADS, 1, TM_IN, LANES), lambda i: (i // nb, 0, i % nb, 0, 0)),
        pl.BlockSpec((1, DIFF_HEADS, 1, V_ROWS, TM_IN), lambda i: (i // nb, 0, i % nb, 0, 0)),
        tok_spec(D_MODEL), tok_spec(D_MODEL),
    )
    return pl.pallas_call(
        _in_kernel,
        grid=(TOKENS // TM_IN,),
        in_specs=[
            tok_spec(D_MODEL),
            pl.BlockSpec((1, N_MOD, D_MODEL), lambda i: (i // nb, 0, 0)),
            const2(1, D_MODEL),
            pl.BlockSpec((D_MODEL, D_IN_ALLOC), lambda i: (0, 0), pipeline_mode=pl.Buffered(1)),
            const2(NORM_BLK, NORM_BLK),
            const2(1, DIFF_QK_W), const2(1, DIFF_QK_W),
            tok_spec(1),
            const2(1, LANES),
        ],
        out_specs=out_specs,
        out_shape=out_shapes,
        scratch_shapes=[pltpu.VMEM((TM_IN, LANES), F32), pltpu.VMEM((TM_IN, LANES), F32)],
        compiler_params=_cparams(("parallel",)),
        name="in_proj",
    )(x2, mod, norm1_g, w_in_p, bd, qg_row, kg_row, pos_col, invf_row)


def _gla_kernel(qk_ref, v_ref, r_ref, lr_ref, au_ref, ab_ref, og_ref, o_ref,
                state_ref, oacc_ref, snew_ref):
    tt = qk_ref.shape[0]
    n_chunks = tt // GLA_CHUNK

    @pl.when(pl.program_id(1) == 0)
    def _():
        state_ref[...] = jnp.zeros_like(state_ref)

    lr = lr_ref[...]
    lr_hi = lr.astype(BF16)
    lr_lo = (lr - lr_hi.astype(F32)).astype(BF16)
    lr_lane = lax.broadcasted_iota(I32, lr.shape, 1)
    in_lo_group = (lr_lane >= GLA_GATE_RANK) & (lr_lane < 2 * GLA_GATE_RANK)
    stacked = jnp.where(in_lo_group, lr_lo, lr_hi)
    z = jnp.dot(stacked, au_ref[...], preferred_element_type=F32) + ab_ref[...]
    g = (jnp.minimum(z, 0.0) - jnp.log(1.0 + jnp.exp(-jnp.abs(z)))) * (1.0 / GLA_GATE_TAU)

    row = lax.broadcasted_iota(I32, (tt, GLA_QK_W), 0) % GLA_CHUNK
    b = g
    step = 1
    while step < GLA_CHUNK:
        b = b + jnp.where(row >= step, pltpu.roll(b, step, 0), 0.0)
        step *= 2

    b_last_rows = [b[c * GLA_CHUNK + GLA_CHUNK - 1:(c + 1) * GLA_CHUNK, :] for c in range(n_chunks)]
    b_last = jnp.concatenate(
        [jnp.broadcast_to(bl, (GLA_CHUNK, GLA_QK_W)) for bl in b_last_rows], axis=0)

    qk = qk_ref[...].astype(F32)
    q = qk[:, :GLA_QK_W] * (GLA_DK ** -0.5)
    k = qk[:, GLA_QK_W:]
    q_in = (q * jnp.exp(b)).astype(BF16)
    k_in = (k * jnp.exp(-b)).astype(BF16)
    k_dec = (k * jnp.exp(b_last - b)).astype(BF16)

    ci = lax.broadcasted_iota(I32, (GLA_CHUNK, GLA_CHUNK), 0)
    cj = lax.broadcasted_iota(I32, (GLA_CHUNK, GLA_CHUNK), 1)
    causal = ci >= cj

    pairs = [(c, hd) for c in range(n_chunks) for hd in range(GLA_HEADS)]

    def rows(c):
        return slice(c * GLA_CHUNK, (c + 1) * GLA_CHUNK)

    def kcols(hd):
        return slice(hd * GLA_DK, (hd + 1) * GLA_DK)

    def vcols(hd):
        return slice(hd * GLA_DV, (hd + 1) * GLA_DV)

    att = {}
    for c, hd in pairs:
        a = lax.dot_general(q_in[rows(c), kcols(hd)], k_in[rows(c), kcols(hd)],
                            (((1,), (1,)), ((), ())), preferred_element_type=F32)
        att[c, hd] = jnp.where(causal, a, 0.0).astype(BF16)
    for c, hd in pairs:
        oacc_ref[rows(c), vcols(hd)] = jnp.dot(att[c, hd], v_ref[rows(c), vcols(hd)],
                                               preferred_element_type=F32)
    for c, hd in pairs:
        snew_ref[c * GLA_HEADS + hd] = lax.dot_general(
            k_dec[rows(c), kcols(hd)], v_ref[rows(c), vcols(hd)],
            (((0,), (0,)), ((), ())), preferred_element_type=F32)

    decay_rows = jnp.exp(jnp.concatenate(
        b_last_rows + [jnp.zeros((LANES - n_chunks, GLA_QK_W), F32)], axis=0))
    decay_cols = decay_rows.T
    states = [state_ref[hd] for hd in range(GLA_HEADS)]
    for c in range(n_chunks):
        for hd in range(GLA_HEADS):
            s_prev = states[hd]
            oacc_ref[rows(c), vcols(hd)] += jnp.dot(
                q_in[rows(c), kcols(hd)], s_prev.astype(BF16), preferred_element_type=F32)
            dcol = decay_cols[kcols(hd), c:c + 1]
            states[hd] = s_prev * dcol + snew_ref[c * GLA_HEADS + hd]
    for hd in range(GLA_HEADS):
        state_ref[hd] = states[hd]

    for hd in range(GLA_HEADS):
        vs = slice(hd * GLA_DV, (hd + 1) * GLA_DV)
        oh = oacc_ref[:, vs]
        ms = jnp.mean(oh * oh, axis=-1, keepdims=True)
        y = oh * lax.rsqrt(ms + NORM_EPS) * og_ref[...]
        r = r_ref[:, vs].astype(F32)
        o_ref[:, vs] = (y * (r * _sigmoid(r))).astype(BF16)


def _gla_call(gqk, gv, gr, lr, au_stack, ab_row, og_row):
    nt = SEQ // TT_GLA
    tok = lambda w: pl.BlockSpec((TT_GLA, w), lambda b, t: (b * nt + t, 0))
    const2 = lambda r, c: pl.BlockSpec((r, c), lambda b, t: (0, 0))
    return pl.pallas_call(
        _gla_kernel,
        grid=(BATCH, nt),
        in_specs=[tok(2 * GLA_QK_W), tok(GLA_V_W), tok(GLA_V_W), tok(LR_PAD),
                  const2(LR_PAD, GLA_QK_W), const2(1, GLA_QK_W), const2(1, GLA_DV)],
        out_specs=tok(GLA_V_W),
        out_shape=jax.ShapeDtypeStruct((TOKENS, GLA_V_W), BF16),
        scratch_shapes=[pltpu.VMEM((GLA_HEADS, GLA_DK, GLA_DV), F32),
                        pltpu.VMEM((TT_GLA, GLA_V_W), F32),
                        pltpu.VMEM((TT_GLA // GLA_CHUNK * GLA_HEADS, GLA_DK, GLA_DV), F32)],
        compiler_params=_cparams(("parallel", "arbitrary")),
        name="gla",
    )(gqk, gv, gr, lr, au_stack, ab_row, og_row)


def _attn_kernel(qT_ref, k_ref, vT_ref, lam_ref, og_ref, o_ref,
                 q_scr, s_scr, p_scr, acc_scr, m_scr):
    _attn_load_q(qT_ref, q_scr, 0, 0)
    s_scr[0] = jnp.dot(k_ref[0, 0, 0], q_scr[0, 0], preferred_element_type=F32)

    def q_block(i, carry):
        _attn_q_block(i, qT_ref, k_ref, vT_ref, lam_ref, og_ref, o_ref,
                      q_scr, s_scr, p_scr, acc_scr, m_scr)
        return carry

    lax.fori_loop(0, N_KV, q_block, 0)


def _attn_load_q(qT_ref, q_scr, i, slot):
    qT = qT_ref[0, 0, i]
    rowq = lax.broadcasted_iota(I32, qT.shape, 0)
    zero = jnp.zeros_like(qT)
    q_scr[slot, 0] = jnp.where(rowq < DIFF_DH, qT, zero)
    q_scr[slot, 1] = jnp.where(rowq >= DIFF_DH, qT, zero)


def _attn_q_block(i, qT_ref, k_ref, vT_ref, lam_ref, og_ref, o_ref,
                  q_scr, s_scr, p_scr, acc_scr, m_scr):
    slot = i % 2
    n_sub = TQ // ATT_SUB

    def fold8(t, op):
        return op(t.reshape(t.shape[0] // SUBLANES, SUBLANES, TQ), axis=0)

    def scores(c, j):
        s_scr[c] = jnp.dot(k_ref[0, 0, j], q_scr[slot, c], preferred_element_type=F32)

    def load_s(c, r, mask, j=None):
        s = s_scr[c, r * ATT_SUB:(r + 1) * ATT_SUB, :]
        if mask is not None:
            key_i = lax.broadcasted_iota(I32, (ATT_SUB, TQ), 0) + r * ATT_SUB
            qry_i = lax.broadcasted_iota(I32, (ATT_SUB, TQ), 1)
            keep = key_i <= qry_i
            if mask == "maybe":
                keep = jnp.logical_or(keep, j < i)
            s = jnp.where(keep, s, NEG_INF)
        return s

    def block_max(c, mask, j=None):
        m8 = fold8(load_s(c, 0, mask, j), jnp.max)
        for r in range(1, n_sub):
            m8 = jnp.maximum(m8, fold8(load_s(c, r, mask, j), jnp.max))
        return jnp.max(m8, axis=0, keepdims=True)

    def exp_to_p(c, shift, mask):
        for r in range(n_sub):
            p = jnp.exp2(load_s(c, r, mask) - shift)
            p_scr[c, r * ATT_SUB:(r + 1) * ATT_SUB, :] = p.astype(BF16)

    def p_times_v(c, j):
        return jnp.dot(vT_ref[0, 0, j], p_scr[c], preferred_element_type=F32)

    acc_scr[...] = jnp.zeros(acc_scr.shape, F32)

    def fast_pv(c, j, mask):
        exp_to_p(c, m_scr[c], mask)
        acc_scr[c] = acc_scr[c] + p_times_v(c, j)

    def fast_body(j, carry):
        scores(1, j)
        fast_pv(0, j, None)
        scores(0, j + 1)
        fast_pv(1, j, None)
        return carry

    scores(1, 0)
    m_scr[0] = block_max(0, "maybe", 0)
    m_scr[1] = block_max(1, "maybe", 0)
    lax.fori_loop(0, i, fast_body, 0)
    scores(1, i)
    fast_pv(0, i, "diag")
    fast_pv(1, i, "diag")

    total = jnp.sum(jnp.abs(acc_scr[0])) + jnp.sum(jnp.abs(acc_scr[1]))
    overflowed = jnp.logical_not(total < jnp.inf)

    @pl.when(overflowed)
    def _():
        m_scr[...] = jnp.full(m_scr.shape, NEG_INF, F32)
        acc_scr[...] = jnp.zeros(acc_scr.shape, F32)

        def safe_pv(c, j, mask):
            m_old = m_scr[c]
            m_new = jnp.maximum(m_old, block_max(c, mask))
            alpha = jnp.exp2(m_old - m_new)
            exp_to_p(c, m_new, mask)
            m_scr[c] = m_new
            acc_scr[c] = acc_scr[c] * alpha + p_times_v(c, j)

        def safe_body(j, carry):
            scores(0, j)
            scores(1, j)
            safe_pv(0, j, None)
            safe_pv(1, j, None)
            return carry

        lax.fori_loop(0, i, safe_body, 0)
        scores(0, i)
        scores(1, i)
        safe_pv(0, i, "diag")
        safe_pv(1, i, "diag")

    nxt = jnp.minimum(i + 1, N_KV - 1)
    _attn_load_q(qT_ref, q_scr, nxt, 1 - slot)
    s_scr[0] = jnp.dot(k_ref[0, 0, 0], q_scr[1 - slot, 0], preferred_element_type=F32)

    l1 = acc_scr[0, DIFF_DV:DIFF_DV + 1, :]
    l2 = acc_scr[1, DIFF_DV:DIFF_DV + 1, :]

    lam_p = lam_ref[...]
    lam = (jnp.exp(jnp.sum(lam_p[0:1] * lam_p[1:2], axis=1, keepdims=True))
           - jnp.exp(jnp.sum(lam_p[2:3] * lam_p[3:4], axis=1, keepdims=True)) + LAMBDA_INIT)
    oT = acc_scr[0, :DIFF_DV, :] / l1 - lam * (acc_scr[1, :DIFF_DV, :] / l2)
    ms = jnp.mean(oT * oT, axis=0, keepdims=True)
    y = oT * lax.rsqrt(ms + NORM_EPS) * og_ref[...] * (1.0 - LAMBDA_INIT)
    o_ref[pl.ds(pl.multiple_of(i * TQ, TQ), TQ), :] = y.T.astype(BF16)


def _attn_call(qT, kk, vT, lam_p, og_col):
    return pl.pallas_call(
        _attn_kernel,
        grid=(BATCH, DIFF_HEADS),
        in_specs=[
            pl.BlockSpec((1, 1, N_KV, LANES, TQ), lambda b, h: (b, h, 0, 0, 0)),
            pl.BlockSpec((1, 1, N_KV, TQ, LANES), lambda b, h: (b, h, 0, 0, 0)),
            pl.BlockSpec((1, 1, N_KV, V_ROWS, TQ), lambda b, h: (b, h, 0, 0, 0)),
            pl.BlockSpec((4, DIFF_DH), lambda b, h: (0, 0)),
            pl.BlockSpec((DIFF_DV, 1), lambda b, h: (0, 0)),
        ],
        out_specs=pl.BlockSpec((SEQ, DIFF_DV), lambda b, h: (b, h)),
        out_shape=jax.ShapeDtypeStruct((TOKENS, DIFF_V_W), BF16),
        scratch_shapes=[pltpu.VMEM((2, 2, LANES, TQ), BF16),
                        pltpu.VMEM((2, TQ, TQ), F32),
                        pltpu.VMEM((2, TQ, TQ), BF16),
                        pltpu.VMEM((2, V_ROWS, TQ), F32),
                        pltpu.VMEM((2, 1, TQ), F32)],
        compiler_params=_cparams(("parallel", "parallel")),
        name="attn",
    )(qT, kk, vT, lam_p, og_col)


def _merge_kernel(x_ref, oa_ref, ob_ref, sa_ref, sb_ref, mod_ref, g2_ref, wa_ref, wb_ref, wo_ref,
                  wr2_ref, br_ref, x1_ref, h2_ref, rf_ref, rank_ref, cnt_ref,
                  carry_ref, tri_ref, lg_ref):
    tm = x_ref.shape[0]
    step = pl.program_id(0)

    @pl.when(step == 0)
    def _():
        carry_ref[...] = jnp.zeros_like(carry_ref)
        lg_ref[...] = jnp.zeros_like(lg_ref)
        ti = lax.broadcasted_iota(I32, (tm, tm), 0)
        tj = lax.broadcasted_iota(I32, (tm, tm), 1)
        tri_ref[...] = jnp.where(ti > tj, 1.0, 0.0).astype(BF16)

    ma = jnp.dot(oa_ref[...], wa_ref[...], preferred_element_type=F32)
    mb = jnp.dot(ob_ref[...], wb_ref[...], preferred_element_type=F32)
    merged = sa_ref[...].astype(F32) * ma + sb_ref[...].astype(F32) * mb
    y = jnp.dot(merged.astype(BF16), wo_ref[...], preferred_element_type=F32)
    logits = lg_ref[...]
    lane = lax.broadcasted_iota(I32, (tm, LANES), 1).astype(F32)
    ninf = -jnp.inf
    big = float(LANES)

    def first_argmax(v):
        vmax = jnp.max(v, axis=1, keepdims=True)
        idx = jnp.min(jnp.where(v == vmax, lane, big), axis=1, keepdims=True)
        return vmax, idx

    gl = jnp.where(lane < N_GROUPS, logits, ninf)
    gmax, gidx = first_argmax(gl)
    p_top = 1.0 / jnp.sum(jnp.exp(gl - gmax), axis=1, keepdims=True)
    lo = N_GROUPS + EXPERTS_PER_GROUP * gidx
    el = jnp.where((lane >= lo) & (lane < lo + EXPERTS_PER_GROUP), logits, ninf)
    e1max, e1 = first_argmax(el)
    e2max, e2 = first_argmax(jnp.where(lane == e1, ninf, el))
    t = jnp.exp(e2max - e1max)
    w1 = 1.0 / (1.0 + t)
    w2 = t / (1.0 + t)
    col = lax.broadcasted_iota(I32, (tm, ROUTE_W), 1)
    rf_ref[...] = jnp.where(col == 0, p_top * w1, jnp.where(col == 1, p_top * w2, 0.0))

    x1id = e1 - N_GROUPS
    x2id = e2 - N_GROUPS
    hit1 = lane == x1id
    hit2 = lane == x2id
    onehot = jnp.where(hit1 | hit2, 1.0, 0.0)
    before = jnp.dot(tri_ref[...], onehot.astype(BF16), preferred_element_type=F32) + carry_ref[...]
    r1 = jnp.sum(jnp.where(hit1, before, 0.0), axis=1, keepdims=True)
    r2 = jnp.sum(jnp.where(hit2, before, 0.0), axis=1, keepdims=True)
    real_tile = jnp.where(step > 0, 1.0, 0.0)
    carry_ref[...] = carry_ref[...] + real_tile * jnp.sum(onehot, axis=0, keepdims=True)
    cnt_ref[...] = carry_ref[...]
    cols = jnp.where(lane == 0.0, x1id, jnp.where(lane == 1.0, x2id,
                                                  jnp.where(lane == 2.0, r1,
                                                            jnp.where(lane == 3.0, r2, 0.0))))
    rank_ref[0] = cols.T[0:ROUTE_W, :].astype(I32)

    gate1 = mod_ref[0, 2:3, :]
    shift2 = mod_ref[0, 3:4, :]
    scale2 = mod_ref[0, 4:5, :]
    x1 = x_ref[...] + gate1 * y
    x1_ref[...] = x1
    ms = jnp.mean(x1 * x1, axis=-1, keepdims=True)
    h2 = (x1 * lax.rsqrt(ms + NORM_EPS) * g2_ref[...]) * (1.0 + scale2) + shift2
    _store_row_tiles(h2_ref, h2)
    h2_hi = h2.astype(BF16)
    h2_lo = (h2 - h2_hi.astype(F32)).astype(BF16)
    hi_both = jnp.dot(h2_hi, wr2_ref[...], preferred_element_type=F32)
    next_logits = (hi_both[:, :LANES] + hi_both[:, LANES:]
                   + jnp.dot(h2_lo, wr2_ref[:, :LANES], preferred_element_type=F32)) + br_ref[...]

    lg_ref[...] = next_logits


def _merge_call(x2, o_a, o_b, sa, sb, mod, norm2_g, wa, wb, wo, wr2, br):
    nb = SEQ // TM_MERGE
    n_tiles = TOKENS // TM_MERGE
    cur = lambda i: jnp.minimum(i, n_tiles - 1)
    prev = lambda i: jnp.maximum(i - 1, 0)
    tok = lambda w: pl.BlockSpec((TM_MERGE, w), lambda i: (cur(i), 0))
    const2 = lambda r, c: pl.BlockSpec((r, c), lambda i: (0, 0))
    return pl.pallas_call(
        _merge_kernel,
        grid=(n_tiles + 1,),
        in_specs=[tok(D_MODEL), tok(GLA_V_W), tok(DIFF_V_W), tok(D_MODEL), tok(D_MODEL),
                  pl.BlockSpec((1, N_MOD, D_MODEL), lambda i: (cur(i) // nb, 0, 0)),
                  const2(1, D_MODEL),
                  const2(GLA_V_W, D_MODEL), const2(DIFF_V_W, D_MODEL), const2(D_MODEL, D_MODEL),
                  const2(D_MODEL, 2 * LANES), const2(1, LANES)],
        out_specs=(tok(D_MODEL), _row_tile_spec(TM_MERGE, lambda i: (cur(i), 0)),
                   pl.BlockSpec((TM_MERGE, ROUTE_W), lambda i: (prev(i), 0)),
                   pl.BlockSpec((1, ROUTE_W, TM_MERGE), lambda i: (prev(i), 0, 0)),
                   pl.BlockSpec((1, LANES), lambda i: (0, 0))),
        out_shape=(jax.ShapeDtypeStruct((TOKENS, D_MODEL), F32),
                   jax.ShapeDtypeStruct((TOKENS * ROW_TILE_S, LANES), F32),
                   jax.ShapeDtypeStruct((TOKENS, ROUTE_W), F32),
                   jax.ShapeDtypeStruct((n_tiles, ROUTE_W, TM_MERGE), I32),
                   jax.ShapeDtypeStruct((1, LANES), F32)),
        scratch_shapes=[pltpu.VMEM((1, LANES), F32), pltpu.VMEM((TM_MERGE, TM_MERGE), BF16),
                        pltpu.VMEM((TM_MERGE, LANES), F32)],
        compiler_params=_cparams(("arbitrary",)),
        name="merge",
    )(x2, o_a, o_b, sa, sb, mod, norm2_g, wa, wb, wo, wr2, br)


def _plan_kernel(rk_ref, cnt_ref, dest_ref, blk_ref):
    n_tiles, _, tm = rk_ref.shape
    cnt = cnt_ref[...]
    padded = jnp.floor((cnt + (MOE_BLK - 1)) * (1.0 / MOE_BLK)) * MOE_BLK
    lane = lax.broadcasted_iota(I32, (1, LANES), 1)
    seg_end = padded
    step = 1
    while step < N_EXPERTS:
        seg_end = seg_end + jnp.where(lane >= step, pltpu.roll(seg_end, step, 1), 0.0)
        step *= 2
    seg_start = seg_end - padded
    valid_end = seg_start + cnt

    ei = lax.broadcasted_iota(I32, (LANES, LANES), 0)
    ej = lax.broadcasted_iota(I32, (LANES, LANES), 1)

    def to_col(rowv):
        return jnp.sum(jnp.where(ei == ej, rowv, 0.0), axis=1, keepdims=True)

    start_col, end_col, valid_col = to_col(seg_start), to_col(seg_end), to_col(valid_end)

    nb_pad = blk_ref.shape[1]
    e_sub = lax.broadcasted_iota(I32, (LANES, nb_pad), 0)
    b_start = (lax.broadcasted_iota(I32, (1, nb_pad), 1) * MOE_BLK).astype(F32)
    ends_before = jnp.where((e_sub < N_EXPERTS) & (end_col <= b_start), 1.0, 0.0)
    block_e = jnp.minimum(jnp.sum(ends_before, axis=0, keepdims=True), N_EXPERTS - 1.0)
    block_valid_end = jnp.sum(jnp.where(e_sub.astype(F32) == block_e, valid_col, 0.0),
                              axis=0, keepdims=True)
    n_valid = jnp.clip(block_valid_end - b_start, 0.0, float(MOE_BLK))
    n_used = jnp.max(seg_end, axis=1, keepdims=True) * (1.0 / MOE_BLK)
    nonempty = (to_col(cnt) > 0.0) & (e_sub < N_EXPERTS)
    e_subf = e_sub.astype(F32)
    no_next = float(LANES)
    nxt = jnp.min(jnp.where(nonempty & (e_subf > block_e), e_subf, no_next), axis=0, keepdims=True)
    nxt = jnp.where(nxt == no_next, -1.0, nxt)
    seg_idx = jnp.sum(jnp.where(nonempty & (e_subf < block_e), 1.0, 0.0), axis=0, keepdims=True)
    slot = seg_idx - 2.0 * jnp.floor(seg_idx * 0.5)
    blk_ref[...] = jnp.concatenate(
        [block_e, n_valid, jnp.broadcast_to(n_used, (1, nb_pad)), nxt, slot,
         jnp.zeros((ROUTE_W - 5, nb_pad), F32)], axis=0).astype(I32)

    e_tok = lax.broadcasted_iota(I32, (LANES, tm), 0)

    def tile(t, carry):
        rk = rk_ref[t]
        d = [jnp.sum(jnp.where(e_tok == rk[k:k + 1], start_col, 0.0), axis=0, keepdims=True)
             + rk[k + 2:k + 3].astype(F32) for k in range(TOP_K)]
        dest_ref[t] = jnp.concatenate(d + [jnp.zeros((ROUTE_W - TOP_K, tm), F32)], axis=0).astype(I32)
        return carry

    lax.fori_loop(0, n_tiles, tile, 0)


def _plan_call(rank_rows, cnt):
    n_tiles = TOKENS // TM_MERGE
    nb_pad = -(-N_BLOCKS // LANES) * LANES
    dest, blk = pl.pallas_call(
        _plan_kernel,
        out_shape=(jax.ShapeDtypeStruct((n_tiles, ROUTE_W, TM_MERGE), I32),
                   jax.ShapeDtypeStruct((ROUTE_W, nb_pad), I32)),
        compiler_params=pltpu.CompilerParams(vmem_limit_bytes=VMEM_LIMIT),
        name="plan",
    )(rank_rows, cnt)
    dest1 = dest[:, 0, :].reshape(TOKENS)
    dest2 = dest[:, 1, :].reshape(TOKENS)
    tables = tuple(blk[r, :N_BLOCKS] for r in (0, 1, 3, 4))
    return dest1, dest2, tables, blk[2, :1]


def _sc_params():
    return pltpu.CompilerParams(use_tc_tiling_on_sc=True)


def _sc_mesh():
    return plsc.VectorSubcoreMesh(core_axis_name="core", subcore_axis_name="subcore")


def _sc_worker_base(per_worker):
    wid = lax.axis_index("subcore") * SC_CORES + lax.axis_index("core")
    return wid * per_worker


def _sc_scatter_rows(src, dest1, dest2, n_out):
    n = src.shape[0]
    per_worker = n // SC_WORKERS
    assert per_worker * SC_WORKERS == n and per_worker % SC_CHUNK == 0

    def body(src_hbm, d1_hbm, d2_hbm, out_hbm, idx1_v, idx2_v, rows_v, lsem, ssem1, ssem2):
        base = _sc_worker_base(per_worker)

        @pl.loop(0, per_worker // SC_CHUNK)
        def _(j):
            start = pl.multiple_of(base + j * SC_CHUNK, SC_CHUNK)
            rows_in = pltpu.async_copy(src_hbm.at[pl.ds(start, SC_CHUNK)], rows_v, lsem)
            pltpu.sync_copy(d1_hbm.at[pl.ds(start, SC_CHUNK)], idx1_v)
            pltpu.sync_copy(d2_hbm.at[pl.ds(start, SC_CHUNK)], idx2_v)
            rows_in.wait()
            out1 = pltpu.async_copy(rows_v, out_hbm.at[idx1_v], ssem1)
            out2 = pltpu.async_copy(rows_v, out_hbm.at[idx2_v], ssem2)
            out1.wait()
            out2.wait()

    return pl.kernel(
        body,
        out_type=jax.ShapeDtypeStruct((n_out, ROW_TILE_S, LANES), F32),
        mesh=_sc_mesh(),
        scratch_types=[pltpu.VMEM((SC_CHUNK,), I32), pltpu.VMEM((SC_CHUNK,), I32),
                       pltpu.VMEM((SC_CHUNK, ROW_TILE_S, LANES), F32),
                       pltpu.SemaphoreType.DMA, pltpu.SemaphoreType.DMA, pltpu.SemaphoreType.DMA],
        compiler_params=_sc_params(),
        name="sc_dispatch",
    )(src, dest1, dest2)


def _sc_gather_rows(table, idx):
    n = idx.shape[0]
    per_worker = n // SC_WORKERS
    half = SC_CHUNK // 2
    assert per_worker * SC_WORKERS == n and per_worker % SC_CHUNK == 0

    def body(table_hbm, idx_hbm, out_hbm, idx_v, rows_a, rows_b, gsem_a, gsem_b, wsem_a, wsem_b):
        base = pl.multiple_of(_sc_worker_base(per_worker), SC_CHUNK)
        pltpu.sync_copy(idx_hbm.at[pl.ds(base, per_worker)], idx_v)

        @pl.loop(0, per_worker // SC_CHUNK)
        def _(j):
            off_a = pl.multiple_of(j * SC_CHUNK, SC_CHUNK)
            off_b = pl.multiple_of(j * SC_CHUNK + half, half)
            ga = pltpu.async_copy(table_hbm.at[idx_v.at[pl.ds(off_a, half)]], rows_a, gsem_a)
            gb = pltpu.async_copy(table_hbm.at[idx_v.at[pl.ds(off_b, half)]], rows_b, gsem_b)
            ga.wait()
            wa = pltpu.async_copy(rows_a, out_hbm.at[pl.ds(base + off_a, half)], wsem_a)
            gb.wait()
            wb = pltpu.async_copy(rows_b, out_hbm.at[pl.ds(base + off_b, half)], wsem_b)
            wa.wait()
            wb.wait()

    return pl.kernel(
        body,
        out_type=jax.ShapeDtypeStruct((n, ROW_TILE_S, LANES), F32),
        mesh=_sc_mesh(),
        scratch_types=[pltpu.VMEM((per_worker,), I32),
                       pltpu.VMEM((half, ROW_TILE_S, LANES), F32),
                       pltpu.VMEM((half, ROW_TILE_S, LANES), F32),
                       pltpu.SemaphoreType.DMA, pltpu.SemaphoreType.DMA,
                       pltpu.SemaphoreType.DMA, pltpu.SemaphoreType.DMA],
        compiler_params=_sc_params(),
        name="sc_gather",
    )(table, idx)


def _weight_fetch(w_hbm, stage, sem, e, slot):
    return [pltpu.make_async_copy(w.at[e], st.at[slot], sem.at[slot]) for w, st in zip(w_hbm, stage)]


def _expert_kernel(be_ref, nv_ref, nx_ref, sl_ref, nu_ref, xs_ref, w1_hbm, w3_hbm, w2_hbm, y_ref,
                   w1s, w3s, w2s, w1b, w3b, w2b, sem):
    i = pl.program_id(0)
    e = be_ref[i]
    used = i < nu_ref[0]
    first = (i == 0) | (e != be_ref[jnp.maximum(i - 1, 0)])
    w_hbm = (w1_hbm, w3_hbm, w2_hbm)
    stage = (w1s, w3s, w2s)

    @pl.when(used & first)
    def _():
        slot = sl_ref[i]

        @pl.when(i == 0)
        def _():
            for cp in _weight_fetch(w_hbm, stage, sem, e, slot):
                cp.start()

        for cp in _weight_fetch(w_hbm, stage, sem, e, slot):
            cp.wait()
        w1b[...] = w1s[slot].astype(BF16)
        w3b[...] = w3s[slot].astype(BF16)
        w2b[...] = w2s[slot].astype(BF16)

        @pl.when(nx_ref[i] >= 0)
        def _():
            for cp in _weight_fetch(w_hbm, stage, sem, nx_ref[i], 1 - slot):
                cp.start()

    n_parts = MOE_BLK // MOE_SUB

    def load_x(part):
        row0 = part * MOE_SUB
        row = lax.broadcasted_iota(I32, (MOE_SUB, D_MODEL), 0) + row0
        return jnp.where(row < nv_ref[i], _load_row_tiles(xs_ref, row0, MOE_SUB), 0.0).astype(BF16)

    def zero_y(part):
        y_ref[pl.ds(part * MOE_SUB * ROW_TILE_S, MOE_SUB * ROW_TILE_S), :] = jnp.zeros(
            (MOE_SUB * ROW_TILE_S, LANES), F32)

    def run(parts):
        hmid = []
        for part in parts:
            xb = load_x(part)
            a = jnp.dot(xb, w1b[...], preferred_element_type=F32)
            g = jnp.dot(xb, w3b[...], preferred_element_type=F32)
            hmid.append(((a * _sigmoid(a)) * g).astype(BF16))
        for part, hm in zip(parts, hmid):
            _store_row_tiles(y_ref, jnp.dot(hm, w2b[...], preferred_element_type=F32),
                             part * MOE_SUB)

    n_live = jnp.where(used, (nv_ref[i] + (MOE_SUB - 1)) // MOE_SUB, 0)
    for k in range(n_parts + 1):
        @pl.when(n_live == k)
        def _(k=k):
            if k:
                run(list(range(k)))
            for part in range(k, n_parts):
                zero_y(part)


def _expert_call(tables, n_used, xs, w1, w3, w2):
    n_tab = len(tables)
    grid_spec = pltpu.PrefetchScalarGridSpec(
        num_scalar_prefetch=n_tab + 1,
        grid=(N_BLOCKS,),
        in_specs=[_row_tile_spec(MOE_BLK, lambda i, *pf: (jnp.minimum(i, pf[n_tab][0] - 1), 0)),
                  pl.BlockSpec(memory_space=pl.ANY),
                  pl.BlockSpec(memory_space=pl.ANY),
                  pl.BlockSpec(memory_space=pl.ANY)],
        out_specs=_row_tile_spec(MOE_BLK, lambda i, *pf: (i, 0)),
        scratch_shapes=[pltpu.VMEM((2, D_MODEL, D_EXPERT), F32),
                        pltpu.VMEM((2, D_MODEL, D_EXPERT), F32),
                        pltpu.VMEM((2, D_EXPERT, D_MODEL), F32),
                        pltpu.VMEM((D_MODEL, D_EXPERT), BF16),
                        pltpu.VMEM((D_MODEL, D_EXPERT), BF16),
                        pltpu.VMEM((D_EXPERT, D_MODEL), BF16),
                        pltpu.SemaphoreType.DMA((2,))],
    )
    return pl.pallas_call(
        _expert_kernel,
        grid_spec=grid_spec,
        out_shape=jax.ShapeDtypeStruct((N_BLOCKS * MOE_BLK * ROW_TILE_S, LANES), F32),
        compiler_params=_cparams(("arbitrary",)),
        name="experts",
    )(*tables, n_used, xs, w1, w3, w2)


def _combine_rows_kernel(x1_ref, rf_ref, mod_ref, ya_ref, yb_ref, *rest):
    o_ref = rest[-1]
    rf = rf_ref[...]
    gate2 = mod_ref[0, 5:6, :]
    moe = rf[:, 0:1] * _load_row_tiles(ya_ref) + rf[:, 1:2] * _load_row_tiles(yb_ref)
    o_ref[...] = x1_ref[...] + gate2 * moe


def _combine_rows_call(x1, rf, mod, yg, part, prev=None):
    nb = SEQ // TM_ROW
    n_half = TOKENS // TM_ROW // 2
    t0 = part * n_half
    in_specs = [pl.BlockSpec((TM_ROW, D_MODEL), lambda i: (t0 + i, 0)),
                pl.BlockSpec((TM_ROW, ROUTE_W), lambda i: (t0 + i, 0)),
                pl.BlockSpec((1, N_MOD, D_MODEL), lambda i: ((t0 + i) // nb, 0, 0)),
                _row_tile_spec(TM_ROW, lambda i: (i, 0)),
                _row_tile_spec(TM_ROW, lambda i: (i + n_half, 0))]
    args = [x1, rf, mod, yg, yg]
    aliases = {}
    if prev is not None:
        in_specs.append(pl.BlockSpec(memory_space=pl.ANY))
        args.append(prev)
        aliases = {len(args) - 1: 0}
    return pl.pallas_call(
        _combine_rows_kernel,
        grid=(n_half,),
        in_specs=in_specs,
        out_specs=pl.BlockSpec((TM_ROW, D_MODEL), lambda i: (t0 + i, 0)),
        out_shape=jax.ShapeDtypeStruct((TOKENS, D_MODEL), F32),
        input_output_aliases=aliases,
        compiler_params=_cparams(("parallel",)),
        name="combine",
    )(*args)


def kernel(x, c, positions, ada_w, ada_b, norm1_g, norm2_g, w_in, gla_alpha_up, gla_alpha_b,
           gla_out_g, diff_q_g, diff_k_g, diff_lq1, diff_lk1, diff_lq2, diff_lk2, diff_out_g,
           w_branch_a, w_branch_b, w_out, router_group_w, router_group_b, router_expert_w,
           router_expert_b, expert_w1, expert_w3, expert_w2):
    assert x.shape == (BATCH, SEQ, D_MODEL) and ada_w.shape[0] == 1
    x2 = x.reshape(TOKENS, D_MODEL)

    w_in_p = _wprep_call(w_in[0])
    au_hi = gla_alpha_up[0].astype(BF16)
    au_lo = (gla_alpha_up[0] - au_hi.astype(F32)).astype(BF16)
    au_stack = jnp.concatenate(
        [au_hi, au_hi, au_lo, jnp.zeros((LR_PAD - 3 * GLA_GATE_RANK, GLA_QK_W), BF16)], axis=0)
    gid = jnp.arange(NORM_BLK) // DIFF_DH
    bd = jnp.where(gid[:, None] == gid[None, :], 1.0 / DIFF_DH, 0.0).astype(BF16)
    qg_row = jnp.tile(diff_q_g[0], DIFF_QK_W // DIFF_DH).reshape(1, DIFF_QK_W)
    kg_row = jnp.tile(diff_k_g[0], DIFF_QK_W // DIFF_DH).reshape(1, DIFF_QK_W)
    inv_freq = ROPE_THETA ** (-jnp.arange(ROPE_HALF, dtype=F32) / ROPE_HALF)
    invf64 = jnp.concatenate([inv_freq, inv_freq, jnp.zeros((DIFF_DH - ROPE_DIM,), F32)])
    invf_row = jnp.tile(invf64, LANES // DIFF_DH).reshape(1, LANES)
    pos_col = positions.reshape(TOKENS, 1)
    lam_p = jnp.concatenate([diff_lq1, diff_lk1, diff_lq2, diff_lk2], axis=0)
    wr = (jnp.zeros((D_MODEL, LANES), F32)
          .at[:, :N_GROUPS].set(router_group_w[0])
          .at[:, N_GROUPS:N_GROUPS + N_EXPERTS].set(router_expert_w[0]))
    wr_hi = wr.astype(BF16)
    wr2 = jnp.concatenate([wr_hi, (wr - wr_hi.astype(F32)).astype(BF16)], axis=1)
    br = (jnp.zeros((1, LANES), F32)
          .at[0, :N_GROUPS].set(router_group_b[0])
          .at[0, N_GROUPS:N_GROUPS + N_EXPERTS].set(router_expert_b[0]))

    mod = _mod_call(c, ada_w[0], ada_b[0])
    gqk, gv, gr, lr, qT, kk, vT, sa, sb = _in_call(
        x2, mod, norm1_g, w_in_p, bd, qg_row, kg_row, pos_col, invf_row)
    o_a = _gla_call(gqk, gv, gr, lr, au_stack, gla_alpha_b, gla_out_g)
    o_b = _attn_call(qT, kk, vT, lam_p, diff_out_g.reshape(DIFF_DV, 1))
    x1, h2, rf, rank, cnt = _merge_call(
        x2, o_a, o_b, sa, sb, mod, norm2_g, w_branch_a[0].astype(BF16),
        w_branch_b[0].astype(BF16), w_out[0].astype(BF16), wr2, br)
    dest1, dest2, tables, n_used = _plan_call(rank, cnt)

    xs = _sc_scatter_rows(_as_row_tiles(h2), dest1, dest2, N_BLOCKS * MOE_BLK)
    y = _expert_call(tables, n_used, _as_2d(xs), expert_w1[0], expert_w3[0], expert_w2[0])
    y3 = _as_row_tiles(y)
    half = TOKENS // 2
    out = None
    for part in range(2):
        tok = slice(part * half, (part + 1) * half)
        yg = _sc_gather_rows(y3, jnp.concatenate([dest1[tok], dest2[tok]]))
        out = _combine_rows_call(x1, rf, mod, _as_2d(yg), part, out)
    return out.reshape(BATCH, SEQ, D_MODEL)
```

```python
import math

import jax
import jax.numpy as jnp
from jax import lax
from jax.experimental import pallas as pl
from jax.experimental.pallas import tpu as pltpu
from jax.experimental.pallas import tpu_sc as plsc

F32 = jnp.float32
BF16 = jnp.bfloat16
I32 = jnp.int32

D_MODEL = 1024
BATCH = 4
SEQ = 4096
TOKENS = BATCH * SEQ
N_MOD = 6
NORM_EPS = 1e-6

GLA_HEADS = 4
GLA_DK = 64
GLA_DV = 128
GLA_GATE_RANK = 16
GLA_GATE_TAU = 16.0
GLA_CHUNK = 64
GLA_QK_W = GLA_HEADS * GLA_DK
GLA_V_W = GLA_HEADS * GLA_DV

DIFF_HEADS = 4
DIFF_DH = 64
DIFF_DV = 2 * DIFF_DH
DIFF_QK_W = DIFF_HEADS * 2 * DIFF_DH
DIFF_V_W = DIFF_HEADS * DIFF_DV
ROPE_THETA = 500000.0
ROPE_DIM = DIFF_DH // 4
ROPE_HALF = ROPE_DIM // 2
NEG_INF = -1e30
LAMBDA_INIT = 0.8 - 0.6 * 1.0

N_GROUPS = 4
EXPERTS_PER_GROUP = 8
N_EXPERTS = N_GROUPS * EXPERTS_PER_GROUP
TOP_K = 2
D_EXPERT = 512

LANES = 128
SUBLANES = 8
ROW_TILE_S = D_MODEL // LANES
SC_CORES = 2
SC_SUBCORES = 16
SC_WORKERS = SC_CORES * SC_SUBCORES
SC_CHUNK = 64
LR_PAD = LANES
D_IN_PAD = 2 * GLA_QK_W + 2 * GLA_V_W + 2 * DIFF_QK_W + DIFF_V_W + 2 * D_MODEL + LR_PAD
IN_COLS = {}
_col = 0
for _name, _width in (("gla_qk", 2 * GLA_QK_W), ("gla_v", GLA_V_W), ("gla_r", GLA_V_W),
                      ("diff_q", DIFF_QK_W), ("diff_k", DIFF_QK_W), ("diff_v", DIFF_V_W),
                      ("gate_a", D_MODEL), ("gate_b", D_MODEL), ("gla_lr", LR_PAD)):
    IN_COLS[_name] = (_col, _col + _width)
    _col += _width
assert _col == D_IN_PAD
WPREP_COLS = 512
WPREP_LR_STEP = (D_IN_PAD - LR_PAD) // WPREP_COLS
D_IN_ALLOC = (WPREP_LR_STEP + 1) * WPREP_COLS

TM_IN = 512
TQ = 512
V_ROWS = DIFF_DV + 16
NORM_BLK = 256
ATT_SUB = 32
Q_SCALE = DIFF_DH ** -0.5 * math.log2(math.e)
N_KV = SEQ // TQ
TT_GLA = 512
TM_MERGE = 512
TM_ROW = 512
MOE_BLK = 512
MOE_SUB = 256
N_BLOCKS = (TOKENS * TOP_K + N_EXPERTS * (MOE_BLK - 1) + MOE_BLK - 1) // MOE_BLK
ROUTE_W = 8

VMEM_LIMIT = 56 * 1024 * 1024


def _cparams(sem):
    return pltpu.CompilerParams(dimension_semantics=sem, vmem_limit_bytes=VMEM_LIMIT)


def _sigmoid(x):
    return 1.0 / (1.0 + jnp.exp(-x))


def _row_tile_spec(rows, index_map):
    return pl.BlockSpec((rows * ROW_TILE_S, LANES), index_map)


def _as_row_tiles(a2d):
    return a2d.reshape(a2d.shape[0] // ROW_TILE_S, ROW_TILE_S, LANES)


def _as_2d(a3d):
    return a3d.reshape(a3d.shape[0] * ROW_TILE_S, LANES)


def _store_row_tiles(ref, val, row0=0):
    for s in range(ROW_TILE_S):
        ref[pl.ds(row0 * ROW_TILE_S + s, val.shape[0], stride=ROW_TILE_S), :] = (
            val[:, s * LANES:(s + 1) * LANES])


def _load_row_tiles(ref, row0=0, rows=None):
    rows = ref.shape[0] // ROW_TILE_S if rows is None else rows
    return jnp.concatenate(
        [ref[pl.ds(row0 * ROW_TILE_S + s, rows, stride=ROW_TILE_S), :] for s in range(ROW_TILE_S)],
        axis=1)


def _mod_kernel(ct_ref, w_ref, b_ref, o_ref):
    ct = ct_ref[...]
    ca = ct * _sigmoid(ct)
    w = w_ref[...]
    rows = []
    for b in range(BATCH):
        rows.append(jnp.sum(w * ca[:, b:b + 1], axis=0, keepdims=True) + b_ref[...])
    rows.append(jnp.zeros((8 - BATCH, w.shape[1]), F32))
    o_ref[...] = jnp.concatenate(rows, axis=0)


def _mod_call(c, ada_w, ada_b):
    tn = D_MODEL
    ct = jnp.zeros((D_MODEL, 8), F32).at[:, :BATCH].set(c.T)
    out = pl.pallas_call(
        _mod_kernel,
        grid=(N_MOD,),
        in_specs=[
            pl.BlockSpec((D_MODEL, 8), lambda j: (0, 0)),
            pl.BlockSpec((D_MODEL, tn), lambda j: (0, j)),
            pl.BlockSpec((1, tn), lambda j: (0, j)),
        ],
        out_specs=pl.BlockSpec((8, tn), lambda j: (0, j)),
        out_shape=jax.ShapeDtypeStruct((8, N_MOD * D_MODEL), F32),
        compiler_params=_cparams(("arbitrary",)),
        name="mod",
    )(ct, ada_w, ada_b.reshape(1, N_MOD * D_MODEL))
    return out[:BATCH].reshape(BATCH, N_MOD, D_MODEL)


def _wprep_kernel(w_ref, o_ref):
    j = pl.program_id(0)
    t = w_ref[...].T
    lane = lax.broadcasted_iota(I32, t.shape, 1)
    @pl.when(j < WPREP_LR_STEP)
    def _():
        o_ref[...] = t.astype(BF16)

    @pl.when(j == WPREP_LR_STEP)
    def _():
        rep = jnp.where(lane < GLA_GATE_RANK, t,
                        jnp.where(lane < 2 * GLA_GATE_RANK, pltpu.roll(t, GLA_GATE_RANK, 1),
                                  jnp.where(lane < 3 * GLA_GATE_RANK,
                                            pltpu.roll(t, 2 * GLA_GATE_RANK, 1), 0.0)))
        o_ref[...] = rep.astype(BF16)


def _wprep_call(w_in0):
    w_t = w_in0.T
    lr0 = 2 * GLA_QK_W + 2 * GLA_V_W
    assert lr0 % WPREP_COLS == 0 and (D_IN_PAD - LR_PAD) % WPREP_COLS == 0

    def src_row(j):
        return jnp.where(j < lr0 // WPREP_COLS, j * WPREP_COLS,
                         jnp.where(j < WPREP_LR_STEP, j * WPREP_COLS + GLA_GATE_RANK, lr0))

    return pl.pallas_call(
        _wprep_kernel,
        grid=(WPREP_LR_STEP + 1,),
        in_specs=[pl.BlockSpec((pl.Element(WPREP_COLS), pl.Element(D_MODEL)),
                               lambda j: (pl.multiple_of(src_row(j), SUBLANES), 0))],
        out_specs=pl.BlockSpec((D_MODEL, WPREP_COLS), lambda j: (0, j)),
        out_shape=jax.ShapeDtypeStruct((D_MODEL, D_IN_ALLOC), BF16),
        compiler_params=_cparams(("parallel",)),
        name="w_prep",
    )(w_t)


def _in_kernel(x_ref, mod_ref, g1_ref, w_ref, bd_ref, qg_ref, kg_ref, pos_ref, invf_ref,
               gqk_ref, gv_ref, gr_ref, lr_ref, qT_ref, k_ref, vT_ref, sa_ref, sb_ref,
               cos_scr, sin_scr):
    ang = pos_ref[...].astype(F32) * invf_ref[...]
    cos_scr[...] = jnp.cos(ang)
    sin_scr[...] = jnp.sin(ang)

    x = x_ref[...]
    shift1 = mod_ref[0, 0:1, :]
    scale1 = mod_ref[0, 1:2, :]
    ms = jnp.mean(x * x, axis=-1, keepdims=True)
    h = (x * lax.rsqrt(ms + NORM_EPS) * g1_ref[...]) * (1.0 + scale1) + shift1
    hb = h.astype(BF16)

    def proj(name):
        c0, c1 = IN_COLS[name]
        return jnp.dot(hb, w_ref[:, c0:c1], preferred_element_type=F32)

    tm = x.shape[0]
    cos4 = jnp.concatenate([cos_scr[...]] * DIFF_HEADS, axis=1)
    sin4 = jnp.concatenate([sin_scr[...]] * DIFF_HEADS, axis=1)
    lane = lax.broadcasted_iota(I32, (tm, DIFF_QK_W), 1)
    first_half = (lane % DIFF_DH) < ROPE_HALF
    bd = bd_ref[...]

    def norm_rope(t, gain_row):
        t2 = t * t
        hi = t2.astype(BF16)
        lo = (t2 - hi.astype(F32)).astype(BF16)
        hw = bd.shape[0]
        gms = jnp.concatenate(
            [jnp.dot(hi[:, c0:c0 + hw], bd, preferred_element_type=F32)
             + jnp.dot(lo[:, c0:c0 + hw], bd, preferred_element_type=F32)
             for c0 in range(0, DIFF_QK_W, hw)], axis=1)
        t = t * lax.rsqrt(gms + NORM_EPS) * gain_row
        nxt = pltpu.roll(t, DIFF_QK_W - ROPE_HALF, 1)
        prv = pltpu.roll(t, ROPE_HALF, 1)
        return t * cos4 + jnp.where(first_half, -nxt, prv) * sin4

    dq_raw = proj("diff_q")
    dk_raw = proj("diff_k")
    gqk_ref[...] = proj("gla_qk").astype(BF16)
    dq = norm_rope(dq_raw, qg_ref[...]) * Q_SCALE
    gv_ref[...] = proj("gla_v").astype(BF16)
    dk = norm_rope(dk_raw, kg_ref[...])
    dv = proj("diff_v")
    for hd in range(DIFF_HEADS):
        sl = slice(hd * LANES, (hd + 1) * LANES)
        qT_ref[0, hd, 0] = dq[:, sl].T.astype(BF16)
        k_ref[0, hd, 0] = dk[:, sl].astype(BF16)
    gr_ref[...] = proj("gla_r").astype(BF16)
    for hd in range(DIFF_HEADS):
        sl = slice(hd * LANES, (hd + 1) * LANES)
        vT_ref[0, hd, 0] = jnp.concatenate(
            [dv[:, sl].T, jnp.ones((V_ROWS - DIFF_DV, tm), F32)], axis=0).astype(BF16)
    sa_ref[...] = _sigmoid(proj("gate_a")).astype(BF16)
    sb_ref[...] = _sigmoid(proj("gate_b")).astype(BF16)
    lr_ref[...] = proj("gla_lr")


def _in_call(x2, mod, norm1_g, w_in_p, bd, qg_row, kg_row, pos_col, invf_row):
    nb = SEQ // TM_IN
    tok_spec = lambda w: pl.BlockSpec((TM_IN, w), lambda i: (i, 0))
    const2 = lambda r, c: pl.BlockSpec((r, c), lambda i: (0, 0))
    out_shapes = (
        jax.ShapeDtypeStruct((TOKENS, 2 * GLA_QK_W), BF16),
        jax.ShapeDtypeStruct((TOKENS, GLA_V_W), BF16),
        jax.ShapeDtypeStruct((TOKENS, GLA_V_W), BF16),
        jax.ShapeDtypeStruct((TOKENS, LR_PAD), F32),
        jax.ShapeDtypeStruct((BATCH, DIFF_HEADS, N_KV, LANES, TQ), BF16),
        jax.ShapeDtypeStruct((BATCH, DIFF_HEADS, N_KV, TQ, LANES), BF16),
        jax.ShapeDtypeStruct((BATCH, DIFF_HEADS, N_KV, V_ROWS, TQ), BF16),
        jax.ShapeDtypeStruct((TOKENS, D_MODEL), BF16),
        jax.ShapeDtypeStruct((TOKENS, D_MODEL), BF16),
    )
    out_specs = (
        tok_spec(2 * GLA_QK_W), tok_spec(GLA_V_W), tok_spec(GLA_V_W), tok_spec(LR_PAD),
        pl.BlockSpec((1, DIFF_HEADS, 1, LANES, TM_IN), lambda i: (i // nb, 0, i % nb, 0, 0)),
        pl.BlockSpec((1, DIFF_HEADS, 1, TM_IN, LANES), lambda i: (i // nb, 0, i % nb, 0, 0)),
        pl.BlockSpec((1, DIFF_HEADS, 1, V_ROWS, TM_IN), lambda i: (i // nb, 0, i % nb, 0, 0)),
        tok_spec(D_MODEL), tok_spec(D_MODEL),
    )
    return pl.pallas_call(
        _in_kernel,
        grid=(TOKENS // TM_IN,),
        in_specs=[
            tok_spec(D_MODEL),
            pl.BlockSpec((1, N_MOD, D_MODEL), lambda i: (i // nb, 0, 0)),
            const2(1, D_MODEL),
            pl.BlockSpec((D_MODEL, D_IN_ALLOC), lambda i: (0, 0), pipeline_mode=pl.Buffered(1)),
            const2(NORM_BLK, NORM_BLK),
            const2(1, DIFF_QK_W), const2(1, DIFF_QK_W),
            tok_spec(1),
            const2(1, LANES),
        ],
        out_specs=out_specs,
        out_shape=out_shapes,
        scratch_shapes=[pltpu.VMEM((TM_IN, LANES), F32), pltpu.VMEM((TM_IN, LANES), F32)],
        compiler_params=_cparams(("parallel",)),
        name="in_proj",
    )(x2, mod, norm1_g, w_in_p, bd, qg_row, kg_row, pos_col, invf_row)


def _gla_kernel(qk_ref, v_ref, r_ref, lr_ref, au_ref, ab_ref, og_ref, o_ref,
                state_ref, oacc_ref, snew_ref):
    tt = qk_ref.shape[0]
    n_chunks = tt // GLA_CHUNK

    @pl.when(pl.program_id(1) == 0)
    def _():
        state_ref[...] = jnp.zeros_like(state_ref)

    lr = lr_ref[...]
    lr_hi = lr.astype(BF16)
    lr_lo = (lr - lr_hi.astype(F32)).astype(BF16)
    lr_lane = lax.broadcasted_iota(I32, lr.shape, 1)
    in_lo_group = (lr_lane >= GLA_GATE_RANK) & (lr_lane < 2 * GLA_GATE_RANK)
    stacked = jnp.where(in_lo_group, lr_lo, lr_hi)
    z = jnp.dot(stacked, au_ref[...], preferred_element_type=F32) + ab_ref[...]
    g = (jnp.minimum(z, 0.0) - jnp.log(1.0 + jnp.exp(-jnp.abs(z)))) * (1.0 / GLA_GATE_TAU)

    row = lax.broadcasted_iota(I32, (tt, GLA_QK_W), 0) % GLA_CHUNK
    b = g
    step = 1
    while step < GLA_CHUNK:
        b = b + jnp.where(row >= step, pltpu.roll(b, step, 0), 0.0)
        step *= 2

    b_last_rows = [b[c * GLA_CHUNK + GLA_CHUNK - 1:(c + 1) * GLA_CHUNK, :] for c in range(n_chunks)]
    b_last = jnp.concatenate(
        [jnp.broadcast_to(bl, (GLA_CHUNK, GLA_QK_W)) for bl in b_last_rows], axis=0)

    qk = qk_ref[...].astype(F32)
    q = qk[:, :GLA_QK_W] * (GLA_DK ** -0.5)
    k = qk[:, GLA_QK_W:]
    q_in = (q * jnp.exp(b)).astype(BF16)
    k_in = (k * jnp.exp(-b)).astype(BF16)
    k_dec = (k * jnp.exp(b_last - b)).astype(BF16)

    ci = lax.broadcasted_iota(I32, (GLA_CHUNK, GLA_CHUNK), 0)
    cj = lax.broadcasted_iota(I32, (GLA_CHUNK, GLA_CHUNK), 1)
    causal = ci >= cj

    pairs = [(c, hd) for c in range(n_chunks) for hd in range(GLA_HEADS)]

    def rows(c):
        return slice(c * GLA_CHUNK, (c + 1) * GLA_CHUNK)

    def kcols(hd):
        return slice(hd * GLA_DK, (hd + 1) * GLA_DK)

    def vcols(hd):
        return slice(hd * GLA_DV, (hd + 1) * GLA_DV)

    att = {}
    for c, hd in pairs:
        a = lax.dot_general(q_in[rows(c), kcols(hd)], k_in[rows(c), kcols(hd)],
                            (((1,), (1,)), ((), ())), preferred_element_type=F32)
        att[c, hd] = jnp.where(causal, a, 0.0).astype(BF16)
    for c, hd in pairs:
        oacc_ref[rows(c), vcols(hd)] = jnp.dot(att[c, hd], v_ref[rows(c), vcols(hd)],
                                               preferred_element_type=F32)
    for c, hd in pairs:
        snew_ref[c * GLA_HEADS + hd] = lax.dot_general(
            k_dec[rows(c), kcols(hd)], v_ref[rows(c), vcols(hd)],
            (((0,), (0,)), ((), ())), preferred_element_type=F32)

    decay_rows = jnp.exp(jnp.concatenate(
        b_last_rows + [jnp.zeros((LANES - n_chunks, GLA_QK_W), F32)], axis=0))
    decay_cols = decay_rows.T
    states = [state_ref[hd] for hd in range(GLA_HEADS)]
    for c in range(n_chunks):
        for hd in range(GLA_HEADS):
            s_prev = states[hd]
            oacc_ref[rows(c), vcols(hd)] += jnp.dot(
                q_in[rows(c), kcols(hd)], s_prev.astype(BF16), preferred_element_type=F32)
            dcol = decay_cols[kcols(hd), c:c + 1]
            states[hd] = s_prev * dcol + snew_ref[c * GLA_HEADS + hd]
    for hd in range(GLA_HEADS):
        state_ref[hd] = states[hd]

    for hd in range(GLA_HEADS):
        vs = slice(hd * GLA_DV, (hd + 1) * GLA_DV)
        oh = oacc_ref[:, vs]
        ms = jnp.mean(oh * oh, axis=-1, keepdims=True)
        y = oh * lax.rsqrt(ms + NORM_EPS) * og_ref[...]
        r = r_ref[:, vs].astype(F32)
        o_ref[:, vs] = (y * (r * _sigmoid(r))).astype(BF16)


def _gla_call(gqk, gv, gr, lr, au_stack, ab_row, og_row):
    nt = SEQ // TT_GLA
    tok = lambda w: pl.BlockSpec((TT_GLA, w), lambda b, t: (b * nt + t, 0))
    const2 = lambda r, c: pl.BlockSpec((r, c), lambda b, t: (0, 0))
    return pl.pallas_call(
        _gla_kernel,
        grid=(BATCH, nt),
        in_specs=[tok(2 * GLA_QK_W), tok(GLA_V_W), tok(GLA_V_W), tok(LR_PAD),
                  const2(LR_PAD, GLA_QK_W), const2(1, GLA_QK_W), const2(1, GLA_DV)],
        out_specs=tok(GLA_V_W),
        out_shape=jax.ShapeDtypeStruct((TOKENS, GLA_V_W), BF16),
        scratch_shapes=[pltpu.VMEM((GLA_HEADS, GLA_DK, GLA_DV), F32),
                        pltpu.VMEM((TT_GLA, GLA_V_W), F32),
                        pltpu.VMEM((TT_GLA // GLA_CHUNK * GLA_HEADS, GLA_DK, GLA_DV), F32)],
        compiler_params=_cparams(("parallel", "arbitrary")),
        name="gla",
    )(gqk, gv, gr, lr, au_stack, ab_row, og_row)


def _attn_kernel(qT_ref, k_ref, vT_ref, lam_ref, og_ref, o_ref,
                 q_scr, s_scr, p_scr, acc_scr, m_scr):
    _attn_load_q(qT_ref, q_scr, 0, 0)
    for c in range(2):
        s_scr[c] = jnp.dot(k_ref[0, 0, 0], q_scr[0, c], preferred_element_type=F32)

    def q_block(i, carry):
        _attn_q_block(i, qT_ref, k_ref, vT_ref, lam_ref, og_ref, o_ref,
                      q_scr, s_scr, p_scr, acc_scr, m_scr)
        return carry

    lax.fori_loop(0, N_KV, q_block, 0)


def _attn_load_q(qT_ref, q_scr, i, slot):
    qT = qT_ref[0, 0, i]
    rowq = lax.broadcasted_iota(I32, qT.shape, 0)
    zero = jnp.zeros_like(qT)
    q_scr[slot, 0] = jnp.where(rowq < DIFF_DH, qT, zero)
    q_scr[slot, 1] = jnp.where(rowq >= DIFF_DH, qT, zero)


def _attn_q_block(i, qT_ref, k_ref, vT_ref, lam_ref, og_ref, o_ref,
                  q_scr, s_scr, p_scr, acc_scr, m_scr):
    slot = i % 2
    n_sub = TQ // ATT_SUB

    def fold8(t, op):
        return op(t.reshape(t.shape[0] // SUBLANES, SUBLANES, TQ), axis=0)

    def scores(c, j):
        s_scr[c] = jnp.dot(k_ref[0, 0, j], q_scr[slot, c], preferred_element_type=F32)

    def load_s(c, r, mask, j=None):
        s = s_scr[c, r * ATT_SUB:(r + 1) * ATT_SUB, :]
        if mask is not None:
            key_i = lax.broadcasted_iota(I32, (ATT_SUB, TQ), 0) + r * ATT_SUB
            qry_i = lax.broadcasted_iota(I32, (ATT_SUB, TQ), 1)
            keep = key_i <= qry_i
            if mask == "maybe":
                keep = jnp.logical_or(keep, j < i)
            s = jnp.where(keep, s, NEG_INF)
        return s

    def block_max(c, mask, j=None):
        m8 = fold8(load_s(c, 0, mask, j), jnp.max)
        for r in range(1, n_sub):
            m8 = jnp.maximum(m8, fold8(load_s(c, r, mask, j), jnp.max))
        return jnp.max(m8, axis=0, keepdims=True)

    def exp_to_p(c, shift, mask):
        for r in range(n_sub):
            p = jnp.exp2(load_s(c, r, mask) - shift)
            p_scr[c, r * ATT_SUB:(r + 1) * ATT_SUB, :] = p.astype(BF16)

    def p_times_v(c, j):
        return jnp.dot(vT_ref[0, 0, j], p_scr[c], preferred_element_type=F32)

    acc_scr[...] = jnp.zeros(acc_scr.shape, F32)

    def fast_pv(c, j, mask):
        exp_to_p(c, m_scr[c], mask)
        acc_scr[c] = acc_scr[c] + p_times_v(c, j)

    def fast_body(j, carry):
        scores(1, j)
        fast_pv(0, j, None)
        scores(0, j + 1)
        fast_pv(1, j, None)
        return carry

    m_scr[0] = block_max(0, "maybe", 0)
    m_scr[1] = block_max(1, "maybe", 0)

    @pl.when(i > 0)
    def _():
        fast_pv(0, 0, None)
        scores(0, 1)
        fast_pv(1, 0, None)

    lax.fori_loop(1, i, fast_body, 0)
    scores(1, i)
    fast_pv(0, i, "diag")
    fast_pv(1, i, "diag")

    total = jnp.sum(jnp.abs(acc_scr[0])) + jnp.sum(jnp.abs(acc_scr[1]))
    overflowed = jnp.logical_not(total < jnp.inf)

    @pl.when(overflowed)
    def _():
        m_scr[...] = jnp.full(m_scr.shape, NEG_INF, F32)
        acc_scr[...] = jnp.zeros(acc_scr.shape, F32)

        def safe_pv(c, j, mask):
            m_old = m_scr[c]
            m_new = jnp.maximum(m_old, block_max(c, mask))
            alpha = jnp.exp2(m_old - m_new)
            exp_to_p(c, m_new, mask)
            m_scr[c] = m_new
            acc_scr[c] = acc_scr[c] * alpha + p_times_v(c, j)

        def safe_body(j, carry):
            scores(0, j)
            scores(1, j)
            safe_pv(0, j, None)
            safe_pv(1, j, None)
            return carry

        lax.fori_loop(0, i, safe_body, 0)
        scores(0, i)
        scores(1, i)
        safe_pv(0, i, "diag")
        safe_pv(1, i, "diag")

    nxt = jnp.minimum(i + 1, N_KV - 1)
    _attn_load_q(qT_ref, q_scr, nxt, 1 - slot)
    for c in range(2):
        s_scr[c] = jnp.dot(k_ref[0, 0, 0], q_scr[1 - slot, c], preferred_element_type=F32)

    l1 = acc_scr[0, DIFF_DV:DIFF_DV + 1, :]
    l2 = acc_scr[1, DIFF_DV:DIFF_DV + 1, :]

    lam_p = lam_ref[...]
    lam = (jnp.exp(jnp.sum(lam_p[0:1] * lam_p[1:2], axis=1, keepdims=True))
           - jnp.exp(jnp.sum(lam_p[2:3] * lam_p[3:4], axis=1, keepdims=True)) + LAMBDA_INIT)
    oT = acc_scr[0, :DIFF_DV, :] / l1 - lam * (acc_scr[1, :DIFF_DV, :] / l2)
    ms = jnp.mean(oT * oT, axis=0, keepdims=True)
    y = oT * lax.rsqrt(ms + NORM_EPS) * og_ref[...] * (1.0 - LAMBDA_INIT)
    o_ref[pl.ds(pl.multiple_of(i * TQ, TQ), TQ), :] = y.T.astype(BF16)


def _attn_call(qT, kk, vT, lam_p, og_col):
    return pl.pallas_call(
        _attn_kernel,
        grid=(BATCH, DIFF_HEADS),
        in_specs=[
            pl.BlockSpec((1, 1, N_KV, LANES, TQ), lambda b, h: (b, h, 0, 0, 0)),
            pl.BlockSpec((1, 1, N_KV, TQ, LANES), lambda b, h: (b, h, 0, 0, 0)),
            pl.BlockSpec((1, 1, N_KV, V_ROWS, TQ), lambda b, h: (b, h, 0, 0, 0)),
            pl.BlockSpec((4, DIFF_DH), lambda b, h: (0, 0)),
            pl.BlockSpec((DIFF_DV, 1), lambda b, h: (0, 0)),
        ],
        out_specs=pl.BlockSpec((SEQ, DIFF_DV), lambda b, h: (b, h)),
        out_shape=jax.ShapeDtypeStruct((TOKENS, DIFF_V_W), BF16),
        scratch_shapes=[pltpu.VMEM((2, 2, LANES, TQ), BF16),
                        pltpu.VMEM((2, TQ, TQ), F32),
                        pltpu.VMEM((2, TQ, TQ), BF16),
                        pltpu.VMEM((2, V_ROWS, TQ), F32),
                        pltpu.VMEM((2, 1, TQ), F32)],
        compiler_params=_cparams(("parallel", "parallel")),
        name="attn",
    )(qT, kk, vT, lam_p, og_col)


def _merge_kernel(x_ref, oa_ref, ob_ref, sa_ref, sb_ref, mod_ref, g2_ref, wa_ref, wb_ref, wo_ref,
                  wr2_ref, br_ref, x1_ref, h2_ref, rf_ref, rank_ref, cnt_ref,
                  carry_ref, tri_ref, lg_ref):
    tm = x_ref.shape[0]
    step = pl.program_id(0)

    @pl.when(step == 0)
    def _():
        carry_ref[...] = jnp.zeros_like(carry_ref)
        lg_ref[...] = jnp.zeros_like(lg_ref)
        ti = lax.broadcasted_iota(I32, (tm, tm), 0)
        tj = lax.broadcasted_iota(I32, (tm, tm), 1)
        tri_ref[...] = jnp.where(ti > tj, 1.0, 0.0).astype(BF16)

    ma = jnp.dot(oa_ref[...], wa_ref[...], preferred_element_type=F32)
    mb = jnp.dot(ob_ref[...], wb_ref[...], preferred_element_type=F32)
    merged = sa_ref[...].astype(F32) * ma + sb_ref[...].astype(F32) * mb
    y = jnp.dot(merged.astype(BF16), wo_ref[...], preferred_element_type=F32)
    logits = lg_ref[...]
    lane = lax.broadcasted_iota(I32, (tm, LANES), 1).astype(F32)
    ninf = -jnp.inf
    big = float(LANES)

    def first_argmax(v):
        vmax = jnp.max(v, axis=1, keepdims=True)
        idx = jnp.min(jnp.where(v == vmax, lane, big), axis=1, keepdims=True)
        return vmax, idx

    gl = jnp.where(lane < N_GROUPS, logits, ninf)
    gmax, gidx = first_argmax(gl)
    p_top = 1.0 / jnp.sum(jnp.exp(gl - gmax), axis=1, keepdims=True)
    lo = N_GROUPS + EXPERTS_PER_GROUP * gidx
    el = jnp.where((lane >= lo) & (lane < lo + EXPERTS_PER_GROUP), logits, ninf)
    e1max, e1 = first_argmax(el)
    e2max, e2 = first_argmax(jnp.where(lane == e1, ninf, el))
    t = jnp.exp(e2max - e1max)
    w1 = 1.0 / (1.0 + t)
    w2 = t / (1.0 + t)
    col = lax.broadcasted_iota(I32, (tm, ROUTE_W), 1)
    rf_ref[...] = jnp.where(col == 0, p_top * w1, jnp.where(col == 1, p_top * w2, 0.0))

    x1id = e1 - N_GROUPS
    x2id = e2 - N_GROUPS
    hit1 = lane == x1id
    hit2 = lane == x2id
    onehot = jnp.where(hit1 | hit2, 1.0, 0.0)
    before = jnp.dot(tri_ref[...], onehot.astype(BF16), preferred_element_type=F32) + carry_ref[...]
    r1 = jnp.sum(jnp.where(hit1, before, 0.0), axis=1, keepdims=True)
    r2 = jnp.sum(jnp.where(hit2, before, 0.0), axis=1, keepdims=True)
    real_tile = jnp.where(step > 0, 1.0, 0.0)
    carry_ref[...] = carry_ref[...] + real_tile * jnp.sum(onehot, axis=0, keepdims=True)
    cnt_ref[...] = carry_ref[...]
    cols = jnp.where(lane == 0.0, x1id, jnp.where(lane == 1.0, x2id,
                                                  jnp.where(lane == 2.0, r1,
                                                            jnp.where(lane == 3.0, r2, 0.0))))
    rank_ref[0] = cols.T[0:ROUTE_W, :].astype(I32)

    gate1 = mod_ref[0, 2:3, :]
    shift2 = mod_ref[0, 3:4, :]
    scale2 = mod_ref[0, 4:5, :]
    x1 = x_ref[...] + gate1 * y
    x1_ref[...] = x1
    ms = jnp.mean(x1 * x1, axis=-1, keepdims=True)
    h2 = (x1 * lax.rsqrt(ms + NORM_EPS) * g2_ref[...]) * (1.0 + scale2) + shift2
    _store_row_tiles(h2_ref, h2)
    h2_hi = h2.astype(BF16)
    h2_lo = (h2 - h2_hi.astype(F32)).astype(BF16)
    hi_both = jnp.dot(h2_hi, wr2_ref[...], preferred_element_type=F32)
    next_logits = (hi_both[:, :LANES] + hi_both[:, LANES:]
                   + jnp.dot(h2_lo, wr2_ref[:, :LANES], preferred_element_type=F32)) + br_ref[...]

    lg_ref[...] = next_logits


def _merge_call(x2, o_a, o_b, sa, sb, mod, norm2_g, wa, wb, wo, wr2, br):
    nb = SEQ // TM_MERGE
    n_tiles = TOKENS // TM_MERGE
    cur = lambda i: jnp.minimum(i, n_tiles - 1)
    prev = lambda i: jnp.maximum(i - 1, 0)
    tok = lambda w: pl.BlockSpec((TM_MERGE, w), lambda i: (cur(i), 0))
    const2 = lambda r, c: pl.BlockSpec((r, c), lambda i: (0, 0))
    return pl.pallas_call(
        _merge_kernel,
        grid=(n_tiles + 1,),
        in_specs=[tok(D_MODEL), tok(GLA_V_W), tok(DIFF_V_W), tok(D_MODEL), tok(D_MODEL),
                  pl.BlockSpec((1, N_MOD, D_MODEL), lambda i: (cur(i) // nb, 0, 0)),
                  const2(1, D_MODEL),
                  const2(GLA_V_W, D_MODEL), const2(DIFF_V_W, D_MODEL), const2(D_MODEL, D_MODEL),
                  const2(D_MODEL, 2 * LANES), const2(1, LANES)],
        out_specs=(tok(D_MODEL), _row_tile_spec(TM_MERGE, lambda i: (cur(i), 0)),
                   pl.BlockSpec((TM_MERGE, ROUTE_W), lambda i: (prev(i), 0)),
                   pl.BlockSpec((1, ROUTE_W, TM_MERGE), lambda i: (prev(i), 0, 0)),
                   pl.BlockSpec((1, LANES), lambda i: (0, 0))),
        out_shape=(jax.ShapeDtypeStruct((TOKENS, D_MODEL), F32),
                   jax.ShapeDtypeStruct((TOKENS * ROW_TILE_S, LANES), F32),
                   jax.ShapeDtypeStruct((TOKENS, ROUTE_W), F32),
                   jax.ShapeDtypeStruct((n_tiles, ROUTE_W, TM_MERGE), I32),
                   jax.ShapeDtypeStruct((1, LANES), F32)),
        scratch_shapes=[pltpu.VMEM((1, LANES), F32), pltpu.VMEM((TM_MERGE, TM_MERGE), BF16),
                        pltpu.VMEM((TM_MERGE, LANES), F32)],
        compiler_params=_cparams(("arbitrary",)),
        name="merge",
    )(x2, o_a, o_b, sa, sb, mod, norm2_g, wa, wb, wo, wr2, br)


def _plan_kernel(rk_ref, cnt_ref, dest_ref, blk_ref):
    n_tiles, _, tm = rk_ref.shape
    cnt = cnt_ref[...]
    padded = jnp.floor((cnt + (MOE_BLK - 1)) * (1.0 / MOE_BLK)) * MOE_BLK
    lane = lax.broadcasted_iota(I32, (1, LANES), 1)
    seg_end = padded
    step = 1
    while step < N_EXPERTS:
        seg_end = seg_end + jnp.where(lane >= step, pltpu.roll(seg_end, step, 1), 0.0)
        step *= 2
    seg_start = seg_end - padded
    valid_end = seg_start + cnt

    ei = lax.broadcasted_iota(I32, (LANES, LANES), 0)
    ej = lax.broadcasted_iota(I32, (LANES, LANES), 1)

    def to_col(rowv):
        return jnp.sum(jnp.where(ei == ej, rowv, 0.0), axis=1, keepdims=True)

    start_col, end_col, valid_col = to_col(seg_start), to_col(seg_end), to_col(valid_end)

    nb_pad = blk_ref.shape[1]
    e_sub = lax.broadcasted_iota(I32, (LANES, nb_pad), 0)
    b_start = (lax.broadcasted_iota(I32, (1, nb_pad), 1) * MOE_BLK).astype(F32)
    ends_before = jnp.where((e_sub < N_EXPERTS) & (end_col <= b_start), 1.0, 0.0)
    block_e = jnp.minimum(jnp.sum(ends_before, axis=0, keepdims=True), N_EXPERTS - 1.0)
    block_valid_end = jnp.sum(jnp.where(e_sub.astype(F32) == block_e, valid_col, 0.0),
                              axis=0, keepdims=True)
    n_valid = jnp.clip(block_valid_end - b_start, 0.0, float(MOE_BLK))
    n_used = jnp.max(seg_end, axis=1, keepdims=True) * (1.0 / MOE_BLK)
    nonempty = (to_col(cnt) > 0.0) & (e_sub < N_EXPERTS)
    e_subf = e_sub.astype(F32)
    no_next = float(LANES)
    nxt = jnp.min(jnp.where(nonempty & (e_subf > block_e), e_subf, no_next), axis=0, keepdims=True)
    nxt = jnp.where(nxt == no_next, -1.0, nxt)
    seg_idx = jnp.sum(jnp.where(nonempty & (e_subf < block_e), 1.0, 0.0), axis=0, keepdims=True)
    slot = seg_idx - 2.0 * jnp.floor(seg_idx * 0.5)
    blk_ref[...] = jnp.concatenate(
        [block_e, n_valid, jnp.broadcast_to(n_used, (1, nb_pad)), nxt, slot,
         jnp.zeros((ROUTE_W - 5, nb_pad), F32)], axis=0).astype(I32)

    e_tok = lax.broadcasted_iota(I32, (LANES, tm), 0)

    def tile(t, carry):
        rk = rk_ref[t]
        d = [jnp.sum(jnp.where(e_tok == rk[k:k + 1], start_col, 0.0), axis=0, keepdims=True)
             + rk[k + 2:k + 3].astype(F32) for k in range(TOP_K)]
        dest_ref[t] = jnp.concatenate(d + [jnp.zeros((ROUTE_W - TOP_K, tm), F32)], axis=0).astype(I32)
        return carry

    lax.fori_loop(0, n_tiles, tile, 0)


def _plan_call(rank_rows, cnt):
    n_tiles = TOKENS // TM_MERGE
    nb_pad = -(-N_BLOCKS // LANES) * LANES
    dest, blk = pl.pallas_call(
        _plan_kernel,
        out_shape=(jax.ShapeDtypeStruct((n_tiles, ROUTE_W, TM_MERGE), I32),
                   jax.ShapeDtypeStruct((ROUTE_W, nb_pad), I32)),
        compiler_params=pltpu.CompilerParams(vmem_limit_bytes=VMEM_LIMIT),
        name="plan",
    )(rank_rows, cnt)
    dest1 = dest[:, 0, :].reshape(TOKENS)
    dest2 = dest[:, 1, :].reshape(TOKENS)
    tables = tuple(blk[r, :N_BLOCKS] for r in (0, 1, 3, 4))
    return dest1, dest2, tables, blk[2, :1]


def _sc_params():
    return pltpu.CompilerParams(use_tc_tiling_on_sc=True)


def _sc_mesh():
    return plsc.VectorSubcoreMesh(core_axis_name="core", subcore_axis_name="subcore")


def _sc_worker_base(per_worker):
    wid = lax.axis_index("subcore") * SC_CORES + lax.axis_index("core")
    return wid * per_worker


def _sc_scatter_rows(src, dest1, dest2, n_out):
    n = src.shape[0]
    per_worker = n // SC_WORKERS
    assert per_worker * SC_WORKERS == n and per_worker % SC_CHUNK == 0

    def body(src_hbm, d1_hbm, d2_hbm, out_hbm, idx1_v, idx2_v, rows_v, lsem, ssem1, ssem2):
        base = _sc_worker_base(per_worker)

        @pl.loop(0, per_worker // SC_CHUNK)
        def _(j):
            start = pl.multiple_of(base + j * SC_CHUNK, SC_CHUNK)
            rows_in = pltpu.async_copy(src_hbm.at[pl.ds(start, SC_CHUNK)], rows_v, lsem)
            pltpu.sync_copy(d1_hbm.at[pl.ds(start, SC_CHUNK)], idx1_v)
            pltpu.sync_copy(d2_hbm.at[pl.ds(start, SC_CHUNK)], idx2_v)
            rows_in.wait()
            out1 = pltpu.async_copy(rows_v, out_hbm.at[idx1_v], ssem1)
            out2 = pltpu.async_copy(rows_v, out_hbm.at[idx2_v], ssem2)
            out1.wait()
            out2.wait()

    return pl.kernel(
        body,
        out_type=jax.ShapeDtypeStruct((n_out, ROW_TILE_S, LANES), F32),
        mesh=_sc_mesh(),
        scratch_types=[pltpu.VMEM((SC_CHUNK,), I32), pltpu.VMEM((SC_CHUNK,), I32),
                       pltpu.VMEM((SC_CHUNK, ROW_TILE_S, LANES), F32),
                       pltpu.SemaphoreType.DMA, pltpu.SemaphoreType.DMA, pltpu.SemaphoreType.DMA],
        compiler_params=_sc_params(),
        name="sc_dispatch",
    )(src, dest1, dest2)


def _sc_gather_rows(table, idx):
    n = idx.shape[0]
    per_worker = n // SC_WORKERS
    half = SC_CHUNK // 2
    assert per_worker * SC_WORKERS == n and per_worker % SC_CHUNK == 0

    def body(table_hbm, idx_hbm, out_hbm, idx_v, rows_a, rows_b, gsem_a, gsem_b, wsem_a, wsem_b):
        base = pl.multiple_of(_sc_worker_base(per_worker), SC_CHUNK)
        pltpu.sync_copy(idx_hbm.at[pl.ds(base, per_worker)], idx_v)

        @pl.loop(0, per_worker // SC_CHUNK)
        def _(j):
            off_a = pl.multiple_of(j * SC_CHUNK, SC_CHUNK)
            off_b = pl.multiple_of(j * SC_CHUNK + half, half)
            ga = pltpu.async_copy(table_hbm.at[idx_v.at[pl.ds(off_a, half)]], rows_a, gsem_a)
            gb = pltpu.async_copy(table_hbm.at[idx_v.at[pl.ds(off_b, half)]], rows_b, gsem_b)
            ga.wait()
            wa = pltpu.async_copy(rows_a, out_hbm.at[pl.ds(base + off_a, half)], wsem_a)
            gb.wait()
            wb = pltpu.async_copy(rows_b, out_hbm.at[pl.ds(base + off_b, half)], wsem_b)
            wa.wait()
            wb.wait()

    return pl.kernel(
        body,
        out_type=jax.ShapeDtypeStruct((n, ROW_TILE_S, LANES), F32),
        mesh=_sc_mesh(),
        scratch_types=[pltpu.VMEM((per_worker,), I32),
                       pltpu.VMEM((half, ROW_TILE_S, LANES), F32),
                       pltpu.VMEM((half, ROW_TILE_S, LANES), F32),
                       pltpu.SemaphoreType.DMA, pltpu.SemaphoreType.DMA,
                       pltpu.SemaphoreType.DMA, pltpu.SemaphoreType.DMA],
        compiler_params=_sc_params(),
        name="sc_gather",
    )(table, idx)


def _weight_fetch(w_hbm, stage, sem, e, slot):
    return [pltpu.make_async_copy(w.at[e], st.at[slot], sem.at[slot]) for w, st in zip(w_hbm, stage)]


def _expert_kernel(be_ref, nv_ref, nx_ref, sl_ref, nu_ref, xs_ref, w1_hbm, w3_hbm, w2_hbm, y_ref,
                   w1s, w3s, w2s, w1b, w3b, w2b, sem):
    i = pl.program_id(0)
    e = be_ref[i]
    used = i < nu_ref[0]
    first = (i == 0) | (e != be_ref[jnp.maximum(i - 1, 0)])
    w_hbm = (w1_hbm, w3_hbm, w2_hbm)
    stage = (w1s, w3s, w2s)

    @pl.when(used & first)
    def _():
        slot = sl_ref[i]

        @pl.when(i == 0)
        def _():
            for cp in _weight_fetch(w_hbm, stage, sem, e, slot):
                cp.start()

        for cp in _weight_fetch(w_hbm, stage, sem, e, slot):
            cp.wait()
        w1b[...] = w1s[slot].astype(BF16)
        w3b[...] = w3s[slot].astype(BF16)
        w2b[...] = w2s[slot].astype(BF16)

        @pl.when(nx_ref[i] >= 0)
        def _():
            for cp in _weight_fetch(w_hbm, stage, sem, nx_ref[i], 1 - slot):
                cp.start()

    n_parts = MOE_BLK // MOE_SUB

    def load_x(part):
        row0 = part * MOE_SUB
        row = lax.broadcasted_iota(I32, (MOE_SUB, D_MODEL), 0) + row0
        return jnp.where(row < nv_ref[i], _load_row_tiles(xs_ref, row0, MOE_SUB), 0.0).astype(BF16)

    def zero_y(part):
        y_ref[pl.ds(part * MOE_SUB * ROW_TILE_S, MOE_SUB * ROW_TILE_S), :] = jnp.zeros(
            (MOE_SUB * ROW_TILE_S, LANES), F32)

    def run(parts):
        hmid = []
        for part in parts:
            xb = load_x(part)
            a = jnp.dot(xb, w1b[...], preferred_element_type=F32)
            g = jnp.dot(xb, w3b[...], preferred_element_type=F32)
            hmid.append(((a * _sigmoid(a)) * g).astype(BF16))
        for part, hm in zip(parts, hmid):
            _store_row_tiles(y_ref, jnp.dot(hm, w2b[...], preferred_element_type=F32),
                             part * MOE_SUB)

    n_live = jnp.where(used, (nv_ref[i] + (MOE_SUB - 1)) // MOE_SUB, 0)
    for k in range(n_parts + 1):
        @pl.when(n_live == k)
        def _(k=k):
            if k:
                run(list(range(k)))
            for part in range(k, n_parts):
                zero_y(part)


def _expert_call(tables, n_used, xs, w1, w3, w2):
    n_tab = len(tables)
    grid_spec = pltpu.PrefetchScalarGridSpec(
        num_scalar_prefetch=n_tab + 1,
        grid=(N_BLOCKS,),
        in_specs=[_row_tile_spec(MOE_BLK, lambda i, *pf: (jnp.minimum(i, pf[n_tab][0] - 1), 0)),
                  pl.BlockSpec(memory_space=pl.ANY),
                  pl.BlockSpec(memory_space=pl.ANY),
                  pl.BlockSpec(memory_space=pl.ANY)],
        out_specs=_row_tile_spec(MOE_BLK, lambda i, *pf: (i, 0)),
        scratch_shapes=[pltpu.VMEM((2, D_MODEL, D_EXPERT), F32),
                        pltpu.VMEM((2, D_MODEL, D_EXPERT), F32),
                        pltpu.VMEM((2, D_EXPERT, D_MODEL), F32),
                        pltpu.VMEM((D_MODEL, D_EXPERT), BF16),
                        pltpu.VMEM((D_MODEL, D_EXPERT), BF16),
                        pltpu.VMEM((D_EXPERT, D_MODEL), BF16),
                        pltpu.SemaphoreType.DMA((2,))],
    )
    return pl.pallas_call(
        _expert_kernel,
        grid_spec=grid_spec,
        out_shape=jax.ShapeDtypeStruct((N_BLOCKS * MOE_BLK * ROW_TILE_S, LANES), F32),
        compiler_params=_cparams(("arbitrary",)),
        name="experts",
    )(*tables, n_used, xs, w1, w3, w2)


def _combine_rows_kernel(x1_ref, rf_ref, mod_ref, ya_ref, yb_ref, *rest):
    o_ref = rest[-1]
    rf = rf_ref[...]
    gate2 = mod_ref[0, 5:6, :]
    moe = rf[:, 0:1] * _load_row_tiles(ya_ref) + rf[:, 1:2] * _load_row_tiles(yb_ref)
    o_ref[...] = x1_ref[...] + gate2 * moe


def _combine_rows_call(x1, rf, mod, yg, part, prev=None):
    nb = SEQ // TM_ROW
    n_half = TOKENS // TM_ROW // 2
    t0 = part * n_half
    in_specs = [pl.BlockSpec((TM_ROW, D_MODEL), lambda i: (t0 + i, 0)),
                pl.BlockSpec((TM_ROW, ROUTE_W), lambda i: (t0 + i, 0)),
                pl.BlockSpec((1, N_MOD, D_MODEL), lambda i: ((t0 + i) // nb, 0, 0)),
                _row_tile_spec(TM_ROW, lambda i: (i, 0)),
                _row_tile_spec(TM_ROW, lambda i: (i + n_half, 0))]
    args = [x1, rf, mod, yg, yg]
    aliases = {}
    if prev is not None:
        in_specs.append(pl.BlockSpec(memory_space=pl.ANY))
        args.append(prev)
        aliases = {len(args) - 1: 0}
    return pl.pallas_call(
        _combine_rows_kernel,
        grid=(n_half,),
        in_specs=in_specs,
        out_specs=pl.BlockSpec((TM_ROW, D_MODEL), lambda i: (t0 + i, 0)),
        out_shape=jax.ShapeDtypeStruct((TOKENS, D_MODEL), F32),
        input_output_aliases=aliases,
        compiler_params=_cparams(("parallel",)),
        name="combine",
    )(*args)


def kernel(x, c, positions, ada_w, ada_b, norm1_g, norm2_g, w_in, gla_alpha_up, gla_alpha_b,
           gla_out_g, diff_q_g, diff_k_g, diff_lq1, diff_lk1, diff_lq2, diff_lk2, diff_out_g,
           w_branch_a, w_branch_b, w_out, router_group_w, router_group_b, router_expert_w,
           router_expert_b, expert_w1, expert_w3, expert_w2):
    assert x.shape == (BATCH, SEQ, D_MODEL) and ada_w.shape[0] == 1
    x2 = x.reshape(TOKENS, D_MODEL)

    w_in_p = _wprep_call(w_in[0])
    au_hi = gla_alpha_up[0].astype(BF16)
    au_lo = (gla_alpha_up[0] - au_hi.astype(F32)).astype(BF16)
    au_stack = jnp.concatenate(
        [au_hi, au_hi, au_lo, jnp.zeros((LR_PAD - 3 * GLA_GATE_RANK, GLA_QK_W), BF16)], axis=0)
    gid = jnp.arange(NORM_BLK) // DIFF_DH
    bd = jnp.where(gid[:, None] == gid[None, :], 1.0 / DIFF_DH, 0.0).astype(BF16)
    qg_row = jnp.tile(diff_q_g[0], DIFF_QK_W // DIFF_DH).reshape(1, DIFF_QK_W)
    kg_row = jnp.tile(diff_k_g[0], DIFF_QK_W // DIFF_DH).reshape(1, DIFF_QK_W)
    inv_freq = ROPE_THETA ** (-jnp.arange(ROPE_HALF, dtype=F32) / ROPE_HALF)
    invf64 = jnp.concatenate([inv_freq, inv_freq, jnp.zeros((DIFF_DH - ROPE_DIM,), F32)])
    invf_row = jnp.tile(invf64, LANES // DIFF_DH).reshape(1, LANES)
    pos_col = positions.reshape(TOKENS, 1)
    lam_p = jnp.concatenate([diff_lq1, diff_lk1, diff_lq2, diff_lk2], axis=0)
    wr = (jnp.zeros((D_MODEL, LANES), F32)
          .at[:, :N_GROUPS].set(router_group_w[0])
          .at[:, N_GROUPS:N_GROUPS + N_EXPERTS].set(router_expert_w[0]))
    wr_hi = wr.astype(BF16)
    wr2 = jnp.concatenate([wr_hi, (wr - wr_hi.astype(F32)).astype(BF16)], axis=1)
    br = (jnp.zeros((1, LANES), F32)
          .at[0, :N_GROUPS].set(router_group_b[0])
          .at[0, N_GROUPS:N_GROUPS + N_EXPERTS].set(router_expert_b[0]))

    mod = _mod_call(c, ada_w[0], ada_b[0])
    gqk, gv, gr, lr, qT, kk, vT, sa, sb = _in_call(
        x2, mod, norm1_g, w_in_p, bd, qg_row, kg_row, pos_col, invf_row)
    o_a = _gla_call(gqk, gv, gr, lr, au_stack, gla_alpha_b, gla_out_g)
    o_b = _attn_call(qT, kk, vT, lam_p, diff_out_g.reshape(DIFF_DV, 1))
    x1, h2, rf, rank, cnt = _merge_call(
        x2, o_a, o_b, sa, sb, mod, norm2_g, w_branch_a[0].astype(BF16),
        w_branch_b[0].astype(BF16), w_out[0].astype(BF16), wr2, br)
    dest1, dest2, tables, n_used = _plan_call(rank, cnt)

    xs = _sc_scatter_rows(_as_row_tiles(h2), dest1, dest2, N_BLOCKS * MOE_BLK)
    y = _expert_call(tables, n_used, _as_2d(xs), expert_w1[0], expert_w3[0], expert_w2[0])
    y3 = _as_row_tiles(y)
    half = TOKENS // 2
    out = None
    for part in range(2):
        tok = slice(part * half, (part + 1) * half)
        yg = _sc_gather_rows(y3, jnp.concatenate([dest1[tok], dest2[tok]]))
        out = _combine_rows_call(x1, rf, mod, _as_2d(yg), part, out)
    return out.reshape(BATCH, SEQ, D_MODEL)
```

```python
import math

import jax
import jax.numpy as jnp
from jax import lax
from jax.experimental import pallas as pl
from jax.experimental.pallas import tpu as pltpu
from jax.experimental.pallas import tpu_sc as plsc

F32 = jnp.float32
BF16 = jnp.bfloat16
I32 = jnp.int32

D_MODEL = 1024
BATCH = 4
SEQ = 4096
TOKENS = BATCH * SEQ
N_MOD = 6
NORM_EPS = 1e-6

GLA_HEADS = 4
GLA_DK = 64
GLA_DV = 128
GLA_GATE_RANK = 16
GLA_GATE_TAU = 16.0
GLA_CHUNK = 64
GLA_QK_W = GLA_HEADS * GLA_DK
GLA_V_W = GLA_HEADS * GLA_DV

DIFF_HEADS = 4
DIFF_DH = 64
DIFF_DV = 2 * DIFF_DH
DIFF_QK_W = DIFF_HEADS * 2 * DIFF_DH
DIFF_V_W = DIFF_HEADS * DIFF_DV
ROPE_THETA = 500000.0
ROPE_DIM = DIFF_DH // 4
ROPE_HALF = ROPE_DIM // 2
NEG_INF = -1e30
LAMBDA_INIT = 0.8 - 0.6 * 1.0

N_GROUPS = 4
EXPERTS_PER_GROUP = 8
N_EXPERTS = N_GROUPS * EXPERTS_PER_GROUP
TOP_K = 2
D_EXPERT = 512

LANES = 128
SUBLANES = 8
ROW_TILE_S = D_MODEL // LANES
SC_CORES = 2
SC_SUBCORES = 16
SC_WORKERS = SC_CORES * SC_SUBCORES
SC_CHUNK = 64
LR_PAD = LANES
D_IN_PAD = 2 * GLA_QK_W + 2 * GLA_V_W + 2 * DIFF_QK_W + DIFF_V_W + 2 * D_MODEL + LR_PAD
IN_COLS = {}
_col = 0
for _name, _width in (("gla_qk", 2 * GLA_QK_W), ("gla_v", GLA_V_W), ("gla_r", GLA_V_W),
                      ("diff_q", DIFF_QK_W), ("diff_k", DIFF_QK_W), ("diff_v", DIFF_V_W),
                      ("gate_a", D_MODEL), ("gate_b", D_MODEL), ("gla_lr", LR_PAD)):
    IN_COLS[_name] = (_col, _col + _width)
    _col += _width
assert _col == D_IN_PAD
WPREP_COLS = 512
WPREP_LR_STEP = (D_IN_PAD - LR_PAD) // WPREP_COLS
D_IN_ALLOC = (WPREP_LR_STEP + 1) * WPREP_COLS

TM_IN = 512
TQ = 512
V_ROWS = DIFF_DV + 16
NORM_BLK = 256
ATT_SUB = 32
Q_SCALE = DIFF_DH ** -0.5 * math.log2(math.e)
N_KV = SEQ // TQ
TT_GLA = 512
TM_MERGE = 512
TM_ROW = 512
MOE_BLK = 512
MOE_SUB = 256
N_BLOCKS = (TOKENS * TOP_K + N_EXPERTS * (MOE_BLK - 1) + MOE_BLK - 1) // MOE_BLK
ROUTE_W = 8

VMEM_LIMIT = 56 * 1024 * 1024


def _cparams(sem):
    return pltpu.CompilerParams(dimension_semantics=sem, vmem_limit_bytes=VMEM_LIMIT)


def _sigmoid(x):
    return 1.0 / (1.0 + jnp.exp(-x))


def _row_tile_spec(rows, index_map):
    return pl.BlockSpec((rows * ROW_TILE_S, LANES), index_map)


def _as_row_tiles(a2d):
    return a2d.reshape(a2d.shape[0] // ROW_TILE_S, ROW_TILE_S, LANES)


def _as_2d(a3d):
    return a3d.reshape(a3d.shape[0] * ROW_TILE_S, LANES)


def _store_row_tiles(ref, val, row0=0):
    for s in range(ROW_TILE_S):
        ref[pl.ds(row0 * ROW_TILE_S + s, val.shape[0], stride=ROW_TILE_S), :] = (
            val[:, s * LANES:(s + 1) * LANES])


def _load_row_tiles(ref, row0=0, rows=None):
    rows = ref.shape[0] // ROW_TILE_S if rows is None else rows
    return jnp.concatenate(
        [ref[pl.ds(row0 * ROW_TILE_S + s, rows, stride=ROW_TILE_S), :] for s in range(ROW_TILE_S)],
        axis=1)


def _mod_kernel(ct_ref, w_ref, b_ref, o_ref):
    ct = ct_ref[...]
    ca = ct * _sigmoid(ct)
    w = w_ref[...]
    rows = []
    for b in range(BATCH):
        rows.append(jnp.sum(w * ca[:, b:b + 1], axis=0, keepdims=True) + b_ref[...])
    rows.append(jnp.zeros((8 - BATCH, w.shape[1]), F32))
    o_ref[...] = jnp.concatenate(rows, axis=0)


def _mod_call(c, ada_w, ada_b):
    tn = D_MODEL
    ct = jnp.zeros((D_MODEL, 8), F32).at[:, :BATCH].set(c.T)
    out = pl.pallas_call(
        _mod_kernel,
        grid=(N_MOD,),
        in_specs=[
            pl.BlockSpec((D_MODEL, 8), lambda j: (0, 0)),
            pl.BlockSpec((D_MODEL, tn), lambda j: (0, j)),
            pl.BlockSpec((1, tn), lambda j: (0, j)),
        ],
        out_specs=pl.BlockSpec((8, tn), lambda j: (0, j)),
        out_shape=jax.ShapeDtypeStruct((8, N_MOD * D_MODEL), F32),
        compiler_params=_cparams(("arbitrary",)),
        name="mod",
    )(ct, ada_w, ada_b.reshape(1, N_MOD * D_MODEL))
    return out[:BATCH].reshape(BATCH, N_MOD, D_MODEL)


def _wprep_kernel(w_ref, o_ref):
    j = pl.program_id(0)
    t = w_ref[...].T
    lane = lax.broadcasted_iota(I32, t.shape, 1)
    @pl.when(j < WPREP_LR_STEP)
    def _():
        o_ref[...] = t.astype(BF16)

    @pl.when(j == WPREP_LR_STEP)
    def _():
        rep = jnp.where(lane < GLA_GATE_RANK, t,
                        jnp.where(lane < 2 * GLA_GATE_RANK, pltpu.roll(t, GLA_GATE_RANK, 1),
                                  jnp.where(lane < 3 * GLA_GATE_RANK,
                                            pltpu.roll(t, 2 * GLA_GATE_RANK, 1), 0.0)))
        o_ref[...] = rep.astype(BF16)


def _wprep_call(w_in0):
    w_t = w_in0.T
    lr0 = 2 * GLA_QK_W + 2 * GLA_V_W
    assert lr0 % WPREP_COLS == 0 and (D_IN_PAD - LR_PAD) % WPREP_COLS == 0

    def src_row(j):
        return jnp.where(j < lr0 // WPREP_COLS, j * WPREP_COLS,
                         jnp.where(j < WPREP_LR_STEP, j * WPREP_COLS + GLA_GATE_RANK, lr0))

    return pl.pallas_call(
        _wprep_kernel,
        grid=(WPREP_LR_STEP + 1,),
        in_specs=[pl.BlockSpec((pl.Element(WPREP_COLS), pl.Element(D_MODEL)),
                               lambda j: (pl.multiple_of(src_row(j), SUBLANES), 0))],
        out_specs=pl.BlockSpec((D_MODEL, WPREP_COLS), lambda j: (0, j)),
        out_shape=jax.ShapeDtypeStruct((D_MODEL, D_IN_ALLOC), BF16),
        compiler_params=_cparams(("parallel",)),
        name="w_prep",
    )(w_t)


def _in_kernel(x_ref, mod_ref, g1_ref, w_ref, bd_ref, qg_ref, kg_ref, pos_ref, invf_ref,
               gqk_ref, gv_ref, gr_ref, lr_ref, qT_ref, k_ref, vT_ref, sa_ref, sb_ref,
               cos_scr, sin_scr):
    pos_col = jnp.broadcast_to(pos_ref[0].astype(F32), (LANES, pos_ref.shape[2])).T[:, 0:1]
    ang = pos_col * invf_ref[...]
    cos_scr[...] = jnp.cos(ang)
    sin_scr[...] = jnp.sin(ang)

    x = x_ref[...]
    shift1 = mod_ref[0, 0:1, :]
    scale1 = mod_ref[0, 1:2, :]
    ms = jnp.mean(x * x, axis=-1, keepdims=True)
    h = (x * lax.rsqrt(ms + NORM_EPS) * g1_ref[...]) * (1.0 + scale1) + shift1
    hb = h.astype(BF16)

    def proj(name):
        c0, c1 = IN_COLS[name]
        return jnp.dot(hb, w_ref[:, c0:c1], preferred_element_type=F32)

    tm = x.shape[0]
    cos4 = jnp.concatenate([cos_scr[...]] * DIFF_HEADS, axis=1)
    sin4 = jnp.concatenate([sin_scr[...]] * DIFF_HEADS, axis=1)
    lane = lax.broadcasted_iota(I32, (tm, DIFF_QK_W), 1)
    first_half = (lane % DIFF_DH) < ROPE_HALF
    bd = bd_ref[...]

    def norm_rope(t, gain_row):
        t2 = t * t
        hi = t2.astype(BF16)
        lo = (t2 - hi.astype(F32)).astype(BF16)
        hw = bd.shape[0]
        gms = jnp.concatenate(
            [jnp.dot(hi[:, c0:c0 + hw], bd, preferred_element_type=F32)
             + jnp.dot(lo[:, c0:c0 + hw], bd, preferred_element_type=F32)
             for c0 in range(0, DIFF_QK_W, hw)], axis=1)
        t = t * lax.rsqrt(gms + NORM_EPS) * gain_row
        nxt = pltpu.roll(t, DIFF_QK_W - ROPE_HALF, 1)
        prv = pltpu.roll(t, ROPE_HALF, 1)
        return t * cos4 + jnp.where(first_half, -nxt, prv) * sin4

    dq_raw = proj("diff_q")
    dk_raw = proj("diff_k")
    gqk_ref[...] = proj("gla_qk").astype(BF16)
    dq = norm_rope(dq_raw, qg_ref[...]) * Q_SCALE
    gv_ref[...] = proj("gla_v").astype(BF16)
    dk = norm_rope(dk_raw, kg_ref[...])
    dv = proj("diff_v")
    for hd in range(DIFF_HEADS):
        sl = slice(hd * LANES, (hd + 1) * LANES)
        qT_ref[0, hd, 0] = dq[:, sl].T.astype(BF16)
        k_ref[0, hd, 0] = dk[:, sl].astype(BF16)
    gr_ref[...] = proj("gla_r").astype(BF16)
    for hd in range(DIFF_HEADS):
        sl = slice(hd * LANES, (hd + 1) * LANES)
        vT_ref[0, hd, 0] = jnp.concatenate(
            [dv[:, sl].T, jnp.ones((V_ROWS - DIFF_DV, tm), F32)], axis=0).astype(BF16)
    sa_ref[...] = _sigmoid(proj("gate_a")).astype(BF16)
    sb_ref[...] = _sigmoid(proj("gate_b")).astype(BF16)
    lr_ref[...] = proj("gla_lr")


def _in_call(x2, mod, norm1_g, w_in_p, bd, qg_row, kg_row, pos_rows, invf_row):
    nb = SEQ // TM_IN
    tok_spec = lambda w: pl.BlockSpec((TM_IN, w), lambda i: (i, 0))
    const2 = lambda r, c: pl.BlockSpec((r, c), lambda i: (0, 0))
    out_shapes = (
        jax.ShapeDtypeStruct((TOKENS, 2 * GLA_QK_W), BF16),
        jax.ShapeDtypeStruct((TOKENS, GLA_V_W), BF16),
        jax.ShapeDtypeStruct((TOKENS, GLA_V_W), BF16),
        jax.ShapeDtypeStruct((TOKENS, LR_PAD), F32),
        jax.ShapeDtypeStruct((BATCH, DIFF_HEADS, N_KV, LANES, TQ), BF16),
        jax.ShapeDtypeStruct((BATCH, DIFF_HEADS, N_KV, TQ, LANES), BF16),
        jax.ShapeDtypeStruct((BATCH, DIFF_HEADS, N_KV, V_ROWS, TQ), BF16),
        jax.ShapeDtypeStruct((TOKENS, D_MODEL), BF16),
        jax.ShapeDtypeStruct((TOKENS, D_MODEL), BF16),
    )
    out_specs = (
        tok_spec(2 * GLA_QK_W), tok_spec(GLA_V_W), tok_spec(GLA_V_W), tok_spec(LR_PAD),
        pl.BlockSpec((1, DIFF_HEADS, 1, LANES, TM_IN), lambda i: (i // nb, 0, i % nb, 0, 0)),
        pl.BlockSpec((1, DIFF_HEADS, 1, TM_IN, LANES), lambda i: (i // nb, 0, i % nb, 0, 0)),
        pl.BlockSpec((1, DIFF_HEADS, 1, V_ROWS, TM_IN), lambda i: (i // nb, 0, i % nb, 0, 0)),
        tok_spec(D_MODEL), tok_spec(D_MODEL),
    )
    return pl.pallas_call(
        _in_kernel,
        grid=(TOKENS // TM_IN,),
        in_specs=[
            tok_spec(D_MODEL),
            pl.BlockSpec((1, N_MOD, D_MODEL), lambda i: (i // nb, 0, 0)),
            const2(1, D_MODEL),
            pl.BlockSpec((D_MODEL, D_IN_ALLOC), lambda i: (0, 0), pipeline_mode=pl.Buffered(1)),
            const2(NORM_BLK, NORM_BLK),
            const2(1, DIFF_QK_W), const2(1, DIFF_QK_W),
            pl.BlockSpec((1, 1, TM_IN), lambda i: (i, 0, 0)),
            const2(1, LANES),
        ],
        out_specs=out_specs,
        out_shape=out_shapes,
        scratch_shapes=[pltpu.VMEM((TM_IN, LANES), F32), pltpu.VMEM((TM_IN, LANES), F32)],
        compiler_params=_cparams(("parallel",)),
        name="in_proj",
    )(x2, mod, norm1_g, w_in_p, bd, qg_row, kg_row, pos_rows, invf_row)


def _gla_kernel(qk_ref, v_ref, r_ref, lr_ref, au_ref, ab_ref, og_ref, o_ref,
                state_ref, oacc_ref, snew_ref):
    tt = qk_ref.shape[0]
    n_chunks = tt // GLA_CHUNK

    @pl.when(pl.program_id(1) == 0)
    def _():
        state_ref[...] = jnp.zeros_like(state_ref)

    lr = lr_ref[...]
    lr_hi = lr.astype(BF16)
    lr_lo = (lr - lr_hi.astype(F32)).astype(BF16)
    lr_lane = lax.broadcasted_iota(I32, lr.shape, 1)
    in_lo_group = (lr_lane >= GLA_GATE_RANK) & (lr_lane < 2 * GLA_GATE_RANK)
    stacked = jnp.where(in_lo_group, lr_lo, lr_hi)
    z = jnp.dot(stacked, au_ref[...], preferred_element_type=F32) + ab_ref[...]
    g = (jnp.minimum(z, 0.0) - jnp.log(1.0 + jnp.exp(-jnp.abs(z)))) * (1.0 / GLA_GATE_TAU)

    row = lax.broadcasted_iota(I32, (tt, GLA_QK_W), 0) % GLA_CHUNK
    b = g
    step = 1
    while step < GLA_CHUNK:
        b = b + jnp.where(row >= step, pltpu.roll(b, step, 0), 0.0)
        step *= 2

    b_last_rows = [b[c * GLA_CHUNK + GLA_CHUNK - 1:(c + 1) * GLA_CHUNK, :] for c in range(n_chunks)]
    b_last = jnp.concatenate(
        [jnp.broadcast_to(bl, (GLA_CHUNK, GLA_QK_W)) for bl in b_last_rows], axis=0)

    qk = qk_ref[...].astype(F32)
    q = qk[:, :GLA_QK_W] * (GLA_DK ** -0.5)
    k = qk[:, GLA_QK_W:]
    q_in = (q * jnp.exp(b)).astype(BF16)
    k_in = (k * jnp.exp(-b)).astype(BF16)
    k_dec = (k * jnp.exp(b_last - b)).astype(BF16)

    ci = lax.broadcasted_iota(I32, (GLA_CHUNK, GLA_CHUNK), 0)
    cj = lax.broadcasted_iota(I32, (GLA_CHUNK, GLA_CHUNK), 1)
    causal = ci >= cj

    pairs = [(c, hd) for c in range(n_chunks) for hd in range(GLA_HEADS)]

    def rows(c):
        return slice(c * GLA_CHUNK, (c + 1) * GLA_CHUNK)

    def kcols(hd):
        return slice(hd * GLA_DK, (hd + 1) * GLA_DK)

    def vcols(hd):
        return slice(hd * GLA_DV, (hd + 1) * GLA_DV)

    att = {}
    for c, hd in pairs:
        a = lax.dot_general(q_in[rows(c), kcols(hd)], k_in[rows(c), kcols(hd)],
                            (((1,), (1,)), ((), ())), preferred_element_type=F32)
        att[c, hd] = jnp.where(causal, a, 0.0).astype(BF16)
    for c, hd in pairs:
        oacc_ref[rows(c), vcols(hd)] = jnp.dot(att[c, hd], v_ref[rows(c), vcols(hd)],
                                               preferred_element_type=F32)
    for c, hd in pairs:
        snew_ref[c * GLA_HEADS + hd] = lax.dot_general(
            k_dec[rows(c), kcols(hd)], v_ref[rows(c), vcols(hd)],
            (((0,), (0,)), ((), ())), preferred_element_type=F32)

    decay_rows = jnp.exp(jnp.concatenate(
        b_last_rows + [jnp.zeros((LANES - n_chunks, GLA_QK_W), F32)], axis=0))
    decay_cols = decay_rows.T
    states = [state_ref[hd] for hd in range(GLA_HEADS)]
    for c in range(n_chunks):
        for hd in range(GLA_HEADS):
            s_prev = states[hd]
            oacc_ref[rows(c), vcols(hd)] += jnp.dot(
                q_in[rows(c), kcols(hd)], s_prev.astype(BF16), preferred_element_type=F32)
            dcol = decay_cols[kcols(hd), c:c + 1]
            states[hd] = s_prev * dcol + snew_ref[c * GLA_HEADS + hd]
    for hd in range(GLA_HEADS):
        state_ref[hd] = states[hd]

    for hd in range(GLA_HEADS):
        vs = slice(hd * GLA_DV, (hd + 1) * GLA_DV)
        oh = oacc_ref[:, vs]
        ms = jnp.mean(oh * oh, axis=-1, keepdims=True)
        y = oh * lax.rsqrt(ms + NORM_EPS) * og_ref[...]
        r = r_ref[:, vs].astype(F32)
        o_ref[:, vs] = (y * (r * _sigmoid(r))).astype(BF16)


def _gla_call(gqk, gv, gr, lr, au_stack, ab_row, og_row):
    nt = SEQ // TT_GLA
    tok = lambda w: pl.BlockSpec((TT_GLA, w), lambda b, t: (b * nt + t, 0))
    const2 = lambda r, c: pl.BlockSpec((r, c), lambda b, t: (0, 0))
    return pl.pallas_call(
        _gla_kernel,
        grid=(BATCH, nt),
        in_specs=[tok(2 * GLA_QK_W), tok(GLA_V_W), tok(GLA_V_W), tok(LR_PAD),
                  const2(LR_PAD, GLA_QK_W), const2(1, GLA_QK_W), const2(1, GLA_DV)],
        out_specs=tok(GLA_V_W),
        out_shape=jax.ShapeDtypeStruct((TOKENS, GLA_V_W), BF16),
        scratch_shapes=[pltpu.VMEM((GLA_HEADS, GLA_DK, GLA_DV), F32),
                        pltpu.VMEM((TT_GLA, GLA_V_W), F32),
                        pltpu.VMEM((TT_GLA // GLA_CHUNK * GLA_HEADS, GLA_DK, GLA_DV), F32)],
        compiler_params=_cparams(("parallel", "arbitrary")),
        name="gla",
    )(gqk, gv, gr, lr, au_stack, ab_row, og_row)


def _attn_kernel(qT_ref, k_ref, vT_ref, lam_ref, og_ref, o_ref,
                 q_scr, s_scr, p_scr, acc_scr, m_scr):
    _attn_load_q(qT_ref, q_scr, 0, 0)
    for c in range(2):
        s_scr[c] = jnp.dot(k_ref[0, 0, 0], q_scr[0, c], preferred_element_type=F32)

    def q_block(i, carry):
        _attn_q_block(i, qT_ref, k_ref, vT_ref, lam_ref, og_ref, o_ref,
                      q_scr, s_scr, p_scr, acc_scr, m_scr)
        return carry

    lax.fori_loop(0, N_KV, q_block, 0)


def _attn_load_q(qT_ref, q_scr, i, slot):
    qT = qT_ref[0, 0, i]
    rowq = lax.broadcasted_iota(I32, qT.shape, 0)
    zero = jnp.zeros_like(qT)
    q_scr[slot, 0] = jnp.where(rowq < DIFF_DH, qT, zero)
    q_scr[slot, 1] = jnp.where(rowq >= DIFF_DH, qT, zero)


def _attn_q_block(i, qT_ref, k_ref, vT_ref, lam_ref, og_ref, o_ref,
                  q_scr, s_scr, p_scr, acc_scr, m_scr):
    slot = i % 2
    n_sub = TQ // ATT_SUB

    def fold8(t, op):
        return op(t.reshape(t.shape[0] // SUBLANES, SUBLANES, TQ), axis=0)

    def scores(c, j):
        s_scr[c] = jnp.dot(k_ref[0, 0, j], q_scr[slot, c], preferred_element_type=F32)

    def load_s(c, r, mask, j=None):
        s = s_scr[c, r * ATT_SUB:(r + 1) * ATT_SUB, :]
        if mask is not None:
            key_i = lax.broadcasted_iota(I32, (ATT_SUB, TQ), 0) + r * ATT_SUB
            qry_i = lax.broadcasted_iota(I32, (ATT_SUB, TQ), 1)
            keep = key_i <= qry_i
            if mask == "maybe":
                keep = jnp.logical_or(keep, j < i)
            s = jnp.where(keep, s, NEG_INF)
        return s

    def block_max(c, mask, j=None):
        m8 = fold8(load_s(c, 0, mask, j), jnp.max)
        for r in range(1, n_sub):
            m8 = jnp.maximum(m8, fold8(load_s(c, r, mask, j), jnp.max))
        return jnp.max(m8, axis=0, keepdims=True)

    def exp_to_p(c, shift, mask):
        for r in range(n_sub):
            p = jnp.exp2(load_s(c, r, mask) - shift)
            p_scr[c, r * ATT_SUB:(r + 1) * ATT_SUB, :] = p.astype(BF16)

    def p_times_v(c, j):
        return jnp.dot(vT_ref[0, 0, j], p_scr[c], preferred_element_type=F32)

    acc_scr[...] = jnp.zeros(acc_scr.shape, F32)

    def fast_pv(c, j, mask):
        exp_to_p(c, m_scr[c], mask)
        acc_scr[c] = acc_scr[c] + p_times_v(c, j)

    def fast_body(j, carry):
        scores(1, j)
        fast_pv(0, j, None)
        scores(0, j + 1)
        fast_pv(1, j, None)
        return carry

    m_scr[0] = block_max(0, "maybe", 0)
    m_scr[1] = block_max(1, "maybe", 0)

    @pl.when(i > 0)
    def _():
        fast_pv(0, 0, None)
        scores(0, 1)
        fast_pv(1, 0, None)

    lax.fori_loop(1, i, fast_body, 0)
    scores(1, i)
    fast_pv(0, i, "diag")
    fast_pv(1, i, "diag")

    total = jnp.sum(jnp.abs(acc_scr[0])) + jnp.sum(jnp.abs(acc_scr[1]))
    overflowed = jnp.logical_not(total < jnp.inf)

    @pl.when(overflowed)
    def _():
        m_scr[...] = jnp.full(m_scr.shape, NEG_INF, F32)
        acc_scr[...] = jnp.zeros(acc_scr.shape, F32)

        def safe_pv(c, j, mask):
            m_old = m_scr[c]
            m_new = jnp.maximum(m_old, block_max(c, mask))
            alpha = jnp.exp2(m_old - m_new)
            exp_to_p(c, m_new, mask)
            m_scr[c] = m_new
            acc_scr[c] = acc_scr[c] * alpha + p_times_v(c, j)

        def safe_body(j, carry):
            scores(0, j)
            scores(1, j)
            safe_pv(0, j, None)
            safe_pv(1, j, None)
            return carry

        lax.fori_loop(0, i, safe_body, 0)
        scores(0, i)
        scores(1, i)
        safe_pv(0, i, "diag")
        safe_pv(1, i, "diag")

    nxt = jnp.minimum(i + 1, N_KV - 1)
    _attn_load_q(qT_ref, q_scr, nxt, 1 - slot)
    for c in range(2):
        s_scr[c] = jnp.dot(k_ref[0, 0, 0], q_scr[1 - slot, c], preferred_element_type=F32)

    l1 = acc_scr[0, DIFF_DV:DIFF_DV + 1, :]
    l2 = acc_scr[1, DIFF_DV:DIFF_DV + 1, :]

    lam_p = lam_ref[...]
    lam = (jnp.exp(jnp.sum(lam_p[0:1] * lam_p[1:2], axis=1, keepdims=True))
           - jnp.exp(jnp.sum(lam_p[2:3] * lam_p[3:4], axis=1, keepdims=True)) + LAMBDA_INIT)
    oT = acc_scr[0, :DIFF_DV, :] / l1 - lam * (acc_scr[1, :DIFF_DV, :] / l2)
    ms = jnp.mean(oT * oT, axis=0, keepdims=True)
    y = oT * lax.rsqrt(ms + NORM_EPS) * og_ref[...] * (1.0 - LAMBDA_INIT)
    o_ref[pl.ds(pl.multiple_of(i * TQ, TQ), TQ), :] = y.T.astype(BF16)


def _attn_call(qT, kk, vT, lam_p, og_col):
    return pl.pallas_call(
        _attn_kernel,
        grid=(BATCH, DIFF_HEADS),
        in_specs=[
            pl.BlockSpec((1, 1, N_KV, LANES, TQ), lambda b, h: (b, h, 0, 0, 0)),
            pl.BlockSpec((1, 1, N_KV, TQ, LANES), lambda b, h: (b, h, 0, 0, 0)),
            pl.BlockSpec((1, 1, N_KV, V_ROWS, TQ), lambda b, h: (b, h, 0, 0, 0)),
            pl.BlockSpec((4, DIFF_DH), lambda b, h: (0, 0)),
            pl.BlockSpec((DIFF_DV, 1), lambda b, h: (0, 0)),
        ],
        out_specs=pl.BlockSpec((SEQ, DIFF_DV), lambda b, h: (b, h)),
        out_shape=jax.ShapeDtypeStruct((TOKENS, DIFF_V_W), BF16),
        scratch_shapes=[pltpu.VMEM((2, 2, LANES, TQ), BF16),
                        pltpu.VMEM((2, TQ, TQ), F32),
                        pltpu.VMEM((2, TQ, TQ), BF16),
                        pltpu.VMEM((2, V_ROWS, TQ), F32),
                        pltpu.VMEM((2, 1, TQ), F32)],
        compiler_params=_cparams(("parallel", "parallel")),
        name="attn",
    )(qT, kk, vT, lam_p, og_col)


def _merge_kernel(x_ref, oa_ref, ob_ref, sa_ref, sb_ref, mod_ref, g2_ref, wa_ref, wb_ref, wo_ref,
                  wr2_ref, br_ref, x1_ref, h2_ref, rf_ref, rank_ref, cnt_ref,
                  carry_ref, tri_ref, lg_ref):
    tm = x_ref.shape[0]
    step = pl.program_id(0)

    @pl.when(step == 0)
    def _():
        carry_ref[...] = jnp.zeros_like(carry_ref)
        lg_ref[...] = jnp.zeros_like(lg_ref)
        ti = lax.broadcasted_iota(I32, (tm, tm), 0)
        tj = lax.broadcasted_iota(I32, (tm, tm), 1)
        tri_ref[...] = jnp.where(ti > tj, 1.0, 0.0).astype(BF16)

    ma = jnp.dot(oa_ref[...], wa_ref[...], preferred_element_type=F32)
    mb = jnp.dot(ob_ref[...], wb_ref[...], preferred_element_type=F32)
    merged = sa_ref[...].astype(F32) * ma + sb_ref[...].astype(F32) * mb
    y = jnp.dot(merged.astype(BF16), wo_ref[...], preferred_element_type=F32)
    logits = lg_ref[...]
    lane = lax.broadcasted_iota(I32, (tm, LANES), 1).astype(F32)
    ninf = -jnp.inf
    big = float(LANES)

    def first_argmax(v):
        vmax = jnp.max(v, axis=1, keepdims=True)
        idx = jnp.min(jnp.where(v == vmax, lane, big), axis=1, keepdims=True)
        return vmax, idx

    gl = jnp.where(lane < N_GROUPS, logits, ninf)
    gmax, gidx = first_argmax(gl)
    p_top = 1.0 / jnp.sum(jnp.exp(gl - gmax), axis=1, keepdims=True)
    lo = N_GROUPS + EXPERTS_PER_GROUP * gidx
    el = jnp.where((lane >= lo) & (lane < lo + EXPERTS_PER_GROUP), logits, ninf)
    e1max, e1 = first_argmax(el)
    e2max, e2 = first_argmax(jnp.where(lane == e1, ninf, el))
    t = jnp.exp(e2max - e1max)
    w1 = 1.0 / (1.0 + t)
    w2 = t / (1.0 + t)
    col = lax.broadcasted_iota(I32, (tm, ROUTE_W), 1)
    rf_ref[...] = jnp.where(col == 0, p_top * w1, jnp.where(col == 1, p_top * w2, 0.0))

    x1id = e1 - N_GROUPS
    x2id = e2 - N_GROUPS
    hit1 = lane == x1id
    hit2 = lane == x2id
    onehot = jnp.where(hit1 | hit2, 1.0, 0.0)
    before = jnp.dot(tri_ref[...], onehot.astype(BF16), preferred_element_type=F32) + carry_ref[...]
    r1 = jnp.sum(jnp.where(hit1, before, 0.0), axis=1, keepdims=True)
    r2 = jnp.sum(jnp.where(hit2, before, 0.0), axis=1, keepdims=True)
    real_tile = jnp.where(step > 0, 1.0, 0.0)
    carry_ref[...] = carry_ref[...] + real_tile * jnp.sum(onehot, axis=0, keepdims=True)
    cnt_ref[...] = carry_ref[...]
    cols = jnp.where(lane == 0.0, x1id, jnp.where(lane == 1.0, x2id,
                                                  jnp.where(lane == 2.0, r1,
                                                            jnp.where(lane == 3.0, r2, 0.0))))
    rank_ref[0] = cols.T[0:ROUTE_W, :].astype(I32)

    gate1 = mod_ref[0, 2:3, :]
    shift2 = mod_ref[0, 3:4, :]
    scale2 = mod_ref[0, 4:5, :]
    x1 = x_ref[...] + gate1 * y
    x1_ref[...] = x1
    ms = jnp.mean(x1 * x1, axis=-1, keepdims=True)
    h2 = (x1 * lax.rsqrt(ms + NORM_EPS) * g2_ref[...]) * (1.0 + scale2) + shift2
    _store_row_tiles(h2_ref, h2)
    h2_hi = h2.astype(BF16)
    h2_lo = (h2 - h2_hi.astype(F32)).astype(BF16)
    hi_both = jnp.dot(h2_hi, wr2_ref[...], preferred_element_type=F32)
    next_logits = (hi_both[:, :LANES] + hi_both[:, LANES:]
                   + jnp.dot(h2_lo, wr2_ref[:, :LANES], preferred_element_type=F32)) + br_ref[...]

    lg_ref[...] = next_logits


def _merge_call(x2, o_a, o_b, sa, sb, mod, norm2_g, wa, wb, wo, wr2, br):
    nb = SEQ // TM_MERGE
    n_tiles = TOKENS // TM_MERGE
    cur = lambda i: jnp.minimum(i, n_tiles - 1)
    prev = lambda i: jnp.maximum(i - 1, 0)
    tok = lambda w: pl.BlockSpec((TM_MERGE, w), lambda i: (cur(i), 0))
    const2 = lambda r, c: pl.BlockSpec((r, c), lambda i: (0, 0))
    return pl.pallas_call(
        _merge_kernel,
        grid=(n_tiles + 1,),
        in_specs=[tok(D_MODEL), tok(GLA_V_W), tok(DIFF_V_W), tok(D_MODEL), tok(D_MODEL),
                  pl.BlockSpec((1, N_MOD, D_MODEL), lambda i: (cur(i) // nb, 0, 0)),
                  const2(1, D_MODEL),
                  const2(GLA_V_W, D_MODEL), const2(DIFF_V_W, D_MODEL), const2(D_MODEL, D_MODEL),
                  const2(D_MODEL, 2 * LANES), const2(1, LANES)],
        out_specs=(tok(D_MODEL), _row_tile_spec(TM_MERGE, lambda i: (cur(i), 0)),
                   pl.BlockSpec((TM_MERGE, ROUTE_W), lambda i: (prev(i), 0)),
                   pl.BlockSpec((1, ROUTE_W, TM_MERGE), lambda i: (prev(i), 0, 0)),
                   pl.BlockSpec((1, LANES), lambda i: (0, 0))),
        out_shape=(jax.ShapeDtypeStruct((TOKENS, D_MODEL), F32),
                   jax.ShapeDtypeStruct((TOKENS * ROW_TILE_S, LANES), F32),
                   jax.ShapeDtypeStruct((TOKENS, ROUTE_W), F32),
                   jax.ShapeDtypeStruct((n_tiles, ROUTE_W, TM_MERGE), I32),
                   jax.ShapeDtypeStruct((1, LANES), F32)),
        scratch_shapes=[pltpu.VMEM((1, LANES), F32), pltpu.VMEM((TM_MERGE, TM_MERGE), BF16),
                        pltpu.VMEM((TM_MERGE, LANES), F32)],
        compiler_params=_cparams(("arbitrary",)),
        name="merge",
    )(x2, o_a, o_b, sa, sb, mod, norm2_g, wa, wb, wo, wr2, br)


def _plan_kernel(rk_ref, cnt_ref, dest_ref, blk_ref):
    n_tiles, _, tm = rk_ref.shape
    cnt = cnt_ref[...]
    padded = jnp.floor((cnt + (MOE_BLK - 1)) * (1.0 / MOE_BLK)) * MOE_BLK
    lane = lax.broadcasted_iota(I32, (1, LANES), 1)
    seg_end = padded
    step = 1
    while step < N_EXPERTS:
        seg_end = seg_end + jnp.where(lane >= step, pltpu.roll(seg_end, step, 1), 0.0)
        step *= 2
    seg_start = seg_end - padded
    valid_end = seg_start + cnt

    ei = lax.broadcasted_iota(I32, (LANES, LANES), 0)
    ej = lax.broadcasted_iota(I32, (LANES, LANES), 1)

    def to_col(rowv):
        return jnp.sum(jnp.where(ei == ej, rowv, 0.0), axis=1, keepdims=True)

    start_col, end_col, valid_col = to_col(seg_start), to_col(seg_end), to_col(valid_end)

    nb_pad = blk_ref.shape[1]
    e_sub = lax.broadcasted_iota(I32, (LANES, nb_pad), 0)
    b_start = (lax.broadcasted_iota(I32, (1, nb_pad), 1) * MOE_BLK).astype(F32)
    ends_before = jnp.where((e_sub < N_EXPERTS) & (end_col <= b_start), 1.0, 0.0)
    block_e = jnp.minimum(jnp.sum(ends_before, axis=0, keepdims=True), N_EXPERTS - 1.0)
    block_valid_end = jnp.sum(jnp.where(e_sub.astype(F32) == block_e, valid_col, 0.0),
                              axis=0, keepdims=True)
    n_valid = jnp.clip(block_valid_end - b_start, 0.0, float(MOE_BLK))
    n_used = jnp.max(seg_end, axis=1, keepdims=True) * (1.0 / MOE_BLK)
    nonempty = (to_col(cnt) > 0.0) & (e_sub < N_EXPERTS)
    e_subf = e_sub.astype(F32)
    no_next = float(LANES)
    nxt = jnp.min(jnp.where(nonempty & (e_subf > block_e), e_subf, no_next), axis=0, keepdims=True)
    nxt = jnp.where(nxt == no_next, -1.0, nxt)
    seg_idx = jnp.sum(jnp.where(nonempty & (e_subf < block_e), 1.0, 0.0), axis=0, keepdims=True)
    slot = seg_idx - 2.0 * jnp.floor(seg_idx * 0.5)
    blk_ref[...] = jnp.concatenate(
        [block_e, n_valid, jnp.broadcast_to(n_used, (1, nb_pad)), nxt, slot,
         jnp.zeros((ROUTE_W - 5, nb_pad), F32)], axis=0).astype(I32)

    e_tok = lax.broadcasted_iota(I32, (LANES, tm), 0)

    def tile(t, carry):
        rk = rk_ref[t]
        d = [jnp.sum(jnp.where(e_tok == rk[k:k + 1], start_col, 0.0), axis=0, keepdims=True)
             + rk[k + 2:k + 3].astype(F32) for k in range(TOP_K)]
        dest_ref[t] = jnp.concatenate(d + [jnp.zeros((ROUTE_W - TOP_K, tm), F32)], axis=0).astype(I32)
        return carry

    lax.fori_loop(0, n_tiles, tile, 0)


def _plan_call(rank_rows, cnt):
    n_tiles = TOKENS // TM_MERGE
    nb_pad = -(-N_BLOCKS // LANES) * LANES
    dest, blk = pl.pallas_call(
        _plan_kernel,
        out_shape=(jax.ShapeDtypeStruct((n_tiles, ROUTE_W, TM_MERGE), I32),
                   jax.ShapeDtypeStruct((ROUTE_W, nb_pad), I32)),
        compiler_params=pltpu.CompilerParams(vmem_limit_bytes=VMEM_LIMIT),
        name="plan",
    )(rank_rows, cnt)
    dest1 = dest[:, 0, :].reshape(TOKENS)
    dest2 = dest[:, 1, :].reshape(TOKENS)
    tables = tuple(blk[r, :N_BLOCKS] for r in (0, 1, 3, 4))
    return dest1, dest2, tables, blk[2, :1]


def _sc_params():
    return pltpu.CompilerParams(use_tc_tiling_on_sc=True)


def _sc_mesh():
    return plsc.VectorSubcoreMesh(core_axis_name="core", subcore_axis_name="subcore")


def _sc_worker_base(per_worker):
    wid = lax.axis_index("subcore") * SC_CORES + lax.axis_index("core")
    return wid * per_worker


def _sc_scatter_rows(src, dest1, dest2, n_out):
    n = src.shape[0]
    per_worker = n // SC_WORKERS
    assert per_worker * SC_WORKERS == n and per_worker % SC_CHUNK == 0

    def body(src_hbm, d1_hbm, d2_hbm, out_hbm, idx1_v, idx2_v, rows_v, lsem, ssem1, ssem2):
        base = _sc_worker_base(per_worker)

        @pl.loop(0, per_worker // SC_CHUNK)
        def _(j):
            start = pl.multiple_of(base + j * SC_CHUNK, SC_CHUNK)
            rows_in = pltpu.async_copy(src_hbm.at[pl.ds(start, SC_CHUNK)], rows_v, lsem)
            pltpu.sync_copy(d1_hbm.at[pl.ds(start, SC_CHUNK)], idx1_v)
            pltpu.sync_copy(d2_hbm.at[pl.ds(start, SC_CHUNK)], idx2_v)
            rows_in.wait()
            out1 = pltpu.async_copy(rows_v, out_hbm.at[idx1_v], ssem1)
            out2 = pltpu.async_copy(rows_v, out_hbm.at[idx2_v], ssem2)
            out1.wait()
            out2.wait()

    return pl.kernel(
        body,
        out_type=jax.ShapeDtypeStruct((n_out, ROW_TILE_S, LANES), F32),
        mesh=_sc_mesh(),
        scratch_types=[pltpu.VMEM((SC_CHUNK,), I32), pltpu.VMEM((SC_CHUNK,), I32),
                       pltpu.VMEM((SC_CHUNK, ROW_TILE_S, LANES), F32),
                       pltpu.SemaphoreType.DMA, pltpu.SemaphoreType.DMA, pltpu.SemaphoreType.DMA],
        compiler_params=_sc_params(),
        name="sc_dispatch",
    )(src, dest1, dest2)


def _sc_gather_rows(table, idx):
    n = idx.shape[0]
    per_worker = n // SC_WORKERS
    half = SC_CHUNK // 2
    assert per_worker * SC_WORKERS == n and per_worker % SC_CHUNK == 0

    def body(table_hbm, idx_hbm, out_hbm, idx_v, rows_a, rows_b, gsem_a, gsem_b, wsem_a, wsem_b):
        base = pl.multiple_of(_sc_worker_base(per_worker), SC_CHUNK)
        pltpu.sync_copy(idx_hbm.at[pl.ds(base, per_worker)], idx_v)

        @pl.loop(0, per_worker // SC_CHUNK)
        def _(j):
            off_a = pl.multiple_of(j * SC_CHUNK, SC_CHUNK)
            off_b = pl.multiple_of(j * SC_CHUNK + half, half)
            ga = pltpu.async_copy(table_hbm.at[idx_v.at[pl.ds(off_a, half)]], rows_a, gsem_a)
            gb = pltpu.async_copy(table_hbm.at[idx_v.at[pl.ds(off_b, half)]], rows_b, gsem_b)
            ga.wait()
            wa = pltpu.async_copy(rows_a, out_hbm.at[pl.ds(base + off_a, half)], wsem_a)
            gb.wait()
            wb = pltpu.async_copy(rows_b, out_hbm.at[pl.ds(base + off_b, half)], wsem_b)
            wa.wait()
            wb.wait()

    return pl.kernel(
        body,
        out_type=jax.ShapeDtypeStruct((n, ROW_TILE_S, LANES), F32),
        mesh=_sc_mesh(),
        scratch_types=[pltpu.VMEM((per_worker,), I32),
                       pltpu.VMEM((half, ROW_TILE_S, LANES), F32),
                       pltpu.VMEM((half, ROW_TILE_S, LANES), F32),
                       pltpu.SemaphoreType.DMA, pltpu.SemaphoreType.DMA,
                       pltpu.SemaphoreType.DMA, pltpu.SemaphoreType.DMA],
        compiler_params=_sc_params(),
        name="sc_gather",
    )(table, idx)


def _weight_fetch(w_hbm, stage, sem, e, slot):
    return [pltpu.make_async_copy(w.at[e], st.at[slot], sem.at[slot]) for w, st in zip(w_hbm, stage)]


def _expert_kernel(be_ref, nv_ref, nx_ref, sl_ref, nu_ref, xs_ref, w1_hbm, w3_hbm, w2_hbm, y_ref,
                   w1s, w3s, w2s, w1b, w3b, w2b, sem):
    i = pl.program_id(0)
    e = be_ref[i]
    used = i < nu_ref[0]
    first = (i == 0) | (e != be_ref[jnp.maximum(i - 1, 0)])
    w_hbm = (w1_hbm, w3_hbm, w2_hbm)
    stage = (w1s, w3s, w2s)

    @pl.when(used & first)
    def _():
        slot = sl_ref[i]

        @pl.when(i == 0)
        def _():
            for cp in _weight_fetch(w_hbm, stage, sem, e, slot):
                cp.start()

        for cp in _weight_fetch(w_hbm, stage, sem, e, slot):
            cp.wait()
        w1b[...] = w1s[slot].astype(BF16)
        w3b[...] = w3s[slot].astype(BF16)
        w2b[...] = w2s[slot].astype(BF16)

        @pl.when(nx_ref[i] >= 0)
        def _():
            for cp in _weight_fetch(w_hbm, stage, sem, nx_ref[i], 1 - slot):
                cp.start()

    n_parts = MOE_BLK // MOE_SUB

    def load_x(part):
        row0 = part * MOE_SUB
        row = lax.broadcasted_iota(I32, (MOE_SUB, D_MODEL), 0) + row0
        return jnp.where(row < nv_ref[i], _load_row_tiles(xs_ref, row0, MOE_SUB), 0.0).astype(BF16)

    def zero_y(part):
        y_ref[pl.ds(part * MOE_SUB * ROW_TILE_S, MOE_SUB * ROW_TILE_S), :] = jnp.zeros(
            (MOE_SUB * ROW_TILE_S, LANES), F32)

    def run(parts):
        hmid = []
        for part in parts:
            xb = load_x(part)
            a = jnp.dot(xb, w1b[...], preferred_element_type=F32)
            g = jnp.dot(xb, w3b[...], preferred_element_type=F32)
            hmid.append(((a * _sigmoid(a)) * g).astype(BF16))
        for part, hm in zip(parts, hmid):
            _store_row_tiles(y_ref, jnp.dot(hm, w2b[...], preferred_element_type=F32),
                             part * MOE_SUB)

    n_live = jnp.where(used, (nv_ref[i] + (MOE_SUB - 1)) // MOE_SUB, 0)
    for k in range(n_parts + 1):
        @pl.when(n_live == k)
        def _(k=k):
            if k:
                run(list(range(k)))
            for part in range(k, n_parts):
                zero_y(part)


def _expert_call(tables, n_used, xs, w1, w3, w2):
    n_tab = len(tables)
    grid_spec = pltpu.PrefetchScalarGridSpec(
        num_scalar_prefetch=n_tab + 1,
        grid=(N_BLOCKS,),
        in_specs=[_row_tile_spec(MOE_BLK, lambda i, *pf: (jnp.minimum(i, pf[n_tab][0] - 1), 0)),
                  pl.BlockSpec(memory_space=pl.ANY),
                  pl.BlockSpec(memory_space=pl.ANY),
                  pl.BlockSpec(memory_space=pl.ANY)],
        out_specs=_row_tile_spec(MOE_BLK, lambda i, *pf: (i, 0)),
        scratch_shapes=[pltpu.VMEM((2, D_MODEL, D_EXPERT), F32),
                        pltpu.VMEM((2, D_MODEL, D_EXPERT), F32),
                        pltpu.VMEM((2, D_EXPERT, D_MODEL), F32),
                        pltpu.VMEM((D_MODEL, D_EXPERT), BF16),
                        pltpu.VMEM((D_MODEL, D_EXPERT), BF16),
                        pltpu.VMEM((D_EXPERT, D_MODEL), BF16),
                        pltpu.SemaphoreType.DMA((2,))],
    )
    return pl.pallas_call(
        _expert_kernel,
        grid_spec=grid_spec,
        out_shape=jax.ShapeDtypeStruct((N_BLOCKS * MOE_BLK * ROW_TILE_S, LANES), F32),
        compiler_params=_cparams(("arbitrary",)),
        name="experts",
    )(*tables, n_used, xs, w1, w3, w2)


def _combine_rows_kernel(x1_ref, rf_ref, mod_ref, ya_ref, yb_ref, *rest):
    o_ref = rest[-1]
    rf = rf_ref[...]
    gate2 = mod_ref[0, 5:6, :]
    moe = rf[:, 0:1] * _load_row_tiles(ya_ref) + rf[:, 1:2] * _load_row_tiles(yb_ref)
    o_ref[...] = x1_ref[...] + gate2 * moe


def _combine_rows_call(x1, rf, mod, yg, part, prev=None):
    nb = SEQ // TM_ROW
    n_half = TOKENS // TM_ROW // 2
    t0 = part * n_half
    in_specs = [pl.BlockSpec((TM_ROW, D_MODEL), lambda i: (t0 + i, 0)),
                pl.BlockSpec((TM_ROW, ROUTE_W), lambda i: (t0 + i, 0)),
                pl.BlockSpec((1, N_MOD, D_MODEL), lambda i: ((t0 + i) // nb, 0, 0)),
                _row_tile_spec(TM_ROW, lambda i: (i, 0)),
                _row_tile_spec(TM_ROW, lambda i: (i + n_half, 0))]
    args = [x1, rf, mod, yg, yg]
    aliases = {}
    if prev is not None:
        in_specs.append(pl.BlockSpec(memory_space=pl.ANY))
        args.append(prev)
        aliases = {len(args) - 1: 0}
    return pl.pallas_call(
        _combine_rows_kernel,
        grid=(n_half,),
        in_specs=in_specs,
        out_specs=pl.BlockSpec((TM_ROW, D_MODEL), lambda i: (t0 + i, 0)),
        out_shape=jax.ShapeDtypeStruct((TOKENS, D_MODEL), F32),
        input_output_aliases=aliases,
        compiler_params=_cparams(("parallel",)),
        name="combine",
    )(*args)


def kernel(x, c, positions, ada_w, ada_b, norm1_g, norm2_g, w_in, gla_alpha_up, gla_alpha_b,
           gla_out_g, diff_q_g, diff_k_g, diff_lq1, diff_lk1, diff_lq2, diff_lk2, diff_out_g,
           w_branch_a, w_branch_b, w_out, router_group_w, router_group_b, router_expert_w,
           router_expert_b, expert_w1, expert_w3, expert_w2):
    assert x.shape == (BATCH, SEQ, D_MODEL) and ada_w.shape[0] == 1
    x2 = x.reshape(TOKENS, D_MODEL)

    w_in_p = _wprep_call(w_in[0])
    au_hi = gla_alpha_up[0].astype(BF16)
    au_lo = (gla_alpha_up[0] - au_hi.astype(F32)).astype(BF16)
    au_stack = jnp.concatenate(
        [au_hi, au_hi, au_lo, jnp.zeros((LR_PAD - 3 * GLA_GATE_RANK, GLA_QK_W), BF16)], axis=0)
    gid = jnp.arange(NORM_BLK) // DIFF_DH
    bd = jnp.where(gid[:, None] == gid[None, :], 1.0 / DIFF_DH, 0.0).astype(BF16)
    qg_row = jnp.tile(diff_q_g[0], DIFF_QK_W // DIFF_DH).reshape(1, DIFF_QK_W)
    kg_row = jnp.tile(diff_k_g[0], DIFF_QK_W // DIFF_DH).reshape(1, DIFF_QK_W)
    inv_freq = ROPE_THETA ** (-jnp.arange(ROPE_HALF, dtype=F32) / ROPE_HALF)
    invf64 = jnp.concatenate([inv_freq, inv_freq, jnp.zeros((DIFF_DH - ROPE_DIM,), F32)])
    invf_row = jnp.tile(invf64, LANES // DIFF_DH).reshape(1, LANES)
    pos_rows = positions.reshape(TOKENS // TM_IN, 1, TM_IN)
    lam_p = jnp.concatenate([diff_lq1, diff_lk1, diff_lq2, diff_lk2], axis=0)
    wr = (jnp.zeros((D_MODEL, LANES), F32)
          .at[:, :N_GROUPS].set(router_group_w[0])
          .at[:, N_GROUPS:N_GROUPS + N_EXPERTS].set(router_expert_w[0]))
    wr_hi = wr.astype(BF16)
    wr2 = jnp.concatenate([wr_hi, (wr - wr_hi.astype(F32)).astype(BF16)], axis=1)
    br = (jnp.zeros((1, LANES), F32)
          .at[0, :N_GROUPS].set(router_group_b[0])
          .at[0, N_GROUPS:N_GROUPS + N_EXPERTS].set(router_expert_b[0]))

    mod = _mod_call(c, ada_w[0], ada_b[0])
    gqk, gv, gr, lr, qT, kk, vT, sa, sb = _in_call(
        x2, mod, norm1_g, w_in_p, bd, qg_row, kg_row, pos_rows, invf_row)
    o_a = _gla_call(gqk, gv, gr, lr, au_stack, gla_alpha_b, gla_out_g)
    o_b = _attn_call(qT, kk, vT, lam_p, diff_out_g.reshape(DIFF_DV, 1))
    x1, h2, rf, rank, cnt = _merge_call(
        x2, o_a, o_b, sa, sb, mod, norm2_g, w_branch_a[0].astype(BF16),
        w_branch_b[0].astype(BF16), w_out[0].astype(BF16), wr2, br)
    dest1, dest2, tables, n_used = _plan_call(rank, cnt)

    xs = _sc_scatter_rows(_as_row_tiles(h2), dest1, dest2, N_BLOCKS * MOE_BLK)
    y = _expert_call(tables, n_used, _as_2d(xs), expert_w1[0], expert_w3[0], expert_w2[0])
    y3 = _as_row_tiles(y)
    half = TOKENS // 2
    out = None
    for part in range(2):
        tok = slice(part * half, (part + 1) * half)
        yg = _sc_gather_rows(y3, jnp.concatenate([dest1[tok], dest2[tok]]))
        out = _combine_rows_call(x1, rf, mod, _as_2d(yg), part, out)
    return out.reshape(BATCH, SEQ, D_MODEL)
```

```python
import math

import jax
import jax.numpy as jnp
from jax import lax
from jax.experimental import pallas as pl
from jax.experimental.pallas import tpu as pltpu
from jax.experimental.pallas import tpu_sc as plsc

F32 = jnp.float32
BF16 = jnp.bfloat16
I32 = jnp.int32

D_MODEL = 1024
BATCH = 4
SEQ = 4096
TOKENS = BATCH * SEQ
N_MOD = 6
NORM_EPS = 1e-6

GLA_HEADS = 4
GLA_DK = 64
GLA_DV = 128
GLA_GATE_RANK = 16
GLA_GATE_TAU = 16.0
GLA_CHUNK = 64
GLA_QK_W = GLA_HEADS * GLA_DK
GLA_V_W = GLA_HEADS * GLA_DV

DIFF_HEADS = 4
DIFF_DH = 64
DIFF_DV = 2 * DIFF_DH
DIFF_QK_W = DIFF_HEADS * 2 * DIFF_DH
DIFF_V_W = DIFF_HEADS * DIFF_DV
ROPE_THETA = 500000.0
ROPE_DIM = DIFF_DH // 4
ROPE_HALF = ROPE_DIM // 2
NEG_INF = -1e30
LAMBDA_INIT = 0.8 - 0.6 * 1.0

N_GROUPS = 4
EXPERTS_PER_GROUP = 8
N_EXPERTS = N_GROUPS * EXPERTS_PER_GROUP
TOP_K = 2
D_EXPERT = 512

LANES = 128
SUBLANES = 8
ROW_TILE_S = D_MODEL // LANES
SC_CORES = 2
SC_SUBCORES = 16
SC_WORKERS = SC_CORES * SC_SUBCORES
SC_CHUNK = 64
LR_PAD = LANES
D_IN_PAD = 2 * GLA_QK_W + 2 * GLA_V_W + 2 * DIFF_QK_W + DIFF_V_W + 2 * D_MODEL + LR_PAD
IN_COLS = {}
_col = 0
for _name, _width in (("gla_qk", 2 * GLA_QK_W), ("gla_v", GLA_V_W), ("gla_r", GLA_V_W),
                      ("diff_q", DIFF_QK_W), ("diff_k", DIFF_QK_W), ("diff_v", DIFF_V_W),
                      ("gate_a", D_MODEL), ("gate_b", D_MODEL), ("gla_lr", LR_PAD)):
    IN_COLS[_name] = (_col, _col + _width)
    _col += _width
assert _col == D_IN_PAD
WPREP_COLS = 512
WPREP_LR_STEP = (D_IN_PAD - LR_PAD) // WPREP_COLS
D_IN_ALLOC = (WPREP_LR_STEP + 1) * WPREP_COLS

TM_IN = 512
TQ = 512
V_ROWS = DIFF_DV + 16
NORM_BLK = 256
ATT_SUB = 32
Q_SCALE = DIFF_DH ** -0.5 * math.log2(math.e)
N_KV = SEQ // TQ
TT_GLA = 512
TM_MERGE = 512
TM_ROW = 512
MOE_BLK = 512
MOE_SUB = 256
N_BLOCKS = (TOKENS * TOP_K + N_EXPERTS * (MOE_BLK - 1) + MOE_BLK - 1) // MOE_BLK
ROUTE_W = 8

VMEM_LIMIT = 56 * 1024 * 1024


def _cparams(sem):
    return pltpu.CompilerParams(dimension_semantics=sem, vmem_limit_bytes=VMEM_LIMIT)


def _sigmoid(x):
    return 1.0 / (1.0 + jnp.exp(-x))


def _row_tile_spec(rows, index_map):
    return pl.BlockSpec((rows * ROW_TILE_S, LANES), index_map)


def _as_row_tiles(a2d):
    return a2d.reshape(a2d.shape[0] // ROW_TILE_S, ROW_TILE_S, LANES)


def _as_2d(a3d):
    return a3d.reshape(a3d.shape[0] * ROW_TILE_S, LANES)


def _store_row_tiles(ref, val, row0=0):
    for s in range(ROW_TILE_S):
        ref[pl.ds(row0 * ROW_TILE_S + s, val.shape[0], stride=ROW_TILE_S), :] = (
            val[:, s * LANES:(s + 1) * LANES])


def _load_row_tiles(ref, row0=0, rows=None):
    rows = ref.shape[0] // ROW_TILE_S if rows is None else rows
    return jnp.concatenate(
        [ref[pl.ds(row0 * ROW_TILE_S + s, rows, stride=ROW_TILE_S), :] for s in range(ROW_TILE_S)],
        axis=1)


def _mod_kernel(c_ref, w_ref, b_ref, o_ref):
    c = c_ref[...]
    cpad = jnp.concatenate([c, jnp.zeros((LANES - BATCH, c.shape[1]), F32)], axis=0)
    ct = cpad.T
    ca = ct * _sigmoid(ct)
    w = w_ref[...]
    rows = []
    for b in range(BATCH):
        rows.append(jnp.sum(w * ca[:, b:b + 1], axis=0, keepdims=True) + b_ref[...])
    rows.append(jnp.zeros((8 - BATCH, w.shape[1]), F32))
    o_ref[...] = jnp.concatenate(rows, axis=0)


def _mod_call(c, ada_w, ada_b):
    tn = D_MODEL
    out = pl.pallas_call(
        _mod_kernel,
        grid=(N_MOD,),
        in_specs=[
            pl.BlockSpec((BATCH, D_MODEL), lambda j: (0, 0)),
            pl.BlockSpec((D_MODEL, tn), lambda j: (0, j)),
            pl.BlockSpec((1, tn), lambda j: (0, j)),
        ],
        out_specs=pl.BlockSpec((8, tn), lambda j: (0, j)),
        out_shape=jax.ShapeDtypeStruct((8, N_MOD * D_MODEL), F32),
        compiler_params=_cparams(("arbitrary",)),
        name="mod",
    )(c, ada_w, ada_b.reshape(1, N_MOD * D_MODEL))
    return out[:BATCH].reshape(BATCH, N_MOD, D_MODEL)


def _wprep_kernel(w_ref, o_ref):
    j = pl.program_id(0)
    t = w_ref[...].T
    lane = lax.broadcasted_iota(I32, t.shape, 1)
    @pl.when(j < WPREP_LR_STEP)
    def _():
        o_ref[...] = t.astype(BF16)

    @pl.when(j == WPREP_LR_STEP)
    def _():
        rep = jnp.where(lane < GLA_GATE_RANK, t,
                        jnp.where(lane < 2 * GLA_GATE_RANK, pltpu.roll(t, GLA_GATE_RANK, 1),
                                  jnp.where(lane < 3 * GLA_GATE_RANK,
                                            pltpu.roll(t, 2 * GLA_GATE_RANK, 1), 0.0)))
        o_ref[...] = rep.astype(BF16)


def _wprep_call(w_in0):
    w_t = w_in0.T
    lr0 = 2 * GLA_QK_W + 2 * GLA_V_W
    assert lr0 % WPREP_COLS == 0 and (D_IN_PAD - LR_PAD) % WPREP_COLS == 0

    def src_row(j):
        return jnp.where(j < lr0 // WPREP_COLS, j * WPREP_COLS,
                         jnp.where(j < WPREP_LR_STEP, j * WPREP_COLS + GLA_GATE_RANK, lr0))

    return pl.pallas_call(
        _wprep_kernel,
        grid=(WPREP_LR_STEP + 1,),
        in_specs=[pl.BlockSpec((pl.Element(WPREP_COLS), pl.Element(D_MODEL)),
                               lambda j: (pl.multiple_of(src_row(j), SUBLANES), 0))],
        out_specs=pl.BlockSpec((D_MODEL, WPREP_COLS), lambda j: (0, j)),
        out_shape=jax.ShapeDtypeStruct((D_MODEL, D_IN_ALLOC), BF16),
        compiler_params=_cparams(("parallel",)),
        name="w_prep",
    )(w_t)


def _in_kernel(x_ref, mod_ref, g1_ref, w_ref, bd_ref, qg_ref, kg_ref, pos_ref, invf_ref,
               gqk_ref, gv_ref, gr_ref, lr_ref, qT_ref, k_ref, vT_ref, sa_ref, sb_ref,
               cos_scr, sin_scr):
    pos_col = jnp.broadcast_to(pos_ref[0].astype(F32), (LANES, pos_ref.shape[2])).T[:, 0:1]
    ang = pos_col * invf_ref[...]
    cos_scr[...] = jnp.cos(ang)
    sin_scr[...] = jnp.sin(ang)

    x = x_ref[...]
    shift1 = mod_ref[0, 0:1, :]
    scale1 = mod_ref[0, 1:2, :]
    ms = jnp.mean(x * x, axis=-1, keepdims=True)
    h = (x * lax.rsqrt(ms + NORM_EPS) * g1_ref[...]) * (1.0 + scale1) + shift1
    hb = h.astype(BF16)

    def proj(name):
        c0, c1 = IN_COLS[name]
        return jnp.dot(hb, w_ref[:, c0:c1], preferred_element_type=F32)

    tm = x.shape[0]
    cos4 = jnp.concatenate([cos_scr[...]] * DIFF_HEADS, axis=1)
    sin4 = jnp.concatenate([sin_scr[...]] * DIFF_HEADS, axis=1)
    lane = lax.broadcasted_iota(I32, (tm, DIFF_QK_W), 1)
    first_half = (lane % DIFF_DH) < ROPE_HALF
    bd = bd_ref[...]

    def norm_rope(t, gain_row):
        t2 = t * t
        hi = t2.astype(BF16)
        lo = (t2 - hi.astype(F32)).astype(BF16)
        hw = bd.shape[0]
        gms = jnp.concatenate(
            [jnp.dot(hi[:, c0:c0 + hw], bd, preferred_element_type=F32)
             + jnp.dot(lo[:, c0:c0 + hw], bd, preferred_element_type=F32)
             for c0 in range(0, DIFF_QK_W, hw)], axis=1)
        t = t * lax.rsqrt(gms + NORM_EPS) * gain_row
        nxt = pltpu.roll(t, DIFF_QK_W - ROPE_HALF, 1)
        prv = pltpu.roll(t, ROPE_HALF, 1)
        return t * cos4 + jnp.where(first_half, -nxt, prv) * sin4

    dq_raw = proj("diff_q")
    dk_raw = proj("diff_k")
    gqk_ref[...] = proj("gla_qk").astype(BF16)
    dq = norm_rope(dq_raw, qg_ref[...]) * Q_SCALE
    gv_ref[...] = proj("gla_v").astype(BF16)
    dk = norm_rope(dk_raw, kg_ref[...])
    dv = proj("diff_v")
    for hd in range(DIFF_HEADS):
        sl = slice(hd * LANES, (hd + 1) * LANES)
        qT_ref[0, hd, 0] = dq[:, sl].T.astype(BF16)
        k_ref[0, hd, 0] = dk[:, sl].astype(BF16)
    gr_ref[...] = proj("gla_r").astype(BF16)
    for hd in range(DIFF_HEADS):
        sl = slice(hd * LANES, (hd + 1) * LANES)
        vT_ref[0, hd, 0] = jnp.concatenate(
            [dv[:, sl].T, jnp.ones((V_ROWS - DIFF_DV, tm), F32)], axis=0).astype(BF16)
    sa_ref[...] = _sigmoid(proj("gate_a")).astype(BF16)
    sb_ref[...] = _sigmoid(proj("gate_b")).astype(BF16)
    lr_ref[...] = proj("gla_lr")


def _in_call(x2, mod, norm1_g, w_in_p, bd, qg_row, kg_row, pos_rows, invf_row):
    nb = SEQ // TM_IN
    tok_spec = lambda w: pl.BlockSpec((TM_IN, w), lambda i: (i, 0))
    const2 = lambda r, c: pl.BlockSpec((r, c), lambda i: (0, 0))
    out_shapes = (
        jax.ShapeDtypeStruct((TOKENS, 2 * GLA_QK_W), BF16),
        jax.ShapeDtypeStruct((TOKENS, GLA_V_W), BF16),
        jax.ShapeDtypeStruct((TOKENS, GLA_V_W), BF16),
        jax.ShapeDtypeStruct((TOKENS, LR_PAD), F32),
        jax.ShapeDtypeStruct((BATCH, DIFF_HEADS, N_KV, LANES, TQ), BF16),
        jax.ShapeDtypeStruct((BATCH, DIFF_HEADS, N_KV, TQ, LANES), BF16),
        jax.ShapeDtypeStruct((BATCH, DIFF_HEADS, N_KV, V_ROWS, TQ), BF16),
        jax.ShapeDtypeStruct((TOKENS, D_MODEL), BF16),
        jax.ShapeDtypeStruct((TOKENS, D_MODEL), BF16),
    )
    out_specs = (
        tok_spec(2 * GLA_QK_W), tok_spec(GLA_V_W), tok_spec(GLA_V_W), tok_spec(LR_PAD),
        pl.BlockSpec((1, DIFF_HEADS, 1, LANES, TM_IN), lambda i: (i // nb, 0, i % nb, 0, 0)),
        pl.BlockSpec((1, DIFF_HEADS, 1, TM_IN, LANES), lambda i: (i // nb, 0, i % nb, 0, 0)),
        pl.BlockSpec((1, DIFF_HEADS, 1, V_ROWS, TM_IN), lambda i: (i // nb, 0, i % nb, 0, 0)),
        tok_spec(D_MODEL), tok_spec(D_MODEL),
    )
    return pl.pallas_call(
        _in_kernel,
        grid=(TOKENS // TM_IN,),
        in_specs=[
            tok_spec(D_MODEL),
            pl.BlockSpec((1, N_MOD, D_MODEL), lambda i: (i // nb, 0, 0)),
            const2(1, D_MODEL),
            pl.BlockSpec((D_MODEL, D_IN_ALLOC), lambda i: (0, 0), pipeline_mode=pl.Buffered(1)),
            const2(NORM_BLK, NORM_BLK),
            const2(1, DIFF_QK_W), const2(1, DIFF_QK_W),
            pl.BlockSpec((1, 1, TM_IN), lambda i: (i, 0, 0)),
            const2(1, LANES),
        ],
        out_specs=out_specs,
        out_shape=out_shapes,
        scratch_shapes=[pltpu.VMEM((TM_IN, LANES), F32), pltpu.VMEM((TM_IN, LANES), F32)],
        compiler_params=_cparams(("parallel",)),
        name="in_proj",
    )(x2, mod, norm1_g, w_in_p, bd, qg_row, kg_row, pos_rows, invf_row)


def _gla_kernel(qk_ref, v_ref, r_ref, lr_ref, au_ref, ab_ref, og_ref, o_ref,
                state_ref, oacc_ref, snew_ref):
    tt = qk_ref.shape[0]
    n_chunks = tt // GLA_CHUNK

    @pl.when(pl.program_id(1) == 0)
    def _():
        state_ref[...] = jnp.zeros_like(state_ref)

    lr = lr_ref[...]
    lr_hi = lr.astype(BF16)
    lr_lo = (lr - lr_hi.astype(F32)).astype(BF16)
    lr_lane = lax.broadcasted_iota(I32, lr.shape, 1)
    in_lo_group = (lr_lane >= GLA_GATE_RANK) & (lr_lane < 2 * GLA_GATE_RANK)
    stacked = jnp.where(in_lo_group, lr_lo, lr_hi)
    z = jnp.dot(stacked, au_ref[...], preferred_element_type=F32) + ab_ref[...]
    g = (jnp.minimum(z, 0.0) - jnp.log(1.0 + jnp.exp(-jnp.abs(z)))) * (1.0 / GLA_GATE_TAU)

    row = lax.broadcasted_iota(I32, (tt, GLA_QK_W), 0) % GLA_CHUNK
    b = g
    step = 1
    while step < GLA_CHUNK:
        b = b + jnp.where(row >= step, pltpu.roll(b, step, 0), 0.0)
        step *= 2

    b_last_rows = [b[c * GLA_CHUNK + GLA_CHUNK - 1:(c + 1) * GLA_CHUNK, :] for c in range(n_chunks)]
    b_last = jnp.concatenate(
        [jnp.broadcast_to(bl, (GLA_CHUNK, GLA_QK_W)) for bl in b_last_rows], axis=0)

    qk = qk_ref[...].astype(F32)
    q = qk[:, :GLA_QK_W] * (GLA_DK ** -0.5)
    k = qk[:, GLA_QK_W:]
    q_in = (q * jnp.exp(b)).astype(BF16)
    k_in = (k * jnp.exp(-b)).astype(BF16)
    k_dec = (k * jnp.exp(b_last - b)).astype(BF16)

    ci = lax.broadcasted_iota(I32, (GLA_CHUNK, GLA_CHUNK), 0)
    cj = lax.broadcasted_iota(I32, (GLA_CHUNK, GLA_CHUNK), 1)
    causal = ci >= cj

    pairs = [(c, hd) for c in range(n_chunks) for hd in range(GLA_HEADS)]

    def rows(c):
        return slice(c * GLA_CHUNK, (c + 1) * GLA_CHUNK)

    def kcols(hd):
        return slice(hd * GLA_DK, (hd + 1) * GLA_DK)

    def vcols(hd):
        return slice(hd * GLA_DV, (hd + 1) * GLA_DV)

    att = {}
    for c, hd in pairs:
        a = lax.dot_general(q_in[rows(c), kcols(hd)], k_in[rows(c), kcols(hd)],
                            (((1,), (1,)), ((), ())), preferred_element_type=F32)
        att[c, hd] = jnp.where(causal, a, 0.0).astype(BF16)
    for c, hd in pairs:
        oacc_ref[rows(c), vcols(hd)] = jnp.dot(att[c, hd], v_ref[rows(c), vcols(hd)],
                                               preferred_element_type=F32)
    for c, hd in pairs:
        snew_ref[c * GLA_HEADS + hd] = lax.dot_general(
            k_dec[rows(c), kcols(hd)], v_ref[rows(c), vcols(hd)],
            (((0,), (0,)), ((), ())), preferred_element_type=F32)

    decay_rows = jnp.exp(jnp.concatenate(
        b_last_rows + [jnp.zeros((LANES - n_chunks, GLA_QK_W), F32)], axis=0))
    decay_cols = decay_rows.T
    states = [state_ref[hd] for hd in range(GLA_HEADS)]
    for c in range(n_chunks):
        for hd in range(GLA_HEADS):
            s_prev = states[hd]
            oacc_ref[rows(c), vcols(hd)] += jnp.dot(
                q_in[rows(c), kcols(hd)], s_prev.astype(BF16), preferred_element_type=F32)
            dcol = decay_cols[kcols(hd), c:c + 1]
            states[hd] = s_prev * dcol + snew_ref[c * GLA_HEADS + hd]
    for hd in range(GLA_HEADS):
        state_ref[hd] = states[hd]

    for hd in range(GLA_HEADS):
        vs = slice(hd * GLA_DV, (hd + 1) * GLA_DV)
        oh = oacc_ref[:, vs]
        ms = jnp.mean(oh * oh, axis=-1, keepdims=True)
        y = oh * lax.rsqrt(ms + NORM_EPS) * og_ref[...]
        r = r_ref[:, vs].astype(F32)
        o_ref[:, vs] = (y * (r * _sigmoid(r))).astype(BF16)


def _gla_call(gqk, gv, gr, lr, au_stack, ab_row, og_row):
    nt = SEQ // TT_GLA
    tok = lambda w: pl.BlockSpec((TT_GLA, w), lambda b, t: (b * nt + t, 0))
    const2 = lambda r, c: pl.BlockSpec((r, c), lambda b, t: (0, 0))
    return pl.pallas_call(
        _gla_kernel,
        grid=(BATCH, nt),
        in_specs=[tok(2 * GLA_QK_W), tok(GLA_V_W), tok(GLA_V_W), tok(LR_PAD),
                  const2(LR_PAD, GLA_QK_W), const2(1, GLA_QK_W), const2(1, GLA_DV)],
        out_specs=tok(GLA_V_W),
        out_shape=jax.ShapeDtypeStruct((TOKENS, GLA_V_W), BF16),
        scratch_shapes=[pltpu.VMEM((GLA_HEADS, GLA_DK, GLA_DV), F32),
                        pltpu.VMEM((TT_GLA, GLA_V_W), F32),
                        pltpu.VMEM((TT_GLA // GLA_CHUNK * GLA_HEADS, GLA_DK, GLA_DV), F32)],
        compiler_params=_cparams(("parallel", "arbitrary")),
        name="gla",
    )(gqk, gv, gr, lr, au_stack, ab_row, og_row)


def _attn_kernel(qT_ref, k_ref, vT_ref, lam_ref, og_ref, o_ref,
                 q_scr, s_scr, p_scr, acc_scr, m_scr):
    _attn_load_q(qT_ref, q_scr, 0, 0)
    for c in range(2):
        s_scr[c] = jnp.dot(k_ref[0, 0, 0], q_scr[0, c], preferred_element_type=F32)

    def q_block(i, carry):
        _attn_q_block(i, qT_ref, k_ref, vT_ref, lam_ref, og_ref, o_ref,
                      q_scr, s_scr, p_scr, acc_scr, m_scr)
        return carry

    lax.fori_loop(0, N_KV, q_block, 0)


def _attn_load_q(qT_ref, q_scr, i, slot):
    qT = qT_ref[0, 0, i]
    rowq = lax.broadcasted_iota(I32, qT.shape, 0)
    zero = jnp.zeros_like(qT)
    q_scr[slot, 0] = jnp.where(rowq < DIFF_DH, qT, zero)
    q_scr[slot, 1] = jnp.where(rowq >= DIFF_DH, qT, zero)


def _attn_q_block(i, qT_ref, k_ref, vT_ref, lam_ref, og_ref, o_ref,
                  q_scr, s_scr, p_scr, acc_scr, m_scr):
    slot = i % 2
    n_sub = TQ // ATT_SUB

    def fold8(t, op):
        return op(t.reshape(t.shape[0] // SUBLANES, SUBLANES, TQ), axis=0)

    def scores(c, j):
        s_scr[c] = jnp.dot(k_ref[0, 0, j], q_scr[slot, c], preferred_element_type=F32)

    def load_s(c, r, mask, j=None):
        s = s_scr[c, r * ATT_SUB:(r + 1) * ATT_SUB, :]
        if mask is not None:
            key_i = lax.broadcasted_iota(I32, (ATT_SUB, TQ), 0) + r * ATT_SUB
            qry_i = lax.broadcasted_iota(I32, (ATT_SUB, TQ), 1)
            keep = key_i <= qry_i
            if mask == "maybe":
                keep = jnp.logical_or(keep, j < i)
            s = jnp.where(keep, s, NEG_INF)
        return s

    def block_max(c, mask, j=None):
        m8 = fold8(load_s(c, 0, mask, j), jnp.max)
        for r in range(1, n_sub):
            m8 = jnp.maximum(m8, fold8(load_s(c, r, mask, j), jnp.max))
        return jnp.max(m8, axis=0, keepdims=True)

    def exp_to_p(c, shift, mask):
        for r in range(n_sub):
            p = jnp.exp2(load_s(c, r, mask) - shift)
            p_scr[c, r * ATT_SUB:(r + 1) * ATT_SUB, :] = p.astype(BF16)

    def p_times_v(c, j):
        return jnp.dot(vT_ref[0, 0, j], p_scr[c], preferred_element_type=F32)

    acc_scr[...] = jnp.zeros(acc_scr.shape, F32)

    def fast_pv(c, j, mask):
        exp_to_p(c, m_scr[c], mask)
        acc_scr[c] = acc_scr[c] + p_times_v(c, j)

    def fast_body(j, carry):
        scores(1, j)
        fast_pv(0, j, None)
        scores(0, j + 1)
        fast_pv(1, j, None)
        return carry

    m_scr[0] = block_max(0, "maybe", 0)
    m_scr[1] = block_max(1, "maybe", 0)

    @pl.when(i > 0)
    def _():
        fast_pv(0, 0, None)
        scores(0, 1)
        fast_pv(1, 0, None)

    lax.fori_loop(1, i, fast_body, 0)
    scores(1, i)
    fast_pv(0, i, "diag")
    fast_pv(1, i, "diag")

    total = jnp.sum(jnp.abs(acc_scr[0])) + jnp.sum(jnp.abs(acc_scr[1]))
    overflowed = jnp.logical_not(total < jnp.inf)

    @pl.when(overflowed)
    def _():
        m_scr[...] = jnp.full(m_scr.shape, NEG_INF, F32)
        acc_scr[...] = jnp.zeros(acc_scr.shape, F32)

        def safe_pv(c, j, mask):
            m_old = m_scr[c]
            m_new = jnp.maximum(m_old, block_max(c, mask))
            alpha = jnp.exp2(m_old - m_new)
            exp_to_p(c, m_new, mask)
            m_scr[c] = m_new
            acc_scr[c] = acc_scr[c] * alpha + p_times_v(c, j)

        def safe_body(j, carry):
            scores(0, j)
            scores(1, j)
            safe_pv(0, j, None)
            safe_pv(1, j, None)
            return carry

        lax.fori_loop(0, i, safe_body, 0)
        scores(0, i)
        scores(1, i)
        safe_pv(0, i, "diag")
        safe_pv(1, i, "diag")

    nxt = jnp.minimum(i + 1, N_KV - 1)
    _attn_load_q(qT_ref, q_scr, nxt, 1 - slot)
    for c in range(2):
        s_scr[c] = jnp.dot(k_ref[0, 0, 0], q_scr[1 - slot, c], preferred_element_type=F32)

    l1 = acc_scr[0, DIFF_DV:DIFF_DV + 1, :]
    l2 = acc_scr[1, DIFF_DV:DIFF_DV + 1, :]

    lam_p = lam_ref[...]
    lam = (jnp.exp(jnp.sum(lam_p[0:1] * lam_p[1:2], axis=1, keepdims=True))
           - jnp.exp(jnp.sum(lam_p[2:3] * lam_p[3:4], axis=1, keepdims=True)) + LAMBDA_INIT)
    oT = acc_scr[0, :DIFF_DV, :] / l1 - lam * (acc_scr[1, :DIFF_DV, :] / l2)
    ms = jnp.mean(oT * oT, axis=0, keepdims=True)
    y = oT * lax.rsqrt(ms + NORM_EPS) * og_ref[...] * (1.0 - LAMBDA_INIT)
    o_ref[pl.ds(pl.multiple_of(i * TQ, TQ), TQ), :] = y.T.astype(BF16)


def _attn_call(qT, kk, vT, lam_p, og_col):
    return pl.pallas_call(
        _attn_kernel,
        grid=(BATCH, DIFF_HEADS),
        in_specs=[
            pl.BlockSpec((1, 1, N_KV, LANES, TQ), lambda b, h: (b, h, 0, 0, 0)),
            pl.BlockSpec((1, 1, N_KV, TQ, LANES), lambda b, h: (b, h, 0, 0, 0)),
            pl.BlockSpec((1, 1, N_KV, V_ROWS, TQ), lambda b, h: (b, h, 0, 0, 0)),
            pl.BlockSpec((4, DIFF_DH), lambda b, h: (0, 0)),
            pl.BlockSpec((DIFF_DV, 1), lambda b, h: (0, 0)),
        ],
        out_specs=pl.BlockSpec((SEQ, DIFF_DV), lambda b, h: (b, h)),
        out_shape=jax.ShapeDtypeStruct((TOKENS, DIFF_V_W), BF16),
        scratch_shapes=[pltpu.VMEM((2, 2, LANES, TQ), BF16),
                        pltpu.VMEM((2, TQ, TQ), F32),
                        pltpu.VMEM((2, TQ, TQ), BF16),
                        pltpu.VMEM((2, V_ROWS, TQ), F32),
                        pltpu.VMEM((2, 1, TQ), F32)],
        compiler_params=_cparams(("parallel", "parallel")),
        name="attn",
    )(qT, kk, vT, lam_p, og_col)


def _merge_kernel(x_ref, oa_ref, ob_ref, sa_ref, sb_ref, mod_ref, g2_ref, wa_ref, wb_ref, wo_ref,
                  wr2_ref, br_ref, x1_ref, h2_ref, rf_ref, rank_ref, cnt_ref,
                  carry_ref, tri_ref, lg_ref):
    tm = x_ref.shape[0]
    step = pl.program_id(0)

    @pl.when(step == 0)
    def _():
        carry_ref[...] = jnp.zeros_like(carry_ref)
        lg_ref[...] = jnp.zeros_like(lg_ref)
        ti = lax.broadcasted_iota(I32, (tm, tm), 0)
        tj = lax.broadcasted_iota(I32, (tm, tm), 1)
        tri_ref[...] = jnp.where(ti > tj, 1.0, 0.0).astype(BF16)

    ma = jnp.dot(oa_ref[...], wa_ref[...], preferred_element_type=F32)
    mb = jnp.dot(ob_ref[...], wb_ref[...], preferred_element_type=F32)
    merged = sa_ref[...].astype(F32) * ma + sb_ref[...].astype(F32) * mb
    y = jnp.dot(merged.astype(BF16), wo_ref[...], preferred_element_type=F32)
    logits = lg_ref[...]
    lane = lax.broadcasted_iota(I32, (tm, LANES), 1).astype(F32)
    ninf = -jnp.inf
    big = float(LANES)

    def first_argmax(v):
        vmax = jnp.max(v, axis=1, keepdims=True)
        idx = jnp.min(jnp.where(v == vmax, lane, big), axis=1, keepdims=True)
        return vmax, idx

    gl = jnp.where(lane < N_GROUPS, logits, ninf)
    gmax, gidx = first_argmax(gl)
    p_top = 1.0 / jnp.sum(jnp.exp(gl - gmax), axis=1, keepdims=True)
    lo = N_GROUPS + EXPERTS_PER_GROUP * gidx
    el = jnp.where((lane >= lo) & (lane < lo + EXPERTS_PER_GROUP), logits, ninf)
    e1max, e1 = first_argmax(el)
    e2max, e2 = first_argmax(jnp.where(lane == e1, ninf, el))
    t = jnp.exp(e2max - e1max)
    w1 = 1.0 / (1.0 + t)
    w2 = t / (1.0 + t)
    col = lax.broadcasted_iota(I32, (tm, ROUTE_W), 1)
    rf_ref[...] = jnp.where(col == 0, p_top * w1, jnp.where(col == 1, p_top * w2, 0.0))

    x1id = e1 - N_GROUPS
    x2id = e2 - N_GROUPS
    hit1 = lane == x1id
    hit2 = lane == x2id
    onehot = jnp.where(hit1 | hit2, 1.0, 0.0)
    before = jnp.dot(tri_ref[...], onehot.astype(BF16), preferred_element_type=F32) + carry_ref[...]
    r1 = jnp.sum(jnp.where(hit1, before, 0.0), axis=1, keepdims=True)
    r2 = jnp.sum(jnp.where(hit2, before, 0.0), axis=1, keepdims=True)
    real_tile = jnp.where(step > 0, 1.0, 0.0)
    carry_ref[...] = carry_ref[...] + real_tile * jnp.sum(onehot, axis=0, keepdims=True)
    cnt_ref[...] = carry_ref[...]
    cols = jnp.where(lane == 0.0, x1id, jnp.where(lane == 1.0, x2id,
                                                  jnp.where(lane == 2.0, r1,
                                                            jnp.where(lane == 3.0, r2, 0.0))))
    rank_ref[0] = cols.T[0:ROUTE_W, :].astype(I32)

    gate1 = mod_ref[0, 2:3, :]
    shift2 = mod_ref[0, 3:4, :]
    scale2 = mod_ref[0, 4:5, :]
    x1 = x_ref[...] + gate1 * y
    x1_ref[...] = x1
    ms = jnp.mean(x1 * x1, axis=-1, keepdims=True)
    h2 = (x1 * lax.rsqrt(ms + NORM_EPS) * g2_ref[...]) * (1.0 + scale2) + shift2
    _store_row_tiles(h2_ref, h2)
    h2_hi = h2.astype(BF16)
    h2_lo = (h2 - h2_hi.astype(F32)).astype(BF16)
    hi_both = jnp.dot(h2_hi, wr2_ref[...], preferred_element_type=F32)
    next_logits = (hi_both[:, :LANES] + hi_both[:, LANES:]
                   + jnp.dot(h2_lo, wr2_ref[:, :LANES], preferred_element_type=F32)) + br_ref[...]

    lg_ref[...] = next_logits


def _merge_call(x2, o_a, o_b, sa, sb, mod, norm2_g, wa, wb, wo, wr2, br):
    nb = SEQ // TM_MERGE
    n_tiles = TOKENS // TM_MERGE
    cur = lambda i: jnp.minimum(i, n_tiles - 1)
    prev = lambda i: jnp.maximum(i - 1, 0)
    tok = lambda w: pl.BlockSpec((TM_MERGE, w), lambda i: (cur(i), 0))
    const2 = lambda r, c: pl.BlockSpec((r, c), lambda i: (0, 0))
    return pl.pallas_call(
        _merge_kernel,
        grid=(n_tiles + 1,),
        in_specs=[tok(D_MODEL), tok(GLA_V_W), tok(DIFF_V_W), tok(D_MODEL), tok(D_MODEL),
                  pl.BlockSpec((1, N_MOD, D_MODEL), lambda i: (cur(i) // nb, 0, 0)),
                  const2(1, D_MODEL),
                  const2(GLA_V_W, D_MODEL), const2(DIFF_V_W, D_MODEL), const2(D_MODEL, D_MODEL),
                  const2(D_MODEL, 2 * LANES), const2(1, LANES)],
        out_specs=(tok(D_MODEL), _row_tile_spec(TM_MERGE, lambda i: (cur(i), 0)),
                   pl.BlockSpec((TM_MERGE, ROUTE_W), lambda i: (prev(i), 0)),
                   pl.BlockSpec((1, ROUTE_W, TM_MERGE), lambda i: (prev(i), 0, 0)),
                   pl.BlockSpec((1, LANES), lambda i: (0, 0))),
        out_shape=(jax.ShapeDtypeStruct((TOKENS, D_MODEL), F32),
                   jax.ShapeDtypeStruct((TOKENS * ROW_TILE_S, LANES), F32),
                   jax.ShapeDtypeStruct((TOKENS, ROUTE_W), F32),
                   jax.ShapeDtypeStruct((n_tiles, ROUTE_W, TM_MERGE), I32),
                   jax.ShapeDtypeStruct((1, LANES), F32)),
        scratch_shapes=[pltpu.VMEM((1, LANES), F32), pltpu.VMEM((TM_MERGE, TM_MERGE), BF16),
                        pltpu.VMEM((TM_MERGE, LANES), F32)],
        compiler_params=_cparams(("arbitrary",)),
        name="merge",
    )(x2, o_a, o_b, sa, sb, mod, norm2_g, wa, wb, wo, wr2, br)


def _plan_kernel(rk_ref, cnt_ref, dest_ref, blk_ref):
    n_tiles, _, tm = rk_ref.shape
    cnt = cnt_ref[...]
    padded = jnp.floor((cnt + (MOE_BLK - 1)) * (1.0 / MOE_BLK)) * MOE_BLK
    lane = lax.broadcasted_iota(I32, (1, LANES), 1)
    seg_end = padded
    step = 1
    while step < N_EXPERTS:
        seg_end = seg_end + jnp.where(lane >= step, pltpu.roll(seg_end, step, 1), 0.0)
        step *= 2
    seg_start = seg_end - padded
    valid_end = seg_start + cnt

    ei = lax.broadcasted_iota(I32, (LANES, LANES), 0)
    ej = lax.broadcasted_iota(I32, (LANES, LANES), 1)

    def to_col(rowv):
        return jnp.sum(jnp.where(ei == ej, rowv, 0.0), axis=1, keepdims=True)

    start_col, end_col, valid_col = to_col(seg_start), to_col(seg_end), to_col(valid_end)

    nb_pad = blk_ref.shape[1]
    e_sub = lax.broadcasted_iota(I32, (LANES, nb_pad), 0)
    b_start = (lax.broadcasted_iota(I32, (1, nb_pad), 1) * MOE_BLK).astype(F32)
    ends_before = jnp.where((e_sub < N_EXPERTS) & (end_col <= b_start), 1.0, 0.0)
    block_e = jnp.minimum(jnp.sum(ends_before, axis=0, keepdims=True), N_EXPERTS - 1.0)
    block_valid_end = jnp.sum(jnp.where(e_sub.astype(F32) == block_e, valid_col, 0.0),
                              axis=0, keepdims=True)
    n_valid = jnp.clip(block_valid_end - b_start, 0.0, float(MOE_BLK))
    n_used = jnp.max(seg_end, axis=1, keepdims=True) * (1.0 / MOE_BLK)
    nonempty = (to_col(cnt) > 0.0) & (e_sub < N_EXPERTS)
    e_subf = e_sub.astype(F32)
    no_next = float(LANES)
    nxt = jnp.min(jnp.where(nonempty & (e_subf > block_e), e_subf, no_next), axis=0, keepdims=True)
    nxt = jnp.where(nxt == no_next, -1.0, nxt)
    seg_idx = jnp.sum(jnp.where(nonempty & (e_subf < block_e), 1.0, 0.0), axis=0, keepdims=True)
    slot = seg_idx - 2.0 * jnp.floor(seg_idx * 0.5)
    blk_ref[...] = jnp.concatenate(
        [block_e, n_valid, jnp.broadcast_to(n_used, (1, nb_pad)), nxt, slot,
         jnp.zeros((ROUTE_W - 5, nb_pad), F32)], axis=0).astype(I32)

    e_tok = lax.broadcasted_iota(I32, (LANES, tm), 0)

    def tile(t, carry):
        rk = rk_ref[t]
        d = [jnp.sum(jnp.where(e_tok == rk[k:k + 1], start_col, 0.0), axis=0, keepdims=True)
             + rk[k + 2:k + 3].astype(F32) for k in range(TOP_K)]
        dest_ref[t] = jnp.concatenate(d + [jnp.zeros((ROUTE_W - TOP_K, tm), F32)], axis=0).astype(I32)
        return carry

    lax.fori_loop(0, n_tiles, tile, 0)


def _plan_call(rank_rows, cnt):
    n_tiles = TOKENS // TM_MERGE
    nb_pad = -(-N_BLOCKS // LANES) * LANES
    dest, blk = pl.pallas_call(
        _plan_kernel,
        out_shape=(jax.ShapeDtypeStruct((n_tiles, ROUTE_W, TM_MERGE), I32),
                   jax.ShapeDtypeStruct((ROUTE_W, nb_pad), I32)),
        compiler_params=pltpu.CompilerParams(vmem_limit_bytes=VMEM_LIMIT),
        name="plan",
    )(rank_rows, cnt)
    dest1 = dest[:, 0, :].reshape(TOKENS)
    dest2 = dest[:, 1, :].reshape(TOKENS)
    tables = tuple(blk[r, :N_BLOCKS] for r in (0, 1, 3, 4))
    return dest1, dest2, tables, blk[2, :1]


def _sc_params():
    return pltpu.CompilerParams(use_tc_tiling_on_sc=True)


def _sc_mesh():
    return plsc.VectorSubcoreMesh(core_axis_name="core", subcore_axis_name="subcore")


def _sc_worker_base(per_worker):
    wid = lax.axis_index("subcore") * SC_CORES + lax.axis_index("core")
    return wid * per_worker


def _sc_scatter_rows(src, dest1, dest2, n_out):
    n = src.shape[0]
    per_worker = n // SC_WORKERS
    assert per_worker * SC_WORKERS == n and per_worker % SC_CHUNK == 0

    def body(src_hbm, d1_hbm, d2_hbm, out_hbm, idx1_v, idx2_v, rows_v, lsem, ssem1, ssem2):
        base = _sc_worker_base(per_worker)

        @pl.loop(0, per_worker // SC_CHUNK)
        def _(j):
            start = pl.multiple_of(base + j * SC_CHUNK, SC_CHUNK)
            rows_in = pltpu.async_copy(src_hbm.at[pl.ds(start, SC_CHUNK)], rows_v, lsem)
            pltpu.sync_copy(d1_hbm.at[pl.ds(start, SC_CHUNK)], idx1_v)
            pltpu.sync_copy(d2_hbm.at[pl.ds(start, SC_CHUNK)], idx2_v)
            rows_in.wait()
            out1 = pltpu.async_copy(rows_v, out_hbm.at[idx1_v], ssem1)
            out2 = pltpu.async_copy(rows_v, out_hbm.at[idx2_v], ssem2)
            out1.wait()
            out2.wait()

    return pl.kernel(
        body,
        out_type=jax.ShapeDtypeStruct((n_out, ROW_TILE_S, LANES), F32),
        mesh=_sc_mesh(),
        scratch_types=[pltpu.VMEM((SC_CHUNK,), I32), pltpu.VMEM((SC_CHUNK,), I32),
                       pltpu.VMEM((SC_CHUNK, ROW_TILE_S, LANES), F32),
                       pltpu.SemaphoreType.DMA, pltpu.SemaphoreType.DMA, pltpu.SemaphoreType.DMA],
        compiler_params=_sc_params(),
        name="sc_dispatch",
    )(src, dest1, dest2)


def _sc_gather_rows(table, idx):
    n = idx.shape[0]
    per_worker = n // SC_WORKERS
    half = SC_CHUNK // 2
    assert per_worker * SC_WORKERS == n and per_worker % SC_CHUNK == 0

    def body(table_hbm, idx_hbm, out_hbm, idx_v, rows_a, rows_b, gsem_a, gsem_b, wsem_a, wsem_b):
        base = pl.multiple_of(_sc_worker_base(per_worker), SC_CHUNK)
        pltpu.sync_copy(idx_hbm.at[pl.ds(base, per_worker)], idx_v)

        @pl.loop(0, per_worker // SC_CHUNK)
        def _(j):
            off_a = pl.multiple_of(j * SC_CHUNK, SC_CHUNK)
            off_b = pl.multiple_of(j * SC_CHUNK + half, half)
            ga = pltpu.async_copy(table_hbm.at[idx_v.at[pl.ds(off_a, half)]], rows_a, gsem_a)
            gb = pltpu.async_copy(table_hbm.at[idx_v.at[pl.ds(off_b, half)]], rows_b, gsem_b)
            ga.wait()
            wa = pltpu.async_copy(rows_a, out_hbm.at[pl.ds(base + off_a, half)], wsem_a)
            gb.wait()
            wb = pltpu.async_copy(rows_b, out_hbm.at[pl.ds(base + off_b, half)], wsem_b)
            wa.wait()
            wb.wait()

    return pl.kernel(
        body,
        out_type=jax.ShapeDtypeStruct((n, ROW_TILE_S, LANES), F32),
        mesh=_sc_mesh(),
        scratch_types=[pltpu.VMEM((per_worker,), I32),
                       pltpu.VMEM((half, ROW_TILE_S, LANES), F32),
                       pltpu.VMEM((half, ROW_TILE_S, LANES), F32),
                       pltpu.SemaphoreType.DMA, pltpu.SemaphoreType.DMA,
                       pltpu.SemaphoreType.DMA, pltpu.SemaphoreType.DMA],
        compiler_params=_sc_params(),
        name="sc_gather",
    )(table, idx)


def _weight_fetch(w_hbm, stage, sem, e, slot):
    return [pltpu.make_async_copy(w.at[e], st.at[slot], sem.at[slot]) for w, st in zip(w_hbm, stage)]


def _expert_kernel(be_ref, nv_ref, nx_ref, sl_ref, nu_ref, xs_ref, w1_hbm, w3_hbm, w2_hbm, y_ref,
                   w1s, w3s, w2s, w1b, w3b, w2b, sem):
    i = pl.program_id(0)
    e = be_ref[i]
    used = i < nu_ref[0]
    first = (i == 0) | (e != be_ref[jnp.maximum(i - 1, 0)])
    w_hbm = (w1_hbm, w3_hbm, w2_hbm)
    stage = (w1s, w3s, w2s)

    @pl.when(used & first)
    def _():
        slot = sl_ref[i]

        @pl.when(i == 0)
        def _():
            for cp in _weight_fetch(w_hbm, stage, sem, e, slot):
                cp.start()

        for cp in _weight_fetch(w_hbm, stage, sem, e, slot):
            cp.wait()
        w1b[...] = w1s[slot].astype(BF16)
        w3b[...] = w3s[slot].astype(BF16)
        w2b[...] = w2s[slot].astype(BF16)

        @pl.when(nx_ref[i] >= 0)
        def _():
            for cp in _weight_fetch(w_hbm, stage, sem, nx_ref[i], 1 - slot):
                cp.start()

    n_parts = MOE_BLK // MOE_SUB

    def load_x(part):
        row0 = part * MOE_SUB
        row = lax.broadcasted_iota(I32, (MOE_SUB, D_MODEL), 0) + row0
        return jnp.where(row < nv_ref[i], _load_row_tiles(xs_ref, row0, MOE_SUB), 0.0).astype(BF16)

    def zero_y(part):
        y_ref[pl.ds(part * MOE_SUB * ROW_TILE_S, MOE_SUB * ROW_TILE_S), :] = jnp.zeros(
            (MOE_SUB * ROW_TILE_S, LANES), F32)

    def run(parts):
        hmid = []
        for part in parts:
            xb = load_x(part)
            a = jnp.dot(xb, w1b[...], preferred_element_type=F32)
            g = jnp.dot(xb, w3b[...], preferred_element_type=F32)
            hmid.append(((a * _sigmoid(a)) * g).astype(BF16))
        for part, hm in zip(parts, hmid):
            _store_row_tiles(y_ref, jnp.dot(hm, w2b[...], preferred_element_type=F32),
                             part * MOE_SUB)

    n_live = jnp.where(used, (nv_ref[i] + (MOE_SUB - 1)) // MOE_SUB, 0)
    for k in range(n_parts + 1):
        @pl.when(n_live == k)
        def _(k=k):
            if k:
                run(list(range(k)))
            for part in range(k, n_parts):
                zero_y(part)


def _expert_call(tables, n_used, xs, w1, w3, w2):
    n_tab = len(tables)
    grid_spec = pltpu.PrefetchScalarGridSpec(
        num_scalar_prefetch=n_tab + 1,
        grid=(N_BLOCKS,),
        in_specs=[_row_tile_spec(MOE_BLK, lambda i, *pf: (jnp.minimum(i, pf[n_tab][0] - 1), 0)),
                  pl.BlockSpec(memory_space=pl.ANY),
                  pl.BlockSpec(memory_space=pl.ANY),
                  pl.BlockSpec(memory_space=pl.ANY)],
        out_specs=_row_tile_spec(MOE_BLK, lambda i, *pf: (i, 0)),
        scratch_shapes=[pltpu.VMEM((2, D_MODEL, D_EXPERT), F32),
                        pltpu.VMEM((2, D_MODEL, D_EXPERT), F32),
                        pltpu.VMEM((2, D_EXPERT, D_MODEL), F32),
                        pltpu.VMEM((D_MODEL, D_EXPERT), BF16),
                        pltpu.VMEM((D_MODEL, D_EXPERT), BF16),
                        pltpu.VMEM((D_EXPERT, D_MODEL), BF16),
                        pltpu.SemaphoreType.DMA((2,))],
    )
    return pl.pallas_call(
        _expert_kernel,
        grid_spec=grid_spec,
        out_shape=jax.ShapeDtypeStruct((N_BLOCKS * MOE_BLK * ROW_TILE_S, LANES), F32),
        compiler_params=_cparams(("arbitrary",)),
        name="experts",
    )(*tables, n_used, xs, w1, w3, w2)


def _combine_rows_kernel(x1_ref, rf_ref, mod_ref, ya_ref, yb_ref, *rest):
    o_ref = rest[-1]
    rf = rf_ref[...]
    gate2 = mod_ref[0, 5:6, :]
    moe = rf[:, 0:1] * _load_row_tiles(ya_ref) + rf[:, 1:2] * _load_row_tiles(yb_ref)
    o_ref[...] = x1_ref[...] + gate2 * moe


def _combine_rows_call(x1, rf, mod, yg, part, prev=None):
    nb = SEQ // TM_ROW
    n_half = TOKENS // TM_ROW // 2
    t0 = part * n_half
    in_specs = [pl.BlockSpec((TM_ROW, D_MODEL), lambda i: (t0 + i, 0)),
                pl.BlockSpec((TM_ROW, ROUTE_W), lambda i: (t0 + i, 0)),
                pl.BlockSpec((1, N_MOD, D_MODEL), lambda i: ((t0 + i) // nb, 0, 0)),
                _row_tile_spec(TM_ROW, lambda i: (i, 0)),
                _row_tile_spec(TM_ROW, lambda i: (i + n_half, 0))]
    args = [x1, rf, mod, yg, yg]
    aliases = {}
    if prev is not None:
        in_specs.append(pl.BlockSpec(memory_space=pl.ANY))
        args.append(prev)
        aliases = {len(args) - 1: 0}
    return pl.pallas_call(
        _combine_rows_kernel,
        grid=(n_half,),
        in_specs=in_specs,
        out_specs=pl.BlockSpec((TM_ROW, D_MODEL), lambda i: (t0 + i, 0)),
        out_shape=jax.ShapeDtypeStruct((TOKENS, D_MODEL), F32),
        input_output_aliases=aliases,
        compiler_params=_cparams(("parallel",)),
        name="combine",
    )(*args)


def kernel(x, c, positions, ada_w, ada_b, norm1_g, norm2_g, w_in, gla_alpha_up, gla_alpha_b,
           gla_out_g, diff_q_g, diff_k_g, diff_lq1, diff_lk1, diff_lq2, diff_lk2, diff_out_g,
           w_branch_a, w_branch_b, w_out, router_group_w, router_group_b, router_expert_w,
           router_expert_b, expert_w1, expert_w3, expert_w2):
    assert x.shape == (BATCH, SEQ, D_MODEL) and ada_w.shape[0] == 1
    x2 = x.reshape(TOKENS, D_MODEL)

    w_in_p = _wprep_call(w_in[0])
    au_hi = gla_alpha_up[0].astype(BF16)
    au_lo = (gla_alpha_up[0] - au_hi.astype(F32)).astype(BF16)
    au_stack = jnp.concatenate(
        [au_hi, au_hi, au_lo, jnp.zeros((LR_PAD - 3 * GLA_GATE_RANK, GLA_QK_W), BF16)], axis=0)
    gid = jnp.arange(NORM_BLK) // DIFF_DH
    bd = jnp.where(gid[:, None] == gid[None, :], 1.0 / DIFF_DH, 0.0).astype(BF16)
    qg_row = jnp.tile(diff_q_g[0], DIFF_QK_W // DIFF_DH).reshape(1, DIFF_QK_W)
    kg_row = jnp.tile(diff_k_g[0], DIFF_QK_W // DIFF_DH).reshape(1, DIFF_QK_W)
    inv_freq = ROPE_THETA ** (-jnp.arange(ROPE_HALF, dtype=F32) / ROPE_HALF)
    invf64 = jnp.concatenate([inv_freq, inv_freq, jnp.zeros((DIFF_DH - ROPE_DIM,), F32)])
    invf_row = jnp.tile(invf64, LANES // DIFF_DH).reshape(1, LANES)
    pos_rows = positions.reshape(TOKENS // TM_IN, 1, TM_IN)
    lam_p = jnp.concatenate([diff_lq1, diff_lk1, diff_lq2, diff_lk2], axis=0)
    wr = (jnp.zeros((D_MODEL, LANES), F32)
          .at[:, :N_GROUPS].set(router_group_w[0])
          .at[:, N_GROUPS:N_GROUPS + N_EXPERTS].set(router_expert_w[0]))
    wr_hi = wr.astype(BF16)
    wr2 = jnp.concatenate([wr_hi, (wr - wr_hi.astype(F32)).astype(BF16)], axis=1)
    br = (jnp.zeros((1, LANES), F32)
          .at[0, :N_GROUPS].set(router_group_b[0])
          .at[0, N_GROUPS:N_GROUPS + N_EXPERTS].set(router_expert_b[0]))

    mod = _mod_call(c, ada_w[0], ada_b[0])
    gqk, gv, gr, lr, qT, kk, vT, sa, sb = _in_call(
        x2, mod, norm1_g, w_in_p, bd, qg_row, kg_row, pos_rows, invf_row)
    o_a = _gla_call(gqk, gv, gr, lr, au_stack, gla_alpha_b, gla_out_g)
    o_b = _attn_call(qT, kk, vT, lam_p, diff_out_g.reshape(DIFF_DV, 1))
    x1, h2, rf, rank, cnt = _merge_call(
        x2, o_a, o_b, sa, sb, mod, norm2_g, w_branch_a[0].astype(BF16),
        w_branch_b[0].astype(BF16), w_out[0].astype(BF16), wr2, br)
    dest1, dest2, tables, n_used = _plan_call(rank, cnt)

    xs = _sc_scatter_rows(_as_row_tiles(h2), dest1, dest2, N_BLOCKS * MOE_BLK)
    y = _expert_call(tables, n_used, _as_2d(xs), expert_w1[0], expert_w3[0], expert_w2[0])
    y3 = _as_row_tiles(y)
    half = TOKENS // 2
    out = None
    for part in range(2):
        tok = slice(part * half, (part + 1) * half)
        yg = _sc_gather_rows(y3, jnp.concatenate([dest1[tok], dest2[tok]]))
        out = _combine_rows_call(x1, rf, mod, _as_2d(yg), part, out)
    return out.reshape(BATCH, SEQ, D_MODEL)
```
